```python
import math
import jax, jax.numpy as jnp
from jax import lax
import numpy as np

D_MODEL = 1024
BATCH = 8
SEQ = 8192
DEPTH = 1

SSM_WIDTH = D_MODEL // 2
SSM_GROUP = 16
SSM_GROUPS = SSM_WIDTH // SSM_GROUP
SSM_STATE = 64
CONV_WIDTH = D_MODEL // 2
CONV_KERNEL = 31
FFN_HIDDEN = ((8 * D_MODEL // 3 + 255) // 256) * 256
FFN_KERNEL = 3
N_COND = 6
IN_PROJ_WIDTH = SSM_WIDTH + 2 * CONV_WIDTH + 2 * D_MODEL
DEEPNORM_ALPHA = (2.0 * DEPTH) ** 0.25
DEEPNORM_BETA = (8.0 * DEPTH) ** -0.25
LN_EPS = 1e-5
DT_MIN = 1e-3
DT_MAX = 1e-1

kernel_name = "hybrid_s5_conformer_convffn_deepnorm_adaln"


def _layer_norm_plain(x):
    xf = x.astype(jnp.float32)
    mu = jnp.mean(xf, axis=-1, keepdims=True)
    var = jnp.mean(jnp.square(xf - mu), axis=-1, keepdims=True)
    return ((xf - mu) * lax.rsqrt(var + LN_EPS)).astype(x.dtype)


def _layer_norm_affine(x, g, b):
    xf = x.astype(jnp.float32)
    mu = jnp.mean(xf, axis=-1, keepdims=True)
    var = jnp.mean(jnp.square(xf - mu), axis=-1, keepdims=True)
    y = (xf - mu) * lax.rsqrt(var + LN_EPS) * g.astype(jnp.float32) + b.astype(jnp.float32)
    return y.astype(x.dtype)


def _modulate(h, shift, scale):
    return h * (1 + scale[:, None, :]) + shift[:, None, :]


def _causal_depthwise_conv(x, w, b):
    k = w.shape[0]
    y = lax.conv_general_dilated(
        x, w.astype(x.dtype), window_strides=(1,), padding=[(k - 1, 0)],
        dimension_numbers=("NWC", "WIO", "NWC"), feature_group_count=x.shape[-1])
    return y + b


def _complex_affine_combine(e1, e2):
    a1r, a1i, b1r, b1i = e1
    a2r, a2i, b2r, b2i = e2
    ar = a2r * a1r - a2i * a1i
    ai = a2r * a1i + a2i * a1r
    br = a2r * b1r - a2i * b1i + b2r
    bi = a2r * b1i + a2i * b1r + b2i
    return ar, ai, br, bi


def _s5_branch(u, lam_re, lam_im, log_dt, b_re, b_im, c_re, c_im, d, w_a, w_b):
    bsz, seq = u.shape[0], u.shape[1]
    uf = u.astype(jnp.float32).reshape(bsz, seq, SSM_GROUPS, SSM_GROUP)
    lr = jnp.minimum(lam_re.astype(jnp.float32), -1e-4)
    li = lam_im.astype(jnp.float32)
    dt = jnp.exp(log_dt.astype(jnp.float32))[:, None]
    mag = jnp.exp(lr * dt)
    ang = li * dt
    lbr, lbi = mag * jnp.cos(ang), mag * jnp.sin(ang)
    num_r, num_i = lbr - 1.0, lbi
    den = lr * lr + li * li
    coef_r = (num_r * lr + num_i * li) / den
    coef_i = (num_i * lr - num_r * li) / den
    br, bi = b_re.astype(jnp.float32), b_im.astype(jnp.float32)
    bbar_r = coef_r[..., None] * br - coef_i[..., None] * bi
    bbar_i = coef_r[..., None] * bi + coef_i[..., None] * br
    bu_r = jnp.einsum("bsgp,gnp->bsgn", uf, bbar_r)
    bu_i = jnp.einsum("bsgp,gnp->bsgn", uf, bbar_i)
    a_r = jnp.broadcast_to(lbr, (1, seq, SSM_GROUPS, SSM_STATE))
    a_i = jnp.broadcast_to(lbi, (1, seq, SSM_GROUPS, SSM_STATE))
    _, _, h_r, h_i = lax.associative_scan(_complex_affine_combine, (a_r, a_i, bu_r, bu_i), axis=1)
    y = (jnp.einsum("bsgn,gpn->bsgp", h_r, c_re.astype(jnp.float32))
         - jnp.einsum("bsgn,gpn->bsgp", h_i, c_im.astype(jnp.float32))
         + d.astype(jnp.float32) * uf)
    y = jax.nn.gelu(y.reshape(bsz, seq, SSM_WIDTH), approximate=False).astype(u.dtype)
    return (y @ w_a) * jax.nn.sigmoid(y @ w_b)


def _conformer_branch(a, g, dw_w, dw_b, ln_g, ln_b, w_pw):
    v = a * jax.nn.sigmoid(g)
    v = _causal_depthwise_conv(v, dw_w, dw_b)
    v = _layer_norm_affine(v, ln_g, ln_b)
    return jax.nn.silu(v) @ w_pw


def _token_mixer(h, w_in, b_in, lam_re, lam_im, log_dt, b_re, b_im, c_re, c_im, d,
                 glu_w_a, glu_w_b, cv_dw_w, cv_dw_b, cv_ln_g, cv_ln_b, cv_w_pw, w_out):
    p = h @ w_in + b_in
    o1 = SSM_WIDTH
    o2 = o1 + CONV_WIDTH
    o3 = o2 + CONV_WIDTH
    o4 = o3 + D_MODEL
    u_ssm, cv_a, cv_g, gate_ssm, gate_cv = p[..., :o1], p[..., o1:o2], p[..., o2:o3], p[..., o3:o4], p[..., o4:]
    y_ssm = _s5_branch(u_ssm, lam_re, lam_im, log_dt, b_re, b_im, c_re, c_im, d, glu_w_a, glu_w_b)
    y_cv = _conformer_branch(cv_a, cv_g, cv_dw_w, cv_dw_b, cv_ln_g, cv_ln_b, cv_w_pw)
    merged = jax.nn.sigmoid(gate_ssm) * y_ssm + jax.nn.sigmoid(gate_cv) * y_cv
    return merged @ w_out


def _conv_ffn(h, w_up, dw_w, dw_b, w_down):
    up = _causal_depthwise_conv(h @ w_up, dw_w, dw_b)
    a, v = up[..., :FFN_HIDDEN], up[..., FFN_HIDDEN:]
    return (jax.nn.gelu(a, approximate=False) * v) @ w_down


def _fwd_setup_inputs(seed: int = 0) -> dict:
    key = jax.random.key(seed)
    ks = jax.random.split(key, 32)
    L = DEPTH
    f32 = jnp.float32

    def nrm(k, shape, scale):
        return jax.random.normal(k, shape, f32) * scale

    lam_im_init = jnp.pi * jnp.arange(SSM_STATE, dtype=f32)
    return {
        "x": nrm(ks[0], (BATCH, SEQ, D_MODEL), 1.0),
        "c": nrm(ks[1], (BATCH, D_MODEL), 1.0),
        "w_cond": nrm(ks[2], (L, D_MODEL, N_COND * D_MODEL), D_MODEL ** -0.5),
        "b_cond": nrm(ks[3], (L, N_COND * D_MODEL), 0.02),
        "w_in": nrm(ks[4], (L, D_MODEL, IN_PROJ_WIDTH), D_MODEL ** -0.5),
        "b_in": nrm(ks[5], (L, IN_PROJ_WIDTH), 0.02),
        "ssm_lambda_re": -0.5 + nrm(ks[6], (L, SSM_GROUPS, SSM_STATE), 0.01),
        "ssm_lambda_im": lam_im_init + nrm(ks[7], (L, SSM_GROUPS, SSM_STATE), 0.01),
        "ssm_log_dt": jax.random.uniform(ks[8], (L, SSM_GROUPS), f32, math.log(DT_MIN), math.log(DT_MAX)),
        "ssm_b_re": nrm(ks[9], (L, SSM_GROUPS, SSM_STATE, SSM_GROUP), (2 * SSM_GROUP) ** -0.5),
        "ssm_b_im": nrm(ks[10], (L, SSM_GROUPS, SSM_STATE, SSM_GROUP), (2 * SSM_GROUP) ** -0.5),
        "ssm_c_re": nrm(ks[11], (L, SSM_GROUPS, SSM_GROUP, SSM_STATE), (2 * SSM_STATE) ** -0.5),
        "ssm_c_im": nrm(ks[12], (L, SSM_GROUPS, SSM_GROUP, SSM_STATE), (2 * SSM_STATE) ** -0.5),
        "ssm_d": nrm(ks[13], (L, SSM_GROUPS, SSM_GROUP), 1.0),
        "ssm_glu_w_a": nrm(ks[14], (L, SSM_WIDTH, D_MODEL), SSM_WIDTH ** -0.5),
        "ssm_glu_w_b": nrm(ks[15], (L, SSM_WIDTH, D_MODEL), SSM_WIDTH ** -0.5),
        "cv_dw_w": nrm(ks[16], (L, CONV_KERNEL, 1, CONV_WIDTH), CONV_KERNEL ** -0.5),
        "cv_dw_b": nrm(ks[17], (L, CONV_WIDTH), 0.02),
        "cv_ln_g": 1.0 + nrm(ks[18], (L, CONV_WIDTH), 0.02),
        "cv_ln_b": nrm(ks[19], (L, CONV_WIDTH), 0.02),
        "cv_w_pw": nrm(ks[20], (L, CONV_WIDTH, D_MODEL), CONV_WIDTH ** -0.5),
        "w_out": nrm(ks[21], (L, D_MODEL, D_MODEL), D_MODEL ** -0.5 * DEEPNORM_BETA),
        "ln1_g": 1.0 + nrm(ks[22], (L, D_MODEL), 0.02),
        "ln1_b": nrm(ks[23], (L, D_MODEL), 0.02),
        "ffn_w_up": nrm(ks[24], (L, D_MODEL, 2 * FFN_HIDDEN), D_MODEL ** -0.5),
        "ffn_dw_w": nrm(ks[25], (L, FFN_KERNEL, 1, 2 * FFN_HIDDEN), FFN_KERNEL ** -0.5),
        "ffn_dw_b": nrm(ks[26], (L, 2 * FFN_HIDDEN), 0.02),
        "ffn_w_down": nrm(ks[27], (L, FFN_HIDDEN, D_MODEL), FFN_HIDDEN ** -0.5 * DEEPNORM_BETA),
        "ln2_g": 1.0 + nrm(ks[28], (L, D_MODEL), 0.02),
        "ln2_b": nrm(ks[29], (L, D_MODEL), 0.02),
    }


def _fwd_reference(x, c, w_cond, b_cond, w_in, b_in, ssm_lambda_re, ssm_lambda_im, ssm_log_dt,
              ssm_b_re, ssm_b_im, ssm_c_re, ssm_c_im, ssm_d, ssm_glu_w_a, ssm_glu_w_b,
              cv_dw_w, cv_dw_b, cv_ln_g, cv_ln_b, cv_w_pw, w_out, ln1_g, ln1_b,
              ffn_w_up, ffn_dw_w, ffn_dw_b, ffn_w_down, ln2_g, ln2_b):
    c_act = jax.nn.silu(c)
    for l in range(DEPTH):
        mod = c_act @ w_cond[l] + b_cond[l]
        sh1, sc1, g1, sh2, sc2, g2 = jnp.split(mod, N_COND, axis=-1)
        h = _modulate(_layer_norm_plain(x), sh1, sc1)
        y = _token_mixer(h, w_in[l], b_in[l], ssm_lambda_re[l], ssm_lambda_im[l], ssm_log_dt[l],
                         ssm_b_re[l], ssm_b_im[l], ssm_c_re[l], ssm_c_im[l], ssm_d[l],
                         ssm_glu_w_a[l], ssm_glu_w_b[l], cv_dw_w[l], cv_dw_b[l],
                         cv_ln_g[l], cv_ln_b[l], cv_w_pw[l], w_out[l])
        x = _layer_norm_affine(DEEPNORM_ALPHA * x + g1[:, None, :] * y, ln1_g[l], ln1_b[l])
        h = _modulate(_layer_norm_plain(x), sh2, sc2)
        y = _conv_ffn(h, ffn_w_up[l], ffn_dw_w[l], ffn_dw_b[l], ffn_w_down[l])
        x = _layer_norm_affine(DEEPNORM_ALPHA * x + g2[:, None, :] * y, ln2_g[l], ln2_b[l])
    return x


import jax as _jax
import jax.numpy as _jnp

TWIN_FORMAT = 'train_step'
FWD_PARAMS = ['x', 'c', 'w_cond', 'b_cond', 'w_in', 'b_in', 'ssm_lambda_re', 'ssm_lambda_im', 'ssm_log_dt', 'ssm_b_re', 'ssm_b_im', 'ssm_c_re', 'ssm_c_im', 'ssm_d', 'ssm_glu_w_a', 'ssm_glu_w_b', 'cv_dw_w', 'cv_dw_b', 'cv_ln_g', 'cv_ln_b', 'cv_w_pw', 'w_out', 'ln1_g', 'ln1_b', 'ffn_w_up', 'ffn_dw_w', 'ffn_dw_b', 'ffn_w_down', 'ln2_g', 'ln2_b']
TWIN_WEIGHTS = ['w_cond', 'b_cond', 'w_in', 'b_in', 'ssm_lambda_re', 'ssm_lambda_im', 'ssm_log_dt', 'ssm_b_re', 'ssm_b_im', 'ssm_c_re', 'ssm_c_im', 'ssm_d', 'ssm_glu_w_a', 'ssm_glu_w_b', 'cv_dw_w', 'cv_dw_b', 'cv_ln_g', 'cv_ln_b', 'cv_w_pw', 'w_out', 'ln1_g', 'ln1_b', 'ffn_w_up', 'ffn_dw_w', 'ffn_dw_b', 'ffn_w_down', 'ln2_g', 'ln2_b']
TWIN_DIFF_INPUT = 'x'
TWIN_INPUTS = ['x', 'c', 'w_cond', 'b_cond', 'w_in', 'b_in', 'ssm_lambda_re', 'ssm_lambda_im', 'ssm_log_dt', 'ssm_b_re', 'ssm_b_im', 'ssm_c_re', 'ssm_c_im', 'ssm_d', 'ssm_glu_w_a', 'ssm_glu_w_b', 'cv_dw_w', 'cv_dw_b', 'cv_ln_g', 'cv_ln_b', 'cv_w_pw', 'w_out', 'ln1_g', 'ln1_b', 'ffn_w_up', 'ffn_dw_w', 'ffn_dw_b', 'ffn_w_down', 'ln2_g', 'ln2_b', 'loss_target', 'm_w_cond', 'm_b_cond', 'm_w_in', 'm_b_in', 'm_ssm_lambda_re', 'm_ssm_lambda_im', 'm_ssm_log_dt', 'm_ssm_b_re', 'm_ssm_b_im', 'm_ssm_c_re', 'm_ssm_c_im', 'm_ssm_d', 'm_ssm_glu_w_a', 'm_ssm_glu_w_b', 'm_cv_dw_w', 'm_cv_dw_b', 'm_cv_ln_g', 'm_cv_ln_b', 'm_cv_w_pw', 'm_w_out', 'm_ln1_g', 'm_ln1_b', 'm_ffn_w_up', 'm_ffn_dw_w', 'm_ffn_dw_b', 'm_ffn_w_down', 'm_ln2_g', 'm_ln2_b', 'v_w_cond', 'v_b_cond', 'v_w_in', 'v_b_in', 'v_ssm_lambda_re', 'v_ssm_lambda_im', 'v_ssm_log_dt', 'v_ssm_b_re', 'v_ssm_b_im', 'v_ssm_c_re', 'v_ssm_c_im', 'v_ssm_d', 'v_ssm_glu_w_a', 'v_ssm_glu_w_b', 'v_cv_dw_w', 'v_cv_dw_b', 'v_cv_ln_g', 'v_cv_ln_b', 'v_cv_w_pw', 'v_w_out', 'v_ln1_g', 'v_ln1_b', 'v_ffn_w_up', 'v_ffn_dw_w', 'v_ffn_dw_b', 'v_ffn_w_down', 'v_ln2_g', 'v_ln2_b']
TWIN_OUTPUTS = ['loss', 'grad_x', 'grad_w_cond', 'grad_b_cond', 'grad_w_in', 'grad_b_in', 'grad_ssm_lambda_re', 'grad_ssm_lambda_im', 'grad_ssm_log_dt', 'grad_ssm_b_re', 'grad_ssm_b_im', 'grad_ssm_c_re', 'grad_ssm_c_im', 'grad_ssm_d', 'grad_ssm_glu_w_a', 'grad_ssm_glu_w_b', 'grad_cv_dw_w', 'grad_cv_dw_b', 'grad_cv_ln_g', 'grad_cv_ln_b', 'grad_cv_w_pw', 'grad_w_out', 'grad_ln1_g', 'grad_ln1_b', 'grad_ffn_w_up', 'grad_ffn_dw_w', 'grad_ffn_dw_b', 'grad_ffn_w_down', 'grad_ln2_g', 'grad_ln2_b', 'delta_w_cond', 'delta_b_cond', 'delta_w_in', 'delta_b_in', 'delta_ssm_lambda_re', 'delta_ssm_lambda_im', 'delta_ssm_log_dt', 'delta_ssm_b_re', 'delta_ssm_b_im', 'delta_ssm_c_re', 'delta_ssm_c_im', 'delta_ssm_d', 'delta_ssm_glu_w_a', 'delta_ssm_glu_w_b', 'delta_cv_dw_w', 'delta_cv_dw_b', 'delta_cv_ln_g', 'delta_cv_ln_b', 'delta_cv_w_pw', 'delta_w_out', 'delta_ln1_g', 'delta_ln1_b', 'delta_ffn_w_up', 'delta_ffn_dw_w', 'delta_ffn_dw_b', 'delta_ffn_w_down', 'delta_ln2_g', 'delta_ln2_b', 'new_m_w_cond', 'new_m_b_cond', 'new_m_w_in', 'new_m_b_in', 'new_m_ssm_lambda_re', 'new_m_ssm_lambda_im', 'new_m_ssm_log_dt', 'new_m_ssm_b_re', 'new_m_ssm_b_im', 'new_m_ssm_c_re', 'new_m_ssm_c_im', 'new_m_ssm_d', 'new_m_ssm_glu_w_a', 'new_m_ssm_glu_w_b', 'new_m_cv_dw_w', 'new_m_cv_dw_b', 'new_m_cv_ln_g', 'new_m_cv_ln_b', 'new_m_cv_w_pw', 'new_m_w_out', 'new_m_ln1_g', 'new_m_ln1_b', 'new_m_ffn_w_up', 'new_m_ffn_dw_w', 'new_m_ffn_dw_b', 'new_m_ffn_w_down', 'new_m_ln2_g', 'new_m_ln2_b', 'new_v_w_cond', 'new_v_b_cond', 'new_v_w_in', 'new_v_b_in', 'new_v_ssm_lambda_re', 'new_v_ssm_lambda_im', 'new_v_ssm_log_dt', 'new_v_ssm_b_re', 'new_v_ssm_b_im', 'new_v_ssm_c_re', 'new_v_ssm_c_im', 'new_v_ssm_d', 'new_v_ssm_glu_w_a', 'new_v_ssm_glu_w_b', 'new_v_cv_dw_w', 'new_v_cv_dw_b', 'new_v_cv_ln_g', 'new_v_cv_ln_b', 'new_v_cv_w_pw', 'new_v_w_out', 'new_v_ln1_g', 'new_v_ln1_b', 'new_v_ffn_w_up', 'new_v_ffn_dw_w', 'new_v_ffn_dw_b', 'new_v_ffn_w_down', 'new_v_ln2_g', 'new_v_ln2_b']
TWIN_LEAF_KINDS = {'loss': 'loss', 'grad_x': 'grad_x', 'grad_w_cond': 'grad_w', 'grad_b_cond': 'grad_w', 'grad_w_in': 'grad_w', 'grad_b_in': 'grad_w', 'grad_ssm_lambda_re': 'grad_w', 'grad_ssm_lambda_im': 'grad_w', 'grad_ssm_log_dt': 'grad_w', 'grad_ssm_b_re': 'grad_w', 'grad_ssm_b_im': 'grad_w', 'grad_ssm_c_re': 'grad_w', 'grad_ssm_c_im': 'grad_w', 'grad_ssm_d': 'grad_w', 'grad_ssm_glu_w_a': 'grad_w', 'grad_ssm_glu_w_b': 'grad_w', 'grad_cv_dw_w': 'grad_w', 'grad_cv_dw_b': 'grad_w', 'grad_cv_ln_g': 'grad_w', 'grad_cv_ln_b': 'grad_w', 'grad_cv_w_pw': 'grad_w', 'grad_w_out': 'grad_w', 'grad_ln1_g': 'grad_w', 'grad_ln1_b': 'grad_w', 'grad_ffn_w_up': 'grad_w', 'grad_ffn_dw_w': 'grad_w', 'grad_ffn_dw_b': 'grad_w', 'grad_ffn_w_down': 'grad_w', 'grad_ln2_g': 'grad_w', 'grad_ln2_b': 'grad_w', 'delta_w_cond': 'delta_w', 'delta_b_cond': 'delta_w', 'delta_w_in': 'delta_w', 'delta_b_in': 'delta_w', 'delta_ssm_lambda_re': 'delta_w', 'delta_ssm_lambda_im': 'delta_w', 'delta_ssm_log_dt': 'delta_w', 'delta_ssm_b_re': 'delta_w', 'delta_ssm_b_im': 'delta_w', 'delta_ssm_c_re': 'delta_w', 'delta_ssm_c_im': 'delta_w', 'delta_ssm_d': 'delta_w', 'delta_ssm_glu_w_a': 'delta_w', 'delta_ssm_glu_w_b': 'delta_w', 'delta_cv_dw_w': 'delta_w', 'delta_cv_dw_b': 'delta_w', 'delta_cv_ln_g': 'delta_w', 'delta_cv_ln_b': 'delta_w', 'delta_cv_w_pw': 'delta_w', 'delta_w_out': 'delta_w', 'delta_ln1_g': 'delta_w', 'delta_ln1_b': 'delta_w', 'delta_ffn_w_up': 'delta_w', 'delta_ffn_dw_w': 'delta_w', 'delta_ffn_dw_b': 'delta_w', 'delta_ffn_w_down': 'delta_w', 'delta_ln2_g': 'delta_w', 'delta_ln2_b': 'delta_w', 'new_m_w_cond': 'new_m', 'new_m_b_cond': 'new_m', 'new_m_w_in': 'new_m', 'new_m_b_in': 'new_m', 'new_m_ssm_lambda_re': 'new_m', 'new_m_ssm_lambda_im': 'new_m', 'new_m_ssm_log_dt': 'new_m', 'new_m_ssm_b_re': 'new_m', 'new_m_ssm_b_im': 'new_m', 'new_m_ssm_c_re': 'new_m', 'new_m_ssm_c_im': 'new_m', 'new_m_ssm_d': 'new_m', 'new_m_ssm_glu_w_a': 'new_m', 'new_m_ssm_glu_w_b': 'new_m', 'new_m_cv_dw_w': 'new_m', 'new_m_cv_dw_b': 'new_m', 'new_m_cv_ln_g': 'new_m', 'new_m_cv_ln_b': 'new_m', 'new_m_cv_w_pw': 'new_m', 'new_m_w_out': 'new_m', 'new_m_ln1_g': 'new_m', 'new_m_ln1_b': 'new_m', 'new_m_ffn_w_up': 'new_m', 'new_m_ffn_dw_w': 'new_m', 'new_m_ffn_dw_b': 'new_m', 'new_m_ffn_w_down': 'new_m', 'new_m_ln2_g': 'new_m', 'new_m_ln2_b': 'new_m', 'new_v_w_cond': 'new_v', 'new_v_b_cond': 'new_v', 'new_v_w_in': 'new_v', 'new_v_b_in': 'new_v', 'new_v_ssm_lambda_re': 'new_v', 'new_v_ssm_lambda_im': 'new_v', 'new_v_ssm_log_dt': 'new_v', 'new_v_ssm_b_re': 'new_v', 'new_v_ssm_b_im': 'new_v', 'new_v_ssm_c_re': 'new_v', 'new_v_ssm_c_im': 'new_v', 'new_v_ssm_d': 'new_v', 'new_v_ssm_glu_w_a': 'new_v', 'new_v_ssm_glu_w_b': 'new_v', 'new_v_cv_dw_w': 'new_v', 'new_v_cv_dw_b': 'new_v', 'new_v_cv_ln_g': 'new_v', 'new_v_cv_ln_b': 'new_v', 'new_v_cv_w_pw': 'new_v', 'new_v_w_out': 'new_v', 'new_v_ln1_g': 'new_v', 'new_v_ln1_b': 'new_v', 'new_v_ffn_w_up': 'new_v', 'new_v_ffn_dw_w': 'new_v', 'new_v_ffn_dw_b': 'new_v', 'new_v_ffn_w_down': 'new_v', 'new_v_ln2_g': 'new_v', 'new_v_ln2_b': 'new_v'}


def _forward(args):
    return _fwd_reference(*[args[k] for k in FWD_PARAMS])


def _output_shape():
    out = _jax.eval_shape(lambda: _forward(_fwd_setup_inputs(0)))
    return out.shape, out.dtype

N_MICROBATCH = 1
ADAM_LR = 0.001
ADAM_B1 = 0.9
ADAM_B2 = 0.999
ADAM_EPS = 1e-08
ADAM_WD = 0.01
ADAM_STEP = 10
PER_EXAMPLE_BATCH_AXIS = {'x': 0, 'c': 0, 'loss_target': 0}
SHARED_INPUTS = []
_WEIGHT_DTYPES = {'w_cond': _jnp.float32, 'b_cond': _jnp.float32, 'w_in': _jnp.float32, 'b_in': _jnp.float32, 'ssm_lambda_re': _jnp.float32, 'ssm_lambda_im': _jnp.float32, 'ssm_log_dt': _jnp.float32, 'ssm_b_re': _jnp.float32, 'ssm_b_im': _jnp.float32, 'ssm_c_re': _jnp.float32, 'ssm_c_im': _jnp.float32, 'ssm_d': _jnp.float32, 'ssm_glu_w_a': _jnp.float32, 'ssm_glu_w_b': _jnp.float32, 'cv_dw_w': _jnp.float32, 'cv_dw_b': _jnp.float32, 'cv_ln_g': _jnp.float32, 'cv_ln_b': _jnp.float32, 'cv_w_pw': _jnp.float32, 'w_out': _jnp.float32, 'ln1_g': _jnp.float32, 'ln1_b': _jnp.float32, 'ffn_w_up': _jnp.float32, 'ffn_dw_w': _jnp.float32, 'ffn_dw_b': _jnp.float32, 'ffn_w_down': _jnp.float32, 'ln2_g': _jnp.float32, 'ln2_b': _jnp.float32}
MOMENT_SCALE = {'w_cond': 6.142250e-02, 'b_cond': 1.078939e-01, 'w_in': 2.396167e-02, 'b_in': 2.289681e-02, 'ssm_lambda_re': 4.992491e-03, 'ssm_lambda_im': 5.162632e-03, 'ssm_log_dt': 1.154879e+00, 'ssm_b_re': 2.438584e-03, 'ssm_b_im': 3.102479e-03, 'ssm_c_re': 4.474586e-03, 'ssm_c_im': 5.394074e-03, 'ssm_d': 3.966650e-02, 'ssm_glu_w_a': 2.855711e-02, 'ssm_glu_w_b': 9.559532e-03, 'cv_dw_w': 4.325733e-02, 'cv_dw_b': 6.031578e-02, 'cv_ln_g': 5.013153e-02, 'cv_ln_b': 5.875451e-02, 'cv_w_pw': 3.047861e-02, 'w_out': 6.996365e-02, 'ln1_g': 1.829920e+00, 'ln1_b': 6.925640e-01, 'ffn_w_up': 5.421763e-02, 'ffn_dw_w': 5.384528e-02, 'ffn_dw_b': 4.886764e-02, 'ffn_w_down': 1.457506e-01, 'ln2_g': 6.414330e+01, 'ln2_b': 3.534783e+00}


def _to_microbatches(a, axis):
    t = _jnp.moveaxis(a, axis, 0)
    t = t.reshape((N_MICROBATCH, t.shape[0] // N_MICROBATCH) + t.shape[1:])
    return _jnp.moveaxis(t, 1, axis + 1)


def setup_inputs(seed: int = 0) -> dict:
    inp = _fwd_setup_inputs(seed)
    key = _jax.random.fold_in(_jax.random.key(seed), 7919)
    shape, _ = _output_shape()
    out = dict(inp)
    out["loss_target"] = _jax.random.normal(_jax.random.fold_in(key, 0), shape, _jnp.float32)
    for i, name in enumerate(TWIN_WEIGHTS):
        w = inp[name].astype(_jnp.float32)
        if MOMENT_SCALE is None:
            s = _jnp.sqrt(_jnp.mean(_jnp.square(w)) + 1e-30)
        else:
            s = MOMENT_SCALE[name]
        km, kv = _jax.random.split(_jax.random.fold_in(key, i + 1))
        out[name] = w
        out["m_" + name] = s * _jax.random.normal(km, w.shape, _jnp.float32)
        out["v_" + name] = (s * s) * _jax.random.uniform(kv, w.shape, _jnp.float32, 0.5, 1.5)
    if N_MICROBATCH > 1:
        for name, axis in PER_EXAMPLE_BATCH_AXIS.items():
            out[name] = _to_microbatches(out[name], axis)
    return {'x': out['x'], 'c': out['c'], 'w_cond': out['w_cond'], 'b_cond': out['b_cond'], 'w_in': out['w_in'], 'b_in': out['b_in'], 'ssm_lambda_re': out['ssm_lambda_re'], 'ssm_lambda_im': out['ssm_lambda_im'], 'ssm_log_dt': out['ssm_log_dt'], 'ssm_b_re': out['ssm_b_re'], 'ssm_b_im': out['ssm_b_im'], 'ssm_c_re': out['ssm_c_re'], 'ssm_c_im': out['ssm_c_im'], 'ssm_d': out['ssm_d'], 'ssm_glu_w_a': out['ssm_glu_w_a'], 'ssm_glu_w_b': out['ssm_glu_w_b'], 'cv_dw_w': out['cv_dw_w'], 'cv_dw_b': out['cv_dw_b'], 'cv_ln_g': out['cv_ln_g'], 'cv_ln_b': out['cv_ln_b'], 'cv_w_pw': out['cv_w_pw'], 'w_out': out['w_out'], 'ln1_g': out['ln1_g'], 'ln1_b': out['ln1_b'], 'ffn_w_up': out['ffn_w_up'], 'ffn_dw_w': out['ffn_dw_w'], 'ffn_dw_b': out['ffn_dw_b'], 'ffn_w_down': out['ffn_w_down'], 'ln2_g': out['ln2_g'], 'ln2_b': out['ln2_b'], 'loss_target': out['loss_target'], 'm_w_cond': out['m_w_cond'], 'm_b_cond': out['m_b_cond'], 'm_w_in': out['m_w_in'], 'm_b_in': out['m_b_in'], 'm_ssm_lambda_re': out['m_ssm_lambda_re'], 'm_ssm_lambda_im': out['m_ssm_lambda_im'], 'm_ssm_log_dt': out['m_ssm_log_dt'], 'm_ssm_b_re': out['m_ssm_b_re'], 'm_ssm_b_im': out['m_ssm_b_im'], 'm_ssm_c_re': out['m_ssm_c_re'], 'm_ssm_c_im': out['m_ssm_c_im'], 'm_ssm_d': out['m_ssm_d'], 'm_ssm_glu_w_a': out['m_ssm_glu_w_a'], 'm_ssm_glu_w_b': out['m_ssm_glu_w_b'], 'm_cv_dw_w': out['m_cv_dw_w'], 'm_cv_dw_b': out['m_cv_dw_b'], 'm_cv_ln_g': out['m_cv_ln_g'], 'm_cv_ln_b': out['m_cv_ln_b'], 'm_cv_w_pw': out['m_cv_w_pw'], 'm_w_out': out['m_w_out'], 'm_ln1_g': out['m_ln1_g'], 'm_ln1_b': out['m_ln1_b'], 'm_ffn_w_up': out['m_ffn_w_up'], 'm_ffn_dw_w': out['m_ffn_dw_w'], 'm_ffn_dw_b': out['m_ffn_dw_b'], 'm_ffn_w_down': out['m_ffn_w_down'], 'm_ln2_g': out['m_ln2_g'], 'm_ln2_b': out['m_ln2_b'], 'v_w_cond': out['v_w_cond'], 'v_b_cond': out['v_b_cond'], 'v_w_in': out['v_w_in'], 'v_b_in': out['v_b_in'], 'v_ssm_lambda_re': out['v_ssm_lambda_re'], 'v_ssm_lambda_im': out['v_ssm_lambda_im'], 'v_ssm_log_dt': out['v_ssm_log_dt'], 'v_ssm_b_re': out['v_ssm_b_re'], 'v_ssm_b_im': out['v_ssm_b_im'], 'v_ssm_c_re': out['v_ssm_c_re'], 'v_ssm_c_im': out['v_ssm_c_im'], 'v_ssm_d': out['v_ssm_d'], 'v_ssm_glu_w_a': out['v_ssm_glu_w_a'], 'v_ssm_glu_w_b': out['v_ssm_glu_w_b'], 'v_cv_dw_w': out['v_cv_dw_w'], 'v_cv_dw_b': out['v_cv_dw_b'], 'v_cv_ln_g': out['v_cv_ln_g'], 'v_cv_ln_b': out['v_cv_ln_b'], 'v_cv_w_pw': out['v_cv_w_pw'], 'v_w_out': out['v_w_out'], 'v_ln1_g': out['v_ln1_g'], 'v_ln1_b': out['v_ln1_b'], 'v_ffn_w_up': out['v_ffn_w_up'], 'v_ffn_dw_w': out['v_ffn_dw_w'], 'v_ffn_dw_b': out['v_ffn_dw_b'], 'v_ffn_w_down': out['v_ffn_w_down'], 'v_ln2_g': out['v_ln2_g'], 'v_ln2_b': out['v_ln2_b']}


def _loss(weights, diff, rest, loss_target):
    with _jax.named_scope("forward"):
        args = {**rest, TWIN_DIFF_INPUT: diff, **{k: w.astype(_WEIGHT_DTYPES[k]) for k, w in weights.items()}}
        y = _forward(args)
    with _jax.named_scope("loss_head"):
        err = _jnp.square(y.astype(_jnp.float32) - loss_target)
        return 0.5 * _jnp.sum(_jnp.mean(err, axis=-1)) if err.ndim else 0.5 * err


def _adamw(w, g, m, v):
    m = ADAM_B1 * m + (1.0 - ADAM_B1) * g
    v = ADAM_B2 * v + (1.0 - ADAM_B2) * _jnp.square(g)
    m_hat = m / (1.0 - ADAM_B1 ** ADAM_STEP)
    v_hat = v / (1.0 - ADAM_B2 ** ADAM_STEP)
    delta = -ADAM_LR * (m_hat / (_jnp.sqrt(v_hat) + ADAM_EPS) + ADAM_WD * w)
    return delta, m, v


def reference(x, c, w_cond, b_cond, w_in, b_in, ssm_lambda_re, ssm_lambda_im, ssm_log_dt, ssm_b_re, ssm_b_im, ssm_c_re, ssm_c_im, ssm_d, ssm_glu_w_a, ssm_glu_w_b, cv_dw_w, cv_dw_b, cv_ln_g, cv_ln_b, cv_w_pw, w_out, ln1_g, ln1_b, ffn_w_up, ffn_dw_w, ffn_dw_b, ffn_w_down, ln2_g, ln2_b, loss_target, m_w_cond, m_b_cond, m_w_in, m_b_in, m_ssm_lambda_re, m_ssm_lambda_im, m_ssm_log_dt, m_ssm_b_re, m_ssm_b_im, m_ssm_c_re, m_ssm_c_im, m_ssm_d, m_ssm_glu_w_a, m_ssm_glu_w_b, m_cv_dw_w, m_cv_dw_b, m_cv_ln_g, m_cv_ln_b, m_cv_w_pw, m_w_out, m_ln1_g, m_ln1_b, m_ffn_w_up, m_ffn_dw_w, m_ffn_dw_b, m_ffn_w_down, m_ln2_g, m_ln2_b, v_w_cond, v_b_cond, v_w_in, v_b_in, v_ssm_lambda_re, v_ssm_lambda_im, v_ssm_log_dt, v_ssm_b_re, v_ssm_b_im, v_ssm_c_re, v_ssm_c_im, v_ssm_d, v_ssm_glu_w_a, v_ssm_glu_w_b, v_cv_dw_w, v_cv_dw_b, v_cv_ln_g, v_cv_ln_b, v_cv_w_pw, v_w_out, v_ln1_g, v_ln1_b, v_ffn_w_up, v_ffn_dw_w, v_ffn_dw_b, v_ffn_w_down, v_ln2_g, v_ln2_b):
    given = dict(x=x, c=c, w_cond=w_cond, b_cond=b_cond, w_in=w_in, b_in=b_in, ssm_lambda_re=ssm_lambda_re, ssm_lambda_im=ssm_lambda_im, ssm_log_dt=ssm_log_dt, ssm_b_re=ssm_b_re, ssm_b_im=ssm_b_im, ssm_c_re=ssm_c_re, ssm_c_im=ssm_c_im, ssm_d=ssm_d, ssm_glu_w_a=ssm_glu_w_a, ssm_glu_w_b=ssm_glu_w_b, cv_dw_w=cv_dw_w, cv_dw_b=cv_dw_b, cv_ln_g=cv_ln_g, cv_ln_b=cv_ln_b, cv_w_pw=cv_w_pw, w_out=w_out, ln1_g=ln1_g, ln1_b=ln1_b, ffn_w_up=ffn_w_up, ffn_dw_w=ffn_dw_w, ffn_dw_b=ffn_dw_b, ffn_w_down=ffn_w_down, ln2_g=ln2_g, ln2_b=ln2_b, loss_target=loss_target, m_w_cond=m_w_cond, m_b_cond=m_b_cond, m_w_in=m_w_in, m_b_in=m_b_in, m_ssm_lambda_re=m_ssm_lambda_re, m_ssm_lambda_im=m_ssm_lambda_im, m_ssm_log_dt=m_ssm_log_dt, m_ssm_b_re=m_ssm_b_re, m_ssm_b_im=m_ssm_b_im, m_ssm_c_re=m_ssm_c_re, m_ssm_c_im=m_ssm_c_im, m_ssm_d=m_ssm_d, m_ssm_glu_w_a=m_ssm_glu_w_a, m_ssm_glu_w_b=m_ssm_glu_w_b, m_cv_dw_w=m_cv_dw_w, m_cv_dw_b=m_cv_dw_b, m_cv_ln_g=m_cv_ln_g, m_cv_ln_b=m_cv_ln_b, m_cv_w_pw=m_cv_w_pw, m_w_out=m_w_out, m_ln1_g=m_ln1_g, m_ln1_b=m_ln1_b, m_ffn_w_up=m_ffn_w_up, m_ffn_dw_w=m_ffn_dw_w, m_ffn_dw_b=m_ffn_dw_b, m_ffn_w_down=m_ffn_w_down, m_ln2_g=m_ln2_g, m_ln2_b=m_ln2_b, v_w_cond=v_w_cond, v_b_cond=v_b_cond, v_w_in=v_w_in, v_b_in=v_b_in, v_ssm_lambda_re=v_ssm_lambda_re, v_ssm_lambda_im=v_ssm_lambda_im, v_ssm_log_dt=v_ssm_log_dt, v_ssm_b_re=v_ssm_b_re, v_ssm_b_im=v_ssm_b_im, v_ssm_c_re=v_ssm_c_re, v_ssm_c_im=v_ssm_c_im, v_ssm_d=v_ssm_d, v_ssm_glu_w_a=v_ssm_glu_w_a, v_ssm_glu_w_b=v_ssm_glu_w_b, v_cv_dw_w=v_cv_dw_w, v_cv_dw_b=v_cv_dw_b, v_cv_ln_g=v_cv_ln_g, v_cv_ln_b=v_cv_ln_b, v_cv_w_pw=v_cv_w_pw, v_w_out=v_w_out, v_ln1_g=v_ln1_g, v_ln1_b=v_ln1_b, v_ffn_w_up=v_ffn_w_up, v_ffn_dw_w=v_ffn_dw_w, v_ffn_dw_b=v_ffn_dw_b, v_ffn_w_down=v_ffn_w_down, v_ln2_g=v_ln2_g, v_ln2_b=v_ln2_b)
    weights = {n: given[n] for n in TWIN_WEIGHTS}
    shared = {n: given[n] for n in SHARED_INPUTS}
    per_example = {n: given[n] for n in ['x', 'c']}
    grad_fn = _jax.value_and_grad(_loss, argnums=(0, 1))

    def one_microbatch(ex, loss_target):
        ex = dict(ex)
        diff = ex.pop(TWIN_DIFF_INPUT)
        return grad_fn(weights, diff, {**shared, **ex}, loss_target)

    if N_MICROBATCH == 1:
        loss, (grad_w, grad_x) = one_microbatch(per_example, given["loss_target"])
    else:
        def body(carry, xs):
            loss_sum, grad_sum = carry
            l_k, (gw_k, gx_k) = one_microbatch(xs[0], xs[1])
            with _jax.named_scope("update"):
                return (loss_sum + l_k, _jax.tree.map(_jnp.add, grad_sum, gw_k)), gx_k

        init = (_jnp.zeros((), _jnp.float32), _jax.tree.map(_jnp.zeros_like, weights))
        (loss, grad_w), grad_x = _jax.lax.scan(body, init, (per_example, given["loss_target"]))
    with _jax.named_scope("update"):
        delta_w, new_m, new_v = {}, {}, {}
        for n in TWIN_WEIGHTS:
            delta_w[n], new_m[n], new_v[n] = _adamw(weights[n], grad_w[n], given["m_" + n], given["v_" + n])
    return (loss, grad_x, *[grad_w[n] for n in TWIN_WEIGHTS], *[delta_w[n] for n in TWIN_WEIGHTS],
            *[new_m[n] for n in TWIN_WEIGHTS], *[new_v[n] for n in TWIN_WEIGHTS])
```

```python
import functools
import math

import jax
import jax.numpy as jnp
from jax import lax
from jax.experimental import pallas as pl
from jax.experimental.pallas import tpu as pltpu

F32 = jnp.float32
BF16 = jnp.bfloat16

D_MODEL = 1024
SSM_WIDTH = 512
SSM_GROUP = 16
SSM_GROUPS = 32
SSM_STATE = 64
CONV_WIDTH = 512
CONV_KERNEL = 31
FFN_HIDDEN = 2816
FFN_KERNEL = 3
IN_PROJ_WIDTH = 3584
N_COND = 6
ALPHA = 2.0 ** 0.25
LN_EPS = 1e-5
ADAM_LR, ADAM_B1, ADAM_B2, ADAM_EPS, ADAM_WD, ADAM_STEP = 0.001, 0.9, 0.999, 1e-08, 0.01, 10

N_DEV = 8
N_CHIP = 4
LANES = 128
SSM_CHUNK = 16
LANE_GROUPS = LANES // SSM_GROUP
N_LANE_BLOCKS = SSM_WIDTH // LANES
STATE_COLS = LANE_GROUPS * SSM_STATE
CHUNK_COLS = SSM_CHUNK * LANES
CONV_HALO = 32
VMEM_LIMIT = 56 * 1024 * 1024
MESH = pl.DeviceIdType.MESH

BIG = (
    ("w_in", "col", (D_MODEL, IN_PROJ_WIDTH)),
    ("ssm_glu_w_a", "col", (SSM_WIDTH, D_MODEL)),
    ("ssm_glu_w_b", "col", (SSM_WIDTH, D_MODEL)),
    ("cv_w_pw", "col", (CONV_WIDTH, D_MODEL)),
    ("w_out", "row", (D_MODEL, D_MODEL)),
    ("ffn_w_up", "col", (D_MODEL, 2 * FFN_HIDDEN)),
    ("ffn_w_down", "row", (FFN_HIDDEN, D_MODEL)),
)

WEIGHTS = ['w_cond', 'b_cond', 'w_in', 'b_in', 'ssm_lambda_re', 'ssm_lambda_im', 'ssm_log_dt', 'ssm_b_re', 'ssm_b_im',
           'ssm_c_re', 'ssm_c_im', 'ssm_d', 'ssm_glu_w_a', 'ssm_glu_w_b', 'cv_dw_w', 'cv_dw_b', 'cv_ln_g', 'cv_ln_b',
           'cv_w_pw', 'w_out', 'ln1_g', 'ln1_b', 'ffn_w_up', 'ffn_dw_w', 'ffn_dw_b', 'ffn_w_down', 'ln2_g', 'ln2_b']
INPUTS = ['x', 'c'] + WEIGHTS + ['loss_target'] + ['m_' + n for n in WEIGHTS] + ['v_' + n for n in WEIGHTS]

PACK = (
    ("dmod", N_COND * D_MODEL), ("c_act", D_MODEL), ("b_in", IN_PROJ_WIDTH),
    ("ssm_lambda_re", SSM_GROUPS * SSM_STATE), ("ssm_lambda_im", SSM_GROUPS * SSM_STATE), ("ssm_log_dt", SSM_GROUPS),
    ("ssm_b_re", SSM_GROUPS * SSM_STATE * SSM_GROUP), ("ssm_b_im", SSM_GROUPS * SSM_STATE * SSM_GROUP),
    ("ssm_c_re", SSM_GROUPS * SSM_STATE * SSM_GROUP), ("ssm_c_im", SSM_GROUPS * SSM_STATE * SSM_GROUP),
    ("ssm_d", SSM_GROUPS * SSM_GROUP), ("cv_dw_w", CONV_KERNEL * CONV_WIDTH), ("cv_dw_b", CONV_WIDTH),
    ("cv_ln_g", CONV_WIDTH), ("cv_ln_b", CONV_WIDTH), ("ln1_g", D_MODEL), ("ln1_b", D_MODEL),
    ("ffn_dw_w", FFN_KERNEL * 2 * FFN_HIDDEN), ("ffn_dw_b", 2 * FFN_HIDDEN), ("ln2_g", D_MODEL), ("ln2_b", D_MODEL),
    ("loss", 1),
)
PACK_COLS = 1024
PACK_ROWS = 192
assert sum(n for _, n in PACK) <= PACK_ROWS * PACK_COLS

SMALL_UPD = (
    ("b_cond", N_COND * D_MODEL), ("b_in", IN_PROJ_WIDTH),
    ("ssm_lambda_re", SSM_GROUPS * SSM_STATE), ("ssm_lambda_im", SSM_GROUPS * SSM_STATE), ("ssm_log_dt", SSM_GROUPS),
    ("ssm_b_re", SSM_GROUPS * SSM_STATE * SSM_GROUP), ("ssm_b_im", SSM_GROUPS * SSM_STATE * SSM_GROUP),
    ("ssm_c_re", SSM_GROUPS * SSM_STATE * SSM_GROUP), ("ssm_c_im", SSM_GROUPS * SSM_STATE * SSM_GROUP),
    ("ssm_d", SSM_GROUPS * SSM_GROUP), ("cv_dw_w", CONV_KERNEL * CONV_WIDTH // N_CHIP), ("cv_dw_b", CONV_WIDTH),
    ("cv_ln_g", CONV_WIDTH), ("cv_ln_b", CONV_WIDTH), ("ln1_g", D_MODEL), ("ln1_b", D_MODEL),
    ("ffn_dw_w", FFN_KERNEL * 2 * FFN_HIDDEN // N_CHIP), ("ffn_dw_b", 2 * FFN_HIDDEN), ("ln2_g", D_MODEL),
    ("ln2_b", D_MODEL),
)
assert sum(n for _, n in SMALL_UPD) <= PACK_ROWS * PACK_COLS


def _params(sem=None, **kw):
    return pltpu.CompilerParams(dimension_semantics=sem, vmem_limit_bytes=VMEM_LIMIT, **kw)


def _ln_stats(x):
    mu = jnp.mean(x, axis=-1, keepdims=True)
    xc = x - mu
    var = jnp.mean(xc * xc, axis=-1, keepdims=True)
    rstd = lax.rsqrt(var + LN_EPS)
    return xc * rstd, rstd


def _ln_bwd(dxhat, xhat, rstd):
    m1 = jnp.mean(dxhat, axis=-1, keepdims=True)
    m2 = jnp.mean(dxhat * xhat, axis=-1, keepdims=True)
    return rstd * (dxhat - m1 - xhat * m2)


def _sig(x):
    return 1.0 / (1.0 + jnp.exp(-x))


def _gelu(x):
    return 0.5 * x * (1.0 + lax.erf(x * (1.0 / math.sqrt(2.0))))


def _dgelu(x):
    return 0.5 * (1.0 + lax.erf(x * (1.0 / math.sqrt(2.0)))) + x * jnp.exp(-0.5 * x * x) * (1.0 / math.sqrt(2.0 * math.pi))


def _colsum(a):
    return jnp.sum(a, axis=0, keepdims=True)


def _dot(a, b):
    return jnp.dot(a, b, preferred_element_type=F32)


def _dot_nt(a, b):
    return lax.dot_general(a, b, (((1,), (1,)), ((), ())), preferred_element_type=F32)


def _dot_tn(a, b):
    return lax.dot_general(a, b, (((0,), (0,)), ((), ())), preferred_element_type=F32)


def _load_once(src, dst, sem):
    cp = pltpu.make_async_copy(src, dst, sem)
    cp.start()
    cp.wait()


def _full(a):
    nd = a.ndim
    return pl.BlockSpec(a.shape, lambda *_: (0,) * nd)


ANY = pl.BlockSpec(memory_space=pl.ANY)


def _f1_inproj(x, modv, b_in, w_in, tb):
    t = x.shape[0]
    chunks = [(j * 512, 512) for j in range(IN_PROJ_WIDTH // 512)]

    def body(x_ref, modv_ref, b_ref, w_hbm, u4_ref, prest_ref, h_ref, w_v, sem):
        @pl.when(pl.program_id(0) == 0)
        def _():
            _load_once(w_hbm, w_v, sem)

        xn, _ = _ln_stats(x_ref[...])
        h = (xn * (1.0 + modv_ref[1:2, :]) + modv_ref[0:1, :]).astype(BF16)
        h_ref[...] = h
        for c0, cw in chunks:
            p = _dot(h, w_v[:, c0:c0 + cw]) + b_ref[:, c0:c0 + cw]
            if c0 == 0:
                for b in range(N_LANE_BLOCKS):
                    u4_ref[b] = p[:, b * LANES:(b + 1) * LANES].astype(BF16)
            else:
                prest_ref[:, c0 - SSM_WIDTH:c0 - SSM_WIDTH + cw] = p

    return pl.pallas_call(
        body, name="f1_inproj", grid=(t // tb,),
        in_specs=[pl.BlockSpec((tb, D_MODEL), lambda i: (i, 0)), _full(modv), _full(b_in), ANY],
        out_specs=[pl.BlockSpec((N_LANE_BLOCKS, tb, LANES), lambda i: (0, i, 0)),
                   pl.BlockSpec((tb, IN_PROJ_WIDTH - SSM_WIDTH), lambda i: (i, 0)),
                   pl.BlockSpec((tb, D_MODEL), lambda i: (i, 0))],
        out_shape=[jax.ShapeDtypeStruct((N_LANE_BLOCKS, t, LANES), BF16),
                   jax.ShapeDtypeStruct((t, IN_PROJ_WIDTH - SSM_WIDTH), F32),
                   jax.ShapeDtypeStruct((t, D_MODEL), BF16)],
        scratch_shapes=[pltpu.VMEM(w_in.shape, BF16), pltpu.SemaphoreType.DMA],
        compiler_params=_params(("arbitrary",)),
    )(x, modv, b_in, w_in)


def _s5_build(lam_re, lam_im, log_dt, b_re, b_im, c_re, c_im, d):
    hi = lax.Precision.HIGHEST
    el, g, n, p, lg, nb = SSM_CHUNK, SSM_GROUPS, SSM_STATE, SSM_GROUP, LANE_GROUPS, N_LANE_BLOCKS
    lr = jnp.minimum(lam_re, -1e-4)
    li = lam_im
    dt = jnp.exp(log_dt)[:, None]
    mag = jnp.exp(lr * dt)
    ang = li * dt
    lbr, lbi = mag * jnp.cos(ang), mag * jnp.sin(ang)
    num_r, num_i = lbr - 1.0, lbi
    den = lr * lr + li * li
    coef_r = (num_r * lr + num_i * li) / den
    coef_i = (num_i * lr - num_r * li) / den
    bbar_r = coef_r[..., None] * b_re - coef_i[..., None] * b_im
    bbar_i = coef_r[..., None] * b_im + coef_i[..., None] * b_re
    k = jnp.arange(el + 1, dtype=F32)[:, None, None]
    pmag = jnp.exp(k * (lr * dt)[None])
    pr, pi = pmag * jnp.cos(k * ang[None]), pmag * jnp.sin(k * ang[None])
    car = c_re[None] * pr[:, :, None, :] - c_im[None] * pi[:, :, None, :]
    cai = c_re[None] * pi[:, :, None, :] + c_im[None] * pr[:, :, None, :]
    kern = (jnp.einsum("kgpn,gnq->kgpq", car[:el], bbar_r, precision=hi)
            - jnp.einsum("kgpn,gnq->kgpq", cai[:el], bbar_i, precision=hi))
    kern = kern.at[0].add(jnp.eye(p, dtype=F32)[None] * d[:, :, None])
    eye = jnp.eye(lg, dtype=F32)
    ji = jnp.arange(el)[:, None]
    jo = jnp.arange(el)[None, :]
    kt = kern[jnp.clip(jo - ji, 0, el - 1)] * (jo >= ji).astype(F32)[:, :, None, None, None]
    kt = kt.reshape(el, el, nb, lg, p, p)
    tm = jnp.einsum("ijbgpq,gh->bigqjhp", kt, eye).reshape(nb, el * LANES, el * LANES)
    rev = el - 1 - jnp.arange(el)
    wr = pr[rev][..., None] * bbar_r[None] - pi[rev][..., None] * bbar_i[None]
    wi = pr[rev][..., None] * bbar_i[None] + pi[rev][..., None] * bbar_r[None]
    w_r = jnp.einsum("jbgnq,gh->bjgqhn", wr.reshape(el, nb, lg, n, p), eye).reshape(nb, el * LANES, lg * n)
    w_i = jnp.einsum("jbgnq,gh->bjgqhn", wi.reshape(el, nb, lg, n, p), eye).reshape(nb, el * LANES, lg * n)
    g_r = jnp.einsum("jbgpn,gh->bgnjhp", car[1:].reshape(el, nb, lg, p, n), eye).reshape(nb, lg * n, el * LANES)
    g_i = jnp.einsum("jbgpn,gh->bgnjhp", -cai[1:].reshape(el, nb, lg, p, n), eye).reshape(nb, lg * n, el * LANES)
    a = jnp.stack([pr[el].reshape(nb, lg * n), pi[el].reshape(nb, lg * n)], axis=1)
    return tm, w_r, w_i, g_r, g_i, a


def _s5a_state(u2, w_r, w_i, a8):
    nb, nc, _ = u2.shape
    sc = STATE_COLS

    def body(u_ref, wr_ref, wi_ref, a_ref, hr_ref, hi_ref, xr_s, xi_s):
        u = u_ref[0]
        xr_s[...] = _dot(u, wr_ref[0])
        xi_s[...] = _dot(u, wi_ref[0])
        ar = a_ref[0, 0:1, :]
        ai = a_ref[0, 1:2, :]

        def step(c, carry):
            hr, hi = carry
            hr_ref[0, pl.ds(c, 1), :] = hr
            hi_ref[0, pl.ds(c, 1), :] = hi
            xr = xr_s[pl.ds(c, 1), :]
            xi = xi_s[pl.ds(c, 1), :]
            return ar * hr - ai * hi + xr, ar * hi + ai * hr + xi

        z = jnp.zeros((1, sc), F32)
        lax.fori_loop(0, nc, step, (z, z))

    return pl.pallas_call(
        body, name="s5a_state", grid=(nb,),
        in_specs=[pl.BlockSpec((1, nc, CHUNK_COLS), lambda b: (b, 0, 0)),
                  pl.BlockSpec((1, CHUNK_COLS, sc), lambda b: (b, 0, 0)),
                  pl.BlockSpec((1, CHUNK_COLS, sc), lambda b: (b, 0, 0)),
                  pl.BlockSpec((1, 8, sc), lambda b: (b, 0, 0))],
        out_specs=[pl.BlockSpec((1, nc, sc), lambda b: (b, 0, 0))] * 2,
        out_shape=[jax.ShapeDtypeStruct((nb, nc, sc), F32)] * 2,
        scratch_shapes=[pltpu.VMEM((nc, sc), F32), pltpu.VMEM((nc, sc), F32)],
        compiler_params=_params(("arbitrary",)),
    )(u2, w_r, w_i, a8)


def _s5b_out(u2, tm, g_r, g_i, hr, hi):
    nb, nc, _ = u2.shape
    sc = STATE_COLS
    cw = 512
    nt = CHUNK_COLS // cw

    def body(u_ref, tm_ref, gr_ref, gi_ref, hr_ref, hi_ref, y_ref):
        y = _dot(u_ref[0], tm_ref[0])
        y += _dot(hr_ref[0].astype(BF16), gr_ref[0])
        y += _dot(hi_ref[0].astype(BF16), gi_ref[0])
        y_ref[0] = y

    return pl.pallas_call(
        body, name="s5b_out", grid=(nb, nt),
        in_specs=[pl.BlockSpec((1, nc, CHUNK_COLS), lambda b, j: (b, 0, 0)),
                  pl.BlockSpec((1, CHUNK_COLS, cw), lambda b, j: (b, 0, j)),
                  pl.BlockSpec((1, sc, cw), lambda b, j: (b, 0, j)),
                  pl.BlockSpec((1, sc, cw), lambda b, j: (b, 0, j)),
                  pl.BlockSpec((1, nc, sc), lambda b, j: (b, 0, 0)),
                  pl.BlockSpec((1, nc, sc), lambda b, j: (b, 0, 0))],
        out_specs=pl.BlockSpec((1, nc, cw), lambda b, j: (b, 0, j)),
        out_shape=jax.ShapeDtypeStruct((nb, nc, CHUNK_COLS), F32),
        compiler_params=_params(("arbitrary", "arbitrary")),
    )(u2, tm, g_r, g_i, hr, hi)


def _f4_mixer(ys4, prest, x, modv, cvv, cw32, w_a, w_b, w_pw, w_out, tb):
    t = x.shape[0]
    hb = tb // CONV_HALO

    def body(ys_ref, pr_ref, halo_ref, x_ref, modv_ref, cvv_ref, cw_ref, wa_ref, wb_ref, wpw_ref, wout_ref,
             r1_ref, ya_ref, yb_ref, ycv_ref, vc_ref, yg_ref, vs_ref, mg_ref, vbuf):
        i = pl.program_id(0)
        ys = jnp.concatenate([ys_ref[b] for b in range(N_LANE_BLOCKS)], axis=-1)
        yg = _gelu(ys).astype(BF16)
        yg_ref[...] = yg
        ya = _dot(yg, wa_ref[...])
        yb = _dot(yg, wb_ref[...])
        ya_ref[...] = ya.astype(BF16)
        yb_ref[...] = yb.astype(BF16)
        yssm = ya * _sig(yb)
        hv = halo_ref[:, 0:CONV_WIDTH] * _sig(halo_ref[:, CONV_WIDTH:2 * CONV_WIDTH])
        vbuf[0:CONV_HALO, :] = jnp.where(i == 0, 0.0, hv)
        vbuf[CONV_HALO:, :] = pr_ref[:, 0:CONV_WIDTH] * _sig(pr_ref[:, CONV_WIDTH:2 * CONV_WIDTH])
        acc = jnp.zeros((tb, CONV_WIDTH), F32)
        for k in range(CONV_KERNEL):
            acc += vbuf[pl.ds(CONV_HALO - CONV_KERNEL + 1 + k, tb), :] * cw_ref[k:k + 1, :]
        vc = acc + cvv_ref[0:1, :]
        vc_ref[...] = vc
        xh, _ = _ln_stats(vc)
        vl = xh * cvv_ref[1:2, :] + cvv_ref[2:3, :]
        vs = (vl * _sig(vl)).astype(BF16)
        vs_ref[...] = vs
        ycv = _dot(vs, wpw_ref[...])
        ycv_ref[...] = ycv.astype(BF16)
        gs = pr_ref[:, 2 * CONV_WIDTH:2 * CONV_WIDTH + D_MODEL]
        gc = pr_ref[:, 2 * CONV_WIDTH + D_MODEL:]
        merged = (_sig(gs) * yssm + _sig(gc) * ycv).astype(BF16)
        mg_ref[...] = merged
        ym = _dot(merged, wout_ref[...])
        r1_ref[...] = ALPHA * x_ref[...] + modv_ref[2:3, :] * ym

    tok = lambda w: pl.BlockSpec((tb, w), lambda i: (i, 0))
    return pl.pallas_call(
        body, name="f4_mixer", grid=(t // tb,),
        in_specs=[pl.BlockSpec((N_LANE_BLOCKS, tb, LANES), lambda i: (0, i, 0)), tok(prest.shape[1]),
                  pl.BlockSpec((CONV_HALO, 2 * CONV_WIDTH), lambda i: (jnp.maximum(i * hb - 1, 0), 0)),
                  tok(D_MODEL), _full(modv), _full(cvv), _full(cw32), _full(w_a), _full(w_b), _full(w_pw), _full(w_out)],
        out_specs=[tok(D_MODEL), tok(D_MODEL), tok(D_MODEL), tok(D_MODEL), tok(CONV_WIDTH), tok(SSM_WIDTH),
                   tok(CONV_WIDTH), tok(D_MODEL)],
        out_shape=[jax.ShapeDtypeStruct((t, D_MODEL), F32), jax.ShapeDtypeStruct((t, D_MODEL), BF16),
                   jax.ShapeDtypeStruct((t, D_MODEL), BF16), jax.ShapeDtypeStruct((t, D_MODEL), BF16),
                   jax.ShapeDtypeStruct((t, CONV_WIDTH), F32), jax.ShapeDtypeStruct((t, SSM_WIDTH), BF16),
                   jax.ShapeDtypeStruct((t, CONV_WIDTH), BF16), jax.ShapeDtypeStruct((t, D_MODEL), BF16)],
        scratch_shapes=[pltpu.VMEM((tb + CONV_HALO, CONV_WIDTH), F32)],
        compiler_params=_params(("arbitrary",)),
    )(ys4, prest, prest, x, modv, cvv, cw32, w_a, w_b, w_pw, w_out)


FFN_COLS = 1408


def _f5_ffn(r1, tgt, modv, lnv, fdw, w_up, w_down, tb):
    t = r1.shape[0]
    fw = 2 * FFN_HIDDEN

    def body(r1_ref, tgt_ref, modv_ref, lnv_ref, fdw_ref, wup_hbm, wdn_hbm,
             dr2_ref, up_ref, h2_ref, z_ref, acc_ref, wup_v, wdn_v, upbuf, sems):
        i = pl.program_id(0)

        @pl.when(i == 0)
        def _():
            _load_once(wup_hbm, wup_v, sems.at[0])
            _load_once(wdn_hbm, wdn_v, sems.at[1])
            acc_ref[...] = jnp.zeros_like(acc_ref)
            upbuf[0:8, :] = jnp.zeros((8, fw), F32)

        xh1, _ = _ln_stats(r1_ref[...])
        x1 = xh1 * lnv_ref[0:1, :] + lnv_ref[1:2, :]
        xn2, _ = _ln_stats(x1)
        h2 = (xn2 * (1.0 + modv_ref[4:5, :]) + modv_ref[3:4, :]).astype(BF16)
        h2_ref[...] = h2
        for j in range(fw // FFN_COLS):
            cs = slice(j * FFN_COLS, (j + 1) * FFN_COLS)
            up = _dot(h2, wup_v[:, cs])
            upbuf[8:, cs] = up
            up_ref[:, cs] = up.astype(BF16)

        def conv(cs):
            return (fdw_ref[0:1, cs] * upbuf[pl.ds(6, tb), cs] + fdw_ref[1:2, cs] * upbuf[pl.ds(7, tb), cs]
                    + fdw_ref[2:3, cs] * upbuf[pl.ds(8, tb), cs] + fdw_ref[3:4, cs])

        yf = jnp.zeros((tb, D_MODEL), F32)
        for j in range(FFN_HIDDEN // FFN_COLS):
            ca = slice(j * FFN_COLS, (j + 1) * FFN_COLS)
            cv = slice(FFN_HIDDEN + j * FFN_COLS, FFN_HIDDEN + (j + 1) * FFN_COLS)
            z = (_gelu(conv(ca)) * conv(cv)).astype(BF16)
            z_ref[:, ca] = z
            yf += _dot(z, wdn_v[ca, :])
        upbuf[0:8, :] = upbuf[pl.ds(tb, 8), :]
        r2 = ALPHA * x1 + modv_ref[5:6, :] * yf
        xh2, rstd2 = _ln_stats(r2)
        e = xh2 * lnv_ref[2:3, :] + lnv_ref[3:4, :] - tgt_ref[...]
        dx2 = e * (1.0 / D_MODEL)
        acc_ref[3:4, :] += _colsum(e * e) * (0.5 / D_MODEL)
        acc_ref[0:1, :] += _colsum(dx2 * xh2)
        acc_ref[1:2, :] += _colsum(dx2)
        dr2 = _ln_bwd(dx2 * lnv_ref[2:3, :], xh2, rstd2)
        dr2_ref[...] = dr2
        acc_ref[2:3, :] += _colsum(dr2 * yf)

    tok = lambda w: pl.BlockSpec((tb, w), lambda i: (i, 0))
    return pl.pallas_call(
        body, name="f5_ffn", grid=(t // tb,),
        in_specs=[tok(D_MODEL), tok(D_MODEL), _full(modv), _full(lnv), _full(fdw), ANY, ANY],
        out_specs=[tok(D_MODEL), tok(fw), tok(D_MODEL), tok(FFN_HIDDEN), pl.BlockSpec((8, D_MODEL), lambda i: (0, 0))],
        out_shape=[jax.ShapeDtypeStruct((t, D_MODEL), F32), jax.ShapeDtypeStruct((t, fw), BF16),
                   jax.ShapeDtypeStruct((t, D_MODEL), BF16), jax.ShapeDtypeStruct((t, FFN_HIDDEN), BF16),
                   jax.ShapeDtypeStruct((8, D_MODEL), F32)],
        scratch_shapes=[pltpu.VMEM(w_up.shape, BF16), pltpu.VMEM(w_down.shape, BF16),
                        pltpu.VMEM((tb + 8, fw), F32), pltpu.SemaphoreType.DMA((2,))],
        compiler_params=_params(("arbitrary",)),
    )(r1, tgt, modv, lnv, fdw, w_up, w_down)


def _b1a_ffn_down(dr2, up, modv, fdw, w_down, tb):
    t = dr2.shape[0]
    fw = 2 * FFN_HIDDEN
    nt = t // tb
    hb = tb // 16

    def body(dr2_ref, up_ref, halo_ref, modv_ref, fdw_ref, wdn_hbm, dup_ref, dyf_ref, acc_ref, wdn_v, upbuf, dbuf, sem):
        i = pl.program_id(0)
        ti = nt - 1 - i

        @pl.when(i == 0)
        def _():
            _load_once(wdn_hbm, wdn_v, sem)
            acc_ref[...] = jnp.zeros_like(acc_ref)
            dbuf[pl.ds(tb, 8), :] = jnp.zeros((8, fw), F32)

        dyf = (modv_ref[5:6, :] * dr2_ref[...]).astype(BF16)
        dyf_ref[...] = dyf
        upbuf[0:16, :] = jnp.where(ti == 0, 0.0, halo_ref[...].astype(F32))
        upbuf[16:, :] = up_ref[...].astype(F32)

        def conv(cs):
            return (fdw_ref[0:1, cs] * upbuf[pl.ds(14, tb), cs] + fdw_ref[1:2, cs] * upbuf[pl.ds(15, tb), cs]
                    + fdw_ref[2:3, cs] * upbuf[pl.ds(16, tb), cs] + fdw_ref[3:4, cs])

        for j in range(FFN_HIDDEN // FFN_COLS):
            ca = slice(j * FFN_COLS, (j + 1) * FFN_COLS)
            cv = slice(FFN_HIDDEN + j * FFN_COLS, FFN_HIDDEN + (j + 1) * FFN_COLS)
            a = conv(ca)
            v = conv(cv)
            dz = _dot_nt(dyf, wdn_v[ca, :])
            dbuf[0:tb, ca] = dz * v * _dgelu(a)
            dbuf[0:tb, cv] = dz * _gelu(a)
        for j in range(fw // FFN_COLS):
            cs = slice(j * FFN_COLS, (j + 1) * FFN_COLS)
            d0 = dbuf[pl.ds(0, tb), cs]
            dup = fdw_ref[2:3, cs] * d0 + fdw_ref[1:2, cs] * dbuf[pl.ds(1, tb), cs] + fdw_ref[0:1, cs] * dbuf[pl.ds(2, tb), cs]
            dup_ref[:, cs] = dup.astype(BF16)
            for k in range(FFN_KERNEL):
                acc_ref[k:k + 1, cs] += _colsum(d0 * upbuf[pl.ds(14 + k, tb), cs])
            acc_ref[3:4, cs] += _colsum(d0)
        dbuf[pl.ds(tb, 8), :] = dbuf[0:8, :]

    rtok = lambda w: pl.BlockSpec((tb, w), lambda i: (nt - 1 - i, 0))
    return pl.pallas_call(
        body, name="b1a_ffn_down", grid=(nt,),
        in_specs=[rtok(D_MODEL), rtok(fw),
                  pl.BlockSpec((16, fw), lambda i: (jnp.maximum((nt - 1 - i) * hb - 1, 0), 0)),
                  _full(modv), _full(fdw), ANY],
        out_specs=[rtok(fw), rtok(D_MODEL), pl.BlockSpec((8, fw), lambda i: (0, 0))],
        out_shape=[jax.ShapeDtypeStruct((t, fw), BF16), jax.ShapeDtypeStruct((t, D_MODEL), BF16),
                   jax.ShapeDtypeStruct((8, fw), F32)],
        scratch_shapes=[pltpu.VMEM(w_down.shape, BF16), pltpu.VMEM((tb + 16, fw), F32),
                        pltpu.VMEM((tb + 8, fw), F32), pltpu.SemaphoreType.DMA],
        compiler_params=_params(("arbitrary",)),
    )(dr2, up, up, modv, fdw, w_down)


def _b1b_ffn_up(dup, dr2, r1, modv, lnv, w_up, tb):
    t = dr2.shape[0]
    fw = 2 * FFN_HIDDEN

    def body(dup_ref, dr2_ref, r1_ref, modv_ref, lnv_ref, wup_hbm, dr1_ref, acc_ref, wup_v, sem):
        @pl.when(pl.program_id(0) == 0)
        def _():
            _load_once(wup_hbm, wup_v, sem)
            acc_ref[...] = jnp.zeros_like(acc_ref)

        xh1, rstd1 = _ln_stats(r1_ref[...])
        x1 = xh1 * lnv_ref[0:1, :] + lnv_ref[1:2, :]
        xn2, rstd2 = _ln_stats(x1)
        dh2 = _dot_nt(dup_ref[...], wup_v[...])
        acc_ref[0:1, :] += _colsum(dh2 * xn2)
        acc_ref[1:2, :] += _colsum(dh2)
        dx1 = _ln_bwd(dh2 * (1.0 + modv_ref[4:5, :]), xn2, rstd2) + ALPHA * dr2_ref[...]
        acc_ref[2:3, :] += _colsum(dx1 * xh1)
        acc_ref[3:4, :] += _colsum(dx1)
        dr1_ref[...] = _ln_bwd(dx1 * lnv_ref[0:1, :], xh1, rstd1)

    tok = lambda w: pl.BlockSpec((tb, w), lambda i: (i, 0))
    return pl.pallas_call(
        body, name="b1b_ffn_up", grid=(t // tb,),
        in_specs=[tok(fw), tok(D_MODEL), tok(D_MODEL), _full(modv), _full(lnv), ANY],
        out_specs=[tok(D_MODEL), pl.BlockSpec((8, D_MODEL), lambda i: (0, 0))],
        out_shape=[jax.ShapeDtypeStruct((t, D_MODEL), F32), jax.ShapeDtypeStruct((8, D_MODEL), F32)],
        scratch_shapes=[pltpu.VMEM(w_up.shape, BF16), pltpu.SemaphoreType.DMA],
        compiler_params=_params(("arbitrary",)),
    )(dup, dr2, r1, modv, lnv, w_up)


def _b2_mixer(dr1, ys4, prest, ya, yb, ycv, vc, merged, modv, cvv, cw32, w_a, w_b, w_pw, w_out, tb):
    t = dr1.shape[0]
    nt = t // tb
    hb = tb // CONV_HALO
    cwd = CONV_WIDTH

    def body(dr1_ref, ys_ref, pr_ref, halo_ref, ya_ref, yb_ref, ycv_ref, vc_ref, mg_ref, modv_ref, cvv_ref, cw_ref,
             wa_ref, wb_ref, wpw_ref, wout_ref,
             dys_ref, dpr_ref, dya_ref, dyb_ref, dycv_ref, dym_ref, acc_a, acc_b, acc_w, vbuf, dvbuf):
        i = pl.program_id(0)
        ti = nt - 1 - i

        @pl.when(i == 0)
        def _():
            acc_a[...] = jnp.zeros_like(acc_a)
            acc_b[...] = jnp.zeros_like(acc_b)
            acc_w[...] = jnp.zeros_like(acc_w)
            dvbuf[pl.ds(tb, CONV_HALO), :] = jnp.zeros((CONV_HALO, cwd), F32)

        dr1 = dr1_ref[...]
        dym = (modv_ref[2:3, :] * dr1).astype(BF16)
        dym_ref[...] = dym
        ym = _dot(mg_ref[...], wout_ref[...])
        acc_a[0:1, :] += _colsum(dr1 * ym)
        dmg = _dot_nt(dym, wout_ref[...])
        sgs = _sig(pr_ref[:, 2 * cwd:2 * cwd + D_MODEL])
        sgc = _sig(pr_ref[:, 2 * cwd + D_MODEL:])
        ya_v = ya_ref[...].astype(F32)
        syb = _sig(yb_ref[...].astype(F32))
        ycv_v = ycv_ref[...].astype(F32)
        dpr_ref[:, 2 * cwd:2 * cwd + D_MODEL] = (dmg * (ya_v * syb) * sgs * (1.0 - sgs)).astype(BF16)
        dpr_ref[:, 2 * cwd + D_MODEL:] = (dmg * ycv_v * sgc * (1.0 - sgc)).astype(BF16)
        dyssm = dmg * sgs
        dya = (dyssm * syb).astype(BF16)
        dyb = (dyssm * ya_v * syb * (1.0 - syb)).astype(BF16)
        dya_ref[...] = dya
        dyb_ref[...] = dyb
        dyg = _dot_nt(dya, wa_ref[...]) + _dot_nt(dyb, wb_ref[...])
        ys = jnp.concatenate([ys_ref[b] for b in range(N_LANE_BLOCKS)], axis=-1)
        dys = dyg * _dgelu(ys)
        for b in range(N_LANE_BLOCKS):
            dys_ref[b] = dys[:, b * LANES:(b + 1) * LANES].astype(BF16)
        dycv = (dmg * sgc).astype(BF16)
        dycv_ref[...] = dycv
        dvs = _dot_nt(dycv, wpw_ref[...])
        xh, rstd = _ln_stats(vc_ref[...])
        vl = xh * cvv_ref[1:2, :] + cvv_ref[2:3, :]
        s = _sig(vl)
        dvl = dvs * s * (1.0 + vl * (1.0 - s))
        acc_b[1:2, :] += _colsum(dvl * xh)
        acc_b[2:3, :] += _colsum(dvl)
        dvc = _ln_bwd(dvl * cvv_ref[1:2, :], xh, rstd)
        acc_b[0:1, :] += _colsum(dvc)
        hv = halo_ref[:, 0:cwd] * _sig(halo_ref[:, cwd:2 * cwd])
        vbuf[0:CONV_HALO, :] = jnp.where(ti == 0, 0.0, hv)
        cva = pr_ref[:, 0:cwd]
        scg = _sig(pr_ref[:, cwd:2 * cwd])
        vbuf[CONV_HALO:, :] = cva * scg
        dvbuf[0:tb, :] = dvc
        dv = jnp.zeros((tb, cwd), F32)
        for k in range(CONV_KERNEL):
            dv += dvbuf[pl.ds(CONV_KERNEL - 1 - k, tb), :] * cw_ref[k:k + 1, :]
            acc_w[k:k + 1, :] += _colsum(dvc * vbuf[pl.ds(CONV_HALO - CONV_KERNEL + 1 + k, tb), :])
        dvbuf[pl.ds(tb, CONV_HALO), :] = dvbuf[0:CONV_HALO, :]
        dpr_ref[:, 0:cwd] = (dv * scg).astype(BF16)
        dpr_ref[:, cwd:2 * cwd] = (dv * cva * scg * (1.0 - scg)).astype(BF16)

    rtok = lambda w: pl.BlockSpec((tb, w), lambda i: (nt - 1 - i, 0))
    r4 = pl.BlockSpec((N_LANE_BLOCKS, tb, LANES), lambda i: (0, nt - 1 - i, 0))
    pw = prest.shape[1]
    return pl.pallas_call(
        body, name="b2_mixer", grid=(nt,),
        in_specs=[rtok(D_MODEL), r4, rtok(pw),
                  pl.BlockSpec((CONV_HALO, 2 * cwd), lambda i: (jnp.maximum((nt - 1 - i) * hb - 1, 0), 0)),
                  rtok(D_MODEL), rtok(D_MODEL), rtok(D_MODEL), rtok(cwd), rtok(D_MODEL),
                  _full(modv), _full(cvv), _full(cw32), _full(w_a), _full(w_b), _full(w_pw), _full(w_out)],
        out_specs=[r4, rtok(pw), rtok(D_MODEL), rtok(D_MODEL), rtok(D_MODEL), rtok(D_MODEL),
                   pl.BlockSpec((8, D_MODEL), lambda i: (0, 0)), pl.BlockSpec((8, cwd), lambda i: (0, 0)),
                   pl.BlockSpec((CONV_HALO, cwd), lambda i: (0, 0))],
        out_shape=[jax.ShapeDtypeStruct((N_LANE_BLOCKS, t, LANES), BF16), jax.ShapeDtypeStruct((t, pw), BF16),
                   jax.ShapeDtypeStruct((t, D_MODEL), BF16), jax.ShapeDtypeStruct((t, D_MODEL), BF16),
                   jax.ShapeDtypeStruct((t, D_MODEL), BF16), jax.ShapeDtypeStruct((t, D_MODEL), BF16),
                   jax.ShapeDtypeStruct((8, D_MODEL), F32), jax.ShapeDtypeStruct((8, cwd), F32),
                   jax.ShapeDtypeStruct((CONV_HALO, cwd), F32)],
        scratch_shapes=[pltpu.VMEM((tb + CONV_HALO, cwd), F32), pltpu.VMEM((tb + CONV_HALO, cwd), F32)],
        compiler_params=_params(("arbitrary",)),
    )(dr1, ys4, prest, prest, ya, yb, ycv, vc, merged, modv, cvv, cw32, w_a, w_b, w_pw, w_out)


def _s5c_state_bwd(dy2, g_r, g_i, a8, hr, hi):
    nb, nc, _ = dy2.shape
    sc = STATE_COLS

    def body(dy_ref, gr_ref, gi_ref, a_ref, hr_ref, hi_ref, dxr_ref, dxi_ref, da_ref, dgr_ref, dgi_ref,
             lr_s, li_s, xr_s, xi_s):
        dy = dy_ref[0]
        lr_s[...] = _dot_nt(dy, gr_ref[0])
        li_s[...] = _dot_nt(dy, gi_ref[0])
        ar = a_ref[0, 0:1, :]
        ai = a_ref[0, 1:2, :]

        def step(k, carry):
            pr, pi, dar, dai = carry
            c = nc - 1 - k
            xr_s[pl.ds(c, 1), :] = pr
            xi_s[pl.ds(c, 1), :] = pi
            h_r = hr_ref[0, pl.ds(c, 1), :]
            h_i = hi_ref[0, pl.ds(c, 1), :]
            dar = dar + pr * h_r + pi * h_i
            dai = dai - pr * h_i + pi * h_r
            nr = lr_s[pl.ds(c, 1), :] + ar * pr + ai * pi
            ni = li_s[pl.ds(c, 1), :] - ai * pr + ar * pi
            return nr, ni, dar, dai

        z = jnp.zeros((1, sc), F32)
        _, _, dar, dai = lax.fori_loop(0, nc, step, (z, z, z, z))
        da_ref[0] = jnp.concatenate([dar, dai, jnp.zeros((6, sc), F32)], axis=0)
        dxr_ref[0] = xr_s[...].astype(BF16)
        dxi_ref[0] = xi_s[...].astype(BF16)
        dgr_ref[0] = _dot_tn(hr_ref[0].astype(BF16), dy)
        dgi_ref[0] = _dot_tn(hi_ref[0].astype(BF16), dy)

    blk = lambda r, c: pl.BlockSpec((1, r, c), lambda b: (b, 0, 0))
    return pl.pallas_call(
        body, name="s5c_state_bwd", grid=(nb,),
        in_specs=[blk(nc, CHUNK_COLS), blk(sc, CHUNK_COLS), blk(sc, CHUNK_COLS), blk(8, sc), blk(nc, sc), blk(nc, sc)],
        out_specs=[blk(nc, sc), blk(nc, sc), blk(8, sc), blk(sc, CHUNK_COLS), blk(sc, CHUNK_COLS)],
        out_shape=[jax.ShapeDtypeStruct((nb, nc, sc), BF16), jax.ShapeDtypeStruct((nb, nc, sc), BF16),
                   jax.ShapeDtypeStruct((nb, 8, sc), F32), jax.ShapeDtypeStruct((nb, sc, CHUNK_COLS), F32),
                   jax.ShapeDtypeStruct((nb, sc, CHUNK_COLS), F32)],
        scratch_shapes=[pltpu.VMEM((nc, sc), F32)] * 4,
        compiler_params=_params(("arbitrary",)),
    )(dy2, g_r, g_i, a8, hr, hi)


def _s5d_input_bwd(dy2, u2, tm, w_r, w_i, dxr, dxi):
    nb, nc, _ = dy2.shape
    sc = STATE_COLS
    rw = 512
    nt = CHUNK_COLS // rw

    def body(dy_ref, u_ref, tm_ref, wr_ref, wi_ref, dxr_ref, dxi_ref, du_ref, dtm_ref, dwr_ref, dwi_ref):
        dy = dy_ref[0]
        u = u_ref[0]
        du = _dot_nt(dy, tm_ref[0]) + _dot_nt(dxr_ref[0], wr_ref[0]) + _dot_nt(dxi_ref[0], wi_ref[0])
        du_ref[0] = du.astype(BF16)
        dtm_ref[0] = _dot_tn(u, dy)
        dwr_ref[0] = _dot_tn(u, dxr_ref[0])
        dwi_ref[0] = _dot_tn(u, dxi_ref[0])

    return pl.pallas_call(
        body, name="s5d_input_bwd", grid=(nb, nt),
        in_specs=[pl.BlockSpec((1, nc, CHUNK_COLS), lambda b, j: (b, 0, 0)),
                  pl.BlockSpec((1, nc, rw), lambda b, j: (b, 0, j)),
                  pl.BlockSpec((1, rw, CHUNK_COLS), lambda b, j: (b, j, 0)),
                  pl.BlockSpec((1, rw, sc), lambda b, j: (b, j, 0)),
                  pl.BlockSpec((1, rw, sc), lambda b, j: (b, j, 0)),
                  pl.BlockSpec((1, nc, sc), lambda b, j: (b, 0, 0)),
                  pl.BlockSpec((1, nc, sc), lambda b, j: (b, 0, 0))],
        out_specs=[pl.BlockSpec((1, nc, rw), lambda b, j: (b, 0, j)),
                   pl.BlockSpec((1, rw, CHUNK_COLS), lambda b, j: (b, j, 0)),
                   pl.BlockSpec((1, rw, sc), lambda b, j: (b, j, 0)),
                   pl.BlockSpec((1, rw, sc), lambda b, j: (b, j, 0))],
        out_shape=[jax.ShapeDtypeStruct((nb, nc, CHUNK_COLS), BF16),
                   jax.ShapeDtypeStruct((nb, CHUNK_COLS, CHUNK_COLS), F32),
                   jax.ShapeDtypeStruct((nb, CHUNK_COLS, sc), F32), jax.ShapeDtypeStruct((nb, CHUNK_COLS, sc), F32)],
        compiler_params=_params(("arbitrary", "arbitrary")),
    )(dy2, u2, tm, w_r, w_i, dxr, dxi)


def _b3_inproj(x, dr1, du4, dprest, modv, w_in, tb):
    t = x.shape[0]
    pw = IN_PROJ_WIDTH - SSM_WIDTH

    def body(x_ref, dr1_ref, du_ref, dpr_ref, modv_ref, w_hbm, gx_ref, dp_ref, acc_ref, accb_ref, w_v, sem):
        @pl.when(pl.program_id(0) == 0)
        def _():
            _load_once(w_hbm, w_v, sem)
            acc_ref[...] = jnp.zeros_like(acc_ref)
            accb_ref[...] = jnp.zeros_like(accb_ref)

        du = jnp.concatenate([du_ref[b] for b in range(N_LANE_BLOCKS)], axis=-1)
        dpr = dpr_ref[...]
        dp_ref[:, 0:SSM_WIDTH] = du
        dp_ref[:, SSM_WIDTH:] = dpr
        accb_ref[0:1, 0:SSM_WIDTH] += _colsum(du.astype(F32))
        accb_ref[0:1, SSM_WIDTH:] += _colsum(dpr.astype(F32))
        dh = _dot_nt(du, w_v[:, 0:SSM_WIDTH]) + _dot_nt(dpr, w_v[:, SSM_WIDTH:])
        xn, rstd = _ln_stats(x_ref[...])
        acc_ref[0:1, :] += _colsum(dh * xn)
        acc_ref[1:2, :] += _colsum(dh)
        gx_ref[...] = _ln_bwd(dh * (1.0 + modv_ref[1:2, :]), xn, rstd) + ALPHA * dr1_ref[...]

    tok = lambda w: pl.BlockSpec((tb, w), lambda i: (i, 0))
    return pl.pallas_call(
        body, name="b3_inproj", grid=(t // tb,),
        in_specs=[tok(D_MODEL), tok(D_MODEL), pl.BlockSpec((N_LANE_BLOCKS, tb, LANES), lambda i: (0, i, 0)), tok(pw),
                  _full(modv), ANY],
        out_specs=[tok(D_MODEL), tok(IN_PROJ_WIDTH), pl.BlockSpec((8, D_MODEL), lambda i: (0, 0)),
                   pl.BlockSpec((8, IN_PROJ_WIDTH), lambda i: (0, 0))],
        out_shape=[jax.ShapeDtypeStruct((t, D_MODEL), F32), jax.ShapeDtypeStruct((t, IN_PROJ_WIDTH), BF16),
                   jax.ShapeDtypeStruct((8, D_MODEL), F32), jax.ShapeDtypeStruct((8, IN_PROJ_WIDTH), F32)],
        scratch_shapes=[pltpu.VMEM(w_in.shape, BF16), pltpu.SemaphoreType.DMA],
        compiler_params=_params(("arbitrary",)),
    )(x, dr1, du4, dprest, modv, w_in)


def _tn_matmul(name, a, b, tm, tn):
    t, m = a.shape
    n = b.shape[1]
    tt = min(512, t)
    nk = t // tt

    def body(a_ref, b_ref, o_ref, acc):
        k = pl.program_id(2)

        @pl.when(k == 0)
        def _():
            acc[...] = jnp.zeros_like(acc)

        acc[...] += _dot_tn(a_ref[...], b_ref[...])

        @pl.when(k == nk - 1)
        def _():
            o_ref[...] = acc[...].astype(BF16)

    return pl.pallas_call(
        body, name=name, grid=(m // tm, n // tn, nk),
        in_specs=[pl.BlockSpec((tt, tm), lambda i, j, k: (k, i)), pl.BlockSpec((tt, tn), lambda i, j, k: (k, j))],
        out_specs=pl.BlockSpec((tm, tn), lambda i, j, k: (i, j)),
        out_shape=jax.ShapeDtypeStruct((m, n), BF16),
        scratch_shapes=[pltpu.VMEM((tm, tn), F32)],
        compiler_params=_params(("arbitrary", "arbitrary", "arbitrary")),
    )(a, b)


def _local_step(x, tgt, modv, wb, sp, tb=256):
    t = x.shape[0]
    nc = t // SSM_CHUNK
    row8 = lambda rows, w: jnp.concatenate([r.reshape(1, w) for r in rows] + [jnp.zeros((8 - len(rows), w), F32)], axis=0)
    lnv = row8([sp["ln1_g"], sp["ln1_b"], sp["ln2_g"], sp["ln2_b"]], D_MODEL)
    cvv = row8([sp["cv_dw_b"], sp["cv_ln_g"], sp["cv_ln_b"]], CONV_WIDTH)
    cw32 = jnp.concatenate([sp["cv_dw_w"].reshape(CONV_KERNEL, CONV_WIDTH), jnp.zeros((1, CONV_WIDTH), F32)], axis=0)
    fdw = row8(list(sp["ffn_dw_w"].reshape(FFN_KERNEL, 2 * FFN_HIDDEN)) + [sp["ffn_dw_b"]], 2 * FFN_HIDDEN)
    b_in = sp["b_in"].reshape(1, IN_PROJ_WIDTH)
    ssm = tuple(sp[k] for k in ("ssm_lambda_re", "ssm_lambda_im", "ssm_log_dt", "ssm_b_re", "ssm_b_im", "ssm_c_re",
                                "ssm_c_im", "ssm_d"))
    mats, ssm_vjp = jax.vjp(_s5_build, *ssm)
    tm, w_r, w_i, g_r, g_i = (m.astype(BF16) for m in mats[:5])
    a = mats[5]
    a8 = jnp.concatenate([a, jnp.zeros((N_LANE_BLOCKS, 6, STATE_COLS), F32)], axis=1)

    u4, prest, h1 = _f1_inproj(x, modv, b_in, wb["w_in"], tb)
    u2 = u4.reshape(N_LANE_BLOCKS, nc, CHUNK_COLS)
    hr, hi = _s5a_state(u2, w_r, w_i, a8)
    ys4 = _s5b_out(u2, tm, g_r, g_i, hr, hi).reshape(N_LANE_BLOCKS, t, LANES)
    r1, ya, yb, ycv, vc, yg, vs, merged = _f4_mixer(ys4, prest, x, modv, cvv, cw32, wb["ssm_glu_w_a"],
                                                    wb["ssm_glu_w_b"], wb["cv_w_pw"], wb["w_out"], tb)
    dr2, up, h2, z, acc5 = _f5_ffn(r1, tgt, modv, lnv, fdw, wb["ffn_w_up"], wb["ffn_w_down"], tb)
    dup, dyf, acc1a = _b1a_ffn_down(dr2, up, modv, fdw, wb["ffn_w_down"], tb)
    dr1, acc1b = _b1b_ffn_up(dup, dr2, r1, modv, lnv, wb["ffn_w_up"], tb)
    dys4, dprest, dya, dyb, dycv, dym, acc2a, acc2b, acc2w = _b2_mixer(
        dr1, ys4, prest, ya, yb, ycv, vc, merged, modv, cvv, cw32, wb["ssm_glu_w_a"], wb["ssm_glu_w_b"],
        wb["cv_w_pw"], wb["w_out"], tb)
    dy2 = dys4.reshape(N_LANE_BLOCKS, nc, CHUNK_COLS)
    dxr, dxi, da8, dg_r, dg_i = _s5c_state_bwd(dy2, g_r, g_i, a8, hr, hi)
    du2, dtm, dw_r, dw_i = _s5d_input_bwd(dy2, u2, tm, w_r, w_i, dxr, dxi)
    dssm = ssm_vjp((dtm, dw_r, dw_i, dg_r, dg_i, da8[:, 0:2, :]))
    gx, dp, acc3, acc3b = _b3_inproj(x, dr1, du2.reshape(N_LANE_BLOCKS, t, LANES), dprest, modv, wb["w_in"], tb)

    dbig = {
        "w_in": _tn_matmul("dw_in", h1, dp, 512, 896),
        "ssm_glu_w_a": _tn_matmul("dw_glu_a", yg, dya, 512, 1024),
        "ssm_glu_w_b": _tn_matmul("dw_glu_b", yg, dyb, 512, 1024),
        "cv_w_pw": _tn_matmul("dw_pw", vs, dycv, 512, 1024),
        "w_out": _tn_matmul("dw_out", merged, dym, 512, 1024),
        "ffn_w_up": _tn_matmul("dw_up", h2, dup, 512, FFN_COLS),
        "ffn_w_down": _tn_matmul("dw_down", z, dyf, FFN_COLS, 1024),
    }
    dmod = jnp.concatenate([acc3[1], acc3[0], acc2a[0], acc1b[1], acc1b[0], acc5[2]])
    small = {
        "dmod": dmod, "b_in": acc3b[0],
        "ssm_lambda_re": dssm[0], "ssm_lambda_im": dssm[1], "ssm_log_dt": dssm[2], "ssm_b_re": dssm[3],
        "ssm_b_im": dssm[4], "ssm_c_re": dssm[5], "ssm_c_im": dssm[6], "ssm_d": dssm[7],
        "cv_dw_w": acc2w[0:CONV_KERNEL], "cv_dw_b": acc2b[0], "cv_ln_g": acc2b[1], "cv_ln_b": acc2b[2],
        "ln1_g": acc1b[2], "ln1_b": acc1b[3], "ffn_dw_w": acc1a[0:FFN_KERNEL], "ffn_dw_b": acc1a[3],
        "ln2_g": acc5[0], "ln2_b": acc5[1], "loss": jnp.sum(acc5[3]).reshape(1),
    }
    return gx, dbig, small


def _place():
    x, y, c = lax.axis_index("x"), lax.axis_index("y"), lax.axis_index("c")
    chips = [(1 - x, y), (x, 1 - y), (1 - x, 1 - y)]
    return x, y, c, chips


def _allgather(name, shard):
    m_per, n = shard.shape

    def body(x_ref, out_ref, send_sems, recv_sems, local_sem):
        x, y, c, chips = _place()
        me, sibling = (x, y, c), (x, y, 1 - c)

        def rows(px, py, pc):
            return out_ref.at[pl.ds((4 * px + 2 * py + pc) * m_per, m_per), :]

        def copy(k, block, to, src=None):
            return pltpu.make_async_remote_copy(
                src_ref=rows(*block) if src is None else src, dst_ref=rows(*block),
                send_sem=send_sems.at[k], recv_sem=recv_sems.at[k], device_id=to, device_id_type=MESH)

        mine = pltpu.make_async_copy(x_ref, rows(*me), local_sem)
        mine.start()
        first = [copy(0, me, sibling, src=x_ref)]
        first += [copy(1 + j, me, (*chip, c), src=x_ref) for j, chip in enumerate(chips)]
        for cp in first:
            cp.start()
        passed = [copy(4 + j, (*chip, c), sibling) for j, chip in enumerate(chips)]
        for j, chip in enumerate(chips):
            copy(1 + j, (*chip, c), me).wait_recv()
            passed[j].start()
        copy(0, sibling, me).wait_recv()
        for j, chip in enumerate(chips):
            copy(4 + j, (*chip, 1 - c), me).wait_recv()
        for cp in first + passed:
            cp.wait_send()
        mine.wait()

    return pl.pallas_call(
        body, name=name,
        out_shape=jax.ShapeDtypeStruct((N_DEV * m_per, n), shard.dtype),
        in_specs=[pl.BlockSpec(memory_space=pltpu.VMEM)],
        out_specs=pl.BlockSpec(memory_space=pltpu.VMEM),
        scratch_shapes=[pltpu.SemaphoreType.DMA((7,)), pltpu.SemaphoreType.DMA((7,)), pltpu.SemaphoreType.DMA],
        compiler_params=_params(),
    )(shard)


def _piece(kind, shape):
    r, cc = shape
    return (r // 2, cc // N_CHIP) if kind == "col" else (r // (2 * N_CHIP), cc)


def _piece_at(ref, kind, shape, s, k):
    pr, pc = _piece(kind, shape)
    if kind == "col":
        return ref.at[pl.ds(k * pr, pr), pl.ds(pl.multiple_of(s * pc, LANES), pc)]
    return ref.at[pl.ds(pl.multiple_of((2 * s + k) * pr, 16), pr), :]


def _add_rows(pr):
    return 64 if pr % 64 == 0 else 16


def _gather_weights(shards):
    nm = len(BIG)

    def body(*refs):
        ins, outs = refs[:nm], refs[nm:2 * nm]
        stage = refs[2 * nm:3 * nm]
        send, recv, fsend, frecv, lsem = refs[3 * nm:]
        x, y, c, chips = _place()
        s_me = 2 * x + y
        sibling = (x, y, 1 - c)
        pend = []
        for m, (_, kind, shape) in enumerate(BIG):
            stage[m][...] = ins[m][...].astype(BF16)
        for m, (_, kind, shape) in enumerate(BIG):
            pr, pc = _piece(kind, shape)
            for k in range(2):
                cp = pltpu.make_async_copy(stage[m].at[pl.ds(k * pr, pr), :], _piece_at(outs[m], kind, shape, s_me, k),
                                           lsem.at[m, k])
                cp.start()
                pend.append(cp.wait)
            for j, chip in enumerate(chips):
                cp = pltpu.make_async_remote_copy(
                    src_ref=stage[m].at[pl.ds(pl.multiple_of(c * pr, 16), pr), :],
                    dst_ref=_piece_at(outs[m], kind, shape, s_me, c),
                    send_sem=send.at[m, j], recv_sem=recv.at[m, j], device_id=(*chip, c), device_id_type=MESH)
                cp.start()
                pend.append(cp.wait_send)
        for m, (_, kind, shape) in enumerate(BIG):
            for j, (cx, cy) in enumerate(chips):
                got = _piece_at(outs[m], kind, shape, 2 * cx + cy, c)
                pltpu.make_async_remote_copy(src_ref=got, dst_ref=got, send_sem=send.at[m, j], recv_sem=recv.at[m, j],
                                             device_id=(cx, cy, c), device_id_type=MESH).wait_recv()
                cp = pltpu.make_async_remote_copy(src_ref=got, dst_ref=got, send_sem=fsend.at[m, j],
                                                  recv_sem=frecv.at[m, j], device_id=sibling, device_id_type=MESH)
                cp.start()
                pend.append(cp.wait_send)
        for m, (_, kind, shape) in enumerate(BIG):
            for j, (cx, cy) in enumerate(chips):
                got = _piece_at(outs[m], kind, shape, 2 * cx + cy, 1 - c)
                pltpu.make_async_remote_copy(src_ref=got, dst_ref=got, send_sem=fsend.at[m, j], recv_sem=frecv.at[m, j],
                                             device_id=sibling, device_id_type=MESH).wait_recv()
        for w in pend:
            w()

    sem = lambda *s: pltpu.SemaphoreType.DMA(s)
    return pl.pallas_call(
        body, name="gather_weights",
        out_shape=[jax.ShapeDtypeStruct(shape, BF16) for _, _, shape in BIG],
        in_specs=[pl.BlockSpec(memory_space=pltpu.VMEM)] * nm,
        out_specs=[ANY] * nm,
        scratch_shapes=[pltpu.VMEM(s.shape, BF16) for s in shards] + [sem(nm, 3), sem(nm, 3), sem(nm, 3), sem(nm, 3),
                                                                         sem(nm, 2)],
        compiler_params=_params(),
    )(*shards)


def _rs1_sibling(grads):
    nm = len(BIG)

    def body(*refs):
        ins, outs = refs[:nm], refs[nm:2 * nm]
        send, recv = refs[2 * nm:]
        x, y, c, _ = _place()
        cps = []
        for m, (_, kind, shape) in enumerate(BIG):
            for s in range(N_CHIP):
                cp = pltpu.make_async_remote_copy(
                    src_ref=_piece_at(ins[m], kind, shape, s, 1 - c), dst_ref=outs[m].at[s],
                    send_sem=send.at[m, s], recv_sem=recv.at[m, s], device_id=(x, y, 1 - c), device_id_type=MESH)
                cp.start()
                cps.append(cp)
        for cp in cps:
            cp.wait()

    sem = lambda *s: pltpu.SemaphoreType.DMA(s)
    return pl.pallas_call(
        body, name="rs1_sibling",
        out_shape=[jax.ShapeDtypeStruct((N_CHIP,) + _piece(kind, shape), BF16) for _, kind, shape in BIG],
        in_specs=[ANY] * nm, out_specs=[ANY] * nm,
        scratch_shapes=[sem(nm, N_CHIP), sem(nm, N_CHIP)],
        compiler_params=_params(),
    )(*grads)


def _rs2_chips(grads, halves):
    nm = len(BIG)

    def body(*refs):
        gin, hin = refs[:nm], refs[nm:2 * nm]
        own, got = refs[2 * nm:3 * nm], refs[3 * nm:4 * nm]
        send, recv, lsem = refs[4 * nm:]
        x, y, c, chips = _place()
        s_me = 2 * x + y
        for m, (_, kind, shape) in enumerate(BIG):
            pr, pc = _piece(kind, shape)

            def scoped(a, b, m=m, kind=kind, shape=shape, pr=pr):
                loads = [pltpu.make_async_copy(_piece_at(gin[m], kind, shape, s, c), a.at[s], lsem.at[s])
                         for s in range(N_CHIP)]
                loads.append(pltpu.make_async_copy(hin[m], b, lsem.at[N_CHIP]))
                for cp in loads:
                    cp.start()
                for cp in loads:
                    cp.wait()
                step = _add_rows(pr)
                for s in range(N_CHIP):
                    def add(i, _, s=s):
                        r = pl.ds(pl.multiple_of(i * step, 16), step)
                        a[s, r, :] = (a[s, r, :].astype(F32) + b[s, r, :].astype(F32)).astype(BF16)
                        return 0

                    lax.fori_loop(0, pr // step, add, 0)
                waits = []
                for j, (cx, cy) in enumerate(chips):
                    cp = pltpu.make_async_remote_copy(src_ref=a.at[2 * cx + cy], dst_ref=got[m].at[j], send_sem=send.at[m, j],
                                                      recv_sem=recv.at[m, j], device_id=(cx, cy, c), device_id_type=MESH)
                    cp.start()
                    waits.append(cp.wait_send)
                cp = pltpu.make_async_copy(a.at[s_me], own[m], lsem.at[N_CHIP + 1])
                cp.start()
                waits.append(cp.wait)
                for w in waits:
                    w()

            pl.run_scoped(scoped, pltpu.VMEM((N_CHIP, pr, pc), BF16), pltpu.VMEM((N_CHIP, pr, pc), BF16))
        for m in range(nm):
            for j, (cx, cy) in enumerate(chips):
                pltpu.make_async_remote_copy(src_ref=got[m].at[j], dst_ref=got[m].at[j], send_sem=send.at[m, j],
                                             recv_sem=recv.at[m, j], device_id=(cx, cy, c), device_id_type=MESH).wait_recv()

    sem = lambda *s: pltpu.SemaphoreType.DMA(s)
    pieces = [_piece(kind, shape) for _, kind, shape in BIG]
    return pl.pallas_call(
        body, name="rs2_chips",
        out_shape=[jax.ShapeDtypeStruct(p, BF16) for p in pieces] + [jax.ShapeDtypeStruct((3,) + p, BF16) for p in pieces],
        in_specs=[ANY] * (2 * nm), out_specs=[ANY] * (2 * nm),
        scratch_shapes=[sem(nm, 3), sem(nm, 3), sem(N_CHIP + 2)],
        compiler_params=_params(),
    )(*grads, *halves)


def _rs3_finish(own, got):
    nm = len(BIG)

    def body(*refs):
        oin, gin = refs[:nm], refs[nm:2 * nm]
        outs = refs[2 * nm:3 * nm]
        send, recv, lsem = refs[3 * nm:]
        x, y, c, _ = _place()
        for m, (_, kind, shape) in enumerate(BIG):
            pr, pc = _piece(kind, shape)

            def scoped(a, g, f, m=m, pr=pr):
                loads = [pltpu.make_async_copy(oin[m], a, lsem.at[0]), pltpu.make_async_copy(gin[m], g, lsem.at[1])]
                for cp in loads:
                    cp.start()
                for cp in loads:
                    cp.wait()
                step = _add_rows(pr)

                def add(i, _):
                    r = pl.ds(pl.multiple_of(i * step, 16), step)
                    f[r, :] = ((a[r, :].astype(F32) + g[0, r, :].astype(F32)) + g[1, r, :].astype(F32)) + g[2, r, :].astype(F32)
                    return 0

                lax.fori_loop(0, pr // step, add, 0)
                dst = outs[m].at[pl.ds(pl.multiple_of(c * pr, 8), pr), :]
                mine = pltpu.make_async_copy(f, dst, lsem.at[2])
                mine.start()
                cp = pltpu.make_async_remote_copy(src_ref=f, dst_ref=dst, send_sem=send.at[m], recv_sem=recv.at[m],
                                                  device_id=(x, y, 1 - c), device_id_type=MESH)
                cp.start()
                cp.wait_send()
                mine.wait()

            pl.run_scoped(scoped, pltpu.VMEM((pr, pc), BF16), pltpu.VMEM((3, pr, pc), BF16), pltpu.VMEM((pr, pc), F32))
        for m, (_, kind, shape) in enumerate(BIG):
            pr, pc = _piece(kind, shape)
            dst = outs[m].at[pl.ds(pl.multiple_of((1 - c) * pr, 8), pr), :]
            pltpu.make_async_remote_copy(src_ref=dst, dst_ref=dst, send_sem=send.at[m], recv_sem=recv.at[m],
                                         device_id=(x, y, 1 - c), device_id_type=MESH).wait_recv()

    sem = lambda *s: pltpu.SemaphoreType.DMA(s)
    pieces = [_piece(kind, shape) for _, kind, shape in BIG]
    return pl.pallas_call(
        body, name="rs3_finish",
        out_shape=[jax.ShapeDtypeStruct((2 * pr, pc), F32) for pr, pc in pieces],
        in_specs=[ANY] * (2 * nm), out_specs=[ANY] * nm,
        scratch_shapes=[sem(nm), sem(nm), sem(3)],
        compiler_params=_params(),
    )(*own, *got)


def _cond_fwd(c_all, w_shard, b_shard):
    def body(c_ref, w_ref, b_ref, act_ref, mod_ref):
        cv = c_ref[...]
        act = cv * _sig(cv)
        act_ref[...] = act
        mod_ref[...] = _dot(act.astype(BF16), w_ref[...].astype(BF16)) + b_ref[...]

    return pl.pallas_call(
        body, name="cond_fwd",
        out_shape=[jax.ShapeDtypeStruct(c_all.shape, F32), jax.ShapeDtypeStruct((c_all.shape[0], w_shard.shape[1]), F32)],
        compiler_params=_params(),
    )(c_all, w_shard, b_shard)


def _cond_bwd(act_t, dmod_shard):
    k, n = act_t.shape[0], dmod_shard.shape[1]

    def body(a_ref, d_ref, o_ref):
        acc = a_ref[:, 0:1] * d_ref[0:1, :]
        for e in range(1, N_DEV):
            acc += a_ref[:, e:e + 1] * d_ref[e:e + 1, :]
        o_ref[...] = acc

    tr = 256
    return pl.pallas_call(
        body, name="cond_bwd", grid=(k // tr,),
        in_specs=[pl.BlockSpec((tr, N_DEV), lambda i: (i, 0)), _full(dmod_shard)],
        out_specs=pl.BlockSpec((tr, n), lambda i: (i, 0)),
        out_shape=jax.ShapeDtypeStruct((k, n), F32),
        compiler_params=_params(("arbitrary",)),
    )(act_t, dmod_shard)


def _sum_blocks(allp):
    def body(a_ref, o_ref):
        acc = a_ref[0:PACK_ROWS, :]
        for d in range(1, N_DEV):
            acc += a_ref[d * PACK_ROWS:(d + 1) * PACK_ROWS, :]
        o_ref[...] = acc

    return pl.pallas_call(
        body, name="sum_small", out_shape=jax.ShapeDtypeStruct((PACK_ROWS, PACK_COLS), F32), compiler_params=_params(),
    )(allp)


def _adamw(name, w, g, m, v):
    r, cc = w.shape
    tr = r
    for cand in (256, 128, 64, 32, 16, 8):
        if r % cand == 0:
            tr = cand
            break
    bc1 = 1.0 - ADAM_B1 ** ADAM_STEP
    bc2 = 1.0 - ADAM_B2 ** ADAM_STEP

    def body(w_ref, g_ref, m_ref, v_ref, d_ref, nm_ref, nv_ref):
        gv = g_ref[...]
        nm = ADAM_B1 * m_ref[...] + (1.0 - ADAM_B1) * gv
        nv = ADAM_B2 * v_ref[...] + (1.0 - ADAM_B2) * (gv * gv)
        nm_ref[...] = nm
        nv_ref[...] = nv
        d_ref[...] = -ADAM_LR * ((nm / bc1) / (jnp.sqrt(nv / bc2) + ADAM_EPS) + ADAM_WD * w_ref[...])

    spec = pl.BlockSpec((tr, cc), lambda i: (i, 0))
    return pl.pallas_call(
        body, name=name, grid=(r // tr,), in_specs=[spec] * 4, out_specs=[spec] * 3,
        out_shape=[jax.ShapeDtypeStruct((r, cc), F32)] * 3, compiler_params=_params(("arbitrary",)),
    )(w, g, m, v)


def _pack(fields, layout):
    parts = [fields[name].reshape(-1).astype(F32) if name in fields else jnp.zeros((n,), F32) for name, n in layout]
    used = sum(n for _, n in layout)
    parts.append(jnp.zeros((PACK_ROWS * PACK_COLS - used,), F32))
    return jnp.concatenate(parts).reshape(PACK_ROWS, PACK_COLS)


def _unpack(flat, layout):
    flat = flat.reshape(-1)
    out, o = {}, 0
    for name, n in layout:
        out[name] = flat[o:o + n]
        o += n
    return out


def kernel(x, c, w_cond, b_cond, w_in, b_in, ssm_lambda_re, ssm_lambda_im, ssm_log_dt, ssm_b_re, ssm_b_im, ssm_c_re, ssm_c_im, ssm_d, ssm_glu_w_a, ssm_glu_w_b, cv_dw_w, cv_dw_b, cv_ln_g, cv_ln_b, cv_w_pw, w_out, ln1_g, ln1_b, ffn_w_up, ffn_dw_w, ffn_dw_b, ffn_w_down, ln2_g, ln2_b, loss_target, m_w_cond, m_b_cond, m_w_in, m_b_in, m_ssm_lambda_re, m_ssm_lambda_im, m_ssm_log_dt, m_ssm_b_re, m_ssm_b_im, m_ssm_c_re, m_ssm_c_im, m_ssm_d, m_ssm_glu_w_a, m_ssm_glu_w_b, m_cv_dw_w, m_cv_dw_b, m_cv_ln_g, m_cv_ln_b, m_cv_w_pw, m_w_out, m_ln1_g, m_ln1_b, m_ffn_w_up, m_ffn_dw_w, m_ffn_dw_b, m_ffn_w_down, m_ln2_g, m_ln2_b, v_w_cond, v_b_cond, v_w_in, v_b_in, v_ssm_lambda_re, v_ssm_lambda_im, v_ssm_log_dt, v_ssm_b_re, v_ssm_b_im, v_ssm_c_re, v_ssm_c_im, v_ssm_d, v_ssm_glu_w_a, v_ssm_glu_w_b, v_cv_dw_w, v_cv_dw_b, v_cv_ln_g, v_cv_ln_b, v_cv_w_pw, v_w_out, v_ln1_g, v_ln1_b, v_ffn_w_up, v_ffn_dw_w, v_ffn_dw_b, v_ffn_w_down, v_ln2_g, v_ln2_b):
    given = locals()
    a = {n: given[n] for n in INPUTS}
    xi, yi, ci = lax.axis_index("x"), lax.axis_index("y"), lax.axis_index("c")
    s_me = 2 * xi + yi
    e_me = 4 * xi + 2 * yi + ci

    first = jnp.concatenate([
        jnp.concatenate([a["c"], jnp.zeros((7, D_MODEL), F32)], axis=0),
        jnp.concatenate([a["cv_dw_w"].reshape(-1), a["ffn_dw_w"].reshape(-1)]).reshape(8, D_MODEL)], axis=0)
    first_all = _allgather("gather_c", first).reshape(N_DEV, 16, D_MODEL)
    c_all = first_all[:, 0, :]
    dw_all = first_all[0::2, 8:, :].reshape(N_CHIP, 8 * D_MODEL)
    n_cv = CONV_KERNEL * CONV_WIDTH // N_CHIP
    cv_dw_full = dw_all[:, :n_cv].reshape(N_CHIP, CONV_KERNEL, CONV_WIDTH // N_CHIP).transpose(1, 0, 2) \
        .reshape(CONV_KERNEL, CONV_WIDTH)
    ffn_dw_full = dw_all[:, n_cv:].reshape(N_CHIP, FFN_KERNEL, 2 * FFN_HIDDEN // N_CHIP).transpose(1, 0, 2) \
        .reshape(FFN_KERNEL, 2 * FFN_HIDDEN)
    ncols = N_COND * D_MODEL // N_CHIP
    b_cond_shard = lax.dynamic_slice(a["b_cond"], (0, s_me * ncols), (1, ncols))
    c_act_all, modp = _cond_fwd(c_all, a["w_cond"][0], b_cond_shard)
    modp_all = _allgather("gather_mod", modp).reshape(N_DEV, N_DEV, ncols)[0::2]
    mod_e = lax.dynamic_index_in_dim(modp_all, e_me, axis=1, keepdims=False).reshape(N_COND, D_MODEL)
    modv = jnp.concatenate([mod_e, jnp.zeros((2, D_MODEL), F32)], axis=0)

    wb = dict(zip([n for n, _, _ in BIG], _gather_weights([a[n][0] for n, _, _ in BIG])))
    sp = {n: a[n][0] for n in ("b_in", "ssm_lambda_re", "ssm_lambda_im", "ssm_log_dt", "ssm_b_re", "ssm_b_im",
                               "ssm_c_re", "ssm_c_im", "ssm_d", "cv_dw_b", "cv_ln_g", "cv_ln_b", "ln1_g", "ln1_b",
                               "ffn_dw_b", "ln2_g", "ln2_b")}
    sp["cv_dw_w"] = cv_dw_full
    sp["ffn_dw_w"] = ffn_dw_full
    gx, dbig, small = _local_step(a["x"][0], a["loss_target"][0], modv, wb, sp)

    small["c_act"] = lax.dynamic_index_in_dim(c_act_all, e_me, axis=0, keepdims=False)
    packed_all = _allgather("gather_small", _pack(small, PACK))
    tot = _unpack(_sum_blocks(packed_all), PACK)
    rows = packed_all.reshape(N_DEV, PACK_ROWS * PACK_COLS)
    dmod_all = rows[:, 0:N_COND * D_MODEL]
    act_all = rows[:, N_COND * D_MODEL:(N_COND + 1) * D_MODEL]
    g_w_cond = _cond_bwd(act_all.T, lax.dynamic_slice(dmod_all, (0, s_me * ncols), (N_DEV, ncols)))

    glist = [dbig[n] for n, _, _ in BIG]
    halves = _rs1_sibling(glist)
    r2 = _rs2_chips(glist, halves)
    gsh = _rs3_finish(r2[:len(BIG)], r2[len(BIG):])

    grads = {"w_cond": g_w_cond[None], "b_cond": tot["dmod"].reshape(1, -1)}
    for (n, kind, shape), g in zip(BIG, gsh):
        grads[n] = g.reshape(a[n].shape)
    for n in ("b_in", "ssm_lambda_re", "ssm_lambda_im", "ssm_log_dt", "ssm_b_re", "ssm_b_im", "ssm_c_re", "ssm_c_im",
              "ssm_d", "cv_dw_b", "cv_ln_g", "cv_ln_b", "ln1_g", "ln1_b", "ffn_dw_b", "ln2_g", "ln2_b"):
        grads[n] = tot[n].reshape(a[n].shape)
    wcv = CONV_WIDTH // N_CHIP
    grads["cv_dw_w"] = lax.dynamic_slice(tot["cv_dw_w"].reshape(CONV_KERNEL, CONV_WIDTH), (0, s_me * wcv),
                                         (CONV_KERNEL, wcv)).reshape(a["cv_dw_w"].shape)
    wff = 2 * FFN_HIDDEN // N_CHIP
    grads["ffn_dw_w"] = lax.dynamic_slice(tot["ffn_dw_w"].reshape(FFN_KERNEL, 2 * FFN_HIDDEN), (0, s_me * wff),
                                          (FFN_KERNEL, wff)).reshape(a["ffn_dw_w"].shape)

    delta, new_m, new_v = {}, {}, {}
    for n in ["w_cond"] + [n for n, _, _ in BIG]:
        d, nm_, nv_ = _adamw("adamw_" + n, a[n][0], grads[n][0], a["m_" + n][0], a["v_" + n][0])
        delta[n], new_m[n], new_v[n] = d[None], nm_[None], nv_[None]
    upd = [n for n, _ in SMALL_UPD]
    d, nm_, nv_ = _adamw("adamw_small", _pack({n: a[n] for n in upd}, SMALL_UPD), _pack({n: grads[n] for n in upd}, SMALL_UPD),
                         _pack({n: a["m_" + n] for n in upd}, SMALL_UPD), _pack({n: a["v_" + n] for n in upd}, SMALL_UPD))
    for dst, flat in ((delta, d), (new_m, nm_), (new_v, nv_)):
        for n, val in _unpack(flat, SMALL_UPD).items():
            dst[n] = val.reshape(a[n].shape)

    loss = tot["loss"].reshape(())
    return (loss, gx[None], *[grads[n] for n in WEIGHTS], *[delta[n] for n in WEIGHTS],
            *[new_m[n] for n in WEIGHTS], *[new_v[n] for n in WEIGHTS])
```

```python
import functools
import math

import jax
import jax.numpy as jnp
from jax import lax
from jax.experimental import pallas as pl
from jax.experimental.pallas import tpu as pltpu

F32 = jnp.float32
BF16 = jnp.bfloat16

D_MODEL = 1024
SSM_WIDTH = 512
SSM_GROUP = 16
SSM_GROUPS = 32
SSM_STATE = 64
CONV_WIDTH = 512
CONV_KERNEL = 31
FFN_HIDDEN = 2816
FFN_KERNEL = 3
IN_PROJ_WIDTH = 3584
N_COND = 6
ALPHA = 2.0 ** 0.25
LN_EPS = 1e-5
ADAM_LR, ADAM_B1, ADAM_B2, ADAM_EPS, ADAM_WD, ADAM_STEP = 0.001, 0.9, 0.999, 1e-08, 0.01, 10

N_DEV = 8
N_CHIP = 4
LANES = 128
SSM_CHUNK = 16
LANE_GROUPS = LANES // SSM_GROUP
N_LANE_BLOCKS = SSM_WIDTH // LANES
STATE_COLS = LANE_GROUPS * SSM_STATE
CHUNK_COLS = SSM_CHUNK * LANES
CONV_HALO = 32
VMEM_LIMIT = 56 * 1024 * 1024
MESH = pl.DeviceIdType.MESH

BIG = (
    ("w_in", "col", (D_MODEL, IN_PROJ_WIDTH)),
    ("ssm_glu_w_a", "col", (SSM_WIDTH, D_MODEL)),
    ("ssm_glu_w_b", "col", (SSM_WIDTH, D_MODEL)),
    ("cv_w_pw", "col", (CONV_WIDTH, D_MODEL)),
    ("w_out", "row", (D_MODEL, D_MODEL)),
    ("ffn_w_up", "col", (D_MODEL, 2 * FFN_HIDDEN)),
    ("ffn_w_down", "row", (FFN_HIDDEN, D_MODEL)),
)

WEIGHTS = ['w_cond', 'b_cond', 'w_in', 'b_in', 'ssm_lambda_re', 'ssm_lambda_im', 'ssm_log_dt', 'ssm_b_re', 'ssm_b_im',
           'ssm_c_re', 'ssm_c_im', 'ssm_d', 'ssm_glu_w_a', 'ssm_glu_w_b', 'cv_dw_w', 'cv_dw_b', 'cv_ln_g', 'cv_ln_b',
           'cv_w_pw', 'w_out', 'ln1_g', 'ln1_b', 'ffn_w_up', 'ffn_dw_w', 'ffn_dw_b', 'ffn_w_down', 'ln2_g', 'ln2_b']
INPUTS = ['x', 'c'] + WEIGHTS + ['loss_target'] + ['m_' + n for n in WEIGHTS] + ['v_' + n for n in WEIGHTS]

PACK = (
    ("dmod", N_COND * D_MODEL), ("c_act", D_MODEL), ("b_in", IN_PROJ_WIDTH),
    ("ssm_lambda_re", SSM_GROUPS * SSM_STATE), ("ssm_lambda_im", SSM_GROUPS * SSM_STATE), ("ssm_log_dt", SSM_GROUPS),
    ("ssm_b_re", SSM_GROUPS * SSM_STATE * SSM_GROUP), ("ssm_b_im", SSM_GROUPS * SSM_STATE * SSM_GROUP),
    ("ssm_c_re", SSM_GROUPS * SSM_STATE * SSM_GROUP), ("ssm_c_im", SSM_GROUPS * SSM_STATE * SSM_GROUP),
    ("ssm_d", SSM_GROUPS * SSM_GROUP), ("cv_dw_w", CONV_KERNEL * CONV_WIDTH), ("cv_dw_b", CONV_WIDTH),
    ("cv_ln_g", CONV_WIDTH), ("cv_ln_b", CONV_WIDTH), ("ln1_g", D_MODEL), ("ln1_b", D_MODEL),
    ("ffn_dw_w", FFN_KERNEL * 2 * FFN_HIDDEN), ("ffn_dw_b", 2 * FFN_HIDDEN), ("ln2_g", D_MODEL), ("ln2_b", D_MODEL),
    ("loss", 1),
)
PACK_COLS = 1024
PACK_ROWS = 192
assert sum(n for _, n in PACK) <= PACK_ROWS * PACK_COLS

SMALL_UPD = (
    ("b_cond", N_COND * D_MODEL), ("b_in", IN_PROJ_WIDTH),
    ("ssm_lambda_re", SSM_GROUPS * SSM_STATE), ("ssm_lambda_im", SSM_GROUPS * SSM_STATE), ("ssm_log_dt", SSM_GROUPS),
    ("ssm_b_re", SSM_GROUPS * SSM_STATE * SSM_GROUP), ("ssm_b_im", SSM_GROUPS * SSM_STATE * SSM_GROUP),
    ("ssm_c_re", SSM_GROUPS * SSM_STATE * SSM_GROUP), ("ssm_c_im", SSM_GROUPS * SSM_STATE * SSM_GROUP),
    ("ssm_d", SSM_GROUPS * SSM_GROUP), ("cv_dw_w", CONV_KERNEL * CONV_WIDTH // N_CHIP), ("cv_dw_b", CONV_WIDTH),
    ("cv_ln_g", CONV_WIDTH), ("cv_ln_b", CONV_WIDTH), ("ln1_g", D_MODEL), ("ln1_b", D_MODEL),
    ("ffn_dw_w", FFN_KERNEL * 2 * FFN_HIDDEN // N_CHIP), ("ffn_dw_b", 2 * FFN_HIDDEN), ("ln2_g", D_MODEL),
    ("ln2_b", D_MODEL),
)
assert sum(n for _, n in SMALL_UPD) <= PACK_ROWS * PACK_COLS


def _params(sem=None, **kw):
    return pltpu.CompilerParams(dimension_semantics=sem, vmem_limit_bytes=VMEM_LIMIT, **kw)


def _ln_stats(x):
    mu = jnp.mean(x, axis=-1, keepdims=True)
    xc = x - mu
    var = jnp.mean(xc * xc, axis=-1, keepdims=True)
    rstd = lax.rsqrt(var + LN_EPS)
    return xc * rstd, rstd


def _ln_bwd(dxhat, xhat, rstd):
    m1 = jnp.mean(dxhat, axis=-1, keepdims=True)
    m2 = jnp.mean(dxhat * xhat, axis=-1, keepdims=True)
    return rstd * (dxhat - m1 - xhat * m2)


def _sig(x):
    return 1.0 / (1.0 + jnp.exp(-x))


def _gelu(x):
    return 0.5 * x * (1.0 + lax.erf(x * (1.0 / math.sqrt(2.0))))


def _dgelu(x):
    return 0.5 * (1.0 + lax.erf(x * (1.0 / math.sqrt(2.0)))) + x * jnp.exp(-0.5 * x * x) * (1.0 / math.sqrt(2.0 * math.pi))


def _gelu_and_grad(x):
    er = lax.erf(x * (1.0 / math.sqrt(2.0)))
    cdf = 0.5 * (1.0 + er)
    return x * cdf, cdf + x * jnp.exp(-0.5 * x * x) * (1.0 / math.sqrt(2.0 * math.pi))


def _colsum(a):
    return jnp.sum(a, axis=0, keepdims=True)


def _fill_rotations(buf, rot, rows):
    for r in range(1, 8):
        rot[r - 1] = buf[pl.ds(r, rows), :]


def _rows_at(buf, rot, offset, tb):
    q, r = divmod(offset, 8)
    if r == 0:
        return buf[pl.ds(8 * q, tb), :]
    return rot[r - 1, pl.ds(8 * q, tb), :]


def _dot(a, b):
    return jnp.dot(a, b, preferred_element_type=F32)


def _dot_nt(a, b):
    return lax.dot_general(a, b, (((1,), (1,)), ((), ())), preferred_element_type=F32)


def _dot_tn(a, b):
    return lax.dot_general(a, b, (((0,), (0,)), ((), ())), preferred_element_type=F32)


def _load_once(src, dst, sem):
    cp = pltpu.make_async_copy(src, dst, sem)
    cp.start()
    cp.wait()


def _full(a):
    nd = a.ndim
    return pl.BlockSpec(a.shape, lambda *_: (0,) * nd)


ANY = pl.BlockSpec(memory_space=pl.ANY)


def _f1_inproj(x, modv, b_in, w_in, tb):
    t = x.shape[0]
    chunks = [(j * 512, 512) for j in range(IN_PROJ_WIDTH // 512)]

    def body(x_ref, modv_ref, b_ref, w_hbm, u4_ref, prest_ref, h_ref, w_v, sem):
        @pl.when(pl.program_id(0) == 0)
        def _():
            _load_once(w_hbm, w_v, sem)

        xn, _ = _ln_stats(x_ref[...])
        h = (xn * (1.0 + modv_ref[1:2, :]) + modv_ref[0:1, :]).astype(BF16)
        h_ref[...] = h
        for c0, cw in chunks:
            p = _dot(h, w_v[:, c0:c0 + cw]) + b_ref[:, c0:c0 + cw]
            if c0 == 0:
                for b in range(N_LANE_BLOCKS):
                    u4_ref[b] = p[:, b * LANES:(b + 1) * LANES].astype(BF16)
            else:
                prest_ref[:, c0 - SSM_WIDTH:c0 - SSM_WIDTH + cw] = p

    return pl.pallas_call(
        body, name="f1_inproj", grid=(t // tb,),
        in_specs=[pl.BlockSpec((tb, D_MODEL), lambda i: (i, 0)), _full(modv), _full(b_in), ANY],
        out_specs=[pl.BlockSpec((N_LANE_BLOCKS, tb, LANES), lambda i: (0, i, 0)),
                   pl.BlockSpec((tb, IN_PROJ_WIDTH - SSM_WIDTH), lambda i: (i, 0)),
                   pl.BlockSpec((tb, D_MODEL), lambda i: (i, 0))],
        out_shape=[jax.ShapeDtypeStruct((N_LANE_BLOCKS, t, LANES), BF16),
                   jax.ShapeDtypeStruct((t, IN_PROJ_WIDTH - SSM_WIDTH), F32),
                   jax.ShapeDtypeStruct((t, D_MODEL), BF16)],
        scratch_shapes=[pltpu.VMEM(w_in.shape, BF16), pltpu.SemaphoreType.DMA],
        compiler_params=_params(("arbitrary",)),
    )(x, modv, b_in, w_in)


def _s5_build(lam_re, lam_im, log_dt, b_re, b_im, c_re, c_im, d):
    hi = lax.Precision.HIGHEST
    el, g, n, p, nb = SSM_CHUNK, SSM_GROUPS, SSM_STATE, SSM_GROUP, N_LANE_BLOCKS
    lr = jnp.minimum(lam_re, -1e-4)
    li = lam_im
    dt = jnp.exp(log_dt)[:, None]
    mag = jnp.exp(lr * dt)
    ang = li * dt
    lbr, lbi = mag * jnp.cos(ang), mag * jnp.sin(ang)
    num_r, num_i = lbr - 1.0, lbi
    den = lr * lr + li * li
    coef_r = (num_r * lr + num_i * li) / den
    coef_i = (num_i * lr - num_r * li) / den
    bbar_r = coef_r[..., None] * b_re - coef_i[..., None] * b_im
    bbar_i = coef_r[..., None] * b_im + coef_i[..., None] * b_re
    k = jnp.arange(el + 1, dtype=F32)[:, None, None]
    pmag = jnp.exp(k * (lr * dt)[None])
    pr, pi = pmag * jnp.cos(k * ang[None]), pmag * jnp.sin(k * ang[None])
    car = c_re[None] * pr[:, :, None, :] - c_im[None] * pi[:, :, None, :]
    cai = c_re[None] * pi[:, :, None, :] + c_im[None] * pr[:, :, None, :]
    kern = (jnp.einsum("kgpn,gnq->kgqp", car[:el], bbar_r, precision=hi)
            - jnp.einsum("kgpn,gnq->kgqp", cai[:el], bbar_i, precision=hi))
    kern = kern.at[0].add(jnp.eye(p, dtype=F32)[None] * d[:, None, :])
    kc = kern.reshape(el, g * p, p)
    bt_r = bbar_r.transpose(0, 2, 1)[None]
    bt_i = bbar_i.transpose(0, 2, 1)[None]
    rev = el - 1 - jnp.arange(el)
    qr, qi = pr[rev][:, :, None, :], pi[rev][:, :, None, :]
    sw_r = (qr * bt_r - qi * bt_i).reshape(el, g * p, n)
    sw_i = (qr * bt_i + qi * bt_r).reshape(el, g * p, n)
    sg_r = car[1:].reshape(el, g * p, n)
    sg_i = (-cai[1:]).reshape(el, g * p, n)
    a = jnp.stack([pr[el].reshape(nb, LANE_GROUPS * n), pi[el].reshape(nb, LANE_GROUPS * n)], axis=1)
    return kc, sw_r, sw_i, sg_r, sg_i, a


def _expand(src, reps):
    rows, w = src.shape
    cols = reps * w
    r = lax.broadcasted_iota(jnp.int32, (w, cols), 0)
    c = lax.broadcasted_iota(jnp.int32, (w, cols), 1)
    rep = (r == (c & (w - 1))).astype(BF16)
    out = _dot(src.astype(BF16), rep)
    rg = lax.broadcasted_iota(jnp.int32, (rows, cols), 0) // SSM_GROUP
    cg = lax.broadcasted_iota(jnp.int32, (rows, cols), 1) // w
    return jnp.where(rg == cg, out, 0.0).astype(BF16)


def _fold(x, w):
    rows, cols = x.shape
    rg = lax.broadcasted_iota(jnp.int32, (rows, cols), 0) // SSM_GROUP
    cg = lax.broadcasted_iota(jnp.int32, (rows, cols), 1) // w
    x = jnp.where(rg == cg, x, 0.0)
    while cols > LANES:
        x = x[:, :cols // 2] + x[:, cols // 2:]
        cols //= 2
    s = LANES // 2
    while s >= w:
        x = x + pltpu.roll(x, s, axis=1)
        s //= 2
    return x[:, :w]


def _build_maps(s_ref, dst):
    for j in range(SSM_CHUNK):
        dst[j * LANES:(j + 1) * LANES, :] = _expand(s_ref[j], LANE_GROUPS)


def _build_toeplitz(kc_ref, dst):
    dst[...] = jnp.zeros_like(dst)
    for d in range(SSM_CHUNK):
        blk = _expand(kc_ref[d], LANE_GROUPS)
        for ji in range(SSM_CHUNK - d):
            jo = ji + d
            dst[ji * LANES:(ji + 1) * LANES, jo * LANES:(jo + 1) * LANES] = blk


def _cblk(w):
    return pl.BlockSpec((SSM_CHUNK, LANES, w), lambda b: (0, b, 0))


def _s5a_state(u2, sw_r, sw_i, a8):
    nb, nc, _ = u2.shape
    sc = STATE_COLS

    def body(u_ref, swr_ref, swi_ref, a_ref, hr_ref, hi_ref, w_s, xr_s, xi_s):
        u = u_ref[0]
        _build_maps(swr_ref, w_s)
        xr_s[...] = _dot(u, w_s[...])
        _build_maps(swi_ref, w_s)
        xi_s[...] = _dot(u, w_s[...])
        ar = a_ref[0, 0:1, :]
        ai = a_ref[0, 1:2, :]

        def step(c, carry):
            hr, hi = carry
            hr_ref[0, pl.ds(c, 1), :] = hr
            hi_ref[0, pl.ds(c, 1), :] = hi
            xr = xr_s[pl.ds(c, 1), :]
            xi = xi_s[pl.ds(c, 1), :]
            return ar * hr - ai * hi + xr, ar * hi + ai * hr + xi

        z = jnp.zeros((1, sc), F32)
        lax.fori_loop(0, nc, step, (z, z))

    return pl.pallas_call(
        body, name="s5a_state", grid=(nb,),
        in_specs=[pl.BlockSpec((1, nc, CHUNK_COLS), lambda b: (b, 0, 0)), _cblk(SSM_STATE), _cblk(SSM_STATE),
                  pl.BlockSpec((1, 8, sc), lambda b: (b, 0, 0))],
        out_specs=[pl.BlockSpec((1, nc, sc), lambda b: (b, 0, 0))] * 2,
        out_shape=[jax.ShapeDtypeStruct((nb, nc, sc), F32)] * 2,
        scratch_shapes=[pltpu.VMEM((CHUNK_COLS, sc), BF16), pltpu.VMEM((nc, sc), F32), pltpu.VMEM((nc, sc), F32)],
        compiler_params=_params(("arbitrary",)),
    )(u2, sw_r, sw_i, a8)


def _s5b_out(u2, kc, sg_r, sg_i, hr, hi):
    nb, nc, _ = u2.shape
    sc = STATE_COLS
    cw = 512

    def body(u_ref, kc_ref, sgr_ref, sgi_ref, hr_ref, hi_ref, y_ref, tm_s, gr_s, gi_s):
        _build_toeplitz(kc_ref, tm_s)
        _build_maps(sgr_ref, gr_s)
        _build_maps(sgi_ref, gi_s)
        u = u_ref[0]
        h_r = hr_ref[0].astype(BF16)
        h_i = hi_ref[0].astype(BF16)
        for j in range(CHUNK_COLS // cw):
            cs = slice(j * cw, (j + 1) * cw)
            y_ref[0, :, cs] = _dot(u, tm_s[:, cs]) + _dot_nt(h_r, gr_s[cs, :]) + _dot_nt(h_i, gi_s[cs, :])

    return pl.pallas_call(
        body, name="s5b_out", grid=(nb,),
        in_specs=[pl.BlockSpec((1, nc, CHUNK_COLS), lambda b: (b, 0, 0)), _cblk(SSM_GROUP), _cblk(SSM_STATE),
                  _cblk(SSM_STATE), pl.BlockSpec((1, nc, sc), lambda b: (b, 0, 0)),
                  pl.BlockSpec((1, nc, sc), lambda b: (b, 0, 0))],
        out_specs=pl.BlockSpec((1, nc, CHUNK_COLS), lambda b: (b, 0, 0)),
        out_shape=jax.ShapeDtypeStruct((nb, nc, CHUNK_COLS), F32),
        scratch_shapes=[pltpu.VMEM((CHUNK_COLS, CHUNK_COLS), BF16), pltpu.VMEM((CHUNK_COLS, sc), BF16),
                        pltpu.VMEM((CHUNK_COLS, sc), BF16)],
        compiler_params=_params(("arbitrary",)),
    )(u2, kc, sg_r, sg_i, hr, hi)


def _f4_mixer(ys4, prest, x, modv, cvv, cw32, w_a, w_b, w_pw, w_out, tb):
    t = x.shape[0]
    hb = tb // CONV_HALO

    def body(ys_ref, pr_ref, halo_ref, x_ref, modv_ref, cvv_ref, cw_ref, wa_ref, wb_ref, wpw_ref, wout_ref,
             r1_ref, ya_ref, yb_ref, ycv_ref, vc_ref, yg_ref, vs_ref, mg_ref, vbuf, vrot):
        i = pl.program_id(0)
        ys = jnp.concatenate([ys_ref[b] for b in range(N_LANE_BLOCKS)], axis=-1)
        yg = _gelu(ys).astype(BF16)
        yg_ref[...] = yg
        ya = _dot(yg, wa_ref[...])
        yb = _dot(yg, wb_ref[...])
        ya_ref[...] = ya.astype(BF16)
        yb_ref[...] = yb.astype(BF16)
        yssm = ya * _sig(yb)
        hv = halo_ref[:, 0:CONV_WIDTH] * _sig(halo_ref[:, CONV_WIDTH:2 * CONV_WIDTH])
        vbuf[0:CONV_HALO, :] = jnp.where(i == 0, 0.0, hv)
        vbuf[CONV_HALO:, :] = pr_ref[:, 0:CONV_WIDTH] * _sig(pr_ref[:, CONV_WIDTH:2 * CONV_WIDTH])
        _fill_rotations(vbuf, vrot, tb + CONV_HALO - 8)
        acc = jnp.zeros((tb, CONV_WIDTH), F32)
        for k in range(CONV_KERNEL):
            acc += _rows_at(vbuf, vrot, CONV_HALO - CONV_KERNEL + 1 + k, tb) * cw_ref[k:k + 1, :]
        vc = acc + cvv_ref[0:1, :]
        vc_ref[...] = vc
        xh, _ = _ln_stats(vc)
        vl = xh * cvv_ref[1:2, :] + cvv_ref[2:3, :]
        vs = (vl * _sig(vl)).astype(BF16)
        vs_ref[...] = vs
        ycv = _dot(vs, wpw_ref[...])
        ycv_ref[...] = ycv.astype(BF16)
        gs = pr_ref[:, 2 * CONV_WIDTH:2 * CONV_WIDTH + D_MODEL]
        gc = pr_ref[:, 2 * CONV_WIDTH + D_MODEL:]
        merged = (_sig(gs) * yssm + _sig(gc) * ycv).astype(BF16)
        mg_ref[...] = merged
        ym = _dot(merged, wout_ref[...])
        r1_ref[...] = ALPHA * x_ref[...] + modv_ref[2:3, :] * ym

    tok = lambda w: pl.BlockSpec((tb, w), lambda i: (i, 0))
    return pl.pallas_call(
        body, name="f4_mixer", grid=(t // tb,),
        in_specs=[pl.BlockSpec((N_LANE_BLOCKS, tb, LANES), lambda i: (0, i, 0)), tok(prest.shape[1]),
                  pl.BlockSpec((CONV_HALO, 2 * CONV_WIDTH), lambda i: (jnp.maximum(i * hb - 1, 0), 0)),
                  tok(D_MODEL), _full(modv), _full(cvv), _full(cw32), _full(w_a), _full(w_b), _full(w_pw), _full(w_out)],
        out_specs=[tok(D_MODEL), tok(D_MODEL), tok(D_MODEL), tok(D_MODEL), tok(CONV_WIDTH), tok(SSM_WIDTH),
                   tok(CONV_WIDTH), tok(D_MODEL)],
        out_shape=[jax.ShapeDtypeStruct((t, D_MODEL), F32), jax.ShapeDtypeStruct((t, D_MODEL), BF16),
                   jax.ShapeDtypeStruct((t, D_MODEL), BF16), jax.ShapeDtypeStruct((t, D_MODEL), BF16),
                   jax.ShapeDtypeStruct((t, CONV_WIDTH), F32), jax.ShapeDtypeStruct((t, SSM_WIDTH), BF16),
                   jax.ShapeDtypeStruct((t, CONV_WIDTH), BF16), jax.ShapeDtypeStruct((t, D_MODEL), BF16)],
        scratch_shapes=[pltpu.VMEM((tb + CONV_HALO, CONV_WIDTH), F32),
                        pltpu.VMEM((7, tb + CONV_HALO - 8, CONV_WIDTH), F32)],
        compiler_params=_params(("arbitrary",)),
    )(ys4, prest, prest, x, modv, cvv, cw32, w_a, w_b, w_pw, w_out)


FFN_COLS = 1408


def _f5_ffn(r1, tgt, modv, lnv, fdw, w_up, w_down, tb):
    t = r1.shape[0]
    fw = 2 * FFN_HIDDEN

    def body(r1_ref, tgt_ref, modv_ref, lnv_ref, fdw_ref, wup_hbm, wdn_hbm,
             dr2_ref, up_ref, h2_ref, z_ref, acc_ref, wup_v, wdn_v, upbuf, sems):
        i = pl.program_id(0)

        @pl.when(i == 0)
        def _():
            _load_once(wup_hbm, wup_v, sems.at[0])
            _load_once(wdn_hbm, wdn_v, sems.at[1])
            acc_ref[...] = jnp.zeros_like(acc_ref)
            upbuf[0:8, :] = jnp.zeros((8, fw), F32)

        xh1, _ = _ln_stats(r1_ref[...])
        x1 = xh1 * lnv_ref[0:1, :] + lnv_ref[1:2, :]
        xn2, _ = _ln_stats(x1)
        h2 = (xn2 * (1.0 + modv_ref[4:5, :]) + modv_ref[3:4, :]).astype(BF16)
        h2_ref[...] = h2
        for j in range(fw // FFN_COLS):
            cs = slice(j * FFN_COLS, (j + 1) * FFN_COLS)
            up = _dot(h2, wup_v[:, cs])
            upbuf[8:, cs] = up
            up_ref[:, cs] = up.astype(BF16)

        def conv(cs):
            return (fdw_ref[0:1, cs] * upbuf[pl.ds(6, tb), cs] + fdw_ref[1:2, cs] * upbuf[pl.ds(7, tb), cs]
                    + fdw_ref[2:3, cs] * upbuf[pl.ds(8, tb), cs] + fdw_ref[3:4, cs])

        yf = jnp.zeros((tb, D_MODEL), F32)
        for j in range(FFN_HIDDEN // FFN_COLS):
            ca = slice(j * FFN_COLS, (j + 1) * FFN_COLS)
            cv = slice(FFN_HIDDEN + j * FFN_COLS, FFN_HIDDEN + (j + 1) * FFN_COLS)
            z = (_gelu(conv(ca)) * conv(cv)).astype(BF16)
            z_ref[:, ca] = z
            yf += _dot(z, wdn_v[ca, :])
        upbuf[0:8, :] = upbuf[pl.ds(tb, 8), :]
        r2 = ALPHA * x1 + modv_ref[5:6, :] * yf
        xh2, rstd2 = _ln_stats(r2)
        e = xh2 * lnv_ref[2:3, :] + lnv_ref[3:4, :] - tgt_ref[...]
        dx2 = e * (1.0 / D_MODEL)
        acc_ref[3:4, :] += _colsum(e * e) * (0.5 / D_MODEL)
        acc_ref[0:1, :] += _colsum(dx2 * xh2)
        acc_ref[1:2, :] += _colsum(dx2)
        dr2 = _ln_bwd(dx2 * lnv_ref[2:3, :], xh2, rstd2)
        dr2_ref[...] = dr2
        acc_ref[2:3, :] += _colsum(dr2 * yf)

    tok = lambda w: pl.BlockSpec((tb, w), lambda i: (i, 0))
    return pl.pallas_call(
        body, name="f5_ffn", grid=(t // tb,),
        in_specs=[tok(D_MODEL), tok(D_MODEL), _full(modv), _full(lnv), _full(fdw), ANY, ANY],
        out_specs=[tok(D_MODEL), tok(fw), tok(D_MODEL), tok(FFN_HIDDEN), pl.BlockSpec((8, D_MODEL), lambda i: (0, 0))],
        out_shape=[jax.ShapeDtypeStruct((t, D_MODEL), F32), jax.ShapeDtypeStruct((t, fw), BF16),
                   jax.ShapeDtypeStruct((t, D_MODEL), BF16), jax.ShapeDtypeStruct((t, FFN_HIDDEN), BF16),
                   jax.ShapeDtypeStruct((8, D_MODEL), F32)],
        scratch_shapes=[pltpu.VMEM(w_up.shape, BF16), pltpu.VMEM(w_down.shape, BF16),
                        pltpu.VMEM((tb + 8, fw), F32), pltpu.SemaphoreType.DMA((2,))],
        compiler_params=_params(("arbitrary",)),
    )(r1, tgt, modv, lnv, fdw, w_up, w_down)


def _b1a_ffn_down(dr2, up, modv, fdw, w_down, tb):
    t = dr2.shape[0]
    fw = 2 * FFN_HIDDEN
    nt = t // tb
    hb = tb // 16

    def body(dr2_ref, up_ref, halo_ref, modv_ref, fdw_ref, wdn_hbm, dup_ref, dyf_ref, acc_ref, wdn_v, upbuf, dbuf, sem):
        i = pl.program_id(0)
        ti = nt - 1 - i

        @pl.when(i == 0)
        def _():
            _load_once(wdn_hbm, wdn_v, sem)
            acc_ref[...] = jnp.zeros_like(acc_ref)
            dbuf[pl.ds(tb, 8), :] = jnp.zeros((8, fw), F32)

        dyf = (modv_ref[5:6, :] * dr2_ref[...]).astype(BF16)
        dyf_ref[...] = dyf
        upbuf[0:16, :] = jnp.where(ti == 0, 0.0, halo_ref[...].astype(F32))
        upbuf[16:, :] = up_ref[...].astype(F32)

        def conv(cs):
            return (fdw_ref[0:1, cs] * upbuf[pl.ds(14, tb), cs] + fdw_ref[1:2, cs] * upbuf[pl.ds(15, tb), cs]
                    + fdw_ref[2:3, cs] * upbuf[pl.ds(16, tb), cs] + fdw_ref[3:4, cs])

        for j in range(FFN_HIDDEN // FFN_COLS):
            ca = slice(j * FFN_COLS, (j + 1) * FFN_COLS)
            cv = slice(FFN_HIDDEN + j * FFN_COLS, FFN_HIDDEN + (j + 1) * FFN_COLS)
            a = conv(ca)
            v = conv(cv)
            dz = _dot_nt(dyf, wdn_v[ca, :])
            ga, dga = _gelu_and_grad(a)
            dbuf[0:tb, ca] = dz * v * dga
            dbuf[0:tb, cv] = dz * ga
        for j in range(fw // FFN_COLS):
            cs = slice(j * FFN_COLS, (j + 1) * FFN_COLS)
            d0 = dbuf[pl.ds(0, tb), cs]
            dup = fdw_ref[2:3, cs] * d0 + fdw_ref[1:2, cs] * dbuf[pl.ds(1, tb), cs] + fdw_ref[0:1, cs] * dbuf[pl.ds(2, tb), cs]
            dup_ref[:, cs] = dup.astype(BF16)
            for k in range(FFN_KERNEL):
                acc_ref[k:k + 1, cs] += _colsum(d0 * upbuf[pl.ds(14 + k, tb), cs])
            acc_ref[3:4, cs] += _colsum(d0)
        dbuf[pl.ds(tb, 8), :] = dbuf[0:8, :]

    rtok = lambda w: pl.BlockSpec((tb, w), lambda i: (nt - 1 - i, 0))
    return pl.pallas_call(
        body, name="b1a_ffn_down", grid=(nt,),
        in_specs=[rtok(D_MODEL), rtok(fw),
                  pl.BlockSpec((16, fw), lambda i: (jnp.maximum((nt - 1 - i) * hb - 1, 0), 0)),
                  _full(modv), _full(fdw), ANY],
        out_specs=[rtok(fw), rtok(D_MODEL), pl.BlockSpec((8, fw), lambda i: (0, 0))],
        out_shape=[jax.ShapeDtypeStruct((t, fw), BF16), jax.ShapeDtypeStruct((t, D_MODEL), BF16),
                   jax.ShapeDtypeStruct((8, fw), F32)],
        scratch_shapes=[pltpu.VMEM(w_down.shape, BF16), pltpu.VMEM((tb + 16, fw), F32),
                        pltpu.VMEM((tb + 8, fw), F32), pltpu.SemaphoreType.DMA],
        compiler_params=_params(("arbitrary",)),
    )(dr2, up, up, modv, fdw, w_down)


def _b1b_ffn_up(dup, dr2, r1, modv, lnv, w_up, tb):
    t = dr2.shape[0]
    fw = 2 * FFN_HIDDEN

    def body(dup_ref, dr2_ref, r1_ref, modv_ref, lnv_ref, wup_hbm, dr1_ref, acc_ref, wup_v, sem):
        @pl.when(pl.program_id(0) == 0)
        def _():
            _load_once(wup_hbm, wup_v, sem)
            acc_ref[...] = jnp.zeros_like(acc_ref)

        xh1, rstd1 = _ln_stats(r1_ref[...])
        x1 = xh1 * lnv_ref[0:1, :] + lnv_ref[1:2, :]
        xn2, rstd2 = _ln_stats(x1)
        dh2 = _dot_nt(dup_ref[...], wup_v[...])
        acc_ref[0:1, :] += _colsum(dh2 * xn2)
        acc_ref[1:2, :] += _colsum(dh2)
        dx1 = _ln_bwd(dh2 * (1.0 + modv_ref[4:5, :]), xn2, rstd2) + ALPHA * dr2_ref[...]
        acc_ref[2:3, :] += _colsum(dx1 * xh1)
        acc_ref[3:4, :] += _colsum(dx1)
        dr1_ref[...] = _ln_bwd(dx1 * lnv_ref[0:1, :], xh1, rstd1)

    tok = lambda w: pl.BlockSpec((tb, w), lambda i: (i, 0))
    return pl.pallas_call(
        body, name="b1b_ffn_up", grid=(t // tb,),
        in_specs=[tok(fw), tok(D_MODEL), tok(D_MODEL), _full(modv), _full(lnv), ANY],
        out_specs=[tok(D_MODEL), pl.BlockSpec((8, D_MODEL), lambda i: (0, 0))],
        out_shape=[jax.ShapeDtypeStruct((t, D_MODEL), F32), jax.ShapeDtypeStruct((8, D_MODEL), F32)],
        scratch_shapes=[pltpu.VMEM(w_up.shape, BF16), pltpu.SemaphoreType.DMA],
        compiler_params=_params(("arbitrary",)),
    )(dup, dr2, r1, modv, lnv, w_up)


def _b2_mixer(dr1, ys4, prest, ya, yb, ycv, vc, merged, modv, cvv, cw32, w_a, w_b, w_pw, w_out, tb):
    t = dr1.shape[0]
    nt = t // tb
    hb = tb // CONV_HALO
    cwd = CONV_WIDTH

    def body(dr1_ref, ys_ref, pr_ref, halo_ref, ya_ref, yb_ref, ycv_ref, vc_ref, mg_ref, modv_ref, cvv_ref, cw_ref,
             wa_ref, wb_ref, wpw_ref, wout_ref,
             dys_ref, dpr_ref, dya_ref, dyb_ref, dycv_ref, dym_ref, acc_a, acc_b, acc_w, vbuf, dvbuf, vrot, dvrot):
        i = pl.program_id(0)
        ti = nt - 1 - i

        @pl.when(i == 0)
        def _():
            acc_a[...] = jnp.zeros_like(acc_a)
            acc_b[...] = jnp.zeros_like(acc_b)
            acc_w[...] = jnp.zeros_like(acc_w)
            dvbuf[pl.ds(tb, CONV_HALO), :] = jnp.zeros((CONV_HALO, cwd), F32)

        dr1 = dr1_ref[...]
        dym = (modv_ref[2:3, :] * dr1).astype(BF16)
        dym_ref[...] = dym
        ym = _dot(mg_ref[...], wout_ref[...])
        acc_a[0:1, :] += _colsum(dr1 * ym)
        dmg = _dot_nt(dym, wout_ref[...])
        sgs = _sig(pr_ref[:, 2 * cwd:2 * cwd + D_MODEL])
        sgc = _sig(pr_ref[:, 2 * cwd + D_MODEL:])
        ya_v = ya_ref[...].astype(F32)
        syb = _sig(yb_ref[...].astype(F32))
        ycv_v = ycv_ref[...].astype(F32)
        dpr_ref[:, 2 * cwd:2 * cwd + D_MODEL] = (dmg * (ya_v * syb) * sgs * (1.0 - sgs)).astype(BF16)
        dpr_ref[:, 2 * cwd + D_MODEL:] = (dmg * ycv_v * sgc * (1.0 - sgc)).astype(BF16)
        dyssm = dmg * sgs
        dya = (dyssm * syb).astype(BF16)
        dyb = (dyssm * ya_v * syb * (1.0 - syb)).astype(BF16)
        dya_ref[...] = dya
        dyb_ref[...] = dyb
        dyg = _dot_nt(dya, wa_ref[...]) + _dot_nt(dyb, wb_ref[...])
        ys = jnp.concatenate([ys_ref[b] for b in range(N_LANE_BLOCKS)], axis=-1)
        dys = dyg * _dgelu(ys)
        for b in range(N_LANE_BLOCKS):
            dys_ref[b] = dys[:, b * LANES:(b + 1) * LANES].astype(BF16)
        dycv = (dmg * sgc).astype(BF16)
        dycv_ref[...] = dycv
        dvs = _dot_nt(dycv, wpw_ref[...])
        xh, rstd = _ln_stats(vc_ref[...])
        vl = xh * cvv_ref[1:2, :] + cvv_ref[2:3, :]
        s = _sig(vl)
        dvl = dvs * s * (1.0 + vl * (1.0 - s))
        acc_b[1:2, :] += _colsum(dvl * xh)
        acc_b[2:3, :] += _colsum(dvl)
        dvc = _ln_bwd(dvl * cvv_ref[1:2, :], xh, rstd)
        acc_b[0:1, :] += _colsum(dvc)
        hv = halo_ref[:, 0:cwd] * _sig(halo_ref[:, cwd:2 * cwd])
        vbuf[0:CONV_HALO, :] = jnp.where(ti == 0, 0.0, hv)
        cva = pr_ref[:, 0:cwd]
        scg = _sig(pr_ref[:, cwd:2 * cwd])
        vbuf[CONV_HALO:, :] = cva * scg
        dvbuf[0:tb, :] = dvc
        _fill_rotations(vbuf, vrot, tb + CONV_HALO - 8)
        _fill_rotations(dvbuf, dvrot, tb + CONV_HALO - 8)
        dv = jnp.zeros((tb, cwd), F32)
        for k in range(CONV_KERNEL):
            dv += _rows_at(dvbuf, dvrot, CONV_KERNEL - 1 - k, tb) * cw_ref[k:k + 1, :]
            acc_w[k:k + 1, :] += _colsum(dvc * _rows_at(vbuf, vrot, CONV_HALO - CONV_KERNEL + 1 + k, tb))
        dvbuf[pl.ds(tb, CONV_HALO), :] = dvbuf[0:CONV_HALO, :]
        dpr_ref[:, 0:cwd] = (dv * scg).astype(BF16)
        dpr_ref[:, cwd:2 * cwd] = (dv * cva * scg * (1.0 - scg)).astype(BF16)

    rtok = lambda w: pl.BlockSpec((tb, w), lambda i: (nt - 1 - i, 0))
    r4 = pl.BlockSpec((N_LANE_BLOCKS, tb, LANES), lambda i: (0, nt - 1 - i, 0))
    pw = prest.shape[1]
    return pl.pallas_call(
        body, name="b2_mixer", grid=(nt,),
        in_specs=[rtok(D_MODEL), r4, rtok(pw),
                  pl.BlockSpec((CONV_HALO, 2 * cwd), lambda i: (jnp.maximum((nt - 1 - i) * hb - 1, 0), 0)),
                  rtok(D_MODEL), rtok(D_MODEL), rtok(D_MODEL), rtok(cwd), rtok(D_MODEL),
                  _full(modv), _full(cvv), _full(cw32), _full(w_a), _full(w_b), _full(w_pw), _full(w_out)],
        out_specs=[r4, rtok(pw), rtok(D_MODEL), rtok(D_MODEL), rtok(D_MODEL), rtok(D_MODEL),
                   pl.BlockSpec((8, D_MODEL), lambda i: (0, 0)), pl.BlockSpec((8, cwd), lambda i: (0, 0)),
                   pl.BlockSpec((CONV_HALO, cwd), lambda i: (0, 0))],
        out_shape=[jax.ShapeDtypeStruct((N_LANE_BLOCKS, t, LANES), BF16), jax.ShapeDtypeStruct((t, pw), BF16),
                   jax.ShapeDtypeStruct((t, D_MODEL), BF16), jax.ShapeDtypeStruct((t, D_MODEL), BF16),
                   jax.ShapeDtypeStruct((t, D_MODEL), BF16), jax.ShapeDtypeStruct((t, D_MODEL), BF16),
                   jax.ShapeDtypeStruct((8, D_MODEL), F32), jax.ShapeDtypeStruct((8, cwd), F32),
                   jax.ShapeDtypeStruct((CONV_HALO, cwd), F32)],
        scratch_shapes=[pltpu.VMEM((tb + CONV_HALO, cwd), F32), pltpu.VMEM((tb + CONV_HALO, cwd), F32),
                        pltpu.VMEM((7, tb + CONV_HALO - 8, cwd), F32), pltpu.VMEM((7, tb + CONV_HALO - 8, cwd), F32)],
        compiler_params=_params(("arbitrary",)),
    )(dr1, ys4, prest, prest, ya, yb, ycv, vc, merged, modv, cvv, cw32, w_a, w_b, w_pw, w_out)


def _s5c_state_bwd(dy2, sg_r, sg_i, a8, hr, hi):
    nb, nc, _ = dy2.shape
    sc = STATE_COLS

    def body(dy_ref, sgr_ref, sgi_ref, a_ref, hr_ref, hi_ref, dxr_ref, dxi_ref, da_ref, dsgr_ref, dsgi_ref,
             g_s, lr_s, li_s, xr_s, xi_s):
        dy = dy_ref[0]
        _build_maps(sgr_ref, g_s)
        lr_s[...] = _dot(dy, g_s[...])
        _build_maps(sgi_ref, g_s)
        li_s[...] = _dot(dy, g_s[...])
        ar = a_ref[0, 0:1, :]
        ai = a_ref[0, 1:2, :]

        def step(k, carry):
            pr, pi, dar, dai = carry
            c = nc - 1 - k
            xr_s[pl.ds(c, 1), :] = pr
            xi_s[pl.ds(c, 1), :] = pi
            h_r = hr_ref[0, pl.ds(c, 1), :]
            h_i = hi_ref[0, pl.ds(c, 1), :]
            dar = dar + pr * h_r + pi * h_i
            dai = dai - pr * h_i + pi * h_r
            nr = lr_s[pl.ds(c, 1), :] + ar * pr + ai * pi
            ni = li_s[pl.ds(c, 1), :] - ai * pr + ar * pi
            return nr, ni, dar, dai

        z = jnp.zeros((1, sc), F32)
        _, _, dar, dai = lax.fori_loop(0, nc, step, (z, z, z, z))
        da_ref[0] = jnp.concatenate([dar, dai, jnp.zeros((6, sc), F32)], axis=0)
        dxr_ref[0] = xr_s[...].astype(BF16)
        dxi_ref[0] = xi_s[...].astype(BF16)
        for h_ref, o_ref in ((hr_ref, dsgr_ref), (hi_ref, dsgi_ref)):
            hb = h_ref[0].astype(BF16)
            for j in range(SSM_CHUNK):
                o_ref[j] = _fold(_dot_tn(dy[:, j * LANES:(j + 1) * LANES], hb), SSM_STATE)

    blk = lambda r, c: pl.BlockSpec((1, r, c), lambda b: (b, 0, 0))
    return pl.pallas_call(
        body, name="s5c_state_bwd", grid=(nb,),
        in_specs=[blk(nc, CHUNK_COLS), _cblk(SSM_STATE), _cblk(SSM_STATE), blk(8, sc), blk(nc, sc), blk(nc, sc)],
        out_specs=[blk(nc, sc), blk(nc, sc), blk(8, sc), _cblk(SSM_STATE), _cblk(SSM_STATE)],
        out_shape=[jax.ShapeDtypeStruct((nb, nc, sc), BF16), jax.ShapeDtypeStruct((nb, nc, sc), BF16),
                   jax.ShapeDtypeStruct((nb, 8, sc), F32),
                   jax.ShapeDtypeStruct((SSM_CHUNK, SSM_WIDTH, SSM_STATE), F32),
                   jax.ShapeDtypeStruct((SSM_CHUNK, SSM_WIDTH, SSM_STATE), F32)],
        scratch_shapes=[pltpu.VMEM((CHUNK_COLS, sc), BF16)] + [pltpu.VMEM((nc, sc), F32)] * 4,
        compiler_params=_params(("arbitrary",)),
    )(dy2, sg_r, sg_i, a8, hr, hi)


def _s5d_input_bwd(dy2, u2, kc, sw_r, sw_i, dxr, dxi):
    nb, nc, _ = dy2.shape
    sc = STATE_COLS

    def body(dy_ref, u_ref, kc_ref, swr_ref, swi_ref, dxr_ref, dxi_ref, du_ref, dkc_ref, dswr_ref, dswi_ref,
             tm_s, w_s, dk_s):
        dy = dy_ref[0]
        u = u_ref[0]
        _build_toeplitz(kc_ref, tm_s)
        du = _dot_nt(dy, tm_s[...])
        _build_maps(swr_ref, w_s)
        du += _dot_nt(dxr_ref[0], w_s[...])
        _build_maps(swi_ref, w_s)
        du += _dot_nt(dxi_ref[0], w_s[...])
        du_ref[0] = du.astype(BF16)
        dk_s[...] = jnp.zeros_like(dk_s)
        for ji in range(SSM_CHUNK):
            uj = u[:, ji * LANES:(ji + 1) * LANES]
            rows = _dot_tn(uj, dy)
            for jo in range(ji, SSM_CHUNK):
                dk_s[jo - ji] += rows[:, jo * LANES:(jo + 1) * LANES]
            dswr_ref[ji] = _fold(_dot_tn(uj, dxr_ref[0]), SSM_STATE)
            dswi_ref[ji] = _fold(_dot_tn(uj, dxi_ref[0]), SSM_STATE)
        for d in range(SSM_CHUNK):
            dkc_ref[d] = _fold(dk_s[d], SSM_GROUP)

    blk = lambda r, c: pl.BlockSpec((1, r, c), lambda b: (b, 0, 0))
    return pl.pallas_call(
        body, name="s5d_input_bwd", grid=(nb,),
        in_specs=[blk(nc, CHUNK_COLS), blk(nc, CHUNK_COLS), _cblk(SSM_GROUP), _cblk(SSM_STATE), _cblk(SSM_STATE),
                  blk(nc, sc), blk(nc, sc)],
        out_specs=[blk(nc, CHUNK_COLS), _cblk(SSM_GROUP), _cblk(SSM_STATE), _cblk(SSM_STATE)],
        out_shape=[jax.ShapeDtypeStruct((nb, nc, CHUNK_COLS), BF16),
                   jax.ShapeDtypeStruct((SSM_CHUNK, SSM_WIDTH, SSM_GROUP), F32),
                   jax.ShapeDtypeStruct((SSM_CHUNK, SSM_WIDTH, SSM_STATE), F32),
                   jax.ShapeDtypeStruct((SSM_CHUNK, SSM_WIDTH, SSM_STATE), F32)],
        scratch_shapes=[pltpu.VMEM((CHUNK_COLS, CHUNK_COLS), BF16), pltpu.VMEM((CHUNK_COLS, sc), BF16),
                        pltpu.VMEM((SSM_CHUNK, LANES, LANES), F32)],
        compiler_params=_params(("arbitrary",)),
    )(dy2, u2, kc, sw_r, sw_i, dxr, dxi)


def _b3_inproj(x, dr1, du4, dprest, modv, w_in, tb):
    t = x.shape[0]
    pw = IN_PROJ_WIDTH - SSM_WIDTH

    def body(x_ref, dr1_ref, du_ref, dpr_ref, modv_ref, w_hbm, gx_ref, dp_ref, acc_ref, accb_ref, w_v, sem):
        @pl.when(pl.program_id(0) == 0)
        def _():
            _load_once(w_hbm, w_v, sem)
            acc_ref[...] = jnp.zeros_like(acc_ref)
            accb_ref[...] = jnp.zeros_like(accb_ref)

        du = jnp.concatenate([du_ref[b] for b in range(N_LANE_BLOCKS)], axis=-1)
        dpr = dpr_ref[...]
        dp_ref[:, 0:SSM_WIDTH] = du
        dp_ref[:, SSM_WIDTH:] = dpr
        accb_ref[0:1, 0:SSM_WIDTH] += _colsum(du.astype(F32))
        accb_ref[0:1, SSM_WIDTH:] += _colsum(dpr.astype(F32))
        dh = _dot_nt(du, w_v[:, 0:SSM_WIDTH]) + _dot_nt(dpr, w_v[:, SSM_WIDTH:])
        xn, rstd = _ln_stats(x_ref[...])
        acc_ref[0:1, :] += _colsum(dh * xn)
        acc_ref[1:2, :] += _colsum(dh)
        gx_ref[...] = _ln_bwd(dh * (1.0 + modv_ref[1:2, :]), xn, rstd) + ALPHA * dr1_ref[...]

    tok = lambda w: pl.BlockSpec((tb, w), lambda i: (i, 0))
    return pl.pallas_call(
        body, name="b3_inproj", grid=(t // tb,),
        in_specs=[tok(D_MODEL), tok(D_MODEL), pl.BlockSpec((N_LANE_BLOCKS, tb, LANES), lambda i: (0, i, 0)), tok(pw),
                  _full(modv), ANY],
        out_specs=[tok(D_MODEL), tok(IN_PROJ_WIDTH), pl.BlockSpec((8, D_MODEL), lambda i: (0, 0)),
                   pl.BlockSpec((8, IN_PROJ_WIDTH), lambda i: (0, 0))],
        out_shape=[jax.ShapeDtypeStruct((t, D_MODEL), F32), jax.ShapeDtypeStruct((t, IN_PROJ_WIDTH), BF16),
                   jax.ShapeDtypeStruct((8, D_MODEL), F32), jax.ShapeDtypeStruct((8, IN_PROJ_WIDTH), F32)],
        scratch_shapes=[pltpu.VMEM(w_in.shape, BF16), pltpu.SemaphoreType.DMA],
        compiler_params=_params(("arbitrary",)),
    )(x, dr1, du4, dprest, modv, w_in)


def _tn_matmul(name, a, b, tm, tn):
    t, m = a.shape
    n = b.shape[1]
    tt = min(512, t)
    nk = t // tt

    def body(a_ref, b_ref, o_ref, acc):
        k = pl.program_id(2)

        @pl.when(k == 0)
        def _():
            acc[...] = jnp.zeros_like(acc)

        acc[...] += _dot_tn(a_ref[...], b_ref[...])

        @pl.when(k == nk - 1)
        def _():
            o_ref[...] = acc[...].astype(BF16)

    return pl.pallas_call(
        body, name=name, grid=(m // tm, n // tn, nk),
        in_specs=[pl.BlockSpec((tt, tm), lambda i, j, k: (k, i)), pl.BlockSpec((tt, tn), lambda i, j, k: (k, j))],
        out_specs=pl.BlockSpec((tm, tn), lambda i, j, k: (i, j)),
        out_shape=jax.ShapeDtypeStruct((m, n), BF16),
        scratch_shapes=[pltpu.VMEM((tm, tn), F32)],
        compiler_params=_params(("arbitrary", "arbitrary", "arbitrary")),
    )(a, b)


def _local_step(x, tgt, modv, wb, sp, tb=256):
    t = x.shape[0]
    nc = t // SSM_CHUNK
    row8 = lambda rows, w: jnp.concatenate([r.reshape(1, w) for r in rows] + [jnp.zeros((8 - len(rows), w), F32)], axis=0)
    lnv = row8([sp["ln1_g"], sp["ln1_b"], sp["ln2_g"], sp["ln2_b"]], D_MODEL)
    cvv = row8([sp["cv_dw_b"], sp["cv_ln_g"], sp["cv_ln_b"]], CONV_WIDTH)
    cw32 = jnp.concatenate([sp["cv_dw_w"].reshape(CONV_KERNEL, CONV_WIDTH), jnp.zeros((1, CONV_WIDTH), F32)], axis=0)
    fdw = row8(list(sp["ffn_dw_w"].reshape(FFN_KERNEL, 2 * FFN_HIDDEN)) + [sp["ffn_dw_b"]], 2 * FFN_HIDDEN)
    b_in = sp["b_in"].reshape(1, IN_PROJ_WIDTH)
    ssm = tuple(sp[k] for k in ("ssm_lambda_re", "ssm_lambda_im", "ssm_log_dt", "ssm_b_re", "ssm_b_im", "ssm_c_re",
                                "ssm_c_im", "ssm_d"))
    (kc, sw_r, sw_i, sg_r, sg_i, a), ssm_vjp = jax.vjp(_s5_build, *ssm)
    a8 = jnp.concatenate([a, jnp.zeros((N_LANE_BLOCKS, 6, STATE_COLS), F32)], axis=1)

    u4, prest, h1 = _f1_inproj(x, modv, b_in, wb["w_in"], tb)
    u2 = u4.reshape(N_LANE_BLOCKS, nc, CHUNK_COLS)
    hr, hi = _s5a_state(u2, sw_r, sw_i, a8)
    ys4 = _s5b_out(u2, kc, sg_r, sg_i, hr, hi).reshape(N_LANE_BLOCKS, t, LANES)
    r1, ya, yb, ycv, vc, yg, vs, merged = _f4_mixer(ys4, prest, x, modv, cvv, cw32, wb["ssm_glu_w_a"],
                                                    wb["ssm_glu_w_b"], wb["cv_w_pw"], wb["w_out"], tb)
    dr2, up, h2, z, acc5 = _f5_ffn(r1, tgt, modv, lnv, fdw, wb["ffn_w_up"], wb["ffn_w_down"], tb)
    dup, dyf, acc1a = _b1a_ffn_down(dr2, up, modv, fdw, wb["ffn_w_down"], tb)
    dr1, acc1b = _b1b_ffn_up(dup, dr2, r1, modv, lnv, wb["ffn_w_up"], tb)
    dys4, dprest, dya, dyb, dycv, dym, acc2a, acc2b, acc2w = _b2_mixer(
        dr1, ys4, prest, ya, yb, ycv, vc, merged, modv, cvv, cw32, wb["ssm_glu_w_a"], wb["ssm_glu_w_b"],
        wb["cv_w_pw"], wb["w_out"], tb)
    dy2 = dys4.reshape(N_LANE_BLOCKS, nc, CHUNK_COLS)
    dxr, dxi, da8, dsg_r, dsg_i = _s5c_state_bwd(dy2, sg_r, sg_i, a8, hr, hi)
    du2, dkc, dsw_r, dsw_i = _s5d_input_bwd(dy2, u2, kc, sw_r, sw_i, dxr, dxi)
    dssm = ssm_vjp((dkc, dsw_r, dsw_i, dsg_r, dsg_i, da8[:, 0:2, :]))
    gx, dp, acc3, acc3b = _b3_inproj(x, dr1, du2.reshape(N_LANE_BLOCKS, t, LANES), dprest, modv, wb["w_in"], tb)

    dbig = {
        "w_in": _tn_matmul("dw_in", h1, dp, 512, 896),
        "ssm_glu_w_a": _tn_matmul("dw_glu_a", yg, dya, 512, 1024),
        "ssm_glu_w_b": _tn_matmul("dw_glu_b", yg, dyb, 512, 1024),
        "cv_w_pw": _tn_matmul("dw_pw", vs, dycv, 512, 1024),
        "w_out": _tn_matmul("dw_out", merged, dym, 512, 1024),
        "ffn_w_up": _tn_matmul("dw_up", h2, dup, 512, FFN_COLS),
        "ffn_w_down": _tn_matmul("dw_down", z, dyf, FFN_COLS, 1024),
    }
    dmod = jnp.concatenate([acc3[1], acc3[0], acc2a[0], acc1b[1], acc1b[0], acc5[2]])
    small = {
        "dmod": dmod, "b_in": acc3b[0],
        "ssm_lambda_re": dssm[0], "ssm_lambda_im": dssm[1], "ssm_log_dt": dssm[2], "ssm_b_re": dssm[3],
        "ssm_b_im": dssm[4], "ssm_c_re": dssm[5], "ssm_c_im": dssm[6], "ssm_d": dssm[7],
        "cv_dw_w": acc2w[0:CONV_KERNEL], "cv_dw_b": acc2b[0], "cv_ln_g": acc2b[1], "cv_ln_b": acc2b[2],
        "ln1_g": acc1b[2], "ln1_b": acc1b[3], "ffn_dw_w": acc1a[0:FFN_KERNEL], "ffn_dw_b": acc1a[3],
        "ln2_g": acc5[0], "ln2_b": acc5[1], "loss": jnp.sum(acc5[3]).reshape(1),
    }
    return gx, dbig, small


def _place():
    x, y, c = lax.axis_index("x"), lax.axis_index("y"), lax.axis_index("c")
    chips = [(1 - x, y), (x, 1 - y), (1 - x, 1 - y)]
    return x, y, c, chips


def _allgather(name, shard):
    m_per, n = shard.shape

    def body(x_ref, out_ref, send_sems, recv_sems, local_sem):
        x, y, c, chips = _place()
        me, sibling = (x, y, c), (x, y, 1 - c)

        def rows(px, py, pc):
            return out_ref.at[pl.ds((4 * px + 2 * py + pc) * m_per, m_per), :]

        def copy(k, block, to, src=None):
            return pltpu.make_async_remote_copy(
                src_ref=rows(*block) if src is None else src, dst_ref=rows(*block),
                send_sem=send_sems.at[k], recv_sem=recv_sems.at[k], device_id=to, device_id_type=MESH)

        mine = pltpu.make_async_copy(x_ref, rows(*me), local_sem)
        mine.start()
        first = [copy(0, me, sibling, src=x_ref)]
        first += [copy(1 + j, me, (*chip, c), src=x_ref) for j, chip in enumerate(chips)]
        for cp in first:
            cp.start()
        passed = [copy(4 + j, (*chip, c), sibling) for j, chip in enumerate(chips)]
        for j, chip in enumerate(chips):
            copy(1 + j, (*chip, c), me).wait_recv()
            passed[j].start()
        copy(0, sibling, me).wait_recv()
        for j, chip in enumerate(chips):
            copy(4 + j, (*chip, 1 - c), me).wait_recv()
        for cp in first + passed:
            cp.wait_send()
        mine.wait()

    return pl.pallas_call(
        body, name=name,
        out_shape=jax.ShapeDtypeStruct((N_DEV * m_per, n), shard.dtype),
        in_specs=[pl.BlockSpec(memory_space=pltpu.VMEM)],
        out_specs=pl.BlockSpec(memory_space=pltpu.VMEM),
        scratch_shapes=[pltpu.SemaphoreType.DMA((7,)), pltpu.SemaphoreType.DMA((7,)), pltpu.SemaphoreType.DMA],
        compiler_params=_params(),
    )(shard)


def _piece(kind, shape):
    r, cc = shape
    return (r // 2, cc // N_CHIP) if kind == "col" else (r // (2 * N_CHIP), cc)


def _piece_at(ref, kind, shape, s, k):
    pr, pc = _piece(kind, shape)
    if kind == "col":
        return ref.at[pl.ds(k * pr, pr), pl.ds(pl.multiple_of(s * pc, LANES), pc)]
    return ref.at[pl.ds(pl.multiple_of((2 * s + k) * pr, 16), pr), :]


def _add_rows(pr):
    return 64 if pr % 64 == 0 else 16


def _gather_weights(shards):
    nm = len(BIG)

    def body(*refs):
        ins, outs = refs[:nm], refs[nm:2 * nm]
        stage = refs[2 * nm:3 * nm]
        send, recv, fsend, frecv, lsem = refs[3 * nm:]
        x, y, c, chips = _place()
        s_me = 2 * x + y
        sibling = (x, y, 1 - c)
        pend = []
        for m, (_, kind, shape) in enumerate(BIG):
            stage[m][...] = ins[m][...].astype(BF16)
        for m, (_, kind, shape) in enumerate(BIG):
            pr, pc = _piece(kind, shape)
            for k in range(2):
                cp = pltpu.make_async_copy(stage[m].at[pl.ds(k * pr, pr), :], _piece_at(outs[m], kind, shape, s_me, k),
                                           lsem.at[m, k])
                cp.start()
                pend.append(cp.wait)
            for j, chip in enumerate(chips):
                cp = pltpu.make_async_remote_copy(
                    src_ref=stage[m].at[pl.ds(pl.multiple_of(c * pr, 16), pr), :],
                    dst_ref=_piece_at(outs[m], kind, shape, s_me, c),
                    send_sem=send.at[m, j], recv_sem=recv.at[m, j], device_id=(*chip, c), device_id_type=MESH)
                cp.start()
                pend.append(cp.wait_send)
        for m, (_, kind, shape) in enumerate(BIG):
            for j, (cx, cy) in enumerate(chips):
                got = _piece_at(outs[m], kind, shape, 2 * cx + cy, c)
                pltpu.make_async_remote_copy(src_ref=got, dst_ref=got, send_sem=send.at[m, j], recv_sem=recv.at[m, j],
                                             device_id=(cx, cy, c), device_id_type=MESH).wait_recv()
                cp = pltpu.make_async_remote_copy(src_ref=got, dst_ref=got, send_sem=fsend.at[m, j],
                                                  recv_sem=frecv.at[m, j], device_id=sibling, device_id_type=MESH)
                cp.start()
                pend.append(cp.wait_send)
        for m, (_, kind, shape) in enumerate(BIG):
            for j, (cx, cy) in enumerate(chips):
                got = _piece_at(outs[m], kind, shape, 2 * cx + cy, 1 - c)
                pltpu.make_async_remote_copy(src_ref=got, dst_ref=got, send_sem=fsend.at[m, j], recv_sem=frecv.at[m, j],
                                             device_id=sibling, device_id_type=MESH).wait_recv()
        for w in pend:
            w()

    sem = lambda *s: pltpu.SemaphoreType.DMA(s)
    return pl.pallas_call(
        body, name="gather_weights",
        out_shape=[jax.ShapeDtypeStruct(shape, BF16) for _, _, shape in BIG],
        in_specs=[pl.BlockSpec(memory_space=pltpu.VMEM)] * nm,
        out_specs=[ANY] * nm,
        scratch_shapes=[pltpu.VMEM(s.shape, BF16) for s in shards] + [sem(nm, 3), sem(nm, 3), sem(nm, 3), sem(nm, 3),
                                                                         sem(nm, 2)],
        compiler_params=_params(),
    )(*shards)


def _rs1_sibling(grads):
    nm = len(BIG)

    def body(*refs):
        ins, outs = refs[:nm], refs[nm:2 * nm]
        send, recv = refs[2 * nm:]
        x, y, c, _ = _place()
        cps = []
        for m, (_, kind, shape) in enumerate(BIG):
            for s in range(N_CHIP):
                cp = pltpu.make_async_remote_copy(
                    src_ref=_piece_at(ins[m], kind, shape, s, 1 - c), dst_ref=outs[m].at[s],
                    send_sem=send.at[m, s], recv_sem=recv.at[m, s], device_id=(x, y, 1 - c), device_id_type=MESH)
                cp.start()
                cps.append(cp)
        for cp in cps:
            cp.wait()

    sem = lambda *s: pltpu.SemaphoreType.DMA(s)
    return pl.pallas_call(
        body, name="rs1_sibling",
        out_shape=[jax.ShapeDtypeStruct((N_CHIP,) + _piece(kind, shape), BF16) for _, kind, shape in BIG],
        in_specs=[ANY] * nm, out_specs=[ANY] * nm,
        scratch_shapes=[sem(nm, N_CHIP), sem(nm, N_CHIP)],
        compiler_params=_params(),
    )(*grads)


def _rs2_chips(grads, halves):
    nm = len(BIG)

    def body(*refs):
        gin, hin = refs[:nm], refs[nm:2 * nm]
        own, got = refs[2 * nm:3 * nm], refs[3 * nm:4 * nm]
        send, recv, lsem = refs[4 * nm:]
        x, y, c, chips = _place()
        s_me = 2 * x + y
        for m, (_, kind, shape) in enumerate(BIG):
            pr, pc = _piece(kind, shape)

            def scoped(a, b, m=m, kind=kind, shape=shape, pr=pr):
                loads = [pltpu.make_async_copy(_piece_at(gin[m], kind, shape, s, c), a.at[s], lsem.at[s])
                         for s in range(N_CHIP)]
                loads.append(pltpu.make_async_copy(hin[m], b, lsem.at[N_CHIP]))
                for cp in loads:
                    cp.start()
                for cp in loads:
                    cp.wait()
                step = _add_rows(pr)
                for s in range(N_CHIP):
                    def add(i, _, s=s):
                        r = pl.ds(pl.multiple_of(i * step, 16), step)
                        a[s, r, :] = (a[s, r, :].astype(F32) + b[s, r, :].astype(F32)).astype(BF16)
                        return 0

                    lax.fori_loop(0, pr // step, add, 0)
                waits = []
                for j, (cx, cy) in enumerate(chips):
                    cp = pltpu.make_async_remote_copy(src_ref=a.at[2 * cx + cy], dst_ref=got[m].at[j], send_sem=send.at[m, j],
                                                      recv_sem=recv.at[m, j], device_id=(cx, cy, c), device_id_type=MESH)
                    cp.start()
                    waits.append(cp.wait_send)
                cp = pltpu.make_async_copy(a.at[s_me], own[m], lsem.at[N_CHIP + 1])
                cp.start()
                waits.append(cp.wait)
                for w in waits:
                    w()

            pl.run_scoped(scoped, pltpu.VMEM((N_CHIP, pr, pc), BF16), pltpu.VMEM((N_CHIP, pr, pc), BF16))
        for m in range(nm):
            for j, (cx, cy) in enumerate(chips):
                pltpu.make_async_remote_copy(src_ref=got[m].at[j], dst_ref=got[m].at[j], send_sem=send.at[m, j],
                                             recv_sem=recv.at[m, j], device_id=(cx, cy, c), device_id_type=MESH).wait_recv()

    sem = lambda *s: pltpu.SemaphoreType.DMA(s)
    pieces = [_piece(kind, shape) for _, kind, shape in BIG]
    return pl.pallas_call(
        body, name="rs2_chips",
        out_shape=[jax.ShapeDtypeStruct(p, BF16) for p in pieces] + [jax.ShapeDtypeStruct((3,) + p, BF16) for p in pieces],
        in_specs=[ANY] * (2 * nm), out_specs=[ANY] * (2 * nm),
        scratch_shapes=[sem(nm, 3), sem(nm, 3), sem(N_CHIP + 2)],
        compiler_params=_params(),
    )(*grads, *halves)


def _rs3_finish(own, got):
    nm = len(BIG)

    def body(*refs):
        oin, gin = refs[:nm], refs[nm:2 * nm]
        outs = refs[2 * nm:3 * nm]
        send, recv, lsem = refs[3 * nm:]
        x, y, c, _ = _place()
        for m, (_, kind, shape) in enumerate(BIG):
            pr, pc = _piece(kind, shape)

            def scoped(a, g, f, m=m, pr=pr):
                loads = [pltpu.make_async_copy(oin[m], a, lsem.at[0]), pltpu.make_async_copy(gin[m], g, lsem.at[1])]
                for cp in loads:
                    cp.start()
                for cp in loads:
                    cp.wait()
                step = _add_rows(pr)

                def add(i, _):
                    r = pl.ds(pl.multiple_of(i * step, 16), step)
                    f[r, :] = ((a[r, :].astype(F32) + g[0, r, :].astype(F32)) + g[1, r, :].astype(F32)) + g[2, r, :].astype(F32)
                    return 0

                lax.fori_loop(0, pr // step, add, 0)
                dst = outs[m].at[pl.ds(pl.multiple_of(c * pr, 8), pr), :]
                mine = pltpu.make_async_copy(f, dst, lsem.at[2])
                mine.start()
                cp = pltpu.make_async_remote_copy(src_ref=f, dst_ref=dst, send_sem=send.at[m], recv_sem=recv.at[m],
                                                  device_id=(x, y, 1 - c), device_id_type=MESH)
                cp.start()
                cp.wait_send()
                mine.wait()

            pl.run_scoped(scoped, pltpu.VMEM((pr, pc), BF16), pltpu.VMEM((3, pr, pc), BF16), pltpu.VMEM((pr, pc), F32))
        for m, (_, kind, shape) in enumerate(BIG):
            pr, pc = _piece(kind, shape)
            dst = outs[m].at[pl.ds(pl.multiple_of((1 - c) * pr, 8), pr), :]
            pltpu.make_async_remote_copy(src_ref=dst, dst_ref=dst, send_sem=send.at[m], recv_sem=recv.at[m],
                                         device_id=(x, y, 1 - c), device_id_type=MESH).wait_recv()

    sem = lambda *s: pltpu.SemaphoreType.DMA(s)
    pieces = [_piece(kind, shape) for _, kind, shape in BIG]
    return pl.pallas_call(
        body, name="rs3_finish",
        out_shape=[jax.ShapeDtypeStruct((2 * pr, pc), F32) for pr, pc in pieces],
        in_specs=[ANY] * (2 * nm), out_specs=[ANY] * nm,
        scratch_shapes=[sem(nm), sem(nm), sem(3)],
        compiler_params=_params(),
    )(*own, *got)


def _cond_fwd(c_all, w_shard, b_shard):
    def body(c_ref, w_ref, b_ref, act_ref, mod_ref):
        cv = c_ref[...]
        act = cv * _sig(cv)
        act_ref[...] = act
        mod_ref[...] = _dot(act.astype(BF16), w_ref[...].astype(BF16)) + b_ref[...]

    return pl.pallas_call(
        body, name="cond_fwd",
        out_shape=[jax.ShapeDtypeStruct(c_all.shape, F32), jax.ShapeDtypeStruct((c_all.shape[0], w_shard.shape[1]), F32)],
        compiler_params=_params(),
    )(c_all, w_shard, b_shard)


def _cond_bwd(act_t, dmod_shard):
    k, n = act_t.shape[0], dmod_shard.shape[1]

    def body(a_ref, d_ref, o_ref):
        acc = a_ref[:, 0:1] * d_ref[0:1, :]
        for e in range(1, N_DEV):
            acc += a_ref[:, e:e + 1] * d_ref[e:e + 1, :]
        o_ref[...] = acc

    tr = 256
    return pl.pallas_call(
        body, name="cond_bwd", grid=(k // tr,),
        in_specs=[pl.BlockSpec((tr, N_DEV), lambda i: (i, 0)), _full(dmod_shard)],
        out_specs=pl.BlockSpec((tr, n), lambda i: (i, 0)),
        out_shape=jax.ShapeDtypeStruct((k, n), F32),
        compiler_params=_params(("arbitrary",)),
    )(act_t, dmod_shard)


def _sum_blocks(allp):
    def body(a_ref, o_ref):
        acc = a_ref[0:PACK_ROWS, :]
        for d in range(1, N_DEV):
            acc += a_ref[d * PACK_ROWS:(d + 1) * PACK_ROWS, :]
        o_ref[...] = acc

    return pl.pallas_call(
        body, name="sum_small", out_shape=jax.ShapeDtypeStruct((PACK_ROWS, PACK_COLS), F32), compiler_params=_params(),
    )(allp)


def _adamw(name, w, g, m, v):
    r, cc = w.shape
    tr = r
    for cand in (256, 128, 64, 32, 16, 8):
        if r % cand == 0:
            tr = cand
            break
    bc1 = 1.0 - ADAM_B1 ** ADAM_STEP
    bc2 = 1.0 - ADAM_B2 ** ADAM_STEP

    def body(w_ref, g_ref, m_ref, v_ref, d_ref, nm_ref, nv_ref):
        gv = g_ref[...]
        nm = ADAM_B1 * m_ref[...] + (1.0 - ADAM_B1) * gv
        nv = ADAM_B2 * v_ref[...] + (1.0 - ADAM_B2) * (gv * gv)
        nm_ref[...] = nm
        nv_ref[...] = nv
        d_ref[...] = -ADAM_LR * ((nm / bc1) / (jnp.sqrt(nv / bc2) + ADAM_EPS) + ADAM_WD * w_ref[...])

    spec = pl.BlockSpec((tr, cc), lambda i: (i, 0))
    return pl.pallas_call(
        body, name=name, grid=(r // tr,), in_specs=[spec] * 4, out_specs=[spec] * 3,
        out_shape=[jax.ShapeDtypeStruct((r, cc), F32)] * 3, compiler_params=_params(("arbitrary",)),
    )(w, g, m, v)


def _pack(fields, layout):
    parts = [fields[name].reshape(-1).astype(F32) if name in fields else jnp.zeros((n,), F32) for name, n in layout]
    used = sum(n for _, n in layout)
    parts.append(jnp.zeros((PACK_ROWS * PACK_COLS - used,), F32))
    return jnp.concatenate(parts).reshape(PACK_ROWS, PACK_COLS)


def _unpack(flat, layout):
    flat = flat.reshape(-1)
    out, o = {}, 0
    for name, n in layout:
        out[name] = flat[o:o + n]
        o += n
    return out


def kernel(x, c, w_cond, b_cond, w_in, b_in, ssm_lambda_re, ssm_lambda_im, ssm_log_dt, ssm_b_re, ssm_b_im, ssm_c_re, ssm_c_im, ssm_d, ssm_glu_w_a, ssm_glu_w_b, cv_dw_w, cv_dw_b, cv_ln_g, cv_ln_b, cv_w_pw, w_out, ln1_g, ln1_b, ffn_w_up, ffn_dw_w, ffn_dw_b, ffn_w_down, ln2_g, ln2_b, loss_target, m_w_cond, m_b_cond, m_w_in, m_b_in, m_ssm_lambda_re, m_ssm_lambda_im, m_ssm_log_dt, m_ssm_b_re, m_ssm_b_im, m_ssm_c_re, m_ssm_c_im, m_ssm_d, m_ssm_glu_w_a, m_ssm_glu_w_b, m_cv_dw_w, m_cv_dw_b, m_cv_ln_g, m_cv_ln_b, m_cv_w_pw, m_w_out, m_ln1_g, m_ln1_b, m_ffn_w_up, m_ffn_dw_w, m_ffn_dw_b, m_ffn_w_down, m_ln2_g, m_ln2_b, v_w_cond, v_b_cond, v_w_in, v_b_in, v_ssm_lambda_re, v_ssm_lambda_im, v_ssm_log_dt, v_ssm_b_re, v_ssm_b_im, v_ssm_c_re, v_ssm_c_im, v_ssm_d, v_ssm_glu_w_a, v_ssm_glu_w_b, v_cv_dw_w, v_cv_dw_b, v_cv_ln_g, v_cv_ln_b, v_cv_w_pw, v_w_out, v_ln1_g, v_ln1_b, v_ffn_w_up, v_ffn_dw_w, v_ffn_dw_b, v_ffn_w_down, v_ln2_g, v_ln2_b):
    given = locals()
    a = {n: given[n] for n in INPUTS}
    xi, yi, ci = lax.axis_index("x"), lax.axis_index("y"), lax.axis_index("c")
    s_me = 2 * xi + yi
    e_me = 4 * xi + 2 * yi + ci

    first = jnp.concatenate([
        jnp.concatenate([a["c"], jnp.zeros((7, D_MODEL), F32)], axis=0),
        jnp.concatenate([a["cv_dw_w"].reshape(-1), a["ffn_dw_w"].reshape(-1)]).reshape(8, D_MODEL)], axis=0)
    first_all = _allgather("gather_c", first).reshape(N_DEV, 16, D_MODEL)
    c_all = first_all[:, 0, :]
    dw_all = first_all[0::2, 8:, :].reshape(N_CHIP, 8 * D_MODEL)
    n_cv = CONV_KERNEL * CONV_WIDTH // N_CHIP
    cv_dw_full = dw_all[:, :n_cv].reshape(N_CHIP, CONV_KERNEL, CONV_WIDTH // N_CHIP).transpose(1, 0, 2) \
        .reshape(CONV_KERNEL, CONV_WIDTH)
    ffn_dw_full = dw_all[:, n_cv:].reshape(N_CHIP, FFN_KERNEL, 2 * FFN_HIDDEN // N_CHIP).transpose(1, 0, 2) \
        .reshape(FFN_KERNEL, 2 * FFN_HIDDEN)
    ncols = N_COND * D_MODEL // N_CHIP
    b_cond_shard = lax.dynamic_slice(a["b_cond"], (0, s_me * ncols), (1, ncols))
    c_act_all, modp = _cond_fwd(c_all, a["w_cond"][0], b_cond_shard)
    modp_all = _allgather("gather_mod", modp).reshape(N_DEV, N_DEV, ncols)[0::2]
    mod_e = lax.dynamic_index_in_dim(modp_all, e_me, axis=1, keepdims=False).reshape(N_COND, D_MODEL)
    modv = jnp.concatenate([mod_e, jnp.zeros((2, D_MODEL), F32)], axis=0)

    wb = dict(zip([n for n, _, _ in BIG], _gather_weights([a[n][0] for n, _, _ in BIG])))
    sp = {n: a[n][0] for n in ("b_in", "ssm_lambda_re", "ssm_lambda_im", "ssm_log_dt", "ssm_b_re", "ssm_b_im",
                               "ssm_c_re", "ssm_c_im", "ssm_d", "cv_dw_b", "cv_ln_g", "cv_ln_b", "ln1_g", "ln1_b",
                               "ffn_dw_b", "ln2_g", "ln2_b")}
    sp["cv_dw_w"] = cv_dw_full
    sp["ffn_dw_w"] = ffn_dw_full
    gx, dbig, small = _local_step(a["x"][0], a["loss_target"][0], modv, wb, sp)

    small["c_act"] = lax.dynamic_index_in_dim(c_act_all, e_me, axis=0, keepdims=False)
    packed_all = _allgather("gather_small", _pack(small, PACK))
    tot = _unpack(_sum_blocks(packed_all), PACK)
    rows = packed_all.reshape(N_DEV, PACK_ROWS * PACK_COLS)
    dmod_all = rows[:, 0:N_COND * D_MODEL]
    act_all = rows[:, N_COND * D_MODEL:(N_COND + 1) * D_MODEL]
    g_w_cond = _cond_bwd(act_all.T, lax.dynamic_slice(dmod_all, (0, s_me * ncols), (N_DEV, ncols)))

    glist = [dbig[n] for n, _, _ in BIG]
    halves = _rs1_sibling(glist)
    r2 = _rs2_chips(glist, halves)
    gsh = _rs3_finish(r2[:len(BIG)], r2[len(BIG):])

    grads = {"w_cond": g_w_cond[None], "b_cond": tot["dmod"].reshape(1, -1)}
    for (n, kind, shape), g in zip(BIG, gsh):
        grads[n] = g.reshape(a[n].shape)
    for n in ("b_in", "ssm_lambda_re", "ssm_lambda_im", "ssm_log_dt", "ssm_b_re", "ssm_b_im", "ssm_c_re", "ssm_c_im",
              "ssm_d", "cv_dw_b", "cv_ln_g", "cv_ln_b", "ln1_g", "ln1_b", "ffn_dw_b", "ln2_g", "ln2_b"):
        grads[n] = tot[n].reshape(a[n].shape)
    wcv = CONV_WIDTH // N_CHIP
    grads["cv_dw_w"] = lax.dynamic_slice(tot["cv_dw_w"].reshape(CONV_KERNEL, CONV_WIDTH), (0, s_me * wcv),
                                         (CONV_KERNEL, wcv)).reshape(a["cv_dw_w"].shape)
    wff = 2 * FFN_HIDDEN // N_CHIP
    grads["ffn_dw_w"] = lax.dynamic_slice(tot["ffn_dw_w"].reshape(FFN_KERNEL, 2 * FFN_HIDDEN), (0, s_me * wff),
                                          (FFN_KERNEL, wff)).reshape(a["ffn_dw_w"].shape)

    delta, new_m, new_v = {}, {}, {}
    for n in ["w_cond"] + [n for n, _, _ in BIG]:
        d, nm_, nv_ = _adamw("adamw_" + n, a[n][0], grads[n][0], a["m_" + n][0], a["v_" + n][0])
        delta[n], new_m[n], new_v[n] = d[None], nm_[None], nv_[None]
    upd = [n for n, _ in SMALL_UPD]
    d, nm_, nv_ = _adamw("adamw_small", _pack({n: a[n] for n in upd}, SMALL_UPD), _pack({n: grads[n] for n in upd}, SMALL_UPD),
                         _pack({n: a["m_" + n] for n in upd}, SMALL_UPD), _pack({n: a["v_" + n] for n in upd}, SMALL_UPD))
    for dst, flat in ((delta, d), (new_m, nm_), (new_v, nv_)):
        for n, val in _unpack(flat, SMALL_UPD).items():
            dst[n] = val.reshape(a[n].shape)

    loss = tot["loss"].reshape(())
    return (loss, gx[None], *[grads[n] for n in WEIGHTS], *[delta[n] for n in WEIGHTS],
            *[new_m[n] for n in WEIGHTS], *[new_v[n] for n in WEIGHTS])
```

```python
import functools
import math

import jax
import jax.numpy as jnp
from jax import lax
from jax.experimental import pallas as pl
from jax.experimental.pallas import tpu as pltpu

F32 = jnp.float32
BF16 = jnp.bfloat16

D_MODEL = 1024
SSM_WIDTH = 512
SSM_GROUP = 16
SSM_GROUPS = 32
SSM_STATE = 64
CONV_WIDTH = 512
CONV_KERNEL = 31
FFN_HIDDEN = 2816
FFN_KERNEL = 3
IN_PROJ_WIDTH = 3584
N_COND = 6
ALPHA = 2.0 ** 0.25
LN_EPS = 1e-5
ADAM_LR, ADAM_B1, ADAM_B2, ADAM_EPS, ADAM_WD, ADAM_STEP = 0.001, 0.9, 0.999, 1e-08, 0.01, 10

N_DEV = 8
N_CHIP = 4
LANES = 128
SSM_CHUNK = 16
LANE_GROUPS = LANES // SSM_GROUP
N_LANE_BLOCKS = SSM_WIDTH // LANES
STATE_COLS = LANE_GROUPS * SSM_STATE
CHUNK_COLS = SSM_CHUNK * LANES
CONV_HALO = 32
VMEM_LIMIT = 56 * 1024 * 1024
MESH = pl.DeviceIdType.MESH

BIG = (
    ("w_in", "col", (D_MODEL, IN_PROJ_WIDTH)),
    ("ssm_glu_w_a", "col", (SSM_WIDTH, D_MODEL)),
    ("ssm_glu_w_b", "col", (SSM_WIDTH, D_MODEL)),
    ("cv_w_pw", "col", (CONV_WIDTH, D_MODEL)),
    ("w_out", "row", (D_MODEL, D_MODEL)),
    ("ffn_w_up", "col", (D_MODEL, 2 * FFN_HIDDEN)),
    ("ffn_w_down", "row", (FFN_HIDDEN, D_MODEL)),
)

WEIGHTS = ['w_cond', 'b_cond', 'w_in', 'b_in', 'ssm_lambda_re', 'ssm_lambda_im', 'ssm_log_dt', 'ssm_b_re', 'ssm_b_im',
           'ssm_c_re', 'ssm_c_im', 'ssm_d', 'ssm_glu_w_a', 'ssm_glu_w_b', 'cv_dw_w', 'cv_dw_b', 'cv_ln_g', 'cv_ln_b',
           'cv_w_pw', 'w_out', 'ln1_g', 'ln1_b', 'ffn_w_up', 'ffn_dw_w', 'ffn_dw_b', 'ffn_w_down', 'ln2_g', 'ln2_b']
INPUTS = ['x', 'c'] + WEIGHTS + ['loss_target'] + ['m_' + n for n in WEIGHTS] + ['v_' + n for n in WEIGHTS]

PACK = (
    ("dmod", N_COND * D_MODEL), ("c_act", D_MODEL), ("b_in", IN_PROJ_WIDTH),
    ("ssm_lambda_re", SSM_GROUPS * SSM_STATE), ("ssm_lambda_im", SSM_GROUPS * SSM_STATE), ("ssm_log_dt", SSM_GROUPS),
    ("ssm_b_re", SSM_GROUPS * SSM_STATE * SSM_GROUP), ("ssm_b_im", SSM_GROUPS * SSM_STATE * SSM_GROUP),
    ("ssm_c_re", SSM_GROUPS * SSM_STATE * SSM_GROUP), ("ssm_c_im", SSM_GROUPS * SSM_STATE * SSM_GROUP),
    ("ssm_d", SSM_GROUPS * SSM_GROUP), ("cv_dw_w", CONV_KERNEL * CONV_WIDTH), ("cv_dw_b", CONV_WIDTH),
    ("cv_ln_g", CONV_WIDTH), ("cv_ln_b", CONV_WIDTH), ("ln1_g", D_MODEL), ("ln1_b", D_MODEL),
    ("ffn_dw_w", FFN_KERNEL * 2 * FFN_HIDDEN), ("ffn_dw_b", 2 * FFN_HIDDEN), ("ln2_g", D_MODEL), ("ln2_b", D_MODEL),
    ("loss", 1),
)
PACK_COLS = 1024
PACK_ROWS = 192
assert sum(n for _, n in PACK) <= PACK_ROWS * PACK_COLS

SMALL_UPD = (
    ("b_cond", N_COND * D_MODEL), ("b_in", IN_PROJ_WIDTH),
    ("ssm_lambda_re", SSM_GROUPS * SSM_STATE), ("ssm_lambda_im", SSM_GROUPS * SSM_STATE), ("ssm_log_dt", SSM_GROUPS),
    ("ssm_b_re", SSM_GROUPS * SSM_STATE * SSM_GROUP), ("ssm_b_im", SSM_GROUPS * SSM_STATE * SSM_GROUP),
    ("ssm_c_re", SSM_GROUPS * SSM_STATE * SSM_GROUP), ("ssm_c_im", SSM_GROUPS * SSM_STATE * SSM_GROUP),
    ("ssm_d", SSM_GROUPS * SSM_GROUP), ("cv_dw_w", CONV_KERNEL * CONV_WIDTH // N_CHIP), ("cv_dw_b", CONV_WIDTH),
    ("cv_ln_g", CONV_WIDTH), ("cv_ln_b", CONV_WIDTH), ("ln1_g", D_MODEL), ("ln1_b", D_MODEL),
    ("ffn_dw_w", FFN_KERNEL * 2 * FFN_HIDDEN // N_CHIP), ("ffn_dw_b", 2 * FFN_HIDDEN), ("ln2_g", D_MODEL),
    ("ln2_b", D_MODEL),
)
assert sum(n for _, n in SMALL_UPD) <= PACK_ROWS * PACK_COLS


def _params(sem=None, **kw):
    return pltpu.CompilerParams(dimension_semantics=sem, vmem_limit_bytes=VMEM_LIMIT, **kw)


def _ln_stats(x):
    mu = jnp.mean(x, axis=-1, keepdims=True)
    xc = x - mu
    var = jnp.mean(xc * xc, axis=-1, keepdims=True)
    rstd = lax.rsqrt(var + LN_EPS)
    return xc * rstd, rstd


def _ln_bwd(dxhat, xhat, rstd):
    m1 = jnp.mean(dxhat, axis=-1, keepdims=True)
    m2 = jnp.mean(dxhat * xhat, axis=-1, keepdims=True)
    return rstd * (dxhat - m1 - xhat * m2)


def _sig(x):
    return 1.0 / (1.0 + jnp.exp(-x))


def _gelu(x):
    return 0.5 * x * (1.0 + lax.erf(x * (1.0 / math.sqrt(2.0))))


def _dgelu(x):
    return 0.5 * (1.0 + lax.erf(x * (1.0 / math.sqrt(2.0)))) + x * jnp.exp(-0.5 * x * x) * (1.0 / math.sqrt(2.0 * math.pi))


def _gelu_and_grad(x):
    er = lax.erf(x * (1.0 / math.sqrt(2.0)))
    cdf = 0.5 * (1.0 + er)
    return x * cdf, cdf + x * jnp.exp(-0.5 * x * x) * (1.0 / math.sqrt(2.0 * math.pi))


def _colsum(a):
    return jnp.sum(a, axis=0, keepdims=True)


def _fill_rotations(buf, rot, rows):
    for r in range(1, 8):
        rot[r - 1] = buf[pl.ds(r, rows), :]


def _rows_at(buf, rot, offset, tb):
    q, r = divmod(offset, 8)
    if r == 0:
        return buf[pl.ds(8 * q, tb), :]
    return rot[r - 1, pl.ds(8 * q, tb), :]


def _shift_mats(tb):
    r = lax.broadcasted_iota(jnp.int32, (tb, tb), 0)
    c = lax.broadcasted_iota(jnp.int32, (tb, tb), 1)
    return ([(r - c == k).astype(BF16) for k in (1, 2)], [(c - r == k).astype(BF16) for k in (1, 2)])


def _head_rows(prev8, k, tb):
    row = lax.broadcasted_iota(jnp.int32, prev8.shape, 0)
    head = jnp.where(row < k, pltpu.roll(prev8, k, axis=0), 0.0)
    return jnp.concatenate([head, jnp.zeros((tb - 8, prev8.shape[1]), F32)], axis=0)


def _tail_rows(next8, k, tb):
    row = lax.broadcasted_iota(jnp.int32, next8.shape, 0)
    tail = jnp.where(row >= 8 - k, pltpu.roll(next8, 8 - k, axis=0), 0.0)
    return jnp.concatenate([jnp.zeros((tb - 8, next8.shape[1]), F32), tail], axis=0)


def _dot(a, b):
    return jnp.dot(a, b, preferred_element_type=F32)


def _dot_nt(a, b):
    return lax.dot_general(a, b, (((1,), (1,)), ((), ())), preferred_element_type=F32)


def _dot_tn(a, b):
    return lax.dot_general(a, b, (((0,), (0,)), ((), ())), preferred_element_type=F32)


def _load_once(src, dst, sem):
    cp = pltpu.make_async_copy(src, dst, sem)
    cp.start()
    cp.wait()


def _full(a):
    nd = a.ndim
    return pl.BlockSpec(a.shape, lambda *_: (0,) * nd)


ANY = pl.BlockSpec(memory_space=pl.ANY)


def _f1_inproj(x, modv, b_in, w_in, tb):
    t = x.shape[0]
    chunks = [(j * 512, 512) for j in range(IN_PROJ_WIDTH // 512)]

    def body(x_ref, modv_ref, b_ref, w_hbm, u4_ref, prest_ref, h_ref, w_v, sem):
        @pl.when(pl.program_id(0) == 0)
        def _():
            _load_once(w_hbm, w_v, sem)

        xn, _ = _ln_stats(x_ref[...])
        h = (xn * (1.0 + modv_ref[1:2, :]) + modv_ref[0:1, :]).astype(BF16)
        h_ref[...] = h
        for c0, cw in chunks:
            p = _dot(h, w_v[:, c0:c0 + cw]) + b_ref[:, c0:c0 + cw]
            if c0 == 0:
                for b in range(N_LANE_BLOCKS):
                    u4_ref[b] = p[:, b * LANES:(b + 1) * LANES].astype(BF16)
            else:
                prest_ref[:, c0 - SSM_WIDTH:c0 - SSM_WIDTH + cw] = p

    return pl.pallas_call(
        body, name="f1_inproj", grid=(t // tb,),
        in_specs=[pl.BlockSpec((tb, D_MODEL), lambda i: (i, 0)), _full(modv), _full(b_in), ANY],
        out_specs=[pl.BlockSpec((N_LANE_BLOCKS, tb, LANES), lambda i: (0, i, 0)),
                   pl.BlockSpec((tb, IN_PROJ_WIDTH - SSM_WIDTH), lambda i: (i, 0)),
                   pl.BlockSpec((tb, D_MODEL), lambda i: (i, 0))],
        out_shape=[jax.ShapeDtypeStruct((N_LANE_BLOCKS, t, LANES), BF16),
                   jax.ShapeDtypeStruct((t, IN_PROJ_WIDTH - SSM_WIDTH), F32),
                   jax.ShapeDtypeStruct((t, D_MODEL), BF16)],
        scratch_shapes=[pltpu.VMEM(w_in.shape, BF16), pltpu.SemaphoreType.DMA],
        compiler_params=_params(("arbitrary",)),
    )(x, modv, b_in, w_in)


def _s5_build(lam_re, lam_im, log_dt, b_re, b_im, c_re, c_im, d):
    hi = lax.Precision.HIGHEST
    el, g, n, p, nb = SSM_CHUNK, SSM_GROUPS, SSM_STATE, SSM_GROUP, N_LANE_BLOCKS
    lr = jnp.minimum(lam_re, -1e-4)
    li = lam_im
    dt = jnp.exp(log_dt)[:, None]
    mag = jnp.exp(lr * dt)
    ang = li * dt
    lbr, lbi = mag * jnp.cos(ang), mag * jnp.sin(ang)
    num_r, num_i = lbr - 1.0, lbi
    den = lr * lr + li * li
    coef_r = (num_r * lr + num_i * li) / den
    coef_i = (num_i * lr - num_r * li) / den
    bbar_r = coef_r[..., None] * b_re - coef_i[..., None] * b_im
    bbar_i = coef_r[..., None] * b_im + coef_i[..., None] * b_re
    k = jnp.arange(el + 1, dtype=F32)[:, None, None]
    pmag = jnp.exp(k * (lr * dt)[None])
    pr, pi = pmag * jnp.cos(k * ang[None]), pmag * jnp.sin(k * ang[None])
    car = c_re[None] * pr[:, :, None, :] - c_im[None] * pi[:, :, None, :]
    cai = c_re[None] * pi[:, :, None, :] + c_im[None] * pr[:, :, None, :]
    kern = (jnp.einsum("kgpn,gnq->kgqp", car[:el], bbar_r, precision=hi)
            - jnp.einsum("kgpn,gnq->kgqp", cai[:el], bbar_i, precision=hi))
    kern = kern.at[0].add(jnp.eye(p, dtype=F32)[None] * d[:, None, :])
    kc = kern.reshape(el, g * p, p)
    bt_r = bbar_r.transpose(0, 2, 1)[None]
    bt_i = bbar_i.transpose(0, 2, 1)[None]
    rev = el - 1 - jnp.arange(el)
    qr, qi = pr[rev][:, :, None, :], pi[rev][:, :, None, :]
    sw_r = (qr * bt_r - qi * bt_i).reshape(el, g * p, n)
    sw_i = (qr * bt_i + qi * bt_r).reshape(el, g * p, n)
    sg_r = car[1:].reshape(el, g * p, n)
    sg_i = (-cai[1:]).reshape(el, g * p, n)
    a = jnp.stack([pr[el].reshape(nb, LANE_GROUPS * n), pi[el].reshape(nb, LANE_GROUPS * n)], axis=1)
    return kc, sw_r, sw_i, sg_r, sg_i, a


def _expand(src, reps):
    rows, w = src.shape
    cols = reps * w
    r = lax.broadcasted_iota(jnp.int32, (w, cols), 0)
    c = lax.broadcasted_iota(jnp.int32, (w, cols), 1)
    rep = (r == (c & (w - 1))).astype(BF16)
    out = _dot(src.astype(BF16), rep)
    rg = lax.broadcasted_iota(jnp.int32, (rows, cols), 0) // SSM_GROUP
    cg = lax.broadcasted_iota(jnp.int32, (rows, cols), 1) // w
    return jnp.where(rg == cg, out, 0.0).astype(BF16)


def _fold(x, w):
    rows, cols = x.shape
    rg = lax.broadcasted_iota(jnp.int32, (rows, cols), 0) // SSM_GROUP
    cg = lax.broadcasted_iota(jnp.int32, (rows, cols), 1) // w
    x = jnp.where(rg == cg, x, 0.0)
    while cols > LANES:
        x = x[:, :cols // 2] + x[:, cols // 2:]
        cols //= 2
    s = LANES // 2
    while s >= w:
        x = x + pltpu.roll(x, s, axis=1)
        s //= 2
    return x[:, :w]


def _build_maps(s_ref, dst):
    for j in range(SSM_CHUNK):
        dst[j * LANES:(j + 1) * LANES, :] = _expand(s_ref[j], LANE_GROUPS)


def _build_toeplitz(kc_ref, dst):
    dst[...] = jnp.zeros_like(dst)
    for d in range(SSM_CHUNK):
        blk = _expand(kc_ref[d], LANE_GROUPS)
        for ji in range(SSM_CHUNK - d):
            jo = ji + d
            dst[ji * LANES:(ji + 1) * LANES, jo * LANES:(jo + 1) * LANES] = blk


def _cblk(w):
    return pl.BlockSpec((SSM_CHUNK, LANES, w), lambda b: (0, b, 0))


def _s5a_state(u2, sw_r, sw_i, a8):
    nb, nc, _ = u2.shape
    sc = STATE_COLS

    def body(u_ref, swr_ref, swi_ref, a_ref, hr_ref, hi_ref, w_s, xr_s, xi_s):
        u = u_ref[0]
        _build_maps(swr_ref, w_s)
        xr_s[...] = _dot(u, w_s[...])
        _build_maps(swi_ref, w_s)
        xi_s[...] = _dot(u, w_s[...])
        ar = a_ref[0, 0:1, :]
        ai = a_ref[0, 1:2, :]

        def step(c, carry):
            hr, hi = carry
            hr_ref[0, pl.ds(c, 1), :] = hr
            hi_ref[0, pl.ds(c, 1), :] = hi
            xr = xr_s[pl.ds(c, 1), :]
            xi = xi_s[pl.ds(c, 1), :]
            return ar * hr - ai * hi + xr, ar * hi + ai * hr + xi

        z = jnp.zeros((1, sc), F32)
        lax.fori_loop(0, nc, step, (z, z))

    return pl.pallas_call(
        body, name="s5a_state", grid=(nb,),
        in_specs=[pl.BlockSpec((1, nc, CHUNK_COLS), lambda b: (b, 0, 0)), _cblk(SSM_STATE), _cblk(SSM_STATE),
                  pl.BlockSpec((1, 8, sc), lambda b: (b, 0, 0))],
        out_specs=[pl.BlockSpec((1, nc, sc), lambda b: (b, 0, 0))] * 2,
        out_shape=[jax.ShapeDtypeStruct((nb, nc, sc), F32)] * 2,
        scratch_shapes=[pltpu.VMEM((CHUNK_COLS, sc), BF16), pltpu.VMEM((nc, sc), F32), pltpu.VMEM((nc, sc), F32)],
        compiler_params=_params(("arbitrary",)),
    )(u2, sw_r, sw_i, a8)


def _s5b_out(u2, kc, sg_r, sg_i, hr, hi):
    nb, nc, _ = u2.shape
    sc = STATE_COLS
    cw = 512

    def body(u_ref, kc_ref, sgr_ref, sgi_ref, hr_ref, hi_ref, y_ref, tm_s, gr_s, gi_s):
        _build_toeplitz(kc_ref, tm_s)
        _build_maps(sgr_ref, gr_s)
        _build_maps(sgi_ref, gi_s)
        u = u_ref[0]
        h_r = hr_ref[0].astype(BF16)
        h_i = hi_ref[0].astype(BF16)
        for j in range(CHUNK_COLS // cw):
            cs = slice(j * cw, (j + 1) * cw)
            y_ref[0, :, cs] = _dot(u, tm_s[:, cs]) + _dot_nt(h_r, gr_s[cs, :]) + _dot_nt(h_i, gi_s[cs, :])

    return pl.pallas_call(
        body, name="s5b_out", grid=(nb,),
        in_specs=[pl.BlockSpec((1, nc, CHUNK_COLS), lambda b: (b, 0, 0)), _cblk(SSM_GROUP), _cblk(SSM_STATE),
                  _cblk(SSM_STATE), pl.BlockSpec((1, nc, sc), lambda b: (b, 0, 0)),
                  pl.BlockSpec((1, nc, sc), lambda b: (b, 0, 0))],
        out_specs=pl.BlockSpec((1, nc, CHUNK_COLS), lambda b: (b, 0, 0)),
        out_shape=jax.ShapeDtypeStruct((nb, nc, CHUNK_COLS), F32),
        scratch_shapes=[pltpu.VMEM((CHUNK_COLS, CHUNK_COLS), BF16), pltpu.VMEM((CHUNK_COLS, sc), BF16),
                        pltpu.VMEM((CHUNK_COLS, sc), BF16)],
        compiler_params=_params(("arbitrary",)),
    )(u2, kc, sg_r, sg_i, hr, hi)


def _f4_mixer(ys4, prest, x, modv, cvv, cw32, w_a, w_b, w_pw, w_out, tb):
    t = x.shape[0]
    hb = tb // CONV_HALO

    def body(ys_ref, pr_ref, halo_ref, x_ref, modv_ref, cvv_ref, cw_ref, wa_ref, wb_ref, wpw_ref, wout_ref,
             r1_ref, ya_ref, yb_ref, ycv_ref, vc_ref, yg_ref, vs_ref, mg_ref, vbuf, vrot):
        i = pl.program_id(0)
        ys = jnp.concatenate([ys_ref[b] for b in range(N_LANE_BLOCKS)], axis=-1)
        yg = _gelu(ys).astype(BF16)
        yg_ref[...] = yg
        ya = _dot(yg, wa_ref[...])
        yb = _dot(yg, wb_ref[...])
        ya_ref[...] = ya.astype(BF16)
        yb_ref[...] = yb.astype(BF16)
        yssm = ya * _sig(yb)
        hv = halo_ref[:, 0:CONV_WIDTH] * _sig(halo_ref[:, CONV_WIDTH:2 * CONV_WIDTH])
        vbuf[0:CONV_HALO, :] = jnp.where(i == 0, 0.0, hv)
        vbuf[CONV_HALO:, :] = pr_ref[:, 0:CONV_WIDTH] * _sig(pr_ref[:, CONV_WIDTH:2 * CONV_WIDTH])
        _fill_rotations(vbuf, vrot, tb + CONV_HALO - 8)
        acc = jnp.zeros((tb, CONV_WIDTH), F32)
        for k in range(CONV_KERNEL):
            acc += _rows_at(vbuf, vrot, CONV_HALO - CONV_KERNEL + 1 + k, tb) * cw_ref[k:k + 1, :]
        vc = acc + cvv_ref[0:1, :]
        vc_ref[...] = vc
        xh, _ = _ln_stats(vc)
        vl = xh * cvv_ref[1:2, :] + cvv_ref[2:3, :]
        vs = (vl * _sig(vl)).astype(BF16)
        vs_ref[...] = vs
        ycv = _dot(vs, wpw_ref[...])
        ycv_ref[...] = ycv.astype(BF16)
        gs = pr_ref[:, 2 * CONV_WIDTH:2 * CONV_WIDTH + D_MODEL]
        gc = pr_ref[:, 2 * CONV_WIDTH + D_MODEL:]
        merged = (_sig(gs) * yssm + _sig(gc) * ycv).astype(BF16)
        mg_ref[...] = merged
        ym = _dot(merged, wout_ref[...])
        r1_ref[...] = ALPHA * x_ref[...] + modv_ref[2:3, :] * ym

    tok = lambda w: pl.BlockSpec((tb, w), lambda i: (i, 0))
    return pl.pallas_call(
        body, name="f4_mixer", grid=(t // tb,),
        in_specs=[pl.BlockSpec((N_LANE_BLOCKS, tb, LANES), lambda i: (0, i, 0)), tok(prest.shape[1]),
                  pl.BlockSpec((CONV_HALO, 2 * CONV_WIDTH), lambda i: (jnp.maximum(i * hb - 1, 0), 0)),
                  tok(D_MODEL), _full(modv), _full(cvv), _full(cw32), _full(w_a), _full(w_b), _full(w_pw), _full(w_out)],
        out_specs=[tok(D_MODEL), tok(D_MODEL), tok(D_MODEL), tok(D_MODEL), tok(CONV_WIDTH), tok(SSM_WIDTH),
                   tok(CONV_WIDTH), tok(D_MODEL)],
        out_shape=[jax.ShapeDtypeStruct((t, D_MODEL), F32), jax.ShapeDtypeStruct((t, D_MODEL), BF16),
                   jax.ShapeDtypeStruct((t, D_MODEL), BF16), jax.ShapeDtypeStruct((t, D_MODEL), BF16),
                   jax.ShapeDtypeStruct((t, CONV_WIDTH), F32), jax.ShapeDtypeStruct((t, SSM_WIDTH), BF16),
                   jax.ShapeDtypeStruct((t, CONV_WIDTH), BF16), jax.ShapeDtypeStruct((t, D_MODEL), BF16)],
        scratch_shapes=[pltpu.VMEM((tb + CONV_HALO, CONV_WIDTH), F32),
                        pltpu.VMEM((7, tb + CONV_HALO - 8, CONV_WIDTH), F32)],
        compiler_params=_params(("arbitrary",)),
    )(ys4, prest, prest, x, modv, cvv, cw32, w_a, w_b, w_pw, w_out)


FFN_COLS = 1408


def _f5_ffn(r1, tgt, modv, lnv, fdw, w_up, w_down, tb):
    t = r1.shape[0]
    fw = 2 * FFN_HIDDEN

    def body(r1_ref, tgt_ref, modv_ref, lnv_ref, fdw_ref, wup_hbm, wdn_hbm,
             dr2_ref, up_ref, h2_ref, z_ref, acc_ref, wup_v, wdn_v, upbuf, sems):
        i = pl.program_id(0)

        @pl.when(i == 0)
        def _():
            _load_once(wup_hbm, wup_v, sems.at[0])
            _load_once(wdn_hbm, wdn_v, sems.at[1])
            acc_ref[...] = jnp.zeros_like(acc_ref)
            upbuf[0:8, :] = jnp.zeros((8, fw), F32)

        xh1, _ = _ln_stats(r1_ref[...])
        x1 = xh1 * lnv_ref[0:1, :] + lnv_ref[1:2, :]
        xn2, _ = _ln_stats(x1)
        h2 = (xn2 * (1.0 + modv_ref[4:5, :]) + modv_ref[3:4, :]).astype(BF16)
        h2_ref[...] = h2
        for j in range(fw // FFN_COLS):
            cs = slice(j * FFN_COLS, (j + 1) * FFN_COLS)
            up = _dot(h2, wup_v[:, cs])
            upbuf[8:, cs] = up
            up_ref[:, cs] = up.astype(BF16)

        def conv(cs):
            return (fdw_ref[0:1, cs] * upbuf[pl.ds(6, tb), cs] + fdw_ref[1:2, cs] * upbuf[pl.ds(7, tb), cs]
                    + fdw_ref[2:3, cs] * upbuf[pl.ds(8, tb), cs] + fdw_ref[3:4, cs])

        yf = jnp.zeros((tb, D_MODEL), F32)
        for j in range(FFN_HIDDEN // FFN_COLS):
            ca = slice(j * FFN_COLS, (j + 1) * FFN_COLS)
            cv = slice(FFN_HIDDEN + j * FFN_COLS, FFN_HIDDEN + (j + 1) * FFN_COLS)
            z = (_gelu(conv(ca)) * conv(cv)).astype(BF16)
            z_ref[:, ca] = z
            yf += _dot(z, wdn_v[ca, :])
        upbuf[0:8, :] = upbuf[pl.ds(tb, 8), :]
        r2 = ALPHA * x1 + modv_ref[5:6, :] * yf
        xh2, rstd2 = _ln_stats(r2)
        e = xh2 * lnv_ref[2:3, :] + lnv_ref[3:4, :] - tgt_ref[...]
        dx2 = e * (1.0 / D_MODEL)
        acc_ref[3:4, :] += _colsum(e * e) * (0.5 / D_MODEL)
        acc_ref[0:1, :] += _colsum(dx2 * xh2)
        acc_ref[1:2, :] += _colsum(dx2)
        dr2 = _ln_bwd(dx2 * lnv_ref[2:3, :], xh2, rstd2)
        dr2_ref[...] = dr2
        acc_ref[2:3, :] += _colsum(dr2 * yf)

    tok = lambda w: pl.BlockSpec((tb, w), lambda i: (i, 0))
    return pl.pallas_call(
        body, name="f5_ffn", grid=(t // tb,),
        in_specs=[tok(D_MODEL), tok(D_MODEL), _full(modv), _full(lnv), _full(fdw), ANY, ANY],
        out_specs=[tok(D_MODEL), tok(fw), tok(D_MODEL), tok(FFN_HIDDEN), pl.BlockSpec((8, D_MODEL), lambda i: (0, 0))],
        out_shape=[jax.ShapeDtypeStruct((t, D_MODEL), F32), jax.ShapeDtypeStruct((t, fw), BF16),
                   jax.ShapeDtypeStruct((t, D_MODEL), BF16), jax.ShapeDtypeStruct((t, FFN_HIDDEN), BF16),
                   jax.ShapeDtypeStruct((8, D_MODEL), F32)],
        scratch_shapes=[pltpu.VMEM(w_up.shape, BF16), pltpu.VMEM(w_down.shape, BF16),
                        pltpu.VMEM((tb + 8, fw), F32), pltpu.SemaphoreType.DMA((2,))],
        compiler_params=_params(("arbitrary",)),
    )(r1, tgt, modv, lnv, fdw, w_up, w_down)


def _b1a_ffn_down(dr2, up, modv, fdw, w_down, tb):
    t = dr2.shape[0]
    fw = 2 * FFN_HIDDEN
    nt = t // tb
    hb = tb // 16

    def body(dr2_ref, up_ref, halo_ref, modv_ref, fdw_ref, wdn_hbm, dup_ref, dyf_ref, acc_ref, wdn_v, dbuf, next8, sem):
        i = pl.program_id(0)
        ti = nt - 1 - i

        @pl.when(i == 0)
        def _():
            _load_once(wdn_hbm, wdn_v, sem)
            acc_ref[...] = jnp.zeros_like(acc_ref)
            next8[...] = jnp.zeros_like(next8)

        dyf = (modv_ref[5:6, :] * dr2_ref[...]).astype(BF16)
        dyf_ref[...] = dyf
        down, upm = _shift_mats(tb)

        def taps(cs):
            x = up_ref[:, cs]
            prev8 = jnp.where(ti == 0, 0.0, halo_ref[:, cs].astype(F32)[8:16, :])
            u1 = _dot(down[0], x) + _head_rows(prev8, 1, tb)
            u2 = _dot(down[1], x) + _head_rows(prev8, 2, tb)
            return u2, u1, x.astype(F32)

        def conv(cs, u):
            return fdw_ref[0:1, cs] * u[0] + fdw_ref[1:2, cs] * u[1] + fdw_ref[2:3, cs] * u[2] + fdw_ref[3:4, cs]

        def tap_grads(cs, d, u):
            for k in range(FFN_KERNEL):
                acc_ref[k:k + 1, cs] += _colsum(d * u[k])
            acc_ref[3:4, cs] += _colsum(d)

        for j in range(FFN_HIDDEN // FFN_COLS):
            ca = slice(j * FFN_COLS, (j + 1) * FFN_COLS)
            cv = slice(FFN_HIDDEN + j * FFN_COLS, FFN_HIDDEN + (j + 1) * FFN_COLS)
            ua = taps(ca)
            uv = taps(cv)
            v = conv(cv, uv)
            dz = _dot_nt(dyf, wdn_v[ca, :])
            ga, dga = _gelu_and_grad(conv(ca, ua))
            da = dz * v * dga
            dv = dz * ga
            dbuf[:, ca] = da
            dbuf[:, cv] = dv
            tap_grads(ca, da, ua)
            tap_grads(cv, dv, uv)
        for j in range(fw // FFN_COLS):
            cs = slice(j * FFN_COLS, (j + 1) * FFN_COLS)
            d0 = dbuf[:, cs]
            db = d0.astype(BF16)
            d1 = _dot(upm[0], db) + _tail_rows(next8[:, cs], 1, tb)
            d2 = _dot(upm[1], db) + _tail_rows(next8[:, cs], 2, tb)
            dup = fdw_ref[2:3, cs] * d0 + fdw_ref[1:2, cs] * d1 + fdw_ref[0:1, cs] * d2
            dup_ref[:, cs] = dup.astype(BF16)
        next8[...] = dbuf[0:8, :]

    rtok = lambda w: pl.BlockSpec((tb, w), lambda i: (nt - 1 - i, 0))
    return pl.pallas_call(
        body, name="b1a_ffn_down", grid=(nt,),
        in_specs=[rtok(D_MODEL), rtok(fw),
                  pl.BlockSpec((16, fw), lambda i: (jnp.maximum((nt - 1 - i) * hb - 1, 0), 0)),
                  _full(modv), _full(fdw), ANY],
        out_specs=[rtok(fw), rtok(D_MODEL), pl.BlockSpec((8, fw), lambda i: (0, 0))],
        out_shape=[jax.ShapeDtypeStruct((t, fw), BF16), jax.ShapeDtypeStruct((t, D_MODEL), BF16),
                   jax.ShapeDtypeStruct((8, fw), F32)],
        scratch_shapes=[pltpu.VMEM(w_down.shape, BF16), pltpu.VMEM((tb, fw), F32), pltpu.VMEM((8, fw), F32),
                        pltpu.SemaphoreType.DMA],
        compiler_params=_params(("arbitrary",)),
    )(dr2, up, up, modv, fdw, w_down)


def _b1b_ffn_up(dup, dr2, r1, modv, lnv, w_up, tb):
    t = dr2.shape[0]
    fw = 2 * FFN_HIDDEN

    def body(dup_ref, dr2_ref, r1_ref, modv_ref, lnv_ref, wup_hbm, dr1_ref, acc_ref, wup_v, sem):
        @pl.when(pl.program_id(0) == 0)
        def _():
            _load_once(wup_hbm, wup_v, sem)
            acc_ref[...] = jnp.zeros_like(acc_ref)

        xh1, rstd1 = _ln_stats(r1_ref[...])
        x1 = xh1 * lnv_ref[0:1, :] + lnv_ref[1:2, :]
        xn2, rstd2 = _ln_stats(x1)
        dh2 = _dot_nt(dup_ref[...], wup_v[...])
        acc_ref[0:1, :] += _colsum(dh2 * xn2)
        acc_ref[1:2, :] += _colsum(dh2)
        dx1 = _ln_bwd(dh2 * (1.0 + modv_ref[4:5, :]), xn2, rstd2) + ALPHA * dr2_ref[...]
        acc_ref[2:3, :] += _colsum(dx1 * xh1)
        acc_ref[3:4, :] += _colsum(dx1)
        dr1_ref[...] = _ln_bwd(dx1 * lnv_ref[0:1, :], xh1, rstd1)

    tok = lambda w: pl.BlockSpec((tb, w), lambda i: (i, 0))
    return pl.pallas_call(
        body, name="b1b_ffn_up", grid=(t // tb,),
        in_specs=[tok(fw), tok(D_MODEL), tok(D_MODEL), _full(modv), _full(lnv), ANY],
        out_specs=[tok(D_MODEL), pl.BlockSpec((8, D_MODEL), lambda i: (0, 0))],
        out_shape=[jax.ShapeDtypeStruct((t, D_MODEL), F32), jax.ShapeDtypeStruct((8, D_MODEL), F32)],
        scratch_shapes=[pltpu.VMEM(w_up.shape, BF16), pltpu.SemaphoreType.DMA],
        compiler_params=_params(("arbitrary",)),
    )(dup, dr2, r1, modv, lnv, w_up)


def _b2_mixer(dr1, ys4, prest, ya, yb, ycv, vc, merged, modv, cvv, cw32, w_a, w_b, w_pw, w_out, tb):
    t = dr1.shape[0]
    nt = t // tb
    hb = tb // CONV_HALO
    cwd = CONV_WIDTH

    def body(dr1_ref, ys_ref, pr_ref, halo_ref, ya_ref, yb_ref, ycv_ref, vc_ref, mg_ref, modv_ref, cvv_ref, cw_ref,
             wa_ref, wb_ref, wpw_ref, wout_ref,
             dys_ref, dpr_ref, dya_ref, dyb_ref, dycv_ref, dym_ref, acc_a, acc_b, acc_w, vbuf, dvbuf, vrot, dvrot):
        i = pl.program_id(0)
        ti = nt - 1 - i

        @pl.when(i == 0)
        def _():
            acc_a[...] = jnp.zeros_like(acc_a)
            acc_b[...] = jnp.zeros_like(acc_b)
            acc_w[...] = jnp.zeros_like(acc_w)
            dvbuf[pl.ds(tb, CONV_HALO), :] = jnp.zeros((CONV_HALO, cwd), F32)

        dr1 = dr1_ref[...]
        dym = (modv_ref[2:3, :] * dr1).astype(BF16)
        dym_ref[...] = dym
        ym = _dot(mg_ref[...], wout_ref[...])
        acc_a[0:1, :] += _colsum(dr1 * ym)
        dmg = _dot_nt(dym, wout_ref[...])
        sgs = _sig(pr_ref[:, 2 * cwd:2 * cwd + D_MODEL])
        sgc = _sig(pr_ref[:, 2 * cwd + D_MODEL:])
        ya_v = ya_ref[...].astype(F32)
        syb = _sig(yb_ref[...].astype(F32))
        ycv_v = ycv_ref[...].astype(F32)
        dpr_ref[:, 2 * cwd:2 * cwd + D_MODEL] = (dmg * (ya_v * syb) * sgs * (1.0 - sgs)).astype(BF16)
        dpr_ref[:, 2 * cwd + D_MODEL:] = (dmg * ycv_v * sgc * (1.0 - sgc)).astype(BF16)
        dyssm = dmg * sgs
        dya = (dyssm * syb).astype(BF16)
        dyb = (dyssm * ya_v * syb * (1.0 - syb)).astype(BF16)
        dya_ref[...] = dya
        dyb_ref[...] = dyb
        dyg = _dot_nt(dya, wa_ref[...]) + _dot_nt(dyb, wb_ref[...])
        ys = jnp.concatenate([ys_ref[b] for b in range(N_LANE_BLOCKS)], axis=-1)
        dys = dyg * _dgelu(ys)
        for b in range(N_LANE_BLOCKS):
            dys_ref[b] = dys[:, b * LANES:(b + 1) * LANES].astype(BF16)
        dycv = (dmg * sgc).astype(BF16)
        dycv_ref[...] = dycv
        dvs = _dot_nt(dycv, wpw_ref[...])
        xh, rstd = _ln_stats(vc_ref[...])
        vl = xh * cvv_ref[1:2, :] + cvv_ref[2:3, :]
        s = _sig(vl)
        dvl = dvs * s * (1.0 + vl * (1.0 - s))
        acc_b[1:2, :] += _colsum(dvl * xh)
        acc_b[2:3, :] += _colsum(dvl)
        dvc = _ln_bwd(dvl * cvv_ref[1:2, :], xh, rstd)
        acc_b[0:1, :] += _colsum(dvc)
        hv = halo_ref[:, 0:cwd] * _sig(halo_ref[:, cwd:2 * cwd])
        vbuf[0:CONV_HALO, :] = jnp.where(ti == 0, 0.0, hv)
        cva = pr_ref[:, 0:cwd]
        scg = _sig(pr_ref[:, cwd:2 * cwd])
        vbuf[CONV_HALO:, :] = cva * scg
        dvbuf[0:tb, :] = dvc
        _fill_rotations(vbuf, vrot, tb + CONV_HALO - 8)
        _fill_rotations(dvbuf, dvrot, tb + CONV_HALO - 8)
        dv = jnp.zeros((tb, cwd), F32)
        for k in range(CONV_KERNEL):
            dv += _rows_at(dvbuf, dvrot, CONV_KERNEL - 1 - k, tb) * cw_ref[k:k + 1, :]
            acc_w[k:k + 1, :] += _colsum(dvc * _rows_at(vbuf, vrot, CONV_HALO - CONV_KERNEL + 1 + k, tb))
        dvbuf[pl.ds(tb, CONV_HALO), :] = dvbuf[0:CONV_HALO, :]
        dpr_ref[:, 0:cwd] = (dv * scg).astype(BF16)
        dpr_ref[:, cwd:2 * cwd] = (dv * cva * scg * (1.0 - scg)).astype(BF16)

    rtok = lambda w: pl.BlockSpec((tb, w), lambda i: (nt - 1 - i, 0))
    r4 = pl.BlockSpec((N_LANE_BLOCKS, tb, LANES), lambda i: (0, nt - 1 - i, 0))
    pw = prest.shape[1]
    return pl.pallas_call(
        body, name="b2_mixer", grid=(nt,),
        in_specs=[rtok(D_MODEL), r4, rtok(pw),
                  pl.BlockSpec((CONV_HALO, 2 * cwd), lambda i: (jnp.maximum((nt - 1 - i) * hb - 1, 0), 0)),
                  rtok(D_MODEL), rtok(D_MODEL), rtok(D_MODEL), rtok(cwd), rtok(D_MODEL),
                  _full(modv), _full(cvv), _full(cw32), _full(w_a), _full(w_b), _full(w_pw), _full(w_out)],
        out_specs=[r4, rtok(pw), rtok(D_MODEL), rtok(D_MODEL), rtok(D_MODEL), rtok(D_MODEL),
                   pl.BlockSpec((8, D_MODEL), lambda i: (0, 0)), pl.BlockSpec((8, cwd), lambda i: (0, 0)),
                   pl.BlockSpec((CONV_HALO, cwd), lambda i: (0, 0))],
        out_shape=[jax.ShapeDtypeStruct((N_LANE_BLOCKS, t, LANES), BF16), jax.ShapeDtypeStruct((t, pw), BF16),
                   jax.ShapeDtypeStruct((t, D_MODEL), BF16), jax.ShapeDtypeStruct((t, D_MODEL), BF16),
                   jax.ShapeDtypeStruct((t, D_MODEL), BF16), jax.ShapeDtypeStruct((t, D_MODEL), BF16),
                   jax.ShapeDtypeStruct((8, D_MODEL), F32), jax.ShapeDtypeStruct((8, cwd), F32),
                   jax.ShapeDtypeStruct((CONV_HALO, cwd), F32)],
        scratch_shapes=[pltpu.VMEM((tb + CONV_HALO, cwd), F32), pltpu.VMEM((tb + CONV_HALO, cwd), F32),
                        pltpu.VMEM((7, tb + CONV_HALO - 8, cwd), F32), pltpu.VMEM((7, tb + CONV_HALO - 8, cwd), F32)],
        compiler_params=_params(("arbitrary",)),
    )(dr1, ys4, prest, prest, ya, yb, ycv, vc, merged, modv, cvv, cw32, w_a, w_b, w_pw, w_out)


def _s5c_state_bwd(dy2, sg_r, sg_i, a8, hr, hi):
    nb, nc, _ = dy2.shape
    sc = STATE_COLS

    def body(dy_ref, sgr_ref, sgi_ref, a_ref, hr_ref, hi_ref, dxr_ref, dxi_ref, da_ref, dsgr_ref, dsgi_ref,
             g_s, lr_s, li_s, xr_s, xi_s):
        dy = dy_ref[0]
        _build_maps(sgr_ref, g_s)
        lr_s[...] = _dot(dy, g_s[...])
        _build_maps(sgi_ref, g_s)
        li_s[...] = _dot(dy, g_s[...])
        ar = a_ref[0, 0:1, :]
        ai = a_ref[0, 1:2, :]

        def step(k, carry):
            pr, pi, dar, dai = carry
            c = nc - 1 - k
            xr_s[pl.ds(c, 1), :] = pr
            xi_s[pl.ds(c, 1), :] = pi
            h_r = hr_ref[0, pl.ds(c, 1), :]
            h_i = hi_ref[0, pl.ds(c, 1), :]
            dar = dar + pr * h_r + pi * h_i
            dai = dai - pr * h_i + pi * h_r
            nr = lr_s[pl.ds(c, 1), :] + ar * pr + ai * pi
            ni = li_s[pl.ds(c, 1), :] - ai * pr + ar * pi
            return nr, ni, dar, dai

        z = jnp.zeros((1, sc), F32)
        _, _, dar, dai = lax.fori_loop(0, nc, step, (z, z, z, z))
        da_ref[0] = jnp.concatenate([dar, dai, jnp.zeros((6, sc), F32)], axis=0)
        dxr_ref[0] = xr_s[...].astype(BF16)
        dxi_ref[0] = xi_s[...].astype(BF16)
        for h_ref, o_ref in ((hr_ref, dsgr_ref), (hi_ref, dsgi_ref)):
            hb = h_ref[0].astype(BF16)
            for j in range(SSM_CHUNK):
                o_ref[j] = _fold(_dot_tn(dy[:, j * LANES:(j + 1) * LANES], hb), SSM_STATE)

    blk = lambda r, c: pl.BlockSpec((1, r, c), lambda b: (b, 0, 0))
    return pl.pallas_call(
        body, name="s5c_state_bwd", grid=(nb,),
        in_specs=[blk(nc, CHUNK_COLS), _cblk(SSM_STATE), _cblk(SSM_STATE), blk(8, sc), blk(nc, sc), blk(nc, sc)],
        out_specs=[blk(nc, sc), blk(nc, sc), blk(8, sc), _cblk(SSM_STATE), _cblk(SSM_STATE)],
        out_shape=[jax.ShapeDtypeStruct((nb, nc, sc), BF16), jax.ShapeDtypeStruct((nb, nc, sc), BF16),
                   jax.ShapeDtypeStruct((nb, 8, sc), F32),
                   jax.ShapeDtypeStruct((SSM_CHUNK, SSM_WIDTH, SSM_STATE), F32),
                   jax.ShapeDtypeStruct((SSM_CHUNK, SSM_WIDTH, SSM_STATE), F32)],
        scratch_shapes=[pltpu.VMEM((CHUNK_COLS, sc), BF16)] + [pltpu.VMEM((nc, sc), F32)] * 4,
        compiler_params=_params(("arbitrary",)),
    )(dy2, sg_r, sg_i, a8, hr, hi)


def _s5d_input_bwd(dy2, u2, kc, sw_r, sw_i, dxr, dxi):
    nb, nc, _ = dy2.shape
    sc = STATE_COLS

    def body(dy_ref, u_ref, kc_ref, swr_ref, swi_ref, dxr_ref, dxi_ref, du_ref, dkc_ref, dswr_ref, dswi_ref,
             tm_s, w_s, dk_s):
        dy = dy_ref[0]
        u = u_ref[0]
        _build_toeplitz(kc_ref, tm_s)
        du = _dot_nt(dy, tm_s[...])
        _build_maps(swr_ref, w_s)
        du += _dot_nt(dxr_ref[0], w_s[...])
        _build_maps(swi_ref, w_s)
        du += _dot_nt(dxi_ref[0], w_s[...])
        du_ref[0] = du.astype(BF16)
        dk_s[...] = jnp.zeros_like(dk_s)
        for ji in range(SSM_CHUNK):
            uj = u[:, ji * LANES:(ji + 1) * LANES]
            rows = _dot_tn(uj, dy)
            for jo in range(ji, SSM_CHUNK):
                dk_s[jo - ji] += rows[:, jo * LANES:(jo + 1) * LANES]
            dswr_ref[ji] = _fold(_dot_tn(uj, dxr_ref[0]), SSM_STATE)
            dswi_ref[ji] = _fold(_dot_tn(uj, dxi_ref[0]), SSM_STATE)
        for d in range(SSM_CHUNK):
            dkc_ref[d] = _fold(dk_s[d], SSM_GROUP)

    blk = lambda r, c: pl.BlockSpec((1, r, c), lambda b: (b, 0, 0))
    return pl.pallas_call(
        body, name="s5d_input_bwd", grid=(nb,),
        in_specs=[blk(nc, CHUNK_COLS), blk(nc, CHUNK_COLS), _cblk(SSM_GROUP), _cblk(SSM_STATE), _cblk(SSM_STATE),
                  blk(nc, sc), blk(nc, sc)],
        out_specs=[blk(nc, CHUNK_COLS), _cblk(SSM_GROUP), _cblk(SSM_STATE), _cblk(SSM_STATE)],
        out_shape=[jax.ShapeDtypeStruct((nb, nc, CHUNK_COLS), BF16),
                   jax.ShapeDtypeStruct((SSM_CHUNK, SSM_WIDTH, SSM_GROUP), F32),
                   jax.ShapeDtypeStruct((SSM_CHUNK, SSM_WIDTH, SSM_STATE), F32),
                   jax.ShapeDtypeStruct((SSM_CHUNK, SSM_WIDTH, SSM_STATE), F32)],
        scratch_shapes=[pltpu.VMEM((CHUNK_COLS, CHUNK_COLS), BF16), pltpu.VMEM((CHUNK_COLS, sc), BF16),
                        pltpu.VMEM((SSM_CHUNK, LANES, LANES), F32)],
        compiler_params=_params(("arbitrary",)),
    )(dy2, u2, kc, sw_r, sw_i, dxr, dxi)


def _b3_inproj(x, dr1, du4, dprest, modv, w_in, tb):
    t = x.shape[0]
    pw = IN_PROJ_WIDTH - SSM_WIDTH

    def body(x_ref, dr1_ref, du_ref, dpr_ref, modv_ref, w_hbm, gx_ref, dp_ref, acc_ref, accb_ref, w_v, sem):
        @pl.when(pl.program_id(0) == 0)
        def _():
            _load_once(w_hbm, w_v, sem)
            acc_ref[...] = jnp.zeros_like(acc_ref)
            accb_ref[...] = jnp.zeros_like(accb_ref)

        du = jnp.concatenate([du_ref[b] for b in range(N_LANE_BLOCKS)], axis=-1)
        dpr = dpr_ref[...]
        dp_ref[:, 0:SSM_WIDTH] = du
        dp_ref[:, SSM_WIDTH:] = dpr
        accb_ref[0:1, 0:SSM_WIDTH] += _colsum(du.astype(F32))
        accb_ref[0:1, SSM_WIDTH:] += _colsum(dpr.astype(F32))
        dh = _dot_nt(du, w_v[:, 0:SSM_WIDTH]) + _dot_nt(dpr, w_v[:, SSM_WIDTH:])
        xn, rstd = _ln_stats(x_ref[...])
        acc_ref[0:1, :] += _colsum(dh * xn)
        acc_ref[1:2, :] += _colsum(dh)
        gx_ref[...] = _ln_bwd(dh * (1.0 + modv_ref[1:2, :]), xn, rstd) + ALPHA * dr1_ref[...]

    tok = lambda w: pl.BlockSpec((tb, w), lambda i: (i, 0))
    return pl.pallas_call(
        body, name="b3_inproj", grid=(t // tb,),
        in_specs=[tok(D_MODEL), tok(D_MODEL), pl.BlockSpec((N_LANE_BLOCKS, tb, LANES), lambda i: (0, i, 0)), tok(pw),
                  _full(modv), ANY],
        out_specs=[tok(D_MODEL), tok(IN_PROJ_WIDTH), pl.BlockSpec((8, D_MODEL), lambda i: (0, 0)),
                   pl.BlockSpec((8, IN_PROJ_WIDTH), lambda i: (0, 0))],
        out_shape=[jax.ShapeDtypeStruct((t, D_MODEL), F32), jax.ShapeDtypeStruct((t, IN_PROJ_WIDTH), BF16),
                   jax.ShapeDtypeStruct((8, D_MODEL), F32), jax.ShapeDtypeStruct((8, IN_PROJ_WIDTH), F32)],
        scratch_shapes=[pltpu.VMEM(w_in.shape, BF16), pltpu.SemaphoreType.DMA],
        compiler_params=_params(("arbitrary",)),
    )(x, dr1, du4, dprest, modv, w_in)


TN_ROWS = 2048


def _tn_matmul(name, a, b, tm, tn):
    t, m = a.shape
    n = b.shape[1]
    tt = min(TN_ROWS, t)
    nk = t // tt

    def body(a_ref, b_ref, o_ref, acc):
        k = pl.program_id(2)

        @pl.when(k == 0)
        def _():
            acc[...] = jnp.zeros_like(acc)

        acc[...] += _dot_tn(a_ref[...], b_ref[...])

        @pl.when(k == nk - 1)
        def _():
            o_ref[...] = acc[...].astype(BF16)

    return pl.pallas_call(
        body, name=name, grid=(m // tm, n // tn, nk),
        in_specs=[pl.BlockSpec((tt, tm), lambda i, j, k: (k, i)), pl.BlockSpec((tt, tn), lambda i, j, k: (k, j))],
        out_specs=pl.BlockSpec((tm, tn), lambda i, j, k: (i, j)),
        out_shape=jax.ShapeDtypeStruct((m, n), BF16),
        scratch_shapes=[pltpu.VMEM((tm, tn), F32)],
        compiler_params=_params(("arbitrary", "arbitrary", "arbitrary")),
    )(a, b)


def _local_step(x, tgt, modv, wb, sp, tb=256):
    t = x.shape[0]
    nc = t // SSM_CHUNK
    row8 = lambda rows, w: jnp.concatenate([r.reshape(1, w) for r in rows] + [jnp.zeros((8 - len(rows), w), F32)], axis=0)
    lnv = row8([sp["ln1_g"], sp["ln1_b"], sp["ln2_g"], sp["ln2_b"]], D_MODEL)
    cvv = row8([sp["cv_dw_b"], sp["cv_ln_g"], sp["cv_ln_b"]], CONV_WIDTH)
    cw32 = jnp.concatenate([sp["cv_dw_w"].reshape(CONV_KERNEL, CONV_WIDTH), jnp.zeros((1, CONV_WIDTH), F32)], axis=0)
    fdw = row8(list(sp["ffn_dw_w"].reshape(FFN_KERNEL, 2 * FFN_HIDDEN)) + [sp["ffn_dw_b"]], 2 * FFN_HIDDEN)
    b_in = sp["b_in"].reshape(1, IN_PROJ_WIDTH)
    ssm = tuple(sp[k] for k in ("ssm_lambda_re", "ssm_lambda_im", "ssm_log_dt", "ssm_b_re", "ssm_b_im", "ssm_c_re",
                                "ssm_c_im", "ssm_d"))
    (kc, sw_r, sw_i, sg_r, sg_i, a), ssm_vjp = jax.vjp(_s5_build, *ssm)
    a8 = jnp.concatenate([a, jnp.zeros((N_LANE_BLOCKS, 6, STATE_COLS), F32)], axis=1)

    u4, prest, h1 = _f1_inproj(x, modv, b_in, wb["w_in"], tb)
    u2 = u4.reshape(N_LANE_BLOCKS, nc, CHUNK_COLS)
    hr, hi = _s5a_state(u2, sw_r, sw_i, a8)
    ys4 = _s5b_out(u2, kc, sg_r, sg_i, hr, hi).reshape(N_LANE_BLOCKS, t, LANES)
    r1, ya, yb, ycv, vc, yg, vs, merged = _f4_mixer(ys4, prest, x, modv, cvv, cw32, wb["ssm_glu_w_a"],
                                                    wb["ssm_glu_w_b"], wb["cv_w_pw"], wb["w_out"], tb)
    dr2, up, h2, z, acc5 = _f5_ffn(r1, tgt, modv, lnv, fdw, wb["ffn_w_up"], wb["ffn_w_down"], tb)
    dup, dyf, acc1a = _b1a_ffn_down(dr2, up, modv, fdw, wb["ffn_w_down"], tb)
    dr1, acc1b = _b1b_ffn_up(dup, dr2, r1, modv, lnv, wb["ffn_w_up"], tb)
    dys4, dprest, dya, dyb, dycv, dym, acc2a, acc2b, acc2w = _b2_mixer(
        dr1, ys4, prest, ya, yb, ycv, vc, merged, modv, cvv, cw32, wb["ssm_glu_w_a"], wb["ssm_glu_w_b"],
        wb["cv_w_pw"], wb["w_out"], tb)
    dy2 = dys4.reshape(N_LANE_BLOCKS, nc, CHUNK_COLS)
    dxr, dxi, da8, dsg_r, dsg_i = _s5c_state_bwd(dy2, sg_r, sg_i, a8, hr, hi)
    du2, dkc, dsw_r, dsw_i = _s5d_input_bwd(dy2, u2, kc, sw_r, sw_i, dxr, dxi)
    dssm = ssm_vjp((dkc, dsw_r, dsw_i, dsg_r, dsg_i, da8[:, 0:2, :]))
    gx, dp, acc3, acc3b = _b3_inproj(x, dr1, du2.reshape(N_LANE_BLOCKS, t, LANES), dprest, modv, wb["w_in"], tb)

    dbig = {
        "w_in": _tn_matmul("dw_in", h1, dp, 1024, 896),
        "ssm_glu_w_a": _tn_matmul("dw_glu_a", yg, dya, 512, 1024),
        "ssm_glu_w_b": _tn_matmul("dw_glu_b", yg, dyb, 512, 1024),
        "cv_w_pw": _tn_matmul("dw_pw", vs, dycv, 512, 1024),
        "w_out": _tn_matmul("dw_out", merged, dym, 1024, 1024),
        "ffn_w_up": _tn_matmul("dw_up", h2, dup, 1024, FFN_COLS),
        "ffn_w_down": _tn_matmul("dw_down", z, dyf, FFN_COLS, 1024),
    }
    dmod = jnp.concatenate([acc3[1], acc3[0], acc2a[0], acc1b[1], acc1b[0], acc5[2]])
    small = {
        "dmod": dmod, "b_in": acc3b[0],
        "ssm_lambda_re": dssm[0], "ssm_lambda_im": dssm[1], "ssm_log_dt": dssm[2], "ssm_b_re": dssm[3],
        "ssm_b_im": dssm[4], "ssm_c_re": dssm[5], "ssm_c_im": dssm[6], "ssm_d": dssm[7],
        "cv_dw_w": acc2w[0:CONV_KERNEL], "cv_dw_b": acc2b[0], "cv_ln_g": acc2b[1], "cv_ln_b": acc2b[2],
        "ln1_g": acc1b[2], "ln1_b": acc1b[3], "ffn_dw_w": acc1a[0:FFN_KERNEL], "ffn_dw_b": acc1a[3],
        "ln2_g": acc5[0], "ln2_b": acc5[1], "loss": jnp.sum(acc5[3]).reshape(1),
    }
    return gx, dbig, small


def _place():
    x, y, c = lax.axis_index("x"), lax.axis_index("y"), lax.axis_index("c")
    chips = [(1 - x, y), (x, 1 - y), (1 - x, 1 - y)]
    return x, y, c, chips


def _allgather(name, shard):
    m_per, n = shard.shape

    def body(x_ref, out_ref, send_sems, recv_sems, local_sem):
        x, y, c, chips = _place()
        me, sibling = (x, y, c), (x, y, 1 - c)

        def rows(px, py, pc):
            return out_ref.at[pl.ds((4 * px + 2 * py + pc) * m_per, m_per), :]

        def copy(k, block, to, src=None):
            return pltpu.make_async_remote_copy(
                src_ref=rows(*block) if src is None else src, dst_ref=rows(*block),
                send_sem=send_sems.at[k], recv_sem=recv_sems.at[k], device_id=to, device_id_type=MESH)

        mine = pltpu.make_async_copy(x_ref, rows(*me), local_sem)
        mine.start()
        first = [copy(0, me, sibling, src=x_ref)]
        first += [copy(1 + j, me, (*chip, c), src=x_ref) for j, chip in enumerate(chips)]
        for cp in first:
            cp.start()
        passed = [copy(4 + j, (*chip, c), sibling) for j, chip in enumerate(chips)]
        for j, chip in enumerate(chips):
            copy(1 + j, (*chip, c), me).wait_recv()
            passed[j].start()
        copy(0, sibling, me).wait_recv()
        for j, chip in enumerate(chips):
            copy(4 + j, (*chip, 1 - c), me).wait_recv()
        for cp in first + passed:
            cp.wait_send()
        mine.wait()

    return pl.pallas_call(
        body, name=name,
        out_shape=jax.ShapeDtypeStruct((N_DEV * m_per, n), shard.dtype),
        in_specs=[pl.BlockSpec(memory_space=pltpu.VMEM)],
        out_specs=pl.BlockSpec(memory_space=pltpu.VMEM),
        scratch_shapes=[pltpu.SemaphoreType.DMA((7,)), pltpu.SemaphoreType.DMA((7,)), pltpu.SemaphoreType.DMA],
        compiler_params=_params(),
    )(shard)


def _piece(kind, shape):
    r, cc = shape
    return (r // 2, cc // N_CHIP) if kind == "col" else (r // (2 * N_CHIP), cc)


def _piece_at(ref, kind, shape, s, k):
    pr, pc = _piece(kind, shape)
    if kind == "col":
        return ref.at[pl.ds(k * pr, pr), pl.ds(pl.multiple_of(s * pc, LANES), pc)]
    return ref.at[pl.ds(pl.multiple_of((2 * s + k) * pr, 16), pr), :]


def _add_rows(pr):
    return 64 if pr % 64 == 0 else 16


def _gather_weights(shards):
    nm = len(BIG)

    def body(*refs):
        ins, outs = refs[:nm], refs[nm:2 * nm]
        stage = refs[2 * nm:3 * nm]
        send, recv, fsend, frecv, lsem = refs[3 * nm:]
        x, y, c, chips = _place()
        s_me = 2 * x + y
        sibling = (x, y, 1 - c)
        pend = []
        for m, (_, kind, shape) in enumerate(BIG):
            stage[m][...] = ins[m][...].astype(BF16)
        for m, (_, kind, shape) in enumerate(BIG):
            pr, pc = _piece(kind, shape)
            for k in range(2):
                cp = pltpu.make_async_copy(stage[m].at[pl.ds(k * pr, pr), :], _piece_at(outs[m], kind, shape, s_me, k),
                                           lsem.at[m, k])
                cp.start()
                pend.append(cp.wait)
            for j, chip in enumerate(chips):
                cp = pltpu.make_async_remote_copy(
                    src_ref=stage[m].at[pl.ds(pl.multiple_of(c * pr, 16), pr), :],
                    dst_ref=_piece_at(outs[m], kind, shape, s_me, c),
                    send_sem=send.at[m, j], recv_sem=recv.at[m, j], device_id=(*chip, c), device_id_type=MESH)
                cp.start()
                pend.append(cp.wait_send)
        for m, (_, kind, shape) in enumerate(BIG):
            for j, (cx, cy) in enumerate(chips):
                got = _piece_at(outs[m], kind, shape, 2 * cx + cy, c)
                pltpu.make_async_remote_copy(src_ref=got, dst_ref=got, send_sem=send.at[m, j], recv_sem=recv.at[m, j],
                                             device_id=(cx, cy, c), device_id_type=MESH).wait_recv()
                cp = pltpu.make_async_remote_copy(src_ref=got, dst_ref=got, send_sem=fsend.at[m, j],
                                                  recv_sem=frecv.at[m, j], device_id=sibling, device_id_type=MESH)
                cp.start()
                pend.append(cp.wait_send)
        for m, (_, kind, shape) in enumerate(BIG):
            for j, (cx, cy) in enumerate(chips):
                got = _piece_at(outs[m], kind, shape, 2 * cx + cy, 1 - c)
                pltpu.make_async_remote_copy(src_ref=got, dst_ref=got, send_sem=fsend.at[m, j], recv_sem=frecv.at[m, j],
                                             device_id=sibling, device_id_type=MESH).wait_recv()
        for w in pend:
            w()

    sem = lambda *s: pltpu.SemaphoreType.DMA(s)
    return pl.pallas_call(
        body, name="gather_weights",
        out_shape=[jax.ShapeDtypeStruct(shape, BF16) for _, _, shape in BIG],
        in_specs=[pl.BlockSpec(memory_space=pltpu.VMEM)] * nm,
        out_specs=[ANY] * nm,
        scratch_shapes=[pltpu.VMEM(s.shape, BF16) for s in shards] + [sem(nm, 3), sem(nm, 3), sem(nm, 3), sem(nm, 3),
                                                                         sem(nm, 2)],
        compiler_params=_params(),
    )(*shards)


def _rs1_sibling(grads):
    nm = len(BIG)

    def body(*refs):
        ins, outs = refs[:nm], refs[nm:2 * nm]
        send, recv = refs[2 * nm:]
        x, y, c, _ = _place()
        cps = []
        for m, (_, kind, shape) in enumerate(BIG):
            for s in range(N_CHIP):
                cp = pltpu.make_async_remote_copy(
                    src_ref=_piece_at(ins[m], kind, shape, s, 1 - c), dst_ref=outs[m].at[s],
                    send_sem=send.at[m, s], recv_sem=recv.at[m, s], device_id=(x, y, 1 - c), device_id_type=MESH)
                cp.start()
                cps.append(cp)
        for cp in cps:
            cp.wait()

    sem = lambda *s: pltpu.SemaphoreType.DMA(s)
    return pl.pallas_call(
        body, name="rs1_sibling",
        out_shape=[jax.ShapeDtypeStruct((N_CHIP,) + _piece(kind, shape), BF16) for _, kind, shape in BIG],
        in_specs=[ANY] * nm, out_specs=[ANY] * nm,
        scratch_shapes=[sem(nm, N_CHIP), sem(nm, N_CHIP)],
        compiler_params=_params(),
    )(*grads)


def _rs2_chips(grads, halves):
    nm = len(BIG)

    def body(*refs):
        gin, hin = refs[:nm], refs[nm:2 * nm]
        own, got = refs[2 * nm:3 * nm], refs[3 * nm:4 * nm]
        send, recv, lsem = refs[4 * nm:]
        x, y, c, chips = _place()
        s_me = 2 * x + y
        for m, (_, kind, shape) in enumerate(BIG):
            pr, pc = _piece(kind, shape)

            def scoped(a, b, m=m, kind=kind, shape=shape, pr=pr):
                loads = [pltpu.make_async_copy(_piece_at(gin[m], kind, shape, s, c), a.at[s], lsem.at[s])
                         for s in range(N_CHIP)]
                loads.append(pltpu.make_async_copy(hin[m], b, lsem.at[N_CHIP]))
                for cp in loads:
                    cp.start()
                for cp in loads:
                    cp.wait()
                step = _add_rows(pr)
                for s in range(N_CHIP):
                    def add(i, _, s=s):
                        r = pl.ds(pl.multiple_of(i * step, 16), step)
                        a[s, r, :] = (a[s, r, :].astype(F32) + b[s, r, :].astype(F32)).astype(BF16)
                        return 0

                    lax.fori_loop(0, pr // step, add, 0)
                waits = []
                for j, (cx, cy) in enumerate(chips):
                    cp = pltpu.make_async_remote_copy(src_ref=a.at[2 * cx + cy], dst_ref=got[m].at[j], send_sem=send.at[m, j],
                                                      recv_sem=recv.at[m, j], device_id=(cx, cy, c), device_id_type=MESH)
                    cp.start()
                    waits.append(cp.wait_send)
                cp = pltpu.make_async_copy(a.at[s_me], own[m], lsem.at[N_CHIP + 1])
                cp.start()
                waits.append(cp.wait)
                for w in waits:
                    w()

            pl.run_scoped(scoped, pltpu.VMEM((N_CHIP, pr, pc), BF16), pltpu.VMEM((N_CHIP, pr, pc), BF16))
        for m in range(nm):
            for j, (cx, cy) in enumerate(chips):
                pltpu.make_async_remote_copy(src_ref=got[m].at[j], dst_ref=got[m].at[j], send_sem=send.at[m, j],
                                             recv_sem=recv.at[m, j], device_id=(cx, cy, c), device_id_type=MESH).wait_recv()

    sem = lambda *s: pltpu.SemaphoreType.DMA(s)
    pieces = [_piece(kind, shape) for _, kind, shape in BIG]
    return pl.pallas_call(
        body, name="rs2_chips",
        out_shape=[jax.ShapeDtypeStruct(p, BF16) for p in pieces] + [jax.ShapeDtypeStruct((3,) + p, BF16) for p in pieces],
        in_specs=[ANY] * (2 * nm), out_specs=[ANY] * (2 * nm),
        scratch_shapes=[sem(nm, 3), sem(nm, 3), sem(N_CHIP + 2)],
        compiler_params=_params(),
    )(*grads, *halves)


def _rs3_finish(own, got):
    nm = len(BIG)

    def body(*refs):
        oin, gin = refs[:nm], refs[nm:2 * nm]
        outs = refs[2 * nm:3 * nm]
        send, recv, lsem = refs[3 * nm:]
        x, y, c, _ = _place()
        for m, (_, kind, shape) in enumerate(BIG):
            pr, pc = _piece(kind, shape)

            def scoped(a, g, f, m=m, pr=pr):
                loads = [pltpu.make_async_copy(oin[m], a, lsem.at[0]), pltpu.make_async_copy(gin[m], g, lsem.at[1])]
                for cp in loads:
                    cp.start()
                for cp in loads:
                    cp.wait()
                step = _add_rows(pr)

                def add(i, _):
                    r = pl.ds(pl.multiple_of(i * step, 16), step)
                    f[r, :] = ((a[r, :].astype(F32) + g[0, r, :].astype(F32)) + g[1, r, :].astype(F32)) + g[2, r, :].astype(F32)
                    return 0

                lax.fori_loop(0, pr // step, add, 0)
                dst = outs[m].at[pl.ds(pl.multiple_of(c * pr, 8), pr), :]
                mine = pltpu.make_async_copy(f, dst, lsem.at[2])
                mine.start()
                cp = pltpu.make_async_remote_copy(src_ref=f, dst_ref=dst, send_sem=send.at[m], recv_sem=recv.at[m],
                                                  device_id=(x, y, 1 - c), device_id_type=MESH)
                cp.start()
                cp.wait_send()
                mine.wait()

            pl.run_scoped(scoped, pltpu.VMEM((pr, pc), BF16), pltpu.VMEM((3, pr, pc), BF16), pltpu.VMEM((pr, pc), F32))
        for m, (_, kind, shape) in enumerate(BIG):
            pr, pc = _piece(kind, shape)
            dst = outs[m].at[pl.ds(pl.multiple_of((1 - c) * pr, 8), pr), :]
            pltpu.make_async_remote_copy(src_ref=dst, dst_ref=dst, send_sem=send.at[m], recv_sem=recv.at[m],
                                         device_id=(x, y, 1 - c), device_id_type=MESH).wait_recv()

    sem = lambda *s: pltpu.SemaphoreType.DMA(s)
    pieces = [_piece(kind, shape) for _, kind, shape in BIG]
    return pl.pallas_call(
        body, name="rs3_finish",
        out_shape=[jax.ShapeDtypeStruct((2 * pr, pc), F32) for pr, pc in pieces],
        in_specs=[ANY] * (2 * nm), out_specs=[ANY] * nm,
        scratch_shapes=[sem(nm), sem(nm), sem(3)],
        compiler_params=_params(),
    )(*own, *got)


def _cond_fwd(c_all, w_shard, b_shard):
    def body(c_ref, w_ref, b_ref, act_ref, mod_ref):
        cv = c_ref[...]
        act = cv * _sig(cv)
        act_ref[...] = act
        mod_ref[...] = _dot(act.astype(BF16), w_ref[...].astype(BF16)) + b_ref[...]

    return pl.pallas_call(
        body, name="cond_fwd",
        out_shape=[jax.ShapeDtypeStruct(c_all.shape, F32), jax.ShapeDtypeStruct((c_all.shape[0], w_shard.shape[1]), F32)],
        compiler_params=_params(),
    )(c_all, w_shard, b_shard)


def _cond_bwd(act_t, dmod_shard):
    k, n = act_t.shape[0], dmod_shard.shape[1]

    def body(a_ref, d_ref, o_ref):
        acc = a_ref[:, 0:1] * d_ref[0:1, :]
        for e in range(1, N_DEV):
            acc += a_ref[:, e:e + 1] * d_ref[e:e + 1, :]
        o_ref[...] = acc

    tr = 256
    return pl.pallas_call(
        body, name="cond_bwd", grid=(k // tr,),
        in_specs=[pl.BlockSpec((tr, N_DEV), lambda i: (i, 0)), _full(dmod_shard)],
        out_specs=pl.BlockSpec((tr, n), lambda i: (i, 0)),
        out_shape=jax.ShapeDtypeStruct((k, n), F32),
        compiler_params=_params(("arbitrary",)),
    )(act_t, dmod_shard)


def _sum_blocks(allp):
    def body(a_ref, o_ref):
        acc = a_ref[0:PACK_ROWS, :]
        for d in range(1, N_DEV):
            acc += a_ref[d * PACK_ROWS:(d + 1) * PACK_ROWS, :]
        o_ref[...] = acc

    return pl.pallas_call(
        body, name="sum_small", out_shape=jax.ShapeDtypeStruct((PACK_ROWS, PACK_COLS), F32), compiler_params=_params(),
    )(allp)


def _adamw(name, w, g, m, v):
    r, cc = w.shape
    tr = r
    for cand in (256, 128, 64, 32, 16, 8):
        if r % cand == 0:
            tr = cand
            break
    bc1 = 1.0 - ADAM_B1 ** ADAM_STEP
    bc2 = 1.0 - ADAM_B2 ** ADAM_STEP

    def body(w_ref, g_ref, m_ref, v_ref, d_ref, nm_ref, nv_ref):
        gv = g_ref[...]
        nm = ADAM_B1 * m_ref[...] + (1.0 - ADAM_B1) * gv
        nv = ADAM_B2 * v_ref[...] + (1.0 - ADAM_B2) * (gv * gv)
        nm_ref[...] = nm
        nv_ref[...] = nv
        d_ref[...] = -ADAM_LR * ((nm / bc1) / (jnp.sqrt(nv / bc2) + ADAM_EPS) + ADAM_WD * w_ref[...])

    spec = pl.BlockSpec((tr, cc), lambda i: (i, 0))
    return pl.pallas_call(
        body, name=name, grid=(r // tr,), in_specs=[spec] * 4, out_specs=[spec] * 3,
        out_shape=[jax.ShapeDtypeStruct((r, cc), F32)] * 3, compiler_params=_params(("arbitrary",)),
    )(w, g, m, v)


def _pack(fields, layout):
    parts = [fields[name].reshape(-1).astype(F32) if name in fields else jnp.zeros((n,), F32) for name, n in layout]
    used = sum(n for _, n in layout)
    parts.append(jnp.zeros((PACK_ROWS * PACK_COLS - used,), F32))
    return jnp.concatenate(parts).reshape(PACK_ROWS, PACK_COLS)


def _unpack(flat, layout):
    flat = flat.reshape(-1)
    out, o = {}, 0
    for name, n in layout:
        out[name] = flat[o:o + n]
        o += n
    return out


def kernel(x, c, w_cond, b_cond, w_in, b_in, ssm_lambda_re, ssm_lambda_im, ssm_log_dt, ssm_b_re, ssm_b_im, ssm_c_re, ssm_c_im, ssm_d, ssm_glu_w_a, ssm_glu_w_b, cv_dw_w, cv_dw_b, cv_ln_g, cv_ln_b, cv_w_pw, w_out, ln1_g, ln1_b, ffn_w_up, ffn_dw_w, ffn_dw_b, ffn_w_down, ln2_g, ln2_b, loss_target, m_w_cond, m_b_cond, m_w_in, m_b_in, m_ssm_lambda_re, m_ssm_lambda_im, m_ssm_log_dt, m_ssm_b_re, m_ssm_b_im, m_ssm_c_re, m_ssm_c_im, m_ssm_d, m_ssm_glu_w_a, m_ssm_glu_w_b, m_cv_dw_w, m_cv_dw_b, m_cv_ln_g, m_cv_ln_b, m_cv_w_pw, m_w_out, m_ln1_g, m_ln1_b, m_ffn_w_up, m_ffn_dw_w, m_ffn_dw_b, m_ffn_w_down, m_ln2_g, m_ln2_b, v_w_cond, v_b_cond, v_w_in, v_b_in, v_ssm_lambda_re, v_ssm_lambda_im, v_ssm_log_dt, v_ssm_b_re, v_ssm_b_im, v_ssm_c_re, v_ssm_c_im, v_ssm_d, v_ssm_glu_w_a, v_ssm_glu_w_b, v_cv_dw_w, v_cv_dw_b, v_cv_ln_g, v_cv_ln_b, v_cv_w_pw, v_w_out, v_ln1_g, v_ln1_b, v_ffn_w_up, v_ffn_dw_w, v_ffn_dw_b, v_ffn_w_down, v_ln2_g, v_ln2_b):
    given = locals()
    a = {n: given[n] for n in INPUTS}
    xi, yi, ci = lax.axis_index("x"), lax.axis_index("y"), lax.axis_index("c")
    s_me = 2 * xi + yi
    e_me = 4 * xi + 2 * yi + ci

    first = jnp.concatenate([
        jnp.concatenate([a["c"], jnp.zeros((7, D_MODEL), F32)], axis=0),
        jnp.concatenate([a["cv_dw_w"].reshape(-1), a["ffn_dw_w"].reshape(-1)]).reshape(8, D_MODEL)], axis=0)
    first_all = _allgather("gather_c", first).reshape(N_DEV, 16, D_MODEL)
    c_all = first_all[:, 0, :]
    dw_all = first_all[0::2, 8:, :].reshape(N_CHIP, 8 * D_MODEL)
    n_cv = CONV_KERNEL * CONV_WIDTH // N_CHIP
    cv_dw_full = dw_all[:, :n_cv].reshape(N_CHIP, CONV_KERNEL, CONV_WIDTH // N_CHIP).transpose(1, 0, 2) \
        .reshape(CONV_KERNEL, CONV_WIDTH)
    ffn_dw_full = dw_all[:, n_cv:].reshape(N_CHIP, FFN_KERNEL, 2 * FFN_HIDDEN // N_CHIP).transpose(1, 0, 2) \
        .reshape(FFN_KERNEL, 2 * FFN_HIDDEN)
    ncols = N_COND * D_MODEL // N_CHIP
    b_cond_shard = lax.dynamic_slice(a["b_cond"], (0, s_me * ncols), (1, ncols))
    c_act_all, modp = _cond_fwd(c_all, a["w_cond"][0], b_cond_shard)
    modp_all = _allgather("gather_mod", modp).reshape(N_DEV, N_DEV, ncols)[0::2]
    mod_e = lax.dynamic_index_in_dim(modp_all, e_me, axis=1, keepdims=False).reshape(N_COND, D_MODEL)
    modv = jnp.concatenate([mod_e, jnp.zeros((2, D_MODEL), F32)], axis=0)

    wb = dict(zip([n for n, _, _ in BIG], _gather_weights([a[n][0] for n, _, _ in BIG])))
    sp = {n: a[n][0] for n in ("b_in", "ssm_lambda_re", "ssm_lambda_im", "ssm_log_dt", "ssm_b_re", "ssm_b_im",
                               "ssm_c_re", "ssm_c_im", "ssm_d", "cv_dw_b", "cv_ln_g", "cv_ln_b", "ln1_g", "ln1_b",
                               "ffn_dw_b", "ln2_g", "ln2_b")}
    sp["cv_dw_w"] = cv_dw_full
    sp["ffn_dw_w"] = ffn_dw_full
    gx, dbig, small = _local_step(a["x"][0], a["loss_target"][0], modv, wb, sp)

    small["c_act"] = lax.dynamic_index_in_dim(c_act_all, e_me, axis=0, keepdims=False)
    packed_all = _allgather("gather_small", _pack(small, PACK))
    tot = _unpack(_sum_blocks(packed_all), PACK)
    rows = packed_all.reshape(N_DEV, PACK_ROWS * PACK_COLS)
    dmod_all = rows[:, 0:N_COND * D_MODEL]
    act_all = rows[:, N_COND * D_MODEL:(N_COND + 1) * D_MODEL]
    g_w_cond = _cond_bwd(act_all.T, lax.dynamic_slice(dmod_all, (0, s_me * ncols), (N_DEV, ncols)))

    glist = [dbig[n] for n, _, _ in BIG]
    halves = _rs1_sibling(glist)
    r2 = _rs2_chips(glist, halves)
    gsh = _rs3_finish(r2[:len(BIG)], r2[len(BIG):])

    grads = {"w_cond": g_w_cond[None], "b_cond": tot["dmod"].reshape(1, -1)}
    for (n, kind, shape), g in zip(BIG, gsh):
        grads[n] = g.reshape(a[n].shape)
    for n in ("b_in", "ssm_lambda_re", "ssm_lambda_im", "ssm_log_dt", "ssm_b_re", "ssm_b_im", "ssm_c_re", "ssm_c_im",
              "ssm_d", "cv_dw_b", "cv_ln_g", "cv_ln_b", "ln1_g", "ln1_b", "ffn_dw_b", "ln2_g", "ln2_b"):
        grads[n] = tot[n].reshape(a[n].shape)
    wcv = CONV_WIDTH // N_CHIP
    grads["cv_dw_w"] = lax.dynamic_slice(tot["cv_dw_w"].reshape(CONV_KERNEL, CONV_WIDTH), (0, s_me * wcv),
                                         (CONV_KERNEL, wcv)).reshape(a["cv_dw_w"].shape)
    wff = 2 * FFN_HIDDEN // N_CHIP
    grads["ffn_dw_w"] = lax.dynamic_slice(tot["ffn_dw_w"].reshape(FFN_KERNEL, 2 * FFN_HIDDEN), (0, s_me * wff),
                                          (FFN_KERNEL, wff)).reshape(a["ffn_dw_w"].shape)

    delta, new_m, new_v = {}, {}, {}
    for n in ["w_cond"] + [n for n, _, _ in BIG]:
        d, nm_, nv_ = _adamw("adamw_" + n, a[n][0], grads[n][0], a["m_" + n][0], a["v_" + n][0])
        delta[n], new_m[n], new_v[n] = d[None], nm_[None], nv_[None]
    upd = [n for n, _ in SMALL_UPD]
    d, nm_, nv_ = _adamw("adamw_small", _pack({n: a[n] for n in upd}, SMALL_UPD), _pack({n: grads[n] for n in upd}, SMALL_UPD),
                         _pack({n: a["m_" + n] for n in upd}, SMALL_UPD), _pack({n: a["v_" + n] for n in upd}, SMALL_UPD))
    for dst, flat in ((delta, d), (new_m, nm_), (new_v, nv_)):
        for n, val in _unpack(flat, SMALL_UPD).items():
            dst[n] = val.reshape(a[n].shape)

    loss = tot["loss"].reshape(())
    return (loss, gx[None], *[grads[n] for n in WEIGHTS], *[delta[n] for n in WEIGHTS],
            *[new_m[n] for n in WEIGHTS], *[new_v[n] for n in WEIGHTS])
```

```python
import functools
import math

import jax
import jax.numpy as jnp
from jax import lax
from jax.experimental import pallas as pl
from jax.experimental.pallas import tpu as pltpu

F32 = jnp.float32
BF16 = jnp.bfloat16

D_MODEL = 1024
SSM_WIDTH = 512
SSM_GROUP = 16
SSM_GROUPS = 32
SSM_STATE = 64
CONV_WIDTH = 512
CONV_KERNEL = 31
FFN_HIDDEN = 2816
FFN_KERNEL = 3
IN_PROJ_WIDTH = 3584
N_COND = 6
ALPHA = 2.0 ** 0.25
LN_EPS = 1e-5
ADAM_LR, ADAM_B1, ADAM_B2, ADAM_EPS, ADAM_WD, ADAM_STEP = 0.001, 0.9, 0.999, 1e-08, 0.01, 10

N_DEV = 8
N_CHIP = 4
LANES = 128
SSM_CHUNK = 16
LANE_GROUPS = LANES // SSM_GROUP
N_LANE_BLOCKS = SSM_WIDTH // LANES
STATE_COLS = LANE_GROUPS * SSM_STATE
CHUNK_COLS = SSM_CHUNK * LANES
CONV_HALO = 32
VMEM_LIMIT = 56 * 1024 * 1024
MESH = pl.DeviceIdType.MESH

BIG = (
    ("w_in", "col", (D_MODEL, IN_PROJ_WIDTH)),
    ("ssm_glu_w_a", "col", (SSM_WIDTH, D_MODEL)),
    ("ssm_glu_w_b", "col", (SSM_WIDTH, D_MODEL)),
    ("cv_w_pw", "col", (CONV_WIDTH, D_MODEL)),
    ("w_out", "row", (D_MODEL, D_MODEL)),
    ("ffn_w_up", "col", (D_MODEL, 2 * FFN_HIDDEN)),
    ("ffn_w_down", "row", (FFN_HIDDEN, D_MODEL)),
)

WEIGHTS = ['w_cond', 'b_cond', 'w_in', 'b_in', 'ssm_lambda_re', 'ssm_lambda_im', 'ssm_log_dt', 'ssm_b_re', 'ssm_b_im',
           'ssm_c_re', 'ssm_c_im', 'ssm_d', 'ssm_glu_w_a', 'ssm_glu_w_b', 'cv_dw_w', 'cv_dw_b', 'cv_ln_g', 'cv_ln_b',
           'cv_w_pw', 'w_out', 'ln1_g', 'ln1_b', 'ffn_w_up', 'ffn_dw_w', 'ffn_dw_b', 'ffn_w_down', 'ln2_g', 'ln2_b']
INPUTS = ['x', 'c'] + WEIGHTS + ['loss_target'] + ['m_' + n for n in WEIGHTS] + ['v_' + n for n in WEIGHTS]

PACK = (
    ("dmod", N_COND * D_MODEL), ("c_act", D_MODEL), ("b_in", IN_PROJ_WIDTH),
    ("ssm_lambda_re", SSM_GROUPS * SSM_STATE), ("ssm_lambda_im", SSM_GROUPS * SSM_STATE), ("ssm_log_dt", SSM_GROUPS),
    ("ssm_b_re", SSM_GROUPS * SSM_STATE * SSM_GROUP), ("ssm_b_im", SSM_GROUPS * SSM_STATE * SSM_GROUP),
    ("ssm_c_re", SSM_GROUPS * SSM_STATE * SSM_GROUP), ("ssm_c_im", SSM_GROUPS * SSM_STATE * SSM_GROUP),
    ("ssm_d", SSM_GROUPS * SSM_GROUP), ("cv_dw_w", CONV_KERNEL * CONV_WIDTH), ("cv_dw_b", CONV_WIDTH),
    ("cv_ln_g", CONV_WIDTH), ("cv_ln_b", CONV_WIDTH), ("ln1_g", D_MODEL), ("ln1_b", D_MODEL),
    ("ffn_dw_w", FFN_KERNEL * 2 * FFN_HIDDEN), ("ffn_dw_b", 2 * FFN_HIDDEN), ("ln2_g", D_MODEL), ("ln2_b", D_MODEL),
    ("loss", 1),
)
PACK_COLS = 1024
PACK_ROWS = 192
assert sum(n for _, n in PACK) <= PACK_ROWS * PACK_COLS

SMALL_UPD = (
    ("b_cond", N_COND * D_MODEL), ("b_in", IN_PROJ_WIDTH),
    ("ssm_lambda_re", SSM_GROUPS * SSM_STATE), ("ssm_lambda_im", SSM_GROUPS * SSM_STATE), ("ssm_log_dt", SSM_GROUPS),
    ("ssm_b_re", SSM_GROUPS * SSM_STATE * SSM_GROUP), ("ssm_b_im", SSM_GROUPS * SSM_STATE * SSM_GROUP),
    ("ssm_c_re", SSM_GROUPS * SSM_STATE * SSM_GROUP), ("ssm_c_im", SSM_GROUPS * SSM_STATE * SSM_GROUP),
    ("ssm_d", SSM_GROUPS * SSM_GROUP), ("cv_dw_w", CONV_KERNEL * CONV_WIDTH // N_CHIP), ("cv_dw_b", CONV_WIDTH),
    ("cv_ln_g", CONV_WIDTH), ("cv_ln_b", CONV_WIDTH), ("ln1_g", D_MODEL), ("ln1_b", D_MODEL),
    ("ffn_dw_w", FFN_KERNEL * 2 * FFN_HIDDEN // N_CHIP), ("ffn_dw_b", 2 * FFN_HIDDEN), ("ln2_g", D_MODEL),
    ("ln2_b", D_MODEL),
)
assert sum(n for _, n in SMALL_UPD) <= PACK_ROWS * PACK_COLS


def _params(sem=None, **kw):
    return pltpu.CompilerParams(dimension_semantics=sem, vmem_limit_bytes=VMEM_LIMIT, **kw)


def _ln_stats(x):
    mu = jnp.mean(x, axis=-1, keepdims=True)
    xc = x - mu
    var = jnp.mean(xc * xc, axis=-1, keepdims=True)
    rstd = lax.rsqrt(var + LN_EPS)
    return xc * rstd, rstd


def _ln_bwd(dxhat, xhat, rstd):
    m1 = jnp.mean(dxhat, axis=-1, keepdims=True)
    m2 = jnp.mean(dxhat * xhat, axis=-1, keepdims=True)
    return rstd * (dxhat - m1 - xhat * m2)


def _sig(x):
    return 1.0 / (1.0 + jnp.exp(-x))


def _gelu(x):
    return 0.5 * x * (1.0 + lax.erf(x * (1.0 / math.sqrt(2.0))))


def _dgelu(x):
    return 0.5 * (1.0 + lax.erf(x * (1.0 / math.sqrt(2.0)))) + x * jnp.exp(-0.5 * x * x) * (1.0 / math.sqrt(2.0 * math.pi))


def _gelu_and_grad(x):
    er = lax.erf(x * (1.0 / math.sqrt(2.0)))
    cdf = 0.5 * (1.0 + er)
    return x * cdf, cdf + x * jnp.exp(-0.5 * x * x) * (1.0 / math.sqrt(2.0 * math.pi))


def _colsum(a):
    return jnp.sum(a, axis=0, keepdims=True)


def _fill_rotations(buf, rot, rows):
    for r in range(1, 8):
        rot[r - 1] = buf[pl.ds(r, rows), :]


def _rows_at(buf, rot, offset, tb):
    q, r = divmod(offset, 8)
    if r == 0:
        return buf[pl.ds(8 * q, tb), :]
    return rot[r - 1, pl.ds(8 * q, tb), :]


def _dot(a, b):
    return jnp.dot(a, b, preferred_element_type=F32)


def _dot_nt(a, b):
    return lax.dot_general(a, b, (((1,), (1,)), ((), ())), preferred_element_type=F32)


def _dot_tn(a, b):
    return lax.dot_general(a, b, (((0,), (0,)), ((), ())), preferred_element_type=F32)


def _load_once(src, dst, sem):
    cp = pltpu.make_async_copy(src, dst, sem)
    cp.start()
    cp.wait()


def _full(a):
    nd = a.ndim
    return pl.BlockSpec(a.shape, lambda *_: (0,) * nd)


ANY = pl.BlockSpec(memory_space=pl.ANY)


def _f1_inproj(x, modv, b_in, w_in, tb):
    t = x.shape[0]
    chunks = [(j * 512, 512) for j in range(IN_PROJ_WIDTH // 512)]

    def body(x_ref, modv_ref, b_ref, w_hbm, u4_ref, prest_ref, h_ref, w_v, sem):
        @pl.when(pl.program_id(0) == 0)
        def _():
            _load_once(w_hbm, w_v, sem)

        xn, _ = _ln_stats(x_ref[...])
        h = (xn * (1.0 + modv_ref[1:2, :]) + modv_ref[0:1, :]).astype(BF16)
        h_ref[...] = h
        for c0, cw in chunks:
            p = _dot(h, w_v[:, c0:c0 + cw]) + b_ref[:, c0:c0 + cw]
            if c0 == 0:
                for b in range(N_LANE_BLOCKS):
                    u4_ref[b] = p[:, b * LANES:(b + 1) * LANES].astype(BF16)
            else:
                prest_ref[:, c0 - SSM_WIDTH:c0 - SSM_WIDTH + cw] = p

    return pl.pallas_call(
        body, name="f1_inproj", grid=(t // tb,),
        in_specs=[pl.BlockSpec((tb, D_MODEL), lambda i: (i, 0)), _full(modv), _full(b_in), ANY],
        out_specs=[pl.BlockSpec((N_LANE_BLOCKS, tb, LANES), lambda i: (0, i, 0)),
                   pl.BlockSpec((tb, IN_PROJ_WIDTH - SSM_WIDTH), lambda i: (i, 0)),
                   pl.BlockSpec((tb, D_MODEL), lambda i: (i, 0))],
        out_shape=[jax.ShapeDtypeStruct((N_LANE_BLOCKS, t, LANES), BF16),
                   jax.ShapeDtypeStruct((t, IN_PROJ_WIDTH - SSM_WIDTH), F32),
                   jax.ShapeDtypeStruct((t, D_MODEL), BF16)],
        scratch_shapes=[pltpu.VMEM(w_in.shape, BF16), pltpu.SemaphoreType.DMA],
        compiler_params=_params(("arbitrary",)),
    )(x, modv, b_in, w_in)


def _s5_build(lam_re, lam_im, log_dt, b_re, b_im, c_re, c_im, d):
    hi = lax.Precision.HIGHEST
    el, g, n, p, nb = SSM_CHUNK, SSM_GROUPS, SSM_STATE, SSM_GROUP, N_LANE_BLOCKS
    lr = jnp.minimum(lam_re, -1e-4)
    li = lam_im
    dt = jnp.exp(log_dt)[:, None]
    mag = jnp.exp(lr * dt)
    ang = li * dt
    lbr, lbi = mag * jnp.cos(ang), mag * jnp.sin(ang)
    num_r, num_i = lbr - 1.0, lbi
    den = lr * lr + li * li
    coef_r = (num_r * lr + num_i * li) / den
    coef_i = (num_i * lr - num_r * li) / den
    bbar_r = coef_r[..., None] * b_re - coef_i[..., None] * b_im
    bbar_i = coef_r[..., None] * b_im + coef_i[..., None] * b_re
    k = jnp.arange(el + 1, dtype=F32)[:, None, None]
    pmag = jnp.exp(k * (lr * dt)[None])
    pr, pi = pmag * jnp.cos(k * ang[None]), pmag * jnp.sin(k * ang[None])
    car = c_re[None] * pr[:, :, None, :] - c_im[None] * pi[:, :, None, :]
    cai = c_re[None] * pi[:, :, None, :] + c_im[None] * pr[:, :, None, :]
    kern = (jnp.einsum("kgpn,gnq->kgqp", car[:el], bbar_r, precision=hi)
            - jnp.einsum("kgpn,gnq->kgqp", cai[:el], bbar_i, precision=hi))
    kern = kern.at[0].add(jnp.eye(p, dtype=F32)[None] * d[:, None, :])
    kc = kern.reshape(el, g * p, p)
    bt_r = bbar_r.transpose(0, 2, 1)[None]
    bt_i = bbar_i.transpose(0, 2, 1)[None]
    rev = el - 1 - jnp.arange(el)
    qr, qi = pr[rev][:, :, None, :], pi[rev][:, :, None, :]
    sw_r = (qr * bt_r - qi * bt_i).reshape(el, g * p, n)
    sw_i = (qr * bt_i + qi * bt_r).reshape(el, g * p, n)
    sg_r = car[1:].reshape(el, g * p, n)
    sg_i = (-cai[1:]).reshape(el, g * p, n)
    a = jnp.stack([pr[el].reshape(nb, LANE_GROUPS * n), pi[el].reshape(nb, LANE_GROUPS * n)], axis=1)
    return kc, sw_r, sw_i, sg_r, sg_i, a


def _expand(src, reps):
    rows, w = src.shape
    cols = reps * w
    r = lax.broadcasted_iota(jnp.int32, (w, cols), 0)
    c = lax.broadcasted_iota(jnp.int32, (w, cols), 1)
    rep = (r == (c & (w - 1))).astype(BF16)
    out = _dot(src.astype(BF16), rep)
    rg = lax.broadcasted_iota(jnp.int32, (rows, cols), 0) // SSM_GROUP
    cg = lax.broadcasted_iota(jnp.int32, (rows, cols), 1) // w
    return jnp.where(rg == cg, out, 0.0).astype(BF16)


def _fold(x, w):
    rows, cols = x.shape
    rg = lax.broadcasted_iota(jnp.int32, (rows, cols), 0) // SSM_GROUP
    cg = lax.broadcasted_iota(jnp.int32, (rows, cols), 1) // w
    x = jnp.where(rg == cg, x, 0.0)
    while cols > LANES:
        x = x[:, :cols // 2] + x[:, cols // 2:]
        cols //= 2
    s = LANES // 2
    while s >= w:
        x = x + pltpu.roll(x, s, axis=1)
        s //= 2
    return x[:, :w]


def _build_maps(s_ref, dst):
    for j in range(SSM_CHUNK):
        dst[j * LANES:(j + 1) * LANES, :] = _expand(s_ref[j], LANE_GROUPS)


def _build_toeplitz(kc_ref, dst):
    dst[...] = jnp.zeros_like(dst)
    for d in range(SSM_CHUNK):
        blk = _expand(kc_ref[d], LANE_GROUPS)
        for ji in range(SSM_CHUNK - d):
            jo = ji + d
            dst[ji * LANES:(ji + 1) * LANES, jo * LANES:(jo + 1) * LANES] = blk


def _cblk(w):
    return pl.BlockSpec((SSM_CHUNK, LANES, w), lambda b: (0, b, 0))


def _s5a_state(u2, sw_r, sw_i, a8):
    nb, nc, _ = u2.shape
    sc = STATE_COLS

    def body(u_ref, swr_ref, swi_ref, a_ref, hr_ref, hi_ref, w_s, xr_s, xi_s):
        u = u_ref[0]
        _build_maps(swr_ref, w_s)
        xr_s[...] = _dot(u, w_s[...])
        _build_maps(swi_ref, w_s)
        xi_s[...] = _dot(u, w_s[...])
        ar = a_ref[0, 0:1, :]
        ai = a_ref[0, 1:2, :]

        def step(c, carry):
            hr, hi = carry
            hr_ref[0, pl.ds(c, 1), :] = hr
            hi_ref[0, pl.ds(c, 1), :] = hi
            xr = xr_s[pl.ds(c, 1), :]
            xi = xi_s[pl.ds(c, 1), :]
            return ar * hr - ai * hi + xr, ar * hi + ai * hr + xi

        z = jnp.zeros((1, sc), F32)
        lax.fori_loop(0, nc, step, (z, z))

    return pl.pallas_call(
        body, name="s5a_state", grid=(nb,),
        in_specs=[pl.BlockSpec((1, nc, CHUNK_COLS), lambda b: (b, 0, 0)), _cblk(SSM_STATE), _cblk(SSM_STATE),
                  pl.BlockSpec((1, 8, sc), lambda b: (b, 0, 0))],
        out_specs=[pl.BlockSpec((1, nc, sc), lambda b: (b, 0, 0))] * 2,
        out_shape=[jax.ShapeDtypeStruct((nb, nc, sc), F32)] * 2,
        scratch_shapes=[pltpu.VMEM((CHUNK_COLS, sc), BF16), pltpu.VMEM((nc, sc), F32), pltpu.VMEM((nc, sc), F32)],
        compiler_params=_params(("arbitrary",)),
    )(u2, sw_r, sw_i, a8)


def _s5b_out(u2, kc, sg_r, sg_i, hr, hi):
    nb, nc, _ = u2.shape
    sc = STATE_COLS
    cw = 512

    def body(u_ref, kc_ref, sgr_ref, sgi_ref, hr_ref, hi_ref, y_ref, tm_s, gr_s, gi_s):
        _build_toeplitz(kc_ref, tm_s)
        _build_maps(sgr_ref, gr_s)
        _build_maps(sgi_ref, gi_s)
        u = u_ref[0]
        h_r = hr_ref[0].astype(BF16)
        h_i = hi_ref[0].astype(BF16)
        for j in range(CHUNK_COLS // cw):
            cs = slice(j * cw, (j + 1) * cw)
            y_ref[0, :, cs] = _dot(u, tm_s[:, cs]) + _dot_nt(h_r, gr_s[cs, :]) + _dot_nt(h_i, gi_s[cs, :])

    return pl.pallas_call(
        body, name="s5b_out", grid=(nb,),
        in_specs=[pl.BlockSpec((1, nc, CHUNK_COLS), lambda b: (b, 0, 0)), _cblk(SSM_GROUP), _cblk(SSM_STATE),
                  _cblk(SSM_STATE), pl.BlockSpec((1, nc, sc), lambda b: (b, 0, 0)),
                  pl.BlockSpec((1, nc, sc), lambda b: (b, 0, 0))],
        out_specs=pl.BlockSpec((1, nc, CHUNK_COLS), lambda b: (b, 0, 0)),
        out_shape=jax.ShapeDtypeStruct((nb, nc, CHUNK_COLS), F32),
        scratch_shapes=[pltpu.VMEM((CHUNK_COLS, CHUNK_COLS), BF16), pltpu.VMEM((CHUNK_COLS, sc), BF16),
                        pltpu.VMEM((CHUNK_COLS, sc), BF16)],
        compiler_params=_params(("arbitrary",)),
    )(u2, kc, sg_r, sg_i, hr, hi)


def _f4_mixer(ys4, prest, x, modv, cvv, cw32, w_a, w_b, w_pw, w_out, tb):
    t = x.shape[0]
    hb = tb // CONV_HALO

    def body(ys_ref, pr_ref, halo_ref, x_ref, modv_ref, cvv_ref, cw_ref, wa_ref, wb_ref, wpw_ref, wout_ref,
             r1_ref, ya_ref, yb_ref, ycv_ref, vc_ref, yg_ref, vs_ref, mg_ref, vbuf, vrot):
        i = pl.program_id(0)
        ys = jnp.concatenate([ys_ref[b] for b in range(N_LANE_BLOCKS)], axis=-1)
        yg = _gelu(ys).astype(BF16)
        yg_ref[...] = yg
        ya = _dot(yg, wa_ref[...])
        yb = _dot(yg, wb_ref[...])
        ya_ref[...] = ya.astype(BF16)
        yb_ref[...] = yb.astype(BF16)
        yssm = ya * _sig(yb)
        hv = halo_ref[:, 0:CONV_WIDTH] * _sig(halo_ref[:, CONV_WIDTH:2 * CONV_WIDTH])
        vbuf[0:CONV_HALO, :] = jnp.where(i == 0, 0.0, hv)
        vbuf[CONV_HALO:, :] = pr_ref[:, 0:CONV_WIDTH] * _sig(pr_ref[:, CONV_WIDTH:2 * CONV_WIDTH])
        _fill_rotations(vbuf, vrot, tb + CONV_HALO - 8)
        acc = jnp.zeros((tb, CONV_WIDTH), F32)
        for k in range(CONV_KERNEL):
            acc += _rows_at(vbuf, vrot, CONV_HALO - CONV_KERNEL + 1 + k, tb) * cw_ref[k:k + 1, :]
        vc = acc + cvv_ref[0:1, :]
        vc_ref[...] = vc
        xh, _ = _ln_stats(vc)
        vl = xh * cvv_ref[1:2, :] + cvv_ref[2:3, :]
        vs = (vl * _sig(vl)).astype(BF16)
        vs_ref[...] = vs
        ycv = _dot(vs, wpw_ref[...])
        ycv_ref[...] = ycv.astype(BF16)
        gs = pr_ref[:, 2 * CONV_WIDTH:2 * CONV_WIDTH + D_MODEL]
        gc = pr_ref[:, 2 * CONV_WIDTH + D_MODEL:]
        merged = (_sig(gs) * yssm + _sig(gc) * ycv).astype(BF16)
        mg_ref[...] = merged
        ym = _dot(merged, wout_ref[...])
        r1_ref[...] = ALPHA * x_ref[...] + modv_ref[2:3, :] * ym

    tok = lambda w: pl.BlockSpec((tb, w), lambda i: (i, 0))
    return pl.pallas_call(
        body, name="f4_mixer", grid=(t // tb,),
        in_specs=[pl.BlockSpec((N_LANE_BLOCKS, tb, LANES), lambda i: (0, i, 0)), tok(prest.shape[1]),
                  pl.BlockSpec((CONV_HALO, 2 * CONV_WIDTH), lambda i: (jnp.maximum(i * hb - 1, 0), 0)),
                  tok(D_MODEL), _full(modv), _full(cvv), _full(cw32), _full(w_a), _full(w_b), _full(w_pw), _full(w_out)],
        out_specs=[tok(D_MODEL), tok(D_MODEL), tok(D_MODEL), tok(D_MODEL), tok(CONV_WIDTH), tok(SSM_WIDTH),
                   tok(CONV_WIDTH), tok(D_MODEL)],
        out_shape=[jax.ShapeDtypeStruct((t, D_MODEL), F32), jax.ShapeDtypeStruct((t, D_MODEL), BF16),
                   jax.ShapeDtypeStruct((t, D_MODEL), BF16), jax.ShapeDtypeStruct((t, D_MODEL), BF16),
                   jax.ShapeDtypeStruct((t, CONV_WIDTH), F32), jax.ShapeDtypeStruct((t, SSM_WIDTH), BF16),
                   jax.ShapeDtypeStruct((t, CONV_WIDTH), BF16), jax.ShapeDtypeStruct((t, D_MODEL), BF16)],
        scratch_shapes=[pltpu.VMEM((tb + CONV_HALO, CONV_WIDTH), F32),
                        pltpu.VMEM((7, tb + CONV_HALO - 8, CONV_WIDTH), F32)],
        compiler_params=_params(("arbitrary",)),
    )(ys4, prest, prest, x, modv, cvv, cw32, w_a, w_b, w_pw, w_out)


FFN_COLS = 1408


def _f5_ffn(r1, tgt, modv, lnv, fdw, w_up, w_down, tb):
    t = r1.shape[0]
    fw = 2 * FFN_HIDDEN

    def body(r1_ref, tgt_ref, modv_ref, lnv_ref, fdw_ref, wup_hbm, wdn_hbm,
             dr2_ref, d_ref, up_ref, z_ref, acc_ref, wup_v, wdn_v, upbuf, gbuf, hbuf, sems):
        i = pl.program_id(0)

        @pl.when(i == 0)
        def _():
            _load_once(wup_hbm, wup_v, sems.at[0])
            _load_once(wdn_hbm, wdn_v, sems.at[1])
            acc_ref[...] = jnp.zeros_like(acc_ref)
            upbuf[0:8, :] = jnp.zeros((8, fw), F32)

        xh1, _ = _ln_stats(r1_ref[...])
        x1 = xh1 * lnv_ref[0:1, :] + lnv_ref[1:2, :]
        xn2, _ = _ln_stats(x1)
        h2 = (xn2 * (1.0 + modv_ref[4:5, :]) + modv_ref[3:4, :]).astype(BF16)
        for j in range(fw // FFN_COLS):
            cs = slice(j * FFN_COLS, (j + 1) * FFN_COLS)
            up = _dot(h2, wup_v[:, cs])
            upbuf[8:, cs] = up
            up_ref[:, cs] = up.astype(BF16)

        def conv(cs):
            return (fdw_ref[0:1, cs] * upbuf[pl.ds(6, tb), cs] + fdw_ref[1:2, cs] * upbuf[pl.ds(7, tb), cs]
                    + fdw_ref[2:3, cs] * upbuf[pl.ds(8, tb), cs] + fdw_ref[3:4, cs])

        halves = [(slice(j * FFN_COLS, (j + 1) * FFN_COLS),
                   slice(FFN_HIDDEN + j * FFN_COLS, FFN_HIDDEN + (j + 1) * FFN_COLS)) for j in range(FFN_HIDDEN // FFN_COLS)]
        yf = jnp.zeros((tb, D_MODEL), F32)
        for ca, cv in halves:
            v = conv(cv)
            g, dg = _gelu_and_grad(conv(ca))
            gbuf[:, ca] = g.astype(BF16)
            hbuf[:, ca] = (v * dg).astype(BF16)
            z = (g * v).astype(BF16)
            z_ref[:, ca] = z
            yf += _dot(z, wdn_v[ca, :])
        r2 = ALPHA * x1 + modv_ref[5:6, :] * yf
        xh2, rstd2 = _ln_stats(r2)
        e = xh2 * lnv_ref[2:3, :] + lnv_ref[3:4, :] - tgt_ref[...]
        dx2 = e * (1.0 / D_MODEL)
        acc_ref[3:4, :] += _colsum(e * e) * (0.5 / D_MODEL)
        acc_ref[0:1, :] += _colsum(dx2 * xh2)
        acc_ref[1:2, :] += _colsum(dx2)
        dr2 = _ln_bwd(dx2 * lnv_ref[2:3, :], xh2, rstd2)
        dr2_ref[...] = dr2
        acc_ref[2:3, :] += _colsum(dr2 * yf)
        dyf = (modv_ref[5:6, :] * dr2).astype(BF16)
        for ca, cv in halves:
            dz = _dot_nt(dyf, wdn_v[ca, :])
            d_ref[:, ca] = (dz * hbuf[:, ca].astype(F32)).astype(BF16)
            d_ref[:, cv] = (dz * gbuf[:, ca].astype(F32)).astype(BF16)
        upbuf[0:8, :] = upbuf[pl.ds(tb, 8), :]

    tok = lambda w: pl.BlockSpec((tb, w), lambda i: (i, 0))
    return pl.pallas_call(
        body, name="f5_ffn", grid=(t // tb,),
        in_specs=[tok(D_MODEL), tok(D_MODEL), _full(modv), _full(lnv), _full(fdw), ANY, ANY],
        out_specs=[tok(D_MODEL), tok(fw), tok(fw), tok(FFN_HIDDEN), pl.BlockSpec((8, D_MODEL), lambda i: (0, 0))],
        out_shape=[jax.ShapeDtypeStruct((t, D_MODEL), F32), jax.ShapeDtypeStruct((t, fw), BF16),
                   jax.ShapeDtypeStruct((t, fw), BF16), jax.ShapeDtypeStruct((t, FFN_HIDDEN), BF16),
                   jax.ShapeDtypeStruct((8, D_MODEL), F32)],
        scratch_shapes=[pltpu.VMEM(w_up.shape, BF16), pltpu.VMEM(w_down.shape, BF16),
                        pltpu.VMEM((tb + 8, fw), F32), pltpu.VMEM((tb, FFN_HIDDEN), BF16),
                        pltpu.VMEM((tb, FFN_HIDDEN), BF16), pltpu.SemaphoreType.DMA((2,))],
        compiler_params=_params(("arbitrary",)),
    )(r1, tgt, modv, lnv, fdw, w_up, w_down)


def _b1b_ffn_up(d, up, dr2, r1, modv, lnv, fdw, w_up, tb):
    t = dr2.shape[0]
    fw = 2 * FFN_HIDDEN
    nt = t // tb
    hb = tb // 16

    def body(d_ref, nxt_ref, up_ref, dr2_ref, r1_ref, modv_ref, lnv_ref, fdw_ref, wup_hbm, dup_ref, dr1_ref, h2_ref,
             dyf_ref, acc_ref, accw_ref, wup_v, dbuf, shifted, sem):
        i = pl.program_id(0)

        @pl.when(i == 0)
        def _():
            _load_once(wup_hbm, wup_v, sem)
            acc_ref[...] = jnp.zeros_like(acc_ref)
            accw_ref[...] = jnp.zeros_like(accw_ref)

        dbuf[0:tb, :] = d_ref[...].astype(F32)
        dbuf[tb:, :] = jnp.where(i == nt - 1, 0.0, nxt_ref[...].astype(F32))
        dh2 = jnp.zeros((tb, D_MODEL), F32)
        for j in range(fw // FFN_COLS):
            cs = slice(j * FFN_COLS, (j + 1) * FFN_COLS)
            for k in range(1, FFN_KERNEL):
                shifted[k - 1] = dbuf[pl.ds(k, tb), cs]
            ds = [dbuf[pl.ds(0, tb), cs], shifted[0], shifted[1]]
            dup = (fdw_ref[2:3, cs] * ds[0] + fdw_ref[1:2, cs] * ds[1] + fdw_ref[0:1, cs] * ds[2]).astype(BF16)
            dup_ref[:, cs] = dup
            dh2 += _dot_nt(dup, wup_v[:, cs])
            upf = up_ref[:, cs].astype(F32)
            for k in range(FFN_KERNEL):
                accw_ref[k:k + 1, cs] += _colsum(ds[FFN_KERNEL - 1 - k] * upf)
            accw_ref[3:4, cs] += _colsum(ds[0])
        xh1, rstd1 = _ln_stats(r1_ref[...])
        x1 = xh1 * lnv_ref[0:1, :] + lnv_ref[1:2, :]
        xn2, rstd2 = _ln_stats(x1)
        h2_ref[...] = (xn2 * (1.0 + modv_ref[4:5, :]) + modv_ref[3:4, :]).astype(BF16)
        dr2 = dr2_ref[...]
        dyf_ref[...] = (modv_ref[5:6, :] * dr2).astype(BF16)
        acc_ref[0:1, :] += _colsum(dh2 * xn2)
        acc_ref[1:2, :] += _colsum(dh2)
        dx1 = _ln_bwd(dh2 * (1.0 + modv_ref[4:5, :]), xn2, rstd2) + ALPHA * dr2
        acc_ref[2:3, :] += _colsum(dx1 * xh1)
        acc_ref[3:4, :] += _colsum(dx1)
        dr1_ref[...] = _ln_bwd(dx1 * lnv_ref[0:1, :], xh1, rstd1)

    tok = lambda w: pl.BlockSpec((tb, w), lambda i: (i, 0))
    return pl.pallas_call(
        body, name="b1b_ffn_up", grid=(nt,),
        in_specs=[tok(fw), pl.BlockSpec((16, fw), lambda i: (jnp.minimum((i + 1) * hb, t // 16 - 1), 0)), tok(fw),
                  tok(D_MODEL), tok(D_MODEL), _full(modv), _full(lnv), _full(fdw), ANY],
        out_specs=[tok(fw), tok(D_MODEL), tok(D_MODEL), tok(D_MODEL), pl.BlockSpec((8, D_MODEL), lambda i: (0, 0)),
                   pl.BlockSpec((8, fw), lambda i: (0, 0))],
        out_shape=[jax.ShapeDtypeStruct((t, fw), BF16), jax.ShapeDtypeStruct((t, D_MODEL), F32),
                   jax.ShapeDtypeStruct((t, D_MODEL), BF16), jax.ShapeDtypeStruct((t, D_MODEL), BF16),
                   jax.ShapeDtypeStruct((8, D_MODEL), F32), jax.ShapeDtypeStruct((8, fw), F32)],
        scratch_shapes=[pltpu.VMEM(w_up.shape, BF16), pltpu.VMEM((tb + 16, fw), F32),
                        pltpu.VMEM((FFN_KERNEL - 1, tb, FFN_COLS), F32), pltpu.SemaphoreType.DMA],
        compiler_params=_params(("arbitrary",)),
    )(d, d, up, dr2, r1, modv, lnv, fdw, w_up)


def _b2_mixer(dr1, ys4, prest, ya, yb, ycv, vc, merged, modv, cvv, cw32, w_a, w_b, w_pw, w_out, tb):
    t = dr1.shape[0]
    nt = t // tb
    hb = tb // CONV_HALO
    cwd = CONV_WIDTH

    def body(dr1_ref, ys_ref, pr_ref, halo_ref, ya_ref, yb_ref, ycv_ref, vc_ref, mg_ref, modv_ref, cvv_ref, cw_ref,
             wa_ref, wb_ref, wpw_ref, wout_ref,
             dys_ref, dpr_ref, dya_ref, dyb_ref, dycv_ref, dym_ref, acc_a, acc_b, acc_w, vbuf, dvbuf, vrot, dvrot):
        i = pl.program_id(0)
        ti = nt - 1 - i

        @pl.when(i == 0)
        def _():
            acc_a[...] = jnp.zeros_like(acc_a)
            acc_b[...] = jnp.zeros_like(acc_b)
            acc_w[...] = jnp.zeros_like(acc_w)
            dvbuf[pl.ds(tb, CONV_HALO), :] = jnp.zeros((CONV_HALO, cwd), F32)

        dr1 = dr1_ref[...]
        dym = (modv_ref[2:3, :] * dr1).astype(BF16)
        dym_ref[...] = dym
        ym = _dot(mg_ref[...], wout_ref[...])
        acc_a[0:1, :] += _colsum(dr1 * ym)
        dmg = _dot_nt(dym, wout_ref[...])
        sgs = _sig(pr_ref[:, 2 * cwd:2 * cwd + D_MODEL])
        sgc = _sig(pr_ref[:, 2 * cwd + D_MODEL:])
        ya_v = ya_ref[...].astype(F32)
        syb = _sig(yb_ref[...].astype(F32))
        ycv_v = ycv_ref[...].astype(F32)
        dpr_ref[:, 2 * cwd:2 * cwd + D_MODEL] = (dmg * (ya_v * syb) * sgs * (1.0 - sgs)).astype(BF16)
        dpr_ref[:, 2 * cwd + D_MODEL:] = (dmg * ycv_v * sgc * (1.0 - sgc)).astype(BF16)
        dyssm = dmg * sgs
        dya = (dyssm * syb).astype(BF16)
        dyb = (dyssm * ya_v * syb * (1.0 - syb)).astype(BF16)
        dya_ref[...] = dya
        dyb_ref[...] = dyb
        dyg = _dot_nt(dya, wa_ref[...]) + _dot_nt(dyb, wb_ref[...])
        ys = jnp.concatenate([ys_ref[b] for b in range(N_LANE_BLOCKS)], axis=-1)
        dys = dyg * _dgelu(ys)
        for b in range(N_LANE_BLOCKS):
            dys_ref[b] = dys[:, b * LANES:(b + 1) * LANES].astype(BF16)
        dycv = (dmg * sgc).astype(BF16)
        dycv_ref[...] = dycv
        dvs = _dot_nt(dycv, wpw_ref[...])
        xh, rstd = _ln_stats(vc_ref[...])
        vl = xh * cvv_ref[1:2, :] + cvv_ref[2:3, :]
        s = _sig(vl)
        dvl = dvs * s * (1.0 + vl * (1.0 - s))
        acc_b[1:2, :] += _colsum(dvl * xh)
        acc_b[2:3, :] += _colsum(dvl)
        dvc = _ln_bwd(dvl * cvv_ref[1:2, :], xh, rstd)
        acc_b[0:1, :] += _colsum(dvc)
        hv = halo_ref[:, 0:cwd] * _sig(halo_ref[:, cwd:2 * cwd])
        vbuf[0:CONV_HALO, :] = jnp.where(ti == 0, 0.0, hv)
        cva = pr_ref[:, 0:cwd]
        scg = _sig(pr_ref[:, cwd:2 * cwd])
        vbuf[CONV_HALO:, :] = cva * scg
        dvbuf[0:tb, :] = dvc
        _fill_rotations(vbuf, vrot, tb + CONV_HALO - 8)
        _fill_rotations(dvbuf, dvrot, tb + CONV_HALO - 8)
        dv = jnp.zeros((tb, cwd), F32)
        for k in range(CONV_KERNEL):
            dv += _rows_at(dvbuf, dvrot, CONV_KERNEL - 1 - k, tb) * cw_ref[k:k + 1, :]
            acc_w[k:k + 1, :] += _colsum(dvc * _rows_at(vbuf, vrot, CONV_HALO - CONV_KERNEL + 1 + k, tb))
        dvbuf[pl.ds(tb, CONV_HALO), :] = dvbuf[0:CONV_HALO, :]
        dpr_ref[:, 0:cwd] = (dv * scg).astype(BF16)
        dpr_ref[:, cwd:2 * cwd] = (dv * cva * scg * (1.0 - scg)).astype(BF16)

    rtok = lambda w: pl.BlockSpec((tb, w), lambda i: (nt - 1 - i, 0))
    r4 = pl.BlockSpec((N_LANE_BLOCKS, tb, LANES), lambda i: (0, nt - 1 - i, 0))
    pw = prest.shape[1]
    return pl.pallas_call(
        body, name="b2_mixer", grid=(nt,),
        in_specs=[rtok(D_MODEL), r4, rtok(pw),
                  pl.BlockSpec((CONV_HALO, 2 * cwd), lambda i: (jnp.maximum((nt - 1 - i) * hb - 1, 0), 0)),
                  rtok(D_MODEL), rtok(D_MODEL), rtok(D_MODEL), rtok(cwd), rtok(D_MODEL),
                  _full(modv), _full(cvv), _full(cw32), _full(w_a), _full(w_b), _full(w_pw), _full(w_out)],
        out_specs=[r4, rtok(pw), rtok(D_MODEL), rtok(D_MODEL), rtok(D_MODEL), rtok(D_MODEL),
                   pl.BlockSpec((8, D_MODEL), lambda i: (0, 0)), pl.BlockSpec((8, cwd), lambda i: (0, 0)),
                   pl.BlockSpec((CONV_HALO, cwd), lambda i: (0, 0))],
        out_shape=[jax.ShapeDtypeStruct((N_LANE_BLOCKS, t, LANES), BF16), jax.ShapeDtypeStruct((t, pw), BF16),
                   jax.ShapeDtypeStruct((t, D_MODEL), BF16), jax.ShapeDtypeStruct((t, D_MODEL), BF16),
                   jax.ShapeDtypeStruct((t, D_MODEL), BF16), jax.ShapeDtypeStruct((t, D_MODEL), BF16),
                   jax.ShapeDtypeStruct((8, D_MODEL), F32), jax.ShapeDtypeStruct((8, cwd), F32),
                   jax.ShapeDtypeStruct((CONV_HALO, cwd), F32)],
        scratch_shapes=[pltpu.VMEM((tb + CONV_HALO, cwd), F32), pltpu.VMEM((tb + CONV_HALO, cwd), F32),
                        pltpu.VMEM((7, tb + CONV_HALO - 8, cwd), F32), pltpu.VMEM((7, tb + CONV_HALO - 8, cwd), F32)],
        compiler_params=_params(("arbitrary",)),
    )(dr1, ys4, prest, prest, ya, yb, ycv, vc, merged, modv, cvv, cw32, w_a, w_b, w_pw, w_out)


def _s5c_state_bwd(dy2, sg_r, sg_i, a8, hr, hi):
    nb, nc, _ = dy2.shape
    sc = STATE_COLS

    def body(dy_ref, sgr_ref, sgi_ref, a_ref, hr_ref, hi_ref, dxr_ref, dxi_ref, da_ref, dsgr_ref, dsgi_ref,
             g_s, lr_s, li_s, xr_s, xi_s):
        dy = dy_ref[0]
        _build_maps(sgr_ref, g_s)
        lr_s[...] = _dot(dy, g_s[...])
        _build_maps(sgi_ref, g_s)
        li_s[...] = _dot(dy, g_s[...])
        ar = a_ref[0, 0:1, :]
        ai = a_ref[0, 1:2, :]

        def step(k, carry):
            pr, pi, dar, dai = carry
            c = nc - 1 - k
            xr_s[pl.ds(c, 1), :] = pr
            xi_s[pl.ds(c, 1), :] = pi
            h_r = hr_ref[0, pl.ds(c, 1), :]
            h_i = hi_ref[0, pl.ds(c, 1), :]
            dar = dar + pr * h_r + pi * h_i
            dai = dai - pr * h_i + pi * h_r
            nr = lr_s[pl.ds(c, 1), :] + ar * pr + ai * pi
            ni = li_s[pl.ds(c, 1), :] - ai * pr + ar * pi
            return nr, ni, dar, dai

        z = jnp.zeros((1, sc), F32)
        _, _, dar, dai = lax.fori_loop(0, nc, step, (z, z, z, z))
        da_ref[0] = jnp.concatenate([dar, dai, jnp.zeros((6, sc), F32)], axis=0)
        dxr_ref[0] = xr_s[...].astype(BF16)
        dxi_ref[0] = xi_s[...].astype(BF16)
        for h_ref, o_ref in ((hr_ref, dsgr_ref), (hi_ref, dsgi_ref)):
            hb = h_ref[0].astype(BF16)
            for j in range(SSM_CHUNK):
                o_ref[j] = _fold(_dot_tn(dy[:, j * LANES:(j + 1) * LANES], hb), SSM_STATE)

    blk = lambda r, c: pl.BlockSpec((1, r, c), lambda b: (b, 0, 0))
    return pl.pallas_call(
        body, name="s5c_state_bwd", grid=(nb,),
        in_specs=[blk(nc, CHUNK_COLS), _cblk(SSM_STATE), _cblk(SSM_STATE), blk(8, sc), blk(nc, sc), blk(nc, sc)],
        out_specs=[blk(nc, sc), blk(nc, sc), blk(8, sc), _cblk(SSM_STATE), _cblk(SSM_STATE)],
        out_shape=[jax.ShapeDtypeStruct((nb, nc, sc), BF16), jax.ShapeDtypeStruct((nb, nc, sc), BF16),
                   jax.ShapeDtypeStruct((nb, 8, sc), F32),
                   jax.ShapeDtypeStruct((SSM_CHUNK, SSM_WIDTH, SSM_STATE), F32),
                   jax.ShapeDtypeStruct((SSM_CHUNK, SSM_WIDTH, SSM_STATE), F32)],
        scratch_shapes=[pltpu.VMEM((CHUNK_COLS, sc), BF16)] + [pltpu.VMEM((nc, sc), F32)] * 4,
        compiler_params=_params(("arbitrary",)),
    )(dy2, sg_r, sg_i, a8, hr, hi)


def _s5d_input_bwd(dy2, u2, kc, sw_r, sw_i, dxr, dxi):
    nb, nc, _ = dy2.shape
    sc = STATE_COLS

    def body(dy_ref, u_ref, kc_ref, swr_ref, swi_ref, dxr_ref, dxi_ref, du_ref, dkc_ref, dswr_ref, dswi_ref,
             tm_s, w_s, dk_s):
        dy = dy_ref[0]
        u = u_ref[0]
        _build_toeplitz(kc_ref, tm_s)
        du = _dot_nt(dy, tm_s[...])
        _build_maps(swr_ref, w_s)
        du += _dot_nt(dxr_ref[0], w_s[...])
        _build_maps(swi_ref, w_s)
        du += _dot_nt(dxi_ref[0], w_s[...])
        du_ref[0] = du.astype(BF16)
        dk_s[...] = jnp.zeros_like(dk_s)
        for ji in range(SSM_CHUNK):
            uj = u[:, ji * LANES:(ji + 1) * LANES]
            rows = _dot_tn(uj, dy)
            for jo in range(ji, SSM_CHUNK):
                dk_s[jo - ji] += rows[:, jo * LANES:(jo + 1) * LANES]
            dswr_ref[ji] = _fold(_dot_tn(uj, dxr_ref[0]), SSM_STATE)
            dswi_ref[ji] = _fold(_dot_tn(uj, dxi_ref[0]), SSM_STATE)
        for d in range(SSM_CHUNK):
            dkc_ref[d] = _fold(dk_s[d], SSM_GROUP)

    blk = lambda r, c: pl.BlockSpec((1, r, c), lambda b: (b, 0, 0))
    return pl.pallas_call(
        body, name="s5d_input_bwd", grid=(nb,),
        in_specs=[blk(nc, CHUNK_COLS), blk(nc, CHUNK_COLS), _cblk(SSM_GROUP), _cblk(SSM_STATE), _cblk(SSM_STATE),
                  blk(nc, sc), blk(nc, sc)],
        out_specs=[blk(nc, CHUNK_COLS), _cblk(SSM_GROUP), _cblk(SSM_STATE), _cblk(SSM_STATE)],
        out_shape=[jax.ShapeDtypeStruct((nb, nc, CHUNK_COLS), BF16),
                   jax.ShapeDtypeStruct((SSM_CHUNK, SSM_WIDTH, SSM_GROUP), F32),
                   jax.ShapeDtypeStruct((SSM_CHUNK, SSM_WIDTH, SSM_STATE), F32),
                   jax.ShapeDtypeStruct((SSM_CHUNK, SSM_WIDTH, SSM_STATE), F32)],
        scratch_shapes=[pltpu.VMEM((CHUNK_COLS, CHUNK_COLS), BF16), pltpu.VMEM((CHUNK_COLS, sc), BF16),
                        pltpu.VMEM((SSM_CHUNK, LANES, LANES), F32)],
        compiler_params=_params(("arbitrary",)),
    )(dy2, u2, kc, sw_r, sw_i, dxr, dxi)


def _b3_inproj(x, dr1, du4, dprest, modv, w_in, tb):
    t = x.shape[0]
    pw = IN_PROJ_WIDTH - SSM_WIDTH

    def body(x_ref, dr1_ref, du_ref, dpr_ref, modv_ref, w_hbm, gx_ref, dp_ref, acc_ref, accb_ref, w_v, sem):
        @pl.when(pl.program_id(0) == 0)
        def _():
            _load_once(w_hbm, w_v, sem)
            acc_ref[...] = jnp.zeros_like(acc_ref)
            accb_ref[...] = jnp.zeros_like(accb_ref)

        du = jnp.concatenate([du_ref[b] for b in range(N_LANE_BLOCKS)], axis=-1)
        dpr = dpr_ref[...]
        dp_ref[:, 0:SSM_WIDTH] = du
        dp_ref[:, SSM_WIDTH:] = dpr
        accb_ref[0:1, 0:SSM_WIDTH] += _colsum(du.astype(F32))
        accb_ref[0:1, SSM_WIDTH:] += _colsum(dpr.astype(F32))
        dh = _dot_nt(du, w_v[:, 0:SSM_WIDTH]) + _dot_nt(dpr, w_v[:, SSM_WIDTH:])
        xn, rstd = _ln_stats(x_ref[...])
        acc_ref[0:1, :] += _colsum(dh * xn)
        acc_ref[1:2, :] += _colsum(dh)
        gx_ref[...] = _ln_bwd(dh * (1.0 + modv_ref[1:2, :]), xn, rstd) + ALPHA * dr1_ref[...]

    tok = lambda w: pl.BlockSpec((tb, w), lambda i: (i, 0))
    return pl.pallas_call(
        body, name="b3_inproj", grid=(t // tb,),
        in_specs=[tok(D_MODEL), tok(D_MODEL), pl.BlockSpec((N_LANE_BLOCKS, tb, LANES), lambda i: (0, i, 0)), tok(pw),
                  _full(modv), ANY],
        out_specs=[tok(D_MODEL), tok(IN_PROJ_WIDTH), pl.BlockSpec((8, D_MODEL), lambda i: (0, 0)),
                   pl.BlockSpec((8, IN_PROJ_WIDTH), lambda i: (0, 0))],
        out_shape=[jax.ShapeDtypeStruct((t, D_MODEL), F32), jax.ShapeDtypeStruct((t, IN_PROJ_WIDTH), BF16),
                   jax.ShapeDtypeStruct((8, D_MODEL), F32), jax.ShapeDtypeStruct((8, IN_PROJ_WIDTH), F32)],
        scratch_shapes=[pltpu.VMEM(w_in.shape, BF16), pltpu.SemaphoreType.DMA],
        compiler_params=_params(("arbitrary",)),
    )(x, dr1, du4, dprest, modv, w_in)


TN_ROWS = 2048


def _tn_matmul(name, a, b, tm, tn):
    t, m = a.shape
    n = b.shape[1]
    tt = min(TN_ROWS, t)
    nk = t // tt

    def body(a_ref, b_ref, o_ref, acc):
        k = pl.program_id(2)

        @pl.when(k == 0)
        def _():
            acc[...] = jnp.zeros_like(acc)

        acc[...] += _dot_tn(a_ref[...], b_ref[...])

        @pl.when(k == nk - 1)
        def _():
            o_ref[...] = acc[...].astype(BF16)

    return pl.pallas_call(
        body, name=name, grid=(m // tm, n // tn, nk),
        in_specs=[pl.BlockSpec((tt, tm), lambda i, j, k: (k, i)), pl.BlockSpec((tt, tn), lambda i, j, k: (k, j))],
        out_specs=pl.BlockSpec((tm, tn), lambda i, j, k: (i, j)),
        out_shape=jax.ShapeDtypeStruct((m, n), BF16),
        scratch_shapes=[pltpu.VMEM((tm, tn), F32)],
        compiler_params=_params(("arbitrary", "arbitrary", "arbitrary")),
    )(a, b)


def _local_step(x, tgt, modv, wb, sp, tb=256):
    t = x.shape[0]
    nc = t // SSM_CHUNK
    row8 = lambda rows, w: jnp.concatenate([r.reshape(1, w) for r in rows] + [jnp.zeros((8 - len(rows), w), F32)], axis=0)
    lnv = row8([sp["ln1_g"], sp["ln1_b"], sp["ln2_g"], sp["ln2_b"]], D_MODEL)
    cvv = row8([sp["cv_dw_b"], sp["cv_ln_g"], sp["cv_ln_b"]], CONV_WIDTH)
    cw32 = jnp.concatenate([sp["cv_dw_w"].reshape(CONV_KERNEL, CONV_WIDTH), jnp.zeros((1, CONV_WIDTH), F32)], axis=0)
    fdw = row8(list(sp["ffn_dw_w"].reshape(FFN_KERNEL, 2 * FFN_HIDDEN)) + [sp["ffn_dw_b"]], 2 * FFN_HIDDEN)
    b_in = sp["b_in"].reshape(1, IN_PROJ_WIDTH)
    ssm = tuple(sp[k] for k in ("ssm_lambda_re", "ssm_lambda_im", "ssm_log_dt", "ssm_b_re", "ssm_b_im", "ssm_c_re",
                                "ssm_c_im", "ssm_d"))
    (kc, sw_r, sw_i, sg_r, sg_i, a), ssm_vjp = jax.vjp(_s5_build, *ssm)
    a8 = jnp.concatenate([a, jnp.zeros((N_LANE_BLOCKS, 6, STATE_COLS), F32)], axis=1)

    u4, prest, h1 = _f1_inproj(x, modv, b_in, wb["w_in"], tb)
    u2 = u4.reshape(N_LANE_BLOCKS, nc, CHUNK_COLS)
    hr, hi = _s5a_state(u2, sw_r, sw_i, a8)
    ys4 = _s5b_out(u2, kc, sg_r, sg_i, hr, hi).reshape(N_LANE_BLOCKS, t, LANES)
    r1, ya, yb, ycv, vc, yg, vs, merged = _f4_mixer(ys4, prest, x, modv, cvv, cw32, wb["ssm_glu_w_a"],
                                                    wb["ssm_glu_w_b"], wb["cv_w_pw"], wb["w_out"], tb)
    dr2, dconv, up, z, acc5 = _f5_ffn(r1, tgt, modv, lnv, fdw, wb["ffn_w_up"], wb["ffn_w_down"], tb)
    dup, dr1, h2, dyf, acc1b, acc1a = _b1b_ffn_up(dconv, up, dr2, r1, modv, lnv, fdw, wb["ffn_w_up"], tb)
    dys4, dprest, dya, dyb, dycv, dym, acc2a, acc2b, acc2w = _b2_mixer(
        dr1, ys4, prest, ya, yb, ycv, vc, merged, modv, cvv, cw32, wb["ssm_glu_w_a"], wb["ssm_glu_w_b"],
        wb["cv_w_pw"], wb["w_out"], tb)
    dy2 = dys4.reshape(N_LANE_BLOCKS, nc, CHUNK_COLS)
    dxr, dxi, da8, dsg_r, dsg_i = _s5c_state_bwd(dy2, sg_r, sg_i, a8, hr, hi)
    du2, dkc, dsw_r, dsw_i = _s5d_input_bwd(dy2, u2, kc, sw_r, sw_i, dxr, dxi)
    dssm = ssm_vjp((dkc, dsw_r, dsw_i, dsg_r, dsg_i, da8[:, 0:2, :]))
    gx, dp, acc3, acc3b = _b3_inproj(x, dr1, du2.reshape(N_LANE_BLOCKS, t, LANES), dprest, modv, wb["w_in"], tb)

    dbig = {
        "w_in": _tn_matmul("dw_in", h1, dp, 1024, 896),
        "ssm_glu_w_a": _tn_matmul("dw_glu_a", yg, dya, 512, 1024),
        "ssm_glu_w_b": _tn_matmul("dw_glu_b", yg, dyb, 512, 1024),
        "cv_w_pw": _tn_matmul("dw_pw", vs, dycv, 512, 1024),
        "w_out": _tn_matmul("dw_out", merged, dym, 1024, 1024),
        "ffn_w_up": _tn_matmul("dw_up", h2, dup, 1024, FFN_COLS),
        "ffn_w_down": _tn_matmul("dw_down", z, dyf, FFN_COLS, 1024),
    }
    dmod = jnp.concatenate([acc3[1], acc3[0], acc2a[0], acc1b[1], acc1b[0], acc5[2]])
    small = {
        "dmod": dmod, "b_in": acc3b[0],
        "ssm_lambda_re": dssm[0], "ssm_lambda_im": dssm[1], "ssm_log_dt": dssm[2], "ssm_b_re": dssm[3],
        "ssm_b_im": dssm[4], "ssm_c_re": dssm[5], "ssm_c_im": dssm[6], "ssm_d": dssm[7],
        "cv_dw_w": acc2w[0:CONV_KERNEL], "cv_dw_b": acc2b[0], "cv_ln_g": acc2b[1], "cv_ln_b": acc2b[2],
        "ln1_g": acc1b[2], "ln1_b": acc1b[3], "ffn_dw_w": acc1a[0:FFN_KERNEL], "ffn_dw_b": acc1a[3],
        "ln2_g": acc5[0], "ln2_b": acc5[1], "loss": jnp.sum(acc5[3]).reshape(1),
    }
    return gx, dbig, small


def _place():
    x, y, c = lax.axis_index("x"), lax.axis_index("y"), lax.axis_index("c")
    chips = [(1 - x, y), (x, 1 - y), (1 - x, 1 - y)]
    return x, y, c, chips


def _allgather(name, shard):
    m_per, n = shard.shape

    def body(x_ref, out_ref, send_sems, recv_sems, local_sem):
        x, y, c, chips = _place()
        me, sibling = (x, y, c), (x, y, 1 - c)

        def rows(px, py, pc):
            return out_ref.at[pl.ds((4 * px + 2 * py + pc) * m_per, m_per), :]

        def copy(k, block, to, src=None):
            return pltpu.make_async_remote_copy(
                src_ref=rows(*block) if src is None else src, dst_ref=rows(*block),
                send_sem=send_sems.at[k], recv_sem=recv_sems.at[k], device_id=to, device_id_type=MESH)

        mine = pltpu.make_async_copy(x_ref, rows(*me), local_sem)
        mine.start()
        first = [copy(0, me, sibling, src=x_ref)]
        first += [copy(1 + j, me, (*chip, c), src=x_ref) for j, chip in enumerate(chips)]
        for cp in first:
            cp.start()
        passed = [copy(4 + j, (*chip, c), sibling) for j, chip in enumerate(chips)]
        for j, chip in enumerate(chips):
            copy(1 + j, (*chip, c), me).wait_recv()
            passed[j].start()
        copy(0, sibling, me).wait_recv()
        for j, chip in enumerate(chips):
            copy(4 + j, (*chip, 1 - c), me).wait_recv()
        for cp in first + passed:
            cp.wait_send()
        mine.wait()

    return pl.pallas_call(
        body, name=name,
        out_shape=jax.ShapeDtypeStruct((N_DEV * m_per, n), shard.dtype),
        in_specs=[pl.BlockSpec(memory_space=pltpu.VMEM)],
        out_specs=pl.BlockSpec(memory_space=pltpu.VMEM),
        scratch_shapes=[pltpu.SemaphoreType.DMA((7,)), pltpu.SemaphoreType.DMA((7,)), pltpu.SemaphoreType.DMA],
        compiler_params=_params(),
    )(shard)


def _piece(kind, shape):
    r, cc = shape
    return (r // 2, cc // N_CHIP) if kind == "col" else (r // (2 * N_CHIP), cc)


def _piece_at(ref, kind, shape, s, k):
    pr, pc = _piece(kind, shape)
    if kind == "col":
        return ref.at[pl.ds(k * pr, pr), pl.ds(pl.multiple_of(s * pc, LANES), pc)]
    return ref.at[pl.ds(pl.multiple_of((2 * s + k) * pr, 16), pr), :]


def _add_rows(pr):
    return 64 if pr % 64 == 0 else 16


def _gather_weights(shards):
    nm = len(BIG)

    def body(*refs):
        ins, outs = refs[:nm], refs[nm:2 * nm]
        stage = refs[2 * nm:3 * nm]
        send, recv, fsend, frecv, lsem = refs[3 * nm:]
        x, y, c, chips = _place()
        s_me = 2 * x + y
        sibling = (x, y, 1 - c)
        pend = []
        for m, (_, kind, shape) in enumerate(BIG):
            stage[m][...] = ins[m][...].astype(BF16)
        for m, (_, kind, shape) in enumerate(BIG):
            pr, pc = _piece(kind, shape)
            for k in range(2):
                cp = pltpu.make_async_copy(stage[m].at[pl.ds(k * pr, pr), :], _piece_at(outs[m], kind, shape, s_me, k),
                                           lsem.at[m, k])
                cp.start()
                pend.append(cp.wait)
            for j, chip in enumerate(chips):
                cp = pltpu.make_async_remote_copy(
                    src_ref=stage[m].at[pl.ds(pl.multiple_of(c * pr, 16), pr), :],
                    dst_ref=_piece_at(outs[m], kind, shape, s_me, c),
                    send_sem=send.at[m, j], recv_sem=recv.at[m, j], device_id=(*chip, c), device_id_type=MESH)
                cp.start()
                pend.append(cp.wait_send)
        for m, (_, kind, shape) in enumerate(BIG):
            for j, (cx, cy) in enumerate(chips):
                got = _piece_at(outs[m], kind, shape, 2 * cx + cy, c)
                pltpu.make_async_remote_copy(src_ref=got, dst_ref=got, send_sem=send.at[m, j], recv_sem=recv.at[m, j],
                                             device_id=(cx, cy, c), device_id_type=MESH).wait_recv()
                cp = pltpu.make_async_remote_copy(src_ref=got, dst_ref=got, send_sem=fsend.at[m, j],
                                                  recv_sem=frecv.at[m, j], device_id=sibling, device_id_type=MESH)
                cp.start()
                pend.append(cp.wait_send)
        for m, (_, kind, shape) in enumerate(BIG):
            for j, (cx, cy) in enumerate(chips):
                got = _piece_at(outs[m], kind, shape, 2 * cx + cy, 1 - c)
                pltpu.make_async_remote_copy(src_ref=got, dst_ref=got, send_sem=fsend.at[m, j], recv_sem=frecv.at[m, j],
                                             device_id=sibling, device_id_type=MESH).wait_recv()
        for w in pend:
            w()

    sem = lambda *s: pltpu.SemaphoreType.DMA(s)
    return pl.pallas_call(
        body, name="gather_weights",
        out_shape=[jax.ShapeDtypeStruct(shape, BF16) for _, _, shape in BIG],
        in_specs=[pl.BlockSpec(memory_space=pltpu.VMEM)] * nm,
        out_specs=[ANY] * nm,
        scratch_shapes=[pltpu.VMEM(s.shape, BF16) for s in shards] + [sem(nm, 3), sem(nm, 3), sem(nm, 3), sem(nm, 3),
                                                                         sem(nm, 2)],
        compiler_params=_params(),
    )(*shards)


def _rs1_sibling(grads):
    nm = len(BIG)

    def body(*refs):
        ins, outs = refs[:nm], refs[nm:2 * nm]
        send, recv = refs[2 * nm:]
        x, y, c, _ = _place()
        cps = []
        for m, (_, kind, shape) in enumerate(BIG):
            for s in range(N_CHIP):
                cp = pltpu.make_async_remote_copy(
                    src_ref=_piece_at(ins[m], kind, shape, s, 1 - c), dst_ref=outs[m].at[s],
                    send_sem=send.at[m, s], recv_sem=recv.at[m, s], device_id=(x, y, 1 - c), device_id_type=MESH)
                cp.start()
                cps.append(cp)
        for cp in cps:
            cp.wait()

    sem = lambda *s: pltpu.SemaphoreType.DMA(s)
    return pl.pallas_call(
        body, name="rs1_sibling",
        out_shape=[jax.ShapeDtypeStruct((N_CHIP,) + _piece(kind, shape), BF16) for _, kind, shape in BIG],
        in_specs=[ANY] * nm, out_specs=[ANY] * nm,
        scratch_shapes=[sem(nm, N_CHIP), sem(nm, N_CHIP)],
        compiler_params=_params(),
    )(*grads)


def _rs2_chips(grads, halves):
    nm = len(BIG)

    def body(*refs):
        gin, hin = refs[:nm], refs[nm:2 * nm]
        own, got = refs[2 * nm:3 * nm], refs[3 * nm:4 * nm]
        send, recv, lsem = refs[4 * nm:]
        x, y, c, chips = _place()
        s_me = 2 * x + y
        for m, (_, kind, shape) in enumerate(BIG):
            pr, pc = _piece(kind, shape)

            def scoped(a, b, m=m, kind=kind, shape=shape, pr=pr):
                loads = [pltpu.make_async_copy(_piece_at(gin[m], kind, shape, s, c), a.at[s], lsem.at[s])
                         for s in range(N_CHIP)]
                loads.append(pltpu.make_async_copy(hin[m], b, lsem.at[N_CHIP]))
                for cp in loads:
                    cp.start()
                for cp in loads:
                    cp.wait()
                step = _add_rows(pr)
                for s in range(N_CHIP):
                    def add(i, _, s=s):
                        r = pl.ds(pl.multiple_of(i * step, 16), step)
                        a[s, r, :] = (a[s, r, :].astype(F32) + b[s, r, :].astype(F32)).astype(BF16)
                        return 0

                    lax.fori_loop(0, pr // step, add, 0)
                waits = []
                for j, (cx, cy) in enumerate(chips):
                    cp = pltpu.make_async_remote_copy(src_ref=a.at[2 * cx + cy], dst_ref=got[m].at[j], send_sem=send.at[m, j],
                                                      recv_sem=recv.at[m, j], device_id=(cx, cy, c), device_id_type=MESH)
                    cp.start()
                    waits.append(cp.wait_send)
                cp = pltpu.make_async_copy(a.at[s_me], own[m], lsem.at[N_CHIP + 1])
                cp.start()
                waits.append(cp.wait)
                for w in waits:
                    w()

            pl.run_scoped(scoped, pltpu.VMEM((N_CHIP, pr, pc), BF16), pltpu.VMEM((N_CHIP, pr, pc), BF16))
        for m in range(nm):
            for j, (cx, cy) in enumerate(chips):
                pltpu.make_async_remote_copy(src_ref=got[m].at[j], dst_ref=got[m].at[j], send_sem=send.at[m, j],
                                             recv_sem=recv.at[m, j], device_id=(cx, cy, c), device_id_type=MESH).wait_recv()

    sem = lambda *s: pltpu.SemaphoreType.DMA(s)
    pieces = [_piece(kind, shape) for _, kind, shape in BIG]
    return pl.pallas_call(
        body, name="rs2_chips",
        out_shape=[jax.ShapeDtypeStruct(p, BF16) for p in pieces] + [jax.ShapeDtypeStruct((3,) + p, BF16) for p in pieces],
        in_specs=[ANY] * (2 * nm), out_specs=[ANY] * (2 * nm),
        scratch_shapes=[sem(nm, 3), sem(nm, 3), sem(N_CHIP + 2)],
        compiler_params=_params(),
    )(*grads, *halves)


def _rs3_finish(own, got):
    nm = len(BIG)

    def body(*refs):
        oin, gin = refs[:nm], refs[nm:2 * nm]
        outs = refs[2 * nm:3 * nm]
        send, recv, lsem = refs[3 * nm:]
        x, y, c, _ = _place()
        for m, (_, kind, shape) in enumerate(BIG):
            pr, pc = _piece(kind, shape)

            def scoped(a, g, f, m=m, pr=pr):
                loads = [pltpu.make_async_copy(oin[m], a, lsem.at[0]), pltpu.make_async_copy(gin[m], g, lsem.at[1])]
                for cp in loads:
                    cp.start()
                for cp in loads:
                    cp.wait()
                step = _add_rows(pr)

                def add(i, _):
                    r = pl.ds(pl.multiple_of(i * step, 16), step)
                    f[r, :] = ((a[r, :].astype(F32) + g[0, r, :].astype(F32)) + g[1, r, :].astype(F32)) + g[2, r, :].astype(F32)
                    return 0

                lax.fori_loop(0, pr // step, add, 0)
                dst = outs[m].at[pl.ds(pl.multiple_of(c * pr, 8), pr), :]
                mine = pltpu.make_async_copy(f, dst, lsem.at[2])
                mine.start()
                cp = pltpu.make_async_remote_copy(src_ref=f, dst_ref=dst, send_sem=send.at[m], recv_sem=recv.at[m],
                                                  device_id=(x, y, 1 - c), device_id_type=MESH)
                cp.start()
                cp.wait_send()
                mine.wait()

            pl.run_scoped(scoped, pltpu.VMEM((pr, pc), BF16), pltpu.VMEM((3, pr, pc), BF16), pltpu.VMEM((pr, pc), F32))
        for m, (_, kind, shape) in enumerate(BIG):
            pr, pc = _piece(kind, shape)
            dst = outs[m].at[pl.ds(pl.multiple_of((1 - c) * pr, 8), pr), :]
            pltpu.make_async_remote_copy(src_ref=dst, dst_ref=dst, send_sem=send.at[m], recv_sem=recv.at[m],
                                         device_id=(x, y, 1 - c), device_id_type=MESH).wait_recv()

    sem = lambda *s: pltpu.SemaphoreType.DMA(s)
    pieces = [_piece(kind, shape) for _, kind, shape in BIG]
    return pl.pallas_call(
        body, name="rs3_finish",
        out_shape=[jax.ShapeDtypeStruct((2 * pr, pc), F32) for pr, pc in pieces],
        in_specs=[ANY] * (2 * nm), out_specs=[ANY] * nm,
        scratch_shapes=[sem(nm), sem(nm), sem(3)],
        compiler_params=_params(),
    )(*own, *got)


def _cond_fwd(c_all, w_shard, b_shard):
    def body(c_ref, w_ref, b_ref, act_ref, mod_ref):
        cv = c_ref[...]
        act = cv * _sig(cv)
        act_ref[...] = act
        mod_ref[...] = _dot(act.astype(BF16), w_ref[...].astype(BF16)) + b_ref[...]

    return pl.pallas_call(
        body, name="cond_fwd",
        out_shape=[jax.ShapeDtypeStruct(c_all.shape, F32), jax.ShapeDtypeStruct((c_all.shape[0], w_shard.shape[1]), F32)],
        compiler_params=_params(),
    )(c_all, w_shard, b_shard)


def _cond_bwd(act_t, dmod_shard):
    k, n = act_t.shape[0], dmod_shard.shape[1]

    def body(a_ref, d_ref, o_ref):
        acc = a_ref[:, 0:1] * d_ref[0:1, :]
        for e in range(1, N_DEV):
            acc += a_ref[:, e:e + 1] * d_ref[e:e + 1, :]
        o_ref[...] = acc

    tr = 256
    return pl.pallas_call(
        body, name="cond_bwd", grid=(k // tr,),
        in_specs=[pl.BlockSpec((tr, N_DEV), lambda i: (i, 0)), _full(dmod_shard)],
        out_specs=pl.BlockSpec((tr, n), lambda i: (i, 0)),
        out_shape=jax.ShapeDtypeStruct((k, n), F32),
        compiler_params=_params(("arbitrary",)),
    )(act_t, dmod_shard)


def _sum_blocks(allp):
    def body(a_ref, o_ref):
        acc = a_ref[0:PACK_ROWS, :]
        for d in range(1, N_DEV):
            acc += a_ref[d * PACK_ROWS:(d + 1) * PACK_ROWS, :]
        o_ref[...] = acc

    return pl.pallas_call(
        body, name="sum_small", out_shape=jax.ShapeDtypeStruct((PACK_ROWS, PACK_COLS), F32), compiler_params=_params(),
    )(allp)


def _adamw(name, w, g, m, v):
    r, cc = w.shape
    tr = r
    for cand in (256, 128, 64, 32, 16, 8):
        if r % cand == 0:
            tr = cand
            break
    bc1 = 1.0 - ADAM_B1 ** ADAM_STEP
    bc2 = 1.0 - ADAM_B2 ** ADAM_STEP

    def body(w_ref, g_ref, m_ref, v_ref, d_ref, nm_ref, nv_ref):
        gv = g_ref[...]
        nm = ADAM_B1 * m_ref[...] + (1.0 - ADAM_B1) * gv
        nv = ADAM_B2 * v_ref[...] + (1.0 - ADAM_B2) * (gv * gv)
        nm_ref[...] = nm
        nv_ref[...] = nv
        d_ref[...] = -ADAM_LR * ((nm / bc1) / (jnp.sqrt(nv / bc2) + ADAM_EPS) + ADAM_WD * w_ref[...])

    spec = pl.BlockSpec((tr, cc), lambda i: (i, 0))
    return pl.pallas_call(
        body, name=name, grid=(r // tr,), in_specs=[spec] * 4, out_specs=[spec] * 3,
        out_shape=[jax.ShapeDtypeStruct((r, cc), F32)] * 3, compiler_params=_params(("arbitrary",)),
    )(w, g, m, v)


def _pack(fields, layout):
    parts = [fields[name].reshape(-1).astype(F32) if name in fields else jnp.zeros((n,), F32) for name, n in layout]
    used = sum(n for _, n in layout)
    parts.append(jnp.zeros((PACK_ROWS * PACK_COLS - used,), F32))
    return jnp.concatenate(parts).reshape(PACK_ROWS, PACK_COLS)


def _unpack(flat, layout):
    flat = flat.reshape(-1)
    out, o = {}, 0
    for name, n in layout:
        out[name] = flat[o:o + n]
        o += n
    return out


def kernel(x, c, w_cond, b_cond, w_in, b_in, ssm_lambda_re, ssm_lambda_im, ssm_log_dt, ssm_b_re, ssm_b_im, ssm_c_re, ssm_c_im, ssm_d, ssm_glu_w_a, ssm_glu_w_b, cv_dw_w, cv_dw_b, cv_ln_g, cv_ln_b, cv_w_pw, w_out, ln1_g, ln1_b, ffn_w_up, ffn_dw_w, ffn_dw_b, ffn_w_down, ln2_g, ln2_b, loss_target, m_w_cond, m_b_cond, m_w_in, m_b_in, m_ssm_lambda_re, m_ssm_lambda_im, m_ssm_log_dt, m_ssm_b_re, m_ssm_b_im, m_ssm_c_re, m_ssm_c_im, m_ssm_d, m_ssm_glu_w_a, m_ssm_glu_w_b, m_cv_dw_w, m_cv_dw_b, m_cv_ln_g, m_cv_ln_b, m_cv_w_pw, m_w_out, m_ln1_g, m_ln1_b, m_ffn_w_up, m_ffn_dw_w, m_ffn_dw_b, m_ffn_w_down, m_ln2_g, m_ln2_b, v_w_cond, v_b_cond, v_w_in, v_b_in, v_ssm_lambda_re, v_ssm_lambda_im, v_ssm_log_dt, v_ssm_b_re, v_ssm_b_im, v_ssm_c_re, v_ssm_c_im, v_ssm_d, v_ssm_glu_w_a, v_ssm_glu_w_b, v_cv_dw_w, v_cv_dw_b, v_cv_ln_g, v_cv_ln_b, v_cv_w_pw, v_w_out, v_ln1_g, v_ln1_b, v_ffn_w_up, v_ffn_dw_w, v_ffn_dw_b, v_ffn_w_down, v_ln2_g, v_ln2_b):
    given = locals()
    a = {n: given[n] for n in INPUTS}
    xi, yi, ci = lax.axis_index("x"), lax.axis_index("y"), lax.axis_index("c")
    s_me = 2 * xi + yi
    e_me = 4 * xi + 2 * yi + ci

    first = jnp.concatenate([
        jnp.concatenate([a["c"], jnp.zeros((7, D_MODEL), F32)], axis=0),
        jnp.concatenate([a["cv_dw_w"].reshape(-1), a["ffn_dw_w"].reshape(-1)]).reshape(8, D_MODEL)], axis=0)
    first_all = _allgather("gather_c", first).reshape(N_DEV, 16, D_MODEL)
    c_all = first_all[:, 0, :]
    dw_all = first_all[0::2, 8:, :].reshape(N_CHIP, 8 * D_MODEL)
    n_cv = CONV_KERNEL * CONV_WIDTH // N_CHIP
    cv_dw_full = dw_all[:, :n_cv].reshape(N_CHIP, CONV_KERNEL, CONV_WIDTH // N_CHIP).transpose(1, 0, 2) \
        .reshape(CONV_KERNEL, CONV_WIDTH)
    ffn_dw_full = dw_all[:, n_cv:].reshape(N_CHIP, FFN_KERNEL, 2 * FFN_HIDDEN // N_CHIP).transpose(1, 0, 2) \
        .reshape(FFN_KERNEL, 2 * FFN_HIDDEN)
    ncols = N_COND * D_MODEL // N_CHIP
    b_cond_shard = lax.dynamic_slice(a["b_cond"], (0, s_me * ncols), (1, ncols))
    c_act_all, modp = _cond_fwd(c_all, a["w_cond"][0], b_cond_shard)
    modp_all = _allgather("gather_mod", modp).reshape(N_DEV, N_DEV, ncols)[0::2]
    mod_e = lax.dynamic_index_in_dim(modp_all, e_me, axis=1, keepdims=False).reshape(N_COND, D_MODEL)
    modv = jnp.concatenate([mod_e, jnp.zeros((2, D_MODEL), F32)], axis=0)

    wb = dict(zip([n for n, _, _ in BIG], _gather_weights([a[n][0] for n, _, _ in BIG])))
    sp = {n: a[n][0] for n in ("b_in", "ssm_lambda_re", "ssm_lambda_im", "ssm_log_dt", "ssm_b_re", "ssm_b_im",
                               "ssm_c_re", "ssm_c_im", "ssm_d", "cv_dw_b", "cv_ln_g", "cv_ln_b", "ln1_g", "ln1_b",
                               "ffn_dw_b", "ln2_g", "ln2_b")}
    sp["cv_dw_w"] = cv_dw_full
    sp["ffn_dw_w"] = ffn_dw_full
    gx, dbig, small = _local_step(a["x"][0], a["loss_target"][0], modv, wb, sp)

    small["c_act"] = lax.dynamic_index_in_dim(c_act_all, e_me, axis=0, keepdims=False)
    packed_all = _allgather("gather_small", _pack(small, PACK))
    tot = _unpack(_sum_blocks(packed_all), PACK)
    rows = packed_all.reshape(N_DEV, PACK_ROWS * PACK_COLS)
    dmod_all = rows[:, 0:N_COND * D_MODEL]
    act_all = rows[:, N_COND * D_MODEL:(N_COND + 1) * D_MODEL]
    g_w_cond = _cond_bwd(act_all.T, lax.dynamic_slice(dmod_all, (0, s_me * ncols), (N_DEV, ncols)))

    glist = [dbig[n] for n, _, _ in BIG]
    halves = _rs1_sibling(glist)
    r2 = _rs2_chips(glist, halves)
    gsh = _rs3_finish(r2[:len(BIG)], r2[len(BIG):])

    grads = {"w_cond": g_w_cond[None], "b_cond": tot["dmod"].reshape(1, -1)}
    for (n, kind, shape), g in zip(BIG, gsh):
        grads[n] = g.reshape(a[n].shape)
    for n in ("b_in", "ssm_lambda_re", "ssm_lambda_im", "ssm_log_dt", "ssm_b_re", "ssm_b_im", "ssm_c_re", "ssm_c_im",
              "ssm_d", "cv_dw_b", "cv_ln_g", "cv_ln_b", "ln1_g", "ln1_b", "ffn_dw_b", "ln2_g", "ln2_b"):
        grads[n] = tot[n].reshape(a[n].shape)
    wcv = CONV_WIDTH // N_CHIP
    grads["cv_dw_w"] = lax.dynamic_slice(tot["cv_dw_w"].reshape(CONV_KERNEL, CONV_WIDTH), (0, s_me * wcv),
                                         (CONV_KERNEL, wcv)).reshape(a["cv_dw_w"].shape)
    wff = 2 * FFN_HIDDEN // N_CHIP
    grads["ffn_dw_w"] = lax.dynamic_slice(tot["ffn_dw_w"].reshape(FFN_KERNEL, 2 * FFN_HIDDEN), (0, s_me * wff),
                                          (FFN_KERNEL, wff)).reshape(a["ffn_dw_w"].shape)

    delta, new_m, new_v = {}, {}, {}
    for n in ["w_cond"] + [n for n, _, _ in BIG]:
        d, nm_, nv_ = _adamw("adamw_" + n, a[n][0], grads[n][0], a["m_" + n][0], a["v_" + n][0])
        delta[n], new_m[n], new_v[n] = d[None], nm_[None], nv_[None]
    upd = [n for n, _ in SMALL_UPD]
    d, nm_, nv_ = _adamw("adamw_small", _pack({n: a[n] for n in upd}, SMALL_UPD), _pack({n: grads[n] for n in upd}, SMALL_UPD),
                         _pack({n: a["m_" + n] for n in upd}, SMALL_UPD), _pack({n: a["v_" + n] for n in upd}, SMALL_UPD))
    for dst, flat in ((delta, d), (new_m, nm_), (new_v, nv_)):
        for n, val in _unpack(flat, SMALL_UPD).items():
            dst[n] = val.reshape(a[n].shape)

    loss = tot["loss"].reshape(())
    return (loss, gx[None], *[grads[n] for n in WEIGHTS], *[delta[n] for n in WEIGHTS],
            *[new_m[n] for n in WEIGHTS], *[new_v[n] for n in WEIGHTS])
```

```python
import functools
import math

import jax
import jax.numpy as jnp
from jax import lax
from jax.experimental import pallas as pl
from jax.experimental.pallas import tpu as pltpu

F32 = jnp.float32
BF16 = jnp.bfloat16

D_MODEL = 1024
SSM_WIDTH = 512
SSM_GROUP = 16
SSM_GROUPS = 32
SSM_STATE = 64
CONV_WIDTH = 512
CONV_KERNEL = 31
FFN_HIDDEN = 2816
FFN_KERNEL = 3
IN_PROJ_WIDTH = 3584
N_COND = 6
ALPHA = 2.0 ** 0.25
LN_EPS = 1e-5
ADAM_LR, ADAM_B1, ADAM_B2, ADAM_EPS, ADAM_WD, ADAM_STEP = 0.001, 0.9, 0.999, 1e-08, 0.01, 10

N_DEV = 8
N_CHIP = 4
LANES = 128
SSM_CHUNK = 16
LANE_GROUPS = LANES // SSM_GROUP
N_LANE_BLOCKS = SSM_WIDTH // LANES
STATE_COLS = LANE_GROUPS * SSM_STATE
CHUNK_COLS = SSM_CHUNK * LANES
CONV_HALO = 32
VMEM_LIMIT = 56 * 1024 * 1024
MESH = pl.DeviceIdType.MESH

BIG = (
    ("w_in", "col", (D_MODEL, IN_PROJ_WIDTH)),
    ("ssm_glu_w_a", "col", (SSM_WIDTH, D_MODEL)),
    ("ssm_glu_w_b", "col", (SSM_WIDTH, D_MODEL)),
    ("cv_w_pw", "col", (CONV_WIDTH, D_MODEL)),
    ("w_out", "row", (D_MODEL, D_MODEL)),
    ("ffn_w_up", "col", (D_MODEL, 2 * FFN_HIDDEN)),
    ("ffn_w_down", "row", (FFN_HIDDEN, D_MODEL)),
)

EARLY = (0, 1, 2, 3, 4)
LATE = (5, 6)

WEIGHTS = ['w_cond', 'b_cond', 'w_in', 'b_in', 'ssm_lambda_re', 'ssm_lambda_im', 'ssm_log_dt', 'ssm_b_re', 'ssm_b_im',
           'ssm_c_re', 'ssm_c_im', 'ssm_d', 'ssm_glu_w_a', 'ssm_glu_w_b', 'cv_dw_w', 'cv_dw_b', 'cv_ln_g', 'cv_ln_b',
           'cv_w_pw', 'w_out', 'ln1_g', 'ln1_b', 'ffn_w_up', 'ffn_dw_w', 'ffn_dw_b', 'ffn_w_down', 'ln2_g', 'ln2_b']
INPUTS = ['x', 'c'] + WEIGHTS + ['loss_target'] + ['m_' + n for n in WEIGHTS] + ['v_' + n for n in WEIGHTS]

PACK = (
    ("dmod", N_COND * D_MODEL), ("c_act", D_MODEL), ("b_in", IN_PROJ_WIDTH),
    ("ssm_lambda_re", SSM_GROUPS * SSM_STATE), ("ssm_lambda_im", SSM_GROUPS * SSM_STATE), ("ssm_log_dt", SSM_GROUPS),
    ("ssm_b_re", SSM_GROUPS * SSM_STATE * SSM_GROUP), ("ssm_b_im", SSM_GROUPS * SSM_STATE * SSM_GROUP),
    ("ssm_c_re", SSM_GROUPS * SSM_STATE * SSM_GROUP), ("ssm_c_im", SSM_GROUPS * SSM_STATE * SSM_GROUP),
    ("ssm_d", SSM_GROUPS * SSM_GROUP), ("cv_dw_w", CONV_KERNEL * CONV_WIDTH), ("cv_dw_b", CONV_WIDTH),
    ("cv_ln_g", CONV_WIDTH), ("cv_ln_b", CONV_WIDTH), ("ln1_g", D_MODEL), ("ln1_b", D_MODEL),
    ("ffn_dw_w", FFN_KERNEL * 2 * FFN_HIDDEN), ("ffn_dw_b", 2 * FFN_HIDDEN), ("ln2_g", D_MODEL), ("ln2_b", D_MODEL),
    ("loss", 1),
)
PACK_COLS = 1024
PACK_ROWS = 192
assert sum(n for _, n in PACK) <= PACK_ROWS * PACK_COLS

SMALL_UPD = (
    ("b_cond", N_COND * D_MODEL), ("b_in", IN_PROJ_WIDTH),
    ("ssm_lambda_re", SSM_GROUPS * SSM_STATE), ("ssm_lambda_im", SSM_GROUPS * SSM_STATE), ("ssm_log_dt", SSM_GROUPS),
    ("ssm_b_re", SSM_GROUPS * SSM_STATE * SSM_GROUP), ("ssm_b_im", SSM_GROUPS * SSM_STATE * SSM_GROUP),
    ("ssm_c_re", SSM_GROUPS * SSM_STATE * SSM_GROUP), ("ssm_c_im", SSM_GROUPS * SSM_STATE * SSM_GROUP),
    ("ssm_d", SSM_GROUPS * SSM_GROUP), ("cv_dw_w", CONV_KERNEL * CONV_WIDTH // N_CHIP), ("cv_dw_b", CONV_WIDTH),
    ("cv_ln_g", CONV_WIDTH), ("cv_ln_b", CONV_WIDTH), ("ln1_g", D_MODEL), ("ln1_b", D_MODEL),
    ("ffn_dw_w", FFN_KERNEL * 2 * FFN_HIDDEN // N_CHIP), ("ffn_dw_b", 2 * FFN_HIDDEN), ("ln2_g", D_MODEL),
    ("ln2_b", D_MODEL),
)
assert sum(n for _, n in SMALL_UPD) <= PACK_ROWS * PACK_COLS


def _params(sem=None, **kw):
    return pltpu.CompilerParams(dimension_semantics=sem, vmem_limit_bytes=VMEM_LIMIT, **kw)


def _ln_stats(x):
    mu = jnp.mean(x, axis=-1, keepdims=True)
    xc = x - mu
    var = jnp.mean(xc * xc, axis=-1, keepdims=True)
    rstd = lax.rsqrt(var + LN_EPS)
    return xc * rstd, rstd


def _ln_bwd(dxhat, xhat, rstd):
    m1 = jnp.mean(dxhat, axis=-1, keepdims=True)
    m2 = jnp.mean(dxhat * xhat, axis=-1, keepdims=True)
    return rstd * (dxhat - m1 - xhat * m2)


def _sig(x):
    return 1.0 / (1.0 + jnp.exp(-x))


def _gelu(x):
    return 0.5 * x * (1.0 + lax.erf(x * (1.0 / math.sqrt(2.0))))


def _dgelu(x):
    return 0.5 * (1.0 + lax.erf(x * (1.0 / math.sqrt(2.0)))) + x * jnp.exp(-0.5 * x * x) * (1.0 / math.sqrt(2.0 * math.pi))


def _gelu_and_grad(x):
    er = lax.erf(x * (1.0 / math.sqrt(2.0)))
    cdf = 0.5 * (1.0 + er)
    return x * cdf, cdf + x * jnp.exp(-0.5 * x * x) * (1.0 / math.sqrt(2.0 * math.pi))


def _colsum(a):
    return jnp.sum(a, axis=0, keepdims=True)


def _fill_rotations(buf, rot, rows):
    for r in range(1, 8):
        rot[r - 1] = buf[pl.ds(r, rows), :]


def _rows_at(buf, rot, offset, tb):
    q, r = divmod(offset, 8)
    if r == 0:
        return buf[pl.ds(8 * q, tb), :]
    return rot[r - 1, pl.ds(8 * q, tb), :]


def _dot(a, b):
    return jnp.dot(a, b, preferred_element_type=F32)


def _dot_nt(a, b):
    return lax.dot_general(a, b, (((1,), (1,)), ((), ())), preferred_element_type=F32)


def _dot_tn(a, b):
    return lax.dot_general(a, b, (((0,), (0,)), ((), ())), preferred_element_type=F32)


def _load_once(src, dst, sem):
    cp = pltpu.make_async_copy(src, dst, sem)
    cp.start()
    cp.wait()


def _full(a):
    nd = a.ndim
    return pl.BlockSpec(a.shape, lambda *_: (0,) * nd)


ANY = pl.BlockSpec(memory_space=pl.ANY)


def _place():
    x, y, c = lax.axis_index("x"), lax.axis_index("y"), lax.axis_index("c")
    chips = [(1 - x, y), (x, 1 - y), (1 - x, 1 - y)]
    return x, y, c, chips


def _piece(kind, shape):
    r, cc = shape
    return (r // 2, cc // N_CHIP) if kind == "col" else (r // (2 * N_CHIP), cc)


def _piece_at(ref, kind, shape, s, k):
    pr, pc = _piece(kind, shape)
    if kind == "col":
        return ref.at[pl.ds(k * pr, pr), pl.ds(pl.multiple_of(s * pc, LANES), pc)]
    return ref.at[pl.ds(pl.multiple_of((2 * s + k) * pr, 16), pr), :]


def _late_gather_start(sh, full, send, recv):
    x, y, c, chips = _place()
    for i, m in enumerate(LATE):
        _, kind, shape = BIG[m]
        pr, _ = _piece(kind, shape)
        for j, chip in enumerate(chips):
            pltpu.make_async_remote_copy(
                src_ref=sh[i].at[pl.ds(pl.multiple_of(c * pr, 16), pr), :], dst_ref=_piece_at(full[i], kind, shape, 2 * x + y, c),
                send_sem=send.at[i, j], recv_sem=recv.at[i, j], device_id=(*chip, c), device_id_type=MESH).start()


def _late_gather_finish(sh, full, send, recv, fsend, frecv):
    x, y, c, chips = _place()
    sibling = (x, y, 1 - c)
    waits = []
    for i, m in enumerate(LATE):
        _, kind, shape = BIG[m]
        pr, _ = _piece(kind, shape)
        for j, (cx, cy) in enumerate(chips):
            got = _piece_at(full[i], kind, shape, 2 * cx + cy, c)
            first = pltpu.make_async_remote_copy(
                src_ref=sh[i].at[pl.ds(pl.multiple_of(c * pr, 16), pr), :], dst_ref=got, send_sem=send.at[i, j],
                recv_sem=recv.at[i, j], device_id=(cx, cy, c), device_id_type=MESH)
            first.wait_recv()
            fwd = pltpu.make_async_remote_copy(src_ref=got, dst_ref=got, send_sem=fsend.at[i, j], recv_sem=frecv.at[i, j],
                                               device_id=sibling, device_id_type=MESH)
            fwd.start()
            waits += [first.wait_send, fwd.wait_send]
    for i, m in enumerate(LATE):
        _, kind, shape = BIG[m]
        for j, (cx, cy) in enumerate(chips):
            got = _piece_at(full[i], kind, shape, 2 * cx + cy, 1 - c)
            pltpu.make_async_remote_copy(src_ref=got, dst_ref=got, send_sem=fsend.at[i, j], recv_sem=frecv.at[i, j],
                                         device_id=sibling, device_id_type=MESH).wait_recv()
    for w in waits:
        w()


def _late_scatter(dw, got, send, recv):
    x, y, c, _ = _place()
    cps = []
    for i, m in enumerate(LATE):
        _, kind, shape = BIG[m]
        for r in range(1, N_DEV):
            tx, ty, tc = (1 - x if r & 4 else x), (1 - y if r & 2 else y), (1 - c if r & 1 else c)
            cps.append(pltpu.make_async_remote_copy(
                src_ref=_piece_at(dw[i], kind, shape, 2 * tx + ty, tc), dst_ref=got[i].at[r - 1],
                send_sem=send.at[i, r - 1], recv_sem=recv.at[i, r - 1], device_id=(tx, ty, tc), device_id_type=MESH))
    return cps


def _f1_inproj(x, modv, b_in, w_in, tb):
    t = x.shape[0]
    chunks = [(j * 512, 512) for j in range(IN_PROJ_WIDTH // 512)]

    def body(x_ref, modv_ref, b_ref, w_hbm, u4_ref, prest_ref, h_ref, w_v, sem):
        @pl.when(pl.program_id(0) == 0)
        def _():
            _load_once(w_hbm, w_v, sem)

        xn, _ = _ln_stats(x_ref[...])
        h = (xn * (1.0 + modv_ref[1:2, :]) + modv_ref[0:1, :]).astype(BF16)
        h_ref[...] = h
        for c0, cw in chunks:
            p = _dot(h, w_v[:, c0:c0 + cw]) + b_ref[:, c0:c0 + cw]
            if c0 == 0:
                for b in range(N_LANE_BLOCKS):
                    u4_ref[b] = p[:, b * LANES:(b + 1) * LANES].astype(BF16)
            else:
                prest_ref[:, c0 - SSM_WIDTH:c0 - SSM_WIDTH + cw] = p

    return pl.pallas_call(
        body, name="f1_inproj", grid=(t // tb,),
        in_specs=[pl.BlockSpec((tb, D_MODEL), lambda i: (i, 0)), _full(modv), _full(b_in), ANY],
        out_specs=[pl.BlockSpec((N_LANE_BLOCKS, tb, LANES), lambda i: (0, i, 0)),
                   pl.BlockSpec((tb, IN_PROJ_WIDTH - SSM_WIDTH), lambda i: (i, 0)),
                   pl.BlockSpec((tb, D_MODEL), lambda i: (i, 0))],
        out_shape=[jax.ShapeDtypeStruct((N_LANE_BLOCKS, t, LANES), BF16),
                   jax.ShapeDtypeStruct((t, IN_PROJ_WIDTH - SSM_WIDTH), F32),
                   jax.ShapeDtypeStruct((t, D_MODEL), BF16)],
        scratch_shapes=[pltpu.VMEM(w_in.shape, BF16), pltpu.SemaphoreType.DMA],
        compiler_params=_params(("arbitrary",)),
    )(x, modv, b_in, w_in)


def _s5_build(lam_re, lam_im, log_dt, b_re, b_im, c_re, c_im, d):
    hi = lax.Precision.HIGHEST
    el, g, n, p, nb = SSM_CHUNK, SSM_GROUPS, SSM_STATE, SSM_GROUP, N_LANE_BLOCKS
    lr = jnp.minimum(lam_re, -1e-4)
    li = lam_im
    dt = jnp.exp(log_dt)[:, None]
    mag = jnp.exp(lr * dt)
    ang = li * dt
    lbr, lbi = mag * jnp.cos(ang), mag * jnp.sin(ang)
    num_r, num_i = lbr - 1.0, lbi
    den = lr * lr + li * li
    coef_r = (num_r * lr + num_i * li) / den
    coef_i = (num_i * lr - num_r * li) / den
    bbar_r = coef_r[..., None] * b_re - coef_i[..., None] * b_im
    bbar_i = coef_r[..., None] * b_im + coef_i[..., None] * b_re
    k = jnp.arange(el + 1, dtype=F32)[:, None, None]
    pmag = jnp.exp(k * (lr * dt)[None])
    pr, pi = pmag * jnp.cos(k * ang[None]), pmag * jnp.sin(k * ang[None])
    car = c_re[None] * pr[:, :, None, :] - c_im[None] * pi[:, :, None, :]
    cai = c_re[None] * pi[:, :, None, :] + c_im[None] * pr[:, :, None, :]
    kern = (jnp.einsum("kgpn,gnq->kgqp", car[:el], bbar_r, precision=hi)
            - jnp.einsum("kgpn,gnq->kgqp", cai[:el], bbar_i, precision=hi))
    kern = kern.at[0].add(jnp.eye(p, dtype=F32)[None] * d[:, None, :])
    kc = kern.reshape(el, g * p, p)
    bt_r = bbar_r.transpose(0, 2, 1)[None]
    bt_i = bbar_i.transpose(0, 2, 1)[None]
    rev = el - 1 - jnp.arange(el)
    qr, qi = pr[rev][:, :, None, :], pi[rev][:, :, None, :]
    sw_r = (qr * bt_r - qi * bt_i).reshape(el, g * p, n)
    sw_i = (qr * bt_i + qi * bt_r).reshape(el, g * p, n)
    sg_r = car[1:].reshape(el, g * p, n)
    sg_i = (-cai[1:]).reshape(el, g * p, n)
    a = jnp.stack([pr[el].reshape(nb, LANE_GROUPS * n), pi[el].reshape(nb, LANE_GROUPS * n)], axis=1)
    return kc, sw_r, sw_i, sg_r, sg_i, a


def _expand(src, reps):
    rows, w = src.shape
    cols = reps * w
    r = lax.broadcasted_iota(jnp.int32, (w, cols), 0)
    c = lax.broadcasted_iota(jnp.int32, (w, cols), 1)
    rep = (r == (c & (w - 1))).astype(BF16)
    out = _dot(src.astype(BF16), rep)
    rg = lax.broadcasted_iota(jnp.int32, (rows, cols), 0) // SSM_GROUP
    cg = lax.broadcasted_iota(jnp.int32, (rows, cols), 1) // w
    return jnp.where(rg == cg, out, 0.0).astype(BF16)


def _fold(x, w):
    rows, cols = x.shape
    rg = lax.broadcasted_iota(jnp.int32, (rows, cols), 0) // SSM_GROUP
    cg = lax.broadcasted_iota(jnp.int32, (rows, cols), 1) // w
    x = jnp.where(rg == cg, x, 0.0)
    while cols > LANES:
        x = x[:, :cols // 2] + x[:, cols // 2:]
        cols //= 2
    s = LANES // 2
    while s >= w:
        x = x + pltpu.roll(x, s, axis=1)
        s //= 2
    return x[:, :w]


def _build_maps(s_ref, dst):
    for j in range(SSM_CHUNK):
        dst[j * LANES:(j + 1) * LANES, :] = _expand(s_ref[j], LANE_GROUPS)


def _build_toeplitz(kc_ref, dst):
    dst[...] = jnp.zeros_like(dst)
    for d in range(SSM_CHUNK):
        blk = _expand(kc_ref[d], LANE_GROUPS)
        for ji in range(SSM_CHUNK - d):
            jo = ji + d
            dst[ji * LANES:(ji + 1) * LANES, jo * LANES:(jo + 1) * LANES] = blk


def _cblk(w):
    return pl.BlockSpec((SSM_CHUNK, LANES, w), lambda b: (0, b, 0))


def _s5a_state(u2, sw_r, sw_i, a8):
    nb, nc, _ = u2.shape
    sc = STATE_COLS

    def body(u_ref, swr_ref, swi_ref, a_ref, hr_ref, hi_ref, w_s, xr_s, xi_s):
        u = u_ref[0]
        _build_maps(swr_ref, w_s)
        xr_s[...] = _dot(u, w_s[...])
        _build_maps(swi_ref, w_s)
        xi_s[...] = _dot(u, w_s[...])
        ar = a_ref[0, 0:1, :]
        ai = a_ref[0, 1:2, :]

        def step(c, carry):
            hr, hi = carry
            hr_ref[0, pl.ds(c, 1), :] = hr
            hi_ref[0, pl.ds(c, 1), :] = hi
            xr = xr_s[pl.ds(c, 1), :]
            xi = xi_s[pl.ds(c, 1), :]
            return ar * hr - ai * hi + xr, ar * hi + ai * hr + xi

        z = jnp.zeros((1, sc), F32)
        lax.fori_loop(0, nc, step, (z, z))

    return pl.pallas_call(
        body, name="s5a_state", grid=(nb,),
        in_specs=[pl.BlockSpec((1, nc, CHUNK_COLS), lambda b: (b, 0, 0)), _cblk(SSM_STATE), _cblk(SSM_STATE),
                  pl.BlockSpec((1, 8, sc), lambda b: (b, 0, 0))],
        out_specs=[pl.BlockSpec((1, nc, sc), lambda b: (b, 0, 0))] * 2,
        out_shape=[jax.ShapeDtypeStruct((nb, nc, sc), F32)] * 2,
        scratch_shapes=[pltpu.VMEM((CHUNK_COLS, sc), BF16), pltpu.VMEM((nc, sc), F32), pltpu.VMEM((nc, sc), F32)],
        compiler_params=_params(("arbitrary",)),
    )(u2, sw_r, sw_i, a8)


def _s5b_out(u2, kc, sg_r, sg_i, hr, hi):
    nb, nc, _ = u2.shape
    sc = STATE_COLS
    cw = 512

    def body(u_ref, kc_ref, sgr_ref, sgi_ref, hr_ref, hi_ref, y_ref, tm_s, gr_s, gi_s):
        _build_toeplitz(kc_ref, tm_s)
        _build_maps(sgr_ref, gr_s)
        _build_maps(sgi_ref, gi_s)
        u = u_ref[0]
        h_r = hr_ref[0].astype(BF16)
        h_i = hi_ref[0].astype(BF16)
        for j in range(CHUNK_COLS // cw):
            cs = slice(j * cw, (j + 1) * cw)
            y_ref[0, :, cs] = _dot(u, tm_s[:, cs]) + _dot_nt(h_r, gr_s[cs, :]) + _dot_nt(h_i, gi_s[cs, :])

    return pl.pallas_call(
        body, name="s5b_out", grid=(nb,),
        in_specs=[pl.BlockSpec((1, nc, CHUNK_COLS), lambda b: (b, 0, 0)), _cblk(SSM_GROUP), _cblk(SSM_STATE),
                  _cblk(SSM_STATE), pl.BlockSpec((1, nc, sc), lambda b: (b, 0, 0)),
                  pl.BlockSpec((1, nc, sc), lambda b: (b, 0, 0))],
        out_specs=pl.BlockSpec((1, nc, CHUNK_COLS), lambda b: (b, 0, 0)),
        out_shape=jax.ShapeDtypeStruct((nb, nc, CHUNK_COLS), F32),
        scratch_shapes=[pltpu.VMEM((CHUNK_COLS, CHUNK_COLS), BF16), pltpu.VMEM((CHUNK_COLS, sc), BF16),
                        pltpu.VMEM((CHUNK_COLS, sc), BF16)],
        compiler_params=_params(("arbitrary",)),
    )(u2, kc, sg_r, sg_i, hr, hi)


def _f4_mixer(ys4, prest, x, modv, cvv, cw32, w_a, w_b, w_pw, w_out, late_sh, late_full, tb):
    t = x.shape[0]
    hb = tb // CONV_HALO
    nt = t // tb
    nl = len(LATE)

    def body(ys_ref, pr_ref, halo_ref, x_ref, modv_ref, cvv_ref, cw_ref, wa_ref, wb_ref, wpw_ref, wout_ref, *rest):
        sh, full = rest[:nl], rest[2 * nl:3 * nl]
        r1_ref, ya_ref, yb_ref, ycv_ref, vc_ref, yg_ref, vs_ref, mg_ref, vbuf, vrot, send, recv, fsend, frecv = rest[3 * nl:]
        i = pl.program_id(0)

        @pl.when(i == 0)
        def _():
            _late_gather_start(sh, full, send, recv)

        ys = jnp.concatenate([ys_ref[b] for b in range(N_LANE_BLOCKS)], axis=-1)
        yg = _gelu(ys).astype(BF16)
        yg_ref[...] = yg
        ya = _dot(yg, wa_ref[...])
        yb = _dot(yg, wb_ref[...])
        ya_ref[...] = ya.astype(BF16)
        yb_ref[...] = yb.astype(BF16)
        yssm = ya * _sig(yb)
        hv = halo_ref[:, 0:CONV_WIDTH] * _sig(halo_ref[:, CONV_WIDTH:2 * CONV_WIDTH])
        vbuf[0:CONV_HALO, :] = jnp.where(i == 0, 0.0, hv)
        vbuf[CONV_HALO:, :] = pr_ref[:, 0:CONV_WIDTH] * _sig(pr_ref[:, CONV_WIDTH:2 * CONV_WIDTH])
        _fill_rotations(vbuf, vrot, tb + CONV_HALO - 8)
        acc = jnp.zeros((tb, CONV_WIDTH), F32)
        for k in range(CONV_KERNEL):
            acc += _rows_at(vbuf, vrot, CONV_HALO - CONV_KERNEL + 1 + k, tb) * cw_ref[k:k + 1, :]
        vc = acc + cvv_ref[0:1, :]
        vc_ref[...] = vc
        xh, _ = _ln_stats(vc)
        vl = xh * cvv_ref[1:2, :] + cvv_ref[2:3, :]
        vs = (vl * _sig(vl)).astype(BF16)
        vs_ref[...] = vs
        ycv = _dot(vs, wpw_ref[...])
        ycv_ref[...] = ycv.astype(BF16)
        gs = pr_ref[:, 2 * CONV_WIDTH:2 * CONV_WIDTH + D_MODEL]
        gc = pr_ref[:, 2 * CONV_WIDTH + D_MODEL:]
        merged = (_sig(gs) * yssm + _sig(gc) * ycv).astype(BF16)
        mg_ref[...] = merged
        ym = _dot(merged, wout_ref[...])
        r1_ref[...] = ALPHA * x_ref[...] + modv_ref[2:3, :] * ym

        @pl.when(i == nt - 1)
        def _():
            _late_gather_finish(sh, full, send, recv, fsend, frecv)

    tok = lambda w: pl.BlockSpec((tb, w), lambda i: (i, 0))
    sem = pltpu.SemaphoreType.DMA((nl, 3))
    n_in = 11
    return pl.pallas_call(
        body, name="f4_mixer", grid=(nt,),
        in_specs=[pl.BlockSpec((N_LANE_BLOCKS, tb, LANES), lambda i: (0, i, 0)), tok(prest.shape[1]),
                  pl.BlockSpec((CONV_HALO, 2 * CONV_WIDTH), lambda i: (jnp.maximum(i * hb - 1, 0), 0)),
                  tok(D_MODEL), _full(modv), _full(cvv), _full(cw32), _full(w_a), _full(w_b), _full(w_pw), _full(w_out)]
        + [ANY] * (2 * nl),
        out_specs=[ANY] * nl + [tok(D_MODEL), tok(D_MODEL), tok(D_MODEL), tok(D_MODEL), tok(CONV_WIDTH), tok(SSM_WIDTH),
                                tok(CONV_WIDTH), tok(D_MODEL)],
        input_output_aliases={n_in + nl + k: k for k in range(nl)},
        out_shape=[jax.ShapeDtypeStruct(f.shape, f.dtype) for f in late_full]
        + [jax.ShapeDtypeStruct((t, D_MODEL), F32), jax.ShapeDtypeStruct((t, D_MODEL), BF16),
                   jax.ShapeDtypeStruct((t, D_MODEL), BF16), jax.ShapeDtypeStruct((t, D_MODEL), BF16),
                   jax.ShapeDtypeStruct((t, CONV_WIDTH), F32), jax.ShapeDtypeStruct((t, SSM_WIDTH), BF16),
                   jax.ShapeDtypeStruct((t, CONV_WIDTH), BF16), jax.ShapeDtypeStruct((t, D_MODEL), BF16)],
        scratch_shapes=[pltpu.VMEM((tb + CONV_HALO, CONV_WIDTH), F32),
                        pltpu.VMEM((7, tb + CONV_HALO - 8, CONV_WIDTH), F32), sem, sem, sem, sem],
        compiler_params=_params(("arbitrary",)),
    )(ys4, prest, prest, x, modv, cvv, cw32, w_a, w_b, w_pw, w_out, *late_sh, *late_full)


FFN_COLS = 1408


def _f5_ffn(r1, tgt, modv, lnv, fdw, w_up, w_down, tb):
    t = r1.shape[0]
    fw = 2 * FFN_HIDDEN

    def body(r1_ref, tgt_ref, modv_ref, lnv_ref, fdw_ref, wup_hbm, wdn_hbm,
             dr2_ref, d_ref, up_ref, z_ref, acc_ref, wup_v, wdn_v, upbuf, gbuf, hbuf, sems):
        i = pl.program_id(0)

        @pl.when(i == 0)
        def _():
            _load_once(wup_hbm, wup_v, sems.at[0])
            _load_once(wdn_hbm, wdn_v, sems.at[1])
            acc_ref[...] = jnp.zeros_like(acc_ref)
            upbuf[0:8, :] = jnp.zeros((8, fw), F32)

        xh1, _ = _ln_stats(r1_ref[...])
        x1 = xh1 * lnv_ref[0:1, :] + lnv_ref[1:2, :]
        xn2, _ = _ln_stats(x1)
        h2 = (xn2 * (1.0 + modv_ref[4:5, :]) + modv_ref[3:4, :]).astype(BF16)
        for j in range(fw // FFN_COLS):
            cs = slice(j * FFN_COLS, (j + 1) * FFN_COLS)
            up = _dot(h2, wup_v[:, cs])
            upbuf[8:, cs] = up
            up_ref[:, cs] = up.astype(BF16)

        def conv(cs):
            return (fdw_ref[0:1, cs] * upbuf[pl.ds(6, tb), cs] + fdw_ref[1:2, cs] * upbuf[pl.ds(7, tb), cs]
                    + fdw_ref[2:3, cs] * upbuf[pl.ds(8, tb), cs] + fdw_ref[3:4, cs])

        halves = [(slice(j * FFN_COLS, (j + 1) * FFN_COLS),
                   slice(FFN_HIDDEN + j * FFN_COLS, FFN_HIDDEN + (j + 1) * FFN_COLS)) for j in range(FFN_HIDDEN // FFN_COLS)]
        yf = jnp.zeros((tb, D_MODEL), F32)
        for ca, cv in halves:
            v = conv(cv)
            g, dg = _gelu_and_grad(conv(ca))
            gbuf[:, ca] = g.astype(BF16)
            hbuf[:, ca] = (v * dg).astype(BF16)
            z = (g * v).astype(BF16)
            z_ref[:, ca] = z
            yf += _dot(z, wdn_v[ca, :])
        r2 = ALPHA * x1 + modv_ref[5:6, :] * yf
        xh2, rstd2 = _ln_stats(r2)
        e = xh2 * lnv_ref[2:3, :] + lnv_ref[3:4, :] - tgt_ref[...]
        dx2 = e * (1.0 / D_MODEL)
        acc_ref[3:4, :] += _colsum(e * e) * (0.5 / D_MODEL)
        acc_ref[0:1, :] += _colsum(dx2 * xh2)
        acc_ref[1:2, :] += _colsum(dx2)
        dr2 = _ln_bwd(dx2 * lnv_ref[2:3, :], xh2, rstd2)
        dr2_ref[...] = dr2
        acc_ref[2:3, :] += _colsum(dr2 * yf)
        dyf = (modv_ref[5:6, :] * dr2).astype(BF16)
        for ca, cv in halves:
            dz = _dot_nt(dyf, wdn_v[ca, :])
            d_ref[:, ca] = (dz * hbuf[:, ca].astype(F32)).astype(BF16)
            d_ref[:, cv] = (dz * gbuf[:, ca].astype(F32)).astype(BF16)
        upbuf[0:8, :] = upbuf[pl.ds(tb, 8), :]

    tok = lambda w: pl.BlockSpec((tb, w), lambda i: (i, 0))
    return pl.pallas_call(
        body, name="f5_ffn", grid=(t // tb,),
        in_specs=[tok(D_MODEL), tok(D_MODEL), _full(modv), _full(lnv), _full(fdw), ANY, ANY],
        out_specs=[tok(D_MODEL), tok(fw), tok(fw), tok(FFN_HIDDEN), pl.BlockSpec((8, D_MODEL), lambda i: (0, 0))],
        out_shape=[jax.ShapeDtypeStruct((t, D_MODEL), F32), jax.ShapeDtypeStruct((t, fw), BF16),
                   jax.ShapeDtypeStruct((t, fw), BF16), jax.ShapeDtypeStruct((t, FFN_HIDDEN), BF16),
                   jax.ShapeDtypeStruct((8, D_MODEL), F32)],
        scratch_shapes=[pltpu.VMEM(w_up.shape, BF16), pltpu.VMEM(w_down.shape, BF16),
                        pltpu.VMEM((tb + 8, fw), F32), pltpu.VMEM((tb, FFN_HIDDEN), BF16),
                        pltpu.VMEM((tb, FFN_HIDDEN), BF16), pltpu.SemaphoreType.DMA((2,))],
        compiler_params=_params(("arbitrary",)),
    )(r1, tgt, modv, lnv, fdw, w_up, w_down)


def _b1b_ffn_up(d, up, dr2, r1, modv, lnv, fdw, w_up, tb):
    t = dr2.shape[0]
    fw = 2 * FFN_HIDDEN
    nt = t // tb
    hb = tb // 16

    def body(d_ref, nxt_ref, up_ref, dr2_ref, r1_ref, modv_ref, lnv_ref, fdw_ref, wup_hbm, dup_ref, dr1_ref, h2_ref,
             dyf_ref, acc_ref, accw_ref, wup_v, dbuf, shifted, sem):
        i = pl.program_id(0)

        @pl.when(i == 0)
        def _():
            _load_once(wup_hbm, wup_v, sem)
            acc_ref[...] = jnp.zeros_like(acc_ref)
            accw_ref[...] = jnp.zeros_like(accw_ref)

        dbuf[0:tb, :] = d_ref[...].astype(F32)
        dbuf[tb:, :] = jnp.where(i == nt - 1, 0.0, nxt_ref[...].astype(F32))
        dh2 = jnp.zeros((tb, D_MODEL), F32)
        for j in range(fw // FFN_COLS):
            cs = slice(j * FFN_COLS, (j + 1) * FFN_COLS)
            for k in range(1, FFN_KERNEL):
                shifted[k - 1] = dbuf[pl.ds(k, tb), cs]
            ds = [dbuf[pl.ds(0, tb), cs], shifted[0], shifted[1]]
            dup = (fdw_ref[2:3, cs] * ds[0] + fdw_ref[1:2, cs] * ds[1] + fdw_ref[0:1, cs] * ds[2]).astype(BF16)
            dup_ref[:, cs] = dup
            dh2 += _dot_nt(dup, wup_v[:, cs])
            upf = up_ref[:, cs].astype(F32)
            for k in range(FFN_KERNEL):
                accw_ref[k:k + 1, cs] += _colsum(ds[FFN_KERNEL - 1 - k] * upf)
            accw_ref[3:4, cs] += _colsum(ds[0])
        xh1, rstd1 = _ln_stats(r1_ref[...])
        x1 = xh1 * lnv_ref[0:1, :] + lnv_ref[1:2, :]
        xn2, rstd2 = _ln_stats(x1)
        h2_ref[...] = (xn2 * (1.0 + modv_ref[4:5, :]) + modv_ref[3:4, :]).astype(BF16)
        dr2 = dr2_ref[...]
        dyf_ref[...] = (modv_ref[5:6, :] * dr2).astype(BF16)
        acc_ref[0:1, :] += _colsum(dh2 * xn2)
        acc_ref[1:2, :] += _colsum(dh2)
        dx1 = _ln_bwd(dh2 * (1.0 + modv_ref[4:5, :]), xn2, rstd2) + ALPHA * dr2
        acc_ref[2:3, :] += _colsum(dx1 * xh1)
        acc_ref[3:4, :] += _colsum(dx1)
        dr1_ref[...] = _ln_bwd(dx1 * lnv_ref[0:1, :], xh1, rstd1)

    tok = lambda w: pl.BlockSpec((tb, w), lambda i: (i, 0))
    return pl.pallas_call(
        body, name="b1b_ffn_up", grid=(nt,),
        in_specs=[tok(fw), pl.BlockSpec((16, fw), lambda i: (jnp.minimum((i + 1) * hb, t // 16 - 1), 0)), tok(fw),
                  tok(D_MODEL), tok(D_MODEL), _full(modv), _full(lnv), _full(fdw), ANY],
        out_specs=[tok(fw), tok(D_MODEL), tok(D_MODEL), tok(D_MODEL), pl.BlockSpec((8, D_MODEL), lambda i: (0, 0)),
                   pl.BlockSpec((8, fw), lambda i: (0, 0))],
        out_shape=[jax.ShapeDtypeStruct((t, fw), BF16), jax.ShapeDtypeStruct((t, D_MODEL), F32),
                   jax.ShapeDtypeStruct((t, D_MODEL), BF16), jax.ShapeDtypeStruct((t, D_MODEL), BF16),
                   jax.ShapeDtypeStruct((8, D_MODEL), F32), jax.ShapeDtypeStruct((8, fw), F32)],
        scratch_shapes=[pltpu.VMEM(w_up.shape, BF16), pltpu.VMEM((tb + 16, fw), F32),
                        pltpu.VMEM((FFN_KERNEL - 1, tb, FFN_COLS), F32), pltpu.SemaphoreType.DMA],
        compiler_params=_params(("arbitrary",)),
    )(d, d, up, dr2, r1, modv, lnv, fdw, w_up)


def _b2_mixer(dr1, ys4, prest, ya, yb, ycv, vc, merged, modv, cvv, cw32, w_a, w_b, w_pw, w_out, late_dw, tb):
    t = dr1.shape[0]
    nt = t // tb
    nl = len(LATE)
    hb = tb // CONV_HALO
    cwd = CONV_WIDTH

    def body(dr1_ref, ys_ref, pr_ref, halo_ref, ya_ref, yb_ref, ycv_ref, vc_ref, mg_ref, modv_ref, cvv_ref, cw_ref,
             wa_ref, wb_ref, wpw_ref, wout_ref, *rest):
        dw, got = rest[:nl], rest[nl:2 * nl]
        (dys_ref, dpr_ref, dya_ref, dyb_ref, dycv_ref, dym_ref, acc_a, acc_b, acc_w, vbuf, dvbuf, vrot, dvrot,
         send, recv) = rest[2 * nl:]
        i = pl.program_id(0)
        ti = nt - 1 - i

        @pl.when(i == 0)
        def _():
            for cp in _late_scatter(dw, got, send, recv):
                cp.start()
            acc_a[...] = jnp.zeros_like(acc_a)
            acc_b[...] = jnp.zeros_like(acc_b)
            acc_w[...] = jnp.zeros_like(acc_w)
            dvbuf[pl.ds(tb, CONV_HALO), :] = jnp.zeros((CONV_HALO, cwd), F32)

        dr1 = dr1_ref[...]
        dym = (modv_ref[2:3, :] * dr1).astype(BF16)
        dym_ref[...] = dym
        ym = _dot(mg_ref[...], wout_ref[...])
        acc_a[0:1, :] += _colsum(dr1 * ym)
        dmg = _dot_nt(dym, wout_ref[...])
        sgs = _sig(pr_ref[:, 2 * cwd:2 * cwd + D_MODEL])
        sgc = _sig(pr_ref[:, 2 * cwd + D_MODEL:])
        ya_v = ya_ref[...].astype(F32)
        syb = _sig(yb_ref[...].astype(F32))
        ycv_v = ycv_ref[...].astype(F32)
        dpr_ref[:, 2 * cwd:2 * cwd + D_MODEL] = (dmg * (ya_v * syb) * sgs * (1.0 - sgs)).astype(BF16)
        dpr_ref[:, 2 * cwd + D_MODEL:] = (dmg * ycv_v * sgc * (1.0 - sgc)).astype(BF16)
        dyssm = dmg * sgs
        dya = (dyssm * syb).astype(BF16)
        dyb = (dyssm * ya_v * syb * (1.0 - syb)).astype(BF16)
        dya_ref[...] = dya
        dyb_ref[...] = dyb
        dyg = _dot_nt(dya, wa_ref[...]) + _dot_nt(dyb, wb_ref[...])
        ys = jnp.concatenate([ys_ref[b] for b in range(N_LANE_BLOCKS)], axis=-1)
        dys = dyg * _dgelu(ys)
        for b in range(N_LANE_BLOCKS):
            dys_ref[b] = dys[:, b * LANES:(b + 1) * LANES].astype(BF16)
        dycv = (dmg * sgc).astype(BF16)
        dycv_ref[...] = dycv
        dvs = _dot_nt(dycv, wpw_ref[...])
        xh, rstd = _ln_stats(vc_ref[...])
        vl = xh * cvv_ref[1:2, :] + cvv_ref[2:3, :]
        s = _sig(vl)
        dvl = dvs * s * (1.0 + vl * (1.0 - s))
        acc_b[1:2, :] += _colsum(dvl * xh)
        acc_b[2:3, :] += _colsum(dvl)
        dvc = _ln_bwd(dvl * cvv_ref[1:2, :], xh, rstd)
        acc_b[0:1, :] += _colsum(dvc)
        hv = halo_ref[:, 0:cwd] * _sig(halo_ref[:, cwd:2 * cwd])
        vbuf[0:CONV_HALO, :] = jnp.where(ti == 0, 0.0, hv)
        cva = pr_ref[:, 0:cwd]
        scg = _sig(pr_ref[:, cwd:2 * cwd])
        vbuf[CONV_HALO:, :] = cva * scg
        dvbuf[0:tb, :] = dvc
        _fill_rotations(vbuf, vrot, tb + CONV_HALO - 8)
        _fill_rotations(dvbuf, dvrot, tb + CONV_HALO - 8)
        dv = jnp.zeros((tb, cwd), F32)
        for k in range(CONV_KERNEL):
            dv += _rows_at(dvbuf, dvrot, CONV_KERNEL - 1 - k, tb) * cw_ref[k:k + 1, :]
            acc_w[k:k + 1, :] += _colsum(dvc * _rows_at(vbuf, vrot, CONV_HALO - CONV_KERNEL + 1 + k, tb))
        dvbuf[pl.ds(tb, CONV_HALO), :] = dvbuf[0:CONV_HALO, :]
        dpr_ref[:, 0:cwd] = (dv * scg).astype(BF16)
        dpr_ref[:, cwd:2 * cwd] = (dv * cva * scg * (1.0 - scg)).astype(BF16)

        @pl.when(i == nt - 1)
        def _():
            for cp in _late_scatter(dw, got, send, recv):
                cp.wait()

    rtok = lambda w: pl.BlockSpec((tb, w), lambda i: (nt - 1 - i, 0))
    r4 = pl.BlockSpec((N_LANE_BLOCKS, tb, LANES), lambda i: (0, nt - 1 - i, 0))
    pw = prest.shape[1]
    return pl.pallas_call(
        body, name="b2_mixer", grid=(nt,),
        in_specs=[rtok(D_MODEL), r4, rtok(pw),
                  pl.BlockSpec((CONV_HALO, 2 * cwd), lambda i: (jnp.maximum((nt - 1 - i) * hb - 1, 0), 0)),
                  rtok(D_MODEL), rtok(D_MODEL), rtok(D_MODEL), rtok(cwd), rtok(D_MODEL),
                  _full(modv), _full(cvv), _full(cw32), _full(w_a), _full(w_b), _full(w_pw), _full(w_out)] + [ANY] * nl,
        out_specs=[ANY] * nl + [r4, rtok(pw), rtok(D_MODEL), rtok(D_MODEL), rtok(D_MODEL), rtok(D_MODEL),
                   pl.BlockSpec((8, D_MODEL), lambda i: (0, 0)), pl.BlockSpec((8, cwd), lambda i: (0, 0)),
                   pl.BlockSpec((CONV_HALO, cwd), lambda i: (0, 0))],
        out_shape=[jax.ShapeDtypeStruct((N_DEV - 1,) + _piece(*BIG[m][1:]), BF16) for m in LATE]
        + [jax.ShapeDtypeStruct((N_LANE_BLOCKS, t, LANES), BF16), jax.ShapeDtypeStruct((t, pw), BF16),
                   jax.ShapeDtypeStruct((t, D_MODEL), BF16), jax.ShapeDtypeStruct((t, D_MODEL), BF16),
                   jax.ShapeDtypeStruct((t, D_MODEL), BF16), jax.ShapeDtypeStruct((t, D_MODEL), BF16),
                   jax.ShapeDtypeStruct((8, D_MODEL), F32), jax.ShapeDtypeStruct((8, cwd), F32),
                   jax.ShapeDtypeStruct((CONV_HALO, cwd), F32)],
        scratch_shapes=[pltpu.VMEM((tb + CONV_HALO, cwd), F32), pltpu.VMEM((tb + CONV_HALO, cwd), F32),
                        pltpu.VMEM((7, tb + CONV_HALO - 8, cwd), F32), pltpu.VMEM((7, tb + CONV_HALO - 8, cwd), F32),
                        pltpu.SemaphoreType.DMA((nl, N_DEV - 1)), pltpu.SemaphoreType.DMA((nl, N_DEV - 1))],
        compiler_params=_params(("arbitrary",)),
    )(dr1, ys4, prest, prest, ya, yb, ycv, vc, merged, modv, cvv, cw32, w_a, w_b, w_pw, w_out, *late_dw)


def _s5c_state_bwd(dy2, sg_r, sg_i, a8, hr, hi):
    nb, nc, _ = dy2.shape
    sc = STATE_COLS

    def body(dy_ref, sgr_ref, sgi_ref, a_ref, hr_ref, hi_ref, dxr_ref, dxi_ref, da_ref, dsgr_ref, dsgi_ref,
             g_s, lr_s, li_s, xr_s, xi_s):
        dy = dy_ref[0]
        _build_maps(sgr_ref, g_s)
        lr_s[...] = _dot(dy, g_s[...])
        _build_maps(sgi_ref, g_s)
        li_s[...] = _dot(dy, g_s[...])
        ar = a_ref[0, 0:1, :]
        ai = a_ref[0, 1:2, :]

        def step(k, carry):
            pr, pi, dar, dai = carry
            c = nc - 1 - k
            xr_s[pl.ds(c, 1), :] = pr
            xi_s[pl.ds(c, 1), :] = pi
            h_r = hr_ref[0, pl.ds(c, 1), :]
            h_i = hi_ref[0, pl.ds(c, 1), :]
            dar = dar + pr * h_r + pi * h_i
            dai = dai - pr * h_i + pi * h_r
            nr = lr_s[pl.ds(c, 1), :] + ar * pr + ai * pi
            ni = li_s[pl.ds(c, 1), :] - ai * pr + ar * pi
            return nr, ni, dar, dai

        z = jnp.zeros((1, sc), F32)
        _, _, dar, dai = lax.fori_loop(0, nc, step, (z, z, z, z))
        da_ref[0] = jnp.concatenate([dar, dai, jnp.zeros((6, sc), F32)], axis=0)
        dxr_ref[0] = xr_s[...].astype(BF16)
        dxi_ref[0] = xi_s[...].astype(BF16)
        for h_ref, o_ref in ((hr_ref, dsgr_ref), (hi_ref, dsgi_ref)):
            hb = h_ref[0].astype(BF16)
            for j in range(SSM_CHUNK):
                o_ref[j] = _fold(_dot_tn(dy[:, j * LANES:(j + 1) * LANES], hb), SSM_STATE)

    blk = lambda r, c: pl.BlockSpec((1, r, c), lambda b: (b, 0, 0))
    return pl.pallas_call(
        body, name="s5c_state_bwd", grid=(nb,),
        in_specs=[blk(nc, CHUNK_COLS), _cblk(SSM_STATE), _cblk(SSM_STATE), blk(8, sc), blk(nc, sc), blk(nc, sc)],
        out_specs=[blk(nc, sc), blk(nc, sc), blk(8, sc), _cblk(SSM_STATE), _cblk(SSM_STATE)],
        out_shape=[jax.ShapeDtypeStruct((nb, nc, sc), BF16), jax.ShapeDtypeStruct((nb, nc, sc), BF16),
                   jax.ShapeDtypeStruct((nb, 8, sc), F32),
                   jax.ShapeDtypeStruct((SSM_CHUNK, SSM_WIDTH, SSM_STATE), F32),
                   jax.ShapeDtypeStruct((SSM_CHUNK, SSM_WIDTH, SSM_STATE), F32)],
        scratch_shapes=[pltpu.VMEM((CHUNK_COLS, sc), BF16)] + [pltpu.VMEM((nc, sc), F32)] * 4,
        compiler_params=_params(("arbitrary",)),
    )(dy2, sg_r, sg_i, a8, hr, hi)


def _s5d_input_bwd(dy2, u2, kc, sw_r, sw_i, dxr, dxi):
    nb, nc, _ = dy2.shape
    sc = STATE_COLS

    def body(dy_ref, u_ref, kc_ref, swr_ref, swi_ref, dxr_ref, dxi_ref, du_ref, dkc_ref, dswr_ref, dswi_ref,
             tm_s, w_s, dk_s):
        dy = dy_ref[0]
        u = u_ref[0]
        _build_toeplitz(kc_ref, tm_s)
        du = _dot_nt(dy, tm_s[...])
        _build_maps(swr_ref, w_s)
        du += _dot_nt(dxr_ref[0], w_s[...])
        _build_maps(swi_ref, w_s)
        du += _dot_nt(dxi_ref[0], w_s[...])
        du_ref[0] = du.astype(BF16)
        dk_s[...] = jnp.zeros_like(dk_s)
        for ji in range(SSM_CHUNK):
            uj = u[:, ji * LANES:(ji + 1) * LANES]
            rows = _dot_tn(uj, dy)
            for jo in range(ji, SSM_CHUNK):
                dk_s[jo - ji] += rows[:, jo * LANES:(jo + 1) * LANES]
            dswr_ref[ji] = _fold(_dot_tn(uj, dxr_ref[0]), SSM_STATE)
            dswi_ref[ji] = _fold(_dot_tn(uj, dxi_ref[0]), SSM_STATE)
        for d in range(SSM_CHUNK):
            dkc_ref[d] = _fold(dk_s[d], SSM_GROUP)

    blk = lambda r, c: pl.BlockSpec((1, r, c), lambda b: (b, 0, 0))
    return pl.pallas_call(
        body, name="s5d_input_bwd", grid=(nb,),
        in_specs=[blk(nc, CHUNK_COLS), blk(nc, CHUNK_COLS), _cblk(SSM_GROUP), _cblk(SSM_STATE), _cblk(SSM_STATE),
                  blk(nc, sc), blk(nc, sc)],
        out_specs=[blk(nc, CHUNK_COLS), _cblk(SSM_GROUP), _cblk(SSM_STATE), _cblk(SSM_STATE)],
        out_shape=[jax.ShapeDtypeStruct((nb, nc, CHUNK_COLS), BF16),
                   jax.ShapeDtypeStruct((SSM_CHUNK, SSM_WIDTH, SSM_GROUP), F32),
                   jax.ShapeDtypeStruct((SSM_CHUNK, SSM_WIDTH, SSM_STATE), F32),
                   jax.ShapeDtypeStruct((SSM_CHUNK, SSM_WIDTH, SSM_STATE), F32)],
        scratch_shapes=[pltpu.VMEM((CHUNK_COLS, CHUNK_COLS), BF16), pltpu.VMEM((CHUNK_COLS, sc), BF16),
                        pltpu.VMEM((SSM_CHUNK, LANES, LANES), F32)],
        compiler_params=_params(("arbitrary",)),
    )(dy2, u2, kc, sw_r, sw_i, dxr, dxi)


def _b3_inproj(x, dr1, du4, dprest, modv, w_in, tb):
    t = x.shape[0]
    pw = IN_PROJ_WIDTH - SSM_WIDTH

    def body(x_ref, dr1_ref, du_ref, dpr_ref, modv_ref, w_hbm, gx_ref, dp_ref, acc_ref, accb_ref, w_v, sem):
        @pl.when(pl.program_id(0) == 0)
        def _():
            _load_once(w_hbm, w_v, sem)
            acc_ref[...] = jnp.zeros_like(acc_ref)
            accb_ref[...] = jnp.zeros_like(accb_ref)

        du = jnp.concatenate([du_ref[b] for b in range(N_LANE_BLOCKS)], axis=-1)
        dpr = dpr_ref[...]
        dp_ref[:, 0:SSM_WIDTH] = du
        dp_ref[:, SSM_WIDTH:] = dpr
        accb_ref[0:1, 0:SSM_WIDTH] += _colsum(du.astype(F32))
        accb_ref[0:1, SSM_WIDTH:] += _colsum(dpr.astype(F32))
        dh = _dot_nt(du, w_v[:, 0:SSM_WIDTH]) + _dot_nt(dpr, w_v[:, SSM_WIDTH:])
        xn, rstd = _ln_stats(x_ref[...])
        acc_ref[0:1, :] += _colsum(dh * xn)
        acc_ref[1:2, :] += _colsum(dh)
        gx_ref[...] = _ln_bwd(dh * (1.0 + modv_ref[1:2, :]), xn, rstd) + ALPHA * dr1_ref[...]

    tok = lambda w: pl.BlockSpec((tb, w), lambda i: (i, 0))
    return pl.pallas_call(
        body, name="b3_inproj", grid=(t // tb,),
        in_specs=[tok(D_MODEL), tok(D_MODEL), pl.BlockSpec((N_LANE_BLOCKS, tb, LANES), lambda i: (0, i, 0)), tok(pw),
                  _full(modv), ANY],
        out_specs=[tok(D_MODEL), tok(IN_PROJ_WIDTH), pl.BlockSpec((8, D_MODEL), lambda i: (0, 0)),
                   pl.BlockSpec((8, IN_PROJ_WIDTH), lambda i: (0, 0))],
        out_shape=[jax.ShapeDtypeStruct((t, D_MODEL), F32), jax.ShapeDtypeStruct((t, IN_PROJ_WIDTH), BF16),
                   jax.ShapeDtypeStruct((8, D_MODEL), F32), jax.ShapeDtypeStruct((8, IN_PROJ_WIDTH), F32)],
        scratch_shapes=[pltpu.VMEM(w_in.shape, BF16), pltpu.SemaphoreType.DMA],
        compiler_params=_params(("arbitrary",)),
    )(x, dr1, du4, dprest, modv, w_in)


TN_ROWS = 2048


def _tn_matmul(name, a, b, tm, tn):
    t, m = a.shape
    n = b.shape[1]
    tt = min(TN_ROWS, t)
    nk = t // tt

    def body(a_ref, b_ref, o_ref, acc):
        k = pl.program_id(2)

        @pl.when(k == 0)
        def _():
            acc[...] = jnp.zeros_like(acc)

        acc[...] += _dot_tn(a_ref[...], b_ref[...])

        @pl.when(k == nk - 1)
        def _():
            o_ref[...] = acc[...].astype(BF16)

    return pl.pallas_call(
        body, name=name, grid=(m // tm, n // tn, nk),
        in_specs=[pl.BlockSpec((tt, tm), lambda i, j, k: (k, i)), pl.BlockSpec((tt, tn), lambda i, j, k: (k, j))],
        out_specs=pl.BlockSpec((tm, tn), lambda i, j, k: (i, j)),
        out_shape=jax.ShapeDtypeStruct((m, n), BF16),
        scratch_shapes=[pltpu.VMEM((tm, tn), F32)],
        compiler_params=_params(("arbitrary", "arbitrary", "arbitrary")),
    )(a, b)


def _local_step(x, tgt, modv, wb, late_sh, sp, tb=256):
    t = x.shape[0]
    nc = t // SSM_CHUNK
    row8 = lambda rows, w: jnp.concatenate([r.reshape(1, w) for r in rows] + [jnp.zeros((8 - len(rows), w), F32)], axis=0)
    lnv = row8([sp["ln1_g"], sp["ln1_b"], sp["ln2_g"], sp["ln2_b"]], D_MODEL)
    cvv = row8([sp["cv_dw_b"], sp["cv_ln_g"], sp["cv_ln_b"]], CONV_WIDTH)
    cw32 = jnp.concatenate([sp["cv_dw_w"].reshape(CONV_KERNEL, CONV_WIDTH), jnp.zeros((1, CONV_WIDTH), F32)], axis=0)
    fdw = row8(list(sp["ffn_dw_w"].reshape(FFN_KERNEL, 2 * FFN_HIDDEN)) + [sp["ffn_dw_b"]], 2 * FFN_HIDDEN)
    b_in = sp["b_in"].reshape(1, IN_PROJ_WIDTH)
    ssm = tuple(sp[k] for k in ("ssm_lambda_re", "ssm_lambda_im", "ssm_log_dt", "ssm_b_re", "ssm_b_im", "ssm_c_re",
                                "ssm_c_im", "ssm_d"))
    (kc, sw_r, sw_i, sg_r, sg_i, a), ssm_vjp = jax.vjp(_s5_build, *ssm)
    a8 = jnp.concatenate([a, jnp.zeros((N_LANE_BLOCKS, 6, STATE_COLS), F32)], axis=1)

    u4, prest, h1 = _f1_inproj(x, modv, b_in, wb["w_in"], tb)
    u2 = u4.reshape(N_LANE_BLOCKS, nc, CHUNK_COLS)
    hr, hi = _s5a_state(u2, sw_r, sw_i, a8)
    ys4 = _s5b_out(u2, kc, sg_r, sg_i, hr, hi).reshape(N_LANE_BLOCKS, t, LANES)
    w_up, w_down, r1, ya, yb, ycv, vc, yg, vs, merged = _f4_mixer(
        ys4, prest, x, modv, cvv, cw32, wb["ssm_glu_w_a"], wb["ssm_glu_w_b"], wb["cv_w_pw"], wb["w_out"], late_sh,
        [wb[BIG[m][0]] for m in LATE], tb)
    dr2, dconv, up, z, acc5 = _f5_ffn(r1, tgt, modv, lnv, fdw, w_up, w_down, tb)
    dup, dr1, h2, dyf, acc1b, acc1a = _b1b_ffn_up(dconv, up, dr2, r1, modv, lnv, fdw, w_up, tb)
    late_dw = [_tn_matmul("dw_up", h2, dup, 1024, FFN_COLS), _tn_matmul("dw_down", z, dyf, FFN_COLS, 1024)]
    got_up, got_down, dys4, dprest, dya, dyb, dycv, dym, acc2a, acc2b, acc2w = _b2_mixer(
        dr1, ys4, prest, ya, yb, ycv, vc, merged, modv, cvv, cw32, wb["ssm_glu_w_a"], wb["ssm_glu_w_b"],
        wb["cv_w_pw"], wb["w_out"], late_dw, tb)
    dy2 = dys4.reshape(N_LANE_BLOCKS, nc, CHUNK_COLS)
    dxr, dxi, da8, dsg_r, dsg_i = _s5c_state_bwd(dy2, sg_r, sg_i, a8, hr, hi)
    du2, dkc, dsw_r, dsw_i = _s5d_input_bwd(dy2, u2, kc, sw_r, sw_i, dxr, dxi)
    dssm = ssm_vjp((dkc, dsw_r, dsw_i, dsg_r, dsg_i, da8[:, 0:2, :]))
    gx, dp, acc3, acc3b = _b3_inproj(x, dr1, du2.reshape(N_LANE_BLOCKS, t, LANES), dprest, modv, wb["w_in"], tb)

    dbig = {
        "w_in": _tn_matmul("dw_in", h1, dp, 1024, 896),
        "ssm_glu_w_a": _tn_matmul("dw_glu_a", yg, dya, 512, 1024),
        "ssm_glu_w_b": _tn_matmul("dw_glu_b", yg, dyb, 512, 1024),
        "cv_w_pw": _tn_matmul("dw_pw", vs, dycv, 512, 1024),
        "w_out": _tn_matmul("dw_out", merged, dym, 1024, 1024),
    }
    dmod = jnp.concatenate([acc3[1], acc3[0], acc2a[0], acc1b[1], acc1b[0], acc5[2]])
    small = {
        "dmod": dmod, "b_in": acc3b[0],
        "ssm_lambda_re": dssm[0], "ssm_lambda_im": dssm[1], "ssm_log_dt": dssm[2], "ssm_b_re": dssm[3],
        "ssm_b_im": dssm[4], "ssm_c_re": dssm[5], "ssm_c_im": dssm[6], "ssm_d": dssm[7],
        "cv_dw_w": acc2w[0:CONV_KERNEL], "cv_dw_b": acc2b[0], "cv_ln_g": acc2b[1], "cv_ln_b": acc2b[2],
        "ln1_g": acc1b[2], "ln1_b": acc1b[3], "ffn_dw_w": acc1a[0:FFN_KERNEL], "ffn_dw_b": acc1a[3],
        "ln2_g": acc5[0], "ln2_b": acc5[1], "loss": jnp.sum(acc5[3]).reshape(1),
    }
    return gx, dbig, (late_dw, [got_up, got_down]), small


def _allgather(name, shard):
    m_per, n = shard.shape

    def body(x_ref, out_ref, send_sems, recv_sems, local_sem):
        x, y, c, chips = _place()
        me, sibling = (x, y, c), (x, y, 1 - c)

        def rows(px, py, pc):
            return out_ref.at[pl.ds((4 * px + 2 * py + pc) * m_per, m_per), :]

        def copy(k, block, to, src=None):
            return pltpu.make_async_remote_copy(
                src_ref=rows(*block) if src is None else src, dst_ref=rows(*block),
                send_sem=send_sems.at[k], recv_sem=recv_sems.at[k], device_id=to, device_id_type=MESH)

        mine = pltpu.make_async_copy(x_ref, rows(*me), local_sem)
        mine.start()
        first = [copy(0, me, sibling, src=x_ref)]
        first += [copy(1 + j, me, (*chip, c), src=x_ref) for j, chip in enumerate(chips)]
        for cp in first:
            cp.start()
        passed = [copy(4 + j, (*chip, c), sibling) for j, chip in enumerate(chips)]
        for j, chip in enumerate(chips):
            copy(1 + j, (*chip, c), me).wait_recv()
            passed[j].start()
        copy(0, sibling, me).wait_recv()
        for j, chip in enumerate(chips):
            copy(4 + j, (*chip, 1 - c), me).wait_recv()
        for cp in first + passed:
            cp.wait_send()
        mine.wait()

    return pl.pallas_call(
        body, name=name,
        out_shape=jax.ShapeDtypeStruct((N_DEV * m_per, n), shard.dtype),
        in_specs=[pl.BlockSpec(memory_space=pltpu.VMEM)],
        out_specs=pl.BlockSpec(memory_space=pltpu.VMEM),
        scratch_shapes=[pltpu.SemaphoreType.DMA((7,)), pltpu.SemaphoreType.DMA((7,)), pltpu.SemaphoreType.DMA],
        compiler_params=_params(),
    )(shard)


def _add_rows(pr):
    return 64 if pr % 64 == 0 else 16


def _gather_weights(shards):
    nm = len(BIG)
    nl = len(LATE)

    def body(*refs):
        ins, outs, lsh = refs[:nm], refs[nm:2 * nm], refs[2 * nm:2 * nm + nl]
        stage = refs[2 * nm + nl:3 * nm + nl]
        send, recv, fsend, frecv, lsem = refs[3 * nm + nl:]
        x, y, c, chips = _place()
        s_me = 2 * x + y
        sibling = (x, y, 1 - c)
        pend = []
        for m in range(nm):
            stage[m][...] = ins[m][...].astype(BF16)
        for m, (_, kind, shape) in enumerate(BIG):
            pr, pc = _piece(kind, shape)
            for k in range(2):
                cp = pltpu.make_async_copy(stage[m].at[pl.ds(k * pr, pr), :], _piece_at(outs[m], kind, shape, s_me, k),
                                           lsem.at[m, k])
                cp.start()
                pend.append(cp.wait)
            if m in LATE:
                cp = pltpu.make_async_copy(stage[m], lsh[LATE.index(m)], lsem.at[m, 2])
                cp.start()
                pend.append(cp.wait)
                continue
            for j, chip in enumerate(chips):
                cp = pltpu.make_async_remote_copy(
                    src_ref=stage[m].at[pl.ds(pl.multiple_of(c * pr, 16), pr), :],
                    dst_ref=_piece_at(outs[m], kind, shape, s_me, c),
                    send_sem=send.at[m, j], recv_sem=recv.at[m, j], device_id=(*chip, c), device_id_type=MESH)
                cp.start()
                pend.append(cp.wait_send)
        for m in EARLY:
            _, kind, shape = BIG[m]
            for j, (cx, cy) in enumerate(chips):
                got = _piece_at(outs[m], kind, shape, 2 * cx + cy, c)
                pltpu.make_async_remote_copy(src_ref=got, dst_ref=got, send_sem=send.at[m, j], recv_sem=recv.at[m, j],
                                             device_id=(cx, cy, c), device_id_type=MESH).wait_recv()
                cp = pltpu.make_async_remote_copy(src_ref=got, dst_ref=got, send_sem=fsend.at[m, j],
                                                  recv_sem=frecv.at[m, j], device_id=sibling, device_id_type=MESH)
                cp.start()
                pend.append(cp.wait_send)
        for m in EARLY:
            _, kind, shape = BIG[m]
            for j, (cx, cy) in enumerate(chips):
                got = _piece_at(outs[m], kind, shape, 2 * cx + cy, 1 - c)
                pltpu.make_async_remote_copy(src_ref=got, dst_ref=got, send_sem=fsend.at[m, j], recv_sem=frecv.at[m, j],
                                             device_id=sibling, device_id_type=MESH).wait_recv()
        for w in pend:
            w()

    sem = lambda *s: pltpu.SemaphoreType.DMA(s)
    res = pl.pallas_call(
        body, name="gather_weights",
        out_shape=[jax.ShapeDtypeStruct(shape, BF16) for _, _, shape in BIG]
        + [jax.ShapeDtypeStruct(shards[m].shape, BF16) for m in LATE],
        in_specs=[pl.BlockSpec(memory_space=pltpu.VMEM)] * nm,
        out_specs=[ANY] * (nm + nl),
        scratch_shapes=[pltpu.VMEM(s.shape, BF16) for s in shards] + [sem(nm, 3), sem(nm, 3), sem(nm, 3), sem(nm, 3),
                                                                         sem(nm, 3)],
        compiler_params=_params(),
    )(*shards)
    return res[:nm], res[nm:]


def _rs1_sibling(grads):
    mats = [BIG[m] for m in EARLY]
    nm = len(mats)

    def body(*refs):
        ins, outs = refs[:nm], refs[nm:2 * nm]
        send, recv = refs[2 * nm:]
        x, y, c, _ = _place()
        cps = []
        for m, (_, kind, shape) in enumerate(mats):
            for s in range(N_CHIP):
                cp = pltpu.make_async_remote_copy(
                    src_ref=_piece_at(ins[m], kind, shape, s, 1 - c), dst_ref=outs[m].at[s],
                    send_sem=send.at[m, s], recv_sem=recv.at[m, s], device_id=(x, y, 1 - c), device_id_type=MESH)
                cp.start()
                cps.append(cp)
        for cp in cps:
            cp.wait()

    sem = lambda *s: pltpu.SemaphoreType.DMA(s)
    return pl.pallas_call(
        body, name="rs1_sibling",
        out_shape=[jax.ShapeDtypeStruct((N_CHIP,) + _piece(kind, shape), BF16) for _, kind, shape in mats],
        in_specs=[ANY] * nm, out_specs=[ANY] * nm,
        scratch_shapes=[sem(nm, N_CHIP), sem(nm, N_CHIP)],
        compiler_params=_params(),
    )(*grads)


def _rs2_chips(grads, halves):
    mats = [BIG[m] for m in EARLY]
    nm = len(mats)

    def body(*refs):
        gin, hin = refs[:nm], refs[nm:2 * nm]
        own, got = refs[2 * nm:3 * nm], refs[3 * nm:4 * nm]
        send, recv, lsem = refs[4 * nm:]
        x, y, c, chips = _place()
        s_me = 2 * x + y
        for m, (_, kind, shape) in enumerate(mats):
            pr, pc = _piece(kind, shape)

            def scoped(a, b, m=m, kind=kind, shape=shape, pr=pr):
                loads = [pltpu.make_async_copy(_piece_at(gin[m], kind, shape, s, c), a.at[s], lsem.at[s])
                         for s in range(N_CHIP)]
                loads.append(pltpu.make_async_copy(hin[m], b, lsem.at[N_CHIP]))
                for cp in loads:
                    cp.start()
                for cp in loads:
                    cp.wait()
                step = _add_rows(pr)
                for s in range(N_CHIP):
                    def add(i, _, s=s):
                        r = pl.ds(pl.multiple_of(i * step, 16), step)
                        a[s, r, :] = (a[s, r, :].astype(F32) + b[s, r, :].astype(F32)).astype(BF16)
                        return 0

                    lax.fori_loop(0, pr // step, add, 0)
                waits = []
                for j, (cx, cy) in enumerate(chips):
                    cp = pltpu.make_async_remote_copy(src_ref=a.at[2 * cx + cy], dst_ref=got[m].at[j], send_sem=send.at[m, j],
                                                      recv_sem=recv.at[m, j], device_id=(cx, cy, c), device_id_type=MESH)
                    cp.start()
                    waits.append(cp.wait_send)
                cp = pltpu.make_async_copy(a.at[s_me], own[m], lsem.at[N_CHIP + 1])
                cp.start()
                waits.append(cp.wait)
                for w in waits:
                    w()

            pl.run_scoped(scoped, pltpu.VMEM((N_CHIP, pr, pc), BF16), pltpu.VMEM((N_CHIP, pr, pc), BF16))
        for m in range(nm):
            for j, (cx, cy) in enumerate(chips):
                pltpu.make_async_remote_copy(src_ref=got[m].at[j], dst_ref=got[m].at[j], send_sem=send.at[m, j],
                                             recv_sem=recv.at[m, j], device_id=(cx, cy, c), device_id_type=MESH).wait_recv()

    sem = lambda *s: pltpu.SemaphoreType.DMA(s)
    pieces = [_piece(kind, shape) for _, kind, shape in mats]
    return pl.pallas_call(
        body, name="rs2_chips",
        out_shape=[jax.ShapeDtypeStruct(p, BF16) for p in pieces] + [jax.ShapeDtypeStruct((3,) + p, BF16) for p in pieces],
        in_specs=[ANY] * (2 * nm), out_specs=[ANY] * (2 * nm),
        scratch_shapes=[sem(nm, 3), sem(nm, 3), sem(N_CHIP + 2)],
        compiler_params=_params(),
    )(*grads, *halves)


def _rs3_finish(own, got):
    nm = len(BIG)

    def body(*refs):
        oin, gin = refs[:nm], refs[nm:2 * nm]
        outs = refs[2 * nm:3 * nm]
        send, recv, lsem = refs[3 * nm:]
        x, y, c, _ = _place()
        for m, (_, kind, shape) in enumerate(BIG):
            pr, pc = _piece(kind, shape)
            ng = got[m].shape[0]

            def scoped(a, g, f, m=m, pr=pr, ng=ng, kind=kind, shape=shape):
                mine = _piece_at(oin[m], kind, shape, 2 * x + y, c) if m in LATE else oin[m]
                loads = [pltpu.make_async_copy(mine, a, lsem.at[0]), pltpu.make_async_copy(gin[m], g, lsem.at[1])]
                for cp in loads:
                    cp.start()
                for cp in loads:
                    cp.wait()
                step = _add_rows(pr)

                def add(i, _):
                    r = pl.ds(pl.multiple_of(i * step, 16), step)
                    acc = a[r, :].astype(F32)
                    for q in range(ng):
                        acc = acc + g[q, r, :].astype(F32)
                    f[r, :] = acc
                    return 0

                lax.fori_loop(0, pr // step, add, 0)
                dst = outs[m].at[pl.ds(pl.multiple_of(c * pr, 8), pr), :]
                local = pltpu.make_async_copy(f, dst, lsem.at[2])
                local.start()
                cp = pltpu.make_async_remote_copy(src_ref=f, dst_ref=dst, send_sem=send.at[m], recv_sem=recv.at[m],
                                                  device_id=(x, y, 1 - c), device_id_type=MESH)
                cp.start()
                cp.wait_send()
                local.wait()

            pl.run_scoped(scoped, pltpu.VMEM((pr, pc), BF16), pltpu.VMEM((ng, pr, pc), BF16), pltpu.VMEM((pr, pc), F32))
        for m, (_, kind, shape) in enumerate(BIG):
            pr, pc = _piece(kind, shape)
            dst = outs[m].at[pl.ds(pl.multiple_of((1 - c) * pr, 8), pr), :]
            pltpu.make_async_remote_copy(src_ref=dst, dst_ref=dst, send_sem=send.at[m], recv_sem=recv.at[m],
                                         device_id=(x, y, 1 - c), device_id_type=MESH).wait_recv()

    sem = lambda *s: pltpu.SemaphoreType.DMA(s)
    pieces = [_piece(kind, shape) for _, kind, shape in BIG]
    return pl.pallas_call(
        body, name="rs3_finish",
        out_shape=[jax.ShapeDtypeStruct((2 * pr, pc), F32) for pr, pc in pieces],
        in_specs=[ANY] * (2 * nm), out_specs=[ANY] * nm,
        scratch_shapes=[sem(nm), sem(nm), sem(3)],
        compiler_params=_params(),
    )(*own, *got)


def _cond_fwd(c_all, w_shard, b_shard):
    def body(c_ref, w_ref, b_ref, act_ref, mod_ref):
        cv = c_ref[...]
        act = cv * _sig(cv)
        act_ref[...] = act
        mod_ref[...] = _dot(act.astype(BF16), w_ref[...].astype(BF16)) + b_ref[...]

    return pl.pallas_call(
        body, name="cond_fwd",
        out_shape=[jax.ShapeDtypeStruct(c_all.shape, F32), jax.ShapeDtypeStruct((c_all.shape[0], w_shard.shape[1]), F32)],
        compiler_params=_params(),
    )(c_all, w_shard, b_shard)


def _cond_bwd(act_t, dmod_shard):
    k, n = act_t.shape[0], dmod_shard.shape[1]

    def body(a_ref, d_ref, o_ref):
        acc = a_ref[:, 0:1] * d_ref[0:1, :]
        for e in range(1, N_DEV):
            acc += a_ref[:, e:e + 1] * d_ref[e:e + 1, :]
        o_ref[...] = acc

    tr = 256
    return pl.pallas_call(
        body, name="cond_bwd", grid=(k // tr,),
        in_specs=[pl.BlockSpec((tr, N_DEV), lambda i: (i, 0)), _full(dmod_shard)],
        out_specs=pl.BlockSpec((tr, n), lambda i: (i, 0)),
        out_shape=jax.ShapeDtypeStruct((k, n), F32),
        compiler_params=_params(("arbitrary",)),
    )(act_t, dmod_shard)


def _sum_blocks(allp):
    def body(a_ref, o_ref):
        acc = a_ref[0:PACK_ROWS, :]
        for d in range(1, N_DEV):
            acc += a_ref[d * PACK_ROWS:(d + 1) * PACK_ROWS, :]
        o_ref[...] = acc

    return pl.pallas_call(
        body, name="sum_small", out_shape=jax.ShapeDtypeStruct((PACK_ROWS, PACK_COLS), F32), compiler_params=_params(),
    )(allp)


def _adamw(name, w, g, m, v):
    r, cc = w.shape
    tr = r
    for cand in (256, 128, 64, 32, 16, 8):
        if r % cand == 0:
            tr = cand
            break
    bc1 = 1.0 - ADAM_B1 ** ADAM_STEP
    bc2 = 1.0 - ADAM_B2 ** ADAM_STEP

    def body(w_ref, g_ref, m_ref, v_ref, d_ref, nm_ref, nv_ref):
        gv = g_ref[...]
        nm = ADAM_B1 * m_ref[...] + (1.0 - ADAM_B1) * gv
        nv = ADAM_B2 * v_ref[...] + (1.0 - ADAM_B2) * (gv * gv)
        nm_ref[...] = nm
        nv_ref[...] = nv
        d_ref[...] = -ADAM_LR * ((nm / bc1) / (jnp.sqrt(nv / bc2) + ADAM_EPS) + ADAM_WD * w_ref[...])

    spec = pl.BlockSpec((tr, cc), lambda i: (i, 0))
    return pl.pallas_call(
        body, name=name, grid=(r // tr,), in_specs=[spec] * 4, out_specs=[spec] * 3,
        out_shape=[jax.ShapeDtypeStruct((r, cc), F32)] * 3, compiler_params=_params(("arbitrary",)),
    )(w, g, m, v)


def _pack(fields, layout):
    parts = [fields[name].reshape(-1).astype(F32) if name in fields else jnp.zeros((n,), F32) for name, n in layout]
    used = sum(n for _, n in layout)
    parts.append(jnp.zeros((PACK_ROWS * PACK_COLS - used,), F32))
    return jnp.concatenate(parts).reshape(PACK_ROWS, PACK_COLS)


def _unpack(flat, layout):
    flat = flat.reshape(-1)
    out, o = {}, 0
    for name, n in layout:
        out[name] = flat[o:o + n]
        o += n
    return out


def kernel(x, c, w_cond, b_cond, w_in, b_in, ssm_lambda_re, ssm_lambda_im, ssm_log_dt, ssm_b_re, ssm_b_im, ssm_c_re, ssm_c_im, ssm_d, ssm_glu_w_a, ssm_glu_w_b, cv_dw_w, cv_dw_b, cv_ln_g, cv_ln_b, cv_w_pw, w_out, ln1_g, ln1_b, ffn_w_up, ffn_dw_w, ffn_dw_b, ffn_w_down, ln2_g, ln2_b, loss_target, m_w_cond, m_b_cond, m_w_in, m_b_in, m_ssm_lambda_re, m_ssm_lambda_im, m_ssm_log_dt, m_ssm_b_re, m_ssm_b_im, m_ssm_c_re, m_ssm_c_im, m_ssm_d, m_ssm_glu_w_a, m_ssm_glu_w_b, m_cv_dw_w, m_cv_dw_b, m_cv_ln_g, m_cv_ln_b, m_cv_w_pw, m_w_out, m_ln1_g, m_ln1_b, m_ffn_w_up, m_ffn_dw_w, m_ffn_dw_b, m_ffn_w_down, m_ln2_g, m_ln2_b, v_w_cond, v_b_cond, v_w_in, v_b_in, v_ssm_lambda_re, v_ssm_lambda_im, v_ssm_log_dt, v_ssm_b_re, v_ssm_b_im, v_ssm_c_re, v_ssm_c_im, v_ssm_d, v_ssm_glu_w_a, v_ssm_glu_w_b, v_cv_dw_w, v_cv_dw_b, v_cv_ln_g, v_cv_ln_b, v_cv_w_pw, v_w_out, v_ln1_g, v_ln1_b, v_ffn_w_up, v_ffn_dw_w, v_ffn_dw_b, v_ffn_w_down, v_ln2_g, v_ln2_b):
    given = locals()
    a = {n: given[n] for n in INPUTS}
    xi, yi, ci = lax.axis_index("x"), lax.axis_index("y"), lax.axis_index("c")
    s_me = 2 * xi + yi
    e_me = 4 * xi + 2 * yi + ci

    first = jnp.concatenate([
        jnp.concatenate([a["c"], jnp.zeros((7, D_MODEL), F32)], axis=0),
        jnp.concatenate([a["cv_dw_w"].reshape(-1), a["ffn_dw_w"].reshape(-1)]).reshape(8, D_MODEL)], axis=0)
    first_all = _allgather("gather_c", first).reshape(N_DEV, 16, D_MODEL)
    c_all = first_all[:, 0, :]
    dw_all = first_all[0::2, 8:, :].reshape(N_CHIP, 8 * D_MODEL)
    n_cv = CONV_KERNEL * CONV_WIDTH // N_CHIP
    cv_dw_full = dw_all[:, :n_cv].reshape(N_CHIP, CONV_KERNEL, CONV_WIDTH // N_CHIP).transpose(1, 0, 2) \
        .reshape(CONV_KERNEL, CONV_WIDTH)
    ffn_dw_full = dw_all[:, n_cv:].reshape(N_CHIP, FFN_KERNEL, 2 * FFN_HIDDEN // N_CHIP).transpose(1, 0, 2) \
        .reshape(FFN_KERNEL, 2 * FFN_HIDDEN)
    ncols = N_COND * D_MODEL // N_CHIP
    b_cond_shard = lax.dynamic_slice(a["b_cond"], (0, s_me * ncols), (1, ncols))
    c_act_all, modp = _cond_fwd(c_all, a["w_cond"][0], b_cond_shard)
    modp_all = _allgather("gather_mod", modp).reshape(N_DEV, N_DEV, ncols)[0::2]
    mod_e = lax.dynamic_index_in_dim(modp_all, e_me, axis=1, keepdims=False).reshape(N_COND, D_MODEL)
    modv = jnp.concatenate([mod_e, jnp.zeros((2, D_MODEL), F32)], axis=0)

    full, late_sh = _gather_weights([a[n][0] for n, _, _ in BIG])
    wb = dict(zip([n for n, _, _ in BIG], full))
    sp = {n: a[n][0] for n in ("b_in", "ssm_lambda_re", "ssm_lambda_im", "ssm_log_dt", "ssm_b_re", "ssm_b_im",
                               "ssm_c_re", "ssm_c_im", "ssm_d", "cv_dw_b", "cv_ln_g", "cv_ln_b", "ln1_g", "ln1_b",
                               "ffn_dw_b", "ln2_g", "ln2_b")}
    sp["cv_dw_w"] = cv_dw_full
    sp["ffn_dw_w"] = ffn_dw_full
    gx, dbig, (late_dw, late_got), small = _local_step(a["x"][0], a["loss_target"][0], modv, wb, late_sh, sp)

    small["c_act"] = lax.dynamic_index_in_dim(c_act_all, e_me, axis=0, keepdims=False)
    packed_all = _allgather("gather_small", _pack(small, PACK))
    tot = _unpack(_sum_blocks(packed_all), PACK)
    rows = packed_all.reshape(N_DEV, PACK_ROWS * PACK_COLS)
    dmod_all = rows[:, 0:N_COND * D_MODEL]
    act_all = rows[:, N_COND * D_MODEL:(N_COND + 1) * D_MODEL]
    g_w_cond = _cond_bwd(act_all.T, lax.dynamic_slice(dmod_all, (0, s_me * ncols), (N_DEV, ncols)))

    glist = [dbig[BIG[m][0]] for m in EARLY]
    halves = _rs1_sibling(glist)
    r2 = _rs2_chips(glist, halves)
    gsh = _rs3_finish(list(r2[:len(EARLY)]) + late_dw, list(r2[len(EARLY):]) + late_got)

    grads = {"w_cond": g_w_cond[None], "b_cond": tot["dmod"].reshape(1, -1)}
    for (n, kind, shape), g in zip(BIG, gsh):
        grads[n] = g.reshape(a[n].shape)
    for n in ("b_in", "ssm_lambda_re", "ssm_lambda_im", "ssm_log_dt", "ssm_b_re", "ssm_b_im", "ssm_c_re", "ssm_c_im",
              "ssm_d", "cv_dw_b", "cv_ln_g", "cv_ln_b", "ln1_g", "ln1_b", "ffn_dw_b", "ln2_g", "ln2_b"):
        grads[n] = tot[n].reshape(a[n].shape)
    wcv = CONV_WIDTH // N_CHIP
    grads["cv_dw_w"] = lax.dynamic_slice(tot["cv_dw_w"].reshape(CONV_KERNEL, CONV_WIDTH), (0, s_me * wcv),
                                         (CONV_KERNEL, wcv)).reshape(a["cv_dw_w"].shape)
    wff = 2 * FFN_HIDDEN // N_CHIP
    grads["ffn_dw_w"] = lax.dynamic_slice(tot["ffn_dw_w"].reshape(FFN_KERNEL, 2 * FFN_HIDDEN), (0, s_me * wff),
                                          (FFN_KERNEL, wff)).reshape(a["ffn_dw_w"].shape)

    delta, new_m, new_v = {}, {}, {}
    for n in ["w_cond"] + [n for n, _, _ in BIG]:
        d, nm_, nv_ = _adamw("adamw_" + n, a[n][0], grads[n][0], a["m_" + n][0], a["v_" + n][0])
        delta[n], new_m[n], new_v[n] = d[None], nm_[None], nv_[None]
    upd = [n for n, _ in SMALL_UPD]
    d, nm_, nv_ = _adamw("adamw_small", _pack({n: a[n] for n in upd}, SMALL_UPD), _pack({n: grads[n] for n in upd}, SMALL_UPD),
                         _pack({n: a["m_" + n] for n in upd}, SMALL_UPD), _pack({n: a["v_" + n] for n in upd}, SMALL_UPD))
    for dst, flat in ((delta, d), (new_m, nm_), (new_v, nv_)):
        for n, val in _unpack(flat, SMALL_UPD).items():
            dst[n] = val.reshape(a[n].shape)

    loss = tot["loss"].reshape(())
    return (loss, gx[None], *[grads[n] for n in WEIGHTS], *[delta[n] for n in WEIGHTS],
            *[new_m[n] for n in WEIGHTS], *[new_v[n] for n in WEIGHTS])
```

```python
import functools
import math

import jax
import jax.numpy as jnp
from jax import lax
from jax.experimental import pallas as pl
from jax.experimental.pallas import tpu as pltpu

F32 = jnp.float32
BF16 = jnp.bfloat16

D_MODEL = 1024
SSM_WIDTH = 512
SSM_GROUP = 16
SSM_GROUPS = 32
SSM_STATE = 64
CONV_WIDTH = 512
CONV_KERNEL = 31
FFN_HIDDEN = 2816
FFN_KERNEL = 3
IN_PROJ_WIDTH = 3584
N_COND = 6
ALPHA = 2.0 ** 0.25
LN_EPS = 1e-5
ADAM_LR, ADAM_B1, ADAM_B2, ADAM_EPS, ADAM_WD, ADAM_STEP = 0.001, 0.9, 0.999, 1e-08, 0.01, 10

N_DEV = 8
N_CHIP = 4
LANES = 128
SSM_CHUNK = 16
LANE_GROUPS = LANES // SSM_GROUP
N_LANE_BLOCKS = SSM_WIDTH // LANES
STATE_COLS = LANE_GROUPS * SSM_STATE
CHUNK_COLS = SSM_CHUNK * LANES
CONV_HALO = 32
VMEM_LIMIT = 56 * 1024 * 1024
MESH = pl.DeviceIdType.MESH

BIG = (
    ("w_in", "col", (D_MODEL, IN_PROJ_WIDTH)),
    ("ssm_glu_w_a", "col", (SSM_WIDTH, D_MODEL)),
    ("ssm_glu_w_b", "col", (SSM_WIDTH, D_MODEL)),
    ("cv_w_pw", "col", (CONV_WIDTH, D_MODEL)),
    ("w_out", "row", (D_MODEL, D_MODEL)),
    ("ffn_w_up", "col", (D_MODEL, 2 * FFN_HIDDEN)),
    ("ffn_w_down", "row", (FFN_HIDDEN, D_MODEL)),
)

EARLY = (0, 1, 2, 3, 4)
LATE = (5, 6)

WEIGHTS = ['w_cond', 'b_cond', 'w_in', 'b_in', 'ssm_lambda_re', 'ssm_lambda_im', 'ssm_log_dt', 'ssm_b_re', 'ssm_b_im',
           'ssm_c_re', 'ssm_c_im', 'ssm_d', 'ssm_glu_w_a', 'ssm_glu_w_b', 'cv_dw_w', 'cv_dw_b', 'cv_ln_g', 'cv_ln_b',
           'cv_w_pw', 'w_out', 'ln1_g', 'ln1_b', 'ffn_w_up', 'ffn_dw_w', 'ffn_dw_b', 'ffn_w_down', 'ln2_g', 'ln2_b']
INPUTS = ['x', 'c'] + WEIGHTS + ['loss_target'] + ['m_' + n for n in WEIGHTS] + ['v_' + n for n in WEIGHTS]

PACK = (
    ("dmod", N_COND * D_MODEL), ("c_act", D_MODEL), ("b_in", IN_PROJ_WIDTH),
    ("ssm_lambda_re", SSM_GROUPS * SSM_STATE), ("ssm_lambda_im", SSM_GROUPS * SSM_STATE), ("ssm_log_dt", SSM_GROUPS),
    ("ssm_b_re", SSM_GROUPS * SSM_STATE * SSM_GROUP), ("ssm_b_im", SSM_GROUPS * SSM_STATE * SSM_GROUP),
    ("ssm_c_re", SSM_GROUPS * SSM_STATE * SSM_GROUP), ("ssm_c_im", SSM_GROUPS * SSM_STATE * SSM_GROUP),
    ("ssm_d", SSM_GROUPS * SSM_GROUP), ("cv_dw_w", CONV_KERNEL * CONV_WIDTH), ("cv_dw_b", CONV_WIDTH),
    ("cv_ln_g", CONV_WIDTH), ("cv_ln_b", CONV_WIDTH), ("ln1_g", D_MODEL), ("ln1_b", D_MODEL),
    ("ffn_dw_w", FFN_KERNEL * 2 * FFN_HIDDEN), ("ffn_dw_b", 2 * FFN_HIDDEN), ("ln2_g", D_MODEL), ("ln2_b", D_MODEL),
    ("loss", 1),
)
PACK_COLS = 1024
PACK_ROWS = 192
assert sum(n for _, n in PACK) <= PACK_ROWS * PACK_COLS

SMALL_UPD = (
    ("b_cond", N_COND * D_MODEL), ("b_in", IN_PROJ_WIDTH),
    ("ssm_lambda_re", SSM_GROUPS * SSM_STATE), ("ssm_lambda_im", SSM_GROUPS * SSM_STATE), ("ssm_log_dt", SSM_GROUPS),
    ("ssm_b_re", SSM_GROUPS * SSM_STATE * SSM_GROUP), ("ssm_b_im", SSM_GROUPS * SSM_STATE * SSM_GROUP),
    ("ssm_c_re", SSM_GROUPS * SSM_STATE * SSM_GROUP), ("ssm_c_im", SSM_GROUPS * SSM_STATE * SSM_GROUP),
    ("ssm_d", SSM_GROUPS * SSM_GROUP), ("cv_dw_w", CONV_KERNEL * CONV_WIDTH // N_CHIP), ("cv_dw_b", CONV_WIDTH),
    ("cv_ln_g", CONV_WIDTH), ("cv_ln_b", CONV_WIDTH), ("ln1_g", D_MODEL), ("ln1_b", D_MODEL),
    ("ffn_dw_w", FFN_KERNEL * 2 * FFN_HIDDEN // N_CHIP), ("ffn_dw_b", 2 * FFN_HIDDEN), ("ln2_g", D_MODEL),
    ("ln2_b", D_MODEL),
)
assert sum(n for _, n in SMALL_UPD) <= PACK_ROWS * PACK_COLS


def _params(sem=None, **kw):
    return pltpu.CompilerParams(dimension_semantics=sem, vmem_limit_bytes=VMEM_LIMIT, **kw)


def _ln_stats(x):
    mu = jnp.mean(x, axis=-1, keepdims=True)
    xc = x - mu
    var = jnp.mean(xc * xc, axis=-1, keepdims=True)
    rstd = lax.rsqrt(var + LN_EPS)
    return xc * rstd, rstd


def _ln_bwd(dxhat, xhat, rstd):
    m1 = jnp.mean(dxhat, axis=-1, keepdims=True)
    m2 = jnp.mean(dxhat * xhat, axis=-1, keepdims=True)
    return rstd * (dxhat - m1 - xhat * m2)


def _sig(x):
    return 1.0 / (1.0 + jnp.exp(-x))


def _gelu(x):
    return 0.5 * x * (1.0 + lax.erf(x * (1.0 / math.sqrt(2.0))))


def _dgelu(x):
    return 0.5 * (1.0 + lax.erf(x * (1.0 / math.sqrt(2.0)))) + x * jnp.exp(-0.5 * x * x) * (1.0 / math.sqrt(2.0 * math.pi))


def _gelu_and_grad(x):
    er = lax.erf(x * (1.0 / math.sqrt(2.0)))
    cdf = 0.5 * (1.0 + er)
    return x * cdf, cdf + x * jnp.exp(-0.5 * x * x) * (1.0 / math.sqrt(2.0 * math.pi))


def _colsum(a):
    return jnp.sum(a, axis=0, keepdims=True)


def _fill_rotations(buf, rot, rows):
    for r in range(1, 8):
        rot[r - 1] = buf[pl.ds(r, rows), :]


def _rows_at(buf, rot, offset, tb):
    q, r = divmod(offset, 8)
    if r == 0:
        return buf[pl.ds(8 * q, tb), :]
    return rot[r - 1, pl.ds(8 * q, tb), :]


def _dot(a, b):
    return jnp.dot(a, b, preferred_element_type=F32)


def _dot_nt(a, b):
    return lax.dot_general(a, b, (((1,), (1,)), ((), ())), preferred_element_type=F32)


def _dot_tn(a, b):
    return lax.dot_general(a, b, (((0,), (0,)), ((), ())), preferred_element_type=F32)


def _load_once(src, dst, sem):
    cp = pltpu.make_async_copy(src, dst, sem)
    cp.start()
    cp.wait()


def _full(a):
    nd = a.ndim
    return pl.BlockSpec(a.shape, lambda *_: (0,) * nd)


ANY = pl.BlockSpec(memory_space=pl.ANY)


def _place():
    x, y, c = lax.axis_index("x"), lax.axis_index("y"), lax.axis_index("c")
    chips = [(1 - x, y), (x, 1 - y), (1 - x, 1 - y)]
    return x, y, c, chips


def _piece(kind, shape):
    r, cc = shape
    return (r // 2, cc // N_CHIP) if kind == "col" else (r // (2 * N_CHIP), cc)


def _piece_at(ref, kind, shape, s, k):
    pr, pc = _piece(kind, shape)
    if kind == "col":
        return ref.at[pl.ds(k * pr, pr), pl.ds(pl.multiple_of(s * pc, LANES), pc)]
    return ref.at[pl.ds(pl.multiple_of((2 * s + k) * pr, 16), pr), :]


def _late_gather_start(sh, full, send, recv):
    x, y, c, chips = _place()
    for i, m in enumerate(LATE):
        _, kind, shape = BIG[m]
        pr, _ = _piece(kind, shape)
        for j, chip in enumerate(chips):
            pltpu.make_async_remote_copy(
                src_ref=sh[i].at[pl.ds(pl.multiple_of(c * pr, 16), pr), :], dst_ref=_piece_at(full[i], kind, shape, 2 * x + y, c),
                send_sem=send.at[i, j], recv_sem=recv.at[i, j], device_id=(*chip, c), device_id_type=MESH).start()


def _late_gather_finish(sh, full, send, recv, fsend, frecv):
    x, y, c, chips = _place()
    sibling = (x, y, 1 - c)
    waits = []
    for i, m in enumerate(LATE):
        _, kind, shape = BIG[m]
        pr, _ = _piece(kind, shape)
        for j, (cx, cy) in enumerate(chips):
            got = _piece_at(full[i], kind, shape, 2 * cx + cy, c)
            first = pltpu.make_async_remote_copy(
                src_ref=sh[i].at[pl.ds(pl.multiple_of(c * pr, 16), pr), :], dst_ref=got, send_sem=send.at[i, j],
                recv_sem=recv.at[i, j], device_id=(cx, cy, c), device_id_type=MESH)
            first.wait_recv()
            fwd = pltpu.make_async_remote_copy(src_ref=got, dst_ref=got, send_sem=fsend.at[i, j], recv_sem=frecv.at[i, j],
                                               device_id=sibling, device_id_type=MESH)
            fwd.start()
            waits += [first.wait_send, fwd.wait_send]
    for i, m in enumerate(LATE):
        _, kind, shape = BIG[m]
        for j, (cx, cy) in enumerate(chips):
            got = _piece_at(full[i], kind, shape, 2 * cx + cy, 1 - c)
            pltpu.make_async_remote_copy(src_ref=got, dst_ref=got, send_sem=fsend.at[i, j], recv_sem=frecv.at[i, j],
                                         device_id=sibling, device_id_type=MESH).wait_recv()
    for w in waits:
        w()


def _late_scatter(dw, got, send, recv):
    x, y, c, _ = _place()
    cps = []
    for i, m in enumerate(LATE):
        _, kind, shape = BIG[m]
        for r in range(1, N_DEV):
            tx, ty, tc = (1 - x if r & 4 else x), (1 - y if r & 2 else y), (1 - c if r & 1 else c)
            cps.append(pltpu.make_async_remote_copy(
                src_ref=_piece_at(dw[i], kind, shape, 2 * tx + ty, tc), dst_ref=got[i].at[r - 1],
                send_sem=send.at[i, r - 1], recv_sem=recv.at[i, r - 1], device_id=(tx, ty, tc), device_id_type=MESH))
    return cps


def _f1_inproj(x, modv, b_in, w_in, tb):
    t = x.shape[0]
    chunks = [(j * 512, 512) for j in range(IN_PROJ_WIDTH // 512)]

    def body(x_ref, modv_ref, b_ref, w_hbm, u4_ref, prest_ref, h_ref, w_v, sem):
        @pl.when(pl.program_id(0) == 0)
        def _():
            _load_once(w_hbm, w_v, sem)

        xn, _ = _ln_stats(x_ref[...])
        h = (xn * (1.0 + modv_ref[1:2, :]) + modv_ref[0:1, :]).astype(BF16)
        h_ref[...] = h
        for c0, cw in chunks:
            p = _dot(h, w_v[:, c0:c0 + cw]) + b_ref[:, c0:c0 + cw]
            if c0 == 0:
                for b in range(N_LANE_BLOCKS):
                    u4_ref[b] = p[:, b * LANES:(b + 1) * LANES]
            else:
                prest_ref[:, c0 - SSM_WIDTH:c0 - SSM_WIDTH + cw] = p

    return pl.pallas_call(
        body, name="f1_inproj", grid=(t // tb,),
        in_specs=[pl.BlockSpec((tb, D_MODEL), lambda i: (i, 0)), _full(modv), _full(b_in), ANY],
        out_specs=[pl.BlockSpec((N_LANE_BLOCKS, tb, LANES), lambda i: (0, i, 0)),
                   pl.BlockSpec((tb, IN_PROJ_WIDTH - SSM_WIDTH), lambda i: (i, 0)),
                   pl.BlockSpec((tb, D_MODEL), lambda i: (i, 0))],
        out_shape=[jax.ShapeDtypeStruct((N_LANE_BLOCKS, t, LANES), F32),
                   jax.ShapeDtypeStruct((t, IN_PROJ_WIDTH - SSM_WIDTH), F32),
                   jax.ShapeDtypeStruct((t, D_MODEL), BF16)],
        scratch_shapes=[pltpu.VMEM(w_in.shape, BF16), pltpu.SemaphoreType.DMA],
        compiler_params=_params(("arbitrary",)),
    )(x, modv, b_in, w_in)


def _s5_build(lam_re, lam_im, log_dt, b_re, b_im, c_re, c_im, d):
    el, g, n, p, nb = SSM_CHUNK, SSM_GROUPS, SSM_STATE, SSM_GROUP, N_LANE_BLOCKS
    lr = jnp.minimum(lam_re, -1e-4)
    li = lam_im
    dt = jnp.exp(log_dt)[:, None]
    mag = jnp.exp(lr * dt)
    ang = li * dt
    lbr, lbi = mag * jnp.cos(ang), mag * jnp.sin(ang)
    num_r, num_i = lbr - 1.0, lbi
    den = lr * lr + li * li
    coef_r = (num_r * lr + num_i * li) / den
    coef_i = (num_i * lr - num_r * li) / den
    bbar_r = coef_r[..., None] * b_re - coef_i[..., None] * b_im
    bbar_i = coef_r[..., None] * b_im + coef_i[..., None] * b_re
    k = jnp.arange(el + 1, dtype=F32)[:, None, None]
    pmag = jnp.exp(k * (lr * dt)[None])
    pr, pi = pmag * jnp.cos(k * ang[None]), pmag * jnp.sin(k * ang[None])
    car = c_re[None] * pr[:, :, None, :] - c_im[None] * pi[:, :, None, :]
    cai = c_re[None] * pi[:, :, None, :] + c_im[None] * pr[:, :, None, :]
    bt_r = bbar_r.transpose(0, 2, 1)[None]
    bt_i = bbar_i.transpose(0, 2, 1)[None]
    kern = jnp.sum(car[:el, :, None, :, :] * bt_r[:, :, :, None, :] - cai[:el, :, None, :, :] * bt_i[:, :, :, None, :],
                   axis=-1)
    kern = kern.at[0].add(jnp.eye(p, dtype=F32)[None] * d[:, None, :])
    kc = kern.reshape(el, g * p, p)
    rev = el - 1 - jnp.arange(el)
    qr, qi = pr[rev][:, :, None, :], pi[rev][:, :, None, :]
    sw_r = (qr * bt_r - qi * bt_i).reshape(el, g * p, n)
    sw_i = (qr * bt_i + qi * bt_r).reshape(el, g * p, n)
    sg_r = car[1:].reshape(el, g * p, n)
    sg_i = (-cai[1:]).reshape(el, g * p, n)
    a = jnp.stack([pr[el].reshape(nb, LANE_GROUPS * n), pi[el].reshape(nb, LANE_GROUPS * n)], axis=1)
    return kc, sw_r, sw_i, sg_r, sg_i, a


def _expand(src, reps):
    rows, w = src.shape
    cols = reps * w
    r = lax.broadcasted_iota(jnp.int32, (w, cols), 0)
    c = lax.broadcasted_iota(jnp.int32, (w, cols), 1)
    rep = (r == (c & (w - 1))).astype(BF16)
    out = _dot(src.astype(BF16), rep)
    rg = lax.broadcasted_iota(jnp.int32, (rows, cols), 0) // SSM_GROUP
    cg = lax.broadcasted_iota(jnp.int32, (rows, cols), 1) // w
    return jnp.where(rg == cg, out, 0.0).astype(BF16)


def _fold(x, w):
    rows, cols = x.shape
    rg = lax.broadcasted_iota(jnp.int32, (rows, cols), 0) // SSM_GROUP
    cg = lax.broadcasted_iota(jnp.int32, (rows, cols), 1) // w
    x = jnp.where(rg == cg, x, 0.0)
    while cols > LANES:
        x = x[:, :cols // 2] + x[:, cols // 2:]
        cols //= 2
    s = LANES // 2
    while s >= w:
        x = x + pltpu.roll(x, s, axis=1)
        s //= 2
    return x[:, :w]


def _build_maps(s_ref, dst):
    for j in range(SSM_CHUNK):
        dst[j * LANES:(j + 1) * LANES, :] = _expand(s_ref[j], LANE_GROUPS)


def _build_toeplitz(kc_ref, dst):
    dst[...] = jnp.zeros_like(dst)
    for d in range(SSM_CHUNK):
        blk = _expand(kc_ref[d], LANE_GROUPS)
        for ji in range(SSM_CHUNK - d):
            jo = ji + d
            dst[ji * LANES:(ji + 1) * LANES, jo * LANES:(jo + 1) * LANES] = blk


def _cblk(w):
    return pl.BlockSpec((SSM_CHUNK, LANES, w), lambda b: (0, b, 0))


def _tblk(t):
    return pl.BlockSpec((1, t, LANES), lambda b: (b, 0, 0))


def _load_chunks(ref, nc):
    return jnp.concatenate([ref[0, pl.ds(j, nc, stride=SSM_CHUNK), :] for j in range(SSM_CHUNK)], axis=-1).astype(BF16)


def _store_chunks(ref, val, nc):
    for j in range(SSM_CHUNK):
        ref[0, pl.ds(j, nc, stride=SSM_CHUNK), :] = val[:, j * LANES:(j + 1) * LANES]


def _s5a_state(u4, sw_r, sw_i, a8):
    nb, t, _ = u4.shape
    nc = t // SSM_CHUNK
    sc = STATE_COLS

    def body(u_ref, swr_ref, swi_ref, a_ref, hr_ref, hi_ref, w_s, xr_s, xi_s):
        u = _load_chunks(u_ref, nc)
        _build_maps(swr_ref, w_s)
        xr_s[...] = _dot(u, w_s[...])
        _build_maps(swi_ref, w_s)
        xi_s[...] = _dot(u, w_s[...])
        ar = a_ref[0, 0:1, :]
        ai = a_ref[0, 1:2, :]

        def step(c, carry):
            hr, hi = carry
            hr_ref[0, pl.ds(c, 1), :] = hr
            hi_ref[0, pl.ds(c, 1), :] = hi
            xr = xr_s[pl.ds(c, 1), :]
            xi = xi_s[pl.ds(c, 1), :]
            return ar * hr - ai * hi + xr, ar * hi + ai * hr + xi

        z = jnp.zeros((1, sc), F32)
        lax.fori_loop(0, nc, step, (z, z))

    return pl.pallas_call(
        body, name="s5a_state", grid=(nb,),
        in_specs=[_tblk(t), _cblk(SSM_STATE), _cblk(SSM_STATE),
                  pl.BlockSpec((1, 8, sc), lambda b: (b, 0, 0))],
        out_specs=[pl.BlockSpec((1, nc, sc), lambda b: (b, 0, 0))] * 2,
        out_shape=[jax.ShapeDtypeStruct((nb, nc, sc), F32)] * 2,
        scratch_shapes=[pltpu.VMEM((CHUNK_COLS, sc), BF16), pltpu.VMEM((nc, sc), F32), pltpu.VMEM((nc, sc), F32)],
        compiler_params=_params(("arbitrary",)),
    )(u4, sw_r, sw_i, a8)


def _s5b_out(u4, kc, sg_r, sg_i, hr, hi):
    nb, t, _ = u4.shape
    nc = t // SSM_CHUNK
    sc = STATE_COLS
    cw = 512

    def body(u_ref, kc_ref, sgr_ref, sgi_ref, hr_ref, hi_ref, y_ref, tm_s, gr_s, gi_s):
        _build_toeplitz(kc_ref, tm_s)
        _build_maps(sgr_ref, gr_s)
        _build_maps(sgi_ref, gi_s)
        u = _load_chunks(u_ref, nc)
        h_r = hr_ref[0].astype(BF16)
        h_i = hi_ref[0].astype(BF16)
        for j in range(CHUNK_COLS // cw):
            cs = slice(j * cw, (j + 1) * cw)
            y = _dot(u, tm_s[:, cs]) + _dot_nt(h_r, gr_s[cs, :]) + _dot_nt(h_i, gi_s[cs, :])
            for q in range(cw // LANES):
                step = j * (cw // LANES) + q
                y_ref[0, pl.ds(step, nc, stride=SSM_CHUNK), :] = y[:, q * LANES:(q + 1) * LANES]

    return pl.pallas_call(
        body, name="s5b_out", grid=(nb,),
        in_specs=[_tblk(t), _cblk(SSM_GROUP), _cblk(SSM_STATE),
                  _cblk(SSM_STATE), pl.BlockSpec((1, nc, sc), lambda b: (b, 0, 0)),
                  pl.BlockSpec((1, nc, sc), lambda b: (b, 0, 0))],
        out_specs=_tblk(t),
        out_shape=jax.ShapeDtypeStruct((nb, t, LANES), F32),
        scratch_shapes=[pltpu.VMEM((CHUNK_COLS, CHUNK_COLS), BF16), pltpu.VMEM((CHUNK_COLS, sc), BF16),
                        pltpu.VMEM((CHUNK_COLS, sc), BF16)],
        compiler_params=_params(("arbitrary",)),
    )(u4, kc, sg_r, sg_i, hr, hi)


def _f4_mixer(ys4, prest, x, modv, cvv, cw32, w_a, w_b, w_pw, w_out, late_sh, late_full, tb):
    t = x.shape[0]
    hb = tb // CONV_HALO
    nt = t // tb
    nl = len(LATE)

    def body(ys_ref, pr_ref, halo_ref, x_ref, modv_ref, cvv_ref, cw_ref, wa_ref, wb_ref, wpw_ref, wout_ref, *rest):
        sh, full = rest[:nl], rest[2 * nl:3 * nl]
        r1_ref, ya_ref, yb_ref, ycv_ref, vc_ref, yg_ref, vs_ref, mg_ref, vbuf, vrot, send, recv, fsend, frecv = rest[3 * nl:]
        i = pl.program_id(0)

        @pl.when(i == 0)
        def _():
            _late_gather_start(sh, full, send, recv)

        ys = jnp.concatenate([ys_ref[b] for b in range(N_LANE_BLOCKS)], axis=-1)
        yg = _gelu(ys).astype(BF16)
        yg_ref[...] = yg
        ya = _dot(yg, wa_ref[...])
        yb = _dot(yg, wb_ref[...])
        ya_ref[...] = ya.astype(BF16)
        yb_ref[...] = yb.astype(BF16)
        yssm = ya * _sig(yb)
        hv = halo_ref[:, 0:CONV_WIDTH] * _sig(halo_ref[:, CONV_WIDTH:2 * CONV_WIDTH])
        vbuf[0:CONV_HALO, :] = jnp.where(i == 0, 0.0, hv)
        vbuf[CONV_HALO:, :] = pr_ref[:, 0:CONV_WIDTH] * _sig(pr_ref[:, CONV_WIDTH:2 * CONV_WIDTH])
        _fill_rotations(vbuf, vrot, tb + CONV_HALO - 8)
        acc = jnp.zeros((tb, CONV_WIDTH), F32)
        for k in range(CONV_KERNEL):
            acc += _rows_at(vbuf, vrot, CONV_HALO - CONV_KERNEL + 1 + k, tb) * cw_ref[k:k + 1, :]
        vc = acc + cvv_ref[0:1, :]
        vc_ref[...] = vc
        xh, _ = _ln_stats(vc)
        vl = xh * cvv_ref[1:2, :] + cvv_ref[2:3, :]
        vs = (vl * _sig(vl)).astype(BF16)
        vs_ref[...] = vs
        ycv = _dot(vs, wpw_ref[...])
        ycv_ref[...] = ycv.astype(BF16)
        gs = pr_ref[:, 2 * CONV_WIDTH:2 * CONV_WIDTH + D_MODEL]
        gc = pr_ref[:, 2 * CONV_WIDTH + D_MODEL:]
        merged = (_sig(gs) * yssm + _sig(gc) * ycv).astype(BF16)
        mg_ref[...] = merged
        ym = _dot(merged, wout_ref[...])
        r1_ref[...] = ALPHA * x_ref[...] + modv_ref[2:3, :] * ym

        @pl.when(i == nt - 1)
        def _():
            _late_gather_finish(sh, full, send, recv, fsend, frecv)

    tok = lambda w: pl.BlockSpec((tb, w), lambda i: (i, 0))
    sem = pltpu.SemaphoreType.DMA((nl, 3))
    n_in = 11
    return pl.pallas_call(
        body, name="f4_mixer", grid=(nt,),
        in_specs=[pl.BlockSpec((N_LANE_BLOCKS, tb, LANES), lambda i: (0, i, 0)), tok(prest.shape[1]),
                  pl.BlockSpec((CONV_HALO, 2 * CONV_WIDTH), lambda i: (jnp.maximum(i * hb - 1, 0), 0)),
                  tok(D_MODEL), _full(modv), _full(cvv), _full(cw32), _full(w_a), _full(w_b), _full(w_pw), _full(w_out)]
        + [ANY] * (2 * nl),
        out_specs=[ANY] * nl + [tok(D_MODEL), tok(D_MODEL), tok(D_MODEL), tok(D_MODEL), tok(CONV_WIDTH), tok(SSM_WIDTH),
                                tok(CONV_WIDTH), tok(D_MODEL)],
        input_output_aliases={n_in + nl + k: k for k in range(nl)},
        out_shape=[jax.ShapeDtypeStruct(f.shape, f.dtype) for f in late_full]
        + [jax.ShapeDtypeStruct((t, D_MODEL), F32), jax.ShapeDtypeStruct((t, D_MODEL), BF16),
                   jax.ShapeDtypeStruct((t, D_MODEL), BF16), jax.ShapeDtypeStruct((t, D_MODEL), BF16),
                   jax.ShapeDtypeStruct((t, CONV_WIDTH), F32), jax.ShapeDtypeStruct((t, SSM_WIDTH), BF16),
                   jax.ShapeDtypeStruct((t, CONV_WIDTH), BF16), jax.ShapeDtypeStruct((t, D_MODEL), BF16)],
        scratch_shapes=[pltpu.VMEM((tb + CONV_HALO, CONV_WIDTH), F32),
                        pltpu.VMEM((7, tb + CONV_HALO - 8, CONV_WIDTH), F32), sem, sem, sem, sem],
        compiler_params=_params(("arbitrary",)),
    )(ys4, prest, prest, x, modv, cvv, cw32, w_a, w_b, w_pw, w_out, *late_sh, *late_full)


FFN_COLS = 1408


def _f5_ffn(r1, tgt, modv, lnv, fdw, w_up, w_down, tb):
    t = r1.shape[0]
    fw = 2 * FFN_HIDDEN

    def body(r1_ref, tgt_ref, modv_ref, lnv_ref, fdw_ref, wup_hbm, wdn_hbm,
             dr2_ref, d_ref, up_ref, z_ref, acc_ref, wup_v, wdn_v, upbuf, gbuf, hbuf, sems):
        i = pl.program_id(0)

        @pl.when(i == 0)
        def _():
            _load_once(wup_hbm, wup_v, sems.at[0])
            _load_once(wdn_hbm, wdn_v, sems.at[1])
            acc_ref[...] = jnp.zeros_like(acc_ref)
            upbuf[0:8, :] = jnp.zeros((8, fw), F32)

        xh1, _ = _ln_stats(r1_ref[...])
        x1 = xh1 * lnv_ref[0:1, :] + lnv_ref[1:2, :]
        xn2, _ = _ln_stats(x1)
        h2 = (xn2 * (1.0 + modv_ref[4:5, :]) + modv_ref[3:4, :]).astype(BF16)
        for j in range(fw // FFN_COLS):
            cs = slice(j * FFN_COLS, (j + 1) * FFN_COLS)
            up = _dot(h2, wup_v[:, cs])
            upbuf[8:, cs] = up
            up_ref[:, cs] = up.astype(BF16)

        def conv(cs):
            return (fdw_ref[0:1, cs] * upbuf[pl.ds(6, tb), cs] + fdw_ref[1:2, cs] * upbuf[pl.ds(7, tb), cs]
                    + fdw_ref[2:3, cs] * upbuf[pl.ds(8, tb), cs] + fdw_ref[3:4, cs])

        halves = [(slice(j * FFN_COLS, (j + 1) * FFN_COLS),
                   slice(FFN_HIDDEN + j * FFN_COLS, FFN_HIDDEN + (j + 1) * FFN_COLS)) for j in range(FFN_HIDDEN // FFN_COLS)]
        yf = jnp.zeros((tb, D_MODEL), F32)
        for ca, cv in halves:
            v = conv(cv)
            g, dg = _gelu_and_grad(conv(ca))
            gbuf[:, ca] = g.astype(BF16)
            hbuf[:, ca] = (v * dg).astype(BF16)
            z = (g * v).astype(BF16)
            z_ref[:, ca] = z
            yf += _dot(z, wdn_v[ca, :])
        r2 = ALPHA * x1 + modv_ref[5:6, :] * yf
        xh2, rstd2 = _ln_stats(r2)
        e = xh2 * lnv_ref[2:3, :] + lnv_ref[3:4, :] - tgt_ref[...]
        dx2 = e * (1.0 / D_MODEL)
        acc_ref[3:4, :] += _colsum(e * e) * (0.5 / D_MODEL)
        acc_ref[0:1, :] += _colsum(dx2 * xh2)
        acc_ref[1:2, :] += _colsum(dx2)
        dr2 = _ln_bwd(dx2 * lnv_ref[2:3, :], xh2, rstd2)
        dr2_ref[...] = dr2
        acc_ref[2:3, :] += _colsum(dr2 * yf)
        dyf = (modv_ref[5:6, :] * dr2).astype(BF16)
        for ca, cv in halves:
            dz = _dot_nt(dyf, wdn_v[ca, :])
            d_ref[:, ca] = (dz * hbuf[:, ca].astype(F32)).astype(BF16)
            d_ref[:, cv] = (dz * gbuf[:, ca].astype(F32)).astype(BF16)
        upbuf[0:8, :] = upbuf[pl.ds(tb, 8), :]

    tok = lambda w: pl.BlockSpec((tb, w), lambda i: (i, 0))
    return pl.pallas_call(
        body, name="f5_ffn", grid=(t // tb,),
        in_specs=[tok(D_MODEL), tok(D_MODEL), _full(modv), _full(lnv), _full(fdw), ANY, ANY],
        out_specs=[tok(D_MODEL), tok(fw), tok(fw), tok(FFN_HIDDEN), pl.BlockSpec((8, D_MODEL), lambda i: (0, 0))],
        out_shape=[jax.ShapeDtypeStruct((t, D_MODEL), F32), jax.ShapeDtypeStruct((t, fw), BF16),
                   jax.ShapeDtypeStruct((t, fw), BF16), jax.ShapeDtypeStruct((t, FFN_HIDDEN), BF16),
                   jax.ShapeDtypeStruct((8, D_MODEL), F32)],
        scratch_shapes=[pltpu.VMEM(w_up.shape, BF16), pltpu.VMEM(w_down.shape, BF16),
                        pltpu.VMEM((tb + 8, fw), F32), pltpu.VMEM((tb, FFN_HIDDEN), BF16),
                        pltpu.VMEM((tb, FFN_HIDDEN), BF16), pltpu.SemaphoreType.DMA((2,))],
        compiler_params=_params(("arbitrary",)),
    )(r1, tgt, modv, lnv, fdw, w_up, w_down)


def _b1b_ffn_up(d, up, dr2, r1, modv, lnv, fdw, w_up, tb):
    t = dr2.shape[0]
    fw = 2 * FFN_HIDDEN
    nt = t // tb
    hb = tb // 16

    def body(d_ref, nxt_ref, up_ref, dr2_ref, r1_ref, modv_ref, lnv_ref, fdw_ref, wup_hbm, dup_ref, dr1_ref, h2_ref,
             dyf_ref, acc_ref, accw_ref, wup_v, dbuf, shifted, sem):
        i = pl.program_id(0)

        @pl.when(i == 0)
        def _():
            _load_once(wup_hbm, wup_v, sem)
            acc_ref[...] = jnp.zeros_like(acc_ref)
            accw_ref[...] = jnp.zeros_like(accw_ref)

        dbuf[0:tb, :] = d_ref[...].astype(F32)
        dbuf[tb:, :] = jnp.where(i == nt - 1, 0.0, nxt_ref[...].astype(F32))
        dh2 = jnp.zeros((tb, D_MODEL), F32)
        for j in range(fw // FFN_COLS):
            cs = slice(j * FFN_COLS, (j + 1) * FFN_COLS)
            for k in range(1, FFN_KERNEL):
                shifted[k - 1] = dbuf[pl.ds(k, tb), cs]
            ds = [dbuf[pl.ds(0, tb), cs], shifted[0], shifted[1]]
            dup = (fdw_ref[2:3, cs] * ds[0] + fdw_ref[1:2, cs] * ds[1] + fdw_ref[0:1, cs] * ds[2]).astype(BF16)
            dup_ref[:, cs] = dup
            dh2 += _dot_nt(dup, wup_v[:, cs])
            upf = up_ref[:, cs].astype(F32)
            for k in range(FFN_KERNEL):
                accw_ref[k:k + 1, cs] += _colsum(ds[FFN_KERNEL - 1 - k] * upf)
            accw_ref[3:4, cs] += _colsum(ds[0])
        xh1, rstd1 = _ln_stats(r1_ref[...])
        x1 = xh1 * lnv_ref[0:1, :] + lnv_ref[1:2, :]
        xn2, rstd2 = _ln_stats(x1)
        h2_ref[...] = (xn2 * (1.0 + modv_ref[4:5, :]) + modv_ref[3:4, :]).astype(BF16)
        dr2 = dr2_ref[...]
        dyf_ref[...] = (modv_ref[5:6, :] * dr2).astype(BF16)
        acc_ref[0:1, :] += _colsum(dh2 * xn2)
        acc_ref[1:2, :] += _colsum(dh2)
        dx1 = _ln_bwd(dh2 * (1.0 + modv_ref[4:5, :]), xn2, rstd2) + ALPHA * dr2
        acc_ref[2:3, :] += _colsum(dx1 * xh1)
        acc_ref[3:4, :] += _colsum(dx1)
        dr1_ref[...] = _ln_bwd(dx1 * lnv_ref[0:1, :], xh1, rstd1)

    tok = lambda w: pl.BlockSpec((tb, w), lambda i: (i, 0))
    return pl.pallas_call(
        body, name="b1b_ffn_up", grid=(nt,),
        in_specs=[tok(fw), pl.BlockSpec((16, fw), lambda i: (jnp.minimum((i + 1) * hb, t // 16 - 1), 0)), tok(fw),
                  tok(D_MODEL), tok(D_MODEL), _full(modv), _full(lnv), _full(fdw), ANY],
        out_specs=[tok(fw), tok(D_MODEL), tok(D_MODEL), tok(D_MODEL), pl.BlockSpec((8, D_MODEL), lambda i: (0, 0)),
                   pl.BlockSpec((8, fw), lambda i: (0, 0))],
        out_shape=[jax.ShapeDtypeStruct((t, fw), BF16), jax.ShapeDtypeStruct((t, D_MODEL), F32),
                   jax.ShapeDtypeStruct((t, D_MODEL), BF16), jax.ShapeDtypeStruct((t, D_MODEL), BF16),
                   jax.ShapeDtypeStruct((8, D_MODEL), F32), jax.ShapeDtypeStruct((8, fw), F32)],
        scratch_shapes=[pltpu.VMEM(w_up.shape, BF16), pltpu.VMEM((tb + 16, fw), F32),
                        pltpu.VMEM((FFN_KERNEL - 1, tb, FFN_COLS), F32), pltpu.SemaphoreType.DMA],
        compiler_params=_params(("arbitrary",)),
    )(d, d, up, dr2, r1, modv, lnv, fdw, w_up)


def _b2_mixer(dr1, ys4, prest, ya, yb, ycv, vc, merged, modv, cvv, cw32, w_a, w_b, w_pw, w_out, late_dw, tb):
    t = dr1.shape[0]
    nt = t // tb
    nl = len(LATE)
    hb = tb // CONV_HALO
    cwd = CONV_WIDTH

    def body(dr1_ref, ys_ref, pr_ref, halo_ref, ya_ref, yb_ref, ycv_ref, vc_ref, mg_ref, modv_ref, cvv_ref, cw_ref,
             wa_ref, wb_ref, wpw_ref, wout_ref, *rest):
        dw, got = rest[:nl], rest[nl:2 * nl]
        (dys_ref, dpr_ref, dya_ref, dyb_ref, dycv_ref, dym_ref, acc_a, acc_b, acc_w, vbuf, dvbuf, vrot, dvrot,
         send, recv) = rest[2 * nl:]
        i = pl.program_id(0)
        ti = nt - 1 - i

        @pl.when(i == 0)
        def _():
            for cp in _late_scatter(dw, got, send, recv):
                cp.start()
            acc_a[...] = jnp.zeros_like(acc_a)
            acc_b[...] = jnp.zeros_like(acc_b)
            acc_w[...] = jnp.zeros_like(acc_w)
            dvbuf[pl.ds(tb, CONV_HALO), :] = jnp.zeros((CONV_HALO, cwd), F32)

        dr1 = dr1_ref[...]
        dym = (modv_ref[2:3, :] * dr1).astype(BF16)
        dym_ref[...] = dym
        ym = _dot(mg_ref[...], wout_ref[...])
        acc_a[0:1, :] += _colsum(dr1 * ym)
        dmg = _dot_nt(dym, wout_ref[...])
        sgs = _sig(pr_ref[:, 2 * cwd:2 * cwd + D_MODEL])
        sgc = _sig(pr_ref[:, 2 * cwd + D_MODEL:])
        ya_v = ya_ref[...].astype(F32)
        syb = _sig(yb_ref[...].astype(F32))
        ycv_v = ycv_ref[...].astype(F32)
        dpr_ref[:, 2 * cwd:2 * cwd + D_MODEL] = (dmg * (ya_v * syb) * sgs * (1.0 - sgs)).astype(BF16)
        dpr_ref[:, 2 * cwd + D_MODEL:] = (dmg * ycv_v * sgc * (1.0 - sgc)).astype(BF16)
        dyssm = dmg * sgs
        dya = (dyssm * syb).astype(BF16)
        dyb = (dyssm * ya_v * syb * (1.0 - syb)).astype(BF16)
        dya_ref[...] = dya
        dyb_ref[...] = dyb
        dyg = _dot_nt(dya, wa_ref[...]) + _dot_nt(dyb, wb_ref[...])
        ys = jnp.concatenate([ys_ref[b] for b in range(N_LANE_BLOCKS)], axis=-1)
        dys = dyg * _dgelu(ys)
        for b in range(N_LANE_BLOCKS):
            dys_ref[b] = dys[:, b * LANES:(b + 1) * LANES]
        dycv = (dmg * sgc).astype(BF16)
        dycv_ref[...] = dycv
        dvs = _dot_nt(dycv, wpw_ref[...])
        xh, rstd = _ln_stats(vc_ref[...])
        vl = xh * cvv_ref[1:2, :] + cvv_ref[2:3, :]
        s = _sig(vl)
        dvl = dvs * s * (1.0 + vl * (1.0 - s))
        acc_b[1:2, :] += _colsum(dvl * xh)
        acc_b[2:3, :] += _colsum(dvl)
        dvc = _ln_bwd(dvl * cvv_ref[1:2, :], xh, rstd)
        acc_b[0:1, :] += _colsum(dvc)
        hv = halo_ref[:, 0:cwd] * _sig(halo_ref[:, cwd:2 * cwd])
        vbuf[0:CONV_HALO, :] = jnp.where(ti == 0, 0.0, hv)
        cva = pr_ref[:, 0:cwd]
        scg = _sig(pr_ref[:, cwd:2 * cwd])
        vbuf[CONV_HALO:, :] = cva * scg
        dvbuf[0:tb, :] = dvc
        _fill_rotations(vbuf, vrot, tb + CONV_HALO - 8)
        _fill_rotations(dvbuf, dvrot, tb + CONV_HALO - 8)
        dv = jnp.zeros((tb, cwd), F32)
        for k in range(CONV_KERNEL):
            dv += _rows_at(dvbuf, dvrot, CONV_KERNEL - 1 - k, tb) * cw_ref[k:k + 1, :]
            acc_w[k:k + 1, :] += _colsum(dvc * _rows_at(vbuf, vrot, CONV_HALO - CONV_KERNEL + 1 + k, tb))
        dvbuf[pl.ds(tb, CONV_HALO), :] = dvbuf[0:CONV_HALO, :]
        dpr_ref[:, 0:cwd] = (dv * scg).astype(BF16)
        dpr_ref[:, cwd:2 * cwd] = (dv * cva * scg * (1.0 - scg)).astype(BF16)

        @pl.when(i == nt - 1)
        def _():
            for cp in _late_scatter(dw, got, send, recv):
                cp.wait()

    rtok = lambda w: pl.BlockSpec((tb, w), lambda i: (nt - 1 - i, 0))
    r4 = pl.BlockSpec((N_LANE_BLOCKS, tb, LANES), lambda i: (0, nt - 1 - i, 0))
    pw = prest.shape[1]
    return pl.pallas_call(
        body, name="b2_mixer", grid=(nt,),
        in_specs=[rtok(D_MODEL), r4, rtok(pw),
                  pl.BlockSpec((CONV_HALO, 2 * cwd), lambda i: (jnp.maximum((nt - 1 - i) * hb - 1, 0), 0)),
                  rtok(D_MODEL), rtok(D_MODEL), rtok(D_MODEL), rtok(cwd), rtok(D_MODEL),
                  _full(modv), _full(cvv), _full(cw32), _full(w_a), _full(w_b), _full(w_pw), _full(w_out)] + [ANY] * nl,
        out_specs=[ANY] * nl + [r4, rtok(pw), rtok(D_MODEL), rtok(D_MODEL), rtok(D_MODEL), rtok(D_MODEL),
                   pl.BlockSpec((8, D_MODEL), lambda i: (0, 0)), pl.BlockSpec((8, cwd), lambda i: (0, 0)),
                   pl.BlockSpec((CONV_HALO, cwd), lambda i: (0, 0))],
        out_shape=[jax.ShapeDtypeStruct((N_DEV - 1,) + _piece(*BIG[m][1:]), BF16) for m in LATE]
        + [jax.ShapeDtypeStruct((N_LANE_BLOCKS, t, LANES), F32), jax.ShapeDtypeStruct((t, pw), BF16),
                   jax.ShapeDtypeStruct((t, D_MODEL), BF16), jax.ShapeDtypeStruct((t, D_MODEL), BF16),
                   jax.ShapeDtypeStruct((t, D_MODEL), BF16), jax.ShapeDtypeStruct((t, D_MODEL), BF16),
                   jax.ShapeDtypeStruct((8, D_MODEL), F32), jax.ShapeDtypeStruct((8, cwd), F32),
                   jax.ShapeDtypeStruct((CONV_HALO, cwd), F32)],
        scratch_shapes=[pltpu.VMEM((tb + CONV_HALO, cwd), F32), pltpu.VMEM((tb + CONV_HALO, cwd), F32),
                        pltpu.VMEM((7, tb + CONV_HALO - 8, cwd), F32), pltpu.VMEM((7, tb + CONV_HALO - 8, cwd), F32),
                        pltpu.SemaphoreType.DMA((nl, N_DEV - 1)), pltpu.SemaphoreType.DMA((nl, N_DEV - 1))],
        compiler_params=_params(("arbitrary",)),
    )(dr1, ys4, prest, prest, ya, yb, ycv, vc, merged, modv, cvv, cw32, w_a, w_b, w_pw, w_out, *late_dw)


def _s5c_state_bwd(dy4, sg_r, sg_i, a8, hr, hi):
    nb, t, _ = dy4.shape
    nc = t // SSM_CHUNK
    sc = STATE_COLS

    def body(dy_ref, sgr_ref, sgi_ref, a_ref, hr_ref, hi_ref, dxr_ref, dxi_ref, da_ref, dsgr_ref, dsgi_ref,
             g_s, lr_s, li_s, xr_s, xi_s):
        dy = _load_chunks(dy_ref, nc)
        _build_maps(sgr_ref, g_s)
        lr_s[...] = _dot(dy, g_s[...])
        _build_maps(sgi_ref, g_s)
        li_s[...] = _dot(dy, g_s[...])
        ar = a_ref[0, 0:1, :]
        ai = a_ref[0, 1:2, :]

        def step(k, carry):
            pr, pi, dar, dai = carry
            c = nc - 1 - k
            xr_s[pl.ds(c, 1), :] = pr
            xi_s[pl.ds(c, 1), :] = pi
            h_r = hr_ref[0, pl.ds(c, 1), :]
            h_i = hi_ref[0, pl.ds(c, 1), :]
            dar = dar + pr * h_r + pi * h_i
            dai = dai - pr * h_i + pi * h_r
            nr = lr_s[pl.ds(c, 1), :] + ar * pr + ai * pi
            ni = li_s[pl.ds(c, 1), :] - ai * pr + ar * pi
            return nr, ni, dar, dai

        z = jnp.zeros((1, sc), F32)
        _, _, dar, dai = lax.fori_loop(0, nc, step, (z, z, z, z))
        da_ref[0] = jnp.concatenate([dar, dai, jnp.zeros((6, sc), F32)], axis=0)
        dxr_ref[0] = xr_s[...].astype(BF16)
        dxi_ref[0] = xi_s[...].astype(BF16)
        for h_ref, o_ref in ((hr_ref, dsgr_ref), (hi_ref, dsgi_ref)):
            hb = h_ref[0].astype(BF16)
            for j in range(SSM_CHUNK):
                o_ref[j] = _fold(_dot_tn(dy[:, j * LANES:(j + 1) * LANES], hb), SSM_STATE)

    blk = lambda r, c: pl.BlockSpec((1, r, c), lambda b: (b, 0, 0))
    return pl.pallas_call(
        body, name="s5c_state_bwd", grid=(nb,),
        in_specs=[_tblk(t), _cblk(SSM_STATE), _cblk(SSM_STATE), blk(8, sc), blk(nc, sc), blk(nc, sc)],
        out_specs=[blk(nc, sc), blk(nc, sc), blk(8, sc), _cblk(SSM_STATE), _cblk(SSM_STATE)],
        out_shape=[jax.ShapeDtypeStruct((nb, nc, sc), BF16), jax.ShapeDtypeStruct((nb, nc, sc), BF16),
                   jax.ShapeDtypeStruct((nb, 8, sc), F32),
                   jax.ShapeDtypeStruct((SSM_CHUNK, SSM_WIDTH, SSM_STATE), F32),
                   jax.ShapeDtypeStruct((SSM_CHUNK, SSM_WIDTH, SSM_STATE), F32)],
        scratch_shapes=[pltpu.VMEM((CHUNK_COLS, sc), BF16)] + [pltpu.VMEM((nc, sc), F32)] * 4,
        compiler_params=_params(("arbitrary",)),
    )(dy4, sg_r, sg_i, a8, hr, hi)


def _s5d_input_bwd(dy4, u4, kc, sw_r, sw_i, dxr, dxi):
    nb, t, _ = dy4.shape
    nc = t // SSM_CHUNK
    sc = STATE_COLS

    def body(dy_ref, u_ref, kc_ref, swr_ref, swi_ref, dxr_ref, dxi_ref, du_ref, dkc_ref, dswr_ref, dswi_ref,
             tm_s, w_s, dk_s):
        dy = _load_chunks(dy_ref, nc)
        u = _load_chunks(u_ref, nc)
        _build_toeplitz(kc_ref, tm_s)
        du = _dot_nt(dy, tm_s[...])
        _build_maps(swr_ref, w_s)
        du += _dot_nt(dxr_ref[0], w_s[...])
        _build_maps(swi_ref, w_s)
        du += _dot_nt(dxi_ref[0], w_s[...])
        _store_chunks(du_ref, du, nc)
        dk_s[...] = jnp.zeros_like(dk_s)
        for ji in range(SSM_CHUNK):
            uj = u[:, ji * LANES:(ji + 1) * LANES]
            rows = _dot_tn(uj, dy)
            for jo in range(ji, SSM_CHUNK):
                dk_s[jo - ji] += rows[:, jo * LANES:(jo + 1) * LANES]
            dswr_ref[ji] = _fold(_dot_tn(uj, dxr_ref[0]), SSM_STATE)
            dswi_ref[ji] = _fold(_dot_tn(uj, dxi_ref[0]), SSM_STATE)
        for d in range(SSM_CHUNK):
            dkc_ref[d] = _fold(dk_s[d], SSM_GROUP)

    blk = lambda r, c: pl.BlockSpec((1, r, c), lambda b: (b, 0, 0))
    return pl.pallas_call(
        body, name="s5d_input_bwd", grid=(nb,),
        in_specs=[_tblk(t), _tblk(t), _cblk(SSM_GROUP), _cblk(SSM_STATE), _cblk(SSM_STATE),
                  blk(nc, sc), blk(nc, sc)],
        out_specs=[_tblk(t), _cblk(SSM_GROUP), _cblk(SSM_STATE), _cblk(SSM_STATE)],
        out_shape=[jax.ShapeDtypeStruct((nb, t, LANES), F32),
                   jax.ShapeDtypeStruct((SSM_CHUNK, SSM_WIDTH, SSM_GROUP), F32),
                   jax.ShapeDtypeStruct((SSM_CHUNK, SSM_WIDTH, SSM_STATE), F32),
                   jax.ShapeDtypeStruct((SSM_CHUNK, SSM_WIDTH, SSM_STATE), F32)],
        scratch_shapes=[pltpu.VMEM((CHUNK_COLS, CHUNK_COLS), BF16), pltpu.VMEM((CHUNK_COLS, sc), BF16),
                        pltpu.VMEM((SSM_CHUNK, LANES, LANES), F32)],
        compiler_params=_params(("arbitrary",)),
    )(dy4, u4, kc, sw_r, sw_i, dxr, dxi)


def _b3_inproj(x, dr1, du4, dprest, modv, w_in, tb):
    t = x.shape[0]
    pw = IN_PROJ_WIDTH - SSM_WIDTH

    def body(x_ref, dr1_ref, du_ref, dpr_ref, modv_ref, w_hbm, gx_ref, dp_ref, acc_ref, accb_ref, w_v, sem):
        @pl.when(pl.program_id(0) == 0)
        def _():
            _load_once(w_hbm, w_v, sem)
            acc_ref[...] = jnp.zeros_like(acc_ref)
            accb_ref[...] = jnp.zeros_like(accb_ref)

        du = jnp.concatenate([du_ref[b] for b in range(N_LANE_BLOCKS)], axis=-1).astype(BF16)
        dpr = dpr_ref[...]
        dp_ref[:, 0:SSM_WIDTH] = du
        dp_ref[:, SSM_WIDTH:] = dpr
        accb_ref[0:1, 0:SSM_WIDTH] += _colsum(du.astype(F32))
        accb_ref[0:1, SSM_WIDTH:] += _colsum(dpr.astype(F32))
        dh = _dot_nt(du, w_v[:, 0:SSM_WIDTH]) + _dot_nt(dpr, w_v[:, SSM_WIDTH:])
        xn, rstd = _ln_stats(x_ref[...])
        acc_ref[0:1, :] += _colsum(dh * xn)
        acc_ref[1:2, :] += _colsum(dh)
        gx_ref[...] = _ln_bwd(dh * (1.0 + modv_ref[1:2, :]), xn, rstd) + ALPHA * dr1_ref[...]

    tok = lambda w: pl.BlockSpec((tb, w), lambda i: (i, 0))
    return pl.pallas_call(
        body, name="b3_inproj", grid=(t // tb,),
        in_specs=[tok(D_MODEL), tok(D_MODEL), pl.BlockSpec((N_LANE_BLOCKS, tb, LANES), lambda i: (0, i, 0)), tok(pw),
                  _full(modv), ANY],
        out_specs=[tok(D_MODEL), tok(IN_PROJ_WIDTH), pl.BlockSpec((8, D_MODEL), lambda i: (0, 0)),
                   pl.BlockSpec((8, IN_PROJ_WIDTH), lambda i: (0, 0))],
        out_shape=[jax.ShapeDtypeStruct((t, D_MODEL), F32), jax.ShapeDtypeStruct((t, IN_PROJ_WIDTH), BF16),
                   jax.ShapeDtypeStruct((8, D_MODEL), F32), jax.ShapeDtypeStruct((8, IN_PROJ_WIDTH), F32)],
        scratch_shapes=[pltpu.VMEM(w_in.shape, BF16), pltpu.SemaphoreType.DMA],
        compiler_params=_params(("arbitrary",)),
    )(x, dr1, du4, dprest, modv, w_in)


TN_ROWS = 2048


def _tn_matmul(name, a, b, tm, tn):
    t, m = a.shape
    n = b.shape[1]
    tt = min(TN_ROWS, t)
    nk = t // tt

    def body(a_ref, b_ref, o_ref, acc):
        k = pl.program_id(2)

        @pl.when(k == 0)
        def _():
            acc[...] = jnp.zeros_like(acc)

        acc[...] += _dot_tn(a_ref[...], b_ref[...])

        @pl.when(k == nk - 1)
        def _():
            o_ref[...] = acc[...].astype(BF16)

    return pl.pallas_call(
        body, name=name, grid=(m // tm, n // tn, nk),
        in_specs=[pl.BlockSpec((tt, tm), lambda i, j, k: (k, i)), pl.BlockSpec((tt, tn), lambda i, j, k: (k, j))],
        out_specs=pl.BlockSpec((tm, tn), lambda i, j, k: (i, j)),
        out_shape=jax.ShapeDtypeStruct((m, n), BF16),
        scratch_shapes=[pltpu.VMEM((tm, tn), F32)],
        compiler_params=_params(("arbitrary", "arbitrary", "arbitrary")),
    )(a, b)


def _local_step(x, tgt, modv, wb, late_sh, sp, tb=256):
    t = x.shape[0]
    row8 = lambda rows, w: jnp.concatenate([r.reshape(1, w) for r in rows] + [jnp.zeros((8 - len(rows), w), F32)], axis=0)
    lnv = row8([sp["ln1_g"], sp["ln1_b"], sp["ln2_g"], sp["ln2_b"]], D_MODEL)
    cvv = row8([sp["cv_dw_b"], sp["cv_ln_g"], sp["cv_ln_b"]], CONV_WIDTH)
    cw32 = jnp.concatenate([sp["cv_dw_w"].reshape(CONV_KERNEL, CONV_WIDTH), jnp.zeros((1, CONV_WIDTH), F32)], axis=0)
    fdw = row8(list(sp["ffn_dw_w"].reshape(FFN_KERNEL, 2 * FFN_HIDDEN)) + [sp["ffn_dw_b"]], 2 * FFN_HIDDEN)
    b_in = sp["b_in"].reshape(1, IN_PROJ_WIDTH)
    ssm = tuple(sp[k] for k in ("ssm_lambda_re", "ssm_lambda_im", "ssm_log_dt", "ssm_b_re", "ssm_b_im", "ssm_c_re",
                                "ssm_c_im", "ssm_d"))
    (kc, sw_r, sw_i, sg_r, sg_i, a), ssm_vjp = jax.vjp(_s5_build, *ssm)
    a8 = jnp.concatenate([a, jnp.zeros((N_LANE_BLOCKS, 6, STATE_COLS), F32)], axis=1)

    u4, prest, h1 = _f1_inproj(x, modv, b_in, wb["w_in"], tb)
    hr, hi = _s5a_state(u4, sw_r, sw_i, a8)
    ys4 = _s5b_out(u4, kc, sg_r, sg_i, hr, hi)
    w_up, w_down, r1, ya, yb, ycv, vc, yg, vs, merged = _f4_mixer(
        ys4, prest, x, modv, cvv, cw32, wb["ssm_glu_w_a"], wb["ssm_glu_w_b"], wb["cv_w_pw"], wb["w_out"], late_sh,
        [wb[BIG[m][0]] for m in LATE], tb)
    dr2, dconv, up, z, acc5 = _f5_ffn(r1, tgt, modv, lnv, fdw, w_up, w_down, tb)
    dup, dr1, h2, dyf, acc1b, acc1a = _b1b_ffn_up(dconv, up, dr2, r1, modv, lnv, fdw, w_up, tb)
    late_dw = [_tn_matmul("dw_up", h2, dup, 1024, FFN_COLS), _tn_matmul("dw_down", z, dyf, FFN_COLS, 1024)]
    got_up, got_down, dys4, dprest, dya, dyb, dycv, dym, acc2a, acc2b, acc2w = _b2_mixer(
        dr1, ys4, prest, ya, yb, ycv, vc, merged, modv, cvv, cw32, wb["ssm_glu_w_a"], wb["ssm_glu_w_b"],
        wb["cv_w_pw"], wb["w_out"], late_dw, tb)
    dxr, dxi, da8, dsg_r, dsg_i = _s5c_state_bwd(dys4, sg_r, sg_i, a8, hr, hi)
    du4, dkc, dsw_r, dsw_i = _s5d_input_bwd(dys4, u4, kc, sw_r, sw_i, dxr, dxi)
    dssm = ssm_vjp((dkc, dsw_r, dsw_i, dsg_r, dsg_i, da8[:, 0:2, :]))
    gx, dp, acc3, acc3b = _b3_inproj(x, dr1, du4, dprest, modv, wb["w_in"], tb)

    dbig = {
        "w_in": _tn_matmul("dw_in", h1, dp, 1024, 896),
        "ssm_glu_w_a": _tn_matmul("dw_glu_a", yg, dya, 512, 1024),
        "ssm_glu_w_b": _tn_matmul("dw_glu_b", yg, dyb, 512, 1024),
        "cv_w_pw": _tn_matmul("dw_pw", vs, dycv, 512, 1024),
        "w_out": _tn_matmul("dw_out", merged, dym, 1024, 1024),
    }
    dmod = jnp.concatenate([acc3[1], acc3[0], acc2a[0], acc1b[1], acc1b[0], acc5[2]])
    small = {
        "dmod": dmod, "b_in": acc3b[0],
        "ssm_lambda_re": dssm[0], "ssm_lambda_im": dssm[1], "ssm_log_dt": dssm[2], "ssm_b_re": dssm[3],
        "ssm_b_im": dssm[4], "ssm_c_re": dssm[5], "ssm_c_im": dssm[6], "ssm_d": dssm[7],
        "cv_dw_w": acc2w[0:CONV_KERNEL], "cv_dw_b": acc2b[0], "cv_ln_g": acc2b[1], "cv_ln_b": acc2b[2],
        "ln1_g": acc1b[2], "ln1_b": acc1b[3], "ffn_dw_w": acc1a[0:FFN_KERNEL], "ffn_dw_b": acc1a[3],
        "ln2_g": acc5[0], "ln2_b": acc5[1], "loss": jnp.sum(acc5[3]).reshape(1),
    }
    return gx, dbig, (late_dw, [got_up, got_down]), small


def _allgather(name, shard):
    m_per, n = shard.shape

    def body(x_ref, out_ref, send_sems, recv_sems, local_sem):
        x, y, c, chips = _place()
        me, sibling = (x, y, c), (x, y, 1 - c)

        def rows(px, py, pc):
            return out_ref.at[pl.ds((4 * px + 2 * py + pc) * m_per, m_per), :]

        def copy(k, block, to, src=None):
            return pltpu.make_async_remote_copy(
                src_ref=rows(*block) if src is None else src, dst_ref=rows(*block),
                send_sem=send_sems.at[k], recv_sem=recv_sems.at[k], device_id=to, device_id_type=MESH)

        mine = pltpu.make_async_copy(x_ref, rows(*me), local_sem)
        mine.start()
        first = [copy(0, me, sibling, src=x_ref)]
        first += [copy(1 + j, me, (*chip, c), src=x_ref) for j, chip in enumerate(chips)]
        for cp in first:
            cp.start()
        passed = [copy(4 + j, (*chip, c), sibling) for j, chip in enumerate(chips)]
        for j, chip in enumerate(chips):
            copy(1 + j, (*chip, c), me).wait_recv()
            passed[j].start()
        copy(0, sibling, me).wait_recv()
        for j, chip in enumerate(chips):
            copy(4 + j, (*chip, 1 - c), me).wait_recv()
        for cp in first + passed:
            cp.wait_send()
        mine.wait()

    return pl.pallas_call(
        body, name=name,
        out_shape=jax.ShapeDtypeStruct((N_DEV * m_per, n), shard.dtype),
        in_specs=[pl.BlockSpec(memory_space=pltpu.VMEM)],
        out_specs=pl.BlockSpec(memory_space=pltpu.VMEM),
        scratch_shapes=[pltpu.SemaphoreType.DMA((7,)), pltpu.SemaphoreType.DMA((7,)), pltpu.SemaphoreType.DMA],
        compiler_params=_params(),
    )(shard)


def _add_rows(pr):
    return 64 if pr % 64 == 0 else 16


def _gather_weights(shards):
    nm = len(BIG)
    nl = len(LATE)

    def body(*refs):
        ins, outs, lsh = refs[:nm], refs[nm:2 * nm], refs[2 * nm:2 * nm + nl]
        stage = refs[2 * nm + nl:3 * nm + nl]
        send, recv, fsend, frecv, lsem = refs[3 * nm + nl:]
        x, y, c, chips = _place()
        s_me = 2 * x + y
        sibling = (x, y, 1 - c)
        pend = []
        for m in range(nm):
            stage[m][...] = ins[m][...].astype(BF16)
        for m, (_, kind, shape) in enumerate(BIG):
            pr, pc = _piece(kind, shape)
            for k in range(2):
                cp = pltpu.make_async_copy(stage[m].at[pl.ds(k * pr, pr), :], _piece_at(outs[m], kind, shape, s_me, k),
                                           lsem.at[m, k])
                cp.start()
                pend.append(cp.wait)
            if m in LATE:
                cp = pltpu.make_async_copy(stage[m], lsh[LATE.index(m)], lsem.at[m, 2])
                cp.start()
                pend.append(cp.wait)
                continue
            for j, chip in enumerate(chips):
                cp = pltpu.make_async_remote_copy(
                    src_ref=stage[m].at[pl.ds(pl.multiple_of(c * pr, 16), pr), :],
                    dst_ref=_piece_at(outs[m], kind, shape, s_me, c),
                    send_sem=send.at[m, j], recv_sem=recv.at[m, j], device_id=(*chip, c), device_id_type=MESH)
                cp.start()
                pend.append(cp.wait_send)
        for m in EARLY:
            _, kind, shape = BIG[m]
            for j, (cx, cy) in enumerate(chips):
                got = _piece_at(outs[m], kind, shape, 2 * cx + cy, c)
                pltpu.make_async_remote_copy(src_ref=got, dst_ref=got, send_sem=send.at[m, j], recv_sem=recv.at[m, j],
                                             device_id=(cx, cy, c), device_id_type=MESH).wait_recv()
                cp = pltpu.make_async_remote_copy(src_ref=got, dst_ref=got, send_sem=fsend.at[m, j],
                                                  recv_sem=frecv.at[m, j], device_id=sibling, device_id_type=MESH)
                cp.start()
                pend.append(cp.wait_send)
        for m in EARLY:
            _, kind, shape = BIG[m]
            for j, (cx, cy) in enumerate(chips):
                got = _piece_at(outs[m], kind, shape, 2 * cx + cy, 1 - c)
                pltpu.make_async_remote_copy(src_ref=got, dst_ref=got, send_sem=fsend.at[m, j], recv_sem=frecv.at[m, j],
                                             device_id=sibling, device_id_type=MESH).wait_recv()
        for w in pend:
            w()

    sem = lambda *s: pltpu.SemaphoreType.DMA(s)
    res = pl.pallas_call(
        body, name="gather_weights",
        out_shape=[jax.ShapeDtypeStruct(shape, BF16) for _, _, shape in BIG]
        + [jax.ShapeDtypeStruct(shards[m].shape, BF16) for m in LATE],
        in_specs=[pl.BlockSpec(memory_space=pltpu.VMEM)] * nm,
        out_specs=[ANY] * (nm + nl),
        scratch_shapes=[pltpu.VMEM(s.shape, BF16) for s in shards] + [sem(nm, 3), sem(nm, 3), sem(nm, 3), sem(nm, 3),
                                                                         sem(nm, 3)],
        compiler_params=_params(),
    )(*shards)
    return res[:nm], res[nm:]


def _rs1_sibling(grads):
    mats = [BIG[m] for m in EARLY]
    nm = len(mats)

    def body(*refs):
        ins, outs = refs[:nm], refs[nm:2 * nm]
        send, recv = refs[2 * nm:]
        x, y, c, _ = _place()
        cps = []
        for m, (_, kind, shape) in enumerate(mats):
            for s in range(N_CHIP):
                cp = pltpu.make_async_remote_copy(
                    src_ref=_piece_at(ins[m], kind, shape, s, 1 - c), dst_ref=outs[m].at[s],
                    send_sem=send.at[m, s], recv_sem=recv.at[m, s], device_id=(x, y, 1 - c), device_id_type=MESH)
                cp.start()
                cps.append(cp)
        for cp in cps:
            cp.wait()

    sem = lambda *s: pltpu.SemaphoreType.DMA(s)
    return pl.pallas_call(
        body, name="rs1_sibling",
        out_shape=[jax.ShapeDtypeStruct((N_CHIP,) + _piece(kind, shape), BF16) for _, kind, shape in mats],
        in_specs=[ANY] * nm, out_specs=[ANY] * nm,
        scratch_shapes=[sem(nm, N_CHIP), sem(nm, N_CHIP)],
        compiler_params=_params(),
    )(*grads)


def _rs2_chips(grads, halves):
    mats = [BIG[m] for m in EARLY]
    nm = len(mats)

    def body(*refs):
        gin, hin = refs[:nm], refs[nm:2 * nm]
        own, got = refs[2 * nm:3 * nm], refs[3 * nm:4 * nm]
        send, recv, lsem = refs[4 * nm:]
        x, y, c, chips = _place()
        s_me = 2 * x + y
        for m, (_, kind, shape) in enumerate(mats):
            pr, pc = _piece(kind, shape)

            def scoped(a, b, m=m, kind=kind, shape=shape, pr=pr):
                loads = [pltpu.make_async_copy(_piece_at(gin[m], kind, shape, s, c), a.at[s], lsem.at[s])
                         for s in range(N_CHIP)]
                loads.append(pltpu.make_async_copy(hin[m], b, lsem.at[N_CHIP]))
                for cp in loads:
                    cp.start()
                for cp in loads:
                    cp.wait()
                step = _add_rows(pr)
                for s in range(N_CHIP):
                    def add(i, _, s=s):
                        r = pl.ds(pl.multiple_of(i * step, 16), step)
                        a[s, r, :] = (a[s, r, :].astype(F32) + b[s, r, :].astype(F32)).astype(BF16)
                        return 0

                    lax.fori_loop(0, pr // step, add, 0)
                waits = []
                for j, (cx, cy) in enumerate(chips):
                    cp = pltpu.make_async_remote_copy(src_ref=a.at[2 * cx + cy], dst_ref=got[m].at[j], send_sem=send.at[m, j],
                                                      recv_sem=recv.at[m, j], device_id=(cx, cy, c), device_id_type=MESH)
                    cp.start()
                    waits.append(cp.wait_send)
                cp = pltpu.make_async_copy(a.at[s_me], own[m], lsem.at[N_CHIP + 1])
                cp.start()
                waits.append(cp.wait)
                for w in waits:
                    w()

            pl.run_scoped(scoped, pltpu.VMEM((N_CHIP, pr, pc), BF16), pltpu.VMEM((N_CHIP, pr, pc), BF16))
        for m in range(nm):
            for j, (cx, cy) in enumerate(chips):
                pltpu.make_async_remote_copy(src_ref=got[m].at[j], dst_ref=got[m].at[j], send_sem=send.at[m, j],
                                             recv_sem=recv.at[m, j], device_id=(cx, cy, c), device_id_type=MESH).wait_recv()

    sem = lambda *s: pltpu.SemaphoreType.DMA(s)
    pieces = [_piece(kind, shape) for _, kind, shape in mats]
    return pl.pallas_call(
        body, name="rs2_chips",
        out_shape=[jax.ShapeDtypeStruct(p, BF16) for p in pieces] + [jax.ShapeDtypeStruct((3,) + p, BF16) for p in pieces],
        in_specs=[ANY] * (2 * nm), out_specs=[ANY] * (2 * nm),
        scratch_shapes=[sem(nm, 3), sem(nm, 3), sem(N_CHIP + 2)],
        compiler_params=_params(),
    )(*grads, *halves)


def _rs3_finish(own, got):
    nm = len(BIG)

    def body(*refs):
        oin, gin = refs[:nm], refs[nm:2 * nm]
        outs = refs[2 * nm:3 * nm]
        send, recv, lsem = refs[3 * nm:]
        x, y, c, _ = _place()
        for m, (_, kind, shape) in enumerate(BIG):
            pr, pc = _piece(kind, shape)
            ng = got[m].shape[0]

            def scoped(a, g, f, m=m, pr=pr, ng=ng, kind=kind, shape=shape):
                mine = _piece_at(oin[m], kind, shape, 2 * x + y, c) if m in LATE else oin[m]
                loads = [pltpu.make_async_copy(mine, a, lsem.at[0]), pltpu.make_async_copy(gin[m], g, lsem.at[1])]
                for cp in loads:
                    cp.start()
                for cp in loads:
                    cp.wait()
                step = _add_rows(pr)

                def add(i, _):
                    r = pl.ds(pl.multiple_of(i * step, 16), step)
                    acc = a[r, :].astype(F32)
                    for q in range(ng):
                        acc = acc + g[q, r, :].astype(F32)
                    f[r, :] = acc
                    return 0

                lax.fori_loop(0, pr // step, add, 0)
                dst = outs[m].at[pl.ds(pl.multiple_of(c * pr, 8), pr), :]
                local = pltpu.make_async_copy(f, dst, lsem.at[2])
                local.start()
                cp = pltpu.make_async_remote_copy(src_ref=f, dst_ref=dst, send_sem=send.at[m], recv_sem=recv.at[m],
                                                  device_id=(x, y, 1 - c), device_id_type=MESH)
                cp.start()
                cp.wait_send()
                local.wait()

            pl.run_scoped(scoped, pltpu.VMEM((pr, pc), BF16), pltpu.VMEM((ng, pr, pc), BF16), pltpu.VMEM((pr, pc), F32))
        for m, (_, kind, shape) in enumerate(BIG):
            pr, pc = _piece(kind, shape)
            dst = outs[m].at[pl.ds(pl.multiple_of((1 - c) * pr, 8), pr), :]
            pltpu.make_async_remote_copy(src_ref=dst, dst_ref=dst, send_sem=send.at[m], recv_sem=recv.at[m],
                                         device_id=(x, y, 1 - c), device_id_type=MESH).wait_recv()

    sem = lambda *s: pltpu.SemaphoreType.DMA(s)
    pieces = [_piece(kind, shape) for _, kind, shape in BIG]
    return pl.pallas_call(
        body, name="rs3_finish",
        out_shape=[jax.ShapeDtypeStruct((2 * pr, pc), F32) for pr, pc in pieces],
        in_specs=[ANY] * (2 * nm), out_specs=[ANY] * nm,
        scratch_shapes=[sem(nm), sem(nm), sem(3)],
        compiler_params=_params(),
    )(*own, *got)


def _cond_fwd(c_all, w_shard, b_shard):
    def body(c_ref, w_ref, b_ref, act_ref, mod_ref):
        cv = c_ref[...]
        act = cv * _sig(cv)
        act_ref[...] = act
        mod_ref[...] = _dot(act.astype(BF16), w_ref[...].astype(BF16)) + b_ref[...]

    return pl.pallas_call(
        body, name="cond_fwd",
        out_shape=[jax.ShapeDtypeStruct(c_all.shape, F32), jax.ShapeDtypeStruct((c_all.shape[0], w_shard.shape[1]), F32)],
        compiler_params=_params(),
    )(c_all, w_shard, b_shard)


def _cond_bwd(act_t, dmod_shard):
    k, n = act_t.shape[0], dmod_shard.shape[1]

    def body(a_ref, d_ref, o_ref):
        acc = a_ref[:, 0:1] * d_ref[0:1, :]
        for e in range(1, N_DEV):
            acc += a_ref[:, e:e + 1] * d_ref[e:e + 1, :]
        o_ref[...] = acc

    tr = 256
    return pl.pallas_call(
        body, name="cond_bwd", grid=(k // tr,),
        in_specs=[pl.BlockSpec((tr, N_DEV), lambda i: (i, 0)), _full(dmod_shard)],
        out_specs=pl.BlockSpec((tr, n), lambda i: (i, 0)),
        out_shape=jax.ShapeDtypeStruct((k, n), F32),
        compiler_params=_params(("arbitrary",)),
    )(act_t, dmod_shard)


def _sum_blocks(allp):
    def body(a_ref, o_ref):
        acc = a_ref[0:PACK_ROWS, :]
        for d in range(1, N_DEV):
            acc += a_ref[d * PACK_ROWS:(d + 1) * PACK_ROWS, :]
        o_ref[...] = acc

    return pl.pallas_call(
        body, name="sum_small", out_shape=jax.ShapeDtypeStruct((PACK_ROWS, PACK_COLS), F32), compiler_params=_params(),
    )(allp)


def _adamw(name, w, g, m, v):
    r, cc = w.shape
    tr = r
    for cand in (256, 128, 64, 32, 16, 8):
        if r % cand == 0:
            tr = cand
            break
    bc1 = 1.0 - ADAM_B1 ** ADAM_STEP
    bc2 = 1.0 - ADAM_B2 ** ADAM_STEP

    def body(w_ref, g_ref, m_ref, v_ref, d_ref, nm_ref, nv_ref):
        gv = g_ref[...]
        nm = ADAM_B1 * m_ref[...] + (1.0 - ADAM_B1) * gv
        nv = ADAM_B2 * v_ref[...] + (1.0 - ADAM_B2) * (gv * gv)
        nm_ref[...] = nm
        nv_ref[...] = nv
        d_ref[...] = -ADAM_LR * ((nm / bc1) / (jnp.sqrt(nv / bc2) + ADAM_EPS) + ADAM_WD * w_ref[...])

    spec = pl.BlockSpec((tr, cc), lambda i: (i, 0))
    return pl.pallas_call(
        body, name=name, grid=(r // tr,), in_specs=[spec] * 4, out_specs=[spec] * 3,
        out_shape=[jax.ShapeDtypeStruct((r, cc), F32)] * 3, compiler_params=_params(("arbitrary",)),
    )(w, g, m, v)


def _pack(fields, layout):
    parts = [fields[name].reshape(-1).astype(F32) if name in fields else jnp.zeros((n,), F32) for name, n in layout]
    used = sum(n for _, n in layout)
    parts.append(jnp.zeros((PACK_ROWS * PACK_COLS - used,), F32))
    return jnp.concatenate(parts).reshape(PACK_ROWS, PACK_COLS)


def _unpack(flat, layout):
    flat = flat.reshape(-1)
    out, o = {}, 0
    for name, n in layout:
        out[name] = flat[o:o + n]
        o += n
    return out


def kernel(x, c, w_cond, b_cond, w_in, b_in, ssm_lambda_re, ssm_lambda_im, ssm_log_dt, ssm_b_re, ssm_b_im, ssm_c_re, ssm_c_im, ssm_d, ssm_glu_w_a, ssm_glu_w_b, cv_dw_w, cv_dw_b, cv_ln_g, cv_ln_b, cv_w_pw, w_out, ln1_g, ln1_b, ffn_w_up, ffn_dw_w, ffn_dw_b, ffn_w_down, ln2_g, ln2_b, loss_target, m_w_cond, m_b_cond, m_w_in, m_b_in, m_ssm_lambda_re, m_ssm_lambda_im, m_ssm_log_dt, m_ssm_b_re, m_ssm_b_im, m_ssm_c_re, m_ssm_c_im, m_ssm_d, m_ssm_glu_w_a, m_ssm_glu_w_b, m_cv_dw_w, m_cv_dw_b, m_cv_ln_g, m_cv_ln_b, m_cv_w_pw, m_w_out, m_ln1_g, m_ln1_b, m_ffn_w_up, m_ffn_dw_w, m_ffn_dw_b, m_ffn_w_down, m_ln2_g, m_ln2_b, v_w_cond, v_b_cond, v_w_in, v_b_in, v_ssm_lambda_re, v_ssm_lambda_im, v_ssm_log_dt, v_ssm_b_re, v_ssm_b_im, v_ssm_c_re, v_ssm_c_im, v_ssm_d, v_ssm_glu_w_a, v_ssm_glu_w_b, v_cv_dw_w, v_cv_dw_b, v_cv_ln_g, v_cv_ln_b, v_cv_w_pw, v_w_out, v_ln1_g, v_ln1_b, v_ffn_w_up, v_ffn_dw_w, v_ffn_dw_b, v_ffn_w_down, v_ln2_g, v_ln2_b):
    given = locals()
    a = {n: given[n] for n in INPUTS}
    xi, yi, ci = lax.axis_index("x"), lax.axis_index("y"), lax.axis_index("c")
    s_me = 2 * xi + yi
    e_me = 4 * xi + 2 * yi + ci

    first = jnp.concatenate([
        jnp.concatenate([a["c"], jnp.zeros((7, D_MODEL), F32)], axis=0),
        jnp.concatenate([a["cv_dw_w"].reshape(-1), a["ffn_dw_w"].reshape(-1)]).reshape(8, D_MODEL)], axis=0)
    first_all = _allgather("gather_c", first).reshape(N_DEV, 16, D_MODEL)
    c_all = first_all[:, 0, :]
    dw_all = first_all[0::2, 8:, :].reshape(N_CHIP, 8 * D_MODEL)
    n_cv = CONV_KERNEL * CONV_WIDTH // N_CHIP
    cv_dw_full = dw_all[:, :n_cv].reshape(N_CHIP, CONV_KERNEL, CONV_WIDTH // N_CHIP).transpose(1, 0, 2) \
        .reshape(CONV_KERNEL, CONV_WIDTH)
    ffn_dw_full = dw_all[:, n_cv:].reshape(N_CHIP, FFN_KERNEL, 2 * FFN_HIDDEN // N_CHIP).transpose(1, 0, 2) \
        .reshape(FFN_KERNEL, 2 * FFN_HIDDEN)
    ncols = N_COND * D_MODEL // N_CHIP
    b_cond_shard = lax.dynamic_slice(a["b_cond"], (0, s_me * ncols), (1, ncols))
    c_act_all, modp = _cond_fwd(c_all, a["w_cond"][0], b_cond_shard)
    modp_all = _allgather("gather_mod", modp).reshape(N_DEV, N_DEV, ncols)[0::2]
    mod_e = lax.dynamic_index_in_dim(modp_all, e_me, axis=1, keepdims=False).reshape(N_COND, D_MODEL)
    modv = jnp.concatenate([mod_e, jnp.zeros((2, D_MODEL), F32)], axis=0)

    full, late_sh = _gather_weights([a[n][0] for n, _, _ in BIG])
    wb = dict(zip([n for n, _, _ in BIG], full))
    sp = {n: a[n][0] for n in ("b_in", "ssm_lambda_re", "ssm_lambda_im", "ssm_log_dt", "ssm_b_re", "ssm_b_im",
                               "ssm_c_re", "ssm_c_im", "ssm_d", "cv_dw_b", "cv_ln_g", "cv_ln_b", "ln1_g", "ln1_b",
                               "ffn_dw_b", "ln2_g", "ln2_b")}
    sp["cv_dw_w"] = cv_dw_full
    sp["ffn_dw_w"] = ffn_dw_full
    gx, dbig, (late_dw, late_got), small = _local_step(a["x"][0], a["loss_target"][0], modv, wb, late_sh, sp)

    small["c_act"] = lax.dynamic_index_in_dim(c_act_all, e_me, axis=0, keepdims=False)
    packed_all = _allgather("gather_small", _pack(small, PACK))
    tot = _unpack(_sum_blocks(packed_all), PACK)
    rows = packed_all.reshape(N_DEV, PACK_ROWS * PACK_COLS)
    dmod_all = rows[:, 0:N_COND * D_MODEL]
    act_all = rows[:, N_COND * D_MODEL:(N_COND + 1) * D_MODEL]
    g_w_cond = _cond_bwd(act_all.T, lax.dynamic_slice(dmod_all, (0, s_me * ncols), (N_DEV, ncols)))

    glist = [dbig[BIG[m][0]] for m in EARLY]
    halves = _rs1_sibling(glist)
    r2 = _rs2_chips(glist, halves)
    gsh = _rs3_finish(list(r2[:len(EARLY)]) + late_dw, list(r2[len(EARLY):]) + late_got)

    grads = {"w_cond": g_w_cond[None], "b_cond": tot["dmod"].reshape(1, -1)}
    for (n, kind, shape), g in zip(BIG, gsh):
        grads[n] = g.reshape(a[n].shape)
    for n in ("b_in", "ssm_lambda_re", "ssm_lambda_im", "ssm_log_dt", "ssm_b_re", "ssm_b_im", "ssm_c_re", "ssm_c_im",
              "ssm_d", "cv_dw_b", "cv_ln_g", "cv_ln_b", "ln1_g", "ln1_b", "ffn_dw_b", "ln2_g", "ln2_b"):
        grads[n] = tot[n].reshape(a[n].shape)
    wcv = CONV_WIDTH // N_CHIP
    grads["cv_dw_w"] = lax.dynamic_slice(tot["cv_dw_w"].reshape(CONV_KERNEL, CONV_WIDTH), (0, s_me * wcv),
                                         (CONV_KERNEL, wcv)).reshape(a["cv_dw_w"].shape)
    wff = 2 * FFN_HIDDEN // N_CHIP
    grads["ffn_dw_w"] = lax.dynamic_slice(tot["ffn_dw_w"].reshape(FFN_KERNEL, 2 * FFN_HIDDEN), (0, s_me * wff),
                                          (FFN_KERNEL, wff)).reshape(a["ffn_dw_w"].shape)

    delta, new_m, new_v = {}, {}, {}
    for n in ["w_cond"] + [n for n, _, _ in BIG]:
        d, nm_, nv_ = _adamw("adamw_" + n, a[n][0], grads[n][0], a["m_" + n][0], a["v_" + n][0])
        delta[n], new_m[n], new_v[n] = d[None], nm_[None], nv_[None]
    upd = [n for n, _ in SMALL_UPD]
    d, nm_, nv_ = _adamw("adamw_small", _pack({n: a[n] for n in upd}, SMALL_UPD), _pack({n: grads[n] for n in upd}, SMALL_UPD),
                         _pack({n: a["m_" + n] for n in upd}, SMALL_UPD), _pack({n: a["v_" + n] for n in upd}, SMALL_UPD))
    for dst, flat in ((delta, d), (new_m, nm_), (new_v, nv_)):
        for n, val in _unpack(flat, SMALL_UPD).items():
            dst[n] = val.reshape(a[n].shape)

    loss = tot["loss"].reshape(())
    return (loss, gx[None], *[grads[n] for n in WEIGHTS], *[delta[n] for n in WEIGHTS],
            *[new_m[n] for n in WEIGHTS], *[new_v[n] for n in WEIGHTS])
```

```python
import functools
import math

import jax
import jax.numpy as jnp
from jax import lax
from jax.experimental import pallas as pl
from jax.experimental.pallas import tpu as pltpu

F32 = jnp.float32
BF16 = jnp.bfloat16

D_MODEL = 1024
SSM_WIDTH = 512
SSM_GROUP = 16
SSM_GROUPS = 32
SSM_STATE = 64
CONV_WIDTH = 512
CONV_KERNEL = 31
FFN_HIDDEN = 2816
FFN_KERNEL = 3
IN_PROJ_WIDTH = 3584
N_COND = 6
ALPHA = 2.0 ** 0.25
LN_EPS = 1e-5
ADAM_LR, ADAM_B1, ADAM_B2, ADAM_EPS, ADAM_WD, ADAM_STEP = 0.001, 0.9, 0.999, 1e-08, 0.01, 10

N_DEV = 8
N_CHIP = 4
LANES = 128
SSM_CHUNK = 16
LANE_GROUPS = LANES // SSM_GROUP
N_LANE_BLOCKS = SSM_WIDTH // LANES
STATE_COLS = LANE_GROUPS * SSM_STATE
CHUNK_COLS = SSM_CHUNK * LANES
CONV_HALO = 32
VMEM_LIMIT = 56 * 1024 * 1024
MESH = pl.DeviceIdType.MESH

BIG = (
    ("w_in", "col", (D_MODEL, IN_PROJ_WIDTH)),
    ("ssm_glu_w_a", "col", (SSM_WIDTH, D_MODEL)),
    ("ssm_glu_w_b", "col", (SSM_WIDTH, D_MODEL)),
    ("cv_w_pw", "col", (CONV_WIDTH, D_MODEL)),
    ("w_out", "row", (D_MODEL, D_MODEL)),
    ("ffn_w_up", "col", (D_MODEL, 2 * FFN_HIDDEN)),
    ("ffn_w_down", "row", (FFN_HIDDEN, D_MODEL)),
)

EARLY = (0,)
MID = (1, 2, 3, 4)
LATE = (5, 6)
DIRECT = MID + LATE

WEIGHTS = ['w_cond', 'b_cond', 'w_in', 'b_in', 'ssm_lambda_re', 'ssm_lambda_im', 'ssm_log_dt', 'ssm_b_re', 'ssm_b_im',
           'ssm_c_re', 'ssm_c_im', 'ssm_d', 'ssm_glu_w_a', 'ssm_glu_w_b', 'cv_dw_w', 'cv_dw_b', 'cv_ln_g', 'cv_ln_b',
           'cv_w_pw', 'w_out', 'ln1_g', 'ln1_b', 'ffn_w_up', 'ffn_dw_w', 'ffn_dw_b', 'ffn_w_down', 'ln2_g', 'ln2_b']
INPUTS = ['x', 'c'] + WEIGHTS + ['loss_target'] + ['m_' + n for n in WEIGHTS] + ['v_' + n for n in WEIGHTS]

PACK = (
    ("dmod", N_COND * D_MODEL), ("c_act", D_MODEL), ("b_in", IN_PROJ_WIDTH),
    ("ssm_lambda_re", SSM_GROUPS * SSM_STATE), ("ssm_lambda_im", SSM_GROUPS * SSM_STATE), ("ssm_log_dt", SSM_GROUPS),
    ("ssm_b_re", SSM_GROUPS * SSM_STATE * SSM_GROUP), ("ssm_b_im", SSM_GROUPS * SSM_STATE * SSM_GROUP),
    ("ssm_c_re", SSM_GROUPS * SSM_STATE * SSM_GROUP), ("ssm_c_im", SSM_GROUPS * SSM_STATE * SSM_GROUP),
    ("ssm_d", SSM_GROUPS * SSM_GROUP), ("cv_dw_w", CONV_KERNEL * CONV_WIDTH), ("cv_dw_b", CONV_WIDTH),
    ("cv_ln_g", CONV_WIDTH), ("cv_ln_b", CONV_WIDTH), ("ln1_g", D_MODEL), ("ln1_b", D_MODEL),
    ("ffn_dw_w", FFN_KERNEL * 2 * FFN_HIDDEN), ("ffn_dw_b", 2 * FFN_HIDDEN), ("ln2_g", D_MODEL), ("ln2_b", D_MODEL),
    ("loss", 1),
)
PACK_COLS = 1024
PACK_ROWS = 192
assert sum(n for _, n in PACK) <= PACK_ROWS * PACK_COLS

SMALL_UPD = (
    ("b_cond", N_COND * D_MODEL), ("b_in", IN_PROJ_WIDTH),
    ("ssm_lambda_re", SSM_GROUPS * SSM_STATE), ("ssm_lambda_im", SSM_GROUPS * SSM_STATE), ("ssm_log_dt", SSM_GROUPS),
    ("ssm_b_re", SSM_GROUPS * SSM_STATE * SSM_GROUP), ("ssm_b_im", SSM_GROUPS * SSM_STATE * SSM_GROUP),
    ("ssm_c_re", SSM_GROUPS * SSM_STATE * SSM_GROUP), ("ssm_c_im", SSM_GROUPS * SSM_STATE * SSM_GROUP),
    ("ssm_d", SSM_GROUPS * SSM_GROUP), ("cv_dw_w", CONV_KERNEL * CONV_WIDTH // N_CHIP), ("cv_dw_b", CONV_WIDTH),
    ("cv_ln_g", CONV_WIDTH), ("cv_ln_b", CONV_WIDTH), ("ln1_g", D_MODEL), ("ln1_b", D_MODEL),
    ("ffn_dw_w", FFN_KERNEL * 2 * FFN_HIDDEN // N_CHIP), ("ffn_dw_b", 2 * FFN_HIDDEN), ("ln2_g", D_MODEL),
    ("ln2_b", D_MODEL),
)
assert sum(n for _, n in SMALL_UPD) <= PACK_ROWS * PACK_COLS


def _params(sem=None, **kw):
    return pltpu.CompilerParams(dimension_semantics=sem, vmem_limit_bytes=VMEM_LIMIT, **kw)


def _ln_stats(x):
    mu = jnp.mean(x, axis=-1, keepdims=True)
    xc = x - mu
    var = jnp.mean(xc * xc, axis=-1, keepdims=True)
    rstd = lax.rsqrt(var + LN_EPS)
    return xc * rstd, rstd


def _ln_bwd(dxhat, xhat, rstd):
    m1 = jnp.mean(dxhat, axis=-1, keepdims=True)
    m2 = jnp.mean(dxhat * xhat, axis=-1, keepdims=True)
    return rstd * (dxhat - m1 - xhat * m2)


def _sig(x):
    return 1.0 / (1.0 + jnp.exp(-x))


def _gelu(x):
    return 0.5 * x * (1.0 + lax.erf(x * (1.0 / math.sqrt(2.0))))


def _dgelu(x):
    return 0.5 * (1.0 + lax.erf(x * (1.0 / math.sqrt(2.0)))) + x * jnp.exp(-0.5 * x * x) * (1.0 / math.sqrt(2.0 * math.pi))


def _gelu_and_grad(x):
    er = lax.erf(x * (1.0 / math.sqrt(2.0)))
    cdf = 0.5 * (1.0 + er)
    return x * cdf, cdf + x * jnp.exp(-0.5 * x * x) * (1.0 / math.sqrt(2.0 * math.pi))


def _colsum(a):
    return jnp.sum(a, axis=0, keepdims=True)


def _fill_rotations(buf, rot, rows):
    for r in range(1, 8):
        rot[r - 1] = buf[pl.ds(r, rows), :]


def _rows_at(buf, rot, offset, tb):
    q, r = divmod(offset, 8)
    if r == 0:
        return buf[pl.ds(8 * q, tb), :]
    return rot[r - 1, pl.ds(8 * q, tb), :]


def _dot(a, b):
    return jnp.dot(a, b, preferred_element_type=F32)


def _dot_nt(a, b):
    return lax.dot_general(a, b, (((1,), (1,)), ((), ())), preferred_element_type=F32)


def _dot_tn(a, b):
    return lax.dot_general(a, b, (((0,), (0,)), ((), ())), preferred_element_type=F32)


def _load_once(src, dst, sem):
    cp = pltpu.make_async_copy(src, dst, sem)
    cp.start()
    cp.wait()


def _full(a):
    nd = a.ndim
    return pl.BlockSpec(a.shape, lambda *_: (0,) * nd)


ANY = pl.BlockSpec(memory_space=pl.ANY)


def _place():
    x, y, c = lax.axis_index("x"), lax.axis_index("y"), lax.axis_index("c")
    chips = [(1 - x, y), (x, 1 - y), (1 - x, 1 - y)]
    return x, y, c, chips


def _piece(kind, shape):
    r, cc = shape
    return (r // 2, cc // N_CHIP) if kind == "col" else (r // (2 * N_CHIP), cc)


def _piece_at(ref, kind, shape, s, k):
    pr, pc = _piece(kind, shape)
    if kind == "col":
        return ref.at[pl.ds(k * pr, pr), pl.ds(pl.multiple_of(s * pc, LANES), pc)]
    return ref.at[pl.ds(pl.multiple_of((2 * s + k) * pr, 16), pr), :]


def _gather_start(idx, sh, full, send, recv):
    x, y, c, chips = _place()
    for i, m in enumerate(idx):
        _, kind, shape = BIG[m]
        pr, _ = _piece(kind, shape)
        for j, chip in enumerate(chips):
            pltpu.make_async_remote_copy(
                src_ref=sh[i].at[pl.ds(pl.multiple_of(c * pr, 16), pr), :], dst_ref=_piece_at(full[i], kind, shape, 2 * x + y, c),
                send_sem=send.at[i, j], recv_sem=recv.at[i, j], device_id=(*chip, c), device_id_type=MESH).start()


def _gather_finish(idx, sh, full, send, recv, fsend, frecv):
    x, y, c, chips = _place()
    sibling = (x, y, 1 - c)
    waits = []
    for i, m in enumerate(idx):
        _, kind, shape = BIG[m]
        pr, _ = _piece(kind, shape)
        for j, (cx, cy) in enumerate(chips):
            got = _piece_at(full[i], kind, shape, 2 * cx + cy, c)
            first = pltpu.make_async_remote_copy(
                src_ref=sh[i].at[pl.ds(pl.multiple_of(c * pr, 16), pr), :], dst_ref=got, send_sem=send.at[i, j],
                recv_sem=recv.at[i, j], device_id=(cx, cy, c), device_id_type=MESH)
            first.wait_recv()
            fwd = pltpu.make_async_remote_copy(src_ref=got, dst_ref=got, send_sem=fsend.at[i, j], recv_sem=frecv.at[i, j],
                                               device_id=sibling, device_id_type=MESH)
            fwd.start()
            waits += [first.wait_send, fwd.wait_send]
    for i, m in enumerate(idx):
        _, kind, shape = BIG[m]
        for j, (cx, cy) in enumerate(chips):
            got = _piece_at(full[i], kind, shape, 2 * cx + cy, 1 - c)
            pltpu.make_async_remote_copy(src_ref=got, dst_ref=got, send_sem=fsend.at[i, j], recv_sem=frecv.at[i, j],
                                         device_id=sibling, device_id_type=MESH).wait_recv()
    for w in waits:
        w()


def _scatter(idx, dw, got, send, recv):
    x, y, c, _ = _place()
    cps = []
    for i, m in enumerate(idx):
        _, kind, shape = BIG[m]
        for r in range(1, N_DEV):
            tx, ty, tc = (1 - x if r & 4 else x), (1 - y if r & 2 else y), (1 - c if r & 1 else c)
            cps.append(pltpu.make_async_remote_copy(
                src_ref=_piece_at(dw[i], kind, shape, 2 * tx + ty, tc), dst_ref=got[i].at[r - 1],
                send_sem=send.at[i, r - 1], recv_sem=recv.at[i, r - 1], device_id=(tx, ty, tc), device_id_type=MESH))
    return cps


def _f1_inproj(x, modv, b_in, w_in, mid_sh, mid_full, tb):
    t = x.shape[0]
    nt = t // tb
    nl = len(MID)
    chunks = [(j * 512, 512) for j in range(IN_PROJ_WIDTH // 512)]

    def body(x_ref, modv_ref, b_ref, w_hbm, *rest):
        sh, full = rest[:nl], rest[2 * nl:3 * nl]
        u4_ref, prest_ref, h_ref, w_v, sem, send, recv, fsend, frecv = rest[3 * nl:]

        @pl.when(pl.program_id(0) == 0)
        def _():
            _gather_start(MID, sh, full, send, recv)
            _load_once(w_hbm, w_v, sem)

        xn, _ = _ln_stats(x_ref[...])
        h = (xn * (1.0 + modv_ref[1:2, :]) + modv_ref[0:1, :]).astype(BF16)
        h_ref[...] = h
        for c0, cw in chunks:
            p = _dot(h, w_v[:, c0:c0 + cw]) + b_ref[:, c0:c0 + cw]
            if c0 == 0:
                for b in range(N_LANE_BLOCKS):
                    u4_ref[b] = p[:, b * LANES:(b + 1) * LANES]
            else:
                prest_ref[:, c0 - SSM_WIDTH:c0 - SSM_WIDTH + cw] = p

        @pl.when(pl.program_id(0) == nt - 1)
        def _():
            _gather_finish(MID, sh, full, send, recv, fsend, frecv)

    gsem = pltpu.SemaphoreType.DMA((nl, 3))
    return pl.pallas_call(
        body, name="f1_inproj", grid=(nt,),
        in_specs=[pl.BlockSpec((tb, D_MODEL), lambda i: (i, 0)), _full(modv), _full(b_in), ANY] + [ANY] * (2 * nl),
        out_specs=[ANY] * nl + [pl.BlockSpec((N_LANE_BLOCKS, tb, LANES), lambda i: (0, i, 0)),
                                pl.BlockSpec((tb, IN_PROJ_WIDTH - SSM_WIDTH), lambda i: (i, 0)),
                                pl.BlockSpec((tb, D_MODEL), lambda i: (i, 0))],
        input_output_aliases={4 + nl + k: k for k in range(nl)},
        out_shape=[jax.ShapeDtypeStruct(f.shape, f.dtype) for f in mid_full]
        + [jax.ShapeDtypeStruct((N_LANE_BLOCKS, t, LANES), F32),
                   jax.ShapeDtypeStruct((t, IN_PROJ_WIDTH - SSM_WIDTH), F32),
                   jax.ShapeDtypeStruct((t, D_MODEL), BF16)],
        scratch_shapes=[pltpu.VMEM(w_in.shape, BF16), pltpu.SemaphoreType.DMA, gsem, gsem, gsem, gsem],
        compiler_params=_params(("arbitrary",)),
    )(x, modv, b_in, w_in, *mid_sh, *mid_full)


def _s5_build(lam_re, lam_im, log_dt, b_re, b_im, c_re, c_im, d):
    el, g, n, p, nb = SSM_CHUNK, SSM_GROUPS, SSM_STATE, SSM_GROUP, N_LANE_BLOCKS
    lr = jnp.minimum(lam_re, -1e-4)
    li = lam_im
    dt = jnp.exp(log_dt)[:, None]
    mag = jnp.exp(lr * dt)
    ang = li * dt
    lbr, lbi = mag * jnp.cos(ang), mag * jnp.sin(ang)
    num_r, num_i = lbr - 1.0, lbi
    den = lr * lr + li * li
    coef_r = (num_r * lr + num_i * li) / den
    coef_i = (num_i * lr - num_r * li) / den
    bbar_r = coef_r[..., None] * b_re - coef_i[..., None] * b_im
    bbar_i = coef_r[..., None] * b_im + coef_i[..., None] * b_re
    k = jnp.arange(el + 1, dtype=F32)[:, None, None]
    pmag = jnp.exp(k * (lr * dt)[None])
    pr, pi = pmag * jnp.cos(k * ang[None]), pmag * jnp.sin(k * ang[None])
    car = c_re[None] * pr[:, :, None, :] - c_im[None] * pi[:, :, None, :]
    cai = c_re[None] * pi[:, :, None, :] + c_im[None] * pr[:, :, None, :]
    bt_r = bbar_r.transpose(0, 2, 1)[None]
    bt_i = bbar_i.transpose(0, 2, 1)[None]
    kern = jnp.sum(car[:el, :, None, :, :] * bt_r[:, :, :, None, :] - cai[:el, :, None, :, :] * bt_i[:, :, :, None, :],
                   axis=-1)
    kern = kern.at[0].add(jnp.eye(p, dtype=F32)[None] * d[:, None, :])
    kc = kern.reshape(el, g * p, p)
    rev = el - 1 - jnp.arange(el)
    qr, qi = pr[rev][:, :, None, :], pi[rev][:, :, None, :]
    sw_r = (qr * bt_r - qi * bt_i).reshape(el, g * p, n)
    sw_i = (qr * bt_i + qi * bt_r).reshape(el, g * p, n)
    sg_r = car[1:].reshape(el, g * p, n)
    sg_i = (-cai[1:]).reshape(el, g * p, n)
    a = jnp.stack([pr[el].reshape(nb, LANE_GROUPS * n), pi[el].reshape(nb, LANE_GROUPS * n)], axis=1)
    return kc, sw_r, sw_i, sg_r, sg_i, a


def _expand(src, reps):
    rows, w = src.shape
    cols = reps * w
    r = lax.broadcasted_iota(jnp.int32, (w, cols), 0)
    c = lax.broadcasted_iota(jnp.int32, (w, cols), 1)
    rep = (r == (c & (w - 1))).astype(BF16)
    out = _dot(src.astype(BF16), rep)
    rg = lax.broadcasted_iota(jnp.int32, (rows, cols), 0) // SSM_GROUP
    cg = lax.broadcasted_iota(jnp.int32, (rows, cols), 1) // w
    return jnp.where(rg == cg, out, 0.0).astype(BF16)


def _fold(x, w):
    rows, cols = x.shape
    rg = lax.broadcasted_iota(jnp.int32, (rows, cols), 0) // SSM_GROUP
    cg = lax.broadcasted_iota(jnp.int32, (rows, cols), 1) // w
    x = jnp.where(rg == cg, x, 0.0)
    while cols > LANES:
        x = x[:, :cols // 2] + x[:, cols // 2:]
        cols //= 2
    s = LANES // 2
    while s >= w:
        x = x + pltpu.roll(x, s, axis=1)
        s //= 2
    return x[:, :w]


def _build_maps(s_ref, dst):
    for j in range(SSM_CHUNK):
        dst[j * LANES:(j + 1) * LANES, :] = _expand(s_ref[j], LANE_GROUPS)


def _build_toeplitz(kc_ref, dst):
    dst[...] = jnp.zeros_like(dst)
    for d in range(SSM_CHUNK):
        blk = _expand(kc_ref[d], LANE_GROUPS)
        for ji in range(SSM_CHUNK - d):
            jo = ji + d
            dst[ji * LANES:(ji + 1) * LANES, jo * LANES:(jo + 1) * LANES] = blk


def _cblk(w):
    return pl.BlockSpec((SSM_CHUNK, LANES, w), lambda b: (0, b, 0))


def _tblk(t):
    return pl.BlockSpec((1, t, LANES), lambda b: (b, 0, 0))


def _load_chunks(ref, nc):
    return jnp.concatenate([ref[0, pl.ds(j, nc, stride=SSM_CHUNK), :] for j in range(SSM_CHUNK)], axis=-1).astype(BF16)


def _store_chunks(ref, val, nc):
    for j in range(SSM_CHUNK):
        ref[0, pl.ds(j, nc, stride=SSM_CHUNK), :] = val[:, j * LANES:(j + 1) * LANES]


def _s5a_state(u4, sw_r, sw_i, a8):
    nb, t, _ = u4.shape
    nc = t // SSM_CHUNK
    sc = STATE_COLS

    def body(u_ref, swr_ref, swi_ref, a_ref, hr_ref, hi_ref, w_s, xr_s, xi_s):
        u = _load_chunks(u_ref, nc)
        _build_maps(swr_ref, w_s)
        xr_s[...] = _dot(u, w_s[...])
        _build_maps(swi_ref, w_s)
        xi_s[...] = _dot(u, w_s[...])
        ar = a_ref[0, 0:1, :]
        ai = a_ref[0, 1:2, :]

        def step(c, carry):
            hr, hi = carry
            hr_ref[0, pl.ds(c, 1), :] = hr
            hi_ref[0, pl.ds(c, 1), :] = hi
            xr = xr_s[pl.ds(c, 1), :]
            xi = xi_s[pl.ds(c, 1), :]
            return ar * hr - ai * hi + xr, ar * hi + ai * hr + xi

        z = jnp.zeros((1, sc), F32)
        lax.fori_loop(0, nc, step, (z, z))

    return pl.pallas_call(
        body, name="s5a_state", grid=(nb,),
        in_specs=[_tblk(t), _cblk(SSM_STATE), _cblk(SSM_STATE),
                  pl.BlockSpec((1, 8, sc), lambda b: (b, 0, 0))],
        out_specs=[pl.BlockSpec((1, nc, sc), lambda b: (b, 0, 0))] * 2,
        out_shape=[jax.ShapeDtypeStruct((nb, nc, sc), F32)] * 2,
        scratch_shapes=[pltpu.VMEM((CHUNK_COLS, sc), BF16), pltpu.VMEM((nc, sc), F32), pltpu.VMEM((nc, sc), F32)],
        compiler_params=_params(("arbitrary",)),
    )(u4, sw_r, sw_i, a8)


def _s5b_out(u4, kc, sg_r, sg_i, hr, hi):
    nb, t, _ = u4.shape
    nc = t // SSM_CHUNK
    sc = STATE_COLS
    cw = 512

    def body(u_ref, kc_ref, sgr_ref, sgi_ref, hr_ref, hi_ref, y_ref, tm_s, gr_s, gi_s):
        _build_toeplitz(kc_ref, tm_s)
        _build_maps(sgr_ref, gr_s)
        _build_maps(sgi_ref, gi_s)
        u = _load_chunks(u_ref, nc)
        h_r = hr_ref[0].astype(BF16)
        h_i = hi_ref[0].astype(BF16)
        for j in range(CHUNK_COLS // cw):
            cs = slice(j * cw, (j + 1) * cw)
            y = _dot(u, tm_s[:, cs]) + _dot_nt(h_r, gr_s[cs, :]) + _dot_nt(h_i, gi_s[cs, :])
            for q in range(cw // LANES):
                step = j * (cw // LANES) + q
                y_ref[0, pl.ds(step, nc, stride=SSM_CHUNK), :] = y[:, q * LANES:(q + 1) * LANES]

    return pl.pallas_call(
        body, name="s5b_out", grid=(nb,),
        in_specs=[_tblk(t), _cblk(SSM_GROUP), _cblk(SSM_STATE),
                  _cblk(SSM_STATE), pl.BlockSpec((1, nc, sc), lambda b: (b, 0, 0)),
                  pl.BlockSpec((1, nc, sc), lambda b: (b, 0, 0))],
        out_specs=_tblk(t),
        out_shape=jax.ShapeDtypeStruct((nb, t, LANES), F32),
        scratch_shapes=[pltpu.VMEM((CHUNK_COLS, CHUNK_COLS), BF16), pltpu.VMEM((CHUNK_COLS, sc), BF16),
                        pltpu.VMEM((CHUNK_COLS, sc), BF16)],
        compiler_params=_params(("arbitrary",)),
    )(u4, kc, sg_r, sg_i, hr, hi)


def _f4_mixer(ys4, prest, x, modv, cvv, cw32, w_a, w_b, w_pw, w_out, late_sh, late_full, tb):
    t = x.shape[0]
    hb = tb // CONV_HALO
    nt = t // tb
    nl = len(LATE)

    def body(ys_ref, pr_ref, halo_ref, x_ref, modv_ref, cvv_ref, cw_ref, wa_ref, wb_ref, wpw_ref, wout_ref, *rest):
        sh, full = rest[:nl], rest[2 * nl:3 * nl]
        r1_ref, ya_ref, yb_ref, ycv_ref, vc_ref, yg_ref, vs_ref, mg_ref, vbuf, vrot, send, recv, fsend, frecv = rest[3 * nl:]
        i = pl.program_id(0)

        @pl.when(i == 0)
        def _():
            _gather_start(LATE, sh, full, send, recv)

        ys = jnp.concatenate([ys_ref[b] for b in range(N_LANE_BLOCKS)], axis=-1)
        yg = _gelu(ys).astype(BF16)
        yg_ref[...] = yg
        ya = _dot(yg, wa_ref[...])
        yb = _dot(yg, wb_ref[...])
        ya_ref[...] = ya.astype(BF16)
        yb_ref[...] = yb.astype(BF16)
        yssm = ya * _sig(yb)
        hv = halo_ref[:, 0:CONV_WIDTH] * _sig(halo_ref[:, CONV_WIDTH:2 * CONV_WIDTH])
        vbuf[0:CONV_HALO, :] = jnp.where(i == 0, 0.0, hv)
        vbuf[CONV_HALO:, :] = pr_ref[:, 0:CONV_WIDTH] * _sig(pr_ref[:, CONV_WIDTH:2 * CONV_WIDTH])
        _fill_rotations(vbuf, vrot, tb + CONV_HALO - 8)
        acc = jnp.zeros((tb, CONV_WIDTH), F32)
        for k in range(CONV_KERNEL):
            acc += _rows_at(vbuf, vrot, CONV_HALO - CONV_KERNEL + 1 + k, tb) * cw_ref[k:k + 1, :]
        vc = acc + cvv_ref[0:1, :]
        vc_ref[...] = vc
        xh, _ = _ln_stats(vc)
        vl = xh * cvv_ref[1:2, :] + cvv_ref[2:3, :]
        vs = (vl * _sig(vl)).astype(BF16)
        vs_ref[...] = vs
        ycv = _dot(vs, wpw_ref[...])
        ycv_ref[...] = ycv.astype(BF16)
        gs = pr_ref[:, 2 * CONV_WIDTH:2 * CONV_WIDTH + D_MODEL]
        gc = pr_ref[:, 2 * CONV_WIDTH + D_MODEL:]
        merged = (_sig(gs) * yssm + _sig(gc) * ycv).astype(BF16)
        mg_ref[...] = merged
        ym = _dot(merged, wout_ref[...])
        r1_ref[...] = ALPHA * x_ref[...] + modv_ref[2:3, :] * ym

        @pl.when(i == nt - 1)
        def _():
            _gather_finish(LATE, sh, full, send, recv, fsend, frecv)

    tok = lambda w: pl.BlockSpec((tb, w), lambda i: (i, 0))
    sem = pltpu.SemaphoreType.DMA((nl, 3))
    n_in = 11
    return pl.pallas_call(
        body, name="f4_mixer", grid=(nt,),
        in_specs=[pl.BlockSpec((N_LANE_BLOCKS, tb, LANES), lambda i: (0, i, 0)), tok(prest.shape[1]),
                  pl.BlockSpec((CONV_HALO, 2 * CONV_WIDTH), lambda i: (jnp.maximum(i * hb - 1, 0), 0)),
                  tok(D_MODEL), _full(modv), _full(cvv), _full(cw32), _full(w_a), _full(w_b), _full(w_pw), _full(w_out)]
        + [ANY] * (2 * nl),
        out_specs=[ANY] * nl + [tok(D_MODEL), tok(D_MODEL), tok(D_MODEL), tok(D_MODEL), tok(CONV_WIDTH), tok(SSM_WIDTH),
                                tok(CONV_WIDTH), tok(D_MODEL)],
        input_output_aliases={n_in + nl + k: k for k in range(nl)},
        out_shape=[jax.ShapeDtypeStruct(f.shape, f.dtype) for f in late_full]
        + [jax.ShapeDtypeStruct((t, D_MODEL), F32), jax.ShapeDtypeStruct((t, D_MODEL), BF16),
                   jax.ShapeDtypeStruct((t, D_MODEL), BF16), jax.ShapeDtypeStruct((t, D_MODEL), BF16),
                   jax.ShapeDtypeStruct((t, CONV_WIDTH), F32), jax.ShapeDtypeStruct((t, SSM_WIDTH), BF16),
                   jax.ShapeDtypeStruct((t, CONV_WIDTH), BF16), jax.ShapeDtypeStruct((t, D_MODEL), BF16)],
        scratch_shapes=[pltpu.VMEM((tb + CONV_HALO, CONV_WIDTH), F32),
                        pltpu.VMEM((7, tb + CONV_HALO - 8, CONV_WIDTH), F32), sem, sem, sem, sem],
        compiler_params=_params(("arbitrary",)),
    )(ys4, prest, prest, x, modv, cvv, cw32, w_a, w_b, w_pw, w_out, *late_sh, *late_full)


FFN_COLS = 1408


def _f5_ffn(r1, tgt, modv, lnv, fdw, w_up, w_down, tb):
    t = r1.shape[0]
    fw = 2 * FFN_HIDDEN

    def body(r1_ref, tgt_ref, modv_ref, lnv_ref, fdw_ref, wup_hbm, wdn_hbm,
             dr2_ref, d_ref, up_ref, z_ref, acc_ref, wup_v, wdn_v, upbuf, gbuf, hbuf, sems):
        i = pl.program_id(0)

        @pl.when(i == 0)
        def _():
            _load_once(wup_hbm, wup_v, sems.at[0])
            _load_once(wdn_hbm, wdn_v, sems.at[1])
            acc_ref[...] = jnp.zeros_like(acc_ref)
            upbuf[0:8, :] = jnp.zeros((8, fw), F32)

        xh1, _ = _ln_stats(r1_ref[...])
        x1 = xh1 * lnv_ref[0:1, :] + lnv_ref[1:2, :]
        xn2, _ = _ln_stats(x1)
        h2 = (xn2 * (1.0 + modv_ref[4:5, :]) + modv_ref[3:4, :]).astype(BF16)
        for j in range(fw // FFN_COLS):
            cs = slice(j * FFN_COLS, (j + 1) * FFN_COLS)
            up = _dot(h2, wup_v[:, cs])
            upbuf[8:, cs] = up
            up_ref[:, cs] = up.astype(BF16)

        def conv(cs):
            return (fdw_ref[0:1, cs] * upbuf[pl.ds(6, tb), cs] + fdw_ref[1:2, cs] * upbuf[pl.ds(7, tb), cs]
                    + fdw_ref[2:3, cs] * upbuf[pl.ds(8, tb), cs] + fdw_ref[3:4, cs])

        halves = [(slice(j * FFN_COLS, (j + 1) * FFN_COLS),
                   slice(FFN_HIDDEN + j * FFN_COLS, FFN_HIDDEN + (j + 1) * FFN_COLS)) for j in range(FFN_HIDDEN // FFN_COLS)]
        yf = jnp.zeros((tb, D_MODEL), F32)
        for ca, cv in halves:
            v = conv(cv)
            g, dg = _gelu_and_grad(conv(ca))
            gbuf[:, ca] = g.astype(BF16)
            hbuf[:, ca] = (v * dg).astype(BF16)
            z = (g * v).astype(BF16)
            z_ref[:, ca] = z
            yf += _dot(z, wdn_v[ca, :])
        r2 = ALPHA * x1 + modv_ref[5:6, :] * yf
        xh2, rstd2 = _ln_stats(r2)
        e = xh2 * lnv_ref[2:3, :] + lnv_ref[3:4, :] - tgt_ref[...]
        dx2 = e * (1.0 / D_MODEL)
        acc_ref[3:4, :] += _colsum(e * e) * (0.5 / D_MODEL)
        acc_ref[0:1, :] += _colsum(dx2 * xh2)
        acc_ref[1:2, :] += _colsum(dx2)
        dr2 = _ln_bwd(dx2 * lnv_ref[2:3, :], xh2, rstd2)
        dr2_ref[...] = dr2
        acc_ref[2:3, :] += _colsum(dr2 * yf)
        dyf = (modv_ref[5:6, :] * dr2).astype(BF16)
        for ca, cv in halves:
            dz = _dot_nt(dyf, wdn_v[ca, :])
            d_ref[:, ca] = (dz * hbuf[:, ca].astype(F32)).astype(BF16)
            d_ref[:, cv] = (dz * gbuf[:, ca].astype(F32)).astype(BF16)
        upbuf[0:8, :] = upbuf[pl.ds(tb, 8), :]

    tok = lambda w: pl.BlockSpec((tb, w), lambda i: (i, 0))
    return pl.pallas_call(
        body, name="f5_ffn", grid=(t // tb,),
        in_specs=[tok(D_MODEL), tok(D_MODEL), _full(modv), _full(lnv), _full(fdw), ANY, ANY],
        out_specs=[tok(D_MODEL), tok(fw), tok(fw), tok(FFN_HIDDEN), pl.BlockSpec((8, D_MODEL), lambda i: (0, 0))],
        out_shape=[jax.ShapeDtypeStruct((t, D_MODEL), F32), jax.ShapeDtypeStruct((t, fw), BF16),
                   jax.ShapeDtypeStruct((t, fw), BF16), jax.ShapeDtypeStruct((t, FFN_HIDDEN), BF16),
                   jax.ShapeDtypeStruct((8, D_MODEL), F32)],
        scratch_shapes=[pltpu.VMEM(w_up.shape, BF16), pltpu.VMEM(w_down.shape, BF16),
                        pltpu.VMEM((tb + 8, fw), F32), pltpu.VMEM((tb, FFN_HIDDEN), BF16),
                        pltpu.VMEM((tb, FFN_HIDDEN), BF16), pltpu.SemaphoreType.DMA((2,))],
        compiler_params=_params(("arbitrary",)),
    )(r1, tgt, modv, lnv, fdw, w_up, w_down)


def _b1b_ffn_up(d, up, dr2, r1, modv, lnv, fdw, w_up, tb):
    t = dr2.shape[0]
    fw = 2 * FFN_HIDDEN
    nt = t // tb
    hb = tb // 16

    def body(d_ref, nxt_ref, up_ref, dr2_ref, r1_ref, modv_ref, lnv_ref, fdw_ref, wup_hbm, dup_ref, dr1_ref, h2_ref,
             dyf_ref, acc_ref, accw_ref, wup_v, dbuf, shifted, sem):
        i = pl.program_id(0)

        @pl.when(i == 0)
        def _():
            _load_once(wup_hbm, wup_v, sem)
            acc_ref[...] = jnp.zeros_like(acc_ref)
            accw_ref[...] = jnp.zeros_like(accw_ref)

        dbuf[0:tb, :] = d_ref[...].astype(F32)
        dbuf[tb:, :] = jnp.where(i == nt - 1, 0.0, nxt_ref[...].astype(F32))
        dh2 = jnp.zeros((tb, D_MODEL), F32)
        for j in range(fw // FFN_COLS):
            cs = slice(j * FFN_COLS, (j + 1) * FFN_COLS)
            for k in range(1, FFN_KERNEL):
                shifted[k - 1] = dbuf[pl.ds(k, tb), cs]
            ds = [dbuf[pl.ds(0, tb), cs], shifted[0], shifted[1]]
            dup = (fdw_ref[2:3, cs] * ds[0] + fdw_ref[1:2, cs] * ds[1] + fdw_ref[0:1, cs] * ds[2]).astype(BF16)
            dup_ref[:, cs] = dup
            dh2 += _dot_nt(dup, wup_v[:, cs])
            upf = up_ref[:, cs].astype(F32)
            for k in range(FFN_KERNEL):
                accw_ref[k:k + 1, cs] += _colsum(ds[FFN_KERNEL - 1 - k] * upf)
            accw_ref[3:4, cs] += _colsum(ds[0])
        xh1, rstd1 = _ln_stats(r1_ref[...])
        x1 = xh1 * lnv_ref[0:1, :] + lnv_ref[1:2, :]
        xn2, rstd2 = _ln_stats(x1)
        h2_ref[...] = (xn2 * (1.0 + modv_ref[4:5, :]) + modv_ref[3:4, :]).astype(BF16)
        dr2 = dr2_ref[...]
        dyf_ref[...] = (modv_ref[5:6, :] * dr2).astype(BF16)
        acc_ref[0:1, :] += _colsum(dh2 * xn2)
        acc_ref[1:2, :] += _colsum(dh2)
        dx1 = _ln_bwd(dh2 * (1.0 + modv_ref[4:5, :]), xn2, rstd2) + ALPHA * dr2
        acc_ref[2:3, :] += _colsum(dx1 * xh1)
        acc_ref[3:4, :] += _colsum(dx1)
        dr1_ref[...] = _ln_bwd(dx1 * lnv_ref[0:1, :], xh1, rstd1)

    tok = lambda w: pl.BlockSpec((tb, w), lambda i: (i, 0))
    return pl.pallas_call(
        body, name="b1b_ffn_up", grid=(nt,),
        in_specs=[tok(fw), pl.BlockSpec((16, fw), lambda i: (jnp.minimum((i + 1) * hb, t // 16 - 1), 0)), tok(fw),
                  tok(D_MODEL), tok(D_MODEL), _full(modv), _full(lnv), _full(fdw), ANY],
        out_specs=[tok(fw), tok(D_MODEL), tok(D_MODEL), tok(D_MODEL), pl.BlockSpec((8, D_MODEL), lambda i: (0, 0)),
                   pl.BlockSpec((8, fw), lambda i: (0, 0))],
        out_shape=[jax.ShapeDtypeStruct((t, fw), BF16), jax.ShapeDtypeStruct((t, D_MODEL), F32),
                   jax.ShapeDtypeStruct((t, D_MODEL), BF16), jax.ShapeDtypeStruct((t, D_MODEL), BF16),
                   jax.ShapeDtypeStruct((8, D_MODEL), F32), jax.ShapeDtypeStruct((8, fw), F32)],
        scratch_shapes=[pltpu.VMEM(w_up.shape, BF16), pltpu.VMEM((tb + 16, fw), F32),
                        pltpu.VMEM((FFN_KERNEL - 1, tb, FFN_COLS), F32), pltpu.SemaphoreType.DMA],
        compiler_params=_params(("arbitrary",)),
    )(d, d, up, dr2, r1, modv, lnv, fdw, w_up)


def _b2_mixer(dr1, ys4, prest, ya, yb, ycv, vc, merged, modv, cvv, cw32, w_a, w_b, w_pw, w_out, late_dw, tb):
    t = dr1.shape[0]
    nt = t // tb
    nl = len(LATE)
    hb = tb // CONV_HALO
    cwd = CONV_WIDTH

    def body(dr1_ref, ys_ref, pr_ref, halo_ref, ya_ref, yb_ref, ycv_ref, vc_ref, mg_ref, modv_ref, cvv_ref, cw_ref,
             wa_ref, wb_ref, wpw_ref, wout_ref, *rest):
        dw, got = rest[:nl], rest[nl:2 * nl]
        (dys_ref, dpr_ref, dya_ref, dyb_ref, dycv_ref, dym_ref, acc_a, acc_b, acc_w, vbuf, dvbuf, vrot, dvrot,
         send, recv) = rest[2 * nl:]
        i = pl.program_id(0)
        ti = nt - 1 - i

        @pl.when(i == 0)
        def _():
            for cp in _scatter(LATE, dw, got, send, recv):
                cp.start()
            acc_a[...] = jnp.zeros_like(acc_a)
            acc_b[...] = jnp.zeros_like(acc_b)
            acc_w[...] = jnp.zeros_like(acc_w)
            dvbuf[pl.ds(tb, CONV_HALO), :] = jnp.zeros((CONV_HALO, cwd), F32)

        dr1 = dr1_ref[...]
        dym = (modv_ref[2:3, :] * dr1).astype(BF16)
        dym_ref[...] = dym
        ym = _dot(mg_ref[...], wout_ref[...])
        acc_a[0:1, :] += _colsum(dr1 * ym)
        dmg = _dot_nt(dym, wout_ref[...])
        sgs = _sig(pr_ref[:, 2 * cwd:2 * cwd + D_MODEL])
        sgc = _sig(pr_ref[:, 2 * cwd + D_MODEL:])
        ya_v = ya_ref[...].astype(F32)
        syb = _sig(yb_ref[...].astype(F32))
        ycv_v = ycv_ref[...].astype(F32)
        dpr_ref[:, 2 * cwd:2 * cwd + D_MODEL] = (dmg * (ya_v * syb) * sgs * (1.0 - sgs)).astype(BF16)
        dpr_ref[:, 2 * cwd + D_MODEL:] = (dmg * ycv_v * sgc * (1.0 - sgc)).astype(BF16)
        dyssm = dmg * sgs
        dya = (dyssm * syb).astype(BF16)
        dyb = (dyssm * ya_v * syb * (1.0 - syb)).astype(BF16)
        dya_ref[...] = dya
        dyb_ref[...] = dyb
        dyg = _dot_nt(dya, wa_ref[...]) + _dot_nt(dyb, wb_ref[...])
        ys = jnp.concatenate([ys_ref[b] for b in range(N_LANE_BLOCKS)], axis=-1)
        dys = dyg * _dgelu(ys)
        for b in range(N_LANE_BLOCKS):
            dys_ref[b] = dys[:, b * LANES:(b + 1) * LANES]
        dycv = (dmg * sgc).astype(BF16)
        dycv_ref[...] = dycv
        dvs = _dot_nt(dycv, wpw_ref[...])
        xh, rstd = _ln_stats(vc_ref[...])
        vl = xh * cvv_ref[1:2, :] + cvv_ref[2:3, :]
        s = _sig(vl)
        dvl = dvs * s * (1.0 + vl * (1.0 - s))
        acc_b[1:2, :] += _colsum(dvl * xh)
        acc_b[2:3, :] += _colsum(dvl)
        dvc = _ln_bwd(dvl * cvv_ref[1:2, :], xh, rstd)
        acc_b[0:1, :] += _colsum(dvc)
        hv = halo_ref[:, 0:cwd] * _sig(halo_ref[:, cwd:2 * cwd])
        vbuf[0:CONV_HALO, :] = jnp.where(ti == 0, 0.0, hv)
        cva = pr_ref[:, 0:cwd]
        scg = _sig(pr_ref[:, cwd:2 * cwd])
        vbuf[CONV_HALO:, :] = cva * scg
        dvbuf[0:tb, :] = dvc
        _fill_rotations(vbuf, vrot, tb + CONV_HALO - 8)
        _fill_rotations(dvbuf, dvrot, tb + CONV_HALO - 8)
        dv = jnp.zeros((tb, cwd), F32)
        for k in range(CONV_KERNEL):
            dv += _rows_at(dvbuf, dvrot, CONV_KERNEL - 1 - k, tb) * cw_ref[k:k + 1, :]
            acc_w[k:k + 1, :] += _colsum(dvc * _rows_at(vbuf, vrot, CONV_HALO - CONV_KERNEL + 1 + k, tb))
        dvbuf[pl.ds(tb, CONV_HALO), :] = dvbuf[0:CONV_HALO, :]
        dpr_ref[:, 0:cwd] = (dv * scg).astype(BF16)
        dpr_ref[:, cwd:2 * cwd] = (dv * cva * scg * (1.0 - scg)).astype(BF16)

        @pl.when(i == nt - 1)
        def _():
            for cp in _scatter(LATE, dw, got, send, recv):
                cp.wait()

    rtok = lambda w: pl.BlockSpec((tb, w), lambda i: (nt - 1 - i, 0))
    r4 = pl.BlockSpec((N_LANE_BLOCKS, tb, LANES), lambda i: (0, nt - 1 - i, 0))
    pw = prest.shape[1]
    return pl.pallas_call(
        body, name="b2_mixer", grid=(nt,),
        in_specs=[rtok(D_MODEL), r4, rtok(pw),
                  pl.BlockSpec((CONV_HALO, 2 * cwd), lambda i: (jnp.maximum((nt - 1 - i) * hb - 1, 0), 0)),
                  rtok(D_MODEL), rtok(D_MODEL), rtok(D_MODEL), rtok(cwd), rtok(D_MODEL),
                  _full(modv), _full(cvv), _full(cw32), _full(w_a), _full(w_b), _full(w_pw), _full(w_out)] + [ANY] * nl,
        out_specs=[ANY] * nl + [r4, rtok(pw), rtok(D_MODEL), rtok(D_MODEL), rtok(D_MODEL), rtok(D_MODEL),
                   pl.BlockSpec((8, D_MODEL), lambda i: (0, 0)), pl.BlockSpec((8, cwd), lambda i: (0, 0)),
                   pl.BlockSpec((CONV_HALO, cwd), lambda i: (0, 0))],
        out_shape=[jax.ShapeDtypeStruct((N_DEV - 1,) + _piece(*BIG[m][1:]), BF16) for m in LATE]
        + [jax.ShapeDtypeStruct((N_LANE_BLOCKS, t, LANES), F32), jax.ShapeDtypeStruct((t, pw), BF16),
                   jax.ShapeDtypeStruct((t, D_MODEL), BF16), jax.ShapeDtypeStruct((t, D_MODEL), BF16),
                   jax.ShapeDtypeStruct((t, D_MODEL), BF16), jax.ShapeDtypeStruct((t, D_MODEL), BF16),
                   jax.ShapeDtypeStruct((8, D_MODEL), F32), jax.ShapeDtypeStruct((8, cwd), F32),
                   jax.ShapeDtypeStruct((CONV_HALO, cwd), F32)],
        scratch_shapes=[pltpu.VMEM((tb + CONV_HALO, cwd), F32), pltpu.VMEM((tb + CONV_HALO, cwd), F32),
                        pltpu.VMEM((7, tb + CONV_HALO - 8, cwd), F32), pltpu.VMEM((7, tb + CONV_HALO - 8, cwd), F32),
                        pltpu.SemaphoreType.DMA((nl, N_DEV - 1)), pltpu.SemaphoreType.DMA((nl, N_DEV - 1))],
        compiler_params=_params(("arbitrary",)),
    )(dr1, ys4, prest, prest, ya, yb, ycv, vc, merged, modv, cvv, cw32, w_a, w_b, w_pw, w_out, *late_dw)


def _s5c_state_bwd(dy4, sg_r, sg_i, a8, hr, hi):
    nb, t, _ = dy4.shape
    nc = t // SSM_CHUNK
    sc = STATE_COLS

    def body(dy_ref, sgr_ref, sgi_ref, a_ref, hr_ref, hi_ref, dxr_ref, dxi_ref, da_ref, dsgr_ref, dsgi_ref,
             g_s, lr_s, li_s, xr_s, xi_s):
        dy = _load_chunks(dy_ref, nc)
        _build_maps(sgr_ref, g_s)
        lr_s[...] = _dot(dy, g_s[...])
        _build_maps(sgi_ref, g_s)
        li_s[...] = _dot(dy, g_s[...])
        ar = a_ref[0, 0:1, :]
        ai = a_ref[0, 1:2, :]

        def step(k, carry):
            pr, pi, dar, dai = carry
            c = nc - 1 - k
            xr_s[pl.ds(c, 1), :] = pr
            xi_s[pl.ds(c, 1), :] = pi
            h_r = hr_ref[0, pl.ds(c, 1), :]
            h_i = hi_ref[0, pl.ds(c, 1), :]
            dar = dar + pr * h_r + pi * h_i
            dai = dai - pr * h_i + pi * h_r
            nr = lr_s[pl.ds(c, 1), :] + ar * pr + ai * pi
            ni = li_s[pl.ds(c, 1), :] - ai * pr + ar * pi
            return nr, ni, dar, dai

        z = jnp.zeros((1, sc), F32)
        _, _, dar, dai = lax.fori_loop(0, nc, step, (z, z, z, z))
        da_ref[0] = jnp.concatenate([dar, dai, jnp.zeros((6, sc), F32)], axis=0)
        dxr_ref[0] = xr_s[...].astype(BF16)
        dxi_ref[0] = xi_s[...].astype(BF16)
        for h_ref, o_ref in ((hr_ref, dsgr_ref), (hi_ref, dsgi_ref)):
            hb = h_ref[0].astype(BF16)
            for j in range(SSM_CHUNK):
                o_ref[j] = _fold(_dot_tn(dy[:, j * LANES:(j + 1) * LANES], hb), SSM_STATE)

    blk = lambda r, c: pl.BlockSpec((1, r, c), lambda b: (b, 0, 0))
    return pl.pallas_call(
        body, name="s5c_state_bwd", grid=(nb,),
        in_specs=[_tblk(t), _cblk(SSM_STATE), _cblk(SSM_STATE), blk(8, sc), blk(nc, sc), blk(nc, sc)],
        out_specs=[blk(nc, sc), blk(nc, sc), blk(8, sc), _cblk(SSM_STATE), _cblk(SSM_STATE)],
        out_shape=[jax.ShapeDtypeStruct((nb, nc, sc), BF16), jax.ShapeDtypeStruct((nb, nc, sc), BF16),
                   jax.ShapeDtypeStruct((nb, 8, sc), F32),
                   jax.ShapeDtypeStruct((SSM_CHUNK, SSM_WIDTH, SSM_STATE), F32),
                   jax.ShapeDtypeStruct((SSM_CHUNK, SSM_WIDTH, SSM_STATE), F32)],
        scratch_shapes=[pltpu.VMEM((CHUNK_COLS, sc), BF16)] + [pltpu.VMEM((nc, sc), F32)] * 4,
        compiler_params=_params(("arbitrary",)),
    )(dy4, sg_r, sg_i, a8, hr, hi)


def _s5d_input_bwd(dy4, u4, kc, sw_r, sw_i, dxr, dxi):
    nb, t, _ = dy4.shape
    nc = t // SSM_CHUNK
    sc = STATE_COLS

    def body(dy_ref, u_ref, kc_ref, swr_ref, swi_ref, dxr_ref, dxi_ref, du_ref, dkc_ref, dswr_ref, dswi_ref,
             tm_s, w_s, dk_s):
        dy = _load_chunks(dy_ref, nc)
        u = _load_chunks(u_ref, nc)
        _build_toeplitz(kc_ref, tm_s)
        du = _dot_nt(dy, tm_s[...])
        _build_maps(swr_ref, w_s)
        du += _dot_nt(dxr_ref[0], w_s[...])
        _build_maps(swi_ref, w_s)
        du += _dot_nt(dxi_ref[0], w_s[...])
        _store_chunks(du_ref, du, nc)
        dk_s[...] = jnp.zeros_like(dk_s)
        for ji in range(SSM_CHUNK):
            uj = u[:, ji * LANES:(ji + 1) * LANES]
            rows = _dot_tn(uj, dy)
            for jo in range(ji, SSM_CHUNK):
                dk_s[jo - ji] += rows[:, jo * LANES:(jo + 1) * LANES]
            dswr_ref[ji] = _fold(_dot_tn(uj, dxr_ref[0]), SSM_STATE)
            dswi_ref[ji] = _fold(_dot_tn(uj, dxi_ref[0]), SSM_STATE)
        for d in range(SSM_CHUNK):
            dkc_ref[d] = _fold(dk_s[d], SSM_GROUP)

    blk = lambda r, c: pl.BlockSpec((1, r, c), lambda b: (b, 0, 0))
    return pl.pallas_call(
        body, name="s5d_input_bwd", grid=(nb,),
        in_specs=[_tblk(t), _tblk(t), _cblk(SSM_GROUP), _cblk(SSM_STATE), _cblk(SSM_STATE),
                  blk(nc, sc), blk(nc, sc)],
        out_specs=[_tblk(t), _cblk(SSM_GROUP), _cblk(SSM_STATE), _cblk(SSM_STATE)],
        out_shape=[jax.ShapeDtypeStruct((nb, t, LANES), F32),
                   jax.ShapeDtypeStruct((SSM_CHUNK, SSM_WIDTH, SSM_GROUP), F32),
                   jax.ShapeDtypeStruct((SSM_CHUNK, SSM_WIDTH, SSM_STATE), F32),
                   jax.ShapeDtypeStruct((SSM_CHUNK, SSM_WIDTH, SSM_STATE), F32)],
        scratch_shapes=[pltpu.VMEM((CHUNK_COLS, CHUNK_COLS), BF16), pltpu.VMEM((CHUNK_COLS, sc), BF16),
                        pltpu.VMEM((SSM_CHUNK, LANES, LANES), F32)],
        compiler_params=_params(("arbitrary",)),
    )(dy4, u4, kc, sw_r, sw_i, dxr, dxi)


def _b3_inproj(x, dr1, du4, dprest, modv, w_in, mid_dw, tb):
    t = x.shape[0]
    nt = t // tb
    nl = len(MID)
    pw = IN_PROJ_WIDTH - SSM_WIDTH

    def body(x_ref, dr1_ref, du_ref, dpr_ref, modv_ref, w_hbm, *rest):
        dw, got = rest[:nl], rest[nl:2 * nl]
        gx_ref, dp_ref, acc_ref, accb_ref, w_v, sem, send, recv = rest[2 * nl:]

        @pl.when(pl.program_id(0) == 0)
        def _():
            for cp in _scatter(MID, dw, got, send, recv):
                cp.start()
            _load_once(w_hbm, w_v, sem)
            acc_ref[...] = jnp.zeros_like(acc_ref)
            accb_ref[...] = jnp.zeros_like(accb_ref)

        du = jnp.concatenate([du_ref[b] for b in range(N_LANE_BLOCKS)], axis=-1).astype(BF16)
        dpr = dpr_ref[...]
        dp_ref[:, 0:SSM_WIDTH] = du
        dp_ref[:, SSM_WIDTH:] = dpr
        accb_ref[0:1, 0:SSM_WIDTH] += _colsum(du.astype(F32))
        accb_ref[0:1, SSM_WIDTH:] += _colsum(dpr.astype(F32))
        dh = _dot_nt(du, w_v[:, 0:SSM_WIDTH]) + _dot_nt(dpr, w_v[:, SSM_WIDTH:])
        xn, rstd = _ln_stats(x_ref[...])
        acc_ref[0:1, :] += _colsum(dh * xn)
        acc_ref[1:2, :] += _colsum(dh)
        gx_ref[...] = _ln_bwd(dh * (1.0 + modv_ref[1:2, :]), xn, rstd) + ALPHA * dr1_ref[...]

        @pl.when(pl.program_id(0) == nt - 1)
        def _():
            for cp in _scatter(MID, dw, got, send, recv):
                cp.wait()

    tok = lambda w: pl.BlockSpec((tb, w), lambda i: (i, 0))
    ssem = pltpu.SemaphoreType.DMA((nl, N_DEV - 1))
    return pl.pallas_call(
        body, name="b3_inproj", grid=(nt,),
        in_specs=[tok(D_MODEL), tok(D_MODEL), pl.BlockSpec((N_LANE_BLOCKS, tb, LANES), lambda i: (0, i, 0)), tok(pw),
                  _full(modv), ANY] + [ANY] * nl,
        out_specs=[ANY] * nl + [tok(D_MODEL), tok(IN_PROJ_WIDTH), pl.BlockSpec((8, D_MODEL), lambda i: (0, 0)),
                                pl.BlockSpec((8, IN_PROJ_WIDTH), lambda i: (0, 0))],
        out_shape=[jax.ShapeDtypeStruct((N_DEV - 1,) + _piece(*BIG[m][1:]), BF16) for m in MID]
        + [jax.ShapeDtypeStruct((t, D_MODEL), F32), jax.ShapeDtypeStruct((t, IN_PROJ_WIDTH), BF16),
                   jax.ShapeDtypeStruct((8, D_MODEL), F32), jax.ShapeDtypeStruct((8, IN_PROJ_WIDTH), F32)],
        scratch_shapes=[pltpu.VMEM(w_in.shape, BF16), pltpu.SemaphoreType.DMA, ssem, ssem],
        compiler_params=_params(("arbitrary",)),
    )(x, dr1, du4, dprest, modv, w_in, *mid_dw)


TN_ROWS = 2048


def _tn_matmul(name, a, b, tm, tn):
    t, m = a.shape
    n = b.shape[1]
    tt = min(TN_ROWS, t)
    nk = t // tt

    def body(a_ref, b_ref, o_ref, acc):
        k = pl.program_id(2)

        @pl.when(k == 0)
        def _():
            acc[...] = jnp.zeros_like(acc)

        acc[...] += _dot_tn(a_ref[...], b_ref[...])

        @pl.when(k == nk - 1)
        def _():
            o_ref[...] = acc[...].astype(BF16)

    return pl.pallas_call(
        body, name=name, grid=(m // tm, n // tn, nk),
        in_specs=[pl.BlockSpec((tt, tm), lambda i, j, k: (k, i)), pl.BlockSpec((tt, tn), lambda i, j, k: (k, j))],
        out_specs=pl.BlockSpec((tm, tn), lambda i, j, k: (i, j)),
        out_shape=jax.ShapeDtypeStruct((m, n), BF16),
        scratch_shapes=[pltpu.VMEM((tm, tn), F32)],
        compiler_params=_params(("arbitrary", "arbitrary", "arbitrary")),
    )(a, b)


def _local_step(x, tgt, modv, wb, shards, sp, tb=256):
    t = x.shape[0]
    row8 = lambda rows, w: jnp.concatenate([r.reshape(1, w) for r in rows] + [jnp.zeros((8 - len(rows), w), F32)], axis=0)
    lnv = row8([sp["ln1_g"], sp["ln1_b"], sp["ln2_g"], sp["ln2_b"]], D_MODEL)
    cvv = row8([sp["cv_dw_b"], sp["cv_ln_g"], sp["cv_ln_b"]], CONV_WIDTH)
    cw32 = jnp.concatenate([sp["cv_dw_w"].reshape(CONV_KERNEL, CONV_WIDTH), jnp.zeros((1, CONV_WIDTH), F32)], axis=0)
    fdw = row8(list(sp["ffn_dw_w"].reshape(FFN_KERNEL, 2 * FFN_HIDDEN)) + [sp["ffn_dw_b"]], 2 * FFN_HIDDEN)
    b_in = sp["b_in"].reshape(1, IN_PROJ_WIDTH)
    ssm = tuple(sp[k] for k in ("ssm_lambda_re", "ssm_lambda_im", "ssm_log_dt", "ssm_b_re", "ssm_b_im", "ssm_c_re",
                                "ssm_c_im", "ssm_d"))
    (kc, sw_r, sw_i, sg_r, sg_i, a), ssm_vjp = jax.vjp(_s5_build, *ssm)
    a8 = jnp.concatenate([a, jnp.zeros((N_LANE_BLOCKS, 6, STATE_COLS), F32)], axis=1)

    name = lambda m: BIG[m][0]
    *mid_w, u4, prest, h1 = _f1_inproj(x, modv, b_in, wb["w_in"], [shards[m] for m in MID], [wb[name(m)] for m in MID], tb)
    w_a, w_b, w_pw, w_out = mid_w
    hr, hi = _s5a_state(u4, sw_r, sw_i, a8)
    ys4 = _s5b_out(u4, kc, sg_r, sg_i, hr, hi)
    w_up, w_down, r1, ya, yb, ycv, vc, yg, vs, merged = _f4_mixer(
        ys4, prest, x, modv, cvv, cw32, w_a, w_b, w_pw, w_out, [shards[m] for m in LATE], [wb[name(m)] for m in LATE], tb)
    dr2, dconv, up, z, acc5 = _f5_ffn(r1, tgt, modv, lnv, fdw, w_up, w_down, tb)
    dup, dr1, h2, dyf, acc1b, acc1a = _b1b_ffn_up(dconv, up, dr2, r1, modv, lnv, fdw, w_up, tb)
    late_dw = [_tn_matmul("dw_up", h2, dup, 1024, FFN_COLS), _tn_matmul("dw_down", z, dyf, FFN_COLS, 1024)]
    got_up, got_down, dys4, dprest, dya, dyb, dycv, dym, acc2a, acc2b, acc2w = _b2_mixer(
        dr1, ys4, prest, ya, yb, ycv, vc, merged, modv, cvv, cw32, w_a, w_b, w_pw, w_out, late_dw, tb)
    mid_dw = [_tn_matmul("dw_glu_a", yg, dya, 512, 1024), _tn_matmul("dw_glu_b", yg, dyb, 512, 1024),
              _tn_matmul("dw_pw", vs, dycv, 512, 1024), _tn_matmul("dw_out", merged, dym, 1024, 1024)]
    dxr, dxi, da8, dsg_r, dsg_i = _s5c_state_bwd(dys4, sg_r, sg_i, a8, hr, hi)
    du4, dkc, dsw_r, dsw_i = _s5d_input_bwd(dys4, u4, kc, sw_r, sw_i, dxr, dxi)
    dssm = ssm_vjp((dkc, dsw_r, dsw_i, dsg_r, dsg_i, da8[:, 0:2, :]))
    *mid_got, gx, dp, acc3, acc3b = _b3_inproj(x, dr1, du4, dprest, modv, wb["w_in"], mid_dw, tb)
    dbig = [_tn_matmul("dw_in", h1, dp, 1024, 896)] + mid_dw + late_dw
    dmod = jnp.concatenate([acc3[1], acc3[0], acc2a[0], acc1b[1], acc1b[0], acc5[2]])
    small = {
        "dmod": dmod, "b_in": acc3b[0],
        "ssm_lambda_re": dssm[0], "ssm_lambda_im": dssm[1], "ssm_log_dt": dssm[2], "ssm_b_re": dssm[3],
        "ssm_b_im": dssm[4], "ssm_c_re": dssm[5], "ssm_c_im": dssm[6], "ssm_d": dssm[7],
        "cv_dw_w": acc2w[0:CONV_KERNEL], "cv_dw_b": acc2b[0], "cv_ln_g": acc2b[1], "cv_ln_b": acc2b[2],
        "ln1_g": acc1b[2], "ln1_b": acc1b[3], "ffn_dw_w": acc1a[0:FFN_KERNEL], "ffn_dw_b": acc1a[3],
        "ln2_g": acc5[0], "ln2_b": acc5[1], "loss": jnp.sum(acc5[3]).reshape(1),
    }
    return gx, dbig, list(mid_got) + [got_up, got_down], small


def _allgather(name, shard):
    m_per, n = shard.shape

    def body(x_ref, out_ref, send_sems, recv_sems, local_sem):
        x, y, c, chips = _place()
        me, sibling = (x, y, c), (x, y, 1 - c)

        def rows(px, py, pc):
            return out_ref.at[pl.ds((4 * px + 2 * py + pc) * m_per, m_per), :]

        def copy(k, block, to, src=None):
            return pltpu.make_async_remote_copy(
                src_ref=rows(*block) if src is None else src, dst_ref=rows(*block),
                send_sem=send_sems.at[k], recv_sem=recv_sems.at[k], device_id=to, device_id_type=MESH)

        mine = pltpu.make_async_copy(x_ref, rows(*me), local_sem)
        mine.start()
        first = [copy(0, me, sibling, src=x_ref)]
        first += [copy(1 + j, me, (*chip, c), src=x_ref) for j, chip in enumerate(chips)]
        for cp in first:
            cp.start()
        passed = [copy(4 + j, (*chip, c), sibling) for j, chip in enumerate(chips)]
        for j, chip in enumerate(chips):
            copy(1 + j, (*chip, c), me).wait_recv()
            passed[j].start()
        copy(0, sibling, me).wait_recv()
        for j, chip in enumerate(chips):
            copy(4 + j, (*chip, 1 - c), me).wait_recv()
        for cp in first + passed:
            cp.wait_send()
        mine.wait()

    return pl.pallas_call(
        body, name=name,
        out_shape=jax.ShapeDtypeStruct((N_DEV * m_per, n), shard.dtype),
        in_specs=[pl.BlockSpec(memory_space=pltpu.VMEM)],
        out_specs=pl.BlockSpec(memory_space=pltpu.VMEM),
        scratch_shapes=[pltpu.SemaphoreType.DMA((7,)), pltpu.SemaphoreType.DMA((7,)), pltpu.SemaphoreType.DMA],
        compiler_params=_params(),
    )(shard)


def _add_rows(pr):
    return 64 if pr % 64 == 0 else 16


def _gather_weights(shards):
    nm = len(BIG)
    nl = len(DIRECT)

    def body(*refs):
        ins, outs, lsh = refs[:nm], refs[nm:2 * nm], refs[2 * nm:2 * nm + nl]
        stage = refs[2 * nm + nl:3 * nm + nl]
        send, recv, fsend, frecv, lsem = refs[3 * nm + nl:]
        x, y, c, chips = _place()
        s_me = 2 * x + y
        sibling = (x, y, 1 - c)
        pend = []
        for m in range(nm):
            stage[m][...] = ins[m][...].astype(BF16)
        for m, (_, kind, shape) in enumerate(BIG):
            pr, pc = _piece(kind, shape)
            for k in range(2):
                cp = pltpu.make_async_copy(stage[m].at[pl.ds(k * pr, pr), :], _piece_at(outs[m], kind, shape, s_me, k),
                                           lsem.at[m, k])
                cp.start()
                pend.append(cp.wait)
            if m in DIRECT:
                cp = pltpu.make_async_copy(stage[m], lsh[DIRECT.index(m)], lsem.at[m, 2])
                cp.start()
                pend.append(cp.wait)
                continue
            for j, chip in enumerate(chips):
                cp = pltpu.make_async_remote_copy(
                    src_ref=stage[m].at[pl.ds(pl.multiple_of(c * pr, 16), pr), :],
                    dst_ref=_piece_at(outs[m], kind, shape, s_me, c),
                    send_sem=send.at[m, j], recv_sem=recv.at[m, j], device_id=(*chip, c), device_id_type=MESH)
                cp.start()
                pend.append(cp.wait_send)
        for m in EARLY:
            _, kind, shape = BIG[m]
            for j, (cx, cy) in enumerate(chips):
                got = _piece_at(outs[m], kind, shape, 2 * cx + cy, c)
                pltpu.make_async_remote_copy(src_ref=got, dst_ref=got, send_sem=send.at[m, j], recv_sem=recv.at[m, j],
                                             device_id=(cx, cy, c), device_id_type=MESH).wait_recv()
                cp = pltpu.make_async_remote_copy(src_ref=got, dst_ref=got, send_sem=fsend.at[m, j],
                                                  recv_sem=frecv.at[m, j], device_id=sibling, device_id_type=MESH)
                cp.start()
                pend.append(cp.wait_send)
        for m in EARLY:
            _, kind, shape = BIG[m]
            for j, (cx, cy) in enumerate(chips):
                got = _piece_at(outs[m], kind, shape, 2 * cx + cy, 1 - c)
                pltpu.make_async_remote_copy(src_ref=got, dst_ref=got, send_sem=fsend.at[m, j], recv_sem=frecv.at[m, j],
                                             device_id=sibling, device_id_type=MESH).wait_recv()
        for w in pend:
            w()

    sem = lambda *s: pltpu.SemaphoreType.DMA(s)
    res = pl.pallas_call(
        body, name="gather_weights",
        out_shape=[jax.ShapeDtypeStruct(shape, BF16) for _, _, shape in BIG]
        + [jax.ShapeDtypeStruct(shards[m].shape, BF16) for m in DIRECT],
        in_specs=[pl.BlockSpec(memory_space=pltpu.VMEM)] * nm,
        out_specs=[ANY] * (nm + nl),
        scratch_shapes=[pltpu.VMEM(s.shape, BF16) for s in shards] + [sem(nm, 3), sem(nm, 3), sem(nm, 3), sem(nm, 3),
                                                                         sem(nm, 3)],
        compiler_params=_params(),
    )(*shards)
    return res[:nm], dict(zip(DIRECT, res[nm:]))


def _rs1_sibling(grads):
    mats = [BIG[m] for m in EARLY]
    nm = len(mats)

    def body(*refs):
        ins, outs = refs[:nm], refs[nm:2 * nm]
        send, recv = refs[2 * nm:]
        x, y, c, _ = _place()
        cps = []
        for m, (_, kind, shape) in enumerate(mats):
            for s in range(N_CHIP):
                cp = pltpu.make_async_remote_copy(
                    src_ref=_piece_at(ins[m], kind, shape, s, 1 - c), dst_ref=outs[m].at[s],
                    send_sem=send.at[m, s], recv_sem=recv.at[m, s], device_id=(x, y, 1 - c), device_id_type=MESH)
                cp.start()
                cps.append(cp)
        for cp in cps:
            cp.wait()

    sem = lambda *s: pltpu.SemaphoreType.DMA(s)
    return pl.pallas_call(
        body, name="rs1_sibling",
        out_shape=[jax.ShapeDtypeStruct((N_CHIP,) + _piece(kind, shape), BF16) for _, kind, shape in mats],
        in_specs=[ANY] * nm, out_specs=[ANY] * nm,
        scratch_shapes=[sem(nm, N_CHIP), sem(nm, N_CHIP)],
        compiler_params=_params(),
    )(*grads)


def _rs2_chips(grads, halves):
    mats = [BIG[m] for m in EARLY]
    nm = len(mats)

    def body(*refs):
        gin, hin = refs[:nm], refs[nm:2 * nm]
        own, got = refs[2 * nm:3 * nm], refs[3 * nm:4 * nm]
        send, recv, lsem = refs[4 * nm:]
        x, y, c, chips = _place()
        s_me = 2 * x + y
        for m, (_, kind, shape) in enumerate(mats):
            pr, pc = _piece(kind, shape)

            def scoped(a, b, m=m, kind=kind, shape=shape, pr=pr):
                loads = [pltpu.make_async_copy(_piece_at(gin[m], kind, shape, s, c), a.at[s], lsem.at[s])
                         for s in range(N_CHIP)]
                loads.append(pltpu.make_async_copy(hin[m], b, lsem.at[N_CHIP]))
                for cp in loads:
                    cp.start()
                for cp in loads:
                    cp.wait()
                step = _add_rows(pr)
                for s in range(N_CHIP):
                    def add(i, _, s=s):
                        r = pl.ds(pl.multiple_of(i * step, 16), step)
                        a[s, r, :] = (a[s, r, :].astype(F32) + b[s, r, :].astype(F32)).astype(BF16)
                        return 0

                    lax.fori_loop(0, pr // step, add, 0)
                waits = []
                for j, (cx, cy) in enumerate(chips):
                    cp = pltpu.make_async_remote_copy(src_ref=a.at[2 * cx + cy], dst_ref=got[m].at[j], send_sem=send.at[m, j],
                                                      recv_sem=recv.at[m, j], device_id=(cx, cy, c), device_id_type=MESH)
                    cp.start()
                    waits.append(cp.wait_send)
                cp = pltpu.make_async_copy(a.at[s_me], own[m], lsem.at[N_CHIP + 1])
                cp.start()
                waits.append(cp.wait)
                for w in waits:
                    w()

            pl.run_scoped(scoped, pltpu.VMEM((N_CHIP, pr, pc), BF16), pltpu.VMEM((N_CHIP, pr, pc), BF16))
        for m in range(nm):
            for j, (cx, cy) in enumerate(chips):
                pltpu.make_async_remote_copy(src_ref=got[m].at[j], dst_ref=got[m].at[j], send_sem=send.at[m, j],
                                             recv_sem=recv.at[m, j], device_id=(cx, cy, c), device_id_type=MESH).wait_recv()

    sem = lambda *s: pltpu.SemaphoreType.DMA(s)
    pieces = [_piece(kind, shape) for _, kind, shape in mats]
    return pl.pallas_call(
        body, name="rs2_chips",
        out_shape=[jax.ShapeDtypeStruct(p, BF16) for p in pieces] + [jax.ShapeDtypeStruct((3,) + p, BF16) for p in pieces],
        in_specs=[ANY] * (2 * nm), out_specs=[ANY] * (2 * nm),
        scratch_shapes=[sem(nm, 3), sem(nm, 3), sem(N_CHIP + 2)],
        compiler_params=_params(),
    )(*grads, *halves)


def _rs3_finish(own, got):
    nm = len(BIG)

    def body(*refs):
        oin, gin = refs[:nm], refs[nm:2 * nm]
        outs = refs[2 * nm:3 * nm]
        send, recv, lsem = refs[3 * nm:]
        x, y, c, _ = _place()
        for m, (_, kind, shape) in enumerate(BIG):
            pr, pc = _piece(kind, shape)
            ng = got[m].shape[0]

            def scoped(a, g, f, m=m, pr=pr, ng=ng, kind=kind, shape=shape):
                mine = _piece_at(oin[m], kind, shape, 2 * x + y, c) if m in DIRECT else oin[m]
                loads = [pltpu.make_async_copy(mine, a, lsem.at[0]), pltpu.make_async_copy(gin[m], g, lsem.at[1])]
                for cp in loads:
                    cp.start()
                for cp in loads:
                    cp.wait()
                step = _add_rows(pr)

                def add(i, _):
                    r = pl.ds(pl.multiple_of(i * step, 16), step)
                    acc = a[r, :].astype(F32)
                    for q in range(ng):
                        acc = acc + g[q, r, :].astype(F32)
                    f[r, :] = acc
                    return 0

                lax.fori_loop(0, pr // step, add, 0)
                dst = outs[m].at[pl.ds(pl.multiple_of(c * pr, 8), pr), :]
                local = pltpu.make_async_copy(f, dst, lsem.at[2])
                local.start()
                cp = pltpu.make_async_remote_copy(src_ref=f, dst_ref=dst, send_sem=send.at[m], recv_sem=recv.at[m],
                                                  device_id=(x, y, 1 - c), device_id_type=MESH)
                cp.start()
                cp.wait_send()
                local.wait()

            pl.run_scoped(scoped, pltpu.VMEM((pr, pc), BF16), pltpu.VMEM((ng, pr, pc), BF16), pltpu.VMEM((pr, pc), F32))
        for m, (_, kind, shape) in enumerate(BIG):
            pr, pc = _piece(kind, shape)
            dst = outs[m].at[pl.ds(pl.multiple_of((1 - c) * pr, 8), pr), :]
            pltpu.make_async_remote_copy(src_ref=dst, dst_ref=dst, send_sem=send.at[m], recv_sem=recv.at[m],
                                         device_id=(x, y, 1 - c), device_id_type=MESH).wait_recv()

    sem = lambda *s: pltpu.SemaphoreType.DMA(s)
    pieces = [_piece(kind, shape) for _, kind, shape in BIG]
    return pl.pallas_call(
        body, name="rs3_finish",
        out_shape=[jax.ShapeDtypeStruct((2 * pr, pc), F32) for pr, pc in pieces],
        in_specs=[ANY] * (2 * nm), out_specs=[ANY] * nm,
        scratch_shapes=[sem(nm), sem(nm), sem(3)],
        compiler_params=_params(),
    )(*own, *got)


def _cond_fwd(c_all, w_shard, b_shard):
    def body(c_ref, w_ref, b_ref, act_ref, mod_ref):
        cv = c_ref[...]
        act = cv * _sig(cv)
        act_ref[...] = act
        mod_ref[...] = _dot(act.astype(BF16), w_ref[...].astype(BF16)) + b_ref[...]

    return pl.pallas_call(
        body, name="cond_fwd",
        out_shape=[jax.ShapeDtypeStruct(c_all.shape, F32), jax.ShapeDtypeStruct((c_all.shape[0], w_shard.shape[1]), F32)],
        compiler_params=_params(),
    )(c_all, w_shard, b_shard)


def _cond_bwd(act_t, dmod_shard):
    k, n = act_t.shape[0], dmod_shard.shape[1]

    def body(a_ref, d_ref, o_ref):
        acc = a_ref[:, 0:1] * d_ref[0:1, :]
        for e in range(1, N_DEV):
            acc += a_ref[:, e:e + 1] * d_ref[e:e + 1, :]
        o_ref[...] = acc

    tr = 256
    return pl.pallas_call(
        body, name="cond_bwd", grid=(k // tr,),
        in_specs=[pl.BlockSpec((tr, N_DEV), lambda i: (i, 0)), _full(dmod_shard)],
        out_specs=pl.BlockSpec((tr, n), lambda i: (i, 0)),
        out_shape=jax.ShapeDtypeStruct((k, n), F32),
        compiler_params=_params(("arbitrary",)),
    )(act_t, dmod_shard)


def _sum_blocks(allp):
    def body(a_ref, o_ref):
        acc = a_ref[0:PACK_ROWS, :]
        for d in range(1, N_DEV):
            acc += a_ref[d * PACK_ROWS:(d + 1) * PACK_ROWS, :]
        o_ref[...] = acc

    return pl.pallas_call(
        body, name="sum_small", out_shape=jax.ShapeDtypeStruct((PACK_ROWS, PACK_COLS), F32), compiler_params=_params(),
    )(allp)


def _adamw(name, w, g, m, v):
    r, cc = w.shape
    tr = r
    for cand in (256, 128, 64, 32, 16, 8):
        if r % cand == 0:
            tr = cand
            break
    bc1 = 1.0 - ADAM_B1 ** ADAM_STEP
    bc2 = 1.0 - ADAM_B2 ** ADAM_STEP

    def body(w_ref, g_ref, m_ref, v_ref, d_ref, nm_ref, nv_ref):
        gv = g_ref[...]
        nm = ADAM_B1 * m_ref[...] + (1.0 - ADAM_B1) * gv
        nv = ADAM_B2 * v_ref[...] + (1.0 - ADAM_B2) * (gv * gv)
        nm_ref[...] = nm
        nv_ref[...] = nv
        d_ref[...] = -ADAM_LR * ((nm / bc1) / (jnp.sqrt(nv / bc2) + ADAM_EPS) + ADAM_WD * w_ref[...])

    spec = pl.BlockSpec((tr, cc), lambda i: (i, 0))
    return pl.pallas_call(
        body, name=name, grid=(r // tr,), in_specs=[spec] * 4, out_specs=[spec] * 3,
        out_shape=[jax.ShapeDtypeStruct((r, cc), F32)] * 3, compiler_params=_params(("arbitrary",)),
    )(w, g, m, v)


def _pack(fields, layout):
    parts = [fields[name].reshape(-1).astype(F32) if name in fields else jnp.zeros((n,), F32) for name, n in layout]
    used = sum(n for _, n in layout)
    parts.append(jnp.zeros((PACK_ROWS * PACK_COLS - used,), F32))
    return jnp.concatenate(parts).reshape(PACK_ROWS, PACK_COLS)


def _unpack(flat, layout):
    flat = flat.reshape(-1)
    out, o = {}, 0
    for name, n in layout:
        out[name] = flat[o:o + n]
        o += n
    return out


def kernel(x, c, w_cond, b_cond, w_in, b_in, ssm_lambda_re, ssm_lambda_im, ssm_log_dt, ssm_b_re, ssm_b_im, ssm_c_re, ssm_c_im, ssm_d, ssm_glu_w_a, ssm_glu_w_b, cv_dw_w, cv_dw_b, cv_ln_g, cv_ln_b, cv_w_pw, w_out, ln1_g, ln1_b, ffn_w_up, ffn_dw_w, ffn_dw_b, ffn_w_down, ln2_g, ln2_b, loss_target, m_w_cond, m_b_cond, m_w_in, m_b_in, m_ssm_lambda_re, m_ssm_lambda_im, m_ssm_log_dt, m_ssm_b_re, m_ssm_b_im, m_ssm_c_re, m_ssm_c_im, m_ssm_d, m_ssm_glu_w_a, m_ssm_glu_w_b, m_cv_dw_w, m_cv_dw_b, m_cv_ln_g, m_cv_ln_b, m_cv_w_pw, m_w_out, m_ln1_g, m_ln1_b, m_ffn_w_up, m_ffn_dw_w, m_ffn_dw_b, m_ffn_w_down, m_ln2_g, m_ln2_b, v_w_cond, v_b_cond, v_w_in, v_b_in, v_ssm_lambda_re, v_ssm_lambda_im, v_ssm_log_dt, v_ssm_b_re, v_ssm_b_im, v_ssm_c_re, v_ssm_c_im, v_ssm_d, v_ssm_glu_w_a, v_ssm_glu_w_b, v_cv_dw_w, v_cv_dw_b, v_cv_ln_g, v_cv_ln_b, v_cv_w_pw, v_w_out, v_ln1_g, v_ln1_b, v_ffn_w_up, v_ffn_dw_w, v_ffn_dw_b, v_ffn_w_down, v_ln2_g, v_ln2_b):
    given = locals()
    a = {n: given[n] for n in INPUTS}
    xi, yi, ci = lax.axis_index("x"), lax.axis_index("y"), lax.axis_index("c")
    s_me = 2 * xi + yi
    e_me = 4 * xi + 2 * yi + ci

    first = jnp.concatenate([
        jnp.concatenate([a["c"], jnp.zeros((7, D_MODEL), F32)], axis=0),
        jnp.concatenate([a["cv_dw_w"].reshape(-1), a["ffn_dw_w"].reshape(-1)]).reshape(8, D_MODEL)], axis=0)
    first_all = _allgather("gather_c", first).reshape(N_DEV, 16, D_MODEL)
    c_all = first_all[:, 0, :]
    dw_all = first_all[0::2, 8:, :].reshape(N_CHIP, 8 * D_MODEL)
    n_cv = CONV_KERNEL * CONV_WIDTH // N_CHIP
    cv_dw_full = dw_all[:, :n_cv].reshape(N_CHIP, CONV_KERNEL, CONV_WIDTH // N_CHIP).transpose(1, 0, 2) \
        .reshape(CONV_KERNEL, CONV_WIDTH)
    ffn_dw_full = dw_all[:, n_cv:].reshape(N_CHIP, FFN_KERNEL, 2 * FFN_HIDDEN // N_CHIP).transpose(1, 0, 2) \
        .reshape(FFN_KERNEL, 2 * FFN_HIDDEN)
    ncols = N_COND * D_MODEL // N_CHIP
    b_cond_shard = lax.dynamic_slice(a["b_cond"], (0, s_me * ncols), (1, ncols))
    c_act_all, modp = _cond_fwd(c_all, a["w_cond"][0], b_cond_shard)
    modp_all = _allgather("gather_mod", modp).reshape(N_DEV, N_DEV, ncols)[0::2]
    mod_e = lax.dynamic_index_in_dim(modp_all, e_me, axis=1, keepdims=False).reshape(N_COND, D_MODEL)
    modv = jnp.concatenate([mod_e, jnp.zeros((2, D_MODEL), F32)], axis=0)

    full, shards = _gather_weights([a[n][0] for n, _, _ in BIG])
    wb = dict(zip([n for n, _, _ in BIG], full))
    sp = {n: a[n][0] for n in ("b_in", "ssm_lambda_re", "ssm_lambda_im", "ssm_log_dt", "ssm_b_re", "ssm_b_im",
                               "ssm_c_re", "ssm_c_im", "ssm_d", "cv_dw_b", "cv_ln_g", "cv_ln_b", "ln1_g", "ln1_b",
                               "ffn_dw_b", "ln2_g", "ln2_b")}
    sp["cv_dw_w"] = cv_dw_full
    sp["ffn_dw_w"] = ffn_dw_full
    gx, dbig, direct_got, small = _local_step(a["x"][0], a["loss_target"][0], modv, wb, shards, sp)

    small["c_act"] = lax.dynamic_index_in_dim(c_act_all, e_me, axis=0, keepdims=False)
    packed_all = _allgather("gather_small", _pack(small, PACK))
    tot = _unpack(_sum_blocks(packed_all), PACK)
    rows = packed_all.reshape(N_DEV, PACK_ROWS * PACK_COLS)
    dmod_all = rows[:, 0:N_COND * D_MODEL]
    act_all = rows[:, N_COND * D_MODEL:(N_COND + 1) * D_MODEL]
    g_w_cond = _cond_bwd(act_all.T, lax.dynamic_slice(dmod_all, (0, s_me * ncols), (N_DEV, ncols)))

    glist = [dbig[m] for m in EARLY]
    halves = _rs1_sibling(glist)
    r2 = _rs2_chips(glist, halves)
    gsh = _rs3_finish(list(r2[:len(EARLY)]) + [dbig[m] for m in DIRECT], list(r2[len(EARLY):]) + direct_got)

    grads = {"w_cond": g_w_cond[None], "b_cond": tot["dmod"].reshape(1, -1)}
    for (n, kind, shape), g in zip(BIG, gsh):
        grads[n] = g.reshape(a[n].shape)
    for n in ("b_in", "ssm_lambda_re", "ssm_lambda_im", "ssm_log_dt", "ssm_b_re", "ssm_b_im", "ssm_c_re", "ssm_c_im",
              "ssm_d", "cv_dw_b", "cv_ln_g", "cv_ln_b", "ln1_g", "ln1_b", "ffn_dw_b", "ln2_g", "ln2_b"):
        grads[n] = tot[n].reshape(a[n].shape)
    wcv = CONV_WIDTH // N_CHIP
    grads["cv_dw_w"] = lax.dynamic_slice(tot["cv_dw_w"].reshape(CONV_KERNEL, CONV_WIDTH), (0, s_me * wcv),
                                         (CONV_KERNEL, wcv)).reshape(a["cv_dw_w"].shape)
    wff = 2 * FFN_HIDDEN // N_CHIP
    grads["ffn_dw_w"] = lax.dynamic_slice(tot["ffn_dw_w"].reshape(FFN_KERNEL, 2 * FFN_HIDDEN), (0, s_me * wff),
                                          (FFN_KERNEL, wff)).reshape(a["ffn_dw_w"].shape)

    delta, new_m, new_v = {}, {}, {}
    for n in ["w_cond"] + [n for n, _, _ in BIG]:
        d, nm_, nv_ = _adamw("adamw_" + n, a[n][0], grads[n][0], a["m_" + n][0], a["v_" + n][0])
        delta[n], new_m[n], new_v[n] = d[None], nm_[None], nv_[None]
    upd = [n for n, _ in SMALL_UPD]
    d, nm_, nv_ = _adamw("adamw_small", _pack({n: a[n] for n in upd}, SMALL_UPD), _pack({n: grads[n] for n in upd}, SMALL_UPD),
                         _pack({n: a["m_" + n] for n in upd}, SMALL_UPD), _pack({n: a["v_" + n] for n in upd}, SMALL_UPD))
    for dst, flat in ((delta, d), (new_m, nm_), (new_v, nv_)):
        for n, val in _unpack(flat, SMALL_UPD).items():
            dst[n] = val.reshape(a[n].shape)

    loss = tot["loss"].reshape(())
    return (loss, gx[None], *[grads[n] for n in WEIGHTS], *[delta[n] for n in WEIGHTS],
            *[new_m[n] for n in WEIGHTS], *[new_v[n] for n in WEIGHTS])
```

```python
import functools
import math

import jax
import jax.numpy as jnp
from jax import lax
from jax.experimental import pallas as pl
from jax.experimental.pallas import tpu as pltpu

F32 = jnp.float32
BF16 = jnp.bfloat16

D_MODEL = 1024
SSM_WIDTH = 512
SSM_GROUP = 16
SSM_GROUPS = 32
SSM_STATE = 64
CONV_WIDTH = 512
CONV_KERNEL = 31
FFN_HIDDEN = 2816
FFN_KERNEL = 3
IN_PROJ_WIDTH = 3584
N_COND = 6
ALPHA = 2.0 ** 0.25
LN_EPS = 1e-5
ADAM_LR, ADAM_B1, ADAM_B2, ADAM_EPS, ADAM_WD, ADAM_STEP = 0.001, 0.9, 0.999, 1e-08, 0.01, 10

N_DEV = 8
N_CHIP = 4
LANES = 128
SSM_CHUNK = 16
LANE_GROUPS = LANES // SSM_GROUP
N_LANE_BLOCKS = SSM_WIDTH // LANES
STATE_COLS = LANE_GROUPS * SSM_STATE
CHUNK_COLS = SSM_CHUNK * LANES
CONV_HALO = 32
VMEM_LIMIT = 56 * 1024 * 1024
MESH = pl.DeviceIdType.MESH

BIG = (
    ("w_in", "col", (D_MODEL, IN_PROJ_WIDTH)),
    ("ssm_glu_w_a", "col", (SSM_WIDTH, D_MODEL)),
    ("ssm_glu_w_b", "col", (SSM_WIDTH, D_MODEL)),
    ("cv_w_pw", "col", (CONV_WIDTH, D_MODEL)),
    ("w_out", "row", (D_MODEL, D_MODEL)),
    ("ffn_w_up", "col", (D_MODEL, 2 * FFN_HIDDEN)),
    ("ffn_w_down", "row", (FFN_HIDDEN, D_MODEL)),
)

EARLY = (0,)
MID = (1, 2, 3, 4)
LATE = (5, 6)
DIRECT = MID + LATE

WEIGHTS = ['w_cond', 'b_cond', 'w_in', 'b_in', 'ssm_lambda_re', 'ssm_lambda_im', 'ssm_log_dt', 'ssm_b_re', 'ssm_b_im',
           'ssm_c_re', 'ssm_c_im', 'ssm_d', 'ssm_glu_w_a', 'ssm_glu_w_b', 'cv_dw_w', 'cv_dw_b', 'cv_ln_g', 'cv_ln_b',
           'cv_w_pw', 'w_out', 'ln1_g', 'ln1_b', 'ffn_w_up', 'ffn_dw_w', 'ffn_dw_b', 'ffn_w_down', 'ln2_g', 'ln2_b']
INPUTS = ['x', 'c'] + WEIGHTS + ['loss_target'] + ['m_' + n for n in WEIGHTS] + ['v_' + n for n in WEIGHTS]

PACK = (
    ("dmod", N_COND * D_MODEL), ("c_act", D_MODEL), ("b_in", IN_PROJ_WIDTH),
    ("ssm_lambda_re", SSM_GROUPS * SSM_STATE), ("ssm_lambda_im", SSM_GROUPS * SSM_STATE), ("ssm_log_dt", SSM_GROUPS),
    ("ssm_b_re", SSM_GROUPS * SSM_STATE * SSM_GROUP), ("ssm_b_im", SSM_GROUPS * SSM_STATE * SSM_GROUP),
    ("ssm_c_re", SSM_GROUPS * SSM_STATE * SSM_GROUP), ("ssm_c_im", SSM_GROUPS * SSM_STATE * SSM_GROUP),
    ("ssm_d", SSM_GROUPS * SSM_GROUP), ("cv_dw_w", CONV_KERNEL * CONV_WIDTH), ("cv_dw_b", CONV_WIDTH),
    ("cv_ln_g", CONV_WIDTH), ("cv_ln_b", CONV_WIDTH), ("ln1_g", D_MODEL), ("ln1_b", D_MODEL),
    ("ffn_dw_w", FFN_KERNEL * 2 * FFN_HIDDEN), ("ffn_dw_b", 2 * FFN_HIDDEN), ("ln2_g", D_MODEL), ("ln2_b", D_MODEL),
    ("loss", 1),
)
PACK_COLS = 1024
PACK_ROWS = 192
assert sum(n for _, n in PACK) <= PACK_ROWS * PACK_COLS

SMALL_UPD = (
    ("b_cond", N_COND * D_MODEL), ("b_in", IN_PROJ_WIDTH),
    ("ssm_lambda_re", SSM_GROUPS * SSM_STATE), ("ssm_lambda_im", SSM_GROUPS * SSM_STATE), ("ssm_log_dt", SSM_GROUPS),
    ("ssm_b_re", SSM_GROUPS * SSM_STATE * SSM_GROUP), ("ssm_b_im", SSM_GROUPS * SSM_STATE * SSM_GROUP),
    ("ssm_c_re", SSM_GROUPS * SSM_STATE * SSM_GROUP), ("ssm_c_im", SSM_GROUPS * SSM_STATE * SSM_GROUP),
    ("ssm_d", SSM_GROUPS * SSM_GROUP), ("cv_dw_w", CONV_KERNEL * CONV_WIDTH // N_CHIP), ("cv_dw_b", CONV_WIDTH),
    ("cv_ln_g", CONV_WIDTH), ("cv_ln_b", CONV_WIDTH), ("ln1_g", D_MODEL), ("ln1_b", D_MODEL),
    ("ffn_dw_w", FFN_KERNEL * 2 * FFN_HIDDEN // N_CHIP), ("ffn_dw_b", 2 * FFN_HIDDEN), ("ln2_g", D_MODEL),
    ("ln2_b", D_MODEL),
)
assert sum(n for _, n in SMALL_UPD) <= PACK_ROWS * PACK_COLS


def _params(sem=None, **kw):
    return pltpu.CompilerParams(dimension_semantics=sem, vmem_limit_bytes=VMEM_LIMIT, **kw)


def _ln_stats(x):
    mu = jnp.mean(x, axis=-1, keepdims=True)
    xc = x - mu
    var = jnp.mean(xc * xc, axis=-1, keepdims=True)
    rstd = lax.rsqrt(var + LN_EPS)
    return xc * rstd, rstd


def _ln_bwd(dxhat, xhat, rstd):
    m1 = jnp.mean(dxhat, axis=-1, keepdims=True)
    m2 = jnp.mean(dxhat * xhat, axis=-1, keepdims=True)
    return rstd * (dxhat - m1 - xhat * m2)


def _sig(x):
    return 1.0 / (1.0 + jnp.exp(-x))


def _gelu(x):
    return 0.5 * x * (1.0 + lax.erf(x * (1.0 / math.sqrt(2.0))))


def _dgelu(x):
    return 0.5 * (1.0 + lax.erf(x * (1.0 / math.sqrt(2.0)))) + x * jnp.exp(-0.5 * x * x) * (1.0 / math.sqrt(2.0 * math.pi))


def _gelu_and_grad(x):
    er = lax.erf(x * (1.0 / math.sqrt(2.0)))
    cdf = 0.5 * (1.0 + er)
    return x * cdf, cdf + x * jnp.exp(-0.5 * x * x) * (1.0 / math.sqrt(2.0 * math.pi))


def _colsum(a):
    return jnp.sum(a, axis=0, keepdims=True)


def _fill_rotations(buf, rot, rows):
    for r in range(1, 8):
        rot[r - 1] = buf[pl.ds(r, rows), :]


def _rows_at(buf, rot, offset, tb):
    q, r = divmod(offset, 8)
    if r == 0:
        return buf[pl.ds(8 * q, tb), :]
    return rot[r - 1, pl.ds(8 * q, tb), :]


def _dot(a, b):
    return jnp.dot(a, b, preferred_element_type=F32)


def _dot_nt(a, b):
    return lax.dot_general(a, b, (((1,), (1,)), ((), ())), preferred_element_type=F32)


def _dot_tn(a, b):
    return lax.dot_general(a, b, (((0,), (0,)), ((), ())), preferred_element_type=F32)


def _load_once(src, dst, sem):
    cp = pltpu.make_async_copy(src, dst, sem)
    cp.start()
    cp.wait()


def _full(a):
    nd = a.ndim
    return pl.BlockSpec(a.shape, lambda *_: (0,) * nd)


ANY = pl.BlockSpec(memory_space=pl.ANY)


def _place():
    x, y, c = lax.axis_index("x"), lax.axis_index("y"), lax.axis_index("c")
    chips = [(1 - x, y), (x, 1 - y), (1 - x, 1 - y)]
    return x, y, c, chips


def _piece(kind, shape):
    r, cc = shape
    return (r // 2, cc // N_CHIP) if kind == "col" else (r // (2 * N_CHIP), cc)


def _piece_at(ref, kind, shape, s, k):
    pr, pc = _piece(kind, shape)
    if kind == "col":
        return ref.at[pl.ds(k * pr, pr), pl.ds(pl.multiple_of(s * pc, LANES), pc)]
    return ref.at[pl.ds(pl.multiple_of((2 * s + k) * pr, 16), pr), :]


def _gather_start(idx, sh, full, send, recv):
    x, y, c, chips = _place()
    for i, m in enumerate(idx):
        _, kind, shape = BIG[m]
        pr, _ = _piece(kind, shape)
        for j, chip in enumerate(chips):
            pltpu.make_async_remote_copy(
                src_ref=sh[i].at[pl.ds(pl.multiple_of(c * pr, 16), pr), :], dst_ref=_piece_at(full[i], kind, shape, 2 * x + y, c),
                send_sem=send.at[i, j], recv_sem=recv.at[i, j], device_id=(*chip, c), device_id_type=MESH).start()


def _gather_finish(idx, sh, full, send, recv, fsend, frecv):
    x, y, c, chips = _place()
    sibling = (x, y, 1 - c)
    waits = []
    for i, m in enumerate(idx):
        _, kind, shape = BIG[m]
        pr, _ = _piece(kind, shape)
        for j, (cx, cy) in enumerate(chips):
            got = _piece_at(full[i], kind, shape, 2 * cx + cy, c)
            first = pltpu.make_async_remote_copy(
                src_ref=sh[i].at[pl.ds(pl.multiple_of(c * pr, 16), pr), :], dst_ref=got, send_sem=send.at[i, j],
                recv_sem=recv.at[i, j], device_id=(cx, cy, c), device_id_type=MESH)
            first.wait_recv()
            fwd = pltpu.make_async_remote_copy(src_ref=got, dst_ref=got, send_sem=fsend.at[i, j], recv_sem=frecv.at[i, j],
                                               device_id=sibling, device_id_type=MESH)
            fwd.start()
            waits += [first.wait_send, fwd.wait_send]
    for i, m in enumerate(idx):
        _, kind, shape = BIG[m]
        for j, (cx, cy) in enumerate(chips):
            got = _piece_at(full[i], kind, shape, 2 * cx + cy, 1 - c)
            pltpu.make_async_remote_copy(src_ref=got, dst_ref=got, send_sem=fsend.at[i, j], recv_sem=frecv.at[i, j],
                                         device_id=sibling, device_id_type=MESH).wait_recv()
    for w in waits:
        w()


def _scatter(idx, dw, got, send, recv):
    x, y, c, _ = _place()
    cps = []
    for i, m in enumerate(idx):
        _, kind, shape = BIG[m]
        for r in range(1, N_DEV):
            tx, ty, tc = (1 - x if r & 4 else x), (1 - y if r & 2 else y), (1 - c if r & 1 else c)
            cps.append(pltpu.make_async_remote_copy(
                src_ref=_piece_at(dw[i], kind, shape, 2 * tx + ty, tc), dst_ref=got[i].at[r - 1],
                send_sem=send.at[i, r - 1], recv_sem=recv.at[i, r - 1], device_id=(tx, ty, tc), device_id_type=MESH))
    return cps


def _f1_inproj(x, modv, b_in, w_in, mid_sh, mid_full, tb):
    t = x.shape[0]
    nt = t // tb
    nl = len(MID)
    chunks = [(j * 512, 512) for j in range(IN_PROJ_WIDTH // 512)]

    def body(x_ref, modv_ref, b_ref, w_hbm, *rest):
        sh, full = rest[:nl], rest[2 * nl:3 * nl]
        u4_ref, prest_ref, h_ref, w_v, sem, send, recv, fsend, frecv = rest[3 * nl:]

        @pl.when(pl.program_id(0) == 0)
        def _():
            _gather_start(MID, sh, full, send, recv)
            _load_once(w_hbm, w_v, sem)

        xn, _ = _ln_stats(x_ref[...])
        h = (xn * (1.0 + modv_ref[1:2, :]) + modv_ref[0:1, :]).astype(BF16)
        h_ref[...] = h
        for c0, cw in chunks:
            p = _dot(h, w_v[:, c0:c0 + cw]) + b_ref[:, c0:c0 + cw]
            if c0 == 0:
                for b in range(N_LANE_BLOCKS):
                    u4_ref[b] = p[:, b * LANES:(b + 1) * LANES]
            else:
                prest_ref[:, c0 - SSM_WIDTH:c0 - SSM_WIDTH + cw] = p

        @pl.when(pl.program_id(0) == nt - 1)
        def _():
            _gather_finish(MID, sh, full, send, recv, fsend, frecv)

    gsem = pltpu.SemaphoreType.DMA((nl, 3))
    return pl.pallas_call(
        body, name="f1_inproj", grid=(nt,),
        in_specs=[pl.BlockSpec((tb, D_MODEL), lambda i: (i, 0)), _full(modv), _full(b_in), ANY] + [ANY] * (2 * nl),
        out_specs=[ANY] * nl + [pl.BlockSpec((N_LANE_BLOCKS, tb, LANES), lambda i: (0, i, 0)),
                                pl.BlockSpec((tb, IN_PROJ_WIDTH - SSM_WIDTH), lambda i: (i, 0)),
                                pl.BlockSpec((tb, D_MODEL), lambda i: (i, 0))],
        input_output_aliases={4 + nl + k: k for k in range(nl)},
        out_shape=[jax.ShapeDtypeStruct(f.shape, f.dtype) for f in mid_full]
        + [jax.ShapeDtypeStruct((N_LANE_BLOCKS, t, LANES), F32),
                   jax.ShapeDtypeStruct((t, IN_PROJ_WIDTH - SSM_WIDTH), F32),
                   jax.ShapeDtypeStruct((t, D_MODEL), BF16)],
        scratch_shapes=[pltpu.VMEM(w_in.shape, BF16), pltpu.SemaphoreType.DMA, gsem, gsem, gsem, gsem],
        compiler_params=_params(("arbitrary",)),
    )(x, modv, b_in, w_in, *mid_sh, *mid_full)


def _s5_build(lam_re, lam_im, log_dt, b_re, b_im, c_re, c_im, d):
    el, g, n, p, nb = SSM_CHUNK, SSM_GROUPS, SSM_STATE, SSM_GROUP, N_LANE_BLOCKS
    lr = jnp.minimum(lam_re, -1e-4)
    li = lam_im
    dt = jnp.exp(log_dt)[:, None]
    mag = jnp.exp(lr * dt)
    ang = li * dt
    lbr, lbi = mag * jnp.cos(ang), mag * jnp.sin(ang)
    num_r, num_i = lbr - 1.0, lbi
    den = lr * lr + li * li
    coef_r = (num_r * lr + num_i * li) / den
    coef_i = (num_i * lr - num_r * li) / den
    bbar_r = coef_r[..., None] * b_re - coef_i[..., None] * b_im
    bbar_i = coef_r[..., None] * b_im + coef_i[..., None] * b_re
    k = jnp.arange(el + 1, dtype=F32)[:, None, None]
    pmag = jnp.exp(k * (lr * dt)[None])
    pr, pi = pmag * jnp.cos(k * ang[None]), pmag * jnp.sin(k * ang[None])
    car = c_re[None] * pr[:, :, None, :] - c_im[None] * pi[:, :, None, :]
    cai = c_re[None] * pi[:, :, None, :] + c_im[None] * pr[:, :, None, :]
    bt_r = bbar_r.transpose(0, 2, 1)[None]
    bt_i = bbar_i.transpose(0, 2, 1)[None]
    kern = jnp.sum(car[:el, :, None, :, :] * bt_r[:, :, :, None, :] - cai[:el, :, None, :, :] * bt_i[:, :, :, None, :],
                   axis=-1)
    kern = kern.at[0].add(jnp.eye(p, dtype=F32)[None] * d[:, None, :])
    kc = kern.reshape(el, g * p, p)
    rev = el - 1 - jnp.arange(el)
    qr, qi = pr[rev][:, :, None, :], pi[rev][:, :, None, :]
    sw_r = (qr * bt_r - qi * bt_i).reshape(el, g * p, n)
    sw_i = (qr * bt_i + qi * bt_r).reshape(el, g * p, n)
    sg_r = car[1:].reshape(el, g * p, n)
    sg_i = (-cai[1:]).reshape(el, g * p, n)
    a = jnp.stack([pr[el].reshape(nb, LANE_GROUPS * n), pi[el].reshape(nb, LANE_GROUPS * n)], axis=1)
    return kc, sw_r, sw_i, sg_r, sg_i, a


def _expand(src, reps):
    rows, w = src.shape
    cols = reps * w
    r = lax.broadcasted_iota(jnp.int32, (w, cols), 0)
    c = lax.broadcasted_iota(jnp.int32, (w, cols), 1)
    rep = (r == (c & (w - 1))).astype(BF16)
    out = _dot(src.astype(BF16), rep)
    rg = lax.broadcasted_iota(jnp.int32, (rows, cols), 0) // SSM_GROUP
    cg = lax.broadcasted_iota(jnp.int32, (rows, cols), 1) // w
    return jnp.where(rg == cg, out, 0.0).astype(BF16)


def _fold(x, w):
    rows, cols = x.shape
    rg = lax.broadcasted_iota(jnp.int32, (rows, cols), 0) // SSM_GROUP
    cg = lax.broadcasted_iota(jnp.int32, (rows, cols), 1) // w
    x = jnp.where(rg == cg, x, 0.0)
    while cols > LANES:
        x = x[:, :cols // 2] + x[:, cols // 2:]
        cols //= 2
    s = LANES // 2
    while s >= w:
        x = x + pltpu.roll(x, s, axis=1)
        s //= 2
    return x[:, :w]


def _build_maps(s_ref, dst):
    for j in range(SSM_CHUNK):
        dst[j * LANES:(j + 1) * LANES, :] = _expand(s_ref[j], LANE_GROUPS)


def _build_toeplitz(kc_ref, dst):
    dst[...] = jnp.zeros_like(dst)
    for d in range(SSM_CHUNK):
        blk = _expand(kc_ref[d], LANE_GROUPS)
        for ji in range(SSM_CHUNK - d):
            jo = ji + d
            dst[ji * LANES:(ji + 1) * LANES, jo * LANES:(jo + 1) * LANES] = blk


def _cblk(w):
    return pl.BlockSpec((SSM_CHUNK, LANES, w), lambda b: (0, b, 0))


def _tblk(t):
    return pl.BlockSpec((1, t, LANES), lambda b: (b, 0, 0))


def _load_chunks(ref, nc):
    return jnp.concatenate([ref[0, pl.ds(j, nc, stride=SSM_CHUNK), :] for j in range(SSM_CHUNK)], axis=-1).astype(BF16)


def _store_chunks(ref, val, nc):
    for j in range(SSM_CHUNK):
        ref[0, pl.ds(j, nc, stride=SSM_CHUNK), :] = val[:, j * LANES:(j + 1) * LANES]


def _s5a_state(u4, sw_r, sw_i, a8):
    nb, t, _ = u4.shape
    nc = t // SSM_CHUNK
    sc = STATE_COLS

    def body(u_ref, swr_ref, swi_ref, a_ref, hr_ref, hi_ref, w_s, xr_s, xi_s):
        u = _load_chunks(u_ref, nc)
        _build_maps(swr_ref, w_s)
        xr_s[...] = _dot(u, w_s[...])
        _build_maps(swi_ref, w_s)
        xi_s[...] = _dot(u, w_s[...])
        ar = a_ref[0, 0:1, :]
        ai = a_ref[0, 1:2, :]

        def step(c, carry):
            hr, hi = carry
            hr_ref[0, pl.ds(c, 1), :] = hr
            hi_ref[0, pl.ds(c, 1), :] = hi
            xr = xr_s[pl.ds(c, 1), :]
            xi = xi_s[pl.ds(c, 1), :]
            return ar * hr - ai * hi + xr, ar * hi + ai * hr + xi

        z = jnp.zeros((1, sc), F32)
        lax.fori_loop(0, nc, step, (z, z))

    return pl.pallas_call(
        body, name="s5a_state", grid=(nb,),
        in_specs=[_tblk(t), _cblk(SSM_STATE), _cblk(SSM_STATE),
                  pl.BlockSpec((1, 8, sc), lambda b: (b, 0, 0))],
        out_specs=[pl.BlockSpec((1, nc, sc), lambda b: (b, 0, 0))] * 2,
        out_shape=[jax.ShapeDtypeStruct((nb, nc, sc), F32)] * 2,
        scratch_shapes=[pltpu.VMEM((CHUNK_COLS, sc), BF16), pltpu.VMEM((nc, sc), F32), pltpu.VMEM((nc, sc), F32)],
        compiler_params=_params(("arbitrary",)),
    )(u4, sw_r, sw_i, a8)


def _s5b_out(u4, kc, sg_r, sg_i, hr, hi):
    nb, t, _ = u4.shape
    nc = t // SSM_CHUNK
    sc = STATE_COLS
    cw = 512

    def body(u_ref, kc_ref, sgr_ref, sgi_ref, hr_ref, hi_ref, y_ref, tm_s, gr_s, gi_s):
        _build_toeplitz(kc_ref, tm_s)
        _build_maps(sgr_ref, gr_s)
        _build_maps(sgi_ref, gi_s)
        u = _load_chunks(u_ref, nc)
        h_r = hr_ref[0].astype(BF16)
        h_i = hi_ref[0].astype(BF16)
        for j in range(CHUNK_COLS // cw):
            cs = slice(j * cw, (j + 1) * cw)
            y = _dot(u, tm_s[:, cs]) + _dot_nt(h_r, gr_s[cs, :]) + _dot_nt(h_i, gi_s[cs, :])
            for q in range(cw // LANES):
                step = j * (cw // LANES) + q
                y_ref[0, pl.ds(step, nc, stride=SSM_CHUNK), :] = y[:, q * LANES:(q + 1) * LANES]

    return pl.pallas_call(
        body, name="s5b_out", grid=(nb,),
        in_specs=[_tblk(t), _cblk(SSM_GROUP), _cblk(SSM_STATE),
                  _cblk(SSM_STATE), pl.BlockSpec((1, nc, sc), lambda b: (b, 0, 0)),
                  pl.BlockSpec((1, nc, sc), lambda b: (b, 0, 0))],
        out_specs=_tblk(t),
        out_shape=jax.ShapeDtypeStruct((nb, t, LANES), F32),
        scratch_shapes=[pltpu.VMEM((CHUNK_COLS, CHUNK_COLS), BF16), pltpu.VMEM((CHUNK_COLS, sc), BF16),
                        pltpu.VMEM((CHUNK_COLS, sc), BF16)],
        compiler_params=_params(("arbitrary",)),
    )(u4, kc, sg_r, sg_i, hr, hi)


def _f4_mixer(ys4, prest, x, modv, cvv, cw32, w_a, w_b, w_pw, w_out, late_sh, late_full, tb):
    t = x.shape[0]
    hb = tb // CONV_HALO
    nt = t // tb
    nl = len(LATE)

    def body(ys_ref, pr_ref, halo_ref, x_ref, modv_ref, cvv_ref, cw_ref, wa_ref, wb_ref, wpw_ref, wout_ref, *rest):
        sh, full = rest[:nl], rest[2 * nl:3 * nl]
        r1_ref, ya_ref, yb_ref, ycv_ref, vc_ref, yg_ref, vs_ref, mg_ref, vbuf, vrot, send, recv, fsend, frecv = rest[3 * nl:]
        i = pl.program_id(0)

        @pl.when(i == 0)
        def _():
            _gather_start(LATE, sh, full, send, recv)

        ys = jnp.concatenate([ys_ref[b] for b in range(N_LANE_BLOCKS)], axis=-1)
        yg = _gelu(ys).astype(BF16)
        yg_ref[...] = yg
        ya = _dot(yg, wa_ref[...])
        yb = _dot(yg, wb_ref[...])
        ya_ref[...] = ya.astype(BF16)
        yb_ref[...] = yb.astype(BF16)
        yssm = ya * _sig(yb)
        hv = halo_ref[:, 0:CONV_WIDTH] * _sig(halo_ref[:, CONV_WIDTH:2 * CONV_WIDTH])
        vbuf[0:CONV_HALO, :] = jnp.where(i == 0, 0.0, hv)
        vbuf[CONV_HALO:, :] = pr_ref[:, 0:CONV_WIDTH] * _sig(pr_ref[:, CONV_WIDTH:2 * CONV_WIDTH])
        _fill_rotations(vbuf, vrot, tb + CONV_HALO - 8)
        acc = jnp.zeros((tb, CONV_WIDTH), F32)
        for k in range(CONV_KERNEL):
            acc += _rows_at(vbuf, vrot, CONV_HALO - CONV_KERNEL + 1 + k, tb) * cw_ref[k:k + 1, :]
        vc = acc + cvv_ref[0:1, :]
        vc_ref[...] = vc
        xh, _ = _ln_stats(vc)
        vl = xh * cvv_ref[1:2, :] + cvv_ref[2:3, :]
        vs = (vl * _sig(vl)).astype(BF16)
        vs_ref[...] = vs
        ycv = _dot(vs, wpw_ref[...])
        ycv_ref[...] = ycv.astype(BF16)
        gs = pr_ref[:, 2 * CONV_WIDTH:2 * CONV_WIDTH + D_MODEL]
        gc = pr_ref[:, 2 * CONV_WIDTH + D_MODEL:]
        merged = (_sig(gs) * yssm + _sig(gc) * ycv).astype(BF16)
        mg_ref[...] = merged
        ym = _dot(merged, wout_ref[...])
        r1_ref[...] = ALPHA * x_ref[...] + modv_ref[2:3, :] * ym

        @pl.when(i == nt - 1)
        def _():
            _gather_finish(LATE, sh, full, send, recv, fsend, frecv)

    tok = lambda w: pl.BlockSpec((tb, w), lambda i: (i, 0))
    sem = pltpu.SemaphoreType.DMA((nl, 3))
    n_in = 11
    return pl.pallas_call(
        body, name="f4_mixer", grid=(nt,),
        in_specs=[pl.BlockSpec((N_LANE_BLOCKS, tb, LANES), lambda i: (0, i, 0)), tok(prest.shape[1]),
                  pl.BlockSpec((CONV_HALO, 2 * CONV_WIDTH), lambda i: (jnp.maximum(i * hb - 1, 0), 0)),
                  tok(D_MODEL), _full(modv), _full(cvv), _full(cw32), _full(w_a), _full(w_b), _full(w_pw), _full(w_out)]
        + [ANY] * (2 * nl),
        out_specs=[ANY] * nl + [tok(D_MODEL), tok(D_MODEL), tok(D_MODEL), tok(D_MODEL), tok(CONV_WIDTH), tok(SSM_WIDTH),
                                tok(CONV_WIDTH), tok(D_MODEL)],
        input_output_aliases={n_in + nl + k: k for k in range(nl)},
        out_shape=[jax.ShapeDtypeStruct(f.shape, f.dtype) for f in late_full]
        + [jax.ShapeDtypeStruct((t, D_MODEL), F32), jax.ShapeDtypeStruct((t, D_MODEL), BF16),
                   jax.ShapeDtypeStruct((t, D_MODEL), BF16), jax.ShapeDtypeStruct((t, D_MODEL), BF16),
                   jax.ShapeDtypeStruct((t, CONV_WIDTH), F32), jax.ShapeDtypeStruct((t, SSM_WIDTH), BF16),
                   jax.ShapeDtypeStruct((t, CONV_WIDTH), BF16), jax.ShapeDtypeStruct((t, D_MODEL), BF16)],
        scratch_shapes=[pltpu.VMEM((tb + CONV_HALO, CONV_WIDTH), F32),
                        pltpu.VMEM((7, tb + CONV_HALO - 8, CONV_WIDTH), F32), sem, sem, sem, sem],
        compiler_params=_params(("arbitrary",)),
    )(ys4, prest, prest, x, modv, cvv, cw32, w_a, w_b, w_pw, w_out, *late_sh, *late_full)


FFN_COLS = 1408


def _f5_ffn(r1, tgt, modv, lnv, fdw, w_up, w_down, tb):
    t = r1.shape[0]
    fw = 2 * FFN_HIDDEN

    def body(r1_ref, tgt_ref, modv_ref, lnv_ref, fdw_ref, wup_hbm, wdn_hbm,
             dr2_ref, d_ref, up_ref, z_ref, acc_ref, wup_v, wdn_v, upbuf, gbuf, hbuf, sems):
        i = pl.program_id(0)

        @pl.when(i == 0)
        def _():
            _load_once(wup_hbm, wup_v, sems.at[0])
            _load_once(wdn_hbm, wdn_v, sems.at[1])
            acc_ref[...] = jnp.zeros_like(acc_ref)
            upbuf[0:8, :] = jnp.zeros((8, fw), F32)

        xh1, _ = _ln_stats(r1_ref[...])
        x1 = xh1 * lnv_ref[0:1, :] + lnv_ref[1:2, :]
        xn2, _ = _ln_stats(x1)
        h2 = (xn2 * (1.0 + modv_ref[4:5, :]) + modv_ref[3:4, :]).astype(BF16)
        for j in range(fw // FFN_COLS):
            cs = slice(j * FFN_COLS, (j + 1) * FFN_COLS)
            up = _dot(h2, wup_v[:, cs])
            upbuf[8:, cs] = up
            up_ref[:, cs] = up.astype(BF16)

        def conv(cs):
            return (fdw_ref[0:1, cs] * upbuf[pl.ds(6, tb), cs] + fdw_ref[1:2, cs] * upbuf[pl.ds(7, tb), cs]
                    + fdw_ref[2:3, cs] * upbuf[pl.ds(8, tb), cs] + fdw_ref[3:4, cs])

        halves = [(slice(j * FFN_COLS, (j + 1) * FFN_COLS),
                   slice(FFN_HIDDEN + j * FFN_COLS, FFN_HIDDEN + (j + 1) * FFN_COLS)) for j in range(FFN_HIDDEN // FFN_COLS)]
        yf = jnp.zeros((tb, D_MODEL), F32)
        for ca, cv in halves:
            v = conv(cv)
            g, dg = _gelu_and_grad(conv(ca))
            gbuf[:, ca] = g.astype(BF16)
            hbuf[:, ca] = (v * dg).astype(BF16)
            z = (g * v).astype(BF16)
            z_ref[:, ca] = z
            yf += _dot(z, wdn_v[ca, :])
        r2 = ALPHA * x1 + modv_ref[5:6, :] * yf
        xh2, rstd2 = _ln_stats(r2)
        e = xh2 * lnv_ref[2:3, :] + lnv_ref[3:4, :] - tgt_ref[...]
        dx2 = e * (1.0 / D_MODEL)
        acc_ref[3:4, :] += _colsum(e * e) * (0.5 / D_MODEL)
        acc_ref[0:1, :] += _colsum(dx2 * xh2)
        acc_ref[1:2, :] += _colsum(dx2)
        dr2 = _ln_bwd(dx2 * lnv_ref[2:3, :], xh2, rstd2)
        dr2_ref[...] = dr2
        acc_ref[2:3, :] += _colsum(dr2 * yf)
        dyf = (modv_ref[5:6, :] * dr2).astype(BF16)
        for ca, cv in halves:
            dz = _dot_nt(dyf, wdn_v[ca, :])
            d_ref[:, ca] = (dz * hbuf[:, ca].astype(F32)).astype(BF16)
            d_ref[:, cv] = (dz * gbuf[:, ca].astype(F32)).astype(BF16)
        upbuf[0:8, :] = upbuf[pl.ds(tb, 8), :]

    tok = lambda w: pl.BlockSpec((tb, w), lambda i: (i, 0))
    return pl.pallas_call(
        body, name="f5_ffn", grid=(t // tb,),
        in_specs=[tok(D_MODEL), tok(D_MODEL), _full(modv), _full(lnv), _full(fdw), ANY, ANY],
        out_specs=[tok(D_MODEL), tok(fw), tok(fw), tok(FFN_HIDDEN), pl.BlockSpec((8, D_MODEL), lambda i: (0, 0))],
        out_shape=[jax.ShapeDtypeStruct((t, D_MODEL), F32), jax.ShapeDtypeStruct((t, fw), BF16),
                   jax.ShapeDtypeStruct((t, fw), BF16), jax.ShapeDtypeStruct((t, FFN_HIDDEN), BF16),
                   jax.ShapeDtypeStruct((8, D_MODEL), F32)],
        scratch_shapes=[pltpu.VMEM(w_up.shape, BF16), pltpu.VMEM(w_down.shape, BF16),
                        pltpu.VMEM((tb + 8, fw), F32), pltpu.VMEM((tb, FFN_HIDDEN), BF16),
                        pltpu.VMEM((tb, FFN_HIDDEN), BF16), pltpu.SemaphoreType.DMA((2,))],
        compiler_params=_params(("arbitrary",)),
    )(r1, tgt, modv, lnv, fdw, w_up, w_down)


def _b1b_ffn_up(d, up, dr2, r1, modv, lnv, fdw, w_up, tb):
    t = dr2.shape[0]
    fw = 2 * FFN_HIDDEN
    nt = t // tb
    hb = tb // 16

    def body(d_ref, nxt_ref, up_ref, dr2_ref, r1_ref, modv_ref, lnv_ref, fdw_ref, wup_hbm, dup_ref, dr1_ref, h2_ref,
             dyf_ref, acc_ref, accw_ref, wup_v, dbuf, shifted, sem):
        i = pl.program_id(0)

        @pl.when(i == 0)
        def _():
            _load_once(wup_hbm, wup_v, sem)
            acc_ref[...] = jnp.zeros_like(acc_ref)
            accw_ref[...] = jnp.zeros_like(accw_ref)

        dbuf[0:tb, :] = d_ref[...].astype(F32)
        dbuf[tb:, :] = jnp.where(i == nt - 1, 0.0, nxt_ref[...].astype(F32))
        dh2 = jnp.zeros((tb, D_MODEL), F32)
        for j in range(fw // FFN_COLS):
            cs = slice(j * FFN_COLS, (j + 1) * FFN_COLS)
            for k in range(1, FFN_KERNEL):
                shifted[k - 1] = dbuf[pl.ds(k, tb), cs]
            ds = [dbuf[pl.ds(0, tb), cs], shifted[0], shifted[1]]
            dup = (fdw_ref[2:3, cs] * ds[0] + fdw_ref[1:2, cs] * ds[1] + fdw_ref[0:1, cs] * ds[2]).astype(BF16)
            dup_ref[:, cs] = dup
            dh2 += _dot_nt(dup, wup_v[:, cs])
            upf = up_ref[:, cs].astype(F32)
            for k in range(FFN_KERNEL):
                accw_ref[k:k + 1, cs] += _colsum(ds[FFN_KERNEL - 1 - k] * upf)
            accw_ref[3:4, cs] += _colsum(ds[0])
        xh1, rstd1 = _ln_stats(r1_ref[...])
        x1 = xh1 * lnv_ref[0:1, :] + lnv_ref[1:2, :]
        xn2, rstd2 = _ln_stats(x1)
        h2_ref[...] = (xn2 * (1.0 + modv_ref[4:5, :]) + modv_ref[3:4, :]).astype(BF16)
        dr2 = dr2_ref[...]
        dyf_ref[...] = (modv_ref[5:6, :] * dr2).astype(BF16)
        acc_ref[0:1, :] += _colsum(dh2 * xn2)
        acc_ref[1:2, :] += _colsum(dh2)
        dx1 = _ln_bwd(dh2 * (1.0 + modv_ref[4:5, :]), xn2, rstd2) + ALPHA * dr2
        acc_ref[2:3, :] += _colsum(dx1 * xh1)
        acc_ref[3:4, :] += _colsum(dx1)
        dr1_ref[...] = _ln_bwd(dx1 * lnv_ref[0:1, :], xh1, rstd1)

    tok = lambda w: pl.BlockSpec((tb, w), lambda i: (i, 0))
    return pl.pallas_call(
        body, name="b1b_ffn_up", grid=(nt,),
        in_specs=[tok(fw), pl.BlockSpec((16, fw), lambda i: (jnp.minimum((i + 1) * hb, t // 16 - 1), 0)), tok(fw),
                  tok(D_MODEL), tok(D_MODEL), _full(modv), _full(lnv), _full(fdw), ANY],
        out_specs=[tok(fw), tok(D_MODEL), tok(D_MODEL), tok(D_MODEL), pl.BlockSpec((8, D_MODEL), lambda i: (0, 0)),
                   pl.BlockSpec((8, fw), lambda i: (0, 0))],
        out_shape=[jax.ShapeDtypeStruct((t, fw), BF16), jax.ShapeDtypeStruct((t, D_MODEL), F32),
                   jax.ShapeDtypeStruct((t, D_MODEL), BF16), jax.ShapeDtypeStruct((t, D_MODEL), BF16),
                   jax.ShapeDtypeStruct((8, D_MODEL), F32), jax.ShapeDtypeStruct((8, fw), F32)],
        scratch_shapes=[pltpu.VMEM(w_up.shape, BF16), pltpu.VMEM((tb + 16, fw), F32),
                        pltpu.VMEM((FFN_KERNEL - 1, tb, FFN_COLS), F32), pltpu.SemaphoreType.DMA],
        compiler_params=_params(("arbitrary",)),
    )(d, d, up, dr2, r1, modv, lnv, fdw, w_up)


def _b2_mixer(dr1, ys4, prest, ya, yb, ycv, vc, merged, modv, cvv, cw32, w_a, w_b, w_pw, w_out, late_dw, tb):
    t = dr1.shape[0]
    nt = t // tb
    nl = len(LATE)
    hb = tb // CONV_HALO
    cwd = CONV_WIDTH

    def body(dr1_ref, ys_ref, pr_ref, halo_ref, ya_ref, yb_ref, ycv_ref, vc_ref, mg_ref, modv_ref, cvv_ref, cw_ref,
             wa_ref, wb_ref, wpw_ref, wout_ref, *rest):
        dw, got = rest[:nl], rest[nl:2 * nl]
        (dys_ref, dpr_ref, dya_ref, dyb_ref, dycv_ref, dym_ref, acc_a, acc_b, acc_w, vbuf, dvbuf, vrot, dvrot,
         send, recv) = rest[2 * nl:]
        i = pl.program_id(0)
        ti = nt - 1 - i

        @pl.when(i == 0)
        def _():
            for cp in _scatter(LATE, dw, got, send, recv):
                cp.start()
            acc_a[...] = jnp.zeros_like(acc_a)
            acc_b[...] = jnp.zeros_like(acc_b)
            acc_w[...] = jnp.zeros_like(acc_w)
            dvbuf[pl.ds(tb, CONV_HALO), :] = jnp.zeros((CONV_HALO, cwd), F32)

        dr1 = dr1_ref[...]
        dym = (modv_ref[2:3, :] * dr1).astype(BF16)
        dym_ref[...] = dym
        ym = _dot(mg_ref[...], wout_ref[...])
        acc_a[0:1, :] += _colsum(dr1 * ym)
        dmg = _dot_nt(dym, wout_ref[...])
        sgs = _sig(pr_ref[:, 2 * cwd:2 * cwd + D_MODEL])
        sgc = _sig(pr_ref[:, 2 * cwd + D_MODEL:])
        ya_v = ya_ref[...].astype(F32)
        syb = _sig(yb_ref[...].astype(F32))
        ycv_v = ycv_ref[...].astype(F32)
        dpr_ref[:, 2 * cwd:2 * cwd + D_MODEL] = (dmg * (ya_v * syb) * sgs * (1.0 - sgs)).astype(BF16)
        dpr_ref[:, 2 * cwd + D_MODEL:] = (dmg * ycv_v * sgc * (1.0 - sgc)).astype(BF16)
        dyssm = dmg * sgs
        dya = (dyssm * syb).astype(BF16)
        dyb = (dyssm * ya_v * syb * (1.0 - syb)).astype(BF16)
        dya_ref[...] = dya
        dyb_ref[...] = dyb
        dyg = _dot_nt(dya, wa_ref[...]) + _dot_nt(dyb, wb_ref[...])
        ys = jnp.concatenate([ys_ref[b] for b in range(N_LANE_BLOCKS)], axis=-1)
        dys = dyg * _dgelu(ys)
        for b in range(N_LANE_BLOCKS):
            dys_ref[b] = dys[:, b * LANES:(b + 1) * LANES]
        dycv = (dmg * sgc).astype(BF16)
        dycv_ref[...] = dycv
        dvs = _dot_nt(dycv, wpw_ref[...])
        xh, rstd = _ln_stats(vc_ref[...])
        vl = xh * cvv_ref[1:2, :] + cvv_ref[2:3, :]
        s = _sig(vl)
        dvl = dvs * s * (1.0 + vl * (1.0 - s))
        acc_b[1:2, :] += _colsum(dvl * xh)
        acc_b[2:3, :] += _colsum(dvl)
        dvc = _ln_bwd(dvl * cvv_ref[1:2, :], xh, rstd)
        acc_b[0:1, :] += _colsum(dvc)
        hv = halo_ref[:, 0:cwd] * _sig(halo_ref[:, cwd:2 * cwd])
        vbuf[0:CONV_HALO, :] = jnp.where(ti == 0, 0.0, hv)
        cva = pr_ref[:, 0:cwd]
        scg = _sig(pr_ref[:, cwd:2 * cwd])
        vbuf[CONV_HALO:, :] = cva * scg
        dvbuf[0:tb, :] = dvc
        _fill_rotations(vbuf, vrot, tb + CONV_HALO - 8)
        _fill_rotations(dvbuf, dvrot, tb + CONV_HALO - 8)
        dv = jnp.zeros((tb, cwd), F32)
        for k in range(CONV_KERNEL):
            dv += _rows_at(dvbuf, dvrot, CONV_KERNEL - 1 - k, tb) * cw_ref[k:k + 1, :]
            acc_w[k:k + 1, :] += _colsum(dvc * _rows_at(vbuf, vrot, CONV_HALO - CONV_KERNEL + 1 + k, tb))
        dvbuf[pl.ds(tb, CONV_HALO), :] = dvbuf[0:CONV_HALO, :]
        dpr_ref[:, 0:cwd] = (dv * scg).astype(BF16)
        dpr_ref[:, cwd:2 * cwd] = (dv * cva * scg * (1.0 - scg)).astype(BF16)

        @pl.when(i == nt - 1)
        def _():
            for cp in _scatter(LATE, dw, got, send, recv):
                cp.wait()

    rtok = lambda w: pl.BlockSpec((tb, w), lambda i: (nt - 1 - i, 0))
    r4 = pl.BlockSpec((N_LANE_BLOCKS, tb, LANES), lambda i: (0, nt - 1 - i, 0))
    pw = prest.shape[1]
    return pl.pallas_call(
        body, name="b2_mixer", grid=(nt,),
        in_specs=[rtok(D_MODEL), r4, rtok(pw),
                  pl.BlockSpec((CONV_HALO, 2 * cwd), lambda i: (jnp.maximum((nt - 1 - i) * hb - 1, 0), 0)),
                  rtok(D_MODEL), rtok(D_MODEL), rtok(D_MODEL), rtok(cwd), rtok(D_MODEL),
                  _full(modv), _full(cvv), _full(cw32), _full(w_a), _full(w_b), _full(w_pw), _full(w_out)] + [ANY] * nl,
        out_specs=[ANY] * nl + [r4, rtok(pw), rtok(D_MODEL), rtok(D_MODEL), rtok(D_MODEL), rtok(D_MODEL),
                   pl.BlockSpec((8, D_MODEL), lambda i: (0, 0)), pl.BlockSpec((8, cwd), lambda i: (0, 0)),
                   pl.BlockSpec((CONV_HALO, cwd), lambda i: (0, 0))],
        out_shape=[jax.ShapeDtypeStruct((N_DEV - 1,) + _piece(*BIG[m][1:]), BF16) for m in LATE]
        + [jax.ShapeDtypeStruct((N_LANE_BLOCKS, t, LANES), F32), jax.ShapeDtypeStruct((t, pw), BF16),
                   jax.ShapeDtypeStruct((t, D_MODEL), BF16), jax.ShapeDtypeStruct((t, D_MODEL), BF16),
                   jax.ShapeDtypeStruct((t, D_MODEL), BF16), jax.ShapeDtypeStruct((t, D_MODEL), BF16),
                   jax.ShapeDtypeStruct((8, D_MODEL), F32), jax.ShapeDtypeStruct((8, cwd), F32),
                   jax.ShapeDtypeStruct((CONV_HALO, cwd), F32)],
        scratch_shapes=[pltpu.VMEM((tb + CONV_HALO, cwd), F32), pltpu.VMEM((tb + CONV_HALO, cwd), F32),
                        pltpu.VMEM((7, tb + CONV_HALO - 8, cwd), F32), pltpu.VMEM((7, tb + CONV_HALO - 8, cwd), F32),
                        pltpu.SemaphoreType.DMA((nl, N_DEV - 1)), pltpu.SemaphoreType.DMA((nl, N_DEV - 1))],
        compiler_params=_params(("arbitrary",)),
    )(dr1, ys4, prest, prest, ya, yb, ycv, vc, merged, modv, cvv, cw32, w_a, w_b, w_pw, w_out, *late_dw)


def _s5c_state_bwd(dy4, sg_r, sg_i, a8, hr, hi):
    nb, t, _ = dy4.shape
    nc = t // SSM_CHUNK
    sc = STATE_COLS

    def body(dy_ref, sgr_ref, sgi_ref, a_ref, hr_ref, hi_ref, dxr_ref, dxi_ref, da_ref, dsgr_ref, dsgi_ref,
             g_s, lr_s, li_s, xr_s, xi_s):
        dy = _load_chunks(dy_ref, nc)
        _build_maps(sgr_ref, g_s)
        lr_s[...] = _dot(dy, g_s[...])
        _build_maps(sgi_ref, g_s)
        li_s[...] = _dot(dy, g_s[...])
        ar = a_ref[0, 0:1, :]
        ai = a_ref[0, 1:2, :]

        def step(k, carry):
            pr, pi, dar, dai = carry
            c = nc - 1 - k
            xr_s[pl.ds(c, 1), :] = pr
            xi_s[pl.ds(c, 1), :] = pi
            h_r = hr_ref[0, pl.ds(c, 1), :]
            h_i = hi_ref[0, pl.ds(c, 1), :]
            dar = dar + pr * h_r + pi * h_i
            dai = dai - pr * h_i + pi * h_r
            nr = lr_s[pl.ds(c, 1), :] + ar * pr + ai * pi
            ni = li_s[pl.ds(c, 1), :] - ai * pr + ar * pi
            return nr, ni, dar, dai

        z = jnp.zeros((1, sc), F32)
        _, _, dar, dai = lax.fori_loop(0, nc, step, (z, z, z, z))
        da_ref[0] = jnp.concatenate([dar, dai, jnp.zeros((6, sc), F32)], axis=0)
        dxr_ref[0] = xr_s[...].astype(BF16)
        dxi_ref[0] = xi_s[...].astype(BF16)
        for h_ref, o_ref in ((hr_ref, dsgr_ref), (hi_ref, dsgi_ref)):
            hb = h_ref[0].astype(BF16)
            for j in range(SSM_CHUNK):
                o_ref[j] = _fold(_dot_tn(dy[:, j * LANES:(j + 1) * LANES], hb), SSM_STATE)

    blk = lambda r, c: pl.BlockSpec((1, r, c), lambda b: (b, 0, 0))
    return pl.pallas_call(
        body, name="s5c_state_bwd", grid=(nb,),
        in_specs=[_tblk(t), _cblk(SSM_STATE), _cblk(SSM_STATE), blk(8, sc), blk(nc, sc), blk(nc, sc)],
        out_specs=[blk(nc, sc), blk(nc, sc), blk(8, sc), _cblk(SSM_STATE), _cblk(SSM_STATE)],
        out_shape=[jax.ShapeDtypeStruct((nb, nc, sc), BF16), jax.ShapeDtypeStruct((nb, nc, sc), BF16),
                   jax.ShapeDtypeStruct((nb, 8, sc), F32),
                   jax.ShapeDtypeStruct((SSM_CHUNK, SSM_WIDTH, SSM_STATE), F32),
                   jax.ShapeDtypeStruct((SSM_CHUNK, SSM_WIDTH, SSM_STATE), F32)],
        scratch_shapes=[pltpu.VMEM((CHUNK_COLS, sc), BF16)] + [pltpu.VMEM((nc, sc), F32)] * 4,
        compiler_params=_params(("arbitrary",)),
    )(dy4, sg_r, sg_i, a8, hr, hi)


def _s5d_input_bwd(dy4, u4, kc, sw_r, sw_i, dxr, dxi, mid_dw):
    nb, t, _ = dy4.shape
    nc = t // SSM_CHUNK
    sc = STATE_COLS
    nl = len(MID)

    def body(dy_ref, u_ref, kc_ref, swr_ref, swi_ref, dxr_ref, dxi_ref, *rest):
        dw, got = rest[:nl], rest[nl:2 * nl]
        du_ref, dkc_ref, dswr_ref, dswi_ref, tm_s, w_s, dk_s, send, recv = rest[2 * nl:]

        @pl.when(pl.program_id(0) == 0)
        def _():
            for cp in _scatter(MID, dw, got, send, recv):
                cp.start()

        dy = _load_chunks(dy_ref, nc)
        u = _load_chunks(u_ref, nc)
        _build_toeplitz(kc_ref, tm_s)
        du = _dot_nt(dy, tm_s[...])
        _build_maps(swr_ref, w_s)
        du += _dot_nt(dxr_ref[0], w_s[...])
        _build_maps(swi_ref, w_s)
        du += _dot_nt(dxi_ref[0], w_s[...])
        _store_chunks(du_ref, du, nc)
        dk_s[...] = jnp.zeros_like(dk_s)
        for ji in range(SSM_CHUNK):
            uj = u[:, ji * LANES:(ji + 1) * LANES]
            rows = _dot_tn(uj, dy)
            for jo in range(ji, SSM_CHUNK):
                dk_s[jo - ji] += rows[:, jo * LANES:(jo + 1) * LANES]
            dswr_ref[ji] = _fold(_dot_tn(uj, dxr_ref[0]), SSM_STATE)
            dswi_ref[ji] = _fold(_dot_tn(uj, dxi_ref[0]), SSM_STATE)
        for d in range(SSM_CHUNK):
            dkc_ref[d] = _fold(dk_s[d], SSM_GROUP)

        @pl.when(pl.program_id(0) == nb - 1)
        def _():
            for cp in _scatter(MID, dw, got, send, recv):
                cp.wait()

    blk = lambda r, c: pl.BlockSpec((1, r, c), lambda b: (b, 0, 0))
    ssem = pltpu.SemaphoreType.DMA((nl, N_DEV - 1))
    return pl.pallas_call(
        body, name="s5d_input_bwd", grid=(nb,),
        in_specs=[_tblk(t), _tblk(t), _cblk(SSM_GROUP), _cblk(SSM_STATE), _cblk(SSM_STATE),
                  blk(nc, sc), blk(nc, sc)] + [ANY] * nl,
        out_specs=[ANY] * nl + [_tblk(t), _cblk(SSM_GROUP), _cblk(SSM_STATE), _cblk(SSM_STATE)],
        out_shape=[jax.ShapeDtypeStruct((N_DEV - 1,) + _piece(*BIG[m][1:]), BF16) for m in MID]
        + [jax.ShapeDtypeStruct((nb, t, LANES), F32),
           jax.ShapeDtypeStruct((SSM_CHUNK, SSM_WIDTH, SSM_GROUP), F32),
           jax.ShapeDtypeStruct((SSM_CHUNK, SSM_WIDTH, SSM_STATE), F32),
           jax.ShapeDtypeStruct((SSM_CHUNK, SSM_WIDTH, SSM_STATE), F32)],
        scratch_shapes=[pltpu.VMEM((CHUNK_COLS, CHUNK_COLS), BF16), pltpu.VMEM((CHUNK_COLS, sc), BF16),
                        pltpu.VMEM((SSM_CHUNK, LANES, LANES), F32), ssem, ssem],
        compiler_params=_params(("arbitrary",)),
    )(dy4, u4, kc, sw_r, sw_i, dxr, dxi, *mid_dw)


def _b3_inproj(x, dr1, du4, dprest, modv, w_in, tb):
    t = x.shape[0]
    pw = IN_PROJ_WIDTH - SSM_WIDTH

    def body(x_ref, dr1_ref, du_ref, dpr_ref, modv_ref, w_hbm, gx_ref, dp_ref, acc_ref, accb_ref, w_v, sem):
        @pl.when(pl.program_id(0) == 0)
        def _():
            _load_once(w_hbm, w_v, sem)
            acc_ref[...] = jnp.zeros_like(acc_ref)
            accb_ref[...] = jnp.zeros_like(accb_ref)

        du = jnp.concatenate([du_ref[b] for b in range(N_LANE_BLOCKS)], axis=-1).astype(BF16)
        dpr = dpr_ref[...]
        dp_ref[:, 0:SSM_WIDTH] = du
        dp_ref[:, SSM_WIDTH:] = dpr
        accb_ref[0:1, 0:SSM_WIDTH] += _colsum(du.astype(F32))
        accb_ref[0:1, SSM_WIDTH:] += _colsum(dpr.astype(F32))
        dh = _dot_nt(du, w_v[:, 0:SSM_WIDTH]) + _dot_nt(dpr, w_v[:, SSM_WIDTH:])
        xn, rstd = _ln_stats(x_ref[...])
        acc_ref[0:1, :] += _colsum(dh * xn)
        acc_ref[1:2, :] += _colsum(dh)
        gx_ref[...] = _ln_bwd(dh * (1.0 + modv_ref[1:2, :]), xn, rstd) + ALPHA * dr1_ref[...]

    tok = lambda w: pl.BlockSpec((tb, w), lambda i: (i, 0))
    return pl.pallas_call(
        body, name="b3_inproj", grid=(t // tb,),
        in_specs=[tok(D_MODEL), tok(D_MODEL), pl.BlockSpec((N_LANE_BLOCKS, tb, LANES), lambda i: (0, i, 0)), tok(pw),
                  _full(modv), ANY],
        out_specs=[tok(D_MODEL), tok(IN_PROJ_WIDTH), pl.BlockSpec((8, D_MODEL), lambda i: (0, 0)),
                   pl.BlockSpec((8, IN_PROJ_WIDTH), lambda i: (0, 0))],
        out_shape=[jax.ShapeDtypeStruct((t, D_MODEL), F32), jax.ShapeDtypeStruct((t, IN_PROJ_WIDTH), BF16),
                   jax.ShapeDtypeStruct((8, D_MODEL), F32), jax.ShapeDtypeStruct((8, IN_PROJ_WIDTH), F32)],
        scratch_shapes=[pltpu.VMEM(w_in.shape, BF16), pltpu.SemaphoreType.DMA],
        compiler_params=_params(("arbitrary",)),
    )(x, dr1, du4, dprest, modv, w_in)


TN_ROWS = 2048


def _tn_matmul(name, a, b, tm, tn):
    t, m = a.shape
    n = b.shape[1]
    tt = min(TN_ROWS, t)
    nk = t // tt

    def body(a_ref, b_ref, o_ref, acc):
        k = pl.program_id(2)

        @pl.when(k == 0)
        def _():
            acc[...] = jnp.zeros_like(acc)

        acc[...] += _dot_tn(a_ref[...], b_ref[...])

        @pl.when(k == nk - 1)
        def _():
            o_ref[...] = acc[...].astype(BF16)

    return pl.pallas_call(
        body, name=name, grid=(m // tm, n // tn, nk),
        in_specs=[pl.BlockSpec((tt, tm), lambda i, j, k: (k, i)), pl.BlockSpec((tt, tn), lambda i, j, k: (k, j))],
        out_specs=pl.BlockSpec((tm, tn), lambda i, j, k: (i, j)),
        out_shape=jax.ShapeDtypeStruct((m, n), BF16),
        scratch_shapes=[pltpu.VMEM((tm, tn), F32)],
        compiler_params=_params(("arbitrary", "arbitrary", "arbitrary")),
    )(a, b)


def _local_step(x, tgt, modv, wb, shards, sp, tb=256):
    t = x.shape[0]
    row8 = lambda rows, w: jnp.concatenate([r.reshape(1, w) for r in rows] + [jnp.zeros((8 - len(rows), w), F32)], axis=0)
    lnv = row8([sp["ln1_g"], sp["ln1_b"], sp["ln2_g"], sp["ln2_b"]], D_MODEL)
    cvv = row8([sp["cv_dw_b"], sp["cv_ln_g"], sp["cv_ln_b"]], CONV_WIDTH)
    cw32 = jnp.concatenate([sp["cv_dw_w"].reshape(CONV_KERNEL, CONV_WIDTH), jnp.zeros((1, CONV_WIDTH), F32)], axis=0)
    fdw = row8(list(sp["ffn_dw_w"].reshape(FFN_KERNEL, 2 * FFN_HIDDEN)) + [sp["ffn_dw_b"]], 2 * FFN_HIDDEN)
    b_in = sp["b_in"].reshape(1, IN_PROJ_WIDTH)
    ssm = tuple(sp[k] for k in ("ssm_lambda_re", "ssm_lambda_im", "ssm_log_dt", "ssm_b_re", "ssm_b_im", "ssm_c_re",
                                "ssm_c_im", "ssm_d"))
    (kc, sw_r, sw_i, sg_r, sg_i, a), ssm_vjp = jax.vjp(_s5_build, *ssm)
    a8 = jnp.concatenate([a, jnp.zeros((N_LANE_BLOCKS, 6, STATE_COLS), F32)], axis=1)

    name = lambda m: BIG[m][0]
    *mid_w, u4, prest, h1 = _f1_inproj(x, modv, b_in, wb["w_in"], [shards[m] for m in MID], [wb[name(m)] for m in MID], tb)
    w_a, w_b, w_pw, w_out = mid_w
    hr, hi = _s5a_state(u4, sw_r, sw_i, a8)
    ys4 = _s5b_out(u4, kc, sg_r, sg_i, hr, hi)
    w_up, w_down, r1, ya, yb, ycv, vc, yg, vs, merged = _f4_mixer(
        ys4, prest, x, modv, cvv, cw32, w_a, w_b, w_pw, w_out, [shards[m] for m in LATE], [wb[name(m)] for m in LATE], tb)
    dr2, dconv, up, z, acc5 = _f5_ffn(r1, tgt, modv, lnv, fdw, w_up, w_down, tb)
    dup, dr1, h2, dyf, acc1b, acc1a = _b1b_ffn_up(dconv, up, dr2, r1, modv, lnv, fdw, w_up, tb)
    late_dw = [_tn_matmul("dw_up", h2, dup, 1024, FFN_COLS), _tn_matmul("dw_down", z, dyf, FFN_COLS, 1024)]
    got_up, got_down, dys4, dprest, dya, dyb, dycv, dym, acc2a, acc2b, acc2w = _b2_mixer(
        dr1, ys4, prest, ya, yb, ycv, vc, merged, modv, cvv, cw32, w_a, w_b, w_pw, w_out, late_dw, tb)
    mid_dw = [_tn_matmul("dw_glu_a", yg, dya, 512, 1024), _tn_matmul("dw_glu_b", yg, dyb, 512, 1024),
              _tn_matmul("dw_pw", vs, dycv, 512, 1024), _tn_matmul("dw_out", merged, dym, 1024, 1024)]
    dxr, dxi, da8, dsg_r, dsg_i = _s5c_state_bwd(dys4, sg_r, sg_i, a8, hr, hi)
    *mid_got, du4, dkc, dsw_r, dsw_i = _s5d_input_bwd(dys4, u4, kc, sw_r, sw_i, dxr, dxi, mid_dw)
    dssm = ssm_vjp((dkc, dsw_r, dsw_i, dsg_r, dsg_i, da8[:, 0:2, :]))
    gx, dp, acc3, acc3b = _b3_inproj(x, dr1, du4, dprest, modv, wb["w_in"], tb)
    dbig = [_tn_matmul("dw_in", h1, dp, 1024, 896)] + mid_dw + late_dw
    dmod = jnp.concatenate([acc3[1], acc3[0], acc2a[0], acc1b[1], acc1b[0], acc5[2]])
    small = {
        "dmod": dmod, "b_in": acc3b[0],
        "ssm_lambda_re": dssm[0], "ssm_lambda_im": dssm[1], "ssm_log_dt": dssm[2], "ssm_b_re": dssm[3],
        "ssm_b_im": dssm[4], "ssm_c_re": dssm[5], "ssm_c_im": dssm[6], "ssm_d": dssm[7],
        "cv_dw_w": acc2w[0:CONV_KERNEL], "cv_dw_b": acc2b[0], "cv_ln_g": acc2b[1], "cv_ln_b": acc2b[2],
        "ln1_g": acc1b[2], "ln1_b": acc1b[3], "ffn_dw_w": acc1a[0:FFN_KERNEL], "ffn_dw_b": acc1a[3],
        "ln2_g": acc5[0], "ln2_b": acc5[1], "loss": jnp.sum(acc5[3]).reshape(1),
    }
    return gx, dbig, list(mid_got) + [got_up, got_down], small


def _allgather(name, shard):
    m_per, n = shard.shape

    def body(x_ref, out_ref, send_sems, recv_sems, local_sem):
        x, y, c, chips = _place()
        me, sibling = (x, y, c), (x, y, 1 - c)

        def rows(px, py, pc):
            return out_ref.at[pl.ds((4 * px + 2 * py + pc) * m_per, m_per), :]

        def copy(k, block, to, src=None):
            return pltpu.make_async_remote_copy(
                src_ref=rows(*block) if src is None else src, dst_ref=rows(*block),
                send_sem=send_sems.at[k], recv_sem=recv_sems.at[k], device_id=to, device_id_type=MESH)

        mine = pltpu.make_async_copy(x_ref, rows(*me), local_sem)
        mine.start()
        first = [copy(0, me, sibling, src=x_ref)]
        first += [copy(1 + j, me, (*chip, c), src=x_ref) for j, chip in enumerate(chips)]
        for cp in first:
            cp.start()
        passed = [copy(4 + j, (*chip, c), sibling) for j, chip in enumerate(chips)]
        for j, chip in enumerate(chips):
            copy(1 + j, (*chip, c), me).wait_recv()
            passed[j].start()
        copy(0, sibling, me).wait_recv()
        for j, chip in enumerate(chips):
            copy(4 + j, (*chip, 1 - c), me).wait_recv()
        for cp in first + passed:
            cp.wait_send()
        mine.wait()

    return pl.pallas_call(
        body, name=name,
        out_shape=jax.ShapeDtypeStruct((N_DEV * m_per, n), shard.dtype),
        in_specs=[pl.BlockSpec(memory_space=pltpu.VMEM)],
        out_specs=pl.BlockSpec(memory_space=pltpu.VMEM),
        scratch_shapes=[pltpu.SemaphoreType.DMA((7,)), pltpu.SemaphoreType.DMA((7,)), pltpu.SemaphoreType.DMA],
        compiler_params=_params(),
    )(shard)


def _add_rows(pr):
    return 64 if pr % 64 == 0 else 16


def _gather_weights(shards):
    nm = len(BIG)
    nl = len(DIRECT)

    def body(*refs):
        ins, outs, lsh = refs[:nm], refs[nm:2 * nm], refs[2 * nm:2 * nm + nl]
        stage = refs[2 * nm + nl:3 * nm + nl]
        send, recv, fsend, frecv, lsem = refs[3 * nm + nl:]
        x, y, c, chips = _place()
        s_me = 2 * x + y
        sibling = (x, y, 1 - c)
        pend = []
        for m in range(nm):
            stage[m][...] = ins[m][...].astype(BF16)
        for m, (_, kind, shape) in enumerate(BIG):
            pr, pc = _piece(kind, shape)
            for k in range(2):
                cp = pltpu.make_async_copy(stage[m].at[pl.ds(k * pr, pr), :], _piece_at(outs[m], kind, shape, s_me, k),
                                           lsem.at[m, k])
                cp.start()
                pend.append(cp.wait)
            if m in DIRECT:
                cp = pltpu.make_async_copy(stage[m], lsh[DIRECT.index(m)], lsem.at[m, 2])
                cp.start()
                pend.append(cp.wait)
                continue
            for j, chip in enumerate(chips):
                cp = pltpu.make_async_remote_copy(
                    src_ref=stage[m].at[pl.ds(pl.multiple_of(c * pr, 16), pr), :],
                    dst_ref=_piece_at(outs[m], kind, shape, s_me, c),
                    send_sem=send.at[m, j], recv_sem=recv.at[m, j], device_id=(*chip, c), device_id_type=MESH)
                cp.start()
                pend.append(cp.wait_send)
        for m in EARLY:
            _, kind, shape = BIG[m]
            for j, (cx, cy) in enumerate(chips):
                got = _piece_at(outs[m], kind, shape, 2 * cx + cy, c)
                pltpu.make_async_remote_copy(src_ref=got, dst_ref=got, send_sem=send.at[m, j], recv_sem=recv.at[m, j],
                                             device_id=(cx, cy, c), device_id_type=MESH).wait_recv()
                cp = pltpu.make_async_remote_copy(src_ref=got, dst_ref=got, send_sem=fsend.at[m, j],
                                                  recv_sem=frecv.at[m, j], device_id=sibling, device_id_type=MESH)
                cp.start()
                pend.append(cp.wait_send)
        for m in EARLY:
            _, kind, shape = BIG[m]
            for j, (cx, cy) in enumerate(chips):
                got = _piece_at(outs[m], kind, shape, 2 * cx + cy, 1 - c)
                pltpu.make_async_remote_copy(src_ref=got, dst_ref=got, send_sem=fsend.at[m, j], recv_sem=frecv.at[m, j],
                                             device_id=sibling, device_id_type=MESH).wait_recv()
        for w in pend:
            w()

    sem = lambda *s: pltpu.SemaphoreType.DMA(s)
    res = pl.pallas_call(
        body, name="gather_weights",
        out_shape=[jax.ShapeDtypeStruct(shape, BF16) for _, _, shape in BIG]
        + [jax.ShapeDtypeStruct(shards[m].shape, BF16) for m in DIRECT],
        in_specs=[pl.BlockSpec(memory_space=pltpu.VMEM)] * nm,
        out_specs=[ANY] * (nm + nl),
        scratch_shapes=[pltpu.VMEM(s.shape, BF16) for s in shards] + [sem(nm, 3), sem(nm, 3), sem(nm, 3), sem(nm, 3),
                                                                         sem(nm, 3)],
        compiler_params=_params(),
    )(*shards)
    return res[:nm], dict(zip(DIRECT, res[nm:]))


def _rs1_sibling(grads):
    mats = [BIG[m] for m in EARLY]
    nm = len(mats)

    def body(*refs):
        ins, outs = refs[:nm], refs[nm:2 * nm]
        send, recv = refs[2 * nm:]
        x, y, c, _ = _place()
        cps = []
        for m, (_, kind, shape) in enumerate(mats):
            for s in range(N_CHIP):
                cp = pltpu.make_async_remote_copy(
                    src_ref=_piece_at(ins[m], kind, shape, s, 1 - c), dst_ref=outs[m].at[s],
                    send_sem=send.at[m, s], recv_sem=recv.at[m, s], device_id=(x, y, 1 - c), device_id_type=MESH)
                cp.start()
                cps.append(cp)
        for cp in cps:
            cp.wait()

    sem = lambda *s: pltpu.SemaphoreType.DMA(s)
    return pl.pallas_call(
        body, name="rs1_sibling",
        out_shape=[jax.ShapeDtypeStruct((N_CHIP,) + _piece(kind, shape), BF16) for _, kind, shape in mats],
        in_specs=[ANY] * nm, out_specs=[ANY] * nm,
        scratch_shapes=[sem(nm, N_CHIP), sem(nm, N_CHIP)],
        compiler_params=_params(),
    )(*grads)


def _rs2_chips(grads, halves):
    mats = [BIG[m] for m in EARLY]
    nm = len(mats)

    def body(*refs):
        gin, hin = refs[:nm], refs[nm:2 * nm]
        own, got = refs[2 * nm:3 * nm], refs[3 * nm:4 * nm]
        send, recv, lsem = refs[4 * nm:]
        x, y, c, chips = _place()
        s_me = 2 * x + y
        for m, (_, kind, shape) in enumerate(mats):
            pr, pc = _piece(kind, shape)

            def scoped(a, b, m=m, kind=kind, shape=shape, pr=pr):
                loads = [pltpu.make_async_copy(_piece_at(gin[m], kind, shape, s, c), a.at[s], lsem.at[s])
                         for s in range(N_CHIP)]
                loads.append(pltpu.make_async_copy(hin[m], b, lsem.at[N_CHIP]))
                for cp in loads:
                    cp.start()
                for cp in loads:
                    cp.wait()
                step = _add_rows(pr)
                for s in range(N_CHIP):
                    def add(i, _, s=s):
                        r = pl.ds(pl.multiple_of(i * step, 16), step)
                        a[s, r, :] = (a[s, r, :].astype(F32) + b[s, r, :].astype(F32)).astype(BF16)
                        return 0

                    lax.fori_loop(0, pr // step, add, 0)
                waits = []
                for j, (cx, cy) in enumerate(chips):
                    cp = pltpu.make_async_remote_copy(src_ref=a.at[2 * cx + cy], dst_ref=got[m].at[j], send_sem=send.at[m, j],
                                                      recv_sem=recv.at[m, j], device_id=(cx, cy, c), device_id_type=MESH)
                    cp.start()
                    waits.append(cp.wait_send)
                cp = pltpu.make_async_copy(a.at[s_me], own[m], lsem.at[N_CHIP + 1])
                cp.start()
                waits.append(cp.wait)
                for w in waits:
                    w()

            pl.run_scoped(scoped, pltpu.VMEM((N_CHIP, pr, pc), BF16), pltpu.VMEM((N_CHIP, pr, pc), BF16))
        for m in range(nm):
            for j, (cx, cy) in enumerate(chips):
                pltpu.make_async_remote_copy(src_ref=got[m].at[j], dst_ref=got[m].at[j], send_sem=send.at[m, j],
                                             recv_sem=recv.at[m, j], device_id=(cx, cy, c), device_id_type=MESH).wait_recv()

    sem = lambda *s: pltpu.SemaphoreType.DMA(s)
    pieces = [_piece(kind, shape) for _, kind, shape in mats]
    return pl.pallas_call(
        body, name="rs2_chips",
        out_shape=[jax.ShapeDtypeStruct(p, BF16) for p in pieces] + [jax.ShapeDtypeStruct((3,) + p, BF16) for p in pieces],
        in_specs=[ANY] * (2 * nm), out_specs=[ANY] * (2 * nm),
        scratch_shapes=[sem(nm, 3), sem(nm, 3), sem(N_CHIP + 2)],
        compiler_params=_params(),
    )(*grads, *halves)


def _rs3_finish(own, got):
    nm = len(BIG)

    def body(*refs):
        oin, gin = refs[:nm], refs[nm:2 * nm]
        outs = refs[2 * nm:3 * nm]
        send, recv, lsem = refs[3 * nm:]
        x, y, c, _ = _place()
        for m, (_, kind, shape) in enumerate(BIG):
            pr, pc = _piece(kind, shape)
            ng = got[m].shape[0]

            def scoped(a, g, f, m=m, pr=pr, ng=ng, kind=kind, shape=shape):
                mine = _piece_at(oin[m], kind, shape, 2 * x + y, c) if m in DIRECT else oin[m]
                loads = [pltpu.make_async_copy(mine, a, lsem.at[0]), pltpu.make_async_copy(gin[m], g, lsem.at[1])]
                for cp in loads:
                    cp.start()
                for cp in loads:
                    cp.wait()
                step = _add_rows(pr)

                def add(i, _):
                    r = pl.ds(pl.multiple_of(i * step, 16), step)
                    acc = a[r, :].astype(F32)
                    for q in range(ng):
                        acc = acc + g[q, r, :].astype(F32)
                    f[r, :] = acc
                    return 0

                lax.fori_loop(0, pr // step, add, 0)
                dst = outs[m].at[pl.ds(pl.multiple_of(c * pr, 8), pr), :]
                local = pltpu.make_async_copy(f, dst, lsem.at[2])
                local.start()
                cp = pltpu.make_async_remote_copy(src_ref=f, dst_ref=dst, send_sem=send.at[m], recv_sem=recv.at[m],
                                                  device_id=(x, y, 1 - c), device_id_type=MESH)
                cp.start()
                cp.wait_send()
                local.wait()

            pl.run_scoped(scoped, pltpu.VMEM((pr, pc), BF16), pltpu.VMEM((ng, pr, pc), BF16), pltpu.VMEM((pr, pc), F32))
        for m, (_, kind, shape) in enumerate(BIG):
            pr, pc = _piece(kind, shape)
            dst = outs[m].at[pl.ds(pl.multiple_of((1 - c) * pr, 8), pr), :]
            pltpu.make_async_remote_copy(src_ref=dst, dst_ref=dst, send_sem=send.at[m], recv_sem=recv.at[m],
                                         device_id=(x, y, 1 - c), device_id_type=MESH).wait_recv()

    sem = lambda *s: pltpu.SemaphoreType.DMA(s)
    pieces = [_piece(kind, shape) for _, kind, shape in BIG]
    return pl.pallas_call(
        body, name="rs3_finish",
        out_shape=[jax.ShapeDtypeStruct((2 * pr, pc), F32) for pr, pc in pieces],
        in_specs=[ANY] * (2 * nm), out_specs=[ANY] * nm,
        scratch_shapes=[sem(nm), sem(nm), sem(3)],
        compiler_params=_params(),
    )(*own, *got)


def _cond_fwd(c_all, w_shard, b_shard):
    def body(c_ref, w_ref, b_ref, act_ref, mod_ref):
        cv = c_ref[...]
        act = cv * _sig(cv)
        act_ref[...] = act
        mod_ref[...] = _dot(act.astype(BF16), w_ref[...].astype(BF16)) + b_ref[...]

    return pl.pallas_call(
        body, name="cond_fwd",
        out_shape=[jax.ShapeDtypeStruct(c_all.shape, F32), jax.ShapeDtypeStruct((c_all.shape[0], w_shard.shape[1]), F32)],
        compiler_params=_params(),
    )(c_all, w_shard, b_shard)


def _cond_bwd(act_t, dmod_shard):
    k, n = act_t.shape[0], dmod_shard.shape[1]

    def body(a_ref, d_ref, o_ref):
        acc = a_ref[:, 0:1] * d_ref[0:1, :]
        for e in range(1, N_DEV):
            acc += a_ref[:, e:e + 1] * d_ref[e:e + 1, :]
        o_ref[...] = acc

    tr = 256
    return pl.pallas_call(
        body, name="cond_bwd", grid=(k // tr,),
        in_specs=[pl.BlockSpec((tr, N_DEV), lambda i: (i, 0)), _full(dmod_shard)],
        out_specs=pl.BlockSpec((tr, n), lambda i: (i, 0)),
        out_shape=jax.ShapeDtypeStruct((k, n), F32),
        compiler_params=_params(("arbitrary",)),
    )(act_t, dmod_shard)


def _sum_blocks(allp):
    def body(a_ref, o_ref):
        acc = a_ref[0:PACK_ROWS, :]
        for d in range(1, N_DEV):
            acc += a_ref[d * PACK_ROWS:(d + 1) * PACK_ROWS, :]
        o_ref[...] = acc

    return pl.pallas_call(
        body, name="sum_small", out_shape=jax.ShapeDtypeStruct((PACK_ROWS, PACK_COLS), F32), compiler_params=_params(),
    )(allp)


def _adamw(name, w, g, m, v):
    r, cc = w.shape
    tr = r
    for cand in (256, 128, 64, 32, 16, 8):
        if r % cand == 0:
            tr = cand
            break
    bc1 = 1.0 - ADAM_B1 ** ADAM_STEP
    bc2 = 1.0 - ADAM_B2 ** ADAM_STEP

    def body(w_ref, g_ref, m_ref, v_ref, d_ref, nm_ref, nv_ref):
        gv = g_ref[...]
        nm = ADAM_B1 * m_ref[...] + (1.0 - ADAM_B1) * gv
        nv = ADAM_B2 * v_ref[...] + (1.0 - ADAM_B2) * (gv * gv)
        nm_ref[...] = nm
        nv_ref[...] = nv
        d_ref[...] = -ADAM_LR * ((nm / bc1) / (jnp.sqrt(nv / bc2) + ADAM_EPS) + ADAM_WD * w_ref[...])

    spec = pl.BlockSpec((tr, cc), lambda i: (i, 0))
    return pl.pallas_call(
        body, name=name, grid=(r // tr,), in_specs=[spec] * 4, out_specs=[spec] * 3,
        out_shape=[jax.ShapeDtypeStruct((r, cc), F32)] * 3, compiler_params=_params(("arbitrary",)),
    )(w, g, m, v)


def _pack(fields, layout):
    parts = [fields[name].reshape(-1).astype(F32) if name in fields else jnp.zeros((n,), F32) for name, n in layout]
    used = sum(n for _, n in layout)
    parts.append(jnp.zeros((PACK_ROWS * PACK_COLS - used,), F32))
    return jnp.concatenate(parts).reshape(PACK_ROWS, PACK_COLS)


def _unpack(flat, layout):
    flat = flat.reshape(-1)
    out, o = {}, 0
    for name, n in layout:
        out[name] = flat[o:o + n]
        o += n
    return out


def kernel(x, c, w_cond, b_cond, w_in, b_in, ssm_lambda_re, ssm_lambda_im, ssm_log_dt, ssm_b_re, ssm_b_im, ssm_c_re, ssm_c_im, ssm_d, ssm_glu_w_a, ssm_glu_w_b, cv_dw_w, cv_dw_b, cv_ln_g, cv_ln_b, cv_w_pw, w_out, ln1_g, ln1_b, ffn_w_up, ffn_dw_w, ffn_dw_b, ffn_w_down, ln2_g, ln2_b, loss_target, m_w_cond, m_b_cond, m_w_in, m_b_in, m_ssm_lambda_re, m_ssm_lambda_im, m_ssm_log_dt, m_ssm_b_re, m_ssm_b_im, m_ssm_c_re, m_ssm_c_im, m_ssm_d, m_ssm_glu_w_a, m_ssm_glu_w_b, m_cv_dw_w, m_cv_dw_b, m_cv_ln_g, m_cv_ln_b, m_cv_w_pw, m_w_out, m_ln1_g, m_ln1_b, m_ffn_w_up, m_ffn_dw_w, m_ffn_dw_b, m_ffn_w_down, m_ln2_g, m_ln2_b, v_w_cond, v_b_cond, v_w_in, v_b_in, v_ssm_lambda_re, v_ssm_lambda_im, v_ssm_log_dt, v_ssm_b_re, v_ssm_b_im, v_ssm_c_re, v_ssm_c_im, v_ssm_d, v_ssm_glu_w_a, v_ssm_glu_w_b, v_cv_dw_w, v_cv_dw_b, v_cv_ln_g, v_cv_ln_b, v_cv_w_pw, v_w_out, v_ln1_g, v_ln1_b, v_ffn_w_up, v_ffn_dw_w, v_ffn_dw_b, v_ffn_w_down, v_ln2_g, v_ln2_b):
    given = locals()
    a = {n: given[n] for n in INPUTS}
    xi, yi, ci = lax.axis_index("x"), lax.axis_index("y"), lax.axis_index("c")
    s_me = 2 * xi + yi
    e_me = 4 * xi + 2 * yi + ci

    first = jnp.concatenate([
        jnp.concatenate([a["c"], jnp.zeros((7, D_MODEL), F32)], axis=0),
        jnp.concatenate([a["cv_dw_w"].reshape(-1), a["ffn_dw_w"].reshape(-1)]).reshape(8, D_MODEL)], axis=0)
    first_all = _allgather("gather_c", first).reshape(N_DEV, 16, D_MODEL)
    c_all = first_all[:, 0, :]
    dw_all = first_all[0::2, 8:, :].reshape(N_CHIP, 8 * D_MODEL)
    n_cv = CONV_KERNEL * CONV_WIDTH // N_CHIP
    cv_dw_full = dw_all[:, :n_cv].reshape(N_CHIP, CONV_KERNEL, CONV_WIDTH // N_CHIP).transpose(1, 0, 2) \
        .reshape(CONV_KERNEL, CONV_WIDTH)
    ffn_dw_full = dw_all[:, n_cv:].reshape(N_CHIP, FFN_KERNEL, 2 * FFN_HIDDEN // N_CHIP).transpose(1, 0, 2) \
        .reshape(FFN_KERNEL, 2 * FFN_HIDDEN)
    ncols = N_COND * D_MODEL // N_CHIP
    b_cond_shard = lax.dynamic_slice(a["b_cond"], (0, s_me * ncols), (1, ncols))
    c_act_all, modp = _cond_fwd(c_all, a["w_cond"][0], b_cond_shard)
    modp_all = _allgather("gather_mod", modp).reshape(N_DEV, N_DEV, ncols)[0::2]
    mod_e = lax.dynamic_index_in_dim(modp_all, e_me, axis=1, keepdims=False).reshape(N_COND, D_MODEL)
    modv = jnp.concatenate([mod_e, jnp.zeros((2, D_MODEL), F32)], axis=0)

    full, shards = _gather_weights([a[n][0] for n, _, _ in BIG])
    wb = dict(zip([n for n, _, _ in BIG], full))
    sp = {n: a[n][0] for n in ("b_in", "ssm_lambda_re", "ssm_lambda_im", "ssm_log_dt", "ssm_b_re", "ssm_b_im",
                               "ssm_c_re", "ssm_c_im", "ssm_d", "cv_dw_b", "cv_ln_g", "cv_ln_b", "ln1_g", "ln1_b",
                               "ffn_dw_b", "ln2_g", "ln2_b")}
    sp["cv_dw_w"] = cv_dw_full
    sp["ffn_dw_w"] = ffn_dw_full
    gx, dbig, direct_got, small = _local_step(a["x"][0], a["loss_target"][0], modv, wb, shards, sp)

    small["c_act"] = lax.dynamic_index_in_dim(c_act_all, e_me, axis=0, keepdims=False)
    packed_all = _allgather("gather_small", _pack(small, PACK))
    tot = _unpack(_sum_blocks(packed_all), PACK)
    rows = packed_all.reshape(N_DEV, PACK_ROWS * PACK_COLS)
    dmod_all = rows[:, 0:N_COND * D_MODEL]
    act_all = rows[:, N_COND * D_MODEL:(N_COND + 1) * D_MODEL]
    g_w_cond = _cond_bwd(act_all.T, lax.dynamic_slice(dmod_all, (0, s_me * ncols), (N_DEV, ncols)))

    glist = [dbig[m] for m in EARLY]
    halves = _rs1_sibling(glist)
    r2 = _rs2_chips(glist, halves)
    gsh = _rs3_finish(list(r2[:len(EARLY)]) + [dbig[m] for m in DIRECT], list(r2[len(EARLY):]) + direct_got)

    grads = {"w_cond": g_w_cond[None], "b_cond": tot["dmod"].reshape(1, -1)}
    for (n, kind, shape), g in zip(BIG, gsh):
        grads[n] = g.reshape(a[n].shape)
    for n in ("b_in", "ssm_lambda_re", "ssm_lambda_im", "ssm_log_dt", "ssm_b_re", "ssm_b_im", "ssm_c_re", "ssm_c_im",
              "ssm_d", "cv_dw_b", "cv_ln_g", "cv_ln_b", "ln1_g", "ln1_b", "ffn_dw_b", "ln2_g", "ln2_b"):
        grads[n] = tot[n].reshape(a[n].shape)
    wcv = CONV_WIDTH // N_CHIP
    grads["cv_dw_w"] = lax.dynamic_slice(tot["cv_dw_w"].reshape(CONV_KERNEL, CONV_WIDTH), (0, s_me * wcv),
                                         (CONV_KERNEL, wcv)).reshape(a["cv_dw_w"].shape)
    wff = 2 * FFN_HIDDEN // N_CHIP
    grads["ffn_dw_w"] = lax.dynamic_slice(tot["ffn_dw_w"].reshape(FFN_KERNEL, 2 * FFN_HIDDEN), (0, s_me * wff),
                                          (FFN_KERNEL, wff)).reshape(a["ffn_dw_w"].shape)

    delta, new_m, new_v = {}, {}, {}
    for n in ["w_cond"] + [n for n, _, _ in BIG]:
        d, nm_, nv_ = _adamw("adamw_" + n, a[n][0], grads[n][0], a["m_" + n][0], a["v_" + n][0])
        delta[n], new_m[n], new_v[n] = d[None], nm_[None], nv_[None]
    upd = [n for n, _ in SMALL_UPD]
    d, nm_, nv_ = _adamw("adamw_small", _pack({n: a[n] for n in upd}, SMALL_UPD), _pack({n: grads[n] for n in upd}, SMALL_UPD),
                         _pack({n: a["m_" + n] for n in upd}, SMALL_UPD), _pack({n: a["v_" + n] for n in upd}, SMALL_UPD))
    for dst, flat in ((delta, d), (new_m, nm_), (new_v, nv_)):
        for n, val in _unpack(flat, SMALL_UPD).items():
            dst[n] = val.reshape(a[n].shape)

    loss = tot["loss"].reshape(())
    return (loss, gx[None], *[grads[n] for n in WEIGHTS], *[delta[n] for n in WEIGHTS],
            *[new_m[n] for n in WEIGHTS], *[new_v[n] for n in WEIGHTS])
```

```python
import functools
import math

import jax
import jax.numpy as jnp
from jax import lax
from jax.experimental import pallas as pl
from jax.experimental.pallas import tpu as pltpu

F32 = jnp.float32
BF16 = jnp.bfloat16

D_MODEL = 1024
SSM_WIDTH = 512
SSM_GROUP = 16
SSM_GROUPS = 32
SSM_STATE = 64
CONV_WIDTH = 512
CONV_KERNEL = 31
FFN_HIDDEN = 2816
FFN_KERNEL = 3
IN_PROJ_WIDTH = 3584
N_COND = 6
ALPHA = 2.0 ** 0.25
LN_EPS = 1e-5
ADAM_LR, ADAM_B1, ADAM_B2, ADAM_EPS, ADAM_WD, ADAM_STEP = 0.001, 0.9, 0.999, 1e-08, 0.01, 10

N_DEV = 8
N_CHIP = 4
LANES = 128
SSM_CHUNK = 16
LANE_GROUPS = LANES // SSM_GROUP
N_LANE_BLOCKS = SSM_WIDTH // LANES
STATE_COLS = LANE_GROUPS * SSM_STATE
CHUNK_COLS = SSM_CHUNK * LANES
CONV_HALO = 32
VMEM_LIMIT = 56 * 1024 * 1024
MESH = pl.DeviceIdType.MESH

BIG = (
    ("w_in", "col", (D_MODEL, IN_PROJ_WIDTH)),
    ("ssm_glu_w_a", "col", (SSM_WIDTH, D_MODEL)),
    ("ssm_glu_w_b", "col", (SSM_WIDTH, D_MODEL)),
    ("cv_w_pw", "col", (CONV_WIDTH, D_MODEL)),
    ("w_out", "row", (D_MODEL, D_MODEL)),
    ("ffn_w_up", "col", (D_MODEL, 2 * FFN_HIDDEN)),
    ("ffn_w_down", "row", (FFN_HIDDEN, D_MODEL)),
)

EARLY = (0,)
MID = (1, 2, 3, 4)
LATE = (5, 6)
DIRECT = MID + LATE

WEIGHTS = ['w_cond', 'b_cond', 'w_in', 'b_in', 'ssm_lambda_re', 'ssm_lambda_im', 'ssm_log_dt', 'ssm_b_re', 'ssm_b_im',
           'ssm_c_re', 'ssm_c_im', 'ssm_d', 'ssm_glu_w_a', 'ssm_glu_w_b', 'cv_dw_w', 'cv_dw_b', 'cv_ln_g', 'cv_ln_b',
           'cv_w_pw', 'w_out', 'ln1_g', 'ln1_b', 'ffn_w_up', 'ffn_dw_w', 'ffn_dw_b', 'ffn_w_down', 'ln2_g', 'ln2_b']
INPUTS = ['x', 'c'] + WEIGHTS + ['loss_target'] + ['m_' + n for n in WEIGHTS] + ['v_' + n for n in WEIGHTS]

PACK = (
    ("dmod", N_COND * D_MODEL), ("c_act", D_MODEL), ("b_in", IN_PROJ_WIDTH),
    ("ssm_lambda_re", SSM_GROUPS * SSM_STATE), ("ssm_lambda_im", SSM_GROUPS * SSM_STATE), ("ssm_log_dt", SSM_GROUPS),
    ("ssm_b_re", SSM_GROUPS * SSM_STATE * SSM_GROUP), ("ssm_b_im", SSM_GROUPS * SSM_STATE * SSM_GROUP),
    ("ssm_c_re", SSM_GROUPS * SSM_STATE * SSM_GROUP), ("ssm_c_im", SSM_GROUPS * SSM_STATE * SSM_GROUP),
    ("ssm_d", SSM_GROUPS * SSM_GROUP), ("cv_dw_w", CONV_KERNEL * CONV_WIDTH), ("cv_dw_b", CONV_WIDTH),
    ("cv_ln_g", CONV_WIDTH), ("cv_ln_b", CONV_WIDTH), ("ln1_g", D_MODEL), ("ln1_b", D_MODEL),
    ("ffn_dw_w", FFN_KERNEL * 2 * FFN_HIDDEN), ("ffn_dw_b", 2 * FFN_HIDDEN), ("ln2_g", D_MODEL), ("ln2_b", D_MODEL),
    ("loss", 1),
)
PACK_COLS = 1024
PACK_ROWS = 192
assert sum(n for _, n in PACK) <= PACK_ROWS * PACK_COLS


def _params(sem=None, **kw):
    return pltpu.CompilerParams(dimension_semantics=sem, vmem_limit_bytes=VMEM_LIMIT, **kw)


def _ln_stats(x):
    mu = jnp.mean(x, axis=-1, keepdims=True)
    xc = x - mu
    var = jnp.mean(xc * xc, axis=-1, keepdims=True)
    rstd = lax.rsqrt(var + LN_EPS)
    return xc * rstd, rstd


def _ln_bwd(dxhat, xhat, rstd):
    m1 = jnp.mean(dxhat, axis=-1, keepdims=True)
    m2 = jnp.mean(dxhat * xhat, axis=-1, keepdims=True)
    return rstd * (dxhat - m1 - xhat * m2)


def _sig(x):
    return 1.0 / (1.0 + jnp.exp(-x))


def _gelu(x):
    return 0.5 * x * (1.0 + lax.erf(x * (1.0 / math.sqrt(2.0))))


def _dgelu(x):
    return 0.5 * (1.0 + lax.erf(x * (1.0 / math.sqrt(2.0)))) + x * jnp.exp(-0.5 * x * x) * (1.0 / math.sqrt(2.0 * math.pi))


def _gelu_and_grad(x):
    er = lax.erf(x * (1.0 / math.sqrt(2.0)))
    cdf = 0.5 * (1.0 + er)
    return x * cdf, cdf + x * jnp.exp(-0.5 * x * x) * (1.0 / math.sqrt(2.0 * math.pi))


def _colsum(a):
    return jnp.sum(a, axis=0, keepdims=True)


def _fill_rotations(buf, rot, rows):
    for r in range(1, 8):
        rot[r - 1] = buf[pl.ds(r, rows), :]


def _rows_at(buf, rot, offset, tb):
    q, r = divmod(offset, 8)
    if r == 0:
        return buf[pl.ds(8 * q, tb), :]
    return rot[r - 1, pl.ds(8 * q, tb), :]


def _dot(a, b):
    return jnp.dot(a, b, preferred_element_type=F32)


def _dot_nt(a, b):
    return lax.dot_general(a, b, (((1,), (1,)), ((), ())), preferred_element_type=F32)


def _dot_tn(a, b):
    return lax.dot_general(a, b, (((0,), (0,)), ((), ())), preferred_element_type=F32)


def _load_once(src, dst, sem):
    cp = pltpu.make_async_copy(src, dst, sem)
    cp.start()
    cp.wait()


def _full(a):
    nd = a.ndim
    return pl.BlockSpec(a.shape, lambda *_: (0,) * nd)


ANY = pl.BlockSpec(memory_space=pl.ANY)


def _place():
    x, y, c = lax.axis_index("x"), lax.axis_index("y"), lax.axis_index("c")
    chips = [(1 - x, y), (x, 1 - y), (1 - x, 1 - y)]
    return x, y, c, chips


def _piece(kind, shape):
    r, cc = shape
    return (r // 2, cc // N_CHIP) if kind == "col" else (r // (2 * N_CHIP), cc)


def _piece_at(ref, kind, shape, s, k):
    pr, pc = _piece(kind, shape)
    if kind == "col":
        return ref.at[pl.ds(k * pr, pr), pl.ds(pl.multiple_of(s * pc, LANES), pc)]
    return ref.at[pl.ds(pl.multiple_of((2 * s + k) * pr, 16), pr), :]


def _gather_start(idx, sh, full, send, recv):
    x, y, c, chips = _place()
    for i, m in enumerate(idx):
        _, kind, shape = BIG[m]
        pr, _ = _piece(kind, shape)
        for j, chip in enumerate(chips):
            pltpu.make_async_remote_copy(
                src_ref=sh[i].at[pl.ds(pl.multiple_of(c * pr, 16), pr), :], dst_ref=_piece_at(full[i], kind, shape, 2 * x + y, c),
                send_sem=send.at[i, j], recv_sem=recv.at[i, j], device_id=(*chip, c), device_id_type=MESH).start()


def _gather_finish(idx, sh, full, send, recv, fsend, frecv):
    x, y, c, chips = _place()
    sibling = (x, y, 1 - c)
    waits = []
    for i, m in enumerate(idx):
        _, kind, shape = BIG[m]
        pr, _ = _piece(kind, shape)
        for j, (cx, cy) in enumerate(chips):
            got = _piece_at(full[i], kind, shape, 2 * cx + cy, c)
            first = pltpu.make_async_remote_copy(
                src_ref=sh[i].at[pl.ds(pl.multiple_of(c * pr, 16), pr), :], dst_ref=got, send_sem=send.at[i, j],
                recv_sem=recv.at[i, j], device_id=(cx, cy, c), device_id_type=MESH)
            first.wait_recv()
            fwd = pltpu.make_async_remote_copy(src_ref=got, dst_ref=got, send_sem=fsend.at[i, j], recv_sem=frecv.at[i, j],
                                               device_id=sibling, device_id_type=MESH)
            fwd.start()
            waits += [first.wait_send, fwd.wait_send]
    for i, m in enumerate(idx):
        _, kind, shape = BIG[m]
        for j, (cx, cy) in enumerate(chips):
            got = _piece_at(full[i], kind, shape, 2 * cx + cy, 1 - c)
            pltpu.make_async_remote_copy(src_ref=got, dst_ref=got, send_sem=fsend.at[i, j], recv_sem=frecv.at[i, j],
                                         device_id=sibling, device_id_type=MESH).wait_recv()
    for w in waits:
        w()


def _scatter(idx, dw, got, send, recv):
    x, y, c, _ = _place()
    cps = []
    for i, m in enumerate(idx):
        _, kind, shape = BIG[m]
        for r in range(1, N_DEV):
            tx, ty, tc = (1 - x if r & 4 else x), (1 - y if r & 2 else y), (1 - c if r & 1 else c)
            cps.append(pltpu.make_async_remote_copy(
                src_ref=_piece_at(dw[i], kind, shape, 2 * tx + ty, tc), dst_ref=got[i].at[r - 1],
                send_sem=send.at[i, r - 1], recv_sem=recv.at[i, r - 1], device_id=(tx, ty, tc), device_id_type=MESH))
    return cps


def _f1_inproj(x, modv, b_in, w_in, mid_sh, mid_full, tb):
    t = x.shape[0]
    nt = t // tb
    nl = len(MID)
    chunks = [(j * 512, 512) for j in range(IN_PROJ_WIDTH // 512)]

    def body(x_ref, modv_ref, b_ref, w_hbm, *rest):
        sh, full = rest[:nl], rest[2 * nl:3 * nl]
        u4_ref, prest_ref, h_ref, w_v, sem, send, recv, fsend, frecv = rest[3 * nl:]

        @pl.when(pl.program_id(0) == 0)
        def _():
            _gather_start(MID, sh, full, send, recv)
            _load_once(w_hbm, w_v, sem)

        xn, _ = _ln_stats(x_ref[...])
        h = (xn * (1.0 + modv_ref[1:2, :]) + modv_ref[0:1, :]).astype(BF16)
        h_ref[...] = h
        for c0, cw in chunks:
            p = _dot(h, w_v[:, c0:c0 + cw]) + b_ref[:, c0:c0 + cw]
            if c0 == 0:
                for b in range(N_LANE_BLOCKS):
                    u4_ref[b] = p[:, b * LANES:(b + 1) * LANES]
            else:
                prest_ref[:, c0 - SSM_WIDTH:c0 - SSM_WIDTH + cw] = p

        @pl.when(pl.program_id(0) == nt - 1)
        def _():
            _gather_finish(MID, sh, full, send, recv, fsend, frecv)

    gsem = pltpu.SemaphoreType.DMA((nl, 3))
    return pl.pallas_call(
        body, name="f1_inproj", grid=(nt,),
        in_specs=[pl.BlockSpec((tb, D_MODEL), lambda i: (i, 0)), _full(modv), _full(b_in), ANY] + [ANY] * (2 * nl),
        out_specs=[ANY] * nl + [pl.BlockSpec((N_LANE_BLOCKS, tb, LANES), lambda i: (0, i, 0)),
                                pl.BlockSpec((tb, IN_PROJ_WIDTH - SSM_WIDTH), lambda i: (i, 0)),
                                pl.BlockSpec((tb, D_MODEL), lambda i: (i, 0))],
        input_output_aliases={4 + nl + k: k for k in range(nl)},
        out_shape=[jax.ShapeDtypeStruct(f.shape, f.dtype) for f in mid_full]
        + [jax.ShapeDtypeStruct((N_LANE_BLOCKS, t, LANES), F32),
                   jax.ShapeDtypeStruct((t, IN_PROJ_WIDTH - SSM_WIDTH), F32),
                   jax.ShapeDtypeStruct((t, D_MODEL), BF16)],
        scratch_shapes=[pltpu.VMEM(w_in.shape, BF16), pltpu.SemaphoreType.DMA, gsem, gsem, gsem, gsem],
        compiler_params=_params(("arbitrary",)),
    )(x, modv, b_in, w_in, *mid_sh, *mid_full)


def _s5_build(lam_re, lam_im, log_dt, b_re, b_im, c_re, c_im, d):
    el, g, n, p, nb = SSM_CHUNK, SSM_GROUPS, SSM_STATE, SSM_GROUP, N_LANE_BLOCKS
    lr = jnp.minimum(lam_re, -1e-4)
    li = lam_im
    dt = jnp.exp(log_dt)[:, None]
    mag = jnp.exp(lr * dt)
    ang = li * dt
    lbr, lbi = mag * jnp.cos(ang), mag * jnp.sin(ang)
    num_r, num_i = lbr - 1.0, lbi
    den = lr * lr + li * li
    coef_r = (num_r * lr + num_i * li) / den
    coef_i = (num_i * lr - num_r * li) / den
    bbar_r = coef_r[..., None] * b_re - coef_i[..., None] * b_im
    bbar_i = coef_r[..., None] * b_im + coef_i[..., None] * b_re
    k = jnp.arange(el + 1, dtype=F32)[:, None, None]
    pmag = jnp.exp(k * (lr * dt)[None])
    pr, pi = pmag * jnp.cos(k * ang[None]), pmag * jnp.sin(k * ang[None])
    car = c_re[None] * pr[:, :, None, :] - c_im[None] * pi[:, :, None, :]
    cai = c_re[None] * pi[:, :, None, :] + c_im[None] * pr[:, :, None, :]
    bt_r = bbar_r.transpose(0, 2, 1)[None]
    bt_i = bbar_i.transpose(0, 2, 1)[None]
    kern = jnp.sum(car[:el, :, None, :, :] * bt_r[:, :, :, None, :] - cai[:el, :, None, :, :] * bt_i[:, :, :, None, :],
                   axis=-1)
    kern = kern.at[0].add(jnp.eye(p, dtype=F32)[None] * d[:, None, :])
    kc = kern.reshape(el, g * p, p)
    rev = el - 1 - jnp.arange(el)
    qr, qi = pr[rev][:, :, None, :], pi[rev][:, :, None, :]
    sw_r = (qr * bt_r - qi * bt_i).reshape(el, g * p, n)
    sw_i = (qr * bt_i + qi * bt_r).reshape(el, g * p, n)
    sg_r = car[1:].reshape(el, g * p, n)
    sg_i = (-cai[1:]).reshape(el, g * p, n)
    a = jnp.stack([pr[el].reshape(nb, LANE_GROUPS * n), pi[el].reshape(nb, LANE_GROUPS * n)], axis=1)
    return kc, sw_r, sw_i, sg_r, sg_i, a


def _expand(src, reps):
    rows, w = src.shape
    cols = reps * w
    r = lax.broadcasted_iota(jnp.int32, (w, cols), 0)
    c = lax.broadcasted_iota(jnp.int32, (w, cols), 1)
    rep = (r == (c & (w - 1))).astype(BF16)
    out = _dot(src.astype(BF16), rep)
    rg = lax.broadcasted_iota(jnp.int32, (rows, cols), 0) // SSM_GROUP
    cg = lax.broadcasted_iota(jnp.int32, (rows, cols), 1) // w
    return jnp.where(rg == cg, out, 0.0).astype(BF16)


def _fold(x, w):
    rows, cols = x.shape
    rg = lax.broadcasted_iota(jnp.int32, (rows, cols), 0) // SSM_GROUP
    cg = lax.broadcasted_iota(jnp.int32, (rows, cols), 1) // w
    x = jnp.where(rg == cg, x, 0.0)
    while cols > LANES:
        x = x[:, :cols // 2] + x[:, cols // 2:]
        cols //= 2
    s = LANES // 2
    while s >= w:
        x = x + pltpu.roll(x, s, axis=1)
        s //= 2
    return x[:, :w]


def _build_maps(s_ref, dst):
    for j in range(SSM_CHUNK):
        dst[j * LANES:(j + 1) * LANES, :] = _expand(s_ref[j], LANE_GROUPS)


def _build_toeplitz(kc_ref, dst):
    dst[...] = jnp.zeros_like(dst)
    for d in range(SSM_CHUNK):
        blk = _expand(kc_ref[d], LANE_GROUPS)
        for ji in range(SSM_CHUNK - d):
            jo = ji + d
            dst[ji * LANES:(ji + 1) * LANES, jo * LANES:(jo + 1) * LANES] = blk


def _cblk(w):
    return pl.BlockSpec((SSM_CHUNK, LANES, w), lambda b: (0, b, 0))


def _tblk(t):
    return pl.BlockSpec((1, t, LANES), lambda b: (b, 0, 0))


def _load_chunks(ref, nc):
    return jnp.concatenate([ref[0, pl.ds(j, nc, stride=SSM_CHUNK), :] for j in range(SSM_CHUNK)], axis=-1).astype(BF16)


def _store_chunks(ref, val, nc):
    for j in range(SSM_CHUNK):
        ref[0, pl.ds(j, nc, stride=SSM_CHUNK), :] = val[:, j * LANES:(j + 1) * LANES]


def _s5a_state(u4, sw_r, sw_i, a8):
    nb, t, _ = u4.shape
    nc = t // SSM_CHUNK
    sc = STATE_COLS

    def body(u_ref, swr_ref, swi_ref, a_ref, hr_ref, hi_ref, w_s, xr_s, xi_s):
        u = _load_chunks(u_ref, nc)
        _build_maps(swr_ref, w_s)
        xr_s[...] = _dot(u, w_s[...])
        _build_maps(swi_ref, w_s)
        xi_s[...] = _dot(u, w_s[...])
        ar = a_ref[0, 0:1, :]
        ai = a_ref[0, 1:2, :]

        def step(c, carry):
            hr, hi = carry
            hr_ref[0, pl.ds(c, 1), :] = hr
            hi_ref[0, pl.ds(c, 1), :] = hi
            xr = xr_s[pl.ds(c, 1), :]
            xi = xi_s[pl.ds(c, 1), :]
            return ar * hr - ai * hi + xr, ar * hi + ai * hr + xi

        z = jnp.zeros((1, sc), F32)
        lax.fori_loop(0, nc, step, (z, z))

    return pl.pallas_call(
        body, name="s5a_state", grid=(nb,),
        in_specs=[_tblk(t), _cblk(SSM_STATE), _cblk(SSM_STATE),
                  pl.BlockSpec((1, 8, sc), lambda b: (b, 0, 0))],
        out_specs=[pl.BlockSpec((1, nc, sc), lambda b: (b, 0, 0))] * 2,
        out_shape=[jax.ShapeDtypeStruct((nb, nc, sc), F32)] * 2,
        scratch_shapes=[pltpu.VMEM((CHUNK_COLS, sc), BF16), pltpu.VMEM((nc, sc), F32), pltpu.VMEM((nc, sc), F32)],
        compiler_params=_params(("arbitrary",)),
    )(u4, sw_r, sw_i, a8)


def _s5b_out(u4, kc, sg_r, sg_i, hr, hi):
    nb, t, _ = u4.shape
    nc = t // SSM_CHUNK
    sc = STATE_COLS
    cw = 512

    def body(u_ref, kc_ref, sgr_ref, sgi_ref, hr_ref, hi_ref, y_ref, tm_s, gr_s, gi_s):
        _build_toeplitz(kc_ref, tm_s)
        _build_maps(sgr_ref, gr_s)
        _build_maps(sgi_ref, gi_s)
        u = _load_chunks(u_ref, nc)
        h_r = hr_ref[0].astype(BF16)
        h_i = hi_ref[0].astype(BF16)
        for j in range(CHUNK_COLS // cw):
            cs = slice(j * cw, (j + 1) * cw)
            y = _dot(u, tm_s[:, cs]) + _dot_nt(h_r, gr_s[cs, :]) + _dot_nt(h_i, gi_s[cs, :])
            for q in range(cw // LANES):
                step = j * (cw // LANES) + q
                y_ref[0, pl.ds(step, nc, stride=SSM_CHUNK), :] = y[:, q * LANES:(q + 1) * LANES]

    return pl.pallas_call(
        body, name="s5b_out", grid=(nb,),
        in_specs=[_tblk(t), _cblk(SSM_GROUP), _cblk(SSM_STATE),
                  _cblk(SSM_STATE), pl.BlockSpec((1, nc, sc), lambda b: (b, 0, 0)),
                  pl.BlockSpec((1, nc, sc), lambda b: (b, 0, 0))],
        out_specs=_tblk(t),
        out_shape=jax.ShapeDtypeStruct((nb, t, LANES), F32),
        scratch_shapes=[pltpu.VMEM((CHUNK_COLS, CHUNK_COLS), BF16), pltpu.VMEM((CHUNK_COLS, sc), BF16),
                        pltpu.VMEM((CHUNK_COLS, sc), BF16)],
        compiler_params=_params(("arbitrary",)),
    )(u4, kc, sg_r, sg_i, hr, hi)


def _f4_mixer(ys4, prest, x, modv, cvv, cw32, w_a, w_b, w_pw, w_out, late_sh, late_full, tb):
    t = x.shape[0]
    hb = tb // CONV_HALO
    nt = t // tb
    nl = len(LATE)

    def body(ys_ref, pr_ref, halo_ref, x_ref, modv_ref, cvv_ref, cw_ref, wa_ref, wb_ref, wpw_ref, wout_ref, *rest):
        sh, full = rest[:nl], rest[2 * nl:3 * nl]
        r1_ref, ya_ref, yb_ref, ycv_ref, vc_ref, yg_ref, vs_ref, mg_ref, vbuf, vrot, send, recv, fsend, frecv = rest[3 * nl:]
        i = pl.program_id(0)

        @pl.when(i == 0)
        def _():
            _gather_start(LATE, sh, full, send, recv)

        ys = jnp.concatenate([ys_ref[b] for b in range(N_LANE_BLOCKS)], axis=-1)
        yg = _gelu(ys).astype(BF16)
        yg_ref[...] = yg
        ya = _dot(yg, wa_ref[...])
        yb = _dot(yg, wb_ref[...])
        ya_ref[...] = ya.astype(BF16)
        yb_ref[...] = yb.astype(BF16)
        yssm = ya * _sig(yb)
        hv = halo_ref[:, 0:CONV_WIDTH] * _sig(halo_ref[:, CONV_WIDTH:2 * CONV_WIDTH])
        vbuf[0:CONV_HALO, :] = jnp.where(i == 0, 0.0, hv)
        vbuf[CONV_HALO:, :] = pr_ref[:, 0:CONV_WIDTH] * _sig(pr_ref[:, CONV_WIDTH:2 * CONV_WIDTH])
        _fill_rotations(vbuf, vrot, tb + CONV_HALO - 8)
        acc = jnp.zeros((tb, CONV_WIDTH), F32)
        for k in range(CONV_KERNEL):
            acc += _rows_at(vbuf, vrot, CONV_HALO - CONV_KERNEL + 1 + k, tb) * cw_ref[k:k + 1, :]
        vc = acc + cvv_ref[0:1, :]
        vc_ref[...] = vc
        xh, _ = _ln_stats(vc)
        vl = xh * cvv_ref[1:2, :] + cvv_ref[2:3, :]
        vs = (vl * _sig(vl)).astype(BF16)
        vs_ref[...] = vs
        ycv = _dot(vs, wpw_ref[...])
        ycv_ref[...] = ycv.astype(BF16)
        gs = pr_ref[:, 2 * CONV_WIDTH:2 * CONV_WIDTH + D_MODEL]
        gc = pr_ref[:, 2 * CONV_WIDTH + D_MODEL:]
        merged = (_sig(gs) * yssm + _sig(gc) * ycv).astype(BF16)
        mg_ref[...] = merged
        ym = _dot(merged, wout_ref[...])
        r1_ref[...] = ALPHA * x_ref[...] + modv_ref[2:3, :] * ym

        @pl.when(i == nt - 1)
        def _():
            _gather_finish(LATE, sh, full, send, recv, fsend, frecv)

    tok = lambda w: pl.BlockSpec((tb, w), lambda i: (i, 0))
    sem = pltpu.SemaphoreType.DMA((nl, 3))
    n_in = 11
    return pl.pallas_call(
        body, name="f4_mixer", grid=(nt,),
        in_specs=[pl.BlockSpec((N_LANE_BLOCKS, tb, LANES), lambda i: (0, i, 0)), tok(prest.shape[1]),
                  pl.BlockSpec((CONV_HALO, 2 * CONV_WIDTH), lambda i: (jnp.maximum(i * hb - 1, 0), 0)),
                  tok(D_MODEL), _full(modv), _full(cvv), _full(cw32), _full(w_a), _full(w_b), _full(w_pw), _full(w_out)]
        + [ANY] * (2 * nl),
        out_specs=[ANY] * nl + [tok(D_MODEL), tok(D_MODEL), tok(D_MODEL), tok(D_MODEL), tok(CONV_WIDTH), tok(SSM_WIDTH),
                                tok(CONV_WIDTH), tok(D_MODEL)],
        input_output_aliases={n_in + nl + k: k for k in range(nl)},
        out_shape=[jax.ShapeDtypeStruct(f.shape, f.dtype) for f in late_full]
        + [jax.ShapeDtypeStruct((t, D_MODEL), F32), jax.ShapeDtypeStruct((t, D_MODEL), BF16),
                   jax.ShapeDtypeStruct((t, D_MODEL), BF16), jax.ShapeDtypeStruct((t, D_MODEL), BF16),
                   jax.ShapeDtypeStruct((t, CONV_WIDTH), F32), jax.ShapeDtypeStruct((t, SSM_WIDTH), BF16),
                   jax.ShapeDtypeStruct((t, CONV_WIDTH), BF16), jax.ShapeDtypeStruct((t, D_MODEL), BF16)],
        scratch_shapes=[pltpu.VMEM((tb + CONV_HALO, CONV_WIDTH), F32),
                        pltpu.VMEM((7, tb + CONV_HALO - 8, CONV_WIDTH), F32), sem, sem, sem, sem],
        compiler_params=_params(("arbitrary",)),
    )(ys4, prest, prest, x, modv, cvv, cw32, w_a, w_b, w_pw, w_out, *late_sh, *late_full)


FFN_COLS = 1408


def _f5_ffn(r1, tgt, modv, lnv, fdw, w_up, w_down, tb):
    t = r1.shape[0]
    fw = 2 * FFN_HIDDEN

    def body(r1_ref, tgt_ref, modv_ref, lnv_ref, fdw_ref, wup_hbm, wdn_hbm,
             dr2_ref, d_ref, up_ref, z_ref, acc_ref, wup_v, wdn_v, upbuf, gbuf, hbuf, sems):
        i = pl.program_id(0)

        @pl.when(i == 0)
        def _():
            _load_once(wup_hbm, wup_v, sems.at[0])
            _load_once(wdn_hbm, wdn_v, sems.at[1])
            acc_ref[...] = jnp.zeros_like(acc_ref)
            upbuf[0:8, :] = jnp.zeros((8, fw), F32)

        xh1, _ = _ln_stats(r1_ref[...])
        x1 = xh1 * lnv_ref[0:1, :] + lnv_ref[1:2, :]
        xn2, _ = _ln_stats(x1)
        h2 = (xn2 * (1.0 + modv_ref[4:5, :]) + modv_ref[3:4, :]).astype(BF16)
        for j in range(fw // FFN_COLS):
            cs = slice(j * FFN_COLS, (j + 1) * FFN_COLS)
            up = _dot(h2, wup_v[:, cs])
            upbuf[8:, cs] = up
            up_ref[:, cs] = up.astype(BF16)

        def conv(cs):
            return (fdw_ref[0:1, cs] * upbuf[pl.ds(6, tb), cs] + fdw_ref[1:2, cs] * upbuf[pl.ds(7, tb), cs]
                    + fdw_ref[2:3, cs] * upbuf[pl.ds(8, tb), cs] + fdw_ref[3:4, cs])

        halves = [(slice(j * FFN_COLS, (j + 1) * FFN_COLS),
                   slice(FFN_HIDDEN + j * FFN_COLS, FFN_HIDDEN + (j + 1) * FFN_COLS)) for j in range(FFN_HIDDEN // FFN_COLS)]
        yf = jnp.zeros((tb, D_MODEL), F32)
        for ca, cv in halves:
            v = conv(cv)
            g, dg = _gelu_and_grad(conv(ca))
            gbuf[:, ca] = g.astype(BF16)
            hbuf[:, ca] = (v * dg).astype(BF16)
            z = (g * v).astype(BF16)
            z_ref[:, ca] = z
            yf += _dot(z, wdn_v[ca, :])
        r2 = ALPHA * x1 + modv_ref[5:6, :] * yf
        xh2, rstd2 = _ln_stats(r2)
        e = xh2 * lnv_ref[2:3, :] + lnv_ref[3:4, :] - tgt_ref[...]
        dx2 = e * (1.0 / D_MODEL)
        acc_ref[3:4, :] += _colsum(e * e) * (0.5 / D_MODEL)
        acc_ref[0:1, :] += _colsum(dx2 * xh2)
        acc_ref[1:2, :] += _colsum(dx2)
        dr2 = _ln_bwd(dx2 * lnv_ref[2:3, :], xh2, rstd2)
        dr2_ref[...] = dr2
        acc_ref[2:3, :] += _colsum(dr2 * yf)
        dyf = (modv_ref[5:6, :] * dr2).astype(BF16)
        for ca, cv in halves:
            dz = _dot_nt(dyf, wdn_v[ca, :])
            d_ref[:, ca] = (dz * hbuf[:, ca].astype(F32)).astype(BF16)
            d_ref[:, cv] = (dz * gbuf[:, ca].astype(F32)).astype(BF16)
        upbuf[0:8, :] = upbuf[pl.ds(tb, 8), :]

    tok = lambda w: pl.BlockSpec((tb, w), lambda i: (i, 0))
    return pl.pallas_call(
        body, name="f5_ffn", grid=(t // tb,),
        in_specs=[tok(D_MODEL), tok(D_MODEL), _full(modv), _full(lnv), _full(fdw), ANY, ANY],
        out_specs=[tok(D_MODEL), tok(fw), tok(fw), tok(FFN_HIDDEN), pl.BlockSpec((8, D_MODEL), lambda i: (0, 0))],
        out_shape=[jax.ShapeDtypeStruct((t, D_MODEL), F32), jax.ShapeDtypeStruct((t, fw), BF16),
                   jax.ShapeDtypeStruct((t, fw), BF16), jax.ShapeDtypeStruct((t, FFN_HIDDEN), BF16),
                   jax.ShapeDtypeStruct((8, D_MODEL), F32)],
        scratch_shapes=[pltpu.VMEM(w_up.shape, BF16), pltpu.VMEM(w_down.shape, BF16),
                        pltpu.VMEM((tb + 8, fw), F32), pltpu.VMEM((tb, FFN_HIDDEN), BF16),
                        pltpu.VMEM((tb, FFN_HIDDEN), BF16), pltpu.SemaphoreType.DMA((2,))],
        compiler_params=_params(("arbitrary",)),
    )(r1, tgt, modv, lnv, fdw, w_up, w_down)


def _b1b_ffn_up(d, up, dr2, r1, modv, lnv, fdw, w_up, tb):
    t = dr2.shape[0]
    fw = 2 * FFN_HIDDEN
    nt = t // tb
    hb = tb // 16

    def body(d_ref, nxt_ref, up_ref, dr2_ref, r1_ref, modv_ref, lnv_ref, fdw_ref, wup_hbm, dup_ref, dr1_ref, h2_ref,
             dyf_ref, acc_ref, accw_ref, wup_v, dbuf, shifted, sem):
        i = pl.program_id(0)

        @pl.when(i == 0)
        def _():
            _load_once(wup_hbm, wup_v, sem)
            acc_ref[...] = jnp.zeros_like(acc_ref)
            accw_ref[...] = jnp.zeros_like(accw_ref)

        dbuf[0:tb, :] = d_ref[...].astype(F32)
        dbuf[tb:, :] = jnp.where(i == nt - 1, 0.0, nxt_ref[...].astype(F32))
        dh2 = jnp.zeros((tb, D_MODEL), F32)
        for j in range(fw // FFN_COLS):
            cs = slice(j * FFN_COLS, (j + 1) * FFN_COLS)
            for k in range(1, FFN_KERNEL):
                shifted[k - 1] = dbuf[pl.ds(k, tb), cs]
            ds = [dbuf[pl.ds(0, tb), cs], shifted[0], shifted[1]]
            dup = (fdw_ref[2:3, cs] * ds[0] + fdw_ref[1:2, cs] * ds[1] + fdw_ref[0:1, cs] * ds[2]).astype(BF16)
            dup_ref[:, cs] = dup
            dh2 += _dot_nt(dup, wup_v[:, cs])
            upf = up_ref[:, cs].astype(F32)
            for k in range(FFN_KERNEL):
                accw_ref[k:k + 1, cs] += _colsum(ds[FFN_KERNEL - 1 - k] * upf)
            accw_ref[3:4, cs] += _colsum(ds[0])
        xh1, rstd1 = _ln_stats(r1_ref[...])
        x1 = xh1 * lnv_ref[0:1, :] + lnv_ref[1:2, :]
        xn2, rstd2 = _ln_stats(x1)
        h2_ref[...] = (xn2 * (1.0 + modv_ref[4:5, :]) + modv_ref[3:4, :]).astype(BF16)
        dr2 = dr2_ref[...]
        dyf_ref[...] = (modv_ref[5:6, :] * dr2).astype(BF16)
        acc_ref[0:1, :] += _colsum(dh2 * xn2)
        acc_ref[1:2, :] += _colsum(dh2)
        dx1 = _ln_bwd(dh2 * (1.0 + modv_ref[4:5, :]), xn2, rstd2) + ALPHA * dr2
        acc_ref[2:3, :] += _colsum(dx1 * xh1)
        acc_ref[3:4, :] += _colsum(dx1)
        dr1_ref[...] = _ln_bwd(dx1 * lnv_ref[0:1, :], xh1, rstd1)

    tok = lambda w: pl.BlockSpec((tb, w), lambda i: (i, 0))
    return pl.pallas_call(
        body, name="b1b_ffn_up", grid=(nt,),
        in_specs=[tok(fw), pl.BlockSpec((16, fw), lambda i: (jnp.minimum((i + 1) * hb, t // 16 - 1), 0)), tok(fw),
                  tok(D_MODEL), tok(D_MODEL), _full(modv), _full(lnv), _full(fdw), ANY],
        out_specs=[tok(fw), tok(D_MODEL), tok(D_MODEL), tok(D_MODEL), pl.BlockSpec((8, D_MODEL), lambda i: (0, 0)),
                   pl.BlockSpec((8, fw), lambda i: (0, 0))],
        out_shape=[jax.ShapeDtypeStruct((t, fw), BF16), jax.ShapeDtypeStruct((t, D_MODEL), F32),
                   jax.ShapeDtypeStruct((t, D_MODEL), BF16), jax.ShapeDtypeStruct((t, D_MODEL), BF16),
                   jax.ShapeDtypeStruct((8, D_MODEL), F32), jax.ShapeDtypeStruct((8, fw), F32)],
        scratch_shapes=[pltpu.VMEM(w_up.shape, BF16), pltpu.VMEM((tb + 16, fw), F32),
                        pltpu.VMEM((FFN_KERNEL - 1, tb, FFN_COLS), F32), pltpu.SemaphoreType.DMA],
        compiler_params=_params(("arbitrary",)),
    )(d, d, up, dr2, r1, modv, lnv, fdw, w_up)


def _b2_mixer(dr1, ys4, prest, ya, yb, ycv, vc, merged, modv, cvv, cw32, w_a, w_b, w_pw, w_out, late_dw, tb):
    t = dr1.shape[0]
    nt = t // tb
    nl = len(LATE)
    hb = tb // CONV_HALO
    cwd = CONV_WIDTH

    def body(dr1_ref, ys_ref, pr_ref, halo_ref, ya_ref, yb_ref, ycv_ref, vc_ref, mg_ref, modv_ref, cvv_ref, cw_ref,
             wa_ref, wb_ref, wpw_ref, wout_ref, *rest):
        dw, got = rest[:nl], rest[nl:2 * nl]
        (dys_ref, dpr_ref, dya_ref, dyb_ref, dycv_ref, dym_ref, acc_a, acc_b, acc_w, vbuf, dvbuf, vrot, dvrot,
         send, recv) = rest[2 * nl:]
        i = pl.program_id(0)
        ti = nt - 1 - i

        @pl.when(i == 0)
        def _():
            for cp in _scatter(LATE, dw, got, send, recv):
                cp.start()
            acc_a[...] = jnp.zeros_like(acc_a)
            acc_b[...] = jnp.zeros_like(acc_b)
            acc_w[...] = jnp.zeros_like(acc_w)
            dvbuf[pl.ds(tb, CONV_HALO), :] = jnp.zeros((CONV_HALO, cwd), F32)

        dr1 = dr1_ref[...]
        dym = (modv_ref[2:3, :] * dr1).astype(BF16)
        dym_ref[...] = dym
        ym = _dot(mg_ref[...], wout_ref[...])
        acc_a[0:1, :] += _colsum(dr1 * ym)
        dmg = _dot_nt(dym, wout_ref[...])
        sgs = _sig(pr_ref[:, 2 * cwd:2 * cwd + D_MODEL])
        sgc = _sig(pr_ref[:, 2 * cwd + D_MODEL:])
        ya_v = ya_ref[...].astype(F32)
        syb = _sig(yb_ref[...].astype(F32))
        ycv_v = ycv_ref[...].astype(F32)
        dpr_ref[:, 2 * cwd:2 * cwd + D_MODEL] = (dmg * (ya_v * syb) * sgs * (1.0 - sgs)).astype(BF16)
        dpr_ref[:, 2 * cwd + D_MODEL:] = (dmg * ycv_v * sgc * (1.0 - sgc)).astype(BF16)
        dyssm = dmg * sgs
        dya = (dyssm * syb).astype(BF16)
        dyb = (dyssm * ya_v * syb * (1.0 - syb)).astype(BF16)
        dya_ref[...] = dya
        dyb_ref[...] = dyb
        dyg = _dot_nt(dya, wa_ref[...]) + _dot_nt(dyb, wb_ref[...])
        ys = jnp.concatenate([ys_ref[b] for b in range(N_LANE_BLOCKS)], axis=-1)
        dys = dyg * _dgelu(ys)
        for b in range(N_LANE_BLOCKS):
            dys_ref[b] = dys[:, b * LANES:(b + 1) * LANES]
        dycv = (dmg * sgc).astype(BF16)
        dycv_ref[...] = dycv
        dvs = _dot_nt(dycv, wpw_ref[...])
        xh, rstd = _ln_stats(vc_ref[...])
        vl = xh * cvv_ref[1:2, :] + cvv_ref[2:3, :]
        s = _sig(vl)
        dvl = dvs * s * (1.0 + vl * (1.0 - s))
        acc_b[1:2, :] += _colsum(dvl * xh)
        acc_b[2:3, :] += _colsum(dvl)
        dvc = _ln_bwd(dvl * cvv_ref[1:2, :], xh, rstd)
        acc_b[0:1, :] += _colsum(dvc)
        hv = halo_ref[:, 0:cwd] * _sig(halo_ref[:, cwd:2 * cwd])
        vbuf[0:CONV_HALO, :] = jnp.where(ti == 0, 0.0, hv)
        cva = pr_ref[:, 0:cwd]
        scg = _sig(pr_ref[:, cwd:2 * cwd])
        vbuf[CONV_HALO:, :] = cva * scg
        dvbuf[0:tb, :] = dvc
        _fill_rotations(vbuf, vrot, tb + CONV_HALO - 8)
        _fill_rotations(dvbuf, dvrot, tb + CONV_HALO - 8)
        dv = jnp.zeros((tb, cwd), F32)
        for k in range(CONV_KERNEL):
            dv += _rows_at(dvbuf, dvrot, CONV_KERNEL - 1 - k, tb) * cw_ref[k:k + 1, :]
            acc_w[k:k + 1, :] += _colsum(dvc * _rows_at(vbuf, vrot, CONV_HALO - CONV_KERNEL + 1 + k, tb))
        dvbuf[pl.ds(tb, CONV_HALO), :] = dvbuf[0:CONV_HALO, :]
        dpr_ref[:, 0:cwd] = (dv * scg).astype(BF16)
        dpr_ref[:, cwd:2 * cwd] = (dv * cva * scg * (1.0 - scg)).astype(BF16)

        @pl.when(i == nt - 1)
        def _():
            for cp in _scatter(LATE, dw, got, send, recv):
                cp.wait()

    rtok = lambda w: pl.BlockSpec((tb, w), lambda i: (nt - 1 - i, 0))
    r4 = pl.BlockSpec((N_LANE_BLOCKS, tb, LANES), lambda i: (0, nt - 1 - i, 0))
    pw = prest.shape[1]
    return pl.pallas_call(
        body, name="b2_mixer", grid=(nt,),
        in_specs=[rtok(D_MODEL), r4, rtok(pw),
                  pl.BlockSpec((CONV_HALO, 2 * cwd), lambda i: (jnp.maximum((nt - 1 - i) * hb - 1, 0), 0)),
                  rtok(D_MODEL), rtok(D_MODEL), rtok(D_MODEL), rtok(cwd), rtok(D_MODEL),
                  _full(modv), _full(cvv), _full(cw32), _full(w_a), _full(w_b), _full(w_pw), _full(w_out)] + [ANY] * nl,
        out_specs=[ANY] * nl + [r4, rtok(pw), rtok(D_MODEL), rtok(D_MODEL), rtok(D_MODEL), rtok(D_MODEL),
                   pl.BlockSpec((8, D_MODEL), lambda i: (0, 0)), pl.BlockSpec((8, cwd), lambda i: (0, 0)),
                   pl.BlockSpec((CONV_HALO, cwd), lambda i: (0, 0))],
        out_shape=[jax.ShapeDtypeStruct((N_DEV - 1,) + _piece(*BIG[m][1:]), BF16) for m in LATE]
        + [jax.ShapeDtypeStruct((N_LANE_BLOCKS, t, LANES), F32), jax.ShapeDtypeStruct((t, pw), BF16),
                   jax.ShapeDtypeStruct((t, D_MODEL), BF16), jax.ShapeDtypeStruct((t, D_MODEL), BF16),
                   jax.ShapeDtypeStruct((t, D_MODEL), BF16), jax.ShapeDtypeStruct((t, D_MODEL), BF16),
                   jax.ShapeDtypeStruct((8, D_MODEL), F32), jax.ShapeDtypeStruct((8, cwd), F32),
                   jax.ShapeDtypeStruct((CONV_HALO, cwd), F32)],
        scratch_shapes=[pltpu.VMEM((tb + CONV_HALO, cwd), F32), pltpu.VMEM((tb + CONV_HALO, cwd), F32),
                        pltpu.VMEM((7, tb + CONV_HALO - 8, cwd), F32), pltpu.VMEM((7, tb + CONV_HALO - 8, cwd), F32),
                        pltpu.SemaphoreType.DMA((nl, N_DEV - 1)), pltpu.SemaphoreType.DMA((nl, N_DEV - 1))],
        compiler_params=_params(("arbitrary",)),
    )(dr1, ys4, prest, prest, ya, yb, ycv, vc, merged, modv, cvv, cw32, w_a, w_b, w_pw, w_out, *late_dw)


def _s5c_state_bwd(dy4, sg_r, sg_i, a8, hr, hi):
    nb, t, _ = dy4.shape
    nc = t // SSM_CHUNK
    sc = STATE_COLS

    def body(dy_ref, sgr_ref, sgi_ref, a_ref, hr_ref, hi_ref, dxr_ref, dxi_ref, da_ref, dsgr_ref, dsgi_ref,
             g_s, lr_s, li_s, xr_s, xi_s):
        dy = _load_chunks(dy_ref, nc)
        _build_maps(sgr_ref, g_s)
        lr_s[...] = _dot(dy, g_s[...])
        _build_maps(sgi_ref, g_s)
        li_s[...] = _dot(dy, g_s[...])
        ar = a_ref[0, 0:1, :]
        ai = a_ref[0, 1:2, :]

        def step(k, carry):
            pr, pi, dar, dai = carry
            c = nc - 1 - k
            xr_s[pl.ds(c, 1), :] = pr
            xi_s[pl.ds(c, 1), :] = pi
            h_r = hr_ref[0, pl.ds(c, 1), :]
            h_i = hi_ref[0, pl.ds(c, 1), :]
            dar = dar + pr * h_r + pi * h_i
            dai = dai - pr * h_i + pi * h_r
            nr = lr_s[pl.ds(c, 1), :] + ar * pr + ai * pi
            ni = li_s[pl.ds(c, 1), :] - ai * pr + ar * pi
            return nr, ni, dar, dai

        z = jnp.zeros((1, sc), F32)
        _, _, dar, dai = lax.fori_loop(0, nc, step, (z, z, z, z))
        da_ref[0] = jnp.concatenate([dar, dai, jnp.zeros((6, sc), F32)], axis=0)
        dxr_ref[0] = xr_s[...].astype(BF16)
        dxi_ref[0] = xi_s[...].astype(BF16)
        for h_ref, o_ref in ((hr_ref, dsgr_ref), (hi_ref, dsgi_ref)):
            hb = h_ref[0].astype(BF16)
            for j in range(SSM_CHUNK):
                o_ref[j] = _fold(_dot_tn(dy[:, j * LANES:(j + 1) * LANES], hb), SSM_STATE)

    blk = lambda r, c: pl.BlockSpec((1, r, c), lambda b: (b, 0, 0))
    return pl.pallas_call(
        body, name="s5c_state_bwd", grid=(nb,),
        in_specs=[_tblk(t), _cblk(SSM_STATE), _cblk(SSM_STATE), blk(8, sc), blk(nc, sc), blk(nc, sc)],
        out_specs=[blk(nc, sc), blk(nc, sc), blk(8, sc), _cblk(SSM_STATE), _cblk(SSM_STATE)],
        out_shape=[jax.ShapeDtypeStruct((nb, nc, sc), BF16), jax.ShapeDtypeStruct((nb, nc, sc), BF16),
                   jax.ShapeDtypeStruct((nb, 8, sc), F32),
                   jax.ShapeDtypeStruct((SSM_CHUNK, SSM_WIDTH, SSM_STATE), F32),
                   jax.ShapeDtypeStruct((SSM_CHUNK, SSM_WIDTH, SSM_STATE), F32)],
        scratch_shapes=[pltpu.VMEM((CHUNK_COLS, sc), BF16)] + [pltpu.VMEM((nc, sc), F32)] * 4,
        compiler_params=_params(("arbitrary",)),
    )(dy4, sg_r, sg_i, a8, hr, hi)


def _s5d_input_bwd(dy4, u4, kc, sw_r, sw_i, dxr, dxi, mid_dw):
    nb, t, _ = dy4.shape
    nc = t // SSM_CHUNK
    sc = STATE_COLS
    nl = len(MID)

    def body(dy_ref, u_ref, kc_ref, swr_ref, swi_ref, dxr_ref, dxi_ref, *rest):
        dw, got = rest[:nl], rest[nl:2 * nl]
        du_ref, dkc_ref, dswr_ref, dswi_ref, tm_s, w_s, dk_s, send, recv = rest[2 * nl:]

        @pl.when(pl.program_id(0) == 0)
        def _():
            for cp in _scatter(MID, dw, got, send, recv):
                cp.start()

        dy = _load_chunks(dy_ref, nc)
        u = _load_chunks(u_ref, nc)
        _build_toeplitz(kc_ref, tm_s)
        du = _dot_nt(dy, tm_s[...])
        _build_maps(swr_ref, w_s)
        du += _dot_nt(dxr_ref[0], w_s[...])
        _build_maps(swi_ref, w_s)
        du += _dot_nt(dxi_ref[0], w_s[...])
        _store_chunks(du_ref, du, nc)
        dk_s[...] = jnp.zeros_like(dk_s)
        for ji in range(SSM_CHUNK):
            uj = u[:, ji * LANES:(ji + 1) * LANES]
            rows = _dot_tn(uj, dy)
            for jo in range(ji, SSM_CHUNK):
                dk_s[jo - ji] += rows[:, jo * LANES:(jo + 1) * LANES]
            dswr_ref[ji] = _fold(_dot_tn(uj, dxr_ref[0]), SSM_STATE)
            dswi_ref[ji] = _fold(_dot_tn(uj, dxi_ref[0]), SSM_STATE)
        for d in range(SSM_CHUNK):
            dkc_ref[d] = _fold(dk_s[d], SSM_GROUP)

        @pl.when(pl.program_id(0) == nb - 1)
        def _():
            for cp in _scatter(MID, dw, got, send, recv):
                cp.wait()

    blk = lambda r, c: pl.BlockSpec((1, r, c), lambda b: (b, 0, 0))
    ssem = pltpu.SemaphoreType.DMA((nl, N_DEV - 1))
    return pl.pallas_call(
        body, name="s5d_input_bwd", grid=(nb,),
        in_specs=[_tblk(t), _tblk(t), _cblk(SSM_GROUP), _cblk(SSM_STATE), _cblk(SSM_STATE),
                  blk(nc, sc), blk(nc, sc)] + [ANY] * nl,
        out_specs=[ANY] * nl + [_tblk(t), _cblk(SSM_GROUP), _cblk(SSM_STATE), _cblk(SSM_STATE)],
        out_shape=[jax.ShapeDtypeStruct((N_DEV - 1,) + _piece(*BIG[m][1:]), BF16) for m in MID]
        + [jax.ShapeDtypeStruct((nb, t, LANES), F32),
           jax.ShapeDtypeStruct((SSM_CHUNK, SSM_WIDTH, SSM_GROUP), F32),
           jax.ShapeDtypeStruct((SSM_CHUNK, SSM_WIDTH, SSM_STATE), F32),
           jax.ShapeDtypeStruct((SSM_CHUNK, SSM_WIDTH, SSM_STATE), F32)],
        scratch_shapes=[pltpu.VMEM((CHUNK_COLS, CHUNK_COLS), BF16), pltpu.VMEM((CHUNK_COLS, sc), BF16),
                        pltpu.VMEM((SSM_CHUNK, LANES, LANES), F32), ssem, ssem],
        compiler_params=_params(("arbitrary",)),
    )(dy4, u4, kc, sw_r, sw_i, dxr, dxi, *mid_dw)


def _b3_inproj(x, dr1, du4, dprest, modv, w_in, tb):
    t = x.shape[0]
    pw = IN_PROJ_WIDTH - SSM_WIDTH

    def body(x_ref, dr1_ref, du_ref, dpr_ref, modv_ref, w_hbm, gx_ref, dp_ref, acc_ref, accb_ref, w_v, sem):
        @pl.when(pl.program_id(0) == 0)
        def _():
            _load_once(w_hbm, w_v, sem)
            acc_ref[...] = jnp.zeros_like(acc_ref)
            accb_ref[...] = jnp.zeros_like(accb_ref)

        du = jnp.concatenate([du_ref[b] for b in range(N_LANE_BLOCKS)], axis=-1).astype(BF16)
        dpr = dpr_ref[...]
        dp_ref[:, 0:SSM_WIDTH] = du
        dp_ref[:, SSM_WIDTH:] = dpr
        accb_ref[0:1, 0:SSM_WIDTH] += _colsum(du.astype(F32))
        accb_ref[0:1, SSM_WIDTH:] += _colsum(dpr.astype(F32))
        dh = _dot_nt(du, w_v[:, 0:SSM_WIDTH]) + _dot_nt(dpr, w_v[:, SSM_WIDTH:])
        xn, rstd = _ln_stats(x_ref[...])
        acc_ref[0:1, :] += _colsum(dh * xn)
        acc_ref[1:2, :] += _colsum(dh)
        gx_ref[...] = _ln_bwd(dh * (1.0 + modv_ref[1:2, :]), xn, rstd) + ALPHA * dr1_ref[...]

    tok = lambda w: pl.BlockSpec((tb, w), lambda i: (i, 0))
    return pl.pallas_call(
        body, name="b3_inproj", grid=(t // tb,),
        in_specs=[tok(D_MODEL), tok(D_MODEL), pl.BlockSpec((N_LANE_BLOCKS, tb, LANES), lambda i: (0, i, 0)), tok(pw),
                  _full(modv), ANY],
        out_specs=[tok(D_MODEL), tok(IN_PROJ_WIDTH), pl.BlockSpec((8, D_MODEL), lambda i: (0, 0)),
                   pl.BlockSpec((8, IN_PROJ_WIDTH), lambda i: (0, 0))],
        out_shape=[jax.ShapeDtypeStruct((t, D_MODEL), F32), jax.ShapeDtypeStruct((t, IN_PROJ_WIDTH), BF16),
                   jax.ShapeDtypeStruct((8, D_MODEL), F32), jax.ShapeDtypeStruct((8, IN_PROJ_WIDTH), F32)],
        scratch_shapes=[pltpu.VMEM(w_in.shape, BF16), pltpu.SemaphoreType.DMA],
        compiler_params=_params(("arbitrary",)),
    )(x, dr1, du4, dprest, modv, w_in)


TN_ROWS = 2048


def _tn_matmul(name, a, b, tm, tn):
    t, m = a.shape
    n = b.shape[1]
    tt = min(TN_ROWS, t)
    nk = t // tt

    def body(a_ref, b_ref, o_ref, acc):
        k = pl.program_id(2)

        @pl.when(k == 0)
        def _():
            acc[...] = jnp.zeros_like(acc)

        acc[...] += _dot_tn(a_ref[...], b_ref[...])

        @pl.when(k == nk - 1)
        def _():
            o_ref[...] = acc[...].astype(BF16)

    return pl.pallas_call(
        body, name=name, grid=(m // tm, n // tn, nk),
        in_specs=[pl.BlockSpec((tt, tm), lambda i, j, k: (k, i)), pl.BlockSpec((tt, tn), lambda i, j, k: (k, j))],
        out_specs=pl.BlockSpec((tm, tn), lambda i, j, k: (i, j)),
        out_shape=jax.ShapeDtypeStruct((m, n), BF16),
        scratch_shapes=[pltpu.VMEM((tm, tn), F32)],
        compiler_params=_params(("arbitrary", "arbitrary", "arbitrary")),
    )(a, b)


def _local_step(x, tgt, modv, wb, shards, sp, tb=256):
    t = x.shape[0]
    row8 = lambda rows, w: jnp.concatenate([r.reshape(1, w) for r in rows] + [jnp.zeros((8 - len(rows), w), F32)], axis=0)
    lnv = row8([sp["ln1_g"], sp["ln1_b"], sp["ln2_g"], sp["ln2_b"]], D_MODEL)
    cvv = row8([sp["cv_dw_b"], sp["cv_ln_g"], sp["cv_ln_b"]], CONV_WIDTH)
    cw32 = jnp.concatenate([sp["cv_dw_w"].reshape(CONV_KERNEL, CONV_WIDTH), jnp.zeros((1, CONV_WIDTH), F32)], axis=0)
    fdw = row8(list(sp["ffn_dw_w"].reshape(FFN_KERNEL, 2 * FFN_HIDDEN)) + [sp["ffn_dw_b"]], 2 * FFN_HIDDEN)
    b_in = sp["b_in"].reshape(1, IN_PROJ_WIDTH)
    ssm = tuple(sp[k] for k in ("ssm_lambda_re", "ssm_lambda_im", "ssm_log_dt", "ssm_b_re", "ssm_b_im", "ssm_c_re",
                                "ssm_c_im", "ssm_d"))
    (kc, sw_r, sw_i, sg_r, sg_i, a), ssm_vjp = jax.vjp(_s5_build, *ssm)
    a8 = jnp.concatenate([a, jnp.zeros((N_LANE_BLOCKS, 6, STATE_COLS), F32)], axis=1)

    name = lambda m: BIG[m][0]
    *mid_w, u4, prest, h1 = _f1_inproj(x, modv, b_in, wb["w_in"], [shards[m] for m in MID], [wb[name(m)] for m in MID], tb)
    w_a, w_b, w_pw, w_out = mid_w
    hr, hi = _s5a_state(u4, sw_r, sw_i, a8)
    ys4 = _s5b_out(u4, kc, sg_r, sg_i, hr, hi)
    w_up, w_down, r1, ya, yb, ycv, vc, yg, vs, merged = _f4_mixer(
        ys4, prest, x, modv, cvv, cw32, w_a, w_b, w_pw, w_out, [shards[m] for m in LATE], [wb[name(m)] for m in LATE], tb)
    dr2, dconv, up, z, acc5 = _f5_ffn(r1, tgt, modv, lnv, fdw, w_up, w_down, tb)
    dup, dr1, h2, dyf, acc1b, acc1a = _b1b_ffn_up(dconv, up, dr2, r1, modv, lnv, fdw, w_up, tb)
    late_dw = [_tn_matmul("dw_up", h2, dup, 1024, FFN_COLS), _tn_matmul("dw_down", z, dyf, FFN_COLS, 1024)]
    got_up, got_down, dys4, dprest, dya, dyb, dycv, dym, acc2a, acc2b, acc2w = _b2_mixer(
        dr1, ys4, prest, ya, yb, ycv, vc, merged, modv, cvv, cw32, w_a, w_b, w_pw, w_out, late_dw, tb)
    mid_dw = [_tn_matmul("dw_glu_a", yg, dya, 512, 1024), _tn_matmul("dw_glu_b", yg, dyb, 512, 1024),
              _tn_matmul("dw_pw", vs, dycv, 512, 1024), _tn_matmul("dw_out", merged, dym, 1024, 1024)]
    dxr, dxi, da8, dsg_r, dsg_i = _s5c_state_bwd(dys4, sg_r, sg_i, a8, hr, hi)
    *mid_got, du4, dkc, dsw_r, dsw_i = _s5d_input_bwd(dys4, u4, kc, sw_r, sw_i, dxr, dxi, mid_dw)
    dssm = ssm_vjp((dkc, dsw_r, dsw_i, dsg_r, dsg_i, da8[:, 0:2, :]))
    gx, dp, acc3, acc3b = _b3_inproj(x, dr1, du4, dprest, modv, wb["w_in"], tb)
    dbig = [_tn_matmul("dw_in", h1, dp, 1024, 896)] + mid_dw + late_dw
    dmod = jnp.concatenate([acc3[1], acc3[0], acc2a[0], acc1b[1], acc1b[0], acc5[2]])
    small = {
        "dmod": dmod, "b_in": acc3b[0],
        "ssm_lambda_re": dssm[0], "ssm_lambda_im": dssm[1], "ssm_log_dt": dssm[2], "ssm_b_re": dssm[3],
        "ssm_b_im": dssm[4], "ssm_c_re": dssm[5], "ssm_c_im": dssm[6], "ssm_d": dssm[7],
        "cv_dw_w": acc2w[0:CONV_KERNEL], "cv_dw_b": acc2b[0], "cv_ln_g": acc2b[1], "cv_ln_b": acc2b[2],
        "ln1_g": acc1b[2], "ln1_b": acc1b[3], "ffn_dw_w": acc1a[0:FFN_KERNEL], "ffn_dw_b": acc1a[3],
        "ln2_g": acc5[0], "ln2_b": acc5[1], "loss": jnp.sum(acc5[3]).reshape(1),
    }
    return gx, dbig, list(mid_got) + [got_up, got_down], small


def _allgather(name, shard):
    m_per, n = shard.shape

    def body(x_ref, out_ref, send_sems, recv_sems, local_sem):
        x, y, c, chips = _place()
        me, sibling = (x, y, c), (x, y, 1 - c)

        def rows(px, py, pc):
            return out_ref.at[pl.ds((4 * px + 2 * py + pc) * m_per, m_per), :]

        def copy(k, block, to, src=None):
            return pltpu.make_async_remote_copy(
                src_ref=rows(*block) if src is None else src, dst_ref=rows(*block),
                send_sem=send_sems.at[k], recv_sem=recv_sems.at[k], device_id=to, device_id_type=MESH)

        mine = pltpu.make_async_copy(x_ref, rows(*me), local_sem)
        mine.start()
        first = [copy(0, me, sibling, src=x_ref)]
        first += [copy(1 + j, me, (*chip, c), src=x_ref) for j, chip in enumerate(chips)]
        for cp in first:
            cp.start()
        passed = [copy(4 + j, (*chip, c), sibling) for j, chip in enumerate(chips)]
        for j, chip in enumerate(chips):
            copy(1 + j, (*chip, c), me).wait_recv()
            passed[j].start()
        copy(0, sibling, me).wait_recv()
        for j, chip in enumerate(chips):
            copy(4 + j, (*chip, 1 - c), me).wait_recv()
        for cp in first + passed:
            cp.wait_send()
        mine.wait()

    return pl.pallas_call(
        body, name=name,
        out_shape=jax.ShapeDtypeStruct((N_DEV * m_per, n), shard.dtype),
        in_specs=[pl.BlockSpec(memory_space=pltpu.VMEM)],
        out_specs=pl.BlockSpec(memory_space=pltpu.VMEM),
        scratch_shapes=[pltpu.SemaphoreType.DMA((7,)), pltpu.SemaphoreType.DMA((7,)), pltpu.SemaphoreType.DMA],
        compiler_params=_params(),
    )(shard)


def _add_rows(pr):
    return 64 if pr % 64 == 0 else 16


def _gather_weights(shards):
    nm = len(BIG)
    nl = len(DIRECT)

    def body(*refs):
        ins, outs, lsh = refs[:nm], refs[nm:2 * nm], refs[2 * nm:2 * nm + nl]
        stage = refs[2 * nm + nl:3 * nm + nl]
        send, recv, fsend, frecv, lsem = refs[3 * nm + nl:]
        x, y, c, chips = _place()
        s_me = 2 * x + y
        sibling = (x, y, 1 - c)
        pend = []
        for m in range(nm):
            stage[m][...] = ins[m][...].astype(BF16)
        for m, (_, kind, shape) in enumerate(BIG):
            pr, pc = _piece(kind, shape)
            for k in range(2):
                cp = pltpu.make_async_copy(stage[m].at[pl.ds(k * pr, pr), :], _piece_at(outs[m], kind, shape, s_me, k),
                                           lsem.at[m, k])
                cp.start()
                pend.append(cp.wait)
            if m in DIRECT:
                cp = pltpu.make_async_copy(stage[m], lsh[DIRECT.index(m)], lsem.at[m, 2])
                cp.start()
                pend.append(cp.wait)
                continue
            for j, chip in enumerate(chips):
                cp = pltpu.make_async_remote_copy(
                    src_ref=stage[m].at[pl.ds(pl.multiple_of(c * pr, 16), pr), :],
                    dst_ref=_piece_at(outs[m], kind, shape, s_me, c),
                    send_sem=send.at[m, j], recv_sem=recv.at[m, j], device_id=(*chip, c), device_id_type=MESH)
                cp.start()
                pend.append(cp.wait_send)
        for m in EARLY:
            _, kind, shape = BIG[m]
            for j, (cx, cy) in enumerate(chips):
                got = _piece_at(outs[m], kind, shape, 2 * cx + cy, c)
                pltpu.make_async_remote_copy(src_ref=got, dst_ref=got, send_sem=send.at[m, j], recv_sem=recv.at[m, j],
                                             device_id=(cx, cy, c), device_id_type=MESH).wait_recv()
                cp = pltpu.make_async_remote_copy(src_ref=got, dst_ref=got, send_sem=fsend.at[m, j],
                                                  recv_sem=frecv.at[m, j], device_id=sibling, device_id_type=MESH)
                cp.start()
                pend.append(cp.wait_send)
        for m in EARLY:
            _, kind, shape = BIG[m]
            for j, (cx, cy) in enumerate(chips):
                got = _piece_at(outs[m], kind, shape, 2 * cx + cy, 1 - c)
                pltpu.make_async_remote_copy(src_ref=got, dst_ref=got, send_sem=fsend.at[m, j], recv_sem=frecv.at[m, j],
                                             device_id=sibling, device_id_type=MESH).wait_recv()
        for w in pend:
            w()

    sem = lambda *s: pltpu.SemaphoreType.DMA(s)
    res = pl.pallas_call(
        body, name="gather_weights",
        out_shape=[jax.ShapeDtypeStruct(shape, BF16) for _, _, shape in BIG]
        + [jax.ShapeDtypeStruct(shards[m].shape, BF16) for m in DIRECT],
        in_specs=[pl.BlockSpec(memory_space=pltpu.VMEM)] * nm,
        out_specs=[ANY] * (nm + nl),
        scratch_shapes=[pltpu.VMEM(s.shape, BF16) for s in shards] + [sem(nm, 3), sem(nm, 3), sem(nm, 3), sem(nm, 3),
                                                                         sem(nm, 3)],
        compiler_params=_params(),
    )(*shards)
    return res[:nm], dict(zip(DIRECT, res[nm:]))


def _rs1_sibling(grads):
    mats = [BIG[m] for m in EARLY]
    nm = len(mats)

    def body(*refs):
        ins, outs = refs[:nm], refs[nm:2 * nm]
        send, recv = refs[2 * nm:]
        x, y, c, _ = _place()
        cps = []
        for m, (_, kind, shape) in enumerate(mats):
            for s in range(N_CHIP):
                cp = pltpu.make_async_remote_copy(
                    src_ref=_piece_at(ins[m], kind, shape, s, 1 - c), dst_ref=outs[m].at[s],
                    send_sem=send.at[m, s], recv_sem=recv.at[m, s], device_id=(x, y, 1 - c), device_id_type=MESH)
                cp.start()
                cps.append(cp)
        for cp in cps:
            cp.wait()

    sem = lambda *s: pltpu.SemaphoreType.DMA(s)
    return pl.pallas_call(
        body, name="rs1_sibling",
        out_shape=[jax.ShapeDtypeStruct((N_CHIP,) + _piece(kind, shape), BF16) for _, kind, shape in mats],
        in_specs=[ANY] * nm, out_specs=[ANY] * nm,
        scratch_shapes=[sem(nm, N_CHIP), sem(nm, N_CHIP)],
        compiler_params=_params(),
    )(*grads)


def _rs2_chips(grads, halves):
    mats = [BIG[m] for m in EARLY]
    nm = len(mats)

    def body(*refs):
        gin, hin = refs[:nm], refs[nm:2 * nm]
        own, got = refs[2 * nm:3 * nm], refs[3 * nm:4 * nm]
        send, recv, lsem = refs[4 * nm:]
        x, y, c, chips = _place()
        s_me = 2 * x + y
        for m, (_, kind, shape) in enumerate(mats):
            pr, pc = _piece(kind, shape)

            def scoped(a, b, m=m, kind=kind, shape=shape, pr=pr):
                loads = [pltpu.make_async_copy(_piece_at(gin[m], kind, shape, s, c), a.at[s], lsem.at[s])
                         for s in range(N_CHIP)]
                loads.append(pltpu.make_async_copy(hin[m], b, lsem.at[N_CHIP]))
                for cp in loads:
                    cp.start()
                for cp in loads:
                    cp.wait()
                step = _add_rows(pr)
                for s in range(N_CHIP):
                    def add(i, _, s=s):
                        r = pl.ds(pl.multiple_of(i * step, 16), step)
                        a[s, r, :] = (a[s, r, :].astype(F32) + b[s, r, :].astype(F32)).astype(BF16)
                        return 0

                    lax.fori_loop(0, pr // step, add, 0)
                waits = []
                for j, (cx, cy) in enumerate(chips):
                    cp = pltpu.make_async_remote_copy(src_ref=a.at[2 * cx + cy], dst_ref=got[m].at[j], send_sem=send.at[m, j],
                                                      recv_sem=recv.at[m, j], device_id=(cx, cy, c), device_id_type=MESH)
                    cp.start()
                    waits.append(cp.wait_send)
                cp = pltpu.make_async_copy(a.at[s_me], own[m], lsem.at[N_CHIP + 1])
                cp.start()
                waits.append(cp.wait)
                for w in waits:
                    w()

            pl.run_scoped(scoped, pltpu.VMEM((N_CHIP, pr, pc), BF16), pltpu.VMEM((N_CHIP, pr, pc), BF16))
        for m in range(nm):
            for j, (cx, cy) in enumerate(chips):
                pltpu.make_async_remote_copy(src_ref=got[m].at[j], dst_ref=got[m].at[j], send_sem=send.at[m, j],
                                             recv_sem=recv.at[m, j], device_id=(cx, cy, c), device_id_type=MESH).wait_recv()

    sem = lambda *s: pltpu.SemaphoreType.DMA(s)
    pieces = [_piece(kind, shape) for _, kind, shape in mats]
    return pl.pallas_call(
        body, name="rs2_chips",
        out_shape=[jax.ShapeDtypeStruct(p, BF16) for p in pieces] + [jax.ShapeDtypeStruct((3,) + p, BF16) for p in pieces],
        in_specs=[ANY] * (2 * nm), out_specs=[ANY] * (2 * nm),
        scratch_shapes=[sem(nm, 3), sem(nm, 3), sem(N_CHIP + 2)],
        compiler_params=_params(),
    )(*grads, *halves)


def _rs3_finish(own, got):
    nm = len(BIG)

    def body(*refs):
        oin, gin = refs[:nm], refs[nm:2 * nm]
        outs = refs[2 * nm:3 * nm]
        send, recv, lsem = refs[3 * nm:]
        x, y, c, _ = _place()
        for m, (_, kind, shape) in enumerate(BIG):
            pr, pc = _piece(kind, shape)
            ng = got[m].shape[0]

            def scoped(a, g, f, m=m, pr=pr, ng=ng, kind=kind, shape=shape):
                mine = _piece_at(oin[m], kind, shape, 2 * x + y, c) if m in DIRECT else oin[m]
                loads = [pltpu.make_async_copy(mine, a, lsem.at[0]), pltpu.make_async_copy(gin[m], g, lsem.at[1])]
                for cp in loads:
                    cp.start()
                for cp in loads:
                    cp.wait()
                step = _add_rows(pr)

                def add(i, _):
                    r = pl.ds(pl.multiple_of(i * step, 16), step)
                    acc = a[r, :].astype(F32)
                    for q in range(ng):
                        acc = acc + g[q, r, :].astype(F32)
                    f[r, :] = acc
                    return 0

                lax.fori_loop(0, pr // step, add, 0)
                dst = outs[m].at[pl.ds(pl.multiple_of(c * pr, 8), pr), :]
                local = pltpu.make_async_copy(f, dst, lsem.at[2])
                local.start()
                cp = pltpu.make_async_remote_copy(src_ref=f, dst_ref=dst, send_sem=send.at[m], recv_sem=recv.at[m],
                                                  device_id=(x, y, 1 - c), device_id_type=MESH)
                cp.start()
                cp.wait_send()
                local.wait()

            pl.run_scoped(scoped, pltpu.VMEM((pr, pc), BF16), pltpu.VMEM((ng, pr, pc), BF16), pltpu.VMEM((pr, pc), F32))
        for m, (_, kind, shape) in enumerate(BIG):
            pr, pc = _piece(kind, shape)
            dst = outs[m].at[pl.ds(pl.multiple_of((1 - c) * pr, 8), pr), :]
            pltpu.make_async_remote_copy(src_ref=dst, dst_ref=dst, send_sem=send.at[m], recv_sem=recv.at[m],
                                         device_id=(x, y, 1 - c), device_id_type=MESH).wait_recv()

    sem = lambda *s: pltpu.SemaphoreType.DMA(s)
    pieces = [_piece(kind, shape) for _, kind, shape in BIG]
    return pl.pallas_call(
        body, name="rs3_finish",
        out_shape=[jax.ShapeDtypeStruct((2 * pr, pc), F32) for pr, pc in pieces],
        in_specs=[ANY] * (2 * nm), out_specs=[ANY] * nm,
        scratch_shapes=[sem(nm), sem(nm), sem(3)],
        compiler_params=_params(),
    )(*own, *got)


def _cond_fwd(c_all, w_shard, b_shard):
    def body(c_ref, w_ref, b_ref, act_ref, mod_ref):
        cv = c_ref[...]
        act = cv * _sig(cv)
        act_ref[...] = act
        mod_ref[...] = _dot(act.astype(BF16), w_ref[...].astype(BF16)) + b_ref[...]

    return pl.pallas_call(
        body, name="cond_fwd",
        out_shape=[jax.ShapeDtypeStruct(c_all.shape, F32), jax.ShapeDtypeStruct((c_all.shape[0], w_shard.shape[1]), F32)],
        compiler_params=_params(),
    )(c_all, w_shard, b_shard)


def _cond_bwd(act_t, dmod_shard):
    k, n = act_t.shape[0], dmod_shard.shape[1]

    def body(a_ref, d_ref, o_ref):
        acc = a_ref[:, 0:1] * d_ref[0:1, :]
        for e in range(1, N_DEV):
            acc += a_ref[:, e:e + 1] * d_ref[e:e + 1, :]
        o_ref[...] = acc

    tr = 256
    return pl.pallas_call(
        body, name="cond_bwd", grid=(k // tr,),
        in_specs=[pl.BlockSpec((tr, N_DEV), lambda i: (i, 0)), _full(dmod_shard)],
        out_specs=pl.BlockSpec((tr, n), lambda i: (i, 0)),
        out_shape=jax.ShapeDtypeStruct((k, n), F32),
        compiler_params=_params(("arbitrary",)),
    )(act_t, dmod_shard)


def _sum_blocks(allp):
    def body(a_ref, o_ref):
        acc = a_ref[0:PACK_ROWS, :]
        for d in range(1, N_DEV):
            acc += a_ref[d * PACK_ROWS:(d + 1) * PACK_ROWS, :]
        o_ref[...] = acc

    return pl.pallas_call(
        body, name="sum_small", out_shape=jax.ShapeDtypeStruct((PACK_ROWS, PACK_COLS), F32), compiler_params=_params(),
    )(allp)


def _adamw(name, w, g, m, v):
    r, cc = w.shape
    tr = r
    for cand in (256, 128, 64, 32, 16, 8):
        if r % cand == 0:
            tr = cand
            break
    bc1 = 1.0 - ADAM_B1 ** ADAM_STEP
    bc2 = 1.0 - ADAM_B2 ** ADAM_STEP

    def body(w_ref, g_ref, m_ref, v_ref, d_ref, nm_ref, nv_ref):
        gv = g_ref[...]
        nm = ADAM_B1 * m_ref[...] + (1.0 - ADAM_B1) * gv
        nv = ADAM_B2 * v_ref[...] + (1.0 - ADAM_B2) * (gv * gv)
        nm_ref[...] = nm
        nv_ref[...] = nv
        d_ref[...] = -ADAM_LR * ((nm / bc1) / (jnp.sqrt(nv / bc2) + ADAM_EPS) + ADAM_WD * w_ref[...])

    spec = pl.BlockSpec((tr, cc), lambda i: (i, 0))
    return pl.pallas_call(
        body, name=name, grid=(r // tr,), in_specs=[spec] * 4, out_specs=[spec] * 3,
        out_shape=[jax.ShapeDtypeStruct((r, cc), F32)] * 3, compiler_params=_params(("arbitrary",)),
    )(w, g, m, v)


def _adamw_small(ws, gs, ms, vs):
    n = len(ws)
    bc1 = 1.0 - ADAM_B1 ** ADAM_STEP
    bc2 = 1.0 - ADAM_B2 ** ADAM_STEP

    def body(*refs):
        w, g, m, v = (refs[k * n:(k + 1) * n] for k in range(4))
        d, nm, nv = (refs[(4 + k) * n:(5 + k) * n] for k in range(3))
        for i in range(n):
            gv = g[i][...]
            m1 = ADAM_B1 * m[i][...] + (1.0 - ADAM_B1) * gv
            v1 = ADAM_B2 * v[i][...] + (1.0 - ADAM_B2) * (gv * gv)
            nm[i][...] = m1
            nv[i][...] = v1
            d[i][...] = -ADAM_LR * ((m1 / bc1) / (jnp.sqrt(v1 / bc2) + ADAM_EPS) + ADAM_WD * w[i][...])

    shapes = [jax.ShapeDtypeStruct(x.shape, F32) for x in ws]
    res = pl.pallas_call(body, name="adamw_small", out_shape=shapes * 3, compiler_params=_params())(*ws, *gs, *ms, *vs)
    return res[:n], res[n:2 * n], res[2 * n:]


def _pack(fields, layout):
    parts = [fields[name].reshape(-1).astype(F32) if name in fields else jnp.zeros((n,), F32) for name, n in layout]
    used = sum(n for _, n in layout)
    parts.append(jnp.zeros((PACK_ROWS * PACK_COLS - used,), F32))
    return jnp.concatenate(parts).reshape(PACK_ROWS, PACK_COLS)


def _unpack(flat, layout):
    flat = flat.reshape(-1)
    out, o = {}, 0
    for name, n in layout:
        out[name] = flat[o:o + n]
        o += n
    return out


def kernel(x, c, w_cond, b_cond, w_in, b_in, ssm_lambda_re, ssm_lambda_im, ssm_log_dt, ssm_b_re, ssm_b_im, ssm_c_re, ssm_c_im, ssm_d, ssm_glu_w_a, ssm_glu_w_b, cv_dw_w, cv_dw_b, cv_ln_g, cv_ln_b, cv_w_pw, w_out, ln1_g, ln1_b, ffn_w_up, ffn_dw_w, ffn_dw_b, ffn_w_down, ln2_g, ln2_b, loss_target, m_w_cond, m_b_cond, m_w_in, m_b_in, m_ssm_lambda_re, m_ssm_lambda_im, m_ssm_log_dt, m_ssm_b_re, m_ssm_b_im, m_ssm_c_re, m_ssm_c_im, m_ssm_d, m_ssm_glu_w_a, m_ssm_glu_w_b, m_cv_dw_w, m_cv_dw_b, m_cv_ln_g, m_cv_ln_b, m_cv_w_pw, m_w_out, m_ln1_g, m_ln1_b, m_ffn_w_up, m_ffn_dw_w, m_ffn_dw_b, m_ffn_w_down, m_ln2_g, m_ln2_b, v_w_cond, v_b_cond, v_w_in, v_b_in, v_ssm_lambda_re, v_ssm_lambda_im, v_ssm_log_dt, v_ssm_b_re, v_ssm_b_im, v_ssm_c_re, v_ssm_c_im, v_ssm_d, v_ssm_glu_w_a, v_ssm_glu_w_b, v_cv_dw_w, v_cv_dw_b, v_cv_ln_g, v_cv_ln_b, v_cv_w_pw, v_w_out, v_ln1_g, v_ln1_b, v_ffn_w_up, v_ffn_dw_w, v_ffn_dw_b, v_ffn_w_down, v_ln2_g, v_ln2_b):
    given = locals()
    a = {n: given[n] for n in INPUTS}
    xi, yi, ci = lax.axis_index("x"), lax.axis_index("y"), lax.axis_index("c")
    s_me = 2 * xi + yi
    e_me = 4 * xi + 2 * yi + ci

    first = jnp.concatenate([
        jnp.concatenate([a["c"], jnp.zeros((7, D_MODEL), F32)], axis=0),
        jnp.concatenate([a["cv_dw_w"].reshape(-1), a["ffn_dw_w"].reshape(-1)]).reshape(8, D_MODEL)], axis=0)
    first_all = _allgather("gather_c", first).reshape(N_DEV, 16, D_MODEL)
    c_all = first_all[:, 0, :]
    dw_all = first_all[0::2, 8:, :].reshape(N_CHIP, 8 * D_MODEL)
    n_cv = CONV_KERNEL * CONV_WIDTH // N_CHIP
    cv_dw_full = dw_all[:, :n_cv].reshape(N_CHIP, CONV_KERNEL, CONV_WIDTH // N_CHIP).transpose(1, 0, 2) \
        .reshape(CONV_KERNEL, CONV_WIDTH)
    ffn_dw_full = dw_all[:, n_cv:].reshape(N_CHIP, FFN_KERNEL, 2 * FFN_HIDDEN // N_CHIP).transpose(1, 0, 2) \
        .reshape(FFN_KERNEL, 2 * FFN_HIDDEN)
    ncols = N_COND * D_MODEL // N_CHIP
    b_cond_shard = lax.dynamic_slice(a["b_cond"], (0, s_me * ncols), (1, ncols))
    c_act_all, modp = _cond_fwd(c_all, a["w_cond"][0], b_cond_shard)
    modp_all = _allgather("gather_mod", modp).reshape(N_DEV, N_DEV, ncols)[0::2]
    mod_e = lax.dynamic_index_in_dim(modp_all, e_me, axis=1, keepdims=False).reshape(N_COND, D_MODEL)
    modv = jnp.concatenate([mod_e, jnp.zeros((2, D_MODEL), F32)], axis=0)

    full, shards = _gather_weights([a[n][0] for n, _, _ in BIG])
    wb = dict(zip([n for n, _, _ in BIG], full))
    sp = {n: a[n][0] for n in ("b_in", "ssm_lambda_re", "ssm_lambda_im", "ssm_log_dt", "ssm_b_re", "ssm_b_im",
                               "ssm_c_re", "ssm_c_im", "ssm_d", "cv_dw_b", "cv_ln_g", "cv_ln_b", "ln1_g", "ln1_b",
                               "ffn_dw_b", "ln2_g", "ln2_b")}
    sp["cv_dw_w"] = cv_dw_full
    sp["ffn_dw_w"] = ffn_dw_full
    gx, dbig, direct_got, small = _local_step(a["x"][0], a["loss_target"][0], modv, wb, shards, sp)

    small["c_act"] = lax.dynamic_index_in_dim(c_act_all, e_me, axis=0, keepdims=False)
    packed_all = _allgather("gather_small", _pack(small, PACK))
    tot = _unpack(_sum_blocks(packed_all), PACK)
    rows = packed_all.reshape(N_DEV, PACK_ROWS * PACK_COLS)
    dmod_all = rows[:, 0:N_COND * D_MODEL]
    act_all = rows[:, N_COND * D_MODEL:(N_COND + 1) * D_MODEL]
    g_w_cond = _cond_bwd(act_all.T, lax.dynamic_slice(dmod_all, (0, s_me * ncols), (N_DEV, ncols)))

    glist = [dbig[m] for m in EARLY]
    halves = _rs1_sibling(glist)
    r2 = _rs2_chips(glist, halves)
    gsh = _rs3_finish(list(r2[:len(EARLY)]) + [dbig[m] for m in DIRECT], list(r2[len(EARLY):]) + direct_got)

    grads = {"w_cond": g_w_cond[None], "b_cond": tot["dmod"].reshape(1, -1)}
    for (n, kind, shape), g in zip(BIG, gsh):
        grads[n] = g.reshape(a[n].shape)
    for n in ("b_in", "ssm_lambda_re", "ssm_lambda_im", "ssm_log_dt", "ssm_b_re", "ssm_b_im", "ssm_c_re", "ssm_c_im",
              "ssm_d", "cv_dw_b", "cv_ln_g", "cv_ln_b", "ln1_g", "ln1_b", "ffn_dw_b", "ln2_g", "ln2_b"):
        grads[n] = tot[n].reshape(a[n].shape)
    wcv = CONV_WIDTH // N_CHIP
    grads["cv_dw_w"] = lax.dynamic_slice(tot["cv_dw_w"].reshape(CONV_KERNEL, CONV_WIDTH), (0, s_me * wcv),
                                         (CONV_KERNEL, wcv)).reshape(a["cv_dw_w"].shape)
    wff = 2 * FFN_HIDDEN // N_CHIP
    grads["ffn_dw_w"] = lax.dynamic_slice(tot["ffn_dw_w"].reshape(FFN_KERNEL, 2 * FFN_HIDDEN), (0, s_me * wff),
                                          (FFN_KERNEL, wff)).reshape(a["ffn_dw_w"].shape)

    delta, new_m, new_v = {}, {}, {}
    for n in ["w_cond"] + [n for n, _, _ in BIG]:
        d, nm_, nv_ = _adamw("adamw_" + n, a[n][0], grads[n][0], a["m_" + n][0], a["v_" + n][0])
        delta[n], new_m[n], new_v[n] = d[None], nm_[None], nv_[None]
    upd = [n for n in WEIGHTS if n not in delta]
    two_d = lambda t: t.reshape(-1, t.shape[-1])
    outs = _adamw_small([two_d(a[n]) for n in upd], [two_d(grads[n]) for n in upd],
                        [two_d(a["m_" + n]) for n in upd], [two_d(a["v_" + n]) for n in upd])
    for dst, vals in zip((delta, new_m, new_v), outs):
        for n, val in zip(upd, vals):
            dst[n] = val.reshape(a[n].shape)

    loss = tot["loss"].reshape(())
    return (loss, gx[None], *[grads[n] for n in WEIGHTS], *[delta[n] for n in WEIGHTS],
            *[new_m[n] for n in WEIGHTS], *[new_v[n] for n in WEIGHTS])
```

```python
import functools
import math

import jax
import jax.numpy as jnp
from jax import lax
from jax.experimental import pallas as pl
from jax.experimental.pallas import tpu as pltpu

F32 = jnp.float32
BF16 = jnp.bfloat16

D_MODEL = 1024
SSM_WIDTH = 512
SSM_GROUP = 16
SSM_GROUPS = 32
SSM_STATE = 64
CONV_WIDTH = 512
CONV_KERNEL = 31
FFN_HIDDEN = 2816
FFN_KERNEL = 3
IN_PROJ_WIDTH = 3584
N_COND = 6
ALPHA = 2.0 ** 0.25
LN_EPS = 1e-5
ADAM_LR, ADAM_B1, ADAM_B2, ADAM_EPS, ADAM_WD, ADAM_STEP = 0.001, 0.9, 0.999, 1e-08, 0.01, 10

N_DEV = 8
N_CHIP = 4
LANES = 128
SSM_CHUNK = 16
LANE_GROUPS = LANES // SSM_GROUP
N_LANE_BLOCKS = SSM_WIDTH // LANES
STATE_COLS = LANE_GROUPS * SSM_STATE
CHUNK_COLS = SSM_CHUNK * LANES
CONV_HALO = 32
VMEM_LIMIT = 56 * 1024 * 1024
MESH = pl.DeviceIdType.MESH

BIG = (
    ("w_in", "col", (D_MODEL, IN_PROJ_WIDTH)),
    ("ssm_glu_w_a", "col", (SSM_WIDTH, D_MODEL)),
    ("ssm_glu_w_b", "col", (SSM_WIDTH, D_MODEL)),
    ("cv_w_pw", "col", (CONV_WIDTH, D_MODEL)),
    ("w_out", "row", (D_MODEL, D_MODEL)),
    ("ffn_w_up", "col", (D_MODEL, 2 * FFN_HIDDEN)),
    ("ffn_w_down", "row", (FFN_HIDDEN, D_MODEL)),
)

EARLY = (0,)
MID = (1, 2, 3, 4)
LATE = (5, 6)
DIRECT = MID + LATE

WEIGHTS = ['w_cond', 'b_cond', 'w_in', 'b_in', 'ssm_lambda_re', 'ssm_lambda_im', 'ssm_log_dt', 'ssm_b_re', 'ssm_b_im',
           'ssm_c_re', 'ssm_c_im', 'ssm_d', 'ssm_glu_w_a', 'ssm_glu_w_b', 'cv_dw_w', 'cv_dw_b', 'cv_ln_g', 'cv_ln_b',
           'cv_w_pw', 'w_out', 'ln1_g', 'ln1_b', 'ffn_w_up', 'ffn_dw_w', 'ffn_dw_b', 'ffn_w_down', 'ln2_g', 'ln2_b']
INPUTS = ['x', 'c'] + WEIGHTS + ['loss_target'] + ['m_' + n for n in WEIGHTS] + ['v_' + n for n in WEIGHTS]

PACK = (
    ("dmod", N_COND * D_MODEL), ("c_act", D_MODEL), ("b_in", IN_PROJ_WIDTH),
    ("ssm_lambda_re", SSM_GROUPS * SSM_STATE), ("ssm_lambda_im", SSM_GROUPS * SSM_STATE), ("ssm_log_dt", SSM_GROUPS),
    ("ssm_b_re", SSM_GROUPS * SSM_STATE * SSM_GROUP), ("ssm_b_im", SSM_GROUPS * SSM_STATE * SSM_GROUP),
    ("ssm_c_re", SSM_GROUPS * SSM_STATE * SSM_GROUP), ("ssm_c_im", SSM_GROUPS * SSM_STATE * SSM_GROUP),
    ("ssm_d", SSM_GROUPS * SSM_GROUP), ("cv_dw_w", CONV_KERNEL * CONV_WIDTH), ("cv_dw_b", CONV_WIDTH),
    ("cv_ln_g", CONV_WIDTH), ("cv_ln_b", CONV_WIDTH), ("ln1_g", D_MODEL), ("ln1_b", D_MODEL),
    ("ffn_dw_w", FFN_KERNEL * 2 * FFN_HIDDEN), ("ffn_dw_b", 2 * FFN_HIDDEN), ("ln2_g", D_MODEL), ("ln2_b", D_MODEL),
    ("loss", 1),
)
PACK_COLS = 1024
PACK_ROWS = 192
assert sum(n for _, n in PACK) <= PACK_ROWS * PACK_COLS


def _params(sem=None, **kw):
    return pltpu.CompilerParams(dimension_semantics=sem, vmem_limit_bytes=VMEM_LIMIT, **kw)


def _ln_stats(x):
    mu = jnp.mean(x, axis=-1, keepdims=True)
    xc = x - mu
    var = jnp.mean(xc * xc, axis=-1, keepdims=True)
    rstd = lax.rsqrt(var + LN_EPS)
    return xc * rstd, rstd


def _ln_bwd(dxhat, xhat, rstd):
    m1 = jnp.mean(dxhat, axis=-1, keepdims=True)
    m2 = jnp.mean(dxhat * xhat, axis=-1, keepdims=True)
    return rstd * (dxhat - m1 - xhat * m2)


def _sig(x):
    return 1.0 / (1.0 + jnp.exp(-x))


def _gelu(x):
    return 0.5 * x * (1.0 + lax.erf(x * (1.0 / math.sqrt(2.0))))


def _dgelu(x):
    return 0.5 * (1.0 + lax.erf(x * (1.0 / math.sqrt(2.0)))) + x * jnp.exp(-0.5 * x * x) * (1.0 / math.sqrt(2.0 * math.pi))


def _gelu_and_grad(x):
    er = lax.erf(x * (1.0 / math.sqrt(2.0)))
    cdf = 0.5 * (1.0 + er)
    return x * cdf, cdf + x * jnp.exp(-0.5 * x * x) * (1.0 / math.sqrt(2.0 * math.pi))


def _colsum(a):
    return jnp.sum(a, axis=0, keepdims=True)


def _fill_rotations(buf, rot, rows):
    for r in range(1, 8):
        rot[r - 1] = buf[pl.ds(r, rows), :]


def _rows_at(buf, rot, offset, tb):
    q, r = divmod(offset, 8)
    if r == 0:
        return buf[pl.ds(8 * q, tb), :]
    return rot[r - 1, pl.ds(8 * q, tb), :]


def _dot(a, b):
    return jnp.dot(a, b, preferred_element_type=F32)


def _dot_nt(a, b):
    return lax.dot_general(a, b, (((1,), (1,)), ((), ())), preferred_element_type=F32)


def _dot_tn(a, b):
    return lax.dot_general(a, b, (((0,), (0,)), ((), ())), preferred_element_type=F32)


def _load_once(src, dst, sem):
    cp = pltpu.make_async_copy(src, dst, sem)
    cp.start()
    cp.wait()


def _full(a):
    nd = a.ndim
    return pl.BlockSpec(a.shape, lambda *_: (0,) * nd)


ANY = pl.BlockSpec(memory_space=pl.ANY)


def _place():
    x, y, c = lax.axis_index("x"), lax.axis_index("y"), lax.axis_index("c")
    chips = [(1 - x, y), (x, 1 - y), (1 - x, 1 - y)]
    return x, y, c, chips


def _piece(kind, shape):
    r, cc = shape
    return (r // 2, cc // N_CHIP) if kind == "col" else (r // (2 * N_CHIP), cc)


def _piece_at(ref, kind, shape, s, k):
    pr, pc = _piece(kind, shape)
    if kind == "col":
        return ref.at[pl.ds(k * pr, pr), pl.ds(pl.multiple_of(s * pc, LANES), pc)]
    return ref.at[pl.ds(pl.multiple_of((2 * s + k) * pr, 16), pr), :]


def _gather_start(idx, sh, full, send, recv):
    x, y, c, chips = _place()
    for i, m in enumerate(idx):
        _, kind, shape = BIG[m]
        pr, _ = _piece(kind, shape)
        for j, chip in enumerate(chips):
            pltpu.make_async_remote_copy(
                src_ref=sh[i].at[pl.ds(pl.multiple_of(c * pr, 16), pr), :], dst_ref=_piece_at(full[i], kind, shape, 2 * x + y, c),
                send_sem=send.at[i, j], recv_sem=recv.at[i, j], device_id=(*chip, c), device_id_type=MESH).start()


def _gather_finish(idx, sh, full, send, recv, fsend, frecv):
    x, y, c, chips = _place()
    sibling = (x, y, 1 - c)
    waits = []
    for i, m in enumerate(idx):
        _, kind, shape = BIG[m]
        pr, _ = _piece(kind, shape)
        for j, (cx, cy) in enumerate(chips):
            got = _piece_at(full[i], kind, shape, 2 * cx + cy, c)
            first = pltpu.make_async_remote_copy(
                src_ref=sh[i].at[pl.ds(pl.multiple_of(c * pr, 16), pr), :], dst_ref=got, send_sem=send.at[i, j],
                recv_sem=recv.at[i, j], device_id=(cx, cy, c), device_id_type=MESH)
            first.wait_recv()
            fwd = pltpu.make_async_remote_copy(src_ref=got, dst_ref=got, send_sem=fsend.at[i, j], recv_sem=frecv.at[i, j],
                                               device_id=sibling, device_id_type=MESH)
            fwd.start()
            waits += [first.wait_send, fwd.wait_send]
    for i, m in enumerate(idx):
        _, kind, shape = BIG[m]
        for j, (cx, cy) in enumerate(chips):
            got = _piece_at(full[i], kind, shape, 2 * cx + cy, 1 - c)
            pltpu.make_async_remote_copy(src_ref=got, dst_ref=got, send_sem=fsend.at[i, j], recv_sem=frecv.at[i, j],
                                         device_id=sibling, device_id_type=MESH).wait_recv()
    for w in waits:
        w()


def _scatter(idx, dw, got, send, recv):
    x, y, c, _ = _place()
    cps = []
    for i, m in enumerate(idx):
        _, kind, shape = BIG[m]
        for r in range(1, N_DEV):
            tx, ty, tc = (1 - x if r & 4 else x), (1 - y if r & 2 else y), (1 - c if r & 1 else c)
            cps.append(pltpu.make_async_remote_copy(
                src_ref=_piece_at(dw[i], kind, shape, 2 * tx + ty, tc), dst_ref=got[i].at[r - 1],
                send_sem=send.at[i, r - 1], recv_sem=recv.at[i, r - 1], device_id=(tx, ty, tc), device_id_type=MESH))
    return cps


def _f1_inproj(x, modv, b_in, w_in, mid_sh, mid_full, tb):
    t = x.shape[0]
    nt = t // tb
    nl = len(MID)
    chunks = [(j * 512, 512) for j in range(IN_PROJ_WIDTH // 512)]

    def body(x_ref, modv_ref, b_ref, w_hbm, *rest):
        sh, full = rest[:nl], rest[2 * nl:3 * nl]
        u4_ref, prest_ref, h_ref, w_v, sem, send, recv, fsend, frecv = rest[3 * nl:]

        @pl.when(pl.program_id(0) == 0)
        def _():
            _gather_start(MID, sh, full, send, recv)
            _load_once(w_hbm, w_v, sem)

        xn, _ = _ln_stats(x_ref[...])
        h = (xn * (1.0 + modv_ref[1:2, :]) + modv_ref[0:1, :]).astype(BF16)
        h_ref[...] = h
        for c0, cw in chunks:
            p = _dot(h, w_v[:, c0:c0 + cw]) + b_ref[:, c0:c0 + cw]
            if c0 == 0:
                for b in range(N_LANE_BLOCKS):
                    u4_ref[b] = p[:, b * LANES:(b + 1) * LANES]
            else:
                prest_ref[:, c0 - SSM_WIDTH:c0 - SSM_WIDTH + cw] = p

        @pl.when(pl.program_id(0) == nt - 1)
        def _():
            _gather_finish(MID, sh, full, send, recv, fsend, frecv)

    gsem = pltpu.SemaphoreType.DMA((nl, 3))
    return pl.pallas_call(
        body, name="f1_inproj", grid=(nt,),
        in_specs=[pl.BlockSpec((tb, D_MODEL), lambda i: (i, 0)), _full(modv), _full(b_in), ANY] + [ANY] * (2 * nl),
        out_specs=[ANY] * nl + [pl.BlockSpec((N_LANE_BLOCKS, tb, LANES), lambda i: (0, i, 0)),
                                pl.BlockSpec((tb, IN_PROJ_WIDTH - SSM_WIDTH), lambda i: (i, 0)),
                                pl.BlockSpec((tb, D_MODEL), lambda i: (i, 0))],
        input_output_aliases={4 + nl + k: k for k in range(nl)},
        out_shape=[jax.ShapeDtypeStruct(f.shape, f.dtype) for f in mid_full]
        + [jax.ShapeDtypeStruct((N_LANE_BLOCKS, t, LANES), F32),
                   jax.ShapeDtypeStruct((t, IN_PROJ_WIDTH - SSM_WIDTH), F32),
                   jax.ShapeDtypeStruct((t, D_MODEL), BF16)],
        scratch_shapes=[pltpu.VMEM(w_in.shape, BF16), pltpu.SemaphoreType.DMA, gsem, gsem, gsem, gsem],
        compiler_params=_params(("arbitrary",)),
    )(x, modv, b_in, w_in, *mid_sh, *mid_full)


TAP_GROUPS = 8


def _dot_f32(a, b, dims):
    return lax.dot_general(a, b, (dims, ((), ())), precision=lax.Precision.HIGHEST, preferred_element_type=F32)


def _taps_fwd(car, cai, bt_r, bt_i):
    el, g, p, n = SSM_CHUNK, SSM_GROUPS, SSM_GROUP, SSM_STATE

    def body(ar_ref, ai_ref, br_ref, bi_ref, o_ref):
        for gl in range(TAP_GROUPS):
            a_r = jnp.concatenate([ar_ref[k, gl] for k in range(el)], axis=0)
            a_i = jnp.concatenate([ai_ref[k, gl] for k in range(el)], axis=0)
            o_ref[gl] = _dot_f32(br_ref[gl], a_r, ((1,), (1,))) - _dot_f32(bi_ref[gl], a_i, ((1,), (1,)))

    ablk = pl.BlockSpec((el + 1, TAP_GROUPS, p, n), lambda i: (0, i, 0, 0))
    bblk = pl.BlockSpec((TAP_GROUPS, p, n), lambda i: (i, 0, 0))
    return pl.pallas_call(
        body, name="s5_taps", grid=(g // TAP_GROUPS,), in_specs=[ablk, ablk, bblk, bblk],
        out_specs=pl.BlockSpec((TAP_GROUPS, p, el * p), lambda i: (i, 0, 0)),
        out_shape=jax.ShapeDtypeStruct((g, p, el * p), F32), compiler_params=_params(("arbitrary",)),
    )(car, cai, bt_r, bt_i)


def _taps_bwd(dr, car, cai, bt_r, bt_i):
    el, g, p, n = SSM_CHUNK, SSM_GROUPS, SSM_GROUP, SSM_STATE

    def body(dr_ref, ar_ref, ai_ref, br_ref, bi_ref, dar_ref, dai_ref, dbr_ref, dbi_ref):
        for gl in range(TAP_GROUPS):
            dv = dr_ref[gl]
            a_r = jnp.concatenate([ar_ref[k, gl] for k in range(el)], axis=0)
            a_i = jnp.concatenate([ai_ref[k, gl] for k in range(el)], axis=0)
            dbr_ref[gl] = _dot_f32(dv, a_r, ((1,), (0,)))
            dbi_ref[gl] = -_dot_f32(dv, a_i, ((1,), (0,)))
            da_r = _dot_f32(dv, br_ref[gl], ((0,), (0,)))
            da_i = -_dot_f32(dv, bi_ref[gl], ((0,), (0,)))
            for k in range(el):
                dar_ref[k, gl] = da_r[k * p:(k + 1) * p, :]
                dai_ref[k, gl] = da_i[k * p:(k + 1) * p, :]
            dar_ref[el, gl] = jnp.zeros((p, n), F32)
            dai_ref[el, gl] = jnp.zeros((p, n), F32)

    ablk = pl.BlockSpec((el + 1, TAP_GROUPS, p, n), lambda i: (0, i, 0, 0))
    bblk = pl.BlockSpec((TAP_GROUPS, p, n), lambda i: (i, 0, 0))
    return pl.pallas_call(
        body, name="s5_taps_bwd", grid=(g // TAP_GROUPS,),
        in_specs=[pl.BlockSpec((TAP_GROUPS, p, el * p), lambda i: (i, 0, 0)), ablk, ablk, bblk, bblk],
        out_specs=[ablk, ablk, bblk, bblk],
        out_shape=[jax.ShapeDtypeStruct(car.shape, F32), jax.ShapeDtypeStruct(car.shape, F32),
                   jax.ShapeDtypeStruct(bt_r.shape, F32), jax.ShapeDtypeStruct(bt_r.shape, F32)],
        compiler_params=_params(("arbitrary",)),
    )(dr, car, cai, bt_r, bt_i)


@jax.custom_vjp
def _taps(car, cai, bt_r, bt_i):
    return _taps_fwd(car, cai, bt_r, bt_i)


_taps.defvjp(lambda *ops: (_taps_fwd(*ops), ops), lambda ops, dr: _taps_bwd(dr, *ops))


def _s5_build(lam_re, lam_im, log_dt, b_re, b_im, c_re, c_im, d):
    el, g, n, p, nb = SSM_CHUNK, SSM_GROUPS, SSM_STATE, SSM_GROUP, N_LANE_BLOCKS
    lr = jnp.minimum(lam_re, -1e-4)
    li = lam_im
    dt = jnp.exp(log_dt)[:, None]
    mag = jnp.exp(lr * dt)
    ang = li * dt
    lbr, lbi = mag * jnp.cos(ang), mag * jnp.sin(ang)
    num_r, num_i = lbr - 1.0, lbi
    den = lr * lr + li * li
    coef_r = (num_r * lr + num_i * li) / den
    coef_i = (num_i * lr - num_r * li) / den
    bbar_r = coef_r[..., None] * b_re - coef_i[..., None] * b_im
    bbar_i = coef_r[..., None] * b_im + coef_i[..., None] * b_re
    k = jnp.arange(el + 1, dtype=F32)[:, None, None]
    pmag = jnp.exp(k * (lr * dt)[None])
    pr, pi = pmag * jnp.cos(k * ang[None]), pmag * jnp.sin(k * ang[None])
    car = c_re[None] * pr[:, :, None, :] - c_im[None] * pi[:, :, None, :]
    cai = c_re[None] * pi[:, :, None, :] + c_im[None] * pr[:, :, None, :]
    bt_r = bbar_r.transpose(0, 2, 1)
    bt_i = bbar_i.transpose(0, 2, 1)
    kern = _taps(car, cai, bt_r, bt_i).reshape(g, p, el, p).transpose(2, 0, 1, 3)
    kern = kern.at[0].add(jnp.eye(p, dtype=F32)[None] * d[:, None, :])
    bt_r, bt_i = bt_r[None], bt_i[None]
    kc = kern.reshape(el, g * p, p)
    rev = el - 1 - jnp.arange(el)
    qr, qi = pr[rev][:, :, None, :], pi[rev][:, :, None, :]
    sw_r = (qr * bt_r - qi * bt_i).reshape(el, g * p, n)
    sw_i = (qr * bt_i + qi * bt_r).reshape(el, g * p, n)
    sg_r = car[1:].reshape(el, g * p, n)
    sg_i = (-cai[1:]).reshape(el, g * p, n)
    a = jnp.stack([pr[el].reshape(nb, LANE_GROUPS * n), pi[el].reshape(nb, LANE_GROUPS * n)], axis=1)
    return kc, sw_r, sw_i, sg_r, sg_i, a


def _expand(src, reps):
    rows, w = src.shape
    cols = reps * w
    r = lax.broadcasted_iota(jnp.int32, (w, cols), 0)
    c = lax.broadcasted_iota(jnp.int32, (w, cols), 1)
    rep = (r == (c & (w - 1))).astype(BF16)
    out = _dot(src.astype(BF16), rep)
    rg = lax.broadcasted_iota(jnp.int32, (rows, cols), 0) // SSM_GROUP
    cg = lax.broadcasted_iota(jnp.int32, (rows, cols), 1) // w
    return jnp.where(rg == cg, out, 0.0).astype(BF16)


def _fold(x, w):
    rows, cols = x.shape
    rg = lax.broadcasted_iota(jnp.int32, (rows, cols), 0) // SSM_GROUP
    cg = lax.broadcasted_iota(jnp.int32, (rows, cols), 1) // w
    x = jnp.where(rg == cg, x, 0.0)
    while cols > LANES:
        x = x[:, :cols // 2] + x[:, cols // 2:]
        cols //= 2
    s = LANES // 2
    while s >= w:
        x = x + pltpu.roll(x, s, axis=1)
        s //= 2
    return x[:, :w]


def _build_maps(s_ref, dst):
    for j in range(SSM_CHUNK):
        dst[j * LANES:(j + 1) * LANES, :] = _expand(s_ref[j], LANE_GROUPS)


def _build_toeplitz(kc_ref, dst):
    dst[...] = jnp.zeros_like(dst)
    for d in range(SSM_CHUNK):
        blk = _expand(kc_ref[d], LANE_GROUPS)
        for ji in range(SSM_CHUNK - d):
            jo = ji + d
            dst[ji * LANES:(ji + 1) * LANES, jo * LANES:(jo + 1) * LANES] = blk


def _cblk(w):
    return pl.BlockSpec((SSM_CHUNK, LANES, w), lambda b: (0, b, 0))


def _tblk(t):
    return pl.BlockSpec((1, t, LANES), lambda b: (b, 0, 0))


def _load_chunks(ref, nc):
    return jnp.concatenate([ref[0, pl.ds(j, nc, stride=SSM_CHUNK), :] for j in range(SSM_CHUNK)], axis=-1).astype(BF16)


def _store_chunks(ref, val, nc):
    for j in range(SSM_CHUNK):
        ref[0, pl.ds(j, nc, stride=SSM_CHUNK), :] = val[:, j * LANES:(j + 1) * LANES]


def _s5a_state(u4, sw_r, sw_i, a8):
    nb, t, _ = u4.shape
    nc = t // SSM_CHUNK
    sc = STATE_COLS

    def body(u_ref, swr_ref, swi_ref, a_ref, hr_ref, hi_ref, w_s, xr_s, xi_s):
        u = _load_chunks(u_ref, nc)
        _build_maps(swr_ref, w_s)
        xr_s[...] = _dot(u, w_s[...])
        _build_maps(swi_ref, w_s)
        xi_s[...] = _dot(u, w_s[...])
        ar = a_ref[0, 0:1, :]
        ai = a_ref[0, 1:2, :]

        def step(c, carry):
            hr, hi = carry
            hr_ref[0, pl.ds(c, 1), :] = hr
            hi_ref[0, pl.ds(c, 1), :] = hi
            xr = xr_s[pl.ds(c, 1), :]
            xi = xi_s[pl.ds(c, 1), :]
            return ar * hr - ai * hi + xr, ar * hi + ai * hr + xi

        z = jnp.zeros((1, sc), F32)
        lax.fori_loop(0, nc, step, (z, z))

    return pl.pallas_call(
        body, name="s5a_state", grid=(nb,),
        in_specs=[_tblk(t), _cblk(SSM_STATE), _cblk(SSM_STATE),
                  pl.BlockSpec((1, 8, sc), lambda b: (b, 0, 0))],
        out_specs=[pl.BlockSpec((1, nc, sc), lambda b: (b, 0, 0))] * 2,
        out_shape=[jax.ShapeDtypeStruct((nb, nc, sc), F32)] * 2,
        scratch_shapes=[pltpu.VMEM((CHUNK_COLS, sc), BF16), pltpu.VMEM((nc, sc), F32), pltpu.VMEM((nc, sc), F32)],
        compiler_params=_params(("arbitrary",)),
    )(u4, sw_r, sw_i, a8)


def _s5b_out(u4, kc, sg_r, sg_i, hr, hi):
    nb, t, _ = u4.shape
    nc = t // SSM_CHUNK
    sc = STATE_COLS
    cw = 512

    def body(u_ref, kc_ref, sgr_ref, sgi_ref, hr_ref, hi_ref, y_ref, tm_s, gr_s, gi_s):
        _build_toeplitz(kc_ref, tm_s)
        _build_maps(sgr_ref, gr_s)
        _build_maps(sgi_ref, gi_s)
        u = _load_chunks(u_ref, nc)
        h_r = hr_ref[0].astype(BF16)
        h_i = hi_ref[0].astype(BF16)
        for j in range(CHUNK_COLS // cw):
            cs = slice(j * cw, (j + 1) * cw)
            y = _dot(u, tm_s[:, cs]) + _dot_nt(h_r, gr_s[cs, :]) + _dot_nt(h_i, gi_s[cs, :])
            for q in range(cw // LANES):
                step = j * (cw // LANES) + q
                y_ref[0, pl.ds(step, nc, stride=SSM_CHUNK), :] = y[:, q * LANES:(q + 1) * LANES]

    return pl.pallas_call(
        body, name="s5b_out", grid=(nb,),
        in_specs=[_tblk(t), _cblk(SSM_GROUP), _cblk(SSM_STATE),
                  _cblk(SSM_STATE), pl.BlockSpec((1, nc, sc), lambda b: (b, 0, 0)),
                  pl.BlockSpec((1, nc, sc), lambda b: (b, 0, 0))],
        out_specs=_tblk(t),
        out_shape=jax.ShapeDtypeStruct((nb, t, LANES), F32),
        scratch_shapes=[pltpu.VMEM((CHUNK_COLS, CHUNK_COLS), BF16), pltpu.VMEM((CHUNK_COLS, sc), BF16),
                        pltpu.VMEM((CHUNK_COLS, sc), BF16)],
        compiler_params=_params(("arbitrary",)),
    )(u4, kc, sg_r, sg_i, hr, hi)


def _f4_mixer(ys4, prest, x, modv, cvv, cw32, w_a, w_b, w_pw, w_out, late_sh, late_full, tb):
    t = x.shape[0]
    hb = tb // CONV_HALO
    nt = t // tb
    nl = len(LATE)

    def body(ys_ref, pr_ref, halo_ref, x_ref, modv_ref, cvv_ref, cw_ref, wa_ref, wb_ref, wpw_ref, wout_ref, *rest):
        sh, full = rest[:nl], rest[2 * nl:3 * nl]
        r1_ref, ya_ref, yb_ref, ycv_ref, vc_ref, yg_ref, vs_ref, mg_ref, vbuf, vrot, send, recv, fsend, frecv = rest[3 * nl:]
        i = pl.program_id(0)

        @pl.when(i == 0)
        def _():
            _gather_start(LATE, sh, full, send, recv)

        ys = jnp.concatenate([ys_ref[b] for b in range(N_LANE_BLOCKS)], axis=-1)
        yg = _gelu(ys).astype(BF16)
        yg_ref[...] = yg
        ya = _dot(yg, wa_ref[...])
        yb = _dot(yg, wb_ref[...])
        ya_ref[...] = ya.astype(BF16)
        yb_ref[...] = yb.astype(BF16)
        yssm = ya * _sig(yb)
        hv = halo_ref[:, 0:CONV_WIDTH] * _sig(halo_ref[:, CONV_WIDTH:2 * CONV_WIDTH])
        vbuf[0:CONV_HALO, :] = jnp.where(i == 0, 0.0, hv)
        vbuf[CONV_HALO:, :] = pr_ref[:, 0:CONV_WIDTH] * _sig(pr_ref[:, CONV_WIDTH:2 * CONV_WIDTH])
        _fill_rotations(vbuf, vrot, tb + CONV_HALO - 8)
        acc = jnp.zeros((tb, CONV_WIDTH), F32)
        for k in range(CONV_KERNEL):
            acc += _rows_at(vbuf, vrot, CONV_HALO - CONV_KERNEL + 1 + k, tb) * cw_ref[k:k + 1, :]
        vc = acc + cvv_ref[0:1, :]
        vc_ref[...] = vc
        xh, _ = _ln_stats(vc)
        vl = xh * cvv_ref[1:2, :] + cvv_ref[2:3, :]
        vs = (vl * _sig(vl)).astype(BF16)
        vs_ref[...] = vs
        ycv = _dot(vs, wpw_ref[...])
        ycv_ref[...] = ycv.astype(BF16)
        gs = pr_ref[:, 2 * CONV_WIDTH:2 * CONV_WIDTH + D_MODEL]
        gc = pr_ref[:, 2 * CONV_WIDTH + D_MODEL:]
        merged = (_sig(gs) * yssm + _sig(gc) * ycv).astype(BF16)
        mg_ref[...] = merged
        ym = _dot(merged, wout_ref[...])
        r1_ref[...] = ALPHA * x_ref[...] + modv_ref[2:3, :] * ym

        @pl.when(i == nt - 1)
        def _():
            _gather_finish(LATE, sh, full, send, recv, fsend, frecv)

    tok = lambda w: pl.BlockSpec((tb, w), lambda i: (i, 0))
    sem = pltpu.SemaphoreType.DMA((nl, 3))
    n_in = 11
    return pl.pallas_call(
        body, name="f4_mixer", grid=(nt,),
        in_specs=[pl.BlockSpec((N_LANE_BLOCKS, tb, LANES), lambda i: (0, i, 0)), tok(prest.shape[1]),
                  pl.BlockSpec((CONV_HALO, 2 * CONV_WIDTH), lambda i: (jnp.maximum(i * hb - 1, 0), 0)),
                  tok(D_MODEL), _full(modv), _full(cvv), _full(cw32), _full(w_a), _full(w_b), _full(w_pw), _full(w_out)]
        + [ANY] * (2 * nl),
        out_specs=[ANY] * nl + [tok(D_MODEL), tok(D_MODEL), tok(D_MODEL), tok(D_MODEL), tok(CONV_WIDTH), tok(SSM_WIDTH),
                                tok(CONV_WIDTH), tok(D_MODEL)],
        input_output_aliases={n_in + nl + k: k for k in range(nl)},
        out_shape=[jax.ShapeDtypeStruct(f.shape, f.dtype) for f in late_full]
        + [jax.ShapeDtypeStruct((t, D_MODEL), F32), jax.ShapeDtypeStruct((t, D_MODEL), BF16),
                   jax.ShapeDtypeStruct((t, D_MODEL), BF16), jax.ShapeDtypeStruct((t, D_MODEL), BF16),
                   jax.ShapeDtypeStruct((t, CONV_WIDTH), F32), jax.ShapeDtypeStruct((t, SSM_WIDTH), BF16),
                   jax.ShapeDtypeStruct((t, CONV_WIDTH), BF16), jax.ShapeDtypeStruct((t, D_MODEL), BF16)],
        scratch_shapes=[pltpu.VMEM((tb + CONV_HALO, CONV_WIDTH), F32),
                        pltpu.VMEM((7, tb + CONV_HALO - 8, CONV_WIDTH), F32), sem, sem, sem, sem],
        compiler_params=_params(("arbitrary",)),
    )(ys4, prest, prest, x, modv, cvv, cw32, w_a, w_b, w_pw, w_out, *late_sh, *late_full)


FFN_COLS = 1408


def _f5_ffn(r1, tgt, modv, lnv, fdw, w_up, w_down, tb):
    t = r1.shape[0]
    fw = 2 * FFN_HIDDEN

    def body(r1_ref, tgt_ref, modv_ref, lnv_ref, fdw_ref, wup_hbm, wdn_hbm,
             dr2_ref, d_ref, up_ref, z_ref, acc_ref, wup_v, wdn_v, upbuf, gbuf, hbuf, sems):
        i = pl.program_id(0)

        @pl.when(i == 0)
        def _():
            _load_once(wup_hbm, wup_v, sems.at[0])
            _load_once(wdn_hbm, wdn_v, sems.at[1])
            acc_ref[...] = jnp.zeros_like(acc_ref)
            upbuf[0:8, :] = jnp.zeros((8, fw), F32)

        xh1, _ = _ln_stats(r1_ref[...])
        x1 = xh1 * lnv_ref[0:1, :] + lnv_ref[1:2, :]
        xn2, _ = _ln_stats(x1)
        h2 = (xn2 * (1.0 + modv_ref[4:5, :]) + modv_ref[3:4, :]).astype(BF16)
        for j in range(fw // FFN_COLS):
            cs = slice(j * FFN_COLS, (j + 1) * FFN_COLS)
            up = _dot(h2, wup_v[:, cs])
            upbuf[8:, cs] = up
            up_ref[:, cs] = up.astype(BF16)

        def conv(cs):
            return (fdw_ref[0:1, cs] * upbuf[pl.ds(6, tb), cs] + fdw_ref[1:2, cs] * upbuf[pl.ds(7, tb), cs]
                    + fdw_ref[2:3, cs] * upbuf[pl.ds(8, tb), cs] + fdw_ref[3:4, cs])

        halves = [(slice(j * FFN_COLS, (j + 1) * FFN_COLS),
                   slice(FFN_HIDDEN + j * FFN_COLS, FFN_HIDDEN + (j + 1) * FFN_COLS)) for j in range(FFN_HIDDEN // FFN_COLS)]
        yf = jnp.zeros((tb, D_MODEL), F32)
        for ca, cv in halves:
            v = conv(cv)
            g, dg = _gelu_and_grad(conv(ca))
            gbuf[:, ca] = g.astype(BF16)
            hbuf[:, ca] = (v * dg).astype(BF16)
            z = (g * v).astype(BF16)
            z_ref[:, ca] = z
            yf += _dot(z, wdn_v[ca, :])
        r2 = ALPHA * x1 + modv_ref[5:6, :] * yf
        xh2, rstd2 = _ln_stats(r2)
        e = xh2 * lnv_ref[2:3, :] + lnv_ref[3:4, :] - tgt_ref[...]
        dx2 = e * (1.0 / D_MODEL)
        acc_ref[3:4, :] += _colsum(e * e) * (0.5 / D_MODEL)
        acc_ref[0:1, :] += _colsum(dx2 * xh2)
        acc_ref[1:2, :] += _colsum(dx2)
        dr2 = _ln_bwd(dx2 * lnv_ref[2:3, :], xh2, rstd2)
        dr2_ref[...] = dr2
        acc_ref[2:3, :] += _colsum(dr2 * yf)
        dyf = (modv_ref[5:6, :] * dr2).astype(BF16)
        for ca, cv in halves:
            dz = _dot_nt(dyf, wdn_v[ca, :])
            d_ref[:, ca] = (dz * hbuf[:, ca].astype(F32)).astype(BF16)
            d_ref[:, cv] = (dz * gbuf[:, ca].astype(F32)).astype(BF16)
        upbuf[0:8, :] = upbuf[pl.ds(tb, 8), :]

    tok = lambda w: pl.BlockSpec((tb, w), lambda i: (i, 0))
    return pl.pallas_call(
        body, name="f5_ffn", grid=(t // tb,),
        in_specs=[tok(D_MODEL), tok(D_MODEL), _full(modv), _full(lnv), _full(fdw), ANY, ANY],
        out_specs=[tok(D_MODEL), tok(fw), tok(fw), tok(FFN_HIDDEN), pl.BlockSpec((8, D_MODEL), lambda i: (0, 0))],
        out_shape=[jax.ShapeDtypeStruct((t, D_MODEL), F32), jax.ShapeDtypeStruct((t, fw), BF16),
                   jax.ShapeDtypeStruct((t, fw), BF16), jax.ShapeDtypeStruct((t, FFN_HIDDEN), BF16),
                   jax.ShapeDtypeStruct((8, D_MODEL), F32)],
        scratch_shapes=[pltpu.VMEM(w_up.shape, BF16), pltpu.VMEM(w_down.shape, BF16),
                        pltpu.VMEM((tb + 8, fw), F32), pltpu.VMEM((tb, FFN_HIDDEN), BF16),
                        pltpu.VMEM((tb, FFN_HIDDEN), BF16), pltpu.SemaphoreType.DMA((2,))],
        compiler_params=_params(("arbitrary",)),
    )(r1, tgt, modv, lnv, fdw, w_up, w_down)


def _b1b_ffn_up(d, up, dr2, r1, modv, lnv, fdw, w_up, tb):
    t = dr2.shape[0]
    fw = 2 * FFN_HIDDEN
    nt = t // tb
    hb = tb // 16

    def body(d_ref, nxt_ref, up_ref, dr2_ref, r1_ref, modv_ref, lnv_ref, fdw_ref, wup_hbm, dup_ref, dr1_ref, h2_ref,
             dyf_ref, acc_ref, accw_ref, wup_v, dbuf, shifted, sem):
        i = pl.program_id(0)

        @pl.when(i == 0)
        def _():
            _load_once(wup_hbm, wup_v, sem)
            acc_ref[...] = jnp.zeros_like(acc_ref)
            accw_ref[...] = jnp.zeros_like(accw_ref)

        dbuf[0:tb, :] = d_ref[...].astype(F32)
        dbuf[tb:, :] = jnp.where(i == nt - 1, 0.0, nxt_ref[...].astype(F32))
        dh2 = jnp.zeros((tb, D_MODEL), F32)
        for j in range(fw // FFN_COLS):
            cs = slice(j * FFN_COLS, (j + 1) * FFN_COLS)
            for k in range(1, FFN_KERNEL):
                shifted[k - 1] = dbuf[pl.ds(k, tb), cs]
            ds = [dbuf[pl.ds(0, tb), cs], shifted[0], shifted[1]]
            dup = (fdw_ref[2:3, cs] * ds[0] + fdw_ref[1:2, cs] * ds[1] + fdw_ref[0:1, cs] * ds[2]).astype(BF16)
            dup_ref[:, cs] = dup
            dh2 += _dot_nt(dup, wup_v[:, cs])
            upf = up_ref[:, cs].astype(F32)
            for k in range(FFN_KERNEL):
                accw_ref[k:k + 1, cs] += _colsum(ds[FFN_KERNEL - 1 - k] * upf)
            accw_ref[3:4, cs] += _colsum(ds[0])
        xh1, rstd1 = _ln_stats(r1_ref[...])
        x1 = xh1 * lnv_ref[0:1, :] + lnv_ref[1:2, :]
        xn2, rstd2 = _ln_stats(x1)
        h2_ref[...] = (xn2 * (1.0 + modv_ref[4:5, :]) + modv_ref[3:4, :]).astype(BF16)
        dr2 = dr2_ref[...]
        dyf_ref[...] = (modv_ref[5:6, :] * dr2).astype(BF16)
        acc_ref[0:1, :] += _colsum(dh2 * xn2)
        acc_ref[1:2, :] += _colsum(dh2)
        dx1 = _ln_bwd(dh2 * (1.0 + modv_ref[4:5, :]), xn2, rstd2) + ALPHA * dr2
        acc_ref[2:3, :] += _colsum(dx1 * xh1)
        acc_ref[3:4, :] += _colsum(dx1)
        dr1_ref[...] = _ln_bwd(dx1 * lnv_ref[0:1, :], xh1, rstd1)

    tok = lambda w: pl.BlockSpec((tb, w), lambda i: (i, 0))
    return pl.pallas_call(
        body, name="b1b_ffn_up", grid=(nt,),
        in_specs=[tok(fw), pl.BlockSpec((16, fw), lambda i: (jnp.minimum((i + 1) * hb, t // 16 - 1), 0)), tok(fw),
                  tok(D_MODEL), tok(D_MODEL), _full(modv), _full(lnv), _full(fdw), ANY],
        out_specs=[tok(fw), tok(D_MODEL), tok(D_MODEL), tok(D_MODEL), pl.BlockSpec((8, D_MODEL), lambda i: (0, 0)),
                   pl.BlockSpec((8, fw), lambda i: (0, 0))],
        out_shape=[jax.ShapeDtypeStruct((t, fw), BF16), jax.ShapeDtypeStruct((t, D_MODEL), F32),
                   jax.ShapeDtypeStruct((t, D_MODEL), BF16), jax.ShapeDtypeStruct((t, D_MODEL), BF16),
                   jax.ShapeDtypeStruct((8, D_MODEL), F32), jax.ShapeDtypeStruct((8, fw), F32)],
        scratch_shapes=[pltpu.VMEM(w_up.shape, BF16), pltpu.VMEM((tb + 16, fw), F32),
                        pltpu.VMEM((FFN_KERNEL - 1, tb, FFN_COLS), F32), pltpu.SemaphoreType.DMA],
        compiler_params=_params(("arbitrary",)),
    )(d, d, up, dr2, r1, modv, lnv, fdw, w_up)


def _b2_mixer(dr1, ys4, prest, ya, yb, ycv, vc, merged, modv, cvv, cw32, w_a, w_b, w_pw, w_out, late_dw, tb):
    t = dr1.shape[0]
    nt = t // tb
    nl = len(LATE)
    hb = tb // CONV_HALO
    cwd = CONV_WIDTH

    def body(dr1_ref, ys_ref, pr_ref, halo_ref, ya_ref, yb_ref, ycv_ref, vc_ref, mg_ref, modv_ref, cvv_ref, cw_ref,
             wa_ref, wb_ref, wpw_ref, wout_ref, *rest):
        dw, got = rest[:nl], rest[nl:2 * nl]
        (dys_ref, dpr_ref, dya_ref, dyb_ref, dycv_ref, dym_ref, acc_a, acc_b, acc_w, vbuf, dvbuf, vrot, dvrot,
         send, recv) = rest[2 * nl:]
        i = pl.program_id(0)
        ti = nt - 1 - i

        @pl.when(i == 0)
        def _():
            for cp in _scatter(LATE, dw, got, send, recv):
                cp.start()
            acc_a[...] = jnp.zeros_like(acc_a)
            acc_b[...] = jnp.zeros_like(acc_b)
            acc_w[...] = jnp.zeros_like(acc_w)
            dvbuf[pl.ds(tb, CONV_HALO), :] = jnp.zeros((CONV_HALO, cwd), F32)

        dr1 = dr1_ref[...]
        dym = (modv_ref[2:3, :] * dr1).astype(BF16)
        dym_ref[...] = dym
        ym = _dot(mg_ref[...], wout_ref[...])
        acc_a[0:1, :] += _colsum(dr1 * ym)
        dmg = _dot_nt(dym, wout_ref[...])
        sgs = _sig(pr_ref[:, 2 * cwd:2 * cwd + D_MODEL])
        sgc = _sig(pr_ref[:, 2 * cwd + D_MODEL:])
        ya_v = ya_ref[...].astype(F32)
        syb = _sig(yb_ref[...].astype(F32))
        ycv_v = ycv_ref[...].astype(F32)
        dpr_ref[:, 2 * cwd:2 * cwd + D_MODEL] = (dmg * (ya_v * syb) * sgs * (1.0 - sgs)).astype(BF16)
        dpr_ref[:, 2 * cwd + D_MODEL:] = (dmg * ycv_v * sgc * (1.0 - sgc)).astype(BF16)
        dyssm = dmg * sgs
        dya = (dyssm * syb).astype(BF16)
        dyb = (dyssm * ya_v * syb * (1.0 - syb)).astype(BF16)
        dya_ref[...] = dya
        dyb_ref[...] = dyb
        dyg = _dot_nt(dya, wa_ref[...]) + _dot_nt(dyb, wb_ref[...])
        ys = jnp.concatenate([ys_ref[b] for b in range(N_LANE_BLOCKS)], axis=-1)
        dys = dyg * _dgelu(ys)
        for b in range(N_LANE_BLOCKS):
            dys_ref[b] = dys[:, b * LANES:(b + 1) * LANES]
        dycv = (dmg * sgc).astype(BF16)
        dycv_ref[...] = dycv
        dvs = _dot_nt(dycv, wpw_ref[...])
        xh, rstd = _ln_stats(vc_ref[...])
        vl = xh * cvv_ref[1:2, :] + cvv_ref[2:3, :]
        s = _sig(vl)
        dvl = dvs * s * (1.0 + vl * (1.0 - s))
        acc_b[1:2, :] += _colsum(dvl * xh)
        acc_b[2:3, :] += _colsum(dvl)
        dvc = _ln_bwd(dvl * cvv_ref[1:2, :], xh, rstd)
        acc_b[0:1, :] += _colsum(dvc)
        hv = halo_ref[:, 0:cwd] * _sig(halo_ref[:, cwd:2 * cwd])
        vbuf[0:CONV_HALO, :] = jnp.where(ti == 0, 0.0, hv)
        cva = pr_ref[:, 0:cwd]
        scg = _sig(pr_ref[:, cwd:2 * cwd])
        vbuf[CONV_HALO:, :] = cva * scg
        dvbuf[0:tb, :] = dvc
        _fill_rotations(vbuf, vrot, tb + CONV_HALO - 8)
        _fill_rotations(dvbuf, dvrot, tb + CONV_HALO - 8)
        dv = jnp.zeros((tb, cwd), F32)
        for k in range(CONV_KERNEL):
            dv += _rows_at(dvbuf, dvrot, CONV_KERNEL - 1 - k, tb) * cw_ref[k:k + 1, :]
            acc_w[k:k + 1, :] += _colsum(dvc * _rows_at(vbuf, vrot, CONV_HALO - CONV_KERNEL + 1 + k, tb))
        dvbuf[pl.ds(tb, CONV_HALO), :] = dvbuf[0:CONV_HALO, :]
        dpr_ref[:, 0:cwd] = (dv * scg).astype(BF16)
        dpr_ref[:, cwd:2 * cwd] = (dv * cva * scg * (1.0 - scg)).astype(BF16)

        @pl.when(i == nt - 1)
        def _():
            for cp in _scatter(LATE, dw, got, send, recv):
                cp.wait()

    rtok = lambda w: pl.BlockSpec((tb, w), lambda i: (nt - 1 - i, 0))
    r4 = pl.BlockSpec((N_LANE_BLOCKS, tb, LANES), lambda i: (0, nt - 1 - i, 0))
    pw = prest.shape[1]
    return pl.pallas_call(
        body, name="b2_mixer", grid=(nt,),
        in_specs=[rtok(D_MODEL), r4, rtok(pw),
                  pl.BlockSpec((CONV_HALO, 2 * cwd), lambda i: (jnp.maximum((nt - 1 - i) * hb - 1, 0), 0)),
                  rtok(D_MODEL), rtok(D_MODEL), rtok(D_MODEL), rtok(cwd), rtok(D_MODEL),
                  _full(modv), _full(cvv), _full(cw32), _full(w_a), _full(w_b), _full(w_pw), _full(w_out)] + [ANY] * nl,
        out_specs=[ANY] * nl + [r4, rtok(pw), rtok(D_MODEL), rtok(D_MODEL), rtok(D_MODEL), rtok(D_MODEL),
                   pl.BlockSpec((8, D_MODEL), lambda i: (0, 0)), pl.BlockSpec((8, cwd), lambda i: (0, 0)),
                   pl.BlockSpec((CONV_HALO, cwd), lambda i: (0, 0))],
        out_shape=[jax.ShapeDtypeStruct((N_DEV - 1,) + _piece(*BIG[m][1:]), BF16) for m in LATE]
        + [jax.ShapeDtypeStruct((N_LANE_BLOCKS, t, LANES), F32), jax.ShapeDtypeStruct((t, pw), BF16),
                   jax.ShapeDtypeStruct((t, D_MODEL), BF16), jax.ShapeDtypeStruct((t, D_MODEL), BF16),
                   jax.ShapeDtypeStruct((t, D_MODEL), BF16), jax.ShapeDtypeStruct((t, D_MODEL), BF16),
                   jax.ShapeDtypeStruct((8, D_MODEL), F32), jax.ShapeDtypeStruct((8, cwd), F32),
                   jax.ShapeDtypeStruct((CONV_HALO, cwd), F32)],
        scratch_shapes=[pltpu.VMEM((tb + CONV_HALO, cwd), F32), pltpu.VMEM((tb + CONV_HALO, cwd), F32),
                        pltpu.VMEM((7, tb + CONV_HALO - 8, cwd), F32), pltpu.VMEM((7, tb + CONV_HALO - 8, cwd), F32),
                        pltpu.SemaphoreType.DMA((nl, N_DEV - 1)), pltpu.SemaphoreType.DMA((nl, N_DEV - 1))],
        compiler_params=_params(("arbitrary",)),
    )(dr1, ys4, prest, prest, ya, yb, ycv, vc, merged, modv, cvv, cw32, w_a, w_b, w_pw, w_out, *late_dw)


def _s5c_state_bwd(dy4, sg_r, sg_i, a8, hr, hi):
    nb, t, _ = dy4.shape
    nc = t // SSM_CHUNK
    sc = STATE_COLS

    def body(dy_ref, sgr_ref, sgi_ref, a_ref, hr_ref, hi_ref, dxr_ref, dxi_ref, da_ref, dsgr_ref, dsgi_ref,
             g_s, lr_s, li_s, xr_s, xi_s):
        dy = _load_chunks(dy_ref, nc)
        _build_maps(sgr_ref, g_s)
        lr_s[...] = _dot(dy, g_s[...])
        _build_maps(sgi_ref, g_s)
        li_s[...] = _dot(dy, g_s[...])
        ar = a_ref[0, 0:1, :]
        ai = a_ref[0, 1:2, :]

        def step(k, carry):
            pr, pi, dar, dai = carry
            c = nc - 1 - k
            xr_s[pl.ds(c, 1), :] = pr
            xi_s[pl.ds(c, 1), :] = pi
            h_r = hr_ref[0, pl.ds(c, 1), :]
            h_i = hi_ref[0, pl.ds(c, 1), :]
            dar = dar + pr * h_r + pi * h_i
            dai = dai - pr * h_i + pi * h_r
            nr = lr_s[pl.ds(c, 1), :] + ar * pr + ai * pi
            ni = li_s[pl.ds(c, 1), :] - ai * pr + ar * pi
            return nr, ni, dar, dai

        z = jnp.zeros((1, sc), F32)
        _, _, dar, dai = lax.fori_loop(0, nc, step, (z, z, z, z))
        da_ref[0] = jnp.concatenate([dar, dai, jnp.zeros((6, sc), F32)], axis=0)
        dxr_ref[0] = xr_s[...].astype(BF16)
        dxi_ref[0] = xi_s[...].astype(BF16)
        for h_ref, o_ref in ((hr_ref, dsgr_ref), (hi_ref, dsgi_ref)):
            hb = h_ref[0].astype(BF16)
            for j in range(SSM_CHUNK):
                o_ref[j] = _fold(_dot_tn(dy[:, j * LANES:(j + 1) * LANES], hb), SSM_STATE)

    blk = lambda r, c: pl.BlockSpec((1, r, c), lambda b: (b, 0, 0))
    return pl.pallas_call(
        body, name="s5c_state_bwd", grid=(nb,),
        in_specs=[_tblk(t), _cblk(SSM_STATE), _cblk(SSM_STATE), blk(8, sc), blk(nc, sc), blk(nc, sc)],
        out_specs=[blk(nc, sc), blk(nc, sc), blk(8, sc), _cblk(SSM_STATE), _cblk(SSM_STATE)],
        out_shape=[jax.ShapeDtypeStruct((nb, nc, sc), BF16), jax.ShapeDtypeStruct((nb, nc, sc), BF16),
                   jax.ShapeDtypeStruct((nb, 8, sc), F32),
                   jax.ShapeDtypeStruct((SSM_CHUNK, SSM_WIDTH, SSM_STATE), F32),
                   jax.ShapeDtypeStruct((SSM_CHUNK, SSM_WIDTH, SSM_STATE), F32)],
        scratch_shapes=[pltpu.VMEM((CHUNK_COLS, sc), BF16)] + [pltpu.VMEM((nc, sc), F32)] * 4,
        compiler_params=_params(("arbitrary",)),
    )(dy4, sg_r, sg_i, a8, hr, hi)


def _s5d_input_bwd(dy4, u4, kc, sw_r, sw_i, dxr, dxi, mid_dw):
    nb, t, _ = dy4.shape
    nc = t // SSM_CHUNK
    sc = STATE_COLS
    nl = len(MID)

    def body(dy_ref, u_ref, kc_ref, swr_ref, swi_ref, dxr_ref, dxi_ref, *rest):
        dw, got = rest[:nl], rest[nl:2 * nl]
        du_ref, dkc_ref, dswr_ref, dswi_ref, tm_s, w_s, dk_s, send, recv = rest[2 * nl:]

        @pl.when(pl.program_id(0) == 0)
        def _():
            for cp in _scatter(MID, dw, got, send, recv):
                cp.start()

        dy = _load_chunks(dy_ref, nc)
        u = _load_chunks(u_ref, nc)
        _build_toeplitz(kc_ref, tm_s)
        du = _dot_nt(dy, tm_s[...])
        _build_maps(swr_ref, w_s)
        du += _dot_nt(dxr_ref[0], w_s[...])
        _build_maps(swi_ref, w_s)
        du += _dot_nt(dxi_ref[0], w_s[...])
        _store_chunks(du_ref, du, nc)
        dk_s[...] = jnp.zeros_like(dk_s)
        for ji in range(SSM_CHUNK):
            uj = u[:, ji * LANES:(ji + 1) * LANES]
            rows = _dot_tn(uj, dy)
            for jo in range(ji, SSM_CHUNK):
                dk_s[jo - ji] += rows[:, jo * LANES:(jo + 1) * LANES]
            dswr_ref[ji] = _fold(_dot_tn(uj, dxr_ref[0]), SSM_STATE)
            dswi_ref[ji] = _fold(_dot_tn(uj, dxi_ref[0]), SSM_STATE)
        for d in range(SSM_CHUNK):
            dkc_ref[d] = _fold(dk_s[d], SSM_GROUP)

        @pl.when(pl.program_id(0) == nb - 1)
        def _():
            for cp in _scatter(MID, dw, got, send, recv):
                cp.wait()

    blk = lambda r, c: pl.BlockSpec((1, r, c), lambda b: (b, 0, 0))
    ssem = pltpu.SemaphoreType.DMA((nl, N_DEV - 1))
    return pl.pallas_call(
        body, name="s5d_input_bwd", grid=(nb,),
        in_specs=[_tblk(t), _tblk(t), _cblk(SSM_GROUP), _cblk(SSM_STATE), _cblk(SSM_STATE),
                  blk(nc, sc), blk(nc, sc)] + [ANY] * nl,
        out_specs=[ANY] * nl + [_tblk(t), _cblk(SSM_GROUP), _cblk(SSM_STATE), _cblk(SSM_STATE)],
        out_shape=[jax.ShapeDtypeStruct((N_DEV - 1,) + _piece(*BIG[m][1:]), BF16) for m in MID]
        + [jax.ShapeDtypeStruct((nb, t, LANES), F32),
           jax.ShapeDtypeStruct((SSM_CHUNK, SSM_WIDTH, SSM_GROUP), F32),
           jax.ShapeDtypeStruct((SSM_CHUNK, SSM_WIDTH, SSM_STATE), F32),
           jax.ShapeDtypeStruct((SSM_CHUNK, SSM_WIDTH, SSM_STATE), F32)],
        scratch_shapes=[pltpu.VMEM((CHUNK_COLS, CHUNK_COLS), BF16), pltpu.VMEM((CHUNK_COLS, sc), BF16),
                        pltpu.VMEM((SSM_CHUNK, LANES, LANES), F32), ssem, ssem],
        compiler_params=_params(("arbitrary",)),
    )(dy4, u4, kc, sw_r, sw_i, dxr, dxi, *mid_dw)


def _b3_inproj(x, dr1, du4, dprest, modv, w_in, tb):
    t = x.shape[0]
    pw = IN_PROJ_WIDTH - SSM_WIDTH

    def body(x_ref, dr1_ref, du_ref, dpr_ref, modv_ref, w_hbm, gx_ref, dp_ref, acc_ref, accb_ref, w_v, sem):
        @pl.when(pl.program_id(0) == 0)
        def _():
            _load_once(w_hbm, w_v, sem)
            acc_ref[...] = jnp.zeros_like(acc_ref)
            accb_ref[...] = jnp.zeros_like(accb_ref)

        du = jnp.concatenate([du_ref[b] for b in range(N_LANE_BLOCKS)], axis=-1).astype(BF16)
        dpr = dpr_ref[...]
        dp_ref[:, 0:SSM_WIDTH] = du
        dp_ref[:, SSM_WIDTH:] = dpr
        accb_ref[0:1, 0:SSM_WIDTH] += _colsum(du.astype(F32))
        accb_ref[0:1, SSM_WIDTH:] += _colsum(dpr.astype(F32))
        dh = _dot_nt(du, w_v[:, 0:SSM_WIDTH]) + _dot_nt(dpr, w_v[:, SSM_WIDTH:])
        xn, rstd = _ln_stats(x_ref[...])
        acc_ref[0:1, :] += _colsum(dh * xn)
        acc_ref[1:2, :] += _colsum(dh)
        gx_ref[...] = _ln_bwd(dh * (1.0 + modv_ref[1:2, :]), xn, rstd) + ALPHA * dr1_ref[...]

    tok = lambda w: pl.BlockSpec((tb, w), lambda i: (i, 0))
    return pl.pallas_call(
        body, name="b3_inproj", grid=(t // tb,),
        in_specs=[tok(D_MODEL), tok(D_MODEL), pl.BlockSpec((N_LANE_BLOCKS, tb, LANES), lambda i: (0, i, 0)), tok(pw),
                  _full(modv), ANY],
        out_specs=[tok(D_MODEL), tok(IN_PROJ_WIDTH), pl.BlockSpec((8, D_MODEL), lambda i: (0, 0)),
                   pl.BlockSpec((8, IN_PROJ_WIDTH), lambda i: (0, 0))],
        out_shape=[jax.ShapeDtypeStruct((t, D_MODEL), F32), jax.ShapeDtypeStruct((t, IN_PROJ_WIDTH), BF16),
                   jax.ShapeDtypeStruct((8, D_MODEL), F32), jax.ShapeDtypeStruct((8, IN_PROJ_WIDTH), F32)],
        scratch_shapes=[pltpu.VMEM(w_in.shape, BF16), pltpu.SemaphoreType.DMA],
        compiler_params=_params(("arbitrary",)),
    )(x, dr1, du4, dprest, modv, w_in)


TN_ROWS = 2048


def _tn_matmul(name, a, b, tm, tn):
    t, m = a.shape
    n = b.shape[1]
    tt = min(TN_ROWS, t)
    nk = t // tt

    def body(a_ref, b_ref, o_ref, acc):
        k = pl.program_id(2)

        @pl.when(k == 0)
        def _():
            acc[...] = jnp.zeros_like(acc)

        acc[...] += _dot_tn(a_ref[...], b_ref[...])

        @pl.when(k == nk - 1)
        def _():
            o_ref[...] = acc[...].astype(BF16)

    return pl.pallas_call(
        body, name=name, grid=(m // tm, n // tn, nk),
        in_specs=[pl.BlockSpec((tt, tm), lambda i, j, k: (k, i)), pl.BlockSpec((tt, tn), lambda i, j, k: (k, j))],
        out_specs=pl.BlockSpec((tm, tn), lambda i, j, k: (i, j)),
        out_shape=jax.ShapeDtypeStruct((m, n), BF16),
        scratch_shapes=[pltpu.VMEM((tm, tn), F32)],
        compiler_params=_params(("arbitrary", "arbitrary", "arbitrary")),
    )(a, b)


def _local_step(x, tgt, modv, wb, shards, sp, tb=256):
    t = x.shape[0]
    row8 = lambda rows, w: jnp.concatenate([r.reshape(1, w) for r in rows] + [jnp.zeros((8 - len(rows), w), F32)], axis=0)
    lnv = row8([sp["ln1_g"], sp["ln1_b"], sp["ln2_g"], sp["ln2_b"]], D_MODEL)
    cvv = row8([sp["cv_dw_b"], sp["cv_ln_g"], sp["cv_ln_b"]], CONV_WIDTH)
    cw32 = jnp.concatenate([sp["cv_dw_w"].reshape(CONV_KERNEL, CONV_WIDTH), jnp.zeros((1, CONV_WIDTH), F32)], axis=0)
    fdw = row8(list(sp["ffn_dw_w"].reshape(FFN_KERNEL, 2 * FFN_HIDDEN)) + [sp["ffn_dw_b"]], 2 * FFN_HIDDEN)
    b_in = sp["b_in"].reshape(1, IN_PROJ_WIDTH)
    ssm = tuple(sp[k] for k in ("ssm_lambda_re", "ssm_lambda_im", "ssm_log_dt", "ssm_b_re", "ssm_b_im", "ssm_c_re",
                                "ssm_c_im", "ssm_d"))
    (kc, sw_r, sw_i, sg_r, sg_i, a), ssm_vjp = jax.vjp(_s5_build, *ssm)
    a8 = jnp.concatenate([a, jnp.zeros((N_LANE_BLOCKS, 6, STATE_COLS), F32)], axis=1)

    name = lambda m: BIG[m][0]
    *mid_w, u4, prest, h1 = _f1_inproj(x, modv, b_in, wb["w_in"], [shards[m] for m in MID], [wb[name(m)] for m in MID], tb)
    w_a, w_b, w_pw, w_out = mid_w
    hr, hi = _s5a_state(u4, sw_r, sw_i, a8)
    ys4 = _s5b_out(u4, kc, sg_r, sg_i, hr, hi)
    w_up, w_down, r1, ya, yb, ycv, vc, yg, vs, merged = _f4_mixer(
        ys4, prest, x, modv, cvv, cw32, w_a, w_b, w_pw, w_out, [shards[m] for m in LATE], [wb[name(m)] for m in LATE], tb)
    dr2, dconv, up, z, acc5 = _f5_ffn(r1, tgt, modv, lnv, fdw, w_up, w_down, tb)
    dup, dr1, h2, dyf, acc1b, acc1a = _b1b_ffn_up(dconv, up, dr2, r1, modv, lnv, fdw, w_up, tb)
    late_dw = [_tn_matmul("dw_up", h2, dup, 1024, FFN_COLS), _tn_matmul("dw_down", z, dyf, FFN_COLS, 1024)]
    got_up, got_down, dys4, dprest, dya, dyb, dycv, dym, acc2a, acc2b, acc2w = _b2_mixer(
        dr1, ys4, prest, ya, yb, ycv, vc, merged, modv, cvv, cw32, w_a, w_b, w_pw, w_out, late_dw, tb)
    mid_dw = [_tn_matmul("dw_glu_a", yg, dya, 512, 1024), _tn_matmul("dw_glu_b", yg, dyb, 512, 1024),
              _tn_matmul("dw_pw", vs, dycv, 512, 1024), _tn_matmul("dw_out", merged, dym, 1024, 1024)]
    dxr, dxi, da8, dsg_r, dsg_i = _s5c_state_bwd(dys4, sg_r, sg_i, a8, hr, hi)
    *mid_got, du4, dkc, dsw_r, dsw_i = _s5d_input_bwd(dys4, u4, kc, sw_r, sw_i, dxr, dxi, mid_dw)
    dssm = ssm_vjp((dkc, dsw_r, dsw_i, dsg_r, dsg_i, da8[:, 0:2, :]))
    gx, dp, acc3, acc3b = _b3_inproj(x, dr1, du4, dprest, modv, wb["w_in"], tb)
    dbig = [_tn_matmul("dw_in", h1, dp, 1024, 896)] + mid_dw + late_dw
    dmod = jnp.concatenate([acc3[1], acc3[0], acc2a[0], acc1b[1], acc1b[0], acc5[2]])
    small = {
        "dmod": dmod, "b_in": acc3b[0],
        "ssm_lambda_re": dssm[0], "ssm_lambda_im": dssm[1], "ssm_log_dt": dssm[2], "ssm_b_re": dssm[3],
        "ssm_b_im": dssm[4], "ssm_c_re": dssm[5], "ssm_c_im": dssm[6], "ssm_d": dssm[7],
        "cv_dw_w": acc2w[0:CONV_KERNEL], "cv_dw_b": acc2b[0], "cv_ln_g": acc2b[1], "cv_ln_b": acc2b[2],
        "ln1_g": acc1b[2], "ln1_b": acc1b[3], "ffn_dw_w": acc1a[0:FFN_KERNEL], "ffn_dw_b": acc1a[3],
        "ln2_g": acc5[0], "ln2_b": acc5[1], "loss": jnp.sum(acc5[3]).reshape(1),
    }
    return gx, dbig, list(mid_got) + [got_up, got_down], small


def _allgather(name, shard):
    m_per, n = shard.shape

    def body(x_ref, out_ref, send_sems, recv_sems, local_sem):
        x, y, c, chips = _place()
        me, sibling = (x, y, c), (x, y, 1 - c)

        def rows(px, py, pc):
            return out_ref.at[pl.ds((4 * px + 2 * py + pc) * m_per, m_per), :]

        def copy(k, block, to, src=None):
            return pltpu.make_async_remote_copy(
                src_ref=rows(*block) if src is None else src, dst_ref=rows(*block),
                send_sem=send_sems.at[k], recv_sem=recv_sems.at[k], device_id=to, device_id_type=MESH)

        mine = pltpu.make_async_copy(x_ref, rows(*me), local_sem)
        mine.start()
        first = [copy(0, me, sibling, src=x_ref)]
        first += [copy(1 + j, me, (*chip, c), src=x_ref) for j, chip in enumerate(chips)]
        for cp in first:
            cp.start()
        passed = [copy(4 + j, (*chip, c), sibling) for j, chip in enumerate(chips)]
        for j, chip in enumerate(chips):
            copy(1 + j, (*chip, c), me).wait_recv()
            passed[j].start()
        copy(0, sibling, me).wait_recv()
        for j, chip in enumerate(chips):
            copy(4 + j, (*chip, 1 - c), me).wait_recv()
        for cp in first + passed:
            cp.wait_send()
        mine.wait()

    return pl.pallas_call(
        body, name=name,
        out_shape=jax.ShapeDtypeStruct((N_DEV * m_per, n), shard.dtype),
        in_specs=[pl.BlockSpec(memory_space=pltpu.VMEM)],
        out_specs=pl.BlockSpec(memory_space=pltpu.VMEM),
        scratch_shapes=[pltpu.SemaphoreType.DMA((7,)), pltpu.SemaphoreType.DMA((7,)), pltpu.SemaphoreType.DMA],
        compiler_params=_params(),
    )(shard)


def _add_rows(pr):
    return 64 if pr % 64 == 0 else 16


def _gather_weights(shards):
    nm = len(BIG)
    nl = len(DIRECT)

    def body(*refs):
        ins, outs, lsh = refs[:nm], refs[nm:2 * nm], refs[2 * nm:2 * nm + nl]
        stage = refs[2 * nm + nl:3 * nm + nl]
        send, recv, fsend, frecv, lsem = refs[3 * nm + nl:]
        x, y, c, chips = _place()
        s_me = 2 * x + y
        sibling = (x, y, 1 - c)
        pend = []
        for m in range(nm):
            stage[m][...] = ins[m][...].astype(BF16)
        for m, (_, kind, shape) in enumerate(BIG):
            pr, pc = _piece(kind, shape)
            for k in range(2):
                cp = pltpu.make_async_copy(stage[m].at[pl.ds(k * pr, pr), :], _piece_at(outs[m], kind, shape, s_me, k),
                                           lsem.at[m, k])
                cp.start()
                pend.append(cp.wait)
            if m in DIRECT:
                cp = pltpu.make_async_copy(stage[m], lsh[DIRECT.index(m)], lsem.at[m, 2])
                cp.start()
                pend.append(cp.wait)
                continue
            for j, chip in enumerate(chips):
                cp = pltpu.make_async_remote_copy(
                    src_ref=stage[m].at[pl.ds(pl.multiple_of(c * pr, 16), pr), :],
                    dst_ref=_piece_at(outs[m], kind, shape, s_me, c),
                    send_sem=send.at[m, j], recv_sem=recv.at[m, j], device_id=(*chip, c), device_id_type=MESH)
                cp.start()
                pend.append(cp.wait_send)
        for m in EARLY:
            _, kind, shape = BIG[m]
            for j, (cx, cy) in enumerate(chips):
                got = _piece_at(outs[m], kind, shape, 2 * cx + cy, c)
                pltpu.make_async_remote_copy(src_ref=got, dst_ref=got, send_sem=send.at[m, j], recv_sem=recv.at[m, j],
                                             device_id=(cx, cy, c), device_id_type=MESH).wait_recv()
                cp = pltpu.make_async_remote_copy(src_ref=got, dst_ref=got, send_sem=fsend.at[m, j],
                                                  recv_sem=frecv.at[m, j], device_id=sibling, device_id_type=MESH)
                cp.start()
                pend.append(cp.wait_send)
        for m in EARLY:
            _, kind, shape = BIG[m]
            for j, (cx, cy) in enumerate(chips):
                got = _piece_at(outs[m], kind, shape, 2 * cx + cy, 1 - c)
                pltpu.make_async_remote_copy(src_ref=got, dst_ref=got, send_sem=fsend.at[m, j], recv_sem=frecv.at[m, j],
                                             device_id=sibling, device_id_type=MESH).wait_recv()
        for w in pend:
            w()

    sem = lambda *s: pltpu.SemaphoreType.DMA(s)
    res = pl.pallas_call(
        body, name="gather_weights",
        out_shape=[jax.ShapeDtypeStruct(shape, BF16) for _, _, shape in BIG]
        + [jax.ShapeDtypeStruct(shards[m].shape, BF16) for m in DIRECT],
        in_specs=[pl.BlockSpec(memory_space=pltpu.VMEM)] * nm,
        out_specs=[ANY] * (nm + nl),
        scratch_shapes=[pltpu.VMEM(s.shape, BF16) for s in shards] + [sem(nm, 3), sem(nm, 3), sem(nm, 3), sem(nm, 3),
                                                                         sem(nm, 3)],
        compiler_params=_params(),
    )(*shards)
    return res[:nm], dict(zip(DIRECT, res[nm:]))


def _rs1_sibling(grads):
    mats = [BIG[m] for m in EARLY]
    nm = len(mats)

    def body(*refs):
        ins, outs = refs[:nm], refs[nm:2 * nm]
        send, recv = refs[2 * nm:]
        x, y, c, _ = _place()
        cps = []
        for m, (_, kind, shape) in enumerate(mats):
            for s in range(N_CHIP):
                cp = pltpu.make_async_remote_copy(
                    src_ref=_piece_at(ins[m], kind, shape, s, 1 - c), dst_ref=outs[m].at[s],
                    send_sem=send.at[m, s], recv_sem=recv.at[m, s], device_id=(x, y, 1 - c), device_id_type=MESH)
                cp.start()
                cps.append(cp)
        for cp in cps:
            cp.wait()

    sem = lambda *s: pltpu.SemaphoreType.DMA(s)
    return pl.pallas_call(
        body, name="rs1_sibling",
        out_shape=[jax.ShapeDtypeStruct((N_CHIP,) + _piece(kind, shape), BF16) for _, kind, shape in mats],
        in_specs=[ANY] * nm, out_specs=[ANY] * nm,
        scratch_shapes=[sem(nm, N_CHIP), sem(nm, N_CHIP)],
        compiler_params=_params(),
    )(*grads)


def _rs2_chips(grads, halves):
    mats = [BIG[m] for m in EARLY]
    nm = len(mats)

    def body(*refs):
        gin, hin = refs[:nm], refs[nm:2 * nm]
        own, got = refs[2 * nm:3 * nm], refs[3 * nm:4 * nm]
        send, recv, lsem = refs[4 * nm:]
        x, y, c, chips = _place()
        s_me = 2 * x + y
        for m, (_, kind, shape) in enumerate(mats):
            pr, pc = _piece(kind, shape)

            def scoped(a, b, m=m, kind=kind, shape=shape, pr=pr):
                loads = [pltpu.make_async_copy(_piece_at(gin[m], kind, shape, s, c), a.at[s], lsem.at[s])
                         for s in range(N_CHIP)]
                loads.append(pltpu.make_async_copy(hin[m], b, lsem.at[N_CHIP]))
                for cp in loads:
                    cp.start()
                for cp in loads:
                    cp.wait()
                step = _add_rows(pr)
                for s in range(N_CHIP):
                    def add(i, _, s=s):
                        r = pl.ds(pl.multiple_of(i * step, 16), step)
                        a[s, r, :] = (a[s, r, :].astype(F32) + b[s, r, :].astype(F32)).astype(BF16)
                        return 0

                    lax.fori_loop(0, pr // step, add, 0)
                waits = []
                for j, (cx, cy) in enumerate(chips):
                    cp = pltpu.make_async_remote_copy(src_ref=a.at[2 * cx + cy], dst_ref=got[m].at[j], send_sem=send.at[m, j],
                                                      recv_sem=recv.at[m, j], device_id=(cx, cy, c), device_id_type=MESH)
                    cp.start()
                    waits.append(cp.wait_send)
                cp = pltpu.make_async_copy(a.at[s_me], own[m], lsem.at[N_CHIP + 1])
                cp.start()
                waits.append(cp.wait)
                for w in waits:
                    w()

            pl.run_scoped(scoped, pltpu.VMEM((N_CHIP, pr, pc), BF16), pltpu.VMEM((N_CHIP, pr, pc), BF16))
        for m in range(nm):
            for j, (cx, cy) in enumerate(chips):
                pltpu.make_async_remote_copy(src_ref=got[m].at[j], dst_ref=got[m].at[j], send_sem=send.at[m, j],
                                             recv_sem=recv.at[m, j], device_id=(cx, cy, c), device_id_type=MESH).wait_recv()

    sem = lambda *s: pltpu.SemaphoreType.DMA(s)
    pieces = [_piece(kind, shape) for _, kind, shape in mats]
    return pl.pallas_call(
        body, name="rs2_chips",
        out_shape=[jax.ShapeDtypeStruct(p, BF16) for p in pieces] + [jax.ShapeDtypeStruct((3,) + p, BF16) for p in pieces],
        in_specs=[ANY] * (2 * nm), out_specs=[ANY] * (2 * nm),
        scratch_shapes=[sem(nm, 3), sem(nm, 3), sem(N_CHIP + 2)],
        compiler_params=_params(),
    )(*grads, *halves)


def _rs3_finish(own, got):
    nm = len(BIG)

    def body(*refs):
        oin, gin = refs[:nm], refs[nm:2 * nm]
        outs = refs[2 * nm:3 * nm]
        send, recv, lsem = refs[3 * nm:]
        x, y, c, _ = _place()
        for m, (_, kind, shape) in enumerate(BIG):
            pr, pc = _piece(kind, shape)
            ng = got[m].shape[0]

            def scoped(a, g, f, m=m, pr=pr, ng=ng, kind=kind, shape=shape):
                mine = _piece_at(oin[m], kind, shape, 2 * x + y, c) if m in DIRECT else oin[m]
                loads = [pltpu.make_async_copy(mine, a, lsem.at[0]), pltpu.make_async_copy(gin[m], g, lsem.at[1])]
                for cp in loads:
                    cp.start()
                for cp in loads:
                    cp.wait()
                step = _add_rows(pr)

                def add(i, _):
                    r = pl.ds(pl.multiple_of(i * step, 16), step)
                    acc = a[r, :].astype(F32)
                    for q in range(ng):
                        acc = acc + g[q, r, :].astype(F32)
                    f[r, :] = acc
                    return 0

                lax.fori_loop(0, pr // step, add, 0)
                dst = outs[m].at[pl.ds(pl.multiple_of(c * pr, 8), pr), :]
                local = pltpu.make_async_copy(f, dst, lsem.at[2])
                local.start()
                cp = pltpu.make_async_remote_copy(src_ref=f, dst_ref=dst, send_sem=send.at[m], recv_sem=recv.at[m],
                                                  device_id=(x, y, 1 - c), device_id_type=MESH)
                cp.start()
                cp.wait_send()
                local.wait()

            pl.run_scoped(scoped, pltpu.VMEM((pr, pc), BF16), pltpu.VMEM((ng, pr, pc), BF16), pltpu.VMEM((pr, pc), F32))
        for m, (_, kind, shape) in enumerate(BIG):
            pr, pc = _piece(kind, shape)
            dst = outs[m].at[pl.ds(pl.multiple_of((1 - c) * pr, 8), pr), :]
            pltpu.make_async_remote_copy(src_ref=dst, dst_ref=dst, send_sem=send.at[m], recv_sem=recv.at[m],
                                         device_id=(x, y, 1 - c), device_id_type=MESH).wait_recv()

    sem = lambda *s: pltpu.SemaphoreType.DMA(s)
    pieces = [_piece(kind, shape) for _, kind, shape in BIG]
    return pl.pallas_call(
        body, name="rs3_finish",
        out_shape=[jax.ShapeDtypeStruct((2 * pr, pc), F32) for pr, pc in pieces],
        in_specs=[ANY] * (2 * nm), out_specs=[ANY] * nm,
        scratch_shapes=[sem(nm), sem(nm), sem(3)],
        compiler_params=_params(),
    )(*own, *got)


def _cond_fwd(c_all, w_shard, b_shard):
    def body(c_ref, w_ref, b_ref, act_ref, mod_ref):
        cv = c_ref[...]
        act = cv * _sig(cv)
        act_ref[...] = act
        mod_ref[...] = _dot(act.astype(BF16), w_ref[...].astype(BF16)) + b_ref[...]

    return pl.pallas_call(
        body, name="cond_fwd",
        out_shape=[jax.ShapeDtypeStruct(c_all.shape, F32), jax.ShapeDtypeStruct((c_all.shape[0], w_shard.shape[1]), F32)],
        compiler_params=_params(),
    )(c_all, w_shard, b_shard)


def _cond_bwd(act_t, dmod_shard):
    k, n = act_t.shape[0], dmod_shard.shape[1]

    def body(a_ref, d_ref, o_ref):
        acc = a_ref[:, 0:1] * d_ref[0:1, :]
        for e in range(1, N_DEV):
            acc += a_ref[:, e:e + 1] * d_ref[e:e + 1, :]
        o_ref[...] = acc

    tr = 256
    return pl.pallas_call(
        body, name="cond_bwd", grid=(k // tr,),
        in_specs=[pl.BlockSpec((tr, N_DEV), lambda i: (i, 0)), _full(dmod_shard)],
        out_specs=pl.BlockSpec((tr, n), lambda i: (i, 0)),
        out_shape=jax.ShapeDtypeStruct((k, n), F32),
        compiler_params=_params(("arbitrary",)),
    )(act_t, dmod_shard)


def _sum_blocks(allp):
    def body(a_ref, o_ref):
        acc = a_ref[0:PACK_ROWS, :]
        for d in range(1, N_DEV):
            acc += a_ref[d * PACK_ROWS:(d + 1) * PACK_ROWS, :]
        o_ref[...] = acc

    return pl.pallas_call(
        body, name="sum_small", out_shape=jax.ShapeDtypeStruct((PACK_ROWS, PACK_COLS), F32), compiler_params=_params(),
    )(allp)


def _adamw(name, w, g, m, v):
    r, cc = w.shape
    tr = r
    for cand in (256, 128, 64, 32, 16, 8):
        if r % cand == 0:
            tr = cand
            break
    bc1 = 1.0 - ADAM_B1 ** ADAM_STEP
    bc2 = 1.0 - ADAM_B2 ** ADAM_STEP

    def body(w_ref, g_ref, m_ref, v_ref, d_ref, nm_ref, nv_ref):
        gv = g_ref[...]
        nm = ADAM_B1 * m_ref[...] + (1.0 - ADAM_B1) * gv
        nv = ADAM_B2 * v_ref[...] + (1.0 - ADAM_B2) * (gv * gv)
        nm_ref[...] = nm
        nv_ref[...] = nv
        d_ref[...] = -ADAM_LR * ((nm / bc1) / (jnp.sqrt(nv / bc2) + ADAM_EPS) + ADAM_WD * w_ref[...])

    spec = pl.BlockSpec((tr, cc), lambda i: (i, 0))
    return pl.pallas_call(
        body, name=name, grid=(r // tr,), in_specs=[spec] * 4, out_specs=[spec] * 3,
        out_shape=[jax.ShapeDtypeStruct((r, cc), F32)] * 3, compiler_params=_params(("arbitrary",)),
    )(w, g, m, v)


def _adamw_small(ws, gs, ms, vs):
    n = len(ws)
    bc1 = 1.0 - ADAM_B1 ** ADAM_STEP
    bc2 = 1.0 - ADAM_B2 ** ADAM_STEP

    def body(*refs):
        w, g, m, v = (refs[k * n:(k + 1) * n] for k in range(4))
        d, nm, nv = (refs[(4 + k) * n:(5 + k) * n] for k in range(3))
        for i in range(n):
            gv = g[i][...]
            m1 = ADAM_B1 * m[i][...] + (1.0 - ADAM_B1) * gv
            v1 = ADAM_B2 * v[i][...] + (1.0 - ADAM_B2) * (gv * gv)
            nm[i][...] = m1
            nv[i][...] = v1
            d[i][...] = -ADAM_LR * ((m1 / bc1) / (jnp.sqrt(v1 / bc2) + ADAM_EPS) + ADAM_WD * w[i][...])

    shapes = [jax.ShapeDtypeStruct(x.shape, F32) for x in ws]
    res = pl.pallas_call(body, name="adamw_small", out_shape=shapes * 3, compiler_params=_params())(*ws, *gs, *ms, *vs)
    return res[:n], res[n:2 * n], res[2 * n:]


def _pack(fields, layout):
    parts = [fields[name].reshape(-1).astype(F32) if name in fields else jnp.zeros((n,), F32) for name, n in layout]
    used = sum(n for _, n in layout)
    parts.append(jnp.zeros((PACK_ROWS * PACK_COLS - used,), F32))
    return jnp.concatenate(parts).reshape(PACK_ROWS, PACK_COLS)


def _unpack(flat, layout):
    flat = flat.reshape(-1)
    out, o = {}, 0
    for name, n in layout:
        out[name] = flat[o:o + n]
        o += n
    return out


def kernel(x, c, w_cond, b_cond, w_in, b_in, ssm_lambda_re, ssm_lambda_im, ssm_log_dt, ssm_b_re, ssm_b_im, ssm_c_re, ssm_c_im, ssm_d, ssm_glu_w_a, ssm_glu_w_b, cv_dw_w, cv_dw_b, cv_ln_g, cv_ln_b, cv_w_pw, w_out, ln1_g, ln1_b, ffn_w_up, ffn_dw_w, ffn_dw_b, ffn_w_down, ln2_g, ln2_b, loss_target, m_w_cond, m_b_cond, m_w_in, m_b_in, m_ssm_lambda_re, m_ssm_lambda_im, m_ssm_log_dt, m_ssm_b_re, m_ssm_b_im, m_ssm_c_re, m_ssm_c_im, m_ssm_d, m_ssm_glu_w_a, m_ssm_glu_w_b, m_cv_dw_w, m_cv_dw_b, m_cv_ln_g, m_cv_ln_b, m_cv_w_pw, m_w_out, m_ln1_g, m_ln1_b, m_ffn_w_up, m_ffn_dw_w, m_ffn_dw_b, m_ffn_w_down, m_ln2_g, m_ln2_b, v_w_cond, v_b_cond, v_w_in, v_b_in, v_ssm_lambda_re, v_ssm_lambda_im, v_ssm_log_dt, v_ssm_b_re, v_ssm_b_im, v_ssm_c_re, v_ssm_c_im, v_ssm_d, v_ssm_glu_w_a, v_ssm_glu_w_b, v_cv_dw_w, v_cv_dw_b, v_cv_ln_g, v_cv_ln_b, v_cv_w_pw, v_w_out, v_ln1_g, v_ln1_b, v_ffn_w_up, v_ffn_dw_w, v_ffn_dw_b, v_ffn_w_down, v_ln2_g, v_ln2_b):
    given = locals()
    a = {n: given[n] for n in INPUTS}
    xi, yi, ci = lax.axis_index("x"), lax.axis_index("y"), lax.axis_index("c")
    s_me = 2 * xi + yi
    e_me = 4 * xi + 2 * yi + ci

    first = jnp.concatenate([
        jnp.concatenate([a["c"], jnp.zeros((7, D_MODEL), F32)], axis=0),
        jnp.concatenate([a["cv_dw_w"].reshape(-1), a["ffn_dw_w"].reshape(-1)]).reshape(8, D_MODEL)], axis=0)
    first_all = _allgather("gather_c", first).reshape(N_DEV, 16, D_MODEL)
    c_all = first_all[:, 0, :]
    dw_all = first_all[0::2, 8:, :].reshape(N_CHIP, 8 * D_MODEL)
    n_cv = CONV_KERNEL * CONV_WIDTH // N_CHIP
    cv_dw_full = dw_all[:, :n_cv].reshape(N_CHIP, CONV_KERNEL, CONV_WIDTH // N_CHIP).transpose(1, 0, 2) \
        .reshape(CONV_KERNEL, CONV_WIDTH)
    ffn_dw_full = dw_all[:, n_cv:].reshape(N_CHIP, FFN_KERNEL, 2 * FFN_HIDDEN // N_CHIP).transpose(1, 0, 2) \
        .reshape(FFN_KERNEL, 2 * FFN_HIDDEN)
    ncols = N_COND * D_MODEL // N_CHIP
    b_cond_shard = lax.dynamic_slice(a["b_cond"], (0, s_me * ncols), (1, ncols))
    c_act_all, modp = _cond_fwd(c_all, a["w_cond"][0], b_cond_shard)
    modp_all = _allgather("gather_mod", modp).reshape(N_DEV, N_DEV, ncols)[0::2]
    mod_e = lax.dynamic_index_in_dim(modp_all, e_me, axis=1, keepdims=False).reshape(N_COND, D_MODEL)
    modv = jnp.concatenate([mod_e, jnp.zeros((2, D_MODEL), F32)], axis=0)

    full, shards = _gather_weights([a[n][0] for n, _, _ in BIG])
    wb = dict(zip([n for n, _, _ in BIG], full))
    sp = {n: a[n][0] for n in ("b_in", "ssm_lambda_re", "ssm_lambda_im", "ssm_log_dt", "ssm_b_re", "ssm_b_im",
                               "ssm_c_re", "ssm_c_im", "ssm_d", "cv_dw_b", "cv_ln_g", "cv_ln_b", "ln1_g", "ln1_b",
                               "ffn_dw_b", "ln2_g", "ln2_b")}
    sp["cv_dw_w"] = cv_dw_full
    sp["ffn_dw_w"] = ffn_dw_full
    gx, dbig, direct_got, small = _local_step(a["x"][0], a["loss_target"][0], modv, wb, shards, sp)

    small["c_act"] = lax.dynamic_index_in_dim(c_act_all, e_me, axis=0, keepdims=False)
    packed_all = _allgather("gather_small", _pack(small, PACK))
    tot = _unpack(_sum_blocks(packed_all), PACK)
    rows = packed_all.reshape(N_DEV, PACK_ROWS * PACK_COLS)
    dmod_all = rows[:, 0:N_COND * D_MODEL]
    act_all = rows[:, N_COND * D_MODEL:(N_COND + 1) * D_MODEL]
    g_w_cond = _cond_bwd(act_all.T, lax.dynamic_slice(dmod_all, (0, s_me * ncols), (N_DEV, ncols)))

    glist = [dbig[m] for m in EARLY]
    halves = _rs1_sibling(glist)
    r2 = _rs2_chips(glist, halves)
    gsh = _rs3_finish(list(r2[:len(EARLY)]) + [dbig[m] for m in DIRECT], list(r2[len(EARLY):]) + direct_got)

    grads = {"w_cond": g_w_cond[None], "b_cond": tot["dmod"].reshape(1, -1)}
    for (n, kind, shape), g in zip(BIG, gsh):
        grads[n] = g.reshape(a[n].shape)
    for n in ("b_in", "ssm_lambda_re", "ssm_lambda_im", "ssm_log_dt", "ssm_b_re", "ssm_b_im", "ssm_c_re", "ssm_c_im",
              "ssm_d", "cv_dw_b", "cv_ln_g", "cv_ln_b", "ln1_g", "ln1_b", "ffn_dw_b", "ln2_g", "ln2_b"):
        grads[n] = tot[n].reshape(a[n].shape)
    wcv = CONV_WIDTH // N_CHIP
    grads["cv_dw_w"] = lax.dynamic_slice(tot["cv_dw_w"].reshape(CONV_KERNEL, CONV_WIDTH), (0, s_me * wcv),
                                         (CONV_KERNEL, wcv)).reshape(a["cv_dw_w"].shape)
    wff = 2 * FFN_HIDDEN // N_CHIP
    grads["ffn_dw_w"] = lax.dynamic_slice(tot["ffn_dw_w"].reshape(FFN_KERNEL, 2 * FFN_HIDDEN), (0, s_me * wff),
                                          (FFN_KERNEL, wff)).reshape(a["ffn_dw_w"].shape)

    delta, new_m, new_v = {}, {}, {}
    for n in ["w_cond"] + [n for n, _, _ in BIG]:
        d, nm_, nv_ = _adamw("adamw_" + n, a[n][0], grads[n][0], a["m_" + n][0], a["v_" + n][0])
        delta[n], new_m[n], new_v[n] = d[None], nm_[None], nv_[None]
    upd = [n for n in WEIGHTS if n not in delta]
    two_d = lambda t: t.reshape(-1, t.shape[-1])
    outs = _adamw_small([two_d(a[n]) for n in upd], [two_d(grads[n]) for n in upd],
                        [two_d(a["m_" + n]) for n in upd], [two_d(a["v_" + n]) for n in upd])
    for dst, vals in zip((delta, new_m, new_v), outs):
        for n, val in zip(upd, vals):
            dst[n] = val.reshape(a[n].shape)

    loss = tot["loss"].reshape(())
    return (loss, gx[None], *[grads[n] for n in WEIGHTS], *[delta[n] for n in WEIGHTS],
            *[new_m[n] for n in WEIGHTS], *[new_v[n] for n in WEIGHTS])
```

```python
import functools
import math

import jax
import jax.numpy as jnp
from jax import lax
from jax.experimental import pallas as pl
from jax.experimental.pallas import tpu as pltpu

F32 = jnp.float32
BF16 = jnp.bfloat16

D_MODEL = 1024
SSM_WIDTH = 512
SSM_GROUP = 16
SSM_GROUPS = 32
SSM_STATE = 64
CONV_WIDTH = 512
CONV_KERNEL = 31
FFN_HIDDEN = 2816
FFN_KERNEL = 3
IN_PROJ_WIDTH = 3584
N_COND = 6
ALPHA = 2.0 ** 0.25
LN_EPS = 1e-5
ADAM_LR, ADAM_B1, ADAM_B2, ADAM_EPS, ADAM_WD, ADAM_STEP = 0.001, 0.9, 0.999, 1e-08, 0.01, 10

N_DEV = 8
N_CHIP = 4
LANES = 128
SSM_CHUNK = 16
LANE_GROUPS = LANES // SSM_GROUP
N_LANE_BLOCKS = SSM_WIDTH // LANES
STATE_COLS = LANE_GROUPS * SSM_STATE
CHUNK_COLS = SSM_CHUNK * LANES
CONV_HALO = 32
VMEM_LIMIT = 56 * 1024 * 1024
MESH = pl.DeviceIdType.MESH

BIG = (
    ("w_in", "col", (D_MODEL, IN_PROJ_WIDTH)),
    ("ssm_glu_w_a", "col", (SSM_WIDTH, D_MODEL)),
    ("ssm_glu_w_b", "col", (SSM_WIDTH, D_MODEL)),
    ("cv_w_pw", "col", (CONV_WIDTH, D_MODEL)),
    ("w_out", "row", (D_MODEL, D_MODEL)),
    ("ffn_w_up", "col", (D_MODEL, 2 * FFN_HIDDEN)),
    ("ffn_w_down", "row", (FFN_HIDDEN, D_MODEL)),
)

EARLY = (0,)
MID = (1, 2, 3, 4)
LATE = (5, 6)
DIRECT = MID + LATE

WEIGHTS = ['w_cond', 'b_cond', 'w_in', 'b_in', 'ssm_lambda_re', 'ssm_lambda_im', 'ssm_log_dt', 'ssm_b_re', 'ssm_b_im',
           'ssm_c_re', 'ssm_c_im', 'ssm_d', 'ssm_glu_w_a', 'ssm_glu_w_b', 'cv_dw_w', 'cv_dw_b', 'cv_ln_g', 'cv_ln_b',
           'cv_w_pw', 'w_out', 'ln1_g', 'ln1_b', 'ffn_w_up', 'ffn_dw_w', 'ffn_dw_b', 'ffn_w_down', 'ln2_g', 'ln2_b']
INPUTS = ['x', 'c'] + WEIGHTS + ['loss_target'] + ['m_' + n for n in WEIGHTS] + ['v_' + n for n in WEIGHTS]

PACK = (
    ("dmod", N_COND * D_MODEL), ("c_act", D_MODEL), ("b_in", IN_PROJ_WIDTH),
    ("ssm_lambda_re", SSM_GROUPS * SSM_STATE), ("ssm_lambda_im", SSM_GROUPS * SSM_STATE), ("ssm_log_dt", SSM_GROUPS),
    ("ssm_b_re", SSM_GROUPS * SSM_STATE * SSM_GROUP), ("ssm_b_im", SSM_GROUPS * SSM_STATE * SSM_GROUP),
    ("ssm_c_re", SSM_GROUPS * SSM_STATE * SSM_GROUP), ("ssm_c_im", SSM_GROUPS * SSM_STATE * SSM_GROUP),
    ("ssm_d", SSM_GROUPS * SSM_GROUP), ("cv_dw_w", CONV_KERNEL * CONV_WIDTH), ("cv_dw_b", CONV_WIDTH),
    ("cv_ln_g", CONV_WIDTH), ("cv_ln_b", CONV_WIDTH), ("ln1_g", D_MODEL), ("ln1_b", D_MODEL),
    ("ffn_dw_w", FFN_KERNEL * 2 * FFN_HIDDEN), ("ffn_dw_b", 2 * FFN_HIDDEN), ("ln2_g", D_MODEL), ("ln2_b", D_MODEL),
    ("loss", 1),
)
PACK_COLS = 1024
PACK_ROWS = 192
assert sum(n for _, n in PACK) <= PACK_ROWS * PACK_COLS


def _params(sem=None, **kw):
    return pltpu.CompilerParams(dimension_semantics=sem, vmem_limit_bytes=VMEM_LIMIT, **kw)


def _ln_stats(x):
    mu = jnp.mean(x, axis=-1, keepdims=True)
    xc = x - mu
    var = jnp.mean(xc * xc, axis=-1, keepdims=True)
    rstd = lax.rsqrt(var + LN_EPS)
    return xc * rstd, rstd


def _ln_bwd(dxhat, xhat, rstd):
    m1 = jnp.mean(dxhat, axis=-1, keepdims=True)
    m2 = jnp.mean(dxhat * xhat, axis=-1, keepdims=True)
    return rstd * (dxhat - m1 - xhat * m2)


def _sig(x):
    return 1.0 / (1.0 + jnp.exp(-x))


def _gelu(x):
    return 0.5 * x * (1.0 + lax.erf(x * (1.0 / math.sqrt(2.0))))


def _dgelu(x):
    return 0.5 * (1.0 + lax.erf(x * (1.0 / math.sqrt(2.0)))) + x * jnp.exp(-0.5 * x * x) * (1.0 / math.sqrt(2.0 * math.pi))


def _gelu_and_grad(x):
    er = lax.erf(x * (1.0 / math.sqrt(2.0)))
    cdf = 0.5 * (1.0 + er)
    return x * cdf, cdf + x * jnp.exp(-0.5 * x * x) * (1.0 / math.sqrt(2.0 * math.pi))


def _colsum(a):
    return jnp.sum(a, axis=0, keepdims=True)


def _fill_rotations(buf, rot, rows):
    for r in range(1, 8):
        rot[r - 1] = buf[pl.ds(r, rows), :]


def _rows_at(buf, rot, offset, tb):
    q, r = divmod(offset, 8)
    if r == 0:
        return buf[pl.ds(8 * q, tb), :]
    return rot[r - 1, pl.ds(8 * q, tb), :]


def _dot(a, b):
    return jnp.dot(a, b, preferred_element_type=F32)


def _dot_nt(a, b):
    return lax.dot_general(a, b, (((1,), (1,)), ((), ())), preferred_element_type=F32)


def _dot_tn(a, b):
    return lax.dot_general(a, b, (((0,), (0,)), ((), ())), preferred_element_type=F32)


def _load_once(src, dst, sem):
    cp = pltpu.make_async_copy(src, dst, sem)
    cp.start()
    cp.wait()


def _full(a):
    nd = a.ndim
    return pl.BlockSpec(a.shape, lambda *_: (0,) * nd)


ANY = pl.BlockSpec(memory_space=pl.ANY)


def _place():
    x, y, c = lax.axis_index("x"), lax.axis_index("y"), lax.axis_index("c")
    chips = [(1 - x, y), (x, 1 - y), (1 - x, 1 - y)]
    return x, y, c, chips


def _piece(kind, shape):
    r, cc = shape
    return (r // 2, cc // N_CHIP) if kind == "col" else (r // (2 * N_CHIP), cc)


def _piece_at(ref, kind, shape, s, k):
    pr, pc = _piece(kind, shape)
    if kind == "col":
        return ref.at[pl.ds(k * pr, pr), pl.ds(pl.multiple_of(s * pc, LANES), pc)]
    return ref.at[pl.ds(pl.multiple_of((2 * s + k) * pr, 16), pr), :]


def _gather_start(idx, sh, full, send, recv):
    x, y, c, chips = _place()
    for i, m in enumerate(idx):
        _, kind, shape = BIG[m]
        pr, _ = _piece(kind, shape)
        for j, chip in enumerate(chips):
            pltpu.make_async_remote_copy(
                src_ref=sh[i].at[pl.ds(pl.multiple_of(c * pr, 16), pr), :], dst_ref=_piece_at(full[i], kind, shape, 2 * x + y, c),
                send_sem=send.at[i, j], recv_sem=recv.at[i, j], device_id=(*chip, c), device_id_type=MESH).start()


def _gather_finish(idx, sh, full, send, recv, fsend, frecv):
    x, y, c, chips = _place()
    sibling = (x, y, 1 - c)
    waits = []
    for i, m in enumerate(idx):
        _, kind, shape = BIG[m]
        pr, _ = _piece(kind, shape)
        for j, (cx, cy) in enumerate(chips):
            got = _piece_at(full[i], kind, shape, 2 * cx + cy, c)
            first = pltpu.make_async_remote_copy(
                src_ref=sh[i].at[pl.ds(pl.multiple_of(c * pr, 16), pr), :], dst_ref=got, send_sem=send.at[i, j],
                recv_sem=recv.at[i, j], device_id=(cx, cy, c), device_id_type=MESH)
            first.wait_recv()
            fwd = pltpu.make_async_remote_copy(src_ref=got, dst_ref=got, send_sem=fsend.at[i, j], recv_sem=frecv.at[i, j],
                                               device_id=sibling, device_id_type=MESH)
            fwd.start()
            waits += [first.wait_send, fwd.wait_send]
    for i, m in enumerate(idx):
        _, kind, shape = BIG[m]
        for j, (cx, cy) in enumerate(chips):
            got = _piece_at(full[i], kind, shape, 2 * cx + cy, 1 - c)
            pltpu.make_async_remote_copy(src_ref=got, dst_ref=got, send_sem=fsend.at[i, j], recv_sem=frecv.at[i, j],
                                         device_id=sibling, device_id_type=MESH).wait_recv()
    for w in waits:
        w()


def _scatter(idx, dw, got, send, recv):
    x, y, c, _ = _place()
    cps = []
    for i, m in enumerate(idx):
        _, kind, shape = BIG[m]
        for r in range(1, N_DEV):
            tx, ty, tc = (1 - x if r & 4 else x), (1 - y if r & 2 else y), (1 - c if r & 1 else c)
            cps.append(pltpu.make_async_remote_copy(
                src_ref=_piece_at(dw[i], kind, shape, 2 * tx + ty, tc), dst_ref=got[i].at[r - 1],
                send_sem=send.at[i, r - 1], recv_sem=recv.at[i, r - 1], device_id=(tx, ty, tc), device_id_type=MESH))
    return cps


def _f1_inproj(x, modv, b_in, w_in, mid_sh, mid_full, tb):
    t = x.shape[0]
    nt = t // tb
    nl = len(MID)
    chunks = [(j * 512, 512) for j in range(IN_PROJ_WIDTH // 512)]

    def body(x_ref, modv_ref, b_ref, w_hbm, *rest):
        sh, full = rest[:nl], rest[2 * nl:3 * nl]
        u4_ref, prest_ref, h_ref, w_v, sem, send, recv, fsend, frecv = rest[3 * nl:]

        @pl.when(pl.program_id(0) == 0)
        def _():
            _gather_start(MID, sh, full, send, recv)
            _load_once(w_hbm, w_v, sem)

        xn, _ = _ln_stats(x_ref[...])
        h = (xn * (1.0 + modv_ref[1:2, :]) + modv_ref[0:1, :]).astype(BF16)
        h_ref[...] = h
        for c0, cw in chunks:
            p = _dot(h, w_v[:, c0:c0 + cw]) + b_ref[:, c0:c0 + cw]
            if c0 == 0:
                for b in range(N_LANE_BLOCKS):
                    u4_ref[b] = p[:, b * LANES:(b + 1) * LANES]
            else:
                prest_ref[:, c0 - SSM_WIDTH:c0 - SSM_WIDTH + cw] = p

        @pl.when(pl.program_id(0) == nt - 1)
        def _():
            _gather_finish(MID, sh, full, send, recv, fsend, frecv)

    gsem = pltpu.SemaphoreType.DMA((nl, 3))
    return pl.pallas_call(
        body, name="f1_inproj", grid=(nt,),
        in_specs=[pl.BlockSpec((tb, D_MODEL), lambda i: (i, 0)), _full(modv), _full(b_in), ANY] + [ANY] * (2 * nl),
        out_specs=[ANY] * nl + [pl.BlockSpec((N_LANE_BLOCKS, tb, LANES), lambda i: (0, i, 0)),
                                pl.BlockSpec((tb, IN_PROJ_WIDTH - SSM_WIDTH), lambda i: (i, 0)),
                                pl.BlockSpec((tb, D_MODEL), lambda i: (i, 0))],
        input_output_aliases={4 + nl + k: k for k in range(nl)},
        out_shape=[jax.ShapeDtypeStruct(f.shape, f.dtype) for f in mid_full]
        + [jax.ShapeDtypeStruct((N_LANE_BLOCKS, t, LANES), F32),
                   jax.ShapeDtypeStruct((t, IN_PROJ_WIDTH - SSM_WIDTH), F32),
                   jax.ShapeDtypeStruct((t, D_MODEL), BF16)],
        scratch_shapes=[pltpu.VMEM(w_in.shape, BF16), pltpu.SemaphoreType.DMA, gsem, gsem, gsem, gsem],
        compiler_params=_params(("arbitrary",)),
    )(x, modv, b_in, w_in, *mid_sh, *mid_full)


TAP_GROUPS = 8


def _dot_f32(a, b, dims):
    return lax.dot_general(a, b, (dims, ((), ())), precision=lax.Precision.HIGHEST, preferred_element_type=F32)


def _taps_fwd(car, cai, bt_r, bt_i):
    el, g, p, n = SSM_CHUNK, SSM_GROUPS, SSM_GROUP, SSM_STATE

    def body(ar_ref, ai_ref, br_ref, bi_ref, o_ref):
        for gl in range(TAP_GROUPS):
            a_r = jnp.concatenate([ar_ref[k, gl] for k in range(el)], axis=0)
            a_i = jnp.concatenate([ai_ref[k, gl] for k in range(el)], axis=0)
            o_ref[gl] = _dot_f32(br_ref[gl], a_r, ((1,), (1,))) - _dot_f32(bi_ref[gl], a_i, ((1,), (1,)))

    ablk = pl.BlockSpec((el + 1, TAP_GROUPS, p, n), lambda i: (0, i, 0, 0))
    bblk = pl.BlockSpec((TAP_GROUPS, p, n), lambda i: (i, 0, 0))
    return pl.pallas_call(
        body, name="s5_taps", grid=(g // TAP_GROUPS,), in_specs=[ablk, ablk, bblk, bblk],
        out_specs=pl.BlockSpec((TAP_GROUPS, p, el * p), lambda i: (i, 0, 0)),
        out_shape=jax.ShapeDtypeStruct((g, p, el * p), F32), compiler_params=_params(("arbitrary",)),
    )(car, cai, bt_r, bt_i)


def _taps_bwd(dr, car, cai, bt_r, bt_i):
    el, g, p, n = SSM_CHUNK, SSM_GROUPS, SSM_GROUP, SSM_STATE

    def body(dr_ref, ar_ref, ai_ref, br_ref, bi_ref, dar_ref, dai_ref, dbr_ref, dbi_ref):
        for gl in range(TAP_GROUPS):
            dv = dr_ref[gl]
            a_r = jnp.concatenate([ar_ref[k, gl] for k in range(el)], axis=0)
            a_i = jnp.concatenate([ai_ref[k, gl] for k in range(el)], axis=0)
            dbr_ref[gl] = _dot_f32(dv, a_r, ((1,), (0,)))
            dbi_ref[gl] = -_dot_f32(dv, a_i, ((1,), (0,)))
            da_r = _dot_f32(dv, br_ref[gl], ((0,), (0,)))
            da_i = -_dot_f32(dv, bi_ref[gl], ((0,), (0,)))
            for k in range(el):
                dar_ref[k, gl] = da_r[k * p:(k + 1) * p, :]
                dai_ref[k, gl] = da_i[k * p:(k + 1) * p, :]
            dar_ref[el, gl] = jnp.zeros((p, n), F32)
            dai_ref[el, gl] = jnp.zeros((p, n), F32)

    ablk = pl.BlockSpec((el + 1, TAP_GROUPS, p, n), lambda i: (0, i, 0, 0))
    bblk = pl.BlockSpec((TAP_GROUPS, p, n), lambda i: (i, 0, 0))
    return pl.pallas_call(
        body, name="s5_taps_bwd", grid=(g // TAP_GROUPS,),
        in_specs=[pl.BlockSpec((TAP_GROUPS, p, el * p), lambda i: (i, 0, 0)), ablk, ablk, bblk, bblk],
        out_specs=[ablk, ablk, bblk, bblk],
        out_shape=[jax.ShapeDtypeStruct(car.shape, F32), jax.ShapeDtypeStruct(car.shape, F32),
                   jax.ShapeDtypeStruct(bt_r.shape, F32), jax.ShapeDtypeStruct(bt_r.shape, F32)],
        compiler_params=_params(("arbitrary",)),
    )(dr, car, cai, bt_r, bt_i)


@jax.custom_vjp
def _taps(car, cai, bt_r, bt_i):
    return _taps_fwd(car, cai, bt_r, bt_i)


_taps.defvjp(lambda *ops: (_taps_fwd(*ops), ops), lambda ops, dr: _taps_bwd(dr, *ops))


def _s5_build(lam_re, lam_im, log_dt, b_re, b_im, c_re, c_im, d):
    el, g, n, p, nb = SSM_CHUNK, SSM_GROUPS, SSM_STATE, SSM_GROUP, N_LANE_BLOCKS
    lr = jnp.minimum(lam_re, -1e-4)
    li = lam_im
    dt = jnp.exp(log_dt)[:, None]
    mag = jnp.exp(lr * dt)
    ang = li * dt
    lbr, lbi = mag * jnp.cos(ang), mag * jnp.sin(ang)
    num_r, num_i = lbr - 1.0, lbi
    den = lr * lr + li * li
    coef_r = (num_r * lr + num_i * li) / den
    coef_i = (num_i * lr - num_r * li) / den
    bbar_r = coef_r[..., None] * b_re - coef_i[..., None] * b_im
    bbar_i = coef_r[..., None] * b_im + coef_i[..., None] * b_re
    k = jnp.arange(el + 1, dtype=F32)[:, None, None]
    pmag = jnp.exp(k * (lr * dt)[None])
    pr, pi = pmag * jnp.cos(k * ang[None]), pmag * jnp.sin(k * ang[None])
    car = c_re[None] * pr[:, :, None, :] - c_im[None] * pi[:, :, None, :]
    cai = c_re[None] * pi[:, :, None, :] + c_im[None] * pr[:, :, None, :]
    bt_r = bbar_r.transpose(0, 2, 1)
    bt_i = bbar_i.transpose(0, 2, 1)
    kern = _taps(car, cai, bt_r, bt_i).reshape(g, p, el, p).transpose(2, 0, 1, 3)
    kern = kern.at[0].add(jnp.eye(p, dtype=F32)[None] * d[:, None, :])
    bt_r, bt_i = bt_r[None], bt_i[None]
    kc = kern.reshape(el, g * p, p)
    rev = el - 1 - jnp.arange(el)
    qr, qi = pr[rev][:, :, None, :], pi[rev][:, :, None, :]
    sw_r = (qr * bt_r - qi * bt_i).reshape(el, g * p, n)
    sw_i = (qr * bt_i + qi * bt_r).reshape(el, g * p, n)
    sg_r = car[1:].reshape(el, g * p, n)
    sg_i = (-cai[1:]).reshape(el, g * p, n)
    a = jnp.stack([pr[el].reshape(nb, LANE_GROUPS * n), pi[el].reshape(nb, LANE_GROUPS * n)], axis=1)
    return kc, sw_r, sw_i, sg_r, sg_i, a


def _expand(src, reps):
    rows, w = src.shape
    cols = reps * w
    r = lax.broadcasted_iota(jnp.int32, (w, cols), 0)
    c = lax.broadcasted_iota(jnp.int32, (w, cols), 1)
    rep = (r == (c & (w - 1))).astype(BF16)
    out = _dot(src.astype(BF16), rep)
    rg = lax.broadcasted_iota(jnp.int32, (rows, cols), 0) // SSM_GROUP
    cg = lax.broadcasted_iota(jnp.int32, (rows, cols), 1) // w
    return jnp.where(rg == cg, out, 0.0).astype(BF16)


def _fold(x, w):
    rows, cols = x.shape
    rg = lax.broadcasted_iota(jnp.int32, (rows, cols), 0) // SSM_GROUP
    cg = lax.broadcasted_iota(jnp.int32, (rows, cols), 1) // w
    x = jnp.where(rg == cg, x, 0.0)
    while cols > LANES:
        x = x[:, :cols // 2] + x[:, cols // 2:]
        cols //= 2
    s = LANES // 2
    while s >= w:
        x = x + pltpu.roll(x, s, axis=1)
        s //= 2
    return x[:, :w]


def _build_maps(s_ref, dst):
    for j in range(SSM_CHUNK):
        dst[j * LANES:(j + 1) * LANES, :] = _expand(s_ref[j], LANE_GROUPS)


def _build_toeplitz(kc_ref, dst):
    dst[...] = jnp.zeros_like(dst)
    for d in range(SSM_CHUNK):
        blk = _expand(kc_ref[d], LANE_GROUPS)
        for ji in range(SSM_CHUNK - d):
            jo = ji + d
            dst[ji * LANES:(ji + 1) * LANES, jo * LANES:(jo + 1) * LANES] = blk


def _cblk(w):
    return pl.BlockSpec((SSM_CHUNK, LANES, w), lambda b: (0, b, 0))


def _tblk(t):
    return pl.BlockSpec((1, t, LANES), lambda b: (b, 0, 0))


def _load_chunks(ref, nc):
    return jnp.concatenate([ref[0, pl.ds(j, nc, stride=SSM_CHUNK), :] for j in range(SSM_CHUNK)], axis=-1).astype(BF16)


def _store_chunks(ref, val, nc):
    for j in range(SSM_CHUNK):
        ref[0, pl.ds(j, nc, stride=SSM_CHUNK), :] = val[:, j * LANES:(j + 1) * LANES]


def _s5a_state(u4, sw_r, sw_i, a8):
    nb, t, _ = u4.shape
    nc = t // SSM_CHUNK
    sc = STATE_COLS

    def body(u_ref, swr_ref, swi_ref, a_ref, hr_ref, hi_ref, w_s, xr_s, xi_s):
        u = _load_chunks(u_ref, nc)
        _build_maps(swr_ref, w_s)
        xr_s[...] = _dot(u, w_s[...])
        _build_maps(swi_ref, w_s)
        xi_s[...] = _dot(u, w_s[...])
        ar = a_ref[0, 0:1, :]
        ai = a_ref[0, 1:2, :]

        def step(c, carry):
            hr, hi = carry
            hr_ref[0, pl.ds(c, 1), :] = hr
            hi_ref[0, pl.ds(c, 1), :] = hi
            xr = xr_s[pl.ds(c, 1), :]
            xi = xi_s[pl.ds(c, 1), :]
            return ar * hr - ai * hi + xr, ar * hi + ai * hr + xi

        z = jnp.zeros((1, sc), F32)
        lax.fori_loop(0, nc, step, (z, z))

    return pl.pallas_call(
        body, name="s5a_state", grid=(nb,),
        in_specs=[_tblk(t), _cblk(SSM_STATE), _cblk(SSM_STATE),
                  pl.BlockSpec((1, 8, sc), lambda b: (b, 0, 0))],
        out_specs=[pl.BlockSpec((1, nc, sc), lambda b: (b, 0, 0))] * 2,
        out_shape=[jax.ShapeDtypeStruct((nb, nc, sc), F32)] * 2,
        scratch_shapes=[pltpu.VMEM((CHUNK_COLS, sc), BF16), pltpu.VMEM((nc, sc), F32), pltpu.VMEM((nc, sc), F32)],
        compiler_params=_params(("arbitrary",)),
    )(u4, sw_r, sw_i, a8)


def _s5b_out(u4, kc, sg_r, sg_i, hr, hi):
    nb, t, _ = u4.shape
    nc = t // SSM_CHUNK
    sc = STATE_COLS
    cw = 512

    def body(u_ref, kc_ref, sgr_ref, sgi_ref, hr_ref, hi_ref, y_ref, tm_s, gr_s, gi_s):
        _build_toeplitz(kc_ref, tm_s)
        _build_maps(sgr_ref, gr_s)
        _build_maps(sgi_ref, gi_s)
        u = _load_chunks(u_ref, nc)
        h_r = hr_ref[0].astype(BF16)
        h_i = hi_ref[0].astype(BF16)
        for j in range(CHUNK_COLS // cw):
            cs = slice(j * cw, (j + 1) * cw)
            y = _dot(u, tm_s[:, cs]) + _dot_nt(h_r, gr_s[cs, :]) + _dot_nt(h_i, gi_s[cs, :])
            for q in range(cw // LANES):
                step = j * (cw // LANES) + q
                y_ref[0, pl.ds(step, nc, stride=SSM_CHUNK), :] = y[:, q * LANES:(q + 1) * LANES]

    return pl.pallas_call(
        body, name="s5b_out", grid=(nb,),
        in_specs=[_tblk(t), _cblk(SSM_GROUP), _cblk(SSM_STATE),
                  _cblk(SSM_STATE), pl.BlockSpec((1, nc, sc), lambda b: (b, 0, 0)),
                  pl.BlockSpec((1, nc, sc), lambda b: (b, 0, 0))],
        out_specs=_tblk(t),
        out_shape=jax.ShapeDtypeStruct((nb, t, LANES), F32),
        scratch_shapes=[pltpu.VMEM((CHUNK_COLS, CHUNK_COLS), BF16), pltpu.VMEM((CHUNK_COLS, sc), BF16),
                        pltpu.VMEM((CHUNK_COLS, sc), BF16)],
        compiler_params=_params(("arbitrary",)),
    )(u4, kc, sg_r, sg_i, hr, hi)


def _f4_mixer(ys4, prest, x, modv, cvv, cw32, w_a, w_b, w_pw, w_out, late_sh, late_full, tb):
    t = x.shape[0]
    hb = tb // CONV_HALO
    nt = t // tb
    nl = len(LATE)

    def body(ys_ref, pr_ref, halo_ref, x_ref, modv_ref, cvv_ref, cw_ref, wa_ref, wb_ref, wpw_ref, wout_ref, *rest):
        sh, full = rest[:nl], rest[2 * nl:3 * nl]
        r1_ref, ya_ref, yb_ref, ycv_ref, vc_ref, yg_ref, vs_ref, mg_ref, vbuf, vrot, send, recv, fsend, frecv = rest[3 * nl:]
        i = pl.program_id(0)

        @pl.when(i == 0)
        def _():
            _gather_start(LATE, sh, full, send, recv)

        ys = jnp.concatenate([ys_ref[b] for b in range(N_LANE_BLOCKS)], axis=-1)
        yg = _gelu(ys).astype(BF16)
        yg_ref[...] = yg
        ya = _dot(yg, wa_ref[...])
        yb = _dot(yg, wb_ref[...])
        ya_ref[...] = ya.astype(BF16)
        yb_ref[...] = yb.astype(BF16)
        yssm = ya * _sig(yb)
        hv = halo_ref[:, 0:CONV_WIDTH] * _sig(halo_ref[:, CONV_WIDTH:2 * CONV_WIDTH])
        vbuf[0:CONV_HALO, :] = jnp.where(i == 0, 0.0, hv)
        vbuf[CONV_HALO:, :] = pr_ref[:, 0:CONV_WIDTH] * _sig(pr_ref[:, CONV_WIDTH:2 * CONV_WIDTH])
        _fill_rotations(vbuf, vrot, tb + CONV_HALO - 8)
        acc = jnp.zeros((tb, CONV_WIDTH), F32)
        for k in range(CONV_KERNEL):
            acc += _rows_at(vbuf, vrot, CONV_HALO - CONV_KERNEL + 1 + k, tb) * cw_ref[k:k + 1, :]
        vc = acc + cvv_ref[0:1, :]
        vc_ref[...] = vc
        xh, _ = _ln_stats(vc)
        vl = xh * cvv_ref[1:2, :] + cvv_ref[2:3, :]
        vs = (vl * _sig(vl)).astype(BF16)
        vs_ref[...] = vs
        ycv = _dot(vs, wpw_ref[...])
        ycv_ref[...] = ycv.astype(BF16)
        gs = pr_ref[:, 2 * CONV_WIDTH:2 * CONV_WIDTH + D_MODEL]
        gc = pr_ref[:, 2 * CONV_WIDTH + D_MODEL:]
        merged = (_sig(gs) * yssm + _sig(gc) * ycv).astype(BF16)
        mg_ref[...] = merged
        ym = _dot(merged, wout_ref[...])
        r1_ref[...] = ALPHA * x_ref[...] + modv_ref[2:3, :] * ym

        @pl.when(i == nt - 1)
        def _():
            _gather_finish(LATE, sh, full, send, recv, fsend, frecv)

    tok = lambda w: pl.BlockSpec((tb, w), lambda i: (i, 0))
    sem = pltpu.SemaphoreType.DMA((nl, 3))
    n_in = 11
    return pl.pallas_call(
        body, name="f4_mixer", grid=(nt,),
        in_specs=[pl.BlockSpec((N_LANE_BLOCKS, tb, LANES), lambda i: (0, i, 0)), tok(prest.shape[1]),
                  pl.BlockSpec((CONV_HALO, 2 * CONV_WIDTH), lambda i: (jnp.maximum(i * hb - 1, 0), 0)),
                  tok(D_MODEL), _full(modv), _full(cvv), _full(cw32), _full(w_a), _full(w_b), _full(w_pw), _full(w_out)]
        + [ANY] * (2 * nl),
        out_specs=[ANY] * nl + [tok(D_MODEL), tok(D_MODEL), tok(D_MODEL), tok(D_MODEL), tok(CONV_WIDTH), tok(SSM_WIDTH),
                                tok(CONV_WIDTH), tok(D_MODEL)],
        input_output_aliases={n_in + nl + k: k for k in range(nl)},
        out_shape=[jax.ShapeDtypeStruct(f.shape, f.dtype) for f in late_full]
        + [jax.ShapeDtypeStruct((t, D_MODEL), F32), jax.ShapeDtypeStruct((t, D_MODEL), BF16),
                   jax.ShapeDtypeStruct((t, D_MODEL), BF16), jax.ShapeDtypeStruct((t, D_MODEL), BF16),
                   jax.ShapeDtypeStruct((t, CONV_WIDTH), F32), jax.ShapeDtypeStruct((t, SSM_WIDTH), BF16),
                   jax.ShapeDtypeStruct((t, CONV_WIDTH), BF16), jax.ShapeDtypeStruct((t, D_MODEL), BF16)],
        scratch_shapes=[pltpu.VMEM((tb + CONV_HALO, CONV_WIDTH), F32),
                        pltpu.VMEM((7, tb + CONV_HALO - 8, CONV_WIDTH), F32), sem, sem, sem, sem],
        compiler_params=_params(("arbitrary",)),
    )(ys4, prest, prest, x, modv, cvv, cw32, w_a, w_b, w_pw, w_out, *late_sh, *late_full)


FFN_COLS = 1408


def _f5_ffn(r1, tgt, modv, lnv, fdw, w_up, w_down, tb):
    t = r1.shape[0]
    fw = 2 * FFN_HIDDEN

    def body(r1_ref, tgt_ref, modv_ref, lnv_ref, fdw_ref, wup_hbm, wdn_hbm,
             dr2_ref, d_ref, up_ref, z_ref, acc_ref, wup_v, wdn_v, upbuf, gbuf, hbuf, sems):
        i = pl.program_id(0)

        @pl.when(i == 0)
        def _():
            _load_once(wup_hbm, wup_v, sems.at[0])
            _load_once(wdn_hbm, wdn_v, sems.at[1])
            acc_ref[...] = jnp.zeros_like(acc_ref)
            upbuf[0:8, :] = jnp.zeros((8, fw), F32)

        xh1, _ = _ln_stats(r1_ref[...])
        x1 = xh1 * lnv_ref[0:1, :] + lnv_ref[1:2, :]
        xn2, _ = _ln_stats(x1)
        h2 = (xn2 * (1.0 + modv_ref[4:5, :]) + modv_ref[3:4, :]).astype(BF16)
        for j in range(fw // FFN_COLS):
            cs = slice(j * FFN_COLS, (j + 1) * FFN_COLS)
            up = _dot(h2, wup_v[:, cs])
            upbuf[8:, cs] = up
            up_ref[:, cs] = up.astype(BF16)

        def conv(cs):
            return (fdw_ref[0:1, cs] * upbuf[pl.ds(6, tb), cs] + fdw_ref[1:2, cs] * upbuf[pl.ds(7, tb), cs]
                    + fdw_ref[2:3, cs] * upbuf[pl.ds(8, tb), cs] + fdw_ref[3:4, cs])

        halves = [(slice(j * FFN_COLS, (j + 1) * FFN_COLS),
                   slice(FFN_HIDDEN + j * FFN_COLS, FFN_HIDDEN + (j + 1) * FFN_COLS)) for j in range(FFN_HIDDEN // FFN_COLS)]
        yf = jnp.zeros((tb, D_MODEL), F32)
        for ca, cv in halves:
            v = conv(cv)
            g, dg = _gelu_and_grad(conv(ca))
            gbuf[:, ca] = g.astype(BF16)
            hbuf[:, ca] = (v * dg).astype(BF16)
            z = (g * v).astype(BF16)
            z_ref[:, ca] = z
            yf += _dot(z, wdn_v[ca, :])
        r2 = ALPHA * x1 + modv_ref[5:6, :] * yf
        xh2, rstd2 = _ln_stats(r2)
        e = xh2 * lnv_ref[2:3, :] + lnv_ref[3:4, :] - tgt_ref[...]
        dx2 = e * (1.0 / D_MODEL)
        acc_ref[3:4, :] += _colsum(e * e) * (0.5 / D_MODEL)
        acc_ref[0:1, :] += _colsum(dx2 * xh2)
        acc_ref[1:2, :] += _colsum(dx2)
        dr2 = _ln_bwd(dx2 * lnv_ref[2:3, :], xh2, rstd2)
        dr2_ref[...] = dr2
        acc_ref[2:3, :] += _colsum(dr2 * yf)
        dyf = (modv_ref[5:6, :] * dr2).astype(BF16)
        for ca, cv in halves:
            dz = _dot_nt(dyf, wdn_v[ca, :])
            d_ref[:, ca] = (dz * hbuf[:, ca].astype(F32)).astype(BF16)
            d_ref[:, cv] = (dz * gbuf[:, ca].astype(F32)).astype(BF16)
        upbuf[0:8, :] = upbuf[pl.ds(tb, 8), :]

    tok = lambda w: pl.BlockSpec((tb, w), lambda i: (i, 0))
    return pl.pallas_call(
        body, name="f5_ffn", grid=(t // tb,),
        in_specs=[tok(D_MODEL), tok(D_MODEL), _full(modv), _full(lnv), _full(fdw), ANY, ANY],
        out_specs=[tok(D_MODEL), tok(fw), tok(fw), tok(FFN_HIDDEN), pl.BlockSpec((8, D_MODEL), lambda i: (0, 0))],
        out_shape=[jax.ShapeDtypeStruct((t, D_MODEL), F32), jax.ShapeDtypeStruct((t, fw), BF16),
                   jax.ShapeDtypeStruct((t, fw), BF16), jax.ShapeDtypeStruct((t, FFN_HIDDEN), BF16),
                   jax.ShapeDtypeStruct((8, D_MODEL), F32)],
        scratch_shapes=[pltpu.VMEM(w_up.shape, BF16), pltpu.VMEM(w_down.shape, BF16),
                        pltpu.VMEM((tb + 8, fw), F32), pltpu.VMEM((tb, FFN_HIDDEN), BF16),
                        pltpu.VMEM((tb, FFN_HIDDEN), BF16), pltpu.SemaphoreType.DMA((2,))],
        compiler_params=_params(("arbitrary",)),
    )(r1, tgt, modv, lnv, fdw, w_up, w_down)


def _b1b_ffn_up(d, up, dr2, r1, modv, lnv, fdw, w_up, tb):
    t = dr2.shape[0]
    fw = 2 * FFN_HIDDEN
    nt = t // tb
    hb = tb // 16

    def body(d_ref, nxt_ref, up_ref, dr2_ref, r1_ref, modv_ref, lnv_ref, fdw_ref, wup_hbm, dup_ref, dr1_ref, h2_ref,
             dyf_ref, acc_ref, accw_ref, wup_v, dbuf, shifted, sem):
        i = pl.program_id(0)

        @pl.when(i == 0)
        def _():
            _load_once(wup_hbm, wup_v, sem)
            acc_ref[...] = jnp.zeros_like(acc_ref)
            accw_ref[...] = jnp.zeros_like(accw_ref)

        dbuf[0:tb, :] = d_ref[...].astype(F32)
        dbuf[tb:, :] = jnp.where(i == nt - 1, 0.0, nxt_ref[...].astype(F32))
        dh2 = jnp.zeros((tb, D_MODEL), F32)
        for j in range(fw // FFN_COLS):
            cs = slice(j * FFN_COLS, (j + 1) * FFN_COLS)
            for k in range(1, FFN_KERNEL):
                shifted[k - 1] = dbuf[pl.ds(k, tb), cs]
            ds = [dbuf[pl.ds(0, tb), cs], shifted[0], shifted[1]]
            dup = (fdw_ref[2:3, cs] * ds[0] + fdw_ref[1:2, cs] * ds[1] + fdw_ref[0:1, cs] * ds[2]).astype(BF16)
            dup_ref[:, cs] = dup
            dh2 += _dot_nt(dup, wup_v[:, cs])
            upf = up_ref[:, cs].astype(F32)
            for k in range(FFN_KERNEL):
                accw_ref[k:k + 1, cs] += _colsum(ds[FFN_KERNEL - 1 - k] * upf)
            accw_ref[3:4, cs] += _colsum(ds[0])
        xh1, rstd1 = _ln_stats(r1_ref[...])
        x1 = xh1 * lnv_ref[0:1, :] + lnv_ref[1:2, :]
        xn2, rstd2 = _ln_stats(x1)
        h2_ref[...] = (xn2 * (1.0 + modv_ref[4:5, :]) + modv_ref[3:4, :]).astype(BF16)
        dr2 = dr2_ref[...]
        dyf_ref[...] = (modv_ref[5:6, :] * dr2).astype(BF16)
        acc_ref[0:1, :] += _colsum(dh2 * xn2)
        acc_ref[1:2, :] += _colsum(dh2)
        dx1 = _ln_bwd(dh2 * (1.0 + modv_ref[4:5, :]), xn2, rstd2) + ALPHA * dr2
        acc_ref[2:3, :] += _colsum(dx1 * xh1)
        acc_ref[3:4, :] += _colsum(dx1)
        dr1_ref[...] = _ln_bwd(dx1 * lnv_ref[0:1, :], xh1, rstd1)

    tok = lambda w: pl.BlockSpec((tb, w), lambda i: (i, 0))
    return pl.pallas_call(
        body, name="b1b_ffn_up", grid=(nt,),
        in_specs=[tok(fw), pl.BlockSpec((16, fw), lambda i: (jnp.minimum((i + 1) * hb, t // 16 - 1), 0)), tok(fw),
                  tok(D_MODEL), tok(D_MODEL), _full(modv), _full(lnv), _full(fdw), ANY],
        out_specs=[tok(fw), tok(D_MODEL), tok(D_MODEL), tok(D_MODEL), pl.BlockSpec((8, D_MODEL), lambda i: (0, 0)),
                   pl.BlockSpec((8, fw), lambda i: (0, 0))],
        out_shape=[jax.ShapeDtypeStruct((t, fw), BF16), jax.ShapeDtypeStruct((t, D_MODEL), F32),
                   jax.ShapeDtypeStruct((t, D_MODEL), BF16), jax.ShapeDtypeStruct((t, D_MODEL), BF16),
                   jax.ShapeDtypeStruct((8, D_MODEL), F32), jax.ShapeDtypeStruct((8, fw), F32)],
        scratch_shapes=[pltpu.VMEM(w_up.shape, BF16), pltpu.VMEM((tb + 16, fw), F32),
                        pltpu.VMEM((FFN_KERNEL - 1, tb, FFN_COLS), F32), pltpu.SemaphoreType.DMA],
        compiler_params=_params(("arbitrary",)),
    )(d, d, up, dr2, r1, modv, lnv, fdw, w_up)


def _b2_mixer(dr1, ys4, prest, ya, yb, ycv, vc, merged, modv, cvv, cw32, w_a, w_b, w_pw, w_out, late_dw, tb):
    t = dr1.shape[0]
    nt = t // tb
    nl = len(LATE)
    hb = tb // CONV_HALO
    cwd = CONV_WIDTH

    def body(dr1_ref, ys_ref, pr_ref, halo_ref, ya_ref, yb_ref, ycv_ref, vc_ref, mg_ref, modv_ref, cvv_ref, cw_ref,
             wa_ref, wb_ref, wpw_ref, wout_ref, *rest):
        dw, got = rest[:nl], rest[nl:2 * nl]
        (dys_ref, dpr_ref, dya_ref, dyb_ref, dycv_ref, dym_ref, acc_a, acc_b, acc_w, vbuf, dvbuf, vrot, dvrot,
         send, recv) = rest[2 * nl:]
        i = pl.program_id(0)
        ti = nt - 1 - i

        @pl.when(i == 0)
        def _():
            for cp in _scatter(LATE, dw, got, send, recv):
                cp.start()
            acc_a[...] = jnp.zeros_like(acc_a)
            acc_b[...] = jnp.zeros_like(acc_b)
            acc_w[...] = jnp.zeros_like(acc_w)
            dvbuf[pl.ds(tb, CONV_HALO), :] = jnp.zeros((CONV_HALO, cwd), F32)

        dr1 = dr1_ref[...]
        dym = (modv_ref[2:3, :] * dr1).astype(BF16)
        dym_ref[...] = dym
        ym = _dot(mg_ref[...], wout_ref[...])
        acc_a[0:1, :] += _colsum(dr1 * ym)
        dmg = _dot_nt(dym, wout_ref[...])
        sgs = _sig(pr_ref[:, 2 * cwd:2 * cwd + D_MODEL])
        sgc = _sig(pr_ref[:, 2 * cwd + D_MODEL:])
        ya_v = ya_ref[...].astype(F32)
        syb = _sig(yb_ref[...].astype(F32))
        ycv_v = ycv_ref[...].astype(F32)
        dpr_ref[:, 2 * cwd:2 * cwd + D_MODEL] = (dmg * (ya_v * syb) * sgs * (1.0 - sgs)).astype(BF16)
        dpr_ref[:, 2 * cwd + D_MODEL:] = (dmg * ycv_v * sgc * (1.0 - sgc)).astype(BF16)
        dyssm = dmg * sgs
        dya = (dyssm * syb).astype(BF16)
        dyb = (dyssm * ya_v * syb * (1.0 - syb)).astype(BF16)
        dya_ref[...] = dya
        dyb_ref[...] = dyb
        dyg = _dot_nt(dya, wa_ref[...]) + _dot_nt(dyb, wb_ref[...])
        ys = jnp.concatenate([ys_ref[b] for b in range(N_LANE_BLOCKS)], axis=-1)
        dys = dyg * _dgelu(ys)
        for b in range(N_LANE_BLOCKS):
            dys_ref[b] = dys[:, b * LANES:(b + 1) * LANES]
        dycv = (dmg * sgc).astype(BF16)
        dycv_ref[...] = dycv
        dvs = _dot_nt(dycv, wpw_ref[...])
        xh, rstd = _ln_stats(vc_ref[...])
        vl = xh * cvv_ref[1:2, :] + cvv_ref[2:3, :]
        s = _sig(vl)
        dvl = dvs * s * (1.0 + vl * (1.0 - s))
        acc_b[1:2, :] += _colsum(dvl * xh)
        acc_b[2:3, :] += _colsum(dvl)
        dvc = _ln_bwd(dvl * cvv_ref[1:2, :], xh, rstd)
        acc_b[0:1, :] += _colsum(dvc)
        hv = halo_ref[:, 0:cwd] * _sig(halo_ref[:, cwd:2 * cwd])
        vbuf[0:CONV_HALO, :] = jnp.where(ti == 0, 0.0, hv)
        cva = pr_ref[:, 0:cwd]
        scg = _sig(pr_ref[:, cwd:2 * cwd])
        vbuf[CONV_HALO:, :] = cva * scg
        dvbuf[0:tb, :] = dvc
        _fill_rotations(vbuf, vrot, tb + CONV_HALO - 8)
        _fill_rotations(dvbuf, dvrot, tb + CONV_HALO - 8)
        dv = jnp.zeros((tb, cwd), F32)
        for k in range(CONV_KERNEL):
            dv += _rows_at(dvbuf, dvrot, CONV_KERNEL - 1 - k, tb) * cw_ref[k:k + 1, :]
            acc_w[k:k + 1, :] += _colsum(dvc * _rows_at(vbuf, vrot, CONV_HALO - CONV_KERNEL + 1 + k, tb))
        dvbuf[pl.ds(tb, CONV_HALO), :] = dvbuf[0:CONV_HALO, :]
        dpr_ref[:, 0:cwd] = (dv * scg).astype(BF16)
        dpr_ref[:, cwd:2 * cwd] = (dv * cva * scg * (1.0 - scg)).astype(BF16)

        @pl.when(i == nt - 1)
        def _():
            for cp in _scatter(LATE, dw, got, send, recv):
                cp.wait()

    rtok = lambda w: pl.BlockSpec((tb, w), lambda i: (nt - 1 - i, 0))
    r4 = pl.BlockSpec((N_LANE_BLOCKS, tb, LANES), lambda i: (0, nt - 1 - i, 0))
    pw = prest.shape[1]
    return pl.pallas_call(
        body, name="b2_mixer", grid=(nt,),
        in_specs=[rtok(D_MODEL), r4, rtok(pw),
                  pl.BlockSpec((CONV_HALO, 2 * cwd), lambda i: (jnp.maximum((nt - 1 - i) * hb - 1, 0), 0)),
                  rtok(D_MODEL), rtok(D_MODEL), rtok(D_MODEL), rtok(cwd), rtok(D_MODEL),
                  _full(modv), _full(cvv), _full(cw32), _full(w_a), _full(w_b), _full(w_pw), _full(w_out)] + [ANY] * nl,
        out_specs=[ANY] * nl + [r4, rtok(pw), rtok(D_MODEL), rtok(D_MODEL), rtok(D_MODEL), rtok(D_MODEL),
                   pl.BlockSpec((8, D_MODEL), lambda i: (0, 0)), pl.BlockSpec((8, cwd), lambda i: (0, 0)),
                   pl.BlockSpec((CONV_HALO, cwd), lambda i: (0, 0))],
        out_shape=[jax.ShapeDtypeStruct((N_DEV - 1,) + _piece(*BIG[m][1:]), BF16) for m in LATE]
        + [jax.ShapeDtypeStruct((N_LANE_BLOCKS, t, LANES), F32), jax.ShapeDtypeStruct((t, pw), BF16),
                   jax.ShapeDtypeStruct((t, D_MODEL), BF16), jax.ShapeDtypeStruct((t, D_MODEL), BF16),
                   jax.ShapeDtypeStruct((t, D_MODEL), BF16), jax.ShapeDtypeStruct((t, D_MODEL), BF16),
                   jax.ShapeDtypeStruct((8, D_MODEL), F32), jax.ShapeDtypeStruct((8, cwd), F32),
                   jax.ShapeDtypeStruct((CONV_HALO, cwd), F32)],
        scratch_shapes=[pltpu.VMEM((tb + CONV_HALO, cwd), F32), pltpu.VMEM((tb + CONV_HALO, cwd), F32),
                        pltpu.VMEM((7, tb + CONV_HALO - 8, cwd), F32), pltpu.VMEM((7, tb + CONV_HALO - 8, cwd), F32),
                        pltpu.SemaphoreType.DMA((nl, N_DEV - 1)), pltpu.SemaphoreType.DMA((nl, N_DEV - 1))],
        compiler_params=_params(("arbitrary",)),
    )(dr1, ys4, prest, prest, ya, yb, ycv, vc, merged, modv, cvv, cw32, w_a, w_b, w_pw, w_out, *late_dw)


def _s5c_state_bwd(dy4, sg_r, sg_i, a8, hr, hi):
    nb, t, _ = dy4.shape
    nc = t // SSM_CHUNK
    sc = STATE_COLS

    def body(dy_ref, sgr_ref, sgi_ref, a_ref, hr_ref, hi_ref, dxr_ref, dxi_ref, da_ref, dsgr_ref, dsgi_ref,
             g_s, lr_s, li_s, xr_s, xi_s):
        dy = _load_chunks(dy_ref, nc)
        _build_maps(sgr_ref, g_s)
        lr_s[...] = _dot(dy, g_s[...])
        _build_maps(sgi_ref, g_s)
        li_s[...] = _dot(dy, g_s[...])
        ar = a_ref[0, 0:1, :]
        ai = a_ref[0, 1:2, :]

        def step(k, carry):
            pr, pi, dar, dai = carry
            c = nc - 1 - k
            xr_s[pl.ds(c, 1), :] = pr
            xi_s[pl.ds(c, 1), :] = pi
            h_r = hr_ref[0, pl.ds(c, 1), :]
            h_i = hi_ref[0, pl.ds(c, 1), :]
            dar = dar + pr * h_r + pi * h_i
            dai = dai - pr * h_i + pi * h_r
            nr = lr_s[pl.ds(c, 1), :] + ar * pr + ai * pi
            ni = li_s[pl.ds(c, 1), :] - ai * pr + ar * pi
            return nr, ni, dar, dai

        z = jnp.zeros((1, sc), F32)
        _, _, dar, dai = lax.fori_loop(0, nc, step, (z, z, z, z))
        da_ref[0] = jnp.concatenate([dar, dai, jnp.zeros((6, sc), F32)], axis=0)
        dxr_ref[0] = xr_s[...].astype(BF16)
        dxi_ref[0] = xi_s[...].astype(BF16)
        for h_ref, o_ref in ((hr_ref, dsgr_ref), (hi_ref, dsgi_ref)):
            hb = h_ref[0].astype(BF16)
            for j in range(SSM_CHUNK):
                o_ref[j] = _fold(_dot_tn(dy[:, j * LANES:(j + 1) * LANES], hb), SSM_STATE)

    blk = lambda r, c: pl.BlockSpec((1, r, c), lambda b: (b, 0, 0))
    return pl.pallas_call(
        body, name="s5c_state_bwd", grid=(nb,),
        in_specs=[_tblk(t), _cblk(SSM_STATE), _cblk(SSM_STATE), blk(8, sc), blk(nc, sc), blk(nc, sc)],
        out_specs=[blk(nc, sc), blk(nc, sc), blk(8, sc), _cblk(SSM_STATE), _cblk(SSM_STATE)],
        out_shape=[jax.ShapeDtypeStruct((nb, nc, sc), BF16), jax.ShapeDtypeStruct((nb, nc, sc), BF16),
                   jax.ShapeDtypeStruct((nb, 8, sc), F32),
                   jax.ShapeDtypeStruct((SSM_CHUNK, SSM_WIDTH, SSM_STATE), F32),
                   jax.ShapeDtypeStruct((SSM_CHUNK, SSM_WIDTH, SSM_STATE), F32)],
        scratch_shapes=[pltpu.VMEM((CHUNK_COLS, sc), BF16)] + [pltpu.VMEM((nc, sc), F32)] * 4,
        compiler_params=_params(("arbitrary",)),
    )(dy4, sg_r, sg_i, a8, hr, hi)


def _s5d_input_bwd(dy4, u4, kc, sw_r, sw_i, dxr, dxi, mid_dw):
    nb, t, _ = dy4.shape
    nc = t // SSM_CHUNK
    sc = STATE_COLS
    nl = len(MID)

    def body(dy_ref, u_ref, kc_ref, swr_ref, swi_ref, dxr_ref, dxi_ref, *rest):
        dw, got = rest[:nl], rest[nl:2 * nl]
        du_ref, dkc_ref, dswr_ref, dswi_ref, tm_s, w_s, dk_s, send, recv = rest[2 * nl:]

        @pl.when(pl.program_id(0) == 0)
        def _():
            for cp in _scatter(MID, dw, got, send, recv):
                cp.start()

        dy = _load_chunks(dy_ref, nc)
        u = _load_chunks(u_ref, nc)
        _build_toeplitz(kc_ref, tm_s)
        du = _dot_nt(dy, tm_s[...])
        _build_maps(swr_ref, w_s)
        du += _dot_nt(dxr_ref[0], w_s[...])
        _build_maps(swi_ref, w_s)
        du += _dot_nt(dxi_ref[0], w_s[...])
        _store_chunks(du_ref, du, nc)
        dk_s[...] = jnp.zeros_like(dk_s)
        for ji in range(SSM_CHUNK):
            uj = u[:, ji * LANES:(ji + 1) * LANES]
            rows = _dot_tn(uj, dy)
            for jo in range(ji, SSM_CHUNK):
                dk_s[jo - ji] += rows[:, jo * LANES:(jo + 1) * LANES]
            dswr_ref[ji] = _fold(_dot_tn(uj, dxr_ref[0]), SSM_STATE)
            dswi_ref[ji] = _fold(_dot_tn(uj, dxi_ref[0]), SSM_STATE)
        for d in range(SSM_CHUNK):
            dkc_ref[d] = _fold(dk_s[d], SSM_GROUP)

        @pl.when(pl.program_id(0) == nb - 1)
        def _():
            for cp in _scatter(MID, dw, got, send, recv):
                cp.wait()

    blk = lambda r, c: pl.BlockSpec((1, r, c), lambda b: (b, 0, 0))
    ssem = pltpu.SemaphoreType.DMA((nl, N_DEV - 1))
    return pl.pallas_call(
        body, name="s5d_input_bwd", grid=(nb,),
        in_specs=[_tblk(t), _tblk(t), _cblk(SSM_GROUP), _cblk(SSM_STATE), _cblk(SSM_STATE),
                  blk(nc, sc), blk(nc, sc)] + [ANY] * nl,
        out_specs=[ANY] * nl + [_tblk(t), _cblk(SSM_GROUP), _cblk(SSM_STATE), _cblk(SSM_STATE)],
        out_shape=[jax.ShapeDtypeStruct((N_DEV - 1,) + _piece(*BIG[m][1:]), BF16) for m in MID]
        + [jax.ShapeDtypeStruct((nb, t, LANES), F32),
           jax.ShapeDtypeStruct((SSM_CHUNK, SSM_WIDTH, SSM_GROUP), F32),
           jax.ShapeDtypeStruct((SSM_CHUNK, SSM_WIDTH, SSM_STATE), F32),
           jax.ShapeDtypeStruct((SSM_CHUNK, SSM_WIDTH, SSM_STATE), F32)],
        scratch_shapes=[pltpu.VMEM((CHUNK_COLS, CHUNK_COLS), BF16), pltpu.VMEM((CHUNK_COLS, sc), BF16),
                        pltpu.VMEM((SSM_CHUNK, LANES, LANES), F32), ssem, ssem],
        compiler_params=_params(("arbitrary",)),
    )(dy4, u4, kc, sw_r, sw_i, dxr, dxi, *mid_dw)


def _b3_inproj(x, dr1, du4, dprest, modv, w_in, tb):
    t = x.shape[0]
    pw = IN_PROJ_WIDTH - SSM_WIDTH

    def body(x_ref, dr1_ref, du_ref, dpr_ref, modv_ref, w_hbm, gx_ref, dp_ref, acc_ref, accb_ref, w_v, sem):
        @pl.when(pl.program_id(0) == 0)
        def _():
            _load_once(w_hbm, w_v, sem)
            acc_ref[...] = jnp.zeros_like(acc_ref)
            accb_ref[...] = jnp.zeros_like(accb_ref)

        du = jnp.concatenate([du_ref[b] for b in range(N_LANE_BLOCKS)], axis=-1).astype(BF16)
        dpr = dpr_ref[...]
        dp_ref[:, 0:SSM_WIDTH] = du
        dp_ref[:, SSM_WIDTH:] = dpr
        accb_ref[0:1, 0:SSM_WIDTH] += _colsum(du.astype(F32))
        accb_ref[0:1, SSM_WIDTH:] += _colsum(dpr.astype(F32))
        dh = _dot_nt(du, w_v[:, 0:SSM_WIDTH]) + _dot_nt(dpr, w_v[:, SSM_WIDTH:])
        xn, rstd = _ln_stats(x_ref[...])
        acc_ref[0:1, :] += _colsum(dh * xn)
        acc_ref[1:2, :] += _colsum(dh)
        gx_ref[...] = _ln_bwd(dh * (1.0 + modv_ref[1:2, :]), xn, rstd) + ALPHA * dr1_ref[...]

    tok = lambda w: pl.BlockSpec((tb, w), lambda i: (i, 0))
    return pl.pallas_call(
        body, name="b3_inproj", grid=(t // tb,),
        in_specs=[tok(D_MODEL), tok(D_MODEL), pl.BlockSpec((N_LANE_BLOCKS, tb, LANES), lambda i: (0, i, 0)), tok(pw),
                  _full(modv), ANY],
        out_specs=[tok(D_MODEL), tok(IN_PROJ_WIDTH), pl.BlockSpec((8, D_MODEL), lambda i: (0, 0)),
                   pl.BlockSpec((8, IN_PROJ_WIDTH), lambda i: (0, 0))],
        out_shape=[jax.ShapeDtypeStruct((t, D_MODEL), F32), jax.ShapeDtypeStruct((t, IN_PROJ_WIDTH), BF16),
                   jax.ShapeDtypeStruct((8, D_MODEL), F32), jax.ShapeDtypeStruct((8, IN_PROJ_WIDTH), F32)],
        scratch_shapes=[pltpu.VMEM(w_in.shape, BF16), pltpu.SemaphoreType.DMA],
        compiler_params=_params(("arbitrary",)),
    )(x, dr1, du4, dprest, modv, w_in)


TN_ROWS = 2048


def _tn_matmul(name, a, b, tm, tn):
    t, m = a.shape
    n = b.shape[1]
    tt = min(TN_ROWS, t)
    nk = t // tt

    def body(a_ref, b_ref, o_ref, acc):
        k = pl.program_id(2)

        @pl.when(k == 0)
        def _():
            acc[...] = jnp.zeros_like(acc)

        acc[...] += _dot_tn(a_ref[...], b_ref[...])

        @pl.when(k == nk - 1)
        def _():
            o_ref[...] = acc[...].astype(BF16)

    return pl.pallas_call(
        body, name=name, grid=(m // tm, n // tn, nk),
        in_specs=[pl.BlockSpec((tt, tm), lambda i, j, k: (k, i)), pl.BlockSpec((tt, tn), lambda i, j, k: (k, j))],
        out_specs=pl.BlockSpec((tm, tn), lambda i, j, k: (i, j)),
        out_shape=jax.ShapeDtypeStruct((m, n), BF16),
        scratch_shapes=[pltpu.VMEM((tm, tn), F32)],
        compiler_params=_params(("arbitrary", "arbitrary", "arbitrary")),
    )(a, b)


def _local_step(x, tgt, modv, wb, shards, sp, tb=256):
    t = x.shape[0]
    row8 = lambda rows, w: jnp.concatenate([r.reshape(1, w) for r in rows] + [jnp.zeros((8 - len(rows), w), F32)], axis=0)
    lnv = row8([sp["ln1_g"], sp["ln1_b"], sp["ln2_g"], sp["ln2_b"]], D_MODEL)
    cvv = row8([sp["cv_dw_b"], sp["cv_ln_g"], sp["cv_ln_b"]], CONV_WIDTH)
    cw32 = jnp.concatenate([sp["cv_dw_w"].reshape(CONV_KERNEL, CONV_WIDTH), jnp.zeros((1, CONV_WIDTH), F32)], axis=0)
    fdw = row8(list(sp["ffn_dw_w"].reshape(FFN_KERNEL, 2 * FFN_HIDDEN)) + [sp["ffn_dw_b"]], 2 * FFN_HIDDEN)
    b_in = sp["b_in"].reshape(1, IN_PROJ_WIDTH)
    ssm = tuple(sp[k] for k in ("ssm_lambda_re", "ssm_lambda_im", "ssm_log_dt", "ssm_b_re", "ssm_b_im", "ssm_c_re",
                                "ssm_c_im", "ssm_d"))
    (kc, sw_r, sw_i, sg_r, sg_i, a), ssm_vjp = jax.vjp(_s5_build, *ssm)
    a8 = jnp.concatenate([a, jnp.zeros((N_LANE_BLOCKS, 6, STATE_COLS), F32)], axis=1)

    name = lambda m: BIG[m][0]
    *mid_w, u4, prest, h1 = _f1_inproj(x, modv, b_in, wb["w_in"], [shards[m] for m in MID], [wb[name(m)] for m in MID], tb)
    w_a, w_b, w_pw, w_out = mid_w
    hr, hi = _s5a_state(u4, sw_r, sw_i, a8)
    ys4 = _s5b_out(u4, kc, sg_r, sg_i, hr, hi)
    w_up, w_down, r1, ya, yb, ycv, vc, yg, vs, merged = _f4_mixer(
        ys4, prest, x, modv, cvv, cw32, w_a, w_b, w_pw, w_out, [shards[m] for m in LATE], [wb[name(m)] for m in LATE], tb)
    dr2, dconv, up, z, acc5 = _f5_ffn(r1, tgt, modv, lnv, fdw, w_up, w_down, tb)
    dup, dr1, h2, dyf, acc1b, acc1a = _b1b_ffn_up(dconv, up, dr2, r1, modv, lnv, fdw, w_up, tb)
    late_dw = [_tn_matmul("dw_up", h2, dup, 1024, FFN_COLS), _tn_matmul("dw_down", z, dyf, FFN_COLS, 1024)]
    got_up, got_down, dys4, dprest, dya, dyb, dycv, dym, acc2a, acc2b, acc2w = _b2_mixer(
        dr1, ys4, prest, ya, yb, ycv, vc, merged, modv, cvv, cw32, w_a, w_b, w_pw, w_out, late_dw, tb)
    mid_dw = [_tn_matmul("dw_glu_a", yg, dya, 512, 1024), _tn_matmul("dw_glu_b", yg, dyb, 512, 1024),
              _tn_matmul("dw_pw", vs, dycv, 512, 1024), _tn_matmul("dw_out", merged, dym, 1024, 1024)]
    dxr, dxi, da8, dsg_r, dsg_i = _s5c_state_bwd(dys4, sg_r, sg_i, a8, hr, hi)
    *mid_got, du4, dkc, dsw_r, dsw_i = _s5d_input_bwd(dys4, u4, kc, sw_r, sw_i, dxr, dxi, mid_dw)
    dssm = ssm_vjp((dkc, dsw_r, dsw_i, dsg_r, dsg_i, da8[:, 0:2, :]))
    gx, dp, acc3, acc3b = _b3_inproj(x, dr1, du4, dprest, modv, wb["w_in"], tb)
    dbig = [_tn_matmul("dw_in", h1, dp, 1024, 896)] + mid_dw + late_dw
    dmod = jnp.concatenate([acc3[1], acc3[0], acc2a[0], acc1b[1], acc1b[0], acc5[2]])
    small = {
        "dmod": dmod, "b_in": acc3b[0],
        "ssm_lambda_re": dssm[0], "ssm_lambda_im": dssm[1], "ssm_log_dt": dssm[2], "ssm_b_re": dssm[3],
        "ssm_b_im": dssm[4], "ssm_c_re": dssm[5], "ssm_c_im": dssm[6], "ssm_d": dssm[7],
        "cv_dw_w": acc2w[0:CONV_KERNEL], "cv_dw_b": acc2b[0], "cv_ln_g": acc2b[1], "cv_ln_b": acc2b[2],
        "ln1_g": acc1b[2], "ln1_b": acc1b[3], "ffn_dw_w": acc1a[0:FFN_KERNEL], "ffn_dw_b": acc1a[3],
        "ln2_g": acc5[0], "ln2_b": acc5[1], "loss": jnp.sum(acc5[3]).reshape(1),
    }
    return gx, dbig, list(mid_got) + [got_up, got_down], small


def _allgather(name, shard):
    m_per, n = shard.shape

    def body(x_ref, out_ref, send_sems, recv_sems, local_sem):
        x, y, c, chips = _place()
        me, sibling = (x, y, c), (x, y, 1 - c)

        def rows(px, py, pc):
            return out_ref.at[pl.ds((4 * px + 2 * py + pc) * m_per, m_per), :]

        def copy(k, block, to, src=None):
            return pltpu.make_async_remote_copy(
                src_ref=rows(*block) if src is None else src, dst_ref=rows(*block),
                send_sem=send_sems.at[k], recv_sem=recv_sems.at[k], device_id=to, device_id_type=MESH)

        mine = pltpu.make_async_copy(x_ref, rows(*me), local_sem)
        mine.start()
        first = [copy(0, me, sibling, src=x_ref)]
        first += [copy(1 + j, me, (*chip, c), src=x_ref) for j, chip in enumerate(chips)]
        for cp in first:
            cp.start()
        passed = [copy(4 + j, (*chip, c), sibling) for j, chip in enumerate(chips)]
        for j, chip in enumerate(chips):
            copy(1 + j, (*chip, c), me).wait_recv()
            passed[j].start()
        copy(0, sibling, me).wait_recv()
        for j, chip in enumerate(chips):
            copy(4 + j, (*chip, 1 - c), me).wait_recv()
        for cp in first + passed:
            cp.wait_send()
        mine.wait()

    return pl.pallas_call(
        body, name=name,
        out_shape=jax.ShapeDtypeStruct((N_DEV * m_per, n), shard.dtype),
        in_specs=[pl.BlockSpec(memory_space=pltpu.VMEM)],
        out_specs=pl.BlockSpec(memory_space=pltpu.VMEM),
        scratch_shapes=[pltpu.SemaphoreType.DMA((7,)), pltpu.SemaphoreType.DMA((7,)), pltpu.SemaphoreType.DMA],
        compiler_params=_params(),
    )(shard)


def _add_rows(pr):
    return 64 if pr % 64 == 0 else 16


def _gather_weights(shards):
    nm = len(BIG)
    nl = len(DIRECT)

    def body(*refs):
        ins, outs, lsh = refs[:nm], refs[nm:2 * nm], refs[2 * nm:2 * nm + nl]
        stage = refs[2 * nm + nl:3 * nm + nl]
        send, recv, fsend, frecv, lsem = refs[3 * nm + nl:]
        x, y, c, chips = _place()
        s_me = 2 * x + y
        sibling = (x, y, 1 - c)
        pend = []
        for m in range(nm):
            stage[m][...] = ins[m][...].astype(BF16)
        for m, (_, kind, shape) in enumerate(BIG):
            pr, pc = _piece(kind, shape)
            for k in range(2):
                cp = pltpu.make_async_copy(stage[m].at[pl.ds(k * pr, pr), :], _piece_at(outs[m], kind, shape, s_me, k),
                                           lsem.at[m, k])
                cp.start()
                pend.append(cp.wait)
            if m in DIRECT:
                cp = pltpu.make_async_copy(stage[m], lsh[DIRECT.index(m)], lsem.at[m, 2])
                cp.start()
                pend.append(cp.wait)
                continue
            for j, chip in enumerate(chips):
                cp = pltpu.make_async_remote_copy(
                    src_ref=stage[m].at[pl.ds(pl.multiple_of(c * pr, 16), pr), :],
                    dst_ref=_piece_at(outs[m], kind, shape, s_me, c),
                    send_sem=send.at[m, j], recv_sem=recv.at[m, j], device_id=(*chip, c), device_id_type=MESH)
                cp.start()
                pend.append(cp.wait_send)
        for m in EARLY:
            _, kind, shape = BIG[m]
            for j, (cx, cy) in enumerate(chips):
                got = _piece_at(outs[m], kind, shape, 2 * cx + cy, c)
                pltpu.make_async_remote_copy(src_ref=got, dst_ref=got, send_sem=send.at[m, j], recv_sem=recv.at[m, j],
                                             device_id=(cx, cy, c), device_id_type=MESH).wait_recv()
                cp = pltpu.make_async_remote_copy(src_ref=got, dst_ref=got, send_sem=fsend.at[m, j],
                                                  recv_sem=frecv.at[m, j], device_id=sibling, device_id_type=MESH)
                cp.start()
                pend.append(cp.wait_send)
        for m in EARLY:
            _, kind, shape = BIG[m]
            for j, (cx, cy) in enumerate(chips):
                got = _piece_at(outs[m], kind, shape, 2 * cx + cy, 1 - c)
                pltpu.make_async_remote_copy(src_ref=got, dst_ref=got, send_sem=fsend.at[m, j], recv_sem=frecv.at[m, j],
                                             device_id=sibling, device_id_type=MESH).wait_recv()
        for w in pend:
            w()

    sem = lambda *s: pltpu.SemaphoreType.DMA(s)
    res = pl.pallas_call(
        body, name="gather_weights",
        out_shape=[jax.ShapeDtypeStruct(shape, BF16) for _, _, shape in BIG]
        + [jax.ShapeDtypeStruct(shards[m].shape, BF16) for m in DIRECT],
        in_specs=[pl.BlockSpec(memory_space=pltpu.VMEM)] * nm,
        out_specs=[ANY] * (nm + nl),
        scratch_shapes=[pltpu.VMEM(s.shape, BF16) for s in shards] + [sem(nm, 3), sem(nm, 3), sem(nm, 3), sem(nm, 3),
                                                                         sem(nm, 3)],
        compiler_params=_params(),
    )(*shards)
    return res[:nm], dict(zip(DIRECT, res[nm:]))


def _rs1_sibling(grads):
    mats = [BIG[m] for m in EARLY]
    nm = len(mats)

    def body(*refs):
        ins, outs = refs[:nm], refs[nm:2 * nm]
        send, recv = refs[2 * nm:]
        x, y, c, _ = _place()
        cps = []
        for m, (_, kind, shape) in enumerate(mats):
            for s in range(N_CHIP):
                cp = pltpu.make_async_remote_copy(
                    src_ref=_piece_at(ins[m], kind, shape, s, 1 - c), dst_ref=outs[m].at[s],
                    send_sem=send.at[m, s], recv_sem=recv.at[m, s], device_id=(x, y, 1 - c), device_id_type=MESH)
                cp.start()
                cps.append(cp)
        for cp in cps:
            cp.wait()

    sem = lambda *s: pltpu.SemaphoreType.DMA(s)
    return pl.pallas_call(
        body, name="rs1_sibling",
        out_shape=[jax.ShapeDtypeStruct((N_CHIP,) + _piece(kind, shape), BF16) for _, kind, shape in mats],
        in_specs=[ANY] * nm, out_specs=[ANY] * nm,
        scratch_shapes=[sem(nm, N_CHIP), sem(nm, N_CHIP)],
        compiler_params=_params(),
    )(*grads)


def _rs2_chips(grads, halves):
    mats = [BIG[m] for m in EARLY]
    nm = len(mats)

    def body(*refs):
        gin, hin = refs[:nm], refs[nm:2 * nm]
        own, got = refs[2 * nm:3 * nm], refs[3 * nm:4 * nm]
        send, recv, lsem = refs[4 * nm:]
        x, y, c, chips = _place()
        s_me = 2 * x + y
        for m, (_, kind, shape) in enumerate(mats):
            pr, pc = _piece(kind, shape)

            def scoped(a, b, m=m, kind=kind, shape=shape, pr=pr):
                loads = [pltpu.make_async_copy(_piece_at(gin[m], kind, shape, s, c), a.at[s], lsem.at[s])
                         for s in range(N_CHIP)]
                loads.append(pltpu.make_async_copy(hin[m], b, lsem.at[N_CHIP]))
                for cp in loads:
                    cp.start()
                for cp in loads:
                    cp.wait()
                step = _add_rows(pr)
                for s in range(N_CHIP):
                    def add(i, _, s=s):
                        r = pl.ds(pl.multiple_of(i * step, 16), step)
                        a[s, r, :] = (a[s, r, :].astype(F32) + b[s, r, :].astype(F32)).astype(BF16)
                        return 0

                    lax.fori_loop(0, pr // step, add, 0)
                waits = []
                for j, (cx, cy) in enumerate(chips):
                    cp = pltpu.make_async_remote_copy(src_ref=a.at[2 * cx + cy], dst_ref=got[m].at[j], send_sem=send.at[m, j],
                                                      recv_sem=recv.at[m, j], device_id=(cx, cy, c), device_id_type=MESH)
                    cp.start()
                    waits.append(cp.wait_send)
                cp = pltpu.make_async_copy(a.at[s_me], own[m], lsem.at[N_CHIP + 1])
                cp.start()
                waits.append(cp.wait)
                for w in waits:
                    w()

            pl.run_scoped(scoped, pltpu.VMEM((N_CHIP, pr, pc), BF16), pltpu.VMEM((N_CHIP, pr, pc), BF16))
        for m in range(nm):
            for j, (cx, cy) in enumerate(chips):
                pltpu.make_async_remote_copy(src_ref=got[m].at[j], dst_ref=got[m].at[j], send_sem=send.at[m, j],
                                             recv_sem=recv.at[m, j], device_id=(cx, cy, c), device_id_type=MESH).wait_recv()

    sem = lambda *s: pltpu.SemaphoreType.DMA(s)
    pieces = [_piece(kind, shape) for _, kind, shape in mats]
    return pl.pallas_call(
        body, name="rs2_chips",
        out_shape=[jax.ShapeDtypeStruct(p, BF16) for p in pieces] + [jax.ShapeDtypeStruct((3,) + p, BF16) for p in pieces],
        in_specs=[ANY] * (2 * nm), out_specs=[ANY] * (2 * nm),
        scratch_shapes=[sem(nm, 3), sem(nm, 3), sem(N_CHIP + 2)],
        compiler_params=_params(),
    )(*grads, *halves)


def _rs3_finish(own, got):
    nm = len(BIG)

    def body(*refs):
        oin, gin = refs[:nm], refs[nm:2 * nm]
        outs = refs[2 * nm:3 * nm]
        send, recv, lsem = refs[3 * nm:]
        x, y, c, _ = _place()
        for m, (_, kind, shape) in enumerate(BIG):
            pr, pc = _piece(kind, shape)
            ng = got[m].shape[0]

            def scoped(a, g, f, m=m, pr=pr, ng=ng, kind=kind, shape=shape):
                mine = _piece_at(oin[m], kind, shape, 2 * x + y, c) if m in DIRECT else oin[m]
                loads = [pltpu.make_async_copy(mine, a, lsem.at[0]), pltpu.make_async_copy(gin[m], g, lsem.at[1])]
                for cp in loads:
                    cp.start()
                for cp in loads:
                    cp.wait()
                step = _add_rows(pr)

                def add(i, _):
                    r = pl.ds(pl.multiple_of(i * step, 16), step)
                    acc = a[r, :].astype(F32)
                    for q in range(ng):
                        acc = acc + g[q, r, :].astype(F32)
                    f[r, :] = acc
                    return 0

                lax.fori_loop(0, pr // step, add, 0)
                dst = outs[m].at[pl.ds(pl.multiple_of(c * pr, 8), pr), :]
                local = pltpu.make_async_copy(f, dst, lsem.at[2])
                local.start()
                cp = pltpu.make_async_remote_copy(src_ref=f, dst_ref=dst, send_sem=send.at[m], recv_sem=recv.at[m],
                                                  device_id=(x, y, 1 - c), device_id_type=MESH)
                cp.start()
                cp.wait_send()
                local.wait()

            pl.run_scoped(scoped, pltpu.VMEM((pr, pc), BF16), pltpu.VMEM((ng, pr, pc), BF16), pltpu.VMEM((pr, pc), F32))
        for m, (_, kind, shape) in enumerate(BIG):
            pr, pc = _piece(kind, shape)
            dst = outs[m].at[pl.ds(pl.multiple_of((1 - c) * pr, 8), pr), :]
            pltpu.make_async_remote_copy(src_ref=dst, dst_ref=dst, send_sem=send.at[m], recv_sem=recv.at[m],
                                         device_id=(x, y, 1 - c), device_id_type=MESH).wait_recv()

    sem = lambda *s: pltpu.SemaphoreType.DMA(s)
    pieces = [_piece(kind, shape) for _, kind, shape in BIG]
    return pl.pallas_call(
        body, name="rs3_finish",
        out_shape=[jax.ShapeDtypeStruct((2 * pr, pc), F32) for pr, pc in pieces],
        in_specs=[ANY] * (2 * nm), out_specs=[ANY] * nm,
        scratch_shapes=[sem(nm), sem(nm), sem(3)],
        compiler_params=_params(),
    )(*own, *got)


def _cond_fwd(c_all, w_shard, b_shard):
    def body(c_ref, w_ref, b_ref, act_ref, mod_ref):
        cv = c_ref[...]
        act = cv * _sig(cv)
        act_ref[...] = act
        mod_ref[...] = _dot(act.astype(BF16), w_ref[...].astype(BF16)) + b_ref[...]

    return pl.pallas_call(
        body, name="cond_fwd",
        out_shape=[jax.ShapeDtypeStruct(c_all.shape, F32), jax.ShapeDtypeStruct((c_all.shape[0], w_shard.shape[1]), F32)],
        compiler_params=_params(),
    )(c_all, w_shard, b_shard)


def _cond_bwd(act_t, dmod_shard):
    k, n = act_t.shape[0], dmod_shard.shape[1]

    def body(a_ref, d_ref, o_ref):
        acc = a_ref[:, 0:1] * d_ref[0:1, :]
        for e in range(1, N_DEV):
            acc += a_ref[:, e:e + 1] * d_ref[e:e + 1, :]
        o_ref[...] = acc

    tr = 256
    return pl.pallas_call(
        body, name="cond_bwd", grid=(k // tr,),
        in_specs=[pl.BlockSpec((tr, N_DEV), lambda i: (i, 0)), _full(dmod_shard)],
        out_specs=pl.BlockSpec((tr, n), lambda i: (i, 0)),
        out_shape=jax.ShapeDtypeStruct((k, n), F32),
        compiler_params=_params(("arbitrary",)),
    )(act_t, dmod_shard)


RAW_ROWS = 8


def _allreduce_small(pack):
    rows = PACK_ROWS // N_DEV

    def body(x_ref, sum_ref, raw_ref, buf, s1, r1, s2, r2, s3, r3):
        x, y, c, _ = _place()
        me = 4 * x + 2 * y + c
        peers = []
        for r in range(1, N_DEV):
            tx, ty, tc = (1 - x if r & 4 else x), (1 - y if r & 2 else y), (1 - c if r & 1 else c)
            peers.append(((tx, ty, tc), 4 * tx + 2 * ty + tc))
        chunk = lambda ref, d: ref.at[pl.ds(pl.multiple_of(d * rows, 8), rows), :]
        mine_raw = raw_ref.at[pl.ds(pl.multiple_of(me * RAW_ROWS, 8), RAW_ROWS), :]
        first = []
        for q, (dev, pd) in enumerate(peers):
            first.append(pltpu.make_async_remote_copy(src_ref=chunk(x_ref, pd), dst_ref=buf.at[q], send_sem=s1.at[q],
                                                      recv_sem=r1.at[q], device_id=dev, device_id_type=MESH))
            first.append(pltpu.make_async_remote_copy(src_ref=x_ref.at[0:RAW_ROWS, :], dst_ref=mine_raw, send_sem=s3.at[q],
                                                      recv_sem=r3.at[q], device_id=dev, device_id_type=MESH))
        for cp in first:
            cp.start()
        raw_ref[pl.ds(pl.multiple_of(me * RAW_ROWS, 8), RAW_ROWS), :] = x_ref[0:RAW_ROWS, :]
        for q, (dev, pd) in enumerate(peers):
            first[2 * q].wait()
        acc = x_ref[pl.ds(pl.multiple_of(me * rows, 8), rows), :]
        for q in range(N_DEV - 1):
            acc = acc + buf[q]
        sum_ref[pl.ds(pl.multiple_of(me * rows, 8), rows), :] = acc
        second = [pltpu.make_async_remote_copy(src_ref=chunk(sum_ref, me), dst_ref=chunk(sum_ref, me), send_sem=s2.at[q],
                                               recv_sem=r2.at[q], device_id=dev, device_id_type=MESH)
                  for q, (dev, pd) in enumerate(peers)]
        for cp in second:
            cp.start()
        for q, (dev, pd) in enumerate(peers):
            pltpu.make_async_remote_copy(src_ref=chunk(sum_ref, pd), dst_ref=chunk(sum_ref, pd), send_sem=s2.at[q],
                                         recv_sem=r2.at[q], device_id=dev, device_id_type=MESH).wait()
            pltpu.make_async_remote_copy(src_ref=x_ref.at[0:RAW_ROWS, :],
                                         dst_ref=raw_ref.at[pl.ds(pl.multiple_of(pd * RAW_ROWS, 8), RAW_ROWS), :],
                                         send_sem=s3.at[q], recv_sem=r3.at[q], device_id=dev, device_id_type=MESH).wait()

    sem = pltpu.SemaphoreType.DMA((N_DEV - 1,))
    return pl.pallas_call(
        body, name="allreduce_small",
        out_shape=[jax.ShapeDtypeStruct((PACK_ROWS, PACK_COLS), F32), jax.ShapeDtypeStruct((N_DEV * RAW_ROWS, PACK_COLS), F32)],
        in_specs=[pl.BlockSpec(memory_space=pltpu.VMEM)],
        out_specs=[pl.BlockSpec(memory_space=pltpu.VMEM)] * 2,
        scratch_shapes=[pltpu.VMEM((N_DEV - 1, rows, PACK_COLS), F32), sem, sem, sem, sem, sem, sem],
        compiler_params=_params(),
    )(pack)


def _adamw(name, w, g, m, v):
    r, cc = w.shape
    tr = r
    for cand in (256, 128, 64, 32, 16, 8):
        if r % cand == 0:
            tr = cand
            break
    bc1 = 1.0 - ADAM_B1 ** ADAM_STEP
    bc2 = 1.0 - ADAM_B2 ** ADAM_STEP

    def body(w_ref, g_ref, m_ref, v_ref, d_ref, nm_ref, nv_ref):
        gv = g_ref[...]
        nm = ADAM_B1 * m_ref[...] + (1.0 - ADAM_B1) * gv
        nv = ADAM_B2 * v_ref[...] + (1.0 - ADAM_B2) * (gv * gv)
        nm_ref[...] = nm
        nv_ref[...] = nv
        d_ref[...] = -ADAM_LR * ((nm / bc1) / (jnp.sqrt(nv / bc2) + ADAM_EPS) + ADAM_WD * w_ref[...])

    spec = pl.BlockSpec((tr, cc), lambda i: (i, 0))
    return pl.pallas_call(
        body, name=name, grid=(r // tr,), in_specs=[spec] * 4, out_specs=[spec] * 3,
        out_shape=[jax.ShapeDtypeStruct((r, cc), F32)] * 3, compiler_params=_params(("arbitrary",)),
    )(w, g, m, v)


def _adamw_small(ws, gs, ms, vs):
    n = len(ws)
    bc1 = 1.0 - ADAM_B1 ** ADAM_STEP
    bc2 = 1.0 - ADAM_B2 ** ADAM_STEP

    def body(*refs):
        w, g, m, v = (refs[k * n:(k + 1) * n] for k in range(4))
        d, nm, nv = (refs[(4 + k) * n:(5 + k) * n] for k in range(3))
        for i in range(n):
            gv = g[i][...]
            m1 = ADAM_B1 * m[i][...] + (1.0 - ADAM_B1) * gv
            v1 = ADAM_B2 * v[i][...] + (1.0 - ADAM_B2) * (gv * gv)
            nm[i][...] = m1
            nv[i][...] = v1
            d[i][...] = -ADAM_LR * ((m1 / bc1) / (jnp.sqrt(v1 / bc2) + ADAM_EPS) + ADAM_WD * w[i][...])

    shapes = [jax.ShapeDtypeStruct(x.shape, F32) for x in ws]
    res = pl.pallas_call(body, name="adamw_small", out_shape=shapes * 3, compiler_params=_params())(*ws, *gs, *ms, *vs)
    return res[:n], res[n:2 * n], res[2 * n:]


def _pack(fields, layout):
    parts = [fields[name].reshape(-1).astype(F32) if name in fields else jnp.zeros((n,), F32) for name, n in layout]
    used = sum(n for _, n in layout)
    parts.append(jnp.zeros((PACK_ROWS * PACK_COLS - used,), F32))
    return jnp.concatenate(parts).reshape(PACK_ROWS, PACK_COLS)


def _unpack(flat, layout):
    flat = flat.reshape(-1)
    out, o = {}, 0
    for name, n in layout:
        out[name] = flat[o:o + n]
        o += n
    return out


def kernel(x, c, w_cond, b_cond, w_in, b_in, ssm_lambda_re, ssm_lambda_im, ssm_log_dt, ssm_b_re, ssm_b_im, ssm_c_re, ssm_c_im, ssm_d, ssm_glu_w_a, ssm_glu_w_b, cv_dw_w, cv_dw_b, cv_ln_g, cv_ln_b, cv_w_pw, w_out, ln1_g, ln1_b, ffn_w_up, ffn_dw_w, ffn_dw_b, ffn_w_down, ln2_g, ln2_b, loss_target, m_w_cond, m_b_cond, m_w_in, m_b_in, m_ssm_lambda_re, m_ssm_lambda_im, m_ssm_log_dt, m_ssm_b_re, m_ssm_b_im, m_ssm_c_re, m_ssm_c_im, m_ssm_d, m_ssm_glu_w_a, m_ssm_glu_w_b, m_cv_dw_w, m_cv_dw_b, m_cv_ln_g, m_cv_ln_b, m_cv_w_pw, m_w_out, m_ln1_g, m_ln1_b, m_ffn_w_up, m_ffn_dw_w, m_ffn_dw_b, m_ffn_w_down, m_ln2_g, m_ln2_b, v_w_cond, v_b_cond, v_w_in, v_b_in, v_ssm_lambda_re, v_ssm_lambda_im, v_ssm_log_dt, v_ssm_b_re, v_ssm_b_im, v_ssm_c_re, v_ssm_c_im, v_ssm_d, v_ssm_glu_w_a, v_ssm_glu_w_b, v_cv_dw_w, v_cv_dw_b, v_cv_ln_g, v_cv_ln_b, v_cv_w_pw, v_w_out, v_ln1_g, v_ln1_b, v_ffn_w_up, v_ffn_dw_w, v_ffn_dw_b, v_ffn_w_down, v_ln2_g, v_ln2_b):
    given = locals()
    a = {n: given[n] for n in INPUTS}
    xi, yi, ci = lax.axis_index("x"), lax.axis_index("y"), lax.axis_index("c")
    s_me = 2 * xi + yi
    e_me = 4 * xi + 2 * yi + ci

    first = jnp.concatenate([
        jnp.concatenate([a["c"], jnp.zeros((7, D_MODEL), F32)], axis=0),
        jnp.concatenate([a["cv_dw_w"].reshape(-1), a["ffn_dw_w"].reshape(-1)]).reshape(8, D_MODEL)], axis=0)
    first_all = _allgather("gather_c", first).reshape(N_DEV, 16, D_MODEL)
    c_all = first_all[:, 0, :]
    dw_all = first_all[0::2, 8:, :].reshape(N_CHIP, 8 * D_MODEL)
    n_cv = CONV_KERNEL * CONV_WIDTH // N_CHIP
    cv_dw_full = dw_all[:, :n_cv].reshape(N_CHIP, CONV_KERNEL, CONV_WIDTH // N_CHIP).transpose(1, 0, 2) \
        .reshape(CONV_KERNEL, CONV_WIDTH)
    ffn_dw_full = dw_all[:, n_cv:].reshape(N_CHIP, FFN_KERNEL, 2 * FFN_HIDDEN // N_CHIP).transpose(1, 0, 2) \
        .reshape(FFN_KERNEL, 2 * FFN_HIDDEN)
    ncols = N_COND * D_MODEL // N_CHIP
    b_cond_shard = lax.dynamic_slice(a["b_cond"], (0, s_me * ncols), (1, ncols))
    c_act_all, modp = _cond_fwd(c_all, a["w_cond"][0], b_cond_shard)
    modp_all = _allgather("gather_mod", modp).reshape(N_DEV, N_DEV, ncols)[0::2]
    mod_e = lax.dynamic_index_in_dim(modp_all, e_me, axis=1, keepdims=False).reshape(N_COND, D_MODEL)
    modv = jnp.concatenate([mod_e, jnp.zeros((2, D_MODEL), F32)], axis=0)

    full, shards = _gather_weights([a[n][0] for n, _, _ in BIG])
    wb = dict(zip([n for n, _, _ in BIG], full))
    sp = {n: a[n][0] for n in ("b_in", "ssm_lambda_re", "ssm_lambda_im", "ssm_log_dt", "ssm_b_re", "ssm_b_im",
                               "ssm_c_re", "ssm_c_im", "ssm_d", "cv_dw_b", "cv_ln_g", "cv_ln_b", "ln1_g", "ln1_b",
                               "ffn_dw_b", "ln2_g", "ln2_b")}
    sp["cv_dw_w"] = cv_dw_full
    sp["ffn_dw_w"] = ffn_dw_full
    gx, dbig, direct_got, small = _local_step(a["x"][0], a["loss_target"][0], modv, wb, shards, sp)

    tot_pack, raw_all = _allreduce_small(_pack(small, PACK))
    tot = _unpack(tot_pack, PACK)
    dmod_all = raw_all.reshape(N_DEV, RAW_ROWS * PACK_COLS)[:, 0:N_COND * D_MODEL]
    g_w_cond = _cond_bwd(c_act_all.T, lax.dynamic_slice(dmod_all, (0, s_me * ncols), (N_DEV, ncols)))

    glist = [dbig[m] for m in EARLY]
    halves = _rs1_sibling(glist)
    r2 = _rs2_chips(glist, halves)
    gsh = _rs3_finish(list(r2[:len(EARLY)]) + [dbig[m] for m in DIRECT], list(r2[len(EARLY):]) + direct_got)

    grads = {"w_cond": g_w_cond[None], "b_cond": tot["dmod"].reshape(1, -1)}
    for (n, kind, shape), g in zip(BIG, gsh):
        grads[n] = g.reshape(a[n].shape)
    for n in ("b_in", "ssm_lambda_re", "ssm_lambda_im", "ssm_log_dt", "ssm_b_re", "ssm_b_im", "ssm_c_re", "ssm_c_im",
              "ssm_d", "cv_dw_b", "cv_ln_g", "cv_ln_b", "ln1_g", "ln1_b", "ffn_dw_b", "ln2_g", "ln2_b"):
        grads[n] = tot[n].reshape(a[n].shape)
    wcv = CONV_WIDTH // N_CHIP
    grads["cv_dw_w"] = lax.dynamic_slice(tot["cv_dw_w"].reshape(CONV_KERNEL, CONV_WIDTH), (0, s_me * wcv),
                                         (CONV_KERNEL, wcv)).reshape(a["cv_dw_w"].shape)
    wff = 2 * FFN_HIDDEN // N_CHIP
    grads["ffn_dw_w"] = lax.dynamic_slice(tot["ffn_dw_w"].reshape(FFN_KERNEL, 2 * FFN_HIDDEN), (0, s_me * wff),
                                          (FFN_KERNEL, wff)).reshape(a["ffn_dw_w"].shape)

    delta, new_m, new_v = {}, {}, {}
    for n in ["w_cond"] + [n for n, _, _ in BIG]:
        d, nm_, nv_ = _adamw("adamw_" + n, a[n][0], grads[n][0], a["m_" + n][0], a["v_" + n][0])
        delta[n], new_m[n], new_v[n] = d[None], nm_[None], nv_[None]
    upd = [n for n in WEIGHTS if n not in delta]
    two_d = lambda t: t.reshape(-1, t.shape[-1])
    outs = _adamw_small([two_d(a[n]) for n in upd], [two_d(grads[n]) for n in upd],
                        [two_d(a["m_" + n]) for n in upd], [two_d(a["v_" + n]) for n in upd])
    for dst, vals in zip((delta, new_m, new_v), outs):
        for n, val in zip(upd, vals):
            dst[n] = val.reshape(a[n].shape)

    loss = tot["loss"].reshape(())
    return (loss, gx[None], *[grads[n] for n in WEIGHTS], *[delta[n] for n in WEIGHTS],
            *[new_m[n] for n in WEIGHTS], *[new_v[n] for n in WEIGHTS])
```

```python
import functools
import math

import jax
import jax.numpy as jnp
from jax import lax
from jax.experimental import pallas as pl
from jax.experimental.pallas import tpu as pltpu

F32 = jnp.float32
BF16 = jnp.bfloat16

D_MODEL = 1024
SSM_WIDTH = 512
SSM_GROUP = 16
SSM_GROUPS = 32
SSM_STATE = 64
CONV_WIDTH = 512
CONV_KERNEL = 31
FFN_HIDDEN = 2816
FFN_KERNEL = 3
IN_PROJ_WIDTH = 3584
N_COND = 6
ALPHA = 2.0 ** 0.25
LN_EPS = 1e-5
ADAM_LR, ADAM_B1, ADAM_B2, ADAM_EPS, ADAM_WD, ADAM_STEP = 0.001, 0.9, 0.999, 1e-08, 0.01, 10

N_DEV = 8
N_CHIP = 4
LANES = 128
SSM_CHUNK = 16
LANE_GROUPS = LANES // SSM_GROUP
N_LANE_BLOCKS = SSM_WIDTH // LANES
STATE_COLS = LANE_GROUPS * SSM_STATE
CHUNK_COLS = SSM_CHUNK * LANES
CONV_HALO = 32
VMEM_LIMIT = 56 * 1024 * 1024
MESH = pl.DeviceIdType.MESH

BIG = (
    ("w_in", "col", (D_MODEL, IN_PROJ_WIDTH)),
    ("ssm_glu_w_a", "col", (SSM_WIDTH, D_MODEL)),
    ("ssm_glu_w_b", "col", (SSM_WIDTH, D_MODEL)),
    ("cv_w_pw", "col", (CONV_WIDTH, D_MODEL)),
    ("w_out", "row", (D_MODEL, D_MODEL)),
    ("ffn_w_up", "col", (D_MODEL, 2 * FFN_HIDDEN)),
    ("ffn_w_down", "row", (FFN_HIDDEN, D_MODEL)),
)

EARLY = (0,)
MID = (1, 2, 3, 4)
LATE = (5, 6)
DIRECT = MID + LATE

WEIGHTS = ['w_cond', 'b_cond', 'w_in', 'b_in', 'ssm_lambda_re', 'ssm_lambda_im', 'ssm_log_dt', 'ssm_b_re', 'ssm_b_im',
           'ssm_c_re', 'ssm_c_im', 'ssm_d', 'ssm_glu_w_a', 'ssm_glu_w_b', 'cv_dw_w', 'cv_dw_b', 'cv_ln_g', 'cv_ln_b',
           'cv_w_pw', 'w_out', 'ln1_g', 'ln1_b', 'ffn_w_up', 'ffn_dw_w', 'ffn_dw_b', 'ffn_w_down', 'ln2_g', 'ln2_b']
INPUTS = ['x', 'c'] + WEIGHTS + ['loss_target'] + ['m_' + n for n in WEIGHTS] + ['v_' + n for n in WEIGHTS]

PACK = (
    ("dmod", N_COND * D_MODEL), ("c_act", D_MODEL), ("b_in", IN_PROJ_WIDTH),
    ("ssm_lambda_re", SSM_GROUPS * SSM_STATE), ("ssm_lambda_im", SSM_GROUPS * SSM_STATE), ("ssm_log_dt", SSM_GROUPS),
    ("ssm_b_re", SSM_GROUPS * SSM_STATE * SSM_GROUP), ("ssm_b_im", SSM_GROUPS * SSM_STATE * SSM_GROUP),
    ("ssm_c_re", SSM_GROUPS * SSM_STATE * SSM_GROUP), ("ssm_c_im", SSM_GROUPS * SSM_STATE * SSM_GROUP),
    ("ssm_d", SSM_GROUPS * SSM_GROUP), ("cv_dw_w", CONV_KERNEL * CONV_WIDTH), ("cv_dw_b", CONV_WIDTH),
    ("cv_ln_g", CONV_WIDTH), ("cv_ln_b", CONV_WIDTH), ("ln1_g", D_MODEL), ("ln1_b", D_MODEL),
    ("ffn_dw_w", FFN_KERNEL * 2 * FFN_HIDDEN), ("ffn_dw_b", 2 * FFN_HIDDEN), ("ln2_g", D_MODEL), ("ln2_b", D_MODEL),
    ("loss", 1),
)
PACK_COLS = 1024
PACK_ROWS = 192
assert sum(n for _, n in PACK) <= PACK_ROWS * PACK_COLS


def _params(sem=None, **kw):
    return pltpu.CompilerParams(dimension_semantics=sem, vmem_limit_bytes=VMEM_LIMIT, **kw)


def _ln_stats(x):
    mu = jnp.mean(x, axis=-1, keepdims=True)
    xc = x - mu
    var = jnp.mean(xc * xc, axis=-1, keepdims=True)
    rstd = lax.rsqrt(var + LN_EPS)
    return xc * rstd, rstd


def _ln_bwd(dxhat, xhat, rstd):
    m1 = jnp.mean(dxhat, axis=-1, keepdims=True)
    m2 = jnp.mean(dxhat * xhat, axis=-1, keepdims=True)
    return rstd * (dxhat - m1 - xhat * m2)


def _sig(x):
    return 1.0 / (1.0 + jnp.exp(-x))


def _gelu(x):
    return 0.5 * x * (1.0 + lax.erf(x * (1.0 / math.sqrt(2.0))))


def _dgelu(x):
    return 0.5 * (1.0 + lax.erf(x * (1.0 / math.sqrt(2.0)))) + x * jnp.exp(-0.5 * x * x) * (1.0 / math.sqrt(2.0 * math.pi))


def _gelu_and_grad(x):
    er = lax.erf(x * (1.0 / math.sqrt(2.0)))
    cdf = 0.5 * (1.0 + er)
    return x * cdf, cdf + x * jnp.exp(-0.5 * x * x) * (1.0 / math.sqrt(2.0 * math.pi))


def _colsum(a):
    return jnp.sum(a, axis=0, keepdims=True)


def _fill_rotations(buf, rot, rows):
    for r in range(1, 8):
        rot[r - 1] = buf[pl.ds(r, rows), :]


def _rows_at(buf, rot, offset, tb):
    q, r = divmod(offset, 8)
    if r == 0:
        return buf[pl.ds(8 * q, tb), :]
    return rot[r - 1, pl.ds(8 * q, tb), :]


def _dot(a, b):
    return jnp.dot(a, b, preferred_element_type=F32)


def _dot_nt(a, b):
    return lax.dot_general(a, b, (((1,), (1,)), ((), ())), preferred_element_type=F32)


def _dot_tn(a, b):
    return lax.dot_general(a, b, (((0,), (0,)), ((), ())), preferred_element_type=F32)


def _load_once(src, dst, sem):
    cp = pltpu.make_async_copy(src, dst, sem)
    cp.start()
    cp.wait()


def _full(a):
    nd = a.ndim
    return pl.BlockSpec(a.shape, lambda *_: (0,) * nd)


ANY = pl.BlockSpec(memory_space=pl.ANY)


def _place():
    x, y, c = lax.axis_index("x"), lax.axis_index("y"), lax.axis_index("c")
    chips = [(1 - x, y), (x, 1 - y), (1 - x, 1 - y)]
    return x, y, c, chips


def _piece(kind, shape):
    r, cc = shape
    return (r // 2, cc // N_CHIP) if kind == "col" else (r // (2 * N_CHIP), cc)


def _piece_at(ref, kind, shape, s, k):
    pr, pc = _piece(kind, shape)
    if kind == "col":
        return ref.at[pl.ds(k * pr, pr), pl.ds(pl.multiple_of(s * pc, LANES), pc)]
    return ref.at[pl.ds(pl.multiple_of((2 * s + k) * pr, 16), pr), :]


def _gather_start(idx, sh, full, send, recv):
    x, y, c, chips = _place()
    for i, m in enumerate(idx):
        _, kind, shape = BIG[m]
        pr, _ = _piece(kind, shape)
        for j, chip in enumerate(chips):
            pltpu.make_async_remote_copy(
                src_ref=sh[i].at[pl.ds(pl.multiple_of(c * pr, 16), pr), :], dst_ref=_piece_at(full[i], kind, shape, 2 * x + y, c),
                send_sem=send.at[i, j], recv_sem=recv.at[i, j], device_id=(*chip, c), device_id_type=MESH).start()


def _gather_finish(idx, sh, full, send, recv, fsend, frecv):
    x, y, c, chips = _place()
    sibling = (x, y, 1 - c)
    waits = []
    for i, m in enumerate(idx):
        _, kind, shape = BIG[m]
        pr, _ = _piece(kind, shape)
        for j, (cx, cy) in enumerate(chips):
            got = _piece_at(full[i], kind, shape, 2 * cx + cy, c)
            first = pltpu.make_async_remote_copy(
                src_ref=sh[i].at[pl.ds(pl.multiple_of(c * pr, 16), pr), :], dst_ref=got, send_sem=send.at[i, j],
                recv_sem=recv.at[i, j], device_id=(cx, cy, c), device_id_type=MESH)
            first.wait_recv()
            fwd = pltpu.make_async_remote_copy(src_ref=got, dst_ref=got, send_sem=fsend.at[i, j], recv_sem=frecv.at[i, j],
                                               device_id=sibling, device_id_type=MESH)
            fwd.start()
            waits += [first.wait_send, fwd.wait_send]
    for i, m in enumerate(idx):
        _, kind, shape = BIG[m]
        for j, (cx, cy) in enumerate(chips):
            got = _piece_at(full[i], kind, shape, 2 * cx + cy, 1 - c)
            pltpu.make_async_remote_copy(src_ref=got, dst_ref=got, send_sem=fsend.at[i, j], recv_sem=frecv.at[i, j],
                                         device_id=sibling, device_id_type=MESH).wait_recv()
    for w in waits:
        w()


def _scatter(idx, dw, got, send, recv):
    x, y, c, _ = _place()
    cps = []
    for i, m in enumerate(idx):
        _, kind, shape = BIG[m]
        for r in range(1, N_DEV):
            tx, ty, tc = (1 - x if r & 4 else x), (1 - y if r & 2 else y), (1 - c if r & 1 else c)
            cps.append(pltpu.make_async_remote_copy(
                src_ref=_piece_at(dw[i], kind, shape, 2 * tx + ty, tc), dst_ref=got[i].at[r - 1],
                send_sem=send.at[i, r - 1], recv_sem=recv.at[i, r - 1], device_id=(tx, ty, tc), device_id_type=MESH))
    return cps


def _f1_inproj(x, modv, b_in, w_in, mid_sh, mid_full, tb):
    t = x.shape[0]
    nt = t // tb
    nl = len(MID)
    chunks = [(j * 512, 512) for j in range(IN_PROJ_WIDTH // 512)]

    def body(x_ref, modv_ref, b_ref, w_hbm, *rest):
        sh, full = rest[:nl], rest[2 * nl:3 * nl]
        u4_ref, prest_ref, h_ref, w_v, sem, send, recv, fsend, frecv = rest[3 * nl:]

        @pl.when(pl.program_id(0) == 0)
        def _():
            _gather_start(MID, sh, full, send, recv)
            _load_once(w_hbm, w_v, sem)

        xn, _ = _ln_stats(x_ref[...])
        h = (xn * (1.0 + modv_ref[1:2, :]) + modv_ref[0:1, :]).astype(BF16)
        h_ref[...] = h
        for c0, cw in chunks:
            p = _dot(h, w_v[:, c0:c0 + cw]) + b_ref[:, c0:c0 + cw]
            if c0 == 0:
                for b in range(N_LANE_BLOCKS):
                    u4_ref[b] = p[:, b * LANES:(b + 1) * LANES]
            else:
                prest_ref[:, c0 - SSM_WIDTH:c0 - SSM_WIDTH + cw] = p

        @pl.when(pl.program_id(0) == nt - 1)
        def _():
            _gather_finish(MID, sh, full, send, recv, fsend, frecv)

    gsem = pltpu.SemaphoreType.DMA((nl, 3))
    return pl.pallas_call(
        body, name="f1_inproj", grid=(nt,),
        in_specs=[pl.BlockSpec((tb, D_MODEL), lambda i: (i, 0)), _full(modv), _full(b_in), ANY] + [ANY] * (2 * nl),
        out_specs=[ANY] * nl + [pl.BlockSpec((N_LANE_BLOCKS, tb, LANES), lambda i: (0, i, 0)),
                                pl.BlockSpec((tb, IN_PROJ_WIDTH - SSM_WIDTH), lambda i: (i, 0)),
                                pl.BlockSpec((tb, D_MODEL), lambda i: (i, 0))],
        input_output_aliases={4 + nl + k: k for k in range(nl)},
        out_shape=[jax.ShapeDtypeStruct(f.shape, f.dtype) for f in mid_full]
        + [jax.ShapeDtypeStruct((N_LANE_BLOCKS, t, LANES), F32),
                   jax.ShapeDtypeStruct((t, IN_PROJ_WIDTH - SSM_WIDTH), F32),
                   jax.ShapeDtypeStruct((t, D_MODEL), BF16)],
        scratch_shapes=[pltpu.VMEM(w_in.shape, BF16), pltpu.SemaphoreType.DMA, gsem, gsem, gsem, gsem],
        compiler_params=_params(("arbitrary",)),
    )(x, modv, b_in, w_in, *mid_sh, *mid_full)


TAP_GROUPS = 8


def _dot_f32(a, b, dims):
    return lax.dot_general(a, b, (dims, ((), ())), precision=lax.Precision.HIGHEST, preferred_element_type=F32)


def _taps_fwd(car, cai, bt_r, bt_i):
    el, g, p, n = SSM_CHUNK, SSM_GROUPS, SSM_GROUP, SSM_STATE

    def body(ar_ref, ai_ref, br_ref, bi_ref, o_ref):
        for gl in range(TAP_GROUPS):
            a_r = jnp.concatenate([ar_ref[k, gl] for k in range(el)], axis=0)
            a_i = jnp.concatenate([ai_ref[k, gl] for k in range(el)], axis=0)
            o_ref[gl] = _dot_f32(br_ref[gl], a_r, ((1,), (1,))) - _dot_f32(bi_ref[gl], a_i, ((1,), (1,)))

    ablk = pl.BlockSpec((el + 1, TAP_GROUPS, p, n), lambda i: (0, i, 0, 0))
    bblk = pl.BlockSpec((TAP_GROUPS, p, n), lambda i: (i, 0, 0))
    return pl.pallas_call(
        body, name="s5_taps", grid=(g // TAP_GROUPS,), in_specs=[ablk, ablk, bblk, bblk],
        out_specs=pl.BlockSpec((TAP_GROUPS, p, el * p), lambda i: (i, 0, 0)),
        out_shape=jax.ShapeDtypeStruct((g, p, el * p), F32), compiler_params=_params(("arbitrary",)),
    )(car, cai, bt_r, bt_i)


def _taps_bwd(dr, car, cai, bt_r, bt_i):
    el, g, p, n = SSM_CHUNK, SSM_GROUPS, SSM_GROUP, SSM_STATE

    def body(dr_ref, ar_ref, ai_ref, br_ref, bi_ref, dar_ref, dai_ref, dbr_ref, dbi_ref):
        for gl in range(TAP_GROUPS):
            dv = dr_ref[gl]
            a_r = jnp.concatenate([ar_ref[k, gl] for k in range(el)], axis=0)
            a_i = jnp.concatenate([ai_ref[k, gl] for k in range(el)], axis=0)
            dbr_ref[gl] = _dot_f32(dv, a_r, ((1,), (0,)))
            dbi_ref[gl] = -_dot_f32(dv, a_i, ((1,), (0,)))
            da_r = _dot_f32(dv, br_ref[gl], ((0,), (0,)))
            da_i = -_dot_f32(dv, bi_ref[gl], ((0,), (0,)))
            for k in range(el):
                dar_ref[k, gl] = da_r[k * p:(k + 1) * p, :]
                dai_ref[k, gl] = da_i[k * p:(k + 1) * p, :]
            dar_ref[el, gl] = jnp.zeros((p, n), F32)
            dai_ref[el, gl] = jnp.zeros((p, n), F32)

    ablk = pl.BlockSpec((el + 1, TAP_GROUPS, p, n), lambda i: (0, i, 0, 0))
    bblk = pl.BlockSpec((TAP_GROUPS, p, n), lambda i: (i, 0, 0))
    return pl.pallas_call(
        body, name="s5_taps_bwd", grid=(g // TAP_GROUPS,),
        in_specs=[pl.BlockSpec((TAP_GROUPS, p, el * p), lambda i: (i, 0, 0)), ablk, ablk, bblk, bblk],
        out_specs=[ablk, ablk, bblk, bblk],
        out_shape=[jax.ShapeDtypeStruct(car.shape, F32), jax.ShapeDtypeStruct(car.shape, F32),
                   jax.ShapeDtypeStruct(bt_r.shape, F32), jax.ShapeDtypeStruct(bt_r.shape, F32)],
        compiler_params=_params(("arbitrary",)),
    )(dr, car, cai, bt_r, bt_i)


@jax.custom_vjp
def _taps(car, cai, bt_r, bt_i):
    return _taps_fwd(car, cai, bt_r, bt_i)


_taps.defvjp(lambda *ops: (_taps_fwd(*ops), ops), lambda ops, dr: _taps_bwd(dr, *ops))


def _s5_build(lam_re, lam_im, log_dt, b_re, b_im, c_re, c_im, d):
    el, g, n, p, nb = SSM_CHUNK, SSM_GROUPS, SSM_STATE, SSM_GROUP, N_LANE_BLOCKS
    lr = jnp.minimum(lam_re, -1e-4)
    li = lam_im
    dt = jnp.exp(log_dt)[:, None]
    mag = jnp.exp(lr * dt)
    ang = li * dt
    lbr, lbi = mag * jnp.cos(ang), mag * jnp.sin(ang)
    num_r, num_i = lbr - 1.0, lbi
    den = lr * lr + li * li
    coef_r = (num_r * lr + num_i * li) / den
    coef_i = (num_i * lr - num_r * li) / den
    bbar_r = coef_r[..., None] * b_re - coef_i[..., None] * b_im
    bbar_i = coef_r[..., None] * b_im + coef_i[..., None] * b_re
    k = jnp.arange(el + 1, dtype=F32)[:, None, None]
    pmag = jnp.exp(k * (lr * dt)[None])
    pr, pi = pmag * jnp.cos(k * ang[None]), pmag * jnp.sin(k * ang[None])
    car = c_re[None] * pr[:, :, None, :] - c_im[None] * pi[:, :, None, :]
    cai = c_re[None] * pi[:, :, None, :] + c_im[None] * pr[:, :, None, :]
    bt_r = bbar_r.transpose(0, 2, 1)
    bt_i = bbar_i.transpose(0, 2, 1)
    kern = _taps(car, cai, bt_r, bt_i).reshape(g, p, el, p).transpose(2, 0, 1, 3)
    kern = kern.at[0].add(jnp.eye(p, dtype=F32)[None] * d[:, None, :])
    bt_r, bt_i = bt_r[None], bt_i[None]
    kc = kern.reshape(el, g * p, p)
    rev = el - 1 - jnp.arange(el)
    qr, qi = pr[rev][:, :, None, :], pi[rev][:, :, None, :]
    sw_r = (qr * bt_r - qi * bt_i).reshape(el, g * p, n)
    sw_i = (qr * bt_i + qi * bt_r).reshape(el, g * p, n)
    sg_r = car[1:].reshape(el, g * p, n)
    sg_i = (-cai[1:]).reshape(el, g * p, n)
    a = jnp.stack([pr[el].reshape(nb, LANE_GROUPS * n), pi[el].reshape(nb, LANE_GROUPS * n)], axis=1)
    return kc, sw_r, sw_i, sg_r, sg_i, a


def _expand(src, reps):
    rows, w = src.shape
    cols = reps * w
    r = lax.broadcasted_iota(jnp.int32, (w, cols), 0)
    c = lax.broadcasted_iota(jnp.int32, (w, cols), 1)
    rep = (r == (c & (w - 1))).astype(BF16)
    out = _dot(src.astype(BF16), rep)
    rg = lax.broadcasted_iota(jnp.int32, (rows, cols), 0) // SSM_GROUP
    cg = lax.broadcasted_iota(jnp.int32, (rows, cols), 1) // w
    return jnp.where(rg == cg, out, 0.0).astype(BF16)


def _fold(x, w):
    rows, cols = x.shape
    rg = lax.broadcasted_iota(jnp.int32, (rows, cols), 0) // SSM_GROUP
    cg = lax.broadcasted_iota(jnp.int32, (rows, cols), 1) // w
    x = jnp.where(rg == cg, x, 0.0)
    while cols > LANES:
        x = x[:, :cols // 2] + x[:, cols // 2:]
        cols //= 2
    s = LANES // 2
    while s >= w:
        x = x + pltpu.roll(x, s, axis=1)
        s //= 2
    return x[:, :w]


def _build_maps(s_ref, dst):
    for j in range(SSM_CHUNK):
        dst[j * LANES:(j + 1) * LANES, :] = _expand(s_ref[j], LANE_GROUPS)


def _build_toeplitz(kc_ref, dst):
    dst[...] = jnp.zeros_like(dst)
    for d in range(SSM_CHUNK):
        blk = _expand(kc_ref[d], LANE_GROUPS)
        for ji in range(SSM_CHUNK - d):
            jo = ji + d
            dst[ji * LANES:(ji + 1) * LANES, jo * LANES:(jo + 1) * LANES] = blk


def _cblk(w):
    return pl.BlockSpec((SSM_CHUNK, LANES, w), lambda b: (0, b, 0))


def _tblk(t):
    return pl.BlockSpec((1, t, LANES), lambda b: (b, 0, 0))


def _load_chunks(ref, nc):
    return jnp.concatenate([ref[0, pl.ds(j, nc, stride=SSM_CHUNK), :] for j in range(SSM_CHUNK)], axis=-1).astype(BF16)


def _store_chunks(ref, val, nc):
    for j in range(SSM_CHUNK):
        ref[0, pl.ds(j, nc, stride=SSM_CHUNK), :] = val[:, j * LANES:(j + 1) * LANES]


def _s5a_state(u4, sw_r, sw_i, a8):
    nb, t, _ = u4.shape
    nc = t // SSM_CHUNK
    sc = STATE_COLS

    def body(u_ref, swr_ref, swi_ref, a_ref, hr_ref, hi_ref, w_s, xr_s, xi_s):
        u = _load_chunks(u_ref, nc)
        _build_maps(swr_ref, w_s)
        xr_s[...] = _dot(u, w_s[...])
        _build_maps(swi_ref, w_s)
        xi_s[...] = _dot(u, w_s[...])
        ar = a_ref[0, 0:1, :]
        ai = a_ref[0, 1:2, :]

        def step(c, carry):
            hr, hi = carry
            hr_ref[0, pl.ds(c, 1), :] = hr
            hi_ref[0, pl.ds(c, 1), :] = hi
            xr = xr_s[pl.ds(c, 1), :]
            xi = xi_s[pl.ds(c, 1), :]
            return ar * hr - ai * hi + xr, ar * hi + ai * hr + xi

        z = jnp.zeros((1, sc), F32)
        lax.fori_loop(0, nc, step, (z, z))

    return pl.pallas_call(
        body, name="s5a_state", grid=(nb,),
        in_specs=[_tblk(t), _cblk(SSM_STATE), _cblk(SSM_STATE),
                  pl.BlockSpec((1, 8, sc), lambda b: (b, 0, 0))],
        out_specs=[pl.BlockSpec((1, nc, sc), lambda b: (b, 0, 0))] * 2,
        out_shape=[jax.ShapeDtypeStruct((nb, nc, sc), F32)] * 2,
        scratch_shapes=[pltpu.VMEM((CHUNK_COLS, sc), BF16), pltpu.VMEM((nc, sc), F32), pltpu.VMEM((nc, sc), F32)],
        compiler_params=_params(("arbitrary",)),
    )(u4, sw_r, sw_i, a8)


def _s5b_out(u4, kc, sg_r, sg_i, hr, hi):
    nb, t, _ = u4.shape
    nc = t // SSM_CHUNK
    sc = STATE_COLS
    cw = 512

    def body(u_ref, kc_ref, sgr_ref, sgi_ref, hr_ref, hi_ref, y_ref, tm_s, gr_s, gi_s):
        _build_toeplitz(kc_ref, tm_s)
        _build_maps(sgr_ref, gr_s)
        _build_maps(sgi_ref, gi_s)
        u = _load_chunks(u_ref, nc)
        h_r = hr_ref[0].astype(BF16)
        h_i = hi_ref[0].astype(BF16)
        for j in range(CHUNK_COLS // cw):
            cs = slice(j * cw, (j + 1) * cw)
            y = _dot(u, tm_s[:, cs]) + _dot_nt(h_r, gr_s[cs, :]) + _dot_nt(h_i, gi_s[cs, :])
            for q in range(cw // LANES):
                step = j * (cw // LANES) + q
                y_ref[0, pl.ds(step, nc, stride=SSM_CHUNK), :] = y[:, q * LANES:(q + 1) * LANES]

    return pl.pallas_call(
        body, name="s5b_out", grid=(nb,),
        in_specs=[_tblk(t), _cblk(SSM_GROUP), _cblk(SSM_STATE),
                  _cblk(SSM_STATE), pl.BlockSpec((1, nc, sc), lambda b: (b, 0, 0)),
                  pl.BlockSpec((1, nc, sc), lambda b: (b, 0, 0))],
        out_specs=_tblk(t),
        out_shape=jax.ShapeDtypeStruct((nb, t, LANES), F32),
        scratch_shapes=[pltpu.VMEM((CHUNK_COLS, CHUNK_COLS), BF16), pltpu.VMEM((CHUNK_COLS, sc), BF16),
                        pltpu.VMEM((CHUNK_COLS, sc), BF16)],
        compiler_params=_params(("arbitrary",)),
    )(u4, kc, sg_r, sg_i, hr, hi)


def _f4_mixer(ys4, prest, x, modv, cvv, cw32, w_a, w_b, w_pw, w_out, late_sh, late_full, tb):
    t = x.shape[0]
    hb = tb // CONV_HALO
    nt = t // tb
    nl = len(LATE)

    def body(ys_ref, pr_ref, halo_ref, x_ref, modv_ref, cvv_ref, cw_ref, wa_ref, wb_ref, wpw_ref, wout_ref, *rest):
        sh, full = rest[:nl], rest[2 * nl:3 * nl]
        r1_ref, ya_ref, yb_ref, ycv_ref, vc_ref, yg_ref, vs_ref, mg_ref, vbuf, vrot, send, recv, fsend, frecv = rest[3 * nl:]
        i = pl.program_id(0)

        @pl.when(i == 0)
        def _():
            _gather_start(LATE, sh, full, send, recv)

        ys = jnp.concatenate([ys_ref[b] for b in range(N_LANE_BLOCKS)], axis=-1)
        yg = _gelu(ys).astype(BF16)
        yg_ref[...] = yg
        ya = _dot(yg, wa_ref[...])
        yb = _dot(yg, wb_ref[...])
        ya_ref[...] = ya.astype(BF16)
        yb_ref[...] = yb.astype(BF16)
        yssm = ya * _sig(yb)
        hv = halo_ref[:, 0:CONV_WIDTH] * _sig(halo_ref[:, CONV_WIDTH:2 * CONV_WIDTH])
        vbuf[0:CONV_HALO, :] = jnp.where(i == 0, 0.0, hv)
        vbuf[CONV_HALO:, :] = pr_ref[:, 0:CONV_WIDTH] * _sig(pr_ref[:, CONV_WIDTH:2 * CONV_WIDTH])
        _fill_rotations(vbuf, vrot, tb + CONV_HALO - 8)
        acc = jnp.zeros((tb, CONV_WIDTH), F32)
        for k in range(CONV_KERNEL):
            acc += _rows_at(vbuf, vrot, CONV_HALO - CONV_KERNEL + 1 + k, tb) * cw_ref[k:k + 1, :]
        vc = acc + cvv_ref[0:1, :]
        vc_ref[...] = vc
        xh, _ = _ln_stats(vc)
        vl = xh * cvv_ref[1:2, :] + cvv_ref[2:3, :]
        vs = (vl * _sig(vl)).astype(BF16)
        vs_ref[...] = vs
        ycv = _dot(vs, wpw_ref[...])
        ycv_ref[...] = ycv.astype(BF16)
        gs = pr_ref[:, 2 * CONV_WIDTH:2 * CONV_WIDTH + D_MODEL]
        gc = pr_ref[:, 2 * CONV_WIDTH + D_MODEL:]
        merged = (_sig(gs) * yssm + _sig(gc) * ycv).astype(BF16)
        mg_ref[...] = merged
        ym = _dot(merged, wout_ref[...])
        r1_ref[...] = ALPHA * x_ref[...] + modv_ref[2:3, :] * ym

        @pl.when(i == nt - 1)
        def _():
            _gather_finish(LATE, sh, full, send, recv, fsend, frecv)

    tok = lambda w: pl.BlockSpec((tb, w), lambda i: (i, 0))
    sem = pltpu.SemaphoreType.DMA((nl, 3))
    n_in = 11
    return pl.pallas_call(
        body, name="f4_mixer", grid=(nt,),
        in_specs=[pl.BlockSpec((N_LANE_BLOCKS, tb, LANES), lambda i: (0, i, 0)), tok(prest.shape[1]),
                  pl.BlockSpec((CONV_HALO, 2 * CONV_WIDTH), lambda i: (jnp.maximum(i * hb - 1, 0), 0)),
                  tok(D_MODEL), _full(modv), _full(cvv), _full(cw32), _full(w_a), _full(w_b), _full(w_pw), _full(w_out)]
        + [ANY] * (2 * nl),
        out_specs=[ANY] * nl + [tok(D_MODEL), tok(D_MODEL), tok(D_MODEL), tok(D_MODEL), tok(CONV_WIDTH), tok(SSM_WIDTH),
                                tok(CONV_WIDTH), tok(D_MODEL)],
        input_output_aliases={n_in + nl + k: k for k in range(nl)},
        out_shape=[jax.ShapeDtypeStruct(f.shape, f.dtype) for f in late_full]
        + [jax.ShapeDtypeStruct((t, D_MODEL), F32), jax.ShapeDtypeStruct((t, D_MODEL), BF16),
                   jax.ShapeDtypeStruct((t, D_MODEL), BF16), jax.ShapeDtypeStruct((t, D_MODEL), BF16),
                   jax.ShapeDtypeStruct((t, CONV_WIDTH), F32), jax.ShapeDtypeStruct((t, SSM_WIDTH), BF16),
                   jax.ShapeDtypeStruct((t, CONV_WIDTH), BF16), jax.ShapeDtypeStruct((t, D_MODEL), BF16)],
        scratch_shapes=[pltpu.VMEM((tb + CONV_HALO, CONV_WIDTH), F32),
                        pltpu.VMEM((7, tb + CONV_HALO - 8, CONV_WIDTH), F32), sem, sem, sem, sem],
        compiler_params=_params(("arbitrary",)),
    )(ys4, prest, prest, x, modv, cvv, cw32, w_a, w_b, w_pw, w_out, *late_sh, *late_full)


FFN_COLS = 1408


def _f5_ffn(r1, tgt, modv, lnv, fdw, w_up, w_down, tb):
    t = r1.shape[0]
    fw = 2 * FFN_HIDDEN

    def body(r1_ref, tgt_ref, modv_ref, lnv_ref, fdw_ref, wup_hbm, wdn_hbm,
             dr2_ref, d_ref, up_ref, z_ref, acc_ref, wup_v, wdn_v, upbuf, gbuf, hbuf, sems):
        i = pl.program_id(0)

        @pl.when(i == 0)
        def _():
            _load_once(wup_hbm, wup_v, sems.at[0])
            _load_once(wdn_hbm, wdn_v, sems.at[1])
            acc_ref[...] = jnp.zeros_like(acc_ref)
            upbuf[0:8, :] = jnp.zeros((8, fw), F32)

        xh1, _ = _ln_stats(r1_ref[...])
        x1 = xh1 * lnv_ref[0:1, :] + lnv_ref[1:2, :]
        xn2, _ = _ln_stats(x1)
        h2 = (xn2 * (1.0 + modv_ref[4:5, :]) + modv_ref[3:4, :]).astype(BF16)
        for j in range(fw // FFN_COLS):
            cs = slice(j * FFN_COLS, (j + 1) * FFN_COLS)
            up = _dot(h2, wup_v[:, cs])
            upbuf[8:, cs] = up
            up_ref[:, cs] = up.astype(BF16)

        def conv(cs):
            return (fdw_ref[0:1, cs] * upbuf[pl.ds(6, tb), cs] + fdw_ref[1:2, cs] * upbuf[pl.ds(7, tb), cs]
                    + fdw_ref[2:3, cs] * upbuf[pl.ds(8, tb), cs] + fdw_ref[3:4, cs])

        halves = [(slice(j * FFN_COLS, (j + 1) * FFN_COLS),
                   slice(FFN_HIDDEN + j * FFN_COLS, FFN_HIDDEN + (j + 1) * FFN_COLS)) for j in range(FFN_HIDDEN // FFN_COLS)]
        yf = jnp.zeros((tb, D_MODEL), F32)
        for ca, cv in halves:
            v = conv(cv)
            g, dg = _gelu_and_grad(conv(ca))
            gbuf[:, ca] = g.astype(BF16)
            hbuf[:, ca] = (v * dg).astype(BF16)
            z = (g * v).astype(BF16)
            z_ref[:, ca] = z
            yf += _dot(z, wdn_v[ca, :])
        r2 = ALPHA * x1 + modv_ref[5:6, :] * yf
        xh2, rstd2 = _ln_stats(r2)
        e = xh2 * lnv_ref[2:3, :] + lnv_ref[3:4, :] - tgt_ref[...]
        dx2 = e * (1.0 / D_MODEL)
        acc_ref[3:4, :] += _colsum(e * e) * (0.5 / D_MODEL)
        acc_ref[0:1, :] += _colsum(dx2 * xh2)
        acc_ref[1:2, :] += _colsum(dx2)
        dr2 = _ln_bwd(dx2 * lnv_ref[2:3, :], xh2, rstd2)
        dr2_ref[...] = dr2
        acc_ref[2:3, :] += _colsum(dr2 * yf)
        dyf = (modv_ref[5:6, :] * dr2).astype(BF16)
        for ca, cv in halves:
            dz = _dot_nt(dyf, wdn_v[ca, :])
            d_ref[:, ca] = (dz * hbuf[:, ca].astype(F32)).astype(BF16)
            d_ref[:, cv] = (dz * gbuf[:, ca].astype(F32)).astype(BF16)
        upbuf[0:8, :] = upbuf[pl.ds(tb, 8), :]

    tok = lambda w: pl.BlockSpec((tb, w), lambda i: (i, 0))
    return pl.pallas_call(
        body, name="f5_ffn", grid=(t // tb,),
        in_specs=[tok(D_MODEL), tok(D_MODEL), _full(modv), _full(lnv), _full(fdw), ANY, ANY],
        out_specs=[tok(D_MODEL), tok(fw), tok(fw), tok(FFN_HIDDEN), pl.BlockSpec((8, D_MODEL), lambda i: (0, 0))],
        out_shape=[jax.ShapeDtypeStruct((t, D_MODEL), F32), jax.ShapeDtypeStruct((t, fw), BF16),
                   jax.ShapeDtypeStruct((t, fw), BF16), jax.ShapeDtypeStruct((t, FFN_HIDDEN), BF16),
                   jax.ShapeDtypeStruct((8, D_MODEL), F32)],
        scratch_shapes=[pltpu.VMEM(w_up.shape, BF16), pltpu.VMEM(w_down.shape, BF16),
                        pltpu.VMEM((tb + 8, fw), F32), pltpu.VMEM((tb, FFN_HIDDEN), BF16),
                        pltpu.VMEM((tb, FFN_HIDDEN), BF16), pltpu.SemaphoreType.DMA((2,))],
        compiler_params=_params(("arbitrary",)),
    )(r1, tgt, modv, lnv, fdw, w_up, w_down)


def _b1b_ffn_up(d, up, dr2, r1, modv, lnv, fdw, w_up, tb):
    t = dr2.shape[0]
    fw = 2 * FFN_HIDDEN
    nt = t // tb
    hb = tb // 16

    def body(d_ref, nxt_ref, up_ref, dr2_ref, r1_ref, modv_ref, lnv_ref, fdw_ref, wup_hbm, dup_ref, dr1_ref, h2_ref,
             dyf_ref, acc_ref, accw_ref, wup_v, dbuf, shifted, sem):
        i = pl.program_id(0)

        @pl.when(i == 0)
        def _():
            _load_once(wup_hbm, wup_v, sem)
            acc_ref[...] = jnp.zeros_like(acc_ref)
            accw_ref[...] = jnp.zeros_like(accw_ref)

        dbuf[0:tb, :] = d_ref[...].astype(F32)
        dbuf[tb:, :] = jnp.where(i == nt - 1, 0.0, nxt_ref[...].astype(F32))
        dh2 = jnp.zeros((tb, D_MODEL), F32)
        for j in range(fw // FFN_COLS):
            cs = slice(j * FFN_COLS, (j + 1) * FFN_COLS)
            for k in range(1, FFN_KERNEL):
                shifted[k - 1] = dbuf[pl.ds(k, tb), cs]
            ds = [dbuf[pl.ds(0, tb), cs], shifted[0], shifted[1]]
            dup = (fdw_ref[2:3, cs] * ds[0] + fdw_ref[1:2, cs] * ds[1] + fdw_ref[0:1, cs] * ds[2]).astype(BF16)
            dup_ref[:, cs] = dup
            dh2 += _dot_nt(dup, wup_v[:, cs])
            upf = up_ref[:, cs].astype(F32)
            for k in range(FFN_KERNEL):
                accw_ref[k:k + 1, cs] += _colsum(ds[FFN_KERNEL - 1 - k] * upf)
            accw_ref[3:4, cs] += _colsum(ds[0])
        xh1, rstd1 = _ln_stats(r1_ref[...])
        x1 = xh1 * lnv_ref[0:1, :] + lnv_ref[1:2, :]
        xn2, rstd2 = _ln_stats(x1)
        h2_ref[...] = (xn2 * (1.0 + modv_ref[4:5, :]) + modv_ref[3:4, :]).astype(BF16)
        dr2 = dr2_ref[...]
        dyf_ref[...] = (modv_ref[5:6, :] * dr2).astype(BF16)
        acc_ref[0:1, :] += _colsum(dh2 * xn2)
        acc_ref[1:2, :] += _colsum(dh2)
        dx1 = _ln_bwd(dh2 * (1.0 + modv_ref[4:5, :]), xn2, rstd2) + ALPHA * dr2
        acc_ref[2:3, :] += _colsum(dx1 * xh1)
        acc_ref[3:4, :] += _colsum(dx1)
        dr1_ref[...] = _ln_bwd(dx1 * lnv_ref[0:1, :], xh1, rstd1)

    tok = lambda w: pl.BlockSpec((tb, w), lambda i: (i, 0))
    return pl.pallas_call(
        body, name="b1b_ffn_up", grid=(nt,),
        in_specs=[tok(fw), pl.BlockSpec((16, fw), lambda i: (jnp.minimum((i + 1) * hb, t // 16 - 1), 0)), tok(fw),
                  tok(D_MODEL), tok(D_MODEL), _full(modv), _full(lnv), _full(fdw), ANY],
        out_specs=[tok(fw), tok(D_MODEL), tok(D_MODEL), tok(D_MODEL), pl.BlockSpec((8, D_MODEL), lambda i: (0, 0)),
                   pl.BlockSpec((8, fw), lambda i: (0, 0))],
        out_shape=[jax.ShapeDtypeStruct((t, fw), BF16), jax.ShapeDtypeStruct((t, D_MODEL), F32),
                   jax.ShapeDtypeStruct((t, D_MODEL), BF16), jax.ShapeDtypeStruct((t, D_MODEL), BF16),
                   jax.ShapeDtypeStruct((8, D_MODEL), F32), jax.ShapeDtypeStruct((8, fw), F32)],
        scratch_shapes=[pltpu.VMEM(w_up.shape, BF16), pltpu.VMEM((tb + 16, fw), F32),
                        pltpu.VMEM((FFN_KERNEL - 1, tb, FFN_COLS), F32), pltpu.SemaphoreType.DMA],
        compiler_params=_params(("arbitrary",)),
    )(d, d, up, dr2, r1, modv, lnv, fdw, w_up)


def _b2_mixer(dr1, ys4, prest, ya, yb, ycv, vc, merged, modv, cvv, cw32, w_a, w_b, w_pw, w_out, late_dw, tb):
    t = dr1.shape[0]
    nt = t // tb
    nl = len(LATE)
    cwd = CONV_WIDTH

    def body(dr1_ref, ys_ref, pr_ref, ya_ref, yb_ref, ycv_ref, vc_ref, mg_ref, modv_ref, cvv_ref, cw_ref,
             wa_ref, wb_ref, wpw_ref, wout_ref, *rest):
        dw, got = rest[:nl], rest[nl:2 * nl]
        (dys_ref, dpr_ref, dya_ref, dyb_ref, dycv_ref, dym_ref, acc_a, acc_b, acc_w, dvbuf, dvrot,
         send, recv) = rest[2 * nl:]
        i = pl.program_id(0)
        ti = nt - 1 - i

        @pl.when(i == 0)
        def _():
            for cp in _scatter(LATE, dw, got, send, recv):
                cp.start()
            acc_a[...] = jnp.zeros_like(acc_a)
            acc_b[...] = jnp.zeros_like(acc_b)
            acc_w[...] = jnp.zeros_like(acc_w)
            dvbuf[pl.ds(tb, CONV_HALO), :] = jnp.zeros((CONV_HALO, cwd), F32)

        dr1 = dr1_ref[...]
        dym = (modv_ref[2:3, :] * dr1).astype(BF16)
        dym_ref[...] = dym
        ym = _dot(mg_ref[...], wout_ref[...])
        acc_a[0:1, :] += _colsum(dr1 * ym)
        dmg = _dot_nt(dym, wout_ref[...])
        sgs = _sig(pr_ref[:, 2 * cwd:2 * cwd + D_MODEL])
        sgc = _sig(pr_ref[:, 2 * cwd + D_MODEL:])
        ya_v = ya_ref[...].astype(F32)
        syb = _sig(yb_ref[...].astype(F32))
        ycv_v = ycv_ref[...].astype(F32)
        dpr_ref[:, 2 * cwd:2 * cwd + D_MODEL] = (dmg * (ya_v * syb) * sgs * (1.0 - sgs)).astype(BF16)
        dpr_ref[:, 2 * cwd + D_MODEL:] = (dmg * ycv_v * sgc * (1.0 - sgc)).astype(BF16)
        dyssm = dmg * sgs
        dya = (dyssm * syb).astype(BF16)
        dyb = (dyssm * ya_v * syb * (1.0 - syb)).astype(BF16)
        dya_ref[...] = dya
        dyb_ref[...] = dyb
        dyg = _dot_nt(dya, wa_ref[...]) + _dot_nt(dyb, wb_ref[...])
        ys = jnp.concatenate([ys_ref[b] for b in range(N_LANE_BLOCKS)], axis=-1)
        dys = dyg * _dgelu(ys)
        for b in range(N_LANE_BLOCKS):
            dys_ref[b] = dys[:, b * LANES:(b + 1) * LANES]
        dycv = (dmg * sgc).astype(BF16)
        dycv_ref[...] = dycv
        dvs = _dot_nt(dycv, wpw_ref[...])
        xh, rstd = _ln_stats(vc_ref[...])
        vl = xh * cvv_ref[1:2, :] + cvv_ref[2:3, :]
        s = _sig(vl)
        dvl = dvs * s * (1.0 + vl * (1.0 - s))
        acc_b[1:2, :] += _colsum(dvl * xh)
        acc_b[2:3, :] += _colsum(dvl)
        dvc = _ln_bwd(dvl * cvv_ref[1:2, :], xh, rstd)
        acc_b[0:1, :] += _colsum(dvc)
        cva = pr_ref[:, 0:cwd]
        scg = _sig(pr_ref[:, cwd:2 * cwd])
        v = cva * scg
        dvbuf[0:tb, :] = dvc
        _fill_rotations(dvbuf, dvrot, tb + CONV_HALO - 8)
        dv = jnp.zeros((tb, cwd), F32)
        for k in range(CONV_KERNEL):
            later = _rows_at(dvbuf, dvrot, CONV_KERNEL - 1 - k, tb)
            dv += later * cw_ref[k:k + 1, :]
            acc_w[k:k + 1, :] += _colsum(later * v)
        dvbuf[pl.ds(tb, CONV_HALO), :] = dvbuf[0:CONV_HALO, :]
        dpr_ref[:, 0:cwd] = (dv * scg).astype(BF16)
        dpr_ref[:, cwd:2 * cwd] = (dv * cva * scg * (1.0 - scg)).astype(BF16)

        @pl.when(i == nt - 1)
        def _():
            for cp in _scatter(LATE, dw, got, send, recv):
                cp.wait()

    rtok = lambda w: pl.BlockSpec((tb, w), lambda i: (nt - 1 - i, 0))
    r4 = pl.BlockSpec((N_LANE_BLOCKS, tb, LANES), lambda i: (0, nt - 1 - i, 0))
    pw = prest.shape[1]
    return pl.pallas_call(
        body, name="b2_mixer", grid=(nt,),
        in_specs=[rtok(D_MODEL), r4, rtok(pw),
                  rtok(D_MODEL), rtok(D_MODEL), rtok(D_MODEL), rtok(cwd), rtok(D_MODEL),
                  _full(modv), _full(cvv), _full(cw32), _full(w_a), _full(w_b), _full(w_pw), _full(w_out)] + [ANY] * nl,
        out_specs=[ANY] * nl + [r4, rtok(pw), rtok(D_MODEL), rtok(D_MODEL), rtok(D_MODEL), rtok(D_MODEL),
                   pl.BlockSpec((8, D_MODEL), lambda i: (0, 0)), pl.BlockSpec((8, cwd), lambda i: (0, 0)),
                   pl.BlockSpec((CONV_HALO, cwd), lambda i: (0, 0))],
        out_shape=[jax.ShapeDtypeStruct((N_DEV - 1,) + _piece(*BIG[m][1:]), BF16) for m in LATE]
        + [jax.ShapeDtypeStruct((N_LANE_BLOCKS, t, LANES), F32), jax.ShapeDtypeStruct((t, pw), BF16),
                   jax.ShapeDtypeStruct((t, D_MODEL), BF16), jax.ShapeDtypeStruct((t, D_MODEL), BF16),
                   jax.ShapeDtypeStruct((t, D_MODEL), BF16), jax.ShapeDtypeStruct((t, D_MODEL), BF16),
                   jax.ShapeDtypeStruct((8, D_MODEL), F32), jax.ShapeDtypeStruct((8, cwd), F32),
                   jax.ShapeDtypeStruct((CONV_HALO, cwd), F32)],
        scratch_shapes=[pltpu.VMEM((tb + CONV_HALO, cwd), F32), pltpu.VMEM((7, tb + CONV_HALO - 8, cwd), F32),
                        pltpu.SemaphoreType.DMA((nl, N_DEV - 1)), pltpu.SemaphoreType.DMA((nl, N_DEV - 1))],
        compiler_params=_params(("arbitrary",)),
    )(dr1, ys4, prest, ya, yb, ycv, vc, merged, modv, cvv, cw32, w_a, w_b, w_pw, w_out, *late_dw)


def _s5c_state_bwd(dy4, sg_r, sg_i, a8, hr, hi):
    nb, t, _ = dy4.shape
    nc = t // SSM_CHUNK
    sc = STATE_COLS

    def body(dy_ref, sgr_ref, sgi_ref, a_ref, hr_ref, hi_ref, dxr_ref, dxi_ref, da_ref, dsgr_ref, dsgi_ref,
             g_s, lr_s, li_s, xr_s, xi_s):
        dy = _load_chunks(dy_ref, nc)
        _build_maps(sgr_ref, g_s)
        lr_s[...] = _dot(dy, g_s[...])
        _build_maps(sgi_ref, g_s)
        li_s[...] = _dot(dy, g_s[...])
        ar = a_ref[0, 0:1, :]
        ai = a_ref[0, 1:2, :]

        def step(k, carry):
            pr, pi, dar, dai = carry
            c = nc - 1 - k
            xr_s[pl.ds(c, 1), :] = pr
            xi_s[pl.ds(c, 1), :] = pi
            h_r = hr_ref[0, pl.ds(c, 1), :]
            h_i = hi_ref[0, pl.ds(c, 1), :]
            dar = dar + pr * h_r + pi * h_i
            dai = dai - pr * h_i + pi * h_r
            nr = lr_s[pl.ds(c, 1), :] + ar * pr + ai * pi
            ni = li_s[pl.ds(c, 1), :] - ai * pr + ar * pi
            return nr, ni, dar, dai

        z = jnp.zeros((1, sc), F32)
        _, _, dar, dai = lax.fori_loop(0, nc, step, (z, z, z, z))
        da_ref[0] = jnp.concatenate([dar, dai, jnp.zeros((6, sc), F32)], axis=0)
        dxr_ref[0] = xr_s[...].astype(BF16)
        dxi_ref[0] = xi_s[...].astype(BF16)
        for h_ref, o_ref in ((hr_ref, dsgr_ref), (hi_ref, dsgi_ref)):
            hb = h_ref[0].astype(BF16)
            for j in range(SSM_CHUNK):
                o_ref[j] = _fold(_dot_tn(dy[:, j * LANES:(j + 1) * LANES], hb), SSM_STATE)

    blk = lambda r, c: pl.BlockSpec((1, r, c), lambda b: (b, 0, 0))
    return pl.pallas_call(
        body, name="s5c_state_bwd", grid=(nb,),
        in_specs=[_tblk(t), _cblk(SSM_STATE), _cblk(SSM_STATE), blk(8, sc), blk(nc, sc), blk(nc, sc)],
        out_specs=[blk(nc, sc), blk(nc, sc), blk(8, sc), _cblk(SSM_STATE), _cblk(SSM_STATE)],
        out_shape=[jax.ShapeDtypeStruct((nb, nc, sc), BF16), jax.ShapeDtypeStruct((nb, nc, sc), BF16),
                   jax.ShapeDtypeStruct((nb, 8, sc), F32),
                   jax.ShapeDtypeStruct((SSM_CHUNK, SSM_WIDTH, SSM_STATE), F32),
                   jax.ShapeDtypeStruct((SSM_CHUNK, SSM_WIDTH, SSM_STATE), F32)],
        scratch_shapes=[pltpu.VMEM((CHUNK_COLS, sc), BF16)] + [pltpu.VMEM((nc, sc), F32)] * 4,
        compiler_params=_params(("arbitrary",)),
    )(dy4, sg_r, sg_i, a8, hr, hi)


def _s5d_input_bwd(dy4, u4, kc, sw_r, sw_i, dxr, dxi, mid_dw):
    nb, t, _ = dy4.shape
    nc = t // SSM_CHUNK
    sc = STATE_COLS
    nl = len(MID)

    def body(dy_ref, u_ref, kc_ref, swr_ref, swi_ref, dxr_ref, dxi_ref, *rest):
        dw, got = rest[:nl], rest[nl:2 * nl]
        du_ref, dkc_ref, dswr_ref, dswi_ref, tm_s, w_s, dk_s, send, recv = rest[2 * nl:]

        @pl.when(pl.program_id(0) == 0)
        def _():
            for cp in _scatter(MID, dw, got, send, recv):
                cp.start()

        dy = _load_chunks(dy_ref, nc)
        u = _load_chunks(u_ref, nc)
        _build_toeplitz(kc_ref, tm_s)
        du = _dot_nt(dy, tm_s[...])
        _build_maps(swr_ref, w_s)
        du += _dot_nt(dxr_ref[0], w_s[...])
        _build_maps(swi_ref, w_s)
        du += _dot_nt(dxi_ref[0], w_s[...])
        _store_chunks(du_ref, du, nc)
        dk_s[...] = jnp.zeros_like(dk_s)
        for ji in range(SSM_CHUNK):
            uj = u[:, ji * LANES:(ji + 1) * LANES]
            rows = _dot_tn(uj, dy)
            for jo in range(ji, SSM_CHUNK):
                dk_s[jo - ji] += rows[:, jo * LANES:(jo + 1) * LANES]
            dswr_ref[ji] = _fold(_dot_tn(uj, dxr_ref[0]), SSM_STATE)
            dswi_ref[ji] = _fold(_dot_tn(uj, dxi_ref[0]), SSM_STATE)
        for d in range(SSM_CHUNK):
            dkc_ref[d] = _fold(dk_s[d], SSM_GROUP)

        @pl.when(pl.program_id(0) == nb - 1)
        def _():
            for cp in _scatter(MID, dw, got, send, recv):
                cp.wait()

    blk = lambda r, c: pl.BlockSpec((1, r, c), lambda b: (b, 0, 0))
    ssem = pltpu.SemaphoreType.DMA((nl, N_DEV - 1))
    return pl.pallas_call(
        body, name="s5d_input_bwd", grid=(nb,),
        in_specs=[_tblk(t), _tblk(t), _cblk(SSM_GROUP), _cblk(SSM_STATE), _cblk(SSM_STATE),
                  blk(nc, sc), blk(nc, sc)] + [ANY] * nl,
        out_specs=[ANY] * nl + [_tblk(t), _cblk(SSM_GROUP), _cblk(SSM_STATE), _cblk(SSM_STATE)],
        out_shape=[jax.ShapeDtypeStruct((N_DEV - 1,) + _piece(*BIG[m][1:]), BF16) for m in MID]
        + [jax.ShapeDtypeStruct((nb, t, LANES), F32),
           jax.ShapeDtypeStruct((SSM_CHUNK, SSM_WIDTH, SSM_GROUP), F32),
           jax.ShapeDtypeStruct((SSM_CHUNK, SSM_WIDTH, SSM_STATE), F32),
           jax.ShapeDtypeStruct((SSM_CHUNK, SSM_WIDTH, SSM_STATE), F32)],
        scratch_shapes=[pltpu.VMEM((CHUNK_COLS, CHUNK_COLS), BF16), pltpu.VMEM((CHUNK_COLS, sc), BF16),
                        pltpu.VMEM((SSM_CHUNK, LANES, LANES), F32), ssem, ssem],
        compiler_params=_params(("arbitrary",)),
    )(dy4, u4, kc, sw_r, sw_i, dxr, dxi, *mid_dw)


def _b3_inproj(x, dr1, du4, dprest, modv, w_in, tb):
    t = x.shape[0]
    pw = IN_PROJ_WIDTH - SSM_WIDTH

    def body(x_ref, dr1_ref, du_ref, dpr_ref, modv_ref, w_hbm, gx_ref, dp_ref, acc_ref, accb_ref, w_v, sem):
        @pl.when(pl.program_id(0) == 0)
        def _():
            _load_once(w_hbm, w_v, sem)
            acc_ref[...] = jnp.zeros_like(acc_ref)
            accb_ref[...] = jnp.zeros_like(accb_ref)

        du = jnp.concatenate([du_ref[b] for b in range(N_LANE_BLOCKS)], axis=-1).astype(BF16)
        dpr = dpr_ref[...]
        dp_ref[:, 0:SSM_WIDTH] = du
        dp_ref[:, SSM_WIDTH:] = dpr
        accb_ref[0:1, 0:SSM_WIDTH] += _colsum(du.astype(F32))
        accb_ref[0:1, SSM_WIDTH:] += _colsum(dpr.astype(F32))
        dh = _dot_nt(du, w_v[:, 0:SSM_WIDTH]) + _dot_nt(dpr, w_v[:, SSM_WIDTH:])
        xn, rstd = _ln_stats(x_ref[...])
        acc_ref[0:1, :] += _colsum(dh * xn)
        acc_ref[1:2, :] += _colsum(dh)
        gx_ref[...] = _ln_bwd(dh * (1.0 + modv_ref[1:2, :]), xn, rstd) + ALPHA * dr1_ref[...]

    tok = lambda w: pl.BlockSpec((tb, w), lambda i: (i, 0))
    return pl.pallas_call(
        body, name="b3_inproj", grid=(t // tb,),
        in_specs=[tok(D_MODEL), tok(D_MODEL), pl.BlockSpec((N_LANE_BLOCKS, tb, LANES), lambda i: (0, i, 0)), tok(pw),
                  _full(modv), ANY],
        out_specs=[tok(D_MODEL), tok(IN_PROJ_WIDTH), pl.BlockSpec((8, D_MODEL), lambda i: (0, 0)),
                   pl.BlockSpec((8, IN_PROJ_WIDTH), lambda i: (0, 0))],
        out_shape=[jax.ShapeDtypeStruct((t, D_MODEL), F32), jax.ShapeDtypeStruct((t, IN_PROJ_WIDTH), BF16),
                   jax.ShapeDtypeStruct((8, D_MODEL), F32), jax.ShapeDtypeStruct((8, IN_PROJ_WIDTH), F32)],
        scratch_shapes=[pltpu.VMEM(w_in.shape, BF16), pltpu.SemaphoreType.DMA],
        compiler_params=_params(("arbitrary",)),
    )(x, dr1, du4, dprest, modv, w_in)


TN_ROWS = 2048


def _tn_matmul(name, a, b, tm, tn):
    t, m = a.shape
    n = b.shape[1]
    tt = min(TN_ROWS, t)
    nk = t // tt

    def body(a_ref, b_ref, o_ref, acc):
        k = pl.program_id(2)

        @pl.when(k == 0)
        def _():
            acc[...] = jnp.zeros_like(acc)

        acc[...] += _dot_tn(a_ref[...], b_ref[...])

        @pl.when(k == nk - 1)
        def _():
            o_ref[...] = acc[...].astype(BF16)

    return pl.pallas_call(
        body, name=name, grid=(m // tm, n // tn, nk),
        in_specs=[pl.BlockSpec((tt, tm), lambda i, j, k: (k, i)), pl.BlockSpec((tt, tn), lambda i, j, k: (k, j))],
        out_specs=pl.BlockSpec((tm, tn), lambda i, j, k: (i, j)),
        out_shape=jax.ShapeDtypeStruct((m, n), BF16),
        scratch_shapes=[pltpu.VMEM((tm, tn), F32)],
        compiler_params=_params(("arbitrary", "arbitrary", "arbitrary")),
    )(a, b)


def _local_step(x, tgt, modv, wb, shards, sp, tb=256):
    t = x.shape[0]
    row8 = lambda rows, w: jnp.concatenate([r.reshape(1, w) for r in rows] + [jnp.zeros((8 - len(rows), w), F32)], axis=0)
    lnv = row8([sp["ln1_g"], sp["ln1_b"], sp["ln2_g"], sp["ln2_b"]], D_MODEL)
    cvv = row8([sp["cv_dw_b"], sp["cv_ln_g"], sp["cv_ln_b"]], CONV_WIDTH)
    cw32 = jnp.concatenate([sp["cv_dw_w"].reshape(CONV_KERNEL, CONV_WIDTH), jnp.zeros((1, CONV_WIDTH), F32)], axis=0)
    fdw = row8(list(sp["ffn_dw_w"].reshape(FFN_KERNEL, 2 * FFN_HIDDEN)) + [sp["ffn_dw_b"]], 2 * FFN_HIDDEN)
    b_in = sp["b_in"].reshape(1, IN_PROJ_WIDTH)
    ssm = tuple(sp[k] for k in ("ssm_lambda_re", "ssm_lambda_im", "ssm_log_dt", "ssm_b_re", "ssm_b_im", "ssm_c_re",
                                "ssm_c_im", "ssm_d"))
    (kc, sw_r, sw_i, sg_r, sg_i, a), ssm_vjp = jax.vjp(_s5_build, *ssm)
    a8 = jnp.concatenate([a, jnp.zeros((N_LANE_BLOCKS, 6, STATE_COLS), F32)], axis=1)

    name = lambda m: BIG[m][0]
    *mid_w, u4, prest, h1 = _f1_inproj(x, modv, b_in, wb["w_in"], [shards[m] for m in MID], [wb[name(m)] for m in MID], tb)
    w_a, w_b, w_pw, w_out = mid_w
    hr, hi = _s5a_state(u4, sw_r, sw_i, a8)
    ys4 = _s5b_out(u4, kc, sg_r, sg_i, hr, hi)
    w_up, w_down, r1, ya, yb, ycv, vc, yg, vs, merged = _f4_mixer(
        ys4, prest, x, modv, cvv, cw32, w_a, w_b, w_pw, w_out, [shards[m] for m in LATE], [wb[name(m)] for m in LATE], tb)
    dr2, dconv, up, z, acc5 = _f5_ffn(r1, tgt, modv, lnv, fdw, w_up, w_down, tb)
    dup, dr1, h2, dyf, acc1b, acc1a = _b1b_ffn_up(dconv, up, dr2, r1, modv, lnv, fdw, w_up, tb)
    late_dw = [_tn_matmul("dw_up", h2, dup, 1024, FFN_COLS), _tn_matmul("dw_down", z, dyf, FFN_COLS, 1024)]
    got_up, got_down, dys4, dprest, dya, dyb, dycv, dym, acc2a, acc2b, acc2w = _b2_mixer(
        dr1, ys4, prest, ya, yb, ycv, vc, merged, modv, cvv, cw32, w_a, w_b, w_pw, w_out, late_dw, tb)
    mid_dw = [_tn_matmul("dw_glu_a", yg, dya, 512, 1024), _tn_matmul("dw_glu_b", yg, dyb, 512, 1024),
              _tn_matmul("dw_pw", vs, dycv, 512, 1024), _tn_matmul("dw_out", merged, dym, 1024, 1024)]
    dxr, dxi, da8, dsg_r, dsg_i = _s5c_state_bwd(dys4, sg_r, sg_i, a8, hr, hi)
    *mid_got, du4, dkc, dsw_r, dsw_i = _s5d_input_bwd(dys4, u4, kc, sw_r, sw_i, dxr, dxi, mid_dw)
    dssm = ssm_vjp((dkc, dsw_r, dsw_i, dsg_r, dsg_i, da8[:, 0:2, :]))
    gx, dp, acc3, acc3b = _b3_inproj(x, dr1, du4, dprest, modv, wb["w_in"], tb)
    dbig = [_tn_matmul("dw_in", h1, dp, 1024, 896)] + mid_dw + late_dw
    dmod = jnp.concatenate([acc3[1], acc3[0], acc2a[0], acc1b[1], acc1b[0], acc5[2]])
    small = {
        "dmod": dmod, "b_in": acc3b[0],
        "ssm_lambda_re": dssm[0], "ssm_lambda_im": dssm[1], "ssm_log_dt": dssm[2], "ssm_b_re": dssm[3],
        "ssm_b_im": dssm[4], "ssm_c_re": dssm[5], "ssm_c_im": dssm[6], "ssm_d": dssm[7],
        "cv_dw_w": acc2w[0:CONV_KERNEL], "cv_dw_b": acc2b[0], "cv_ln_g": acc2b[1], "cv_ln_b": acc2b[2],
        "ln1_g": acc1b[2], "ln1_b": acc1b[3], "ffn_dw_w": acc1a[0:FFN_KERNEL], "ffn_dw_b": acc1a[3],
        "ln2_g": acc5[0], "ln2_b": acc5[1], "loss": jnp.sum(acc5[3]).reshape(1),
    }
    return gx, dbig, list(mid_got) + [got_up, got_down], small


def _allgather_rows(x_ref, out_ref, send_sems, recv_sems, local_sem):
    m_per = x_ref.shape[0]
    x, y, c, chips = _place()
    me, sibling = (x, y, c), (x, y, 1 - c)

    def rows(px, py, pc):
        return out_ref.at[pl.ds((4 * px + 2 * py + pc) * m_per, m_per), :]

    def copy(k, block, to, src=None):
        return pltpu.make_async_remote_copy(
            src_ref=rows(*block) if src is None else src, dst_ref=rows(*block),
            send_sem=send_sems.at[k], recv_sem=recv_sems.at[k], device_id=to, device_id_type=MESH)

    mine = pltpu.make_async_copy(x_ref, rows(*me), local_sem)
    mine.start()
    first = [copy(0, me, sibling, src=x_ref)]
    first += [copy(1 + j, me, (*chip, c), src=x_ref) for j, chip in enumerate(chips)]
    for cp in first:
        cp.start()
    passed = [copy(4 + j, (*chip, c), sibling) for j, chip in enumerate(chips)]
    for j, chip in enumerate(chips):
        copy(1 + j, (*chip, c), me).wait_recv()
        passed[j].start()
    copy(0, sibling, me).wait_recv()
    for j, chip in enumerate(chips):
        copy(4 + j, (*chip, 1 - c), me).wait_recv()
    for cp in first + passed:
        cp.wait_send()
    mine.wait()


def _allgather(name, shard):
    m_per, n = shard.shape

    def body(x_ref, out_ref, send_sems, recv_sems, local_sem):
        _allgather_rows(x_ref, out_ref, send_sems, recv_sems, local_sem)

    return pl.pallas_call(
        body, name=name,
        out_shape=jax.ShapeDtypeStruct((N_DEV * m_per, n), shard.dtype),
        in_specs=[pl.BlockSpec(memory_space=pltpu.VMEM)],
        out_specs=pl.BlockSpec(memory_space=pltpu.VMEM),
        scratch_shapes=[pltpu.SemaphoreType.DMA((7,)), pltpu.SemaphoreType.DMA((7,)), pltpu.SemaphoreType.DMA],
        compiler_params=_params(),
    )(shard)


def _add_rows(pr):
    return 64 if pr % 64 == 0 else 16


def _gather_weights(shards, first):
    nm = len(BIG)
    nl = len(DIRECT)

    def body(*refs):
        ins, first_ref = refs[:nm], refs[nm]
        outs, lsh, first_all = refs[nm + 1:2 * nm + 1], refs[2 * nm + 1:2 * nm + 1 + nl], refs[2 * nm + 1 + nl]
        stage = refs[2 * nm + 2 + nl:3 * nm + 2 + nl]
        send, recv, fsend, frecv, lsem, ag_send, ag_recv, ag_local = refs[3 * nm + 2 + nl:]
        x, y, c, chips = _place()
        s_me = 2 * x + y
        sibling = (x, y, 1 - c)
        pend = []
        for m, (_, kind, shape) in enumerate(BIG):
            stage[m][...] = ins[m][...].astype(BF16)
            pr, pc = _piece(kind, shape)
            for k in range(2):
                cp = pltpu.make_async_copy(stage[m].at[pl.ds(k * pr, pr), :], _piece_at(outs[m], kind, shape, s_me, k),
                                           lsem.at[m, k])
                cp.start()
                pend.append(cp.wait)
            if m in DIRECT:
                cp = pltpu.make_async_copy(stage[m], lsh[DIRECT.index(m)], lsem.at[m, 2])
                cp.start()
                pend.append(cp.wait)
                continue
            for j, chip in enumerate(chips):
                cp = pltpu.make_async_remote_copy(
                    src_ref=stage[m].at[pl.ds(pl.multiple_of(c * pr, 16), pr), :],
                    dst_ref=_piece_at(outs[m], kind, shape, s_me, c),
                    send_sem=send.at[m, j], recv_sem=recv.at[m, j], device_id=(*chip, c), device_id_type=MESH)
                cp.start()
                pend.append(cp.wait_send)
        _allgather_rows(first_ref, first_all, ag_send, ag_recv, ag_local)
        for m in EARLY:
            _, kind, shape = BIG[m]
            for j, (cx, cy) in enumerate(chips):
                got = _piece_at(outs[m], kind, shape, 2 * cx + cy, c)
                pltpu.make_async_remote_copy(src_ref=got, dst_ref=got, send_sem=send.at[m, j], recv_sem=recv.at[m, j],
                                             device_id=(cx, cy, c), device_id_type=MESH).wait_recv()
                cp = pltpu.make_async_remote_copy(src_ref=got, dst_ref=got, send_sem=fsend.at[m, j],
                                                  recv_sem=frecv.at[m, j], device_id=sibling, device_id_type=MESH)
                cp.start()
                pend.append(cp.wait_send)
        for m in EARLY:
            _, kind, shape = BIG[m]
            for j, (cx, cy) in enumerate(chips):
                got = _piece_at(outs[m], kind, shape, 2 * cx + cy, 1 - c)
                pltpu.make_async_remote_copy(src_ref=got, dst_ref=got, send_sem=fsend.at[m, j], recv_sem=frecv.at[m, j],
                                             device_id=sibling, device_id_type=MESH).wait_recv()
        for w in pend:
            w()

    sem = lambda *s: pltpu.SemaphoreType.DMA(s)
    res = pl.pallas_call(
        body, name="gather_weights",
        out_shape=[jax.ShapeDtypeStruct(shape, BF16) for _, _, shape in BIG]
        + [jax.ShapeDtypeStruct(shards[m].shape, BF16) for m in DIRECT]
        + [jax.ShapeDtypeStruct((N_DEV * first.shape[0], first.shape[1]), F32)],
        in_specs=[pl.BlockSpec(memory_space=pltpu.VMEM)] * (nm + 1),
        out_specs=[ANY] * (nm + nl) + [pl.BlockSpec(memory_space=pltpu.VMEM)],
        scratch_shapes=[pltpu.VMEM(s.shape, BF16) for s in shards] + [sem(nm, 3), sem(nm, 3), sem(nm, 3), sem(nm, 3),
                                                                         sem(nm, 3), sem(7), sem(7), sem()],
        compiler_params=_params(),
    )(*shards, first)
    return res[:nm], dict(zip(DIRECT, res[nm:nm + nl])), res[nm + nl]


def _rs1_sibling(grads):
    mats = [BIG[m] for m in EARLY]
    nm = len(mats)

    def body(*refs):
        ins, outs = refs[:nm], refs[nm:2 * nm]
        send, recv = refs[2 * nm:]
        x, y, c, _ = _place()
        cps = []
        for m, (_, kind, shape) in enumerate(mats):
            for s in range(N_CHIP):
                cp = pltpu.make_async_remote_copy(
                    src_ref=_piece_at(ins[m], kind, shape, s, 1 - c), dst_ref=outs[m].at[s],
                    send_sem=send.at[m, s], recv_sem=recv.at[m, s], device_id=(x, y, 1 - c), device_id_type=MESH)
                cp.start()
                cps.append(cp)
        for cp in cps:
            cp.wait()

    sem = lambda *s: pltpu.SemaphoreType.DMA(s)
    return pl.pallas_call(
        body, name="rs1_sibling",
        out_shape=[jax.ShapeDtypeStruct((N_CHIP,) + _piece(kind, shape), BF16) for _, kind, shape in mats],
        in_specs=[ANY] * nm, out_specs=[ANY] * nm,
        scratch_shapes=[sem(nm, N_CHIP), sem(nm, N_CHIP)],
        compiler_params=_params(),
    )(*grads)


def _rs2_chips(grads, halves):
    mats = [BIG[m] for m in EARLY]
    nm = len(mats)

    def body(*refs):
        gin, hin = refs[:nm], refs[nm:2 * nm]
        own, got = refs[2 * nm:3 * nm], refs[3 * nm:4 * nm]
        send, recv, lsem = refs[4 * nm:]
        x, y, c, chips = _place()
        s_me = 2 * x + y
        for m, (_, kind, shape) in enumerate(mats):
            pr, pc = _piece(kind, shape)

            def scoped(a, b, m=m, kind=kind, shape=shape, pr=pr):
                loads = [pltpu.make_async_copy(_piece_at(gin[m], kind, shape, s, c), a.at[s], lsem.at[s])
                         for s in range(N_CHIP)]
                loads.append(pltpu.make_async_copy(hin[m], b, lsem.at[N_CHIP]))
                for cp in loads:
                    cp.start()
                for cp in loads:
                    cp.wait()
                step = _add_rows(pr)
                for s in range(N_CHIP):
                    def add(i, _, s=s):
                        r = pl.ds(pl.multiple_of(i * step, 16), step)
                        a[s, r, :] = (a[s, r, :].astype(F32) + b[s, r, :].astype(F32)).astype(BF16)
                        return 0

                    lax.fori_loop(0, pr // step, add, 0)
                waits = []
                for j, (cx, cy) in enumerate(chips):
                    cp = pltpu.make_async_remote_copy(src_ref=a.at[2 * cx + cy], dst_ref=got[m].at[j], send_sem=send.at[m, j],
                                                      recv_sem=recv.at[m, j], device_id=(cx, cy, c), device_id_type=MESH)
                    cp.start()
                    waits.append(cp.wait_send)
                cp = pltpu.make_async_copy(a.at[s_me], own[m], lsem.at[N_CHIP + 1])
                cp.start()
                waits.append(cp.wait)
                for w in waits:
                    w()

            pl.run_scoped(scoped, pltpu.VMEM((N_CHIP, pr, pc), BF16), pltpu.VMEM((N_CHIP, pr, pc), BF16))
        for m in range(nm):
            for j, (cx, cy) in enumerate(chips):
                pltpu.make_async_remote_copy(src_ref=got[m].at[j], dst_ref=got[m].at[j], send_sem=send.at[m, j],
                                             recv_sem=recv.at[m, j], device_id=(cx, cy, c), device_id_type=MESH).wait_recv()

    sem = lambda *s: pltpu.SemaphoreType.DMA(s)
    pieces = [_piece(kind, shape) for _, kind, shape in mats]
    return pl.pallas_call(
        body, name="rs2_chips",
        out_shape=[jax.ShapeDtypeStruct(p, BF16) for p in pieces] + [jax.ShapeDtypeStruct((3,) + p, BF16) for p in pieces],
        in_specs=[ANY] * (2 * nm), out_specs=[ANY] * (2 * nm),
        scratch_shapes=[sem(nm, 3), sem(nm, 3), sem(N_CHIP + 2)],
        compiler_params=_params(),
    )(*grads, *halves)


def _rs3_finish(own, got):
    nm = len(BIG)

    def body(*refs):
        oin, gin = refs[:nm], refs[nm:2 * nm]
        outs = refs[2 * nm:3 * nm]
        send, recv, lsem = refs[3 * nm:]
        x, y, c, _ = _place()
        for m, (_, kind, shape) in enumerate(BIG):
            pr, pc = _piece(kind, shape)
            ng = got[m].shape[0]

            def scoped(a, g, f, m=m, pr=pr, ng=ng, kind=kind, shape=shape):
                mine = _piece_at(oin[m], kind, shape, 2 * x + y, c) if m in DIRECT else oin[m]
                loads = [pltpu.make_async_copy(mine, a, lsem.at[0]), pltpu.make_async_copy(gin[m], g, lsem.at[1])]
                for cp in loads:
                    cp.start()
                for cp in loads:
                    cp.wait()
                step = _add_rows(pr)

                def add(i, _):
                    r = pl.ds(pl.multiple_of(i * step, 16), step)
                    acc = a[r, :].astype(F32)
                    for q in range(ng):
                        acc = acc + g[q, r, :].astype(F32)
                    f[r, :] = acc
                    return 0

                lax.fori_loop(0, pr // step, add, 0)
                dst = outs[m].at[pl.ds(pl.multiple_of(c * pr, 8), pr), :]
                local = pltpu.make_async_copy(f, dst, lsem.at[2])
                local.start()
                cp = pltpu.make_async_remote_copy(src_ref=f, dst_ref=dst, send_sem=send.at[m], recv_sem=recv.at[m],
                                                  device_id=(x, y, 1 - c), device_id_type=MESH)
                cp.start()
                cp.wait_send()
                local.wait()

            pl.run_scoped(scoped, pltpu.VMEM((pr, pc), BF16), pltpu.VMEM((ng, pr, pc), BF16), pltpu.VMEM((pr, pc), F32))
        for m, (_, kind, shape) in enumerate(BIG):
            pr, pc = _piece(kind, shape)
            dst = outs[m].at[pl.ds(pl.multiple_of((1 - c) * pr, 8), pr), :]
            pltpu.make_async_remote_copy(src_ref=dst, dst_ref=dst, send_sem=send.at[m], recv_sem=recv.at[m],
                                         device_id=(x, y, 1 - c), device_id_type=MESH).wait_recv()

    sem = lambda *s: pltpu.SemaphoreType.DMA(s)
    pieces = [_piece(kind, shape) for _, kind, shape in BIG]
    return pl.pallas_call(
        body, name="rs3_finish",
        out_shape=[jax.ShapeDtypeStruct((2 * pr, pc), F32) for pr, pc in pieces],
        in_specs=[ANY] * (2 * nm), out_specs=[ANY] * nm,
        scratch_shapes=[sem(nm), sem(nm), sem(3)],
        compiler_params=_params(),
    )(*own, *got)


def _cond_fwd(c_all, w_shard, b_shard):
    def body(c_ref, w_ref, b_ref, act_ref, mod_ref):
        cv = c_ref[...]
        act = cv * _sig(cv)
        act_ref[...] = act
        mod_ref[...] = _dot(act.astype(BF16), w_ref[...].astype(BF16)) + b_ref[...]

    return pl.pallas_call(
        body, name="cond_fwd",
        out_shape=[jax.ShapeDtypeStruct(c_all.shape, F32), jax.ShapeDtypeStruct((c_all.shape[0], w_shard.shape[1]), F32)],
        compiler_params=_params(),
    )(c_all, w_shard, b_shard)


def _cond_bwd(act_t, dmod_shard):
    k, n = act_t.shape[0], dmod_shard.shape[1]

    def body(a_ref, d_ref, o_ref):
        acc = a_ref[:, 0:1] * d_ref[0:1, :]
        for e in range(1, N_DEV):
            acc += a_ref[:, e:e + 1] * d_ref[e:e + 1, :]
        o_ref[...] = acc

    tr = 256
    return pl.pallas_call(
        body, name="cond_bwd", grid=(k // tr,),
        in_specs=[pl.BlockSpec((tr, N_DEV), lambda i: (i, 0)), _full(dmod_shard)],
        out_specs=pl.BlockSpec((tr, n), lambda i: (i, 0)),
        out_shape=jax.ShapeDtypeStruct((k, n), F32),
        compiler_params=_params(("arbitrary",)),
    )(act_t, dmod_shard)


RAW_ROWS = 8


def _allreduce_small(pack):
    rows = PACK_ROWS // N_DEV

    def body(x_ref, sum_ref, raw_ref, buf, s1, r1, s2, r2, s3, r3):
        x, y, c, _ = _place()
        me = 4 * x + 2 * y + c
        peers = []
        for r in range(1, N_DEV):
            tx, ty, tc = (1 - x if r & 4 else x), (1 - y if r & 2 else y), (1 - c if r & 1 else c)
            peers.append(((tx, ty, tc), 4 * tx + 2 * ty + tc))
        chunk = lambda ref, d: ref.at[pl.ds(pl.multiple_of(d * rows, 8), rows), :]
        mine_raw = raw_ref.at[pl.ds(pl.multiple_of(me * RAW_ROWS, 8), RAW_ROWS), :]
        first = []
        for q, (dev, pd) in enumerate(peers):
            first.append(pltpu.make_async_remote_copy(src_ref=chunk(x_ref, pd), dst_ref=buf.at[q], send_sem=s1.at[q],
                                                      recv_sem=r1.at[q], device_id=dev, device_id_type=MESH))
            first.append(pltpu.make_async_remote_copy(src_ref=x_ref.at[0:RAW_ROWS, :], dst_ref=mine_raw, send_sem=s3.at[q],
                                                      recv_sem=r3.at[q], device_id=dev, device_id_type=MESH))
        for cp in first:
            cp.start()
        raw_ref[pl.ds(pl.multiple_of(me * RAW_ROWS, 8), RAW_ROWS), :] = x_ref[0:RAW_ROWS, :]
        for q, (dev, pd) in enumerate(peers):
            first[2 * q].wait()
        acc = x_ref[pl.ds(pl.multiple_of(me * rows, 8), rows), :]
        for q in range(N_DEV - 1):
            acc = acc + buf[q]
        sum_ref[pl.ds(pl.multiple_of(me * rows, 8), rows), :] = acc
        second = [pltpu.make_async_remote_copy(src_ref=chunk(sum_ref, me), dst_ref=chunk(sum_ref, me), send_sem=s2.at[q],
                                               recv_sem=r2.at[q], device_id=dev, device_id_type=MESH)
                  for q, (dev, pd) in enumerate(peers)]
        for cp in second:
            cp.start()
        for q, (dev, pd) in enumerate(peers):
            pltpu.make_async_remote_copy(src_ref=chunk(sum_ref, pd), dst_ref=chunk(sum_ref, pd), send_sem=s2.at[q],
                                         recv_sem=r2.at[q], device_id=dev, device_id_type=MESH).wait()
            pltpu.make_async_remote_copy(src_ref=x_ref.at[0:RAW_ROWS, :],
                                         dst_ref=raw_ref.at[pl.ds(pl.multiple_of(pd * RAW_ROWS, 8), RAW_ROWS), :],
                                         send_sem=s3.at[q], recv_sem=r3.at[q], device_id=dev, device_id_type=MESH).wait()

    sem = pltpu.SemaphoreType.DMA((N_DEV - 1,))
    return pl.pallas_call(
        body, name="allreduce_small",
        out_shape=[jax.ShapeDtypeStruct((PACK_ROWS, PACK_COLS), F32), jax.ShapeDtypeStruct((N_DEV * RAW_ROWS, PACK_COLS), F32)],
        in_specs=[pl.BlockSpec(memory_space=pltpu.VMEM)],
        out_specs=[pl.BlockSpec(memory_space=pltpu.VMEM)] * 2,
        scratch_shapes=[pltpu.VMEM((N_DEV - 1, rows, PACK_COLS), F32), sem, sem, sem, sem, sem, sem],
        compiler_params=_params(),
    )(pack)


def _adamw(name, w, g, m, v):
    r, cc = w.shape
    tr = r
    for cand in (256, 128, 64, 32, 16, 8):
        if r % cand == 0:
            tr = cand
            break
    bc1 = 1.0 - ADAM_B1 ** ADAM_STEP
    bc2 = 1.0 - ADAM_B2 ** ADAM_STEP

    def body(w_ref, g_ref, m_ref, v_ref, d_ref, nm_ref, nv_ref):
        gv = g_ref[...]
        nm = ADAM_B1 * m_ref[...] + (1.0 - ADAM_B1) * gv
        nv = ADAM_B2 * v_ref[...] + (1.0 - ADAM_B2) * (gv * gv)
        nm_ref[...] = nm
        nv_ref[...] = nv
        d_ref[...] = -ADAM_LR * ((nm / bc1) / (jnp.sqrt(nv / bc2) + ADAM_EPS) + ADAM_WD * w_ref[...])

    spec = pl.BlockSpec((tr, cc), lambda i: (i, 0))
    return pl.pallas_call(
        body, name=name, grid=(r // tr,), in_specs=[spec] * 4, out_specs=[spec] * 3,
        out_shape=[jax.ShapeDtypeStruct((r, cc), F32)] * 3, compiler_params=_params(("arbitrary",)),
    )(w, g, m, v)


def _adamw_small(ws, gs, ms, vs):
    n = len(ws)
    bc1 = 1.0 - ADAM_B1 ** ADAM_STEP
    bc2 = 1.0 - ADAM_B2 ** ADAM_STEP

    def body(*refs):
        w, g, m, v = (refs[k * n:(k + 1) * n] for k in range(4))
        d, nm, nv = (refs[(4 + k) * n:(5 + k) * n] for k in range(3))
        for i in range(n):
            gv = g[i][...]
            m1 = ADAM_B1 * m[i][...] + (1.0 - ADAM_B1) * gv
            v1 = ADAM_B2 * v[i][...] + (1.0 - ADAM_B2) * (gv * gv)
            nm[i][...] = m1
            nv[i][...] = v1
            d[i][...] = -ADAM_LR * ((m1 / bc1) / (jnp.sqrt(v1 / bc2) + ADAM_EPS) + ADAM_WD * w[i][...])

    shapes = [jax.ShapeDtypeStruct(x.shape, F32) for x in ws]
    res = pl.pallas_call(body, name="adamw_small", out_shape=shapes * 3, compiler_params=_params())(*ws, *gs, *ms, *vs)
    return res[:n], res[n:2 * n], res[2 * n:]


def _pack(fields, layout):
    parts = [fields[name].reshape(-1).astype(F32) if name in fields else jnp.zeros((n,), F32) for name, n in layout]
    used = sum(n for _, n in layout)
    parts.append(jnp.zeros((PACK_ROWS * PACK_COLS - used,), F32))
    return jnp.concatenate(parts).reshape(PACK_ROWS, PACK_COLS)


def _unpack(flat, layout):
    flat = flat.reshape(-1)
    out, o = {}, 0
    for name, n in layout:
        out[name] = flat[o:o + n]
        o += n
    return out


def kernel(x, c, w_cond, b_cond, w_in, b_in, ssm_lambda_re, ssm_lambda_im, ssm_log_dt, ssm_b_re, ssm_b_im, ssm_c_re, ssm_c_im, ssm_d, ssm_glu_w_a, ssm_glu_w_b, cv_dw_w, cv_dw_b, cv_ln_g, cv_ln_b, cv_w_pw, w_out, ln1_g, ln1_b, ffn_w_up, ffn_dw_w, ffn_dw_b, ffn_w_down, ln2_g, ln2_b, loss_target, m_w_cond, m_b_cond, m_w_in, m_b_in, m_ssm_lambda_re, m_ssm_lambda_im, m_ssm_log_dt, m_ssm_b_re, m_ssm_b_im, m_ssm_c_re, m_ssm_c_im, m_ssm_d, m_ssm_glu_w_a, m_ssm_glu_w_b, m_cv_dw_w, m_cv_dw_b, m_cv_ln_g, m_cv_ln_b, m_cv_w_pw, m_w_out, m_ln1_g, m_ln1_b, m_ffn_w_up, m_ffn_dw_w, m_ffn_dw_b, m_ffn_w_down, m_ln2_g, m_ln2_b, v_w_cond, v_b_cond, v_w_in, v_b_in, v_ssm_lambda_re, v_ssm_lambda_im, v_ssm_log_dt, v_ssm_b_re, v_ssm_b_im, v_ssm_c_re, v_ssm_c_im, v_ssm_d, v_ssm_glu_w_a, v_ssm_glu_w_b, v_cv_dw_w, v_cv_dw_b, v_cv_ln_g, v_cv_ln_b, v_cv_w_pw, v_w_out, v_ln1_g, v_ln1_b, v_ffn_w_up, v_ffn_dw_w, v_ffn_dw_b, v_ffn_w_down, v_ln2_g, v_ln2_b):
    given = locals()
    a = {n: given[n] for n in INPUTS}
    xi, yi, ci = lax.axis_index("x"), lax.axis_index("y"), lax.axis_index("c")
    s_me = 2 * xi + yi
    e_me = 4 * xi + 2 * yi + ci

    first = jnp.concatenate([
        jnp.concatenate([a["c"], jnp.zeros((7, D_MODEL), F32)], axis=0),
        jnp.concatenate([a["cv_dw_w"].reshape(-1), a["ffn_dw_w"].reshape(-1)]).reshape(8, D_MODEL)], axis=0)
    full, shards, first_all = _gather_weights([a[n][0] for n, _, _ in BIG], first)
    wb = dict(zip([n for n, _, _ in BIG], full))
    first_all = first_all.reshape(N_DEV, 16, D_MODEL)
    c_all = first_all[:, 0, :]
    dw_all = first_all[0::2, 8:, :].reshape(N_CHIP, 8 * D_MODEL)
    n_cv = CONV_KERNEL * CONV_WIDTH // N_CHIP
    cv_dw_full = dw_all[:, :n_cv].reshape(N_CHIP, CONV_KERNEL, CONV_WIDTH // N_CHIP).transpose(1, 0, 2) \
        .reshape(CONV_KERNEL, CONV_WIDTH)
    ffn_dw_full = dw_all[:, n_cv:].reshape(N_CHIP, FFN_KERNEL, 2 * FFN_HIDDEN // N_CHIP).transpose(1, 0, 2) \
        .reshape(FFN_KERNEL, 2 * FFN_HIDDEN)
    ncols = N_COND * D_MODEL // N_CHIP
    b_cond_shard = lax.dynamic_slice(a["b_cond"], (0, s_me * ncols), (1, ncols))
    c_act_all, modp = _cond_fwd(c_all, a["w_cond"][0], b_cond_shard)
    modp_all = _allgather("gather_mod", modp).reshape(N_DEV, N_DEV, ncols)[0::2]
    mod_e = lax.dynamic_index_in_dim(modp_all, e_me, axis=1, keepdims=False).reshape(N_COND, D_MODEL)
    modv = jnp.concatenate([mod_e, jnp.zeros((2, D_MODEL), F32)], axis=0)

    sp = {n: a[n][0] for n in ("b_in", "ssm_lambda_re", "ssm_lambda_im", "ssm_log_dt", "ssm_b_re", "ssm_b_im",
                               "ssm_c_re", "ssm_c_im", "ssm_d", "cv_dw_b", "cv_ln_g", "cv_ln_b", "ln1_g", "ln1_b",
                               "ffn_dw_b", "ln2_g", "ln2_b")}
    sp["cv_dw_w"] = cv_dw_full
    sp["ffn_dw_w"] = ffn_dw_full
    gx, dbig, direct_got, small = _local_step(a["x"][0], a["loss_target"][0], modv, wb, shards, sp)

    tot_pack, raw_all = _allreduce_small(_pack(small, PACK))
    tot = _unpack(tot_pack, PACK)
    dmod_all = raw_all.reshape(N_DEV, RAW_ROWS * PACK_COLS)[:, 0:N_COND * D_MODEL]
    g_w_cond = _cond_bwd(c_act_all.T, lax.dynamic_slice(dmod_all, (0, s_me * ncols), (N_DEV, ncols)))

    glist = [dbig[m] for m in EARLY]
    halves = _rs1_sibling(glist)
    r2 = _rs2_chips(glist, halves)
    gsh = _rs3_finish(list(r2[:len(EARLY)]) + [dbig[m] for m in DIRECT], list(r2[len(EARLY):]) + direct_got)

    grads = {"w_cond": g_w_cond[None], "b_cond": tot["dmod"].reshape(1, -1)}
    for (n, kind, shape), g in zip(BIG, gsh):
        grads[n] = g.reshape(a[n].shape)
    for n in ("b_in", "ssm_lambda_re", "ssm_lambda_im", "ssm_log_dt", "ssm_b_re", "ssm_b_im", "ssm_c_re", "ssm_c_im",
              "ssm_d", "cv_dw_b", "cv_ln_g", "cv_ln_b", "ln1_g", "ln1_b", "ffn_dw_b", "ln2_g", "ln2_b"):
        grads[n] = tot[n].reshape(a[n].shape)
    wcv = CONV_WIDTH // N_CHIP
    grads["cv_dw_w"] = lax.dynamic_slice(tot["cv_dw_w"].reshape(CONV_KERNEL, CONV_WIDTH), (0, s_me * wcv),
                                         (CONV_KERNEL, wcv)).reshape(a["cv_dw_w"].shape)
    wff = 2 * FFN_HIDDEN // N_CHIP
    grads["ffn_dw_w"] = lax.dynamic_slice(tot["ffn_dw_w"].reshape(FFN_KERNEL, 2 * FFN_HIDDEN), (0, s_me * wff),
                                          (FFN_KERNEL, wff)).reshape(a["ffn_dw_w"].shape)

    delta, new_m, new_v = {}, {}, {}
    for n in ["w_cond"] + [n for n, _, _ in BIG]:
        d, nm_, nv_ = _adamw("adamw_" + n, a[n][0], grads[n][0], a["m_" + n][0], a["v_" + n][0])
        delta[n], new_m[n], new_v[n] = d[None], nm_[None], nv_[None]
    upd = [n for n in WEIGHTS if n not in delta]
    two_d = lambda t: t.reshape(-1, t.shape[-1])
    outs = _adamw_small([two_d(a[n]) for n in upd], [two_d(grads[n]) for n in upd],
                        [two_d(a["m_" + n]) for n in upd], [two_d(a["v_" + n]) for n in upd])
    for dst, vals in zip((delta, new_m, new_v), outs):
        for n, val in zip(upd, vals):
            dst[n] = val.reshape(a[n].shape)

    loss = tot["loss"].reshape(())
    return (loss, gx[None], *[grads[n] for n in WEIGHTS], *[delta[n] for n in WEIGHTS],
            *[new_m[n] for n in WEIGHTS], *[new_v[n] for n in WEIGHTS])
```

```python
import functools
import math

import jax
import jax.numpy as jnp
from jax import lax
from jax.experimental import pallas as pl
from jax.experimental.pallas import tpu as pltpu

F32 = jnp.float32
BF16 = jnp.bfloat16

D_MODEL = 1024
SSM_WIDTH = 512
SSM_GROUP = 16
SSM_GROUPS = 32
SSM_STATE = 64
CONV_WIDTH = 512
CONV_KERNEL = 31
FFN_HIDDEN = 2816
FFN_KERNEL = 3
IN_PROJ_WIDTH = 3584
N_COND = 6
ALPHA = 2.0 ** 0.25
LN_EPS = 1e-5
ADAM_LR, ADAM_B1, ADAM_B2, ADAM_EPS, ADAM_WD, ADAM_STEP = 0.001, 0.9, 0.999, 1e-08, 0.01, 10

N_DEV = 8
N_CHIP = 4
LANES = 128
SSM_CHUNK = 16
LANE_GROUPS = LANES // SSM_GROUP
N_LANE_BLOCKS = SSM_WIDTH // LANES
STATE_COLS = LANE_GROUPS * SSM_STATE
CHUNK_COLS = SSM_CHUNK * LANES
CONV_HALO = 32
VMEM_LIMIT = 56 * 1024 * 1024
MESH = pl.DeviceIdType.MESH

BIG = (
    ("w_in", "col", (D_MODEL, IN_PROJ_WIDTH)),
    ("ssm_glu_w_a", "col", (SSM_WIDTH, D_MODEL)),
    ("ssm_glu_w_b", "col", (SSM_WIDTH, D_MODEL)),
    ("cv_w_pw", "col", (CONV_WIDTH, D_MODEL)),
    ("w_out", "row", (D_MODEL, D_MODEL)),
    ("ffn_w_up", "col", (D_MODEL, 2 * FFN_HIDDEN)),
    ("ffn_w_down", "row", (FFN_HIDDEN, D_MODEL)),
)

EARLY = (0,)
MID = (1, 2, 3, 4)
LATE = (5, 6)
DIRECT = MID + LATE

WEIGHTS = ['w_cond', 'b_cond', 'w_in', 'b_in', 'ssm_lambda_re', 'ssm_lambda_im', 'ssm_log_dt', 'ssm_b_re', 'ssm_b_im',
           'ssm_c_re', 'ssm_c_im', 'ssm_d', 'ssm_glu_w_a', 'ssm_glu_w_b', 'cv_dw_w', 'cv_dw_b', 'cv_ln_g', 'cv_ln_b',
           'cv_w_pw', 'w_out', 'ln1_g', 'ln1_b', 'ffn_w_up', 'ffn_dw_w', 'ffn_dw_b', 'ffn_w_down', 'ln2_g', 'ln2_b']
INPUTS = ['x', 'c'] + WEIGHTS + ['loss_target'] + ['m_' + n for n in WEIGHTS] + ['v_' + n for n in WEIGHTS]

PACK = (
    ("dmod", N_COND * D_MODEL), ("c_act", D_MODEL), ("b_in", IN_PROJ_WIDTH),
    ("ssm_lambda_re", SSM_GROUPS * SSM_STATE), ("ssm_lambda_im", SSM_GROUPS * SSM_STATE), ("ssm_log_dt", SSM_GROUPS),
    ("ssm_b_re", SSM_GROUPS * SSM_STATE * SSM_GROUP), ("ssm_b_im", SSM_GROUPS * SSM_STATE * SSM_GROUP),
    ("ssm_c_re", SSM_GROUPS * SSM_STATE * SSM_GROUP), ("ssm_c_im", SSM_GROUPS * SSM_STATE * SSM_GROUP),
    ("ssm_d", SSM_GROUPS * SSM_GROUP), ("cv_dw_w", CONV_KERNEL * CONV_WIDTH), ("cv_dw_b", CONV_WIDTH),
    ("cv_ln_g", CONV_WIDTH), ("cv_ln_b", CONV_WIDTH), ("ln1_g", D_MODEL), ("ln1_b", D_MODEL),
    ("ffn_dw_w", FFN_KERNEL * 2 * FFN_HIDDEN), ("ffn_dw_b", 2 * FFN_HIDDEN), ("ln2_g", D_MODEL), ("ln2_b", D_MODEL),
    ("loss", 1),
)
PACK_COLS = 1024
PACK_ROWS = 192
assert sum(n for _, n in PACK) <= PACK_ROWS * PACK_COLS


def _params(sem=None, **kw):
    return pltpu.CompilerParams(dimension_semantics=sem, vmem_limit_bytes=VMEM_LIMIT, **kw)


def _ln_stats(x):
    mu = jnp.mean(x, axis=-1, keepdims=True)
    xc = x - mu
    var = jnp.mean(xc * xc, axis=-1, keepdims=True)
    rstd = lax.rsqrt(var + LN_EPS)
    return xc * rstd, rstd


def _ln_bwd(dxhat, xhat, rstd):
    m1 = jnp.mean(dxhat, axis=-1, keepdims=True)
    m2 = jnp.mean(dxhat * xhat, axis=-1, keepdims=True)
    return rstd * (dxhat - m1 - xhat * m2)


def _sig(x):
    return 1.0 / (1.0 + jnp.exp(-x))


def _gelu(x):
    return 0.5 * x * (1.0 + lax.erf(x * (1.0 / math.sqrt(2.0))))


def _dgelu(x):
    return 0.5 * (1.0 + lax.erf(x * (1.0 / math.sqrt(2.0)))) + x * jnp.exp(-0.5 * x * x) * (1.0 / math.sqrt(2.0 * math.pi))


def _gelu_and_grad(x):
    er = lax.erf(x * (1.0 / math.sqrt(2.0)))
    cdf = 0.5 * (1.0 + er)
    return x * cdf, cdf + x * jnp.exp(-0.5 * x * x) * (1.0 / math.sqrt(2.0 * math.pi))


def _colsum(a):
    return jnp.sum(a, axis=0, keepdims=True)


def _fill_rotations(buf, rot, rows):
    for r in range(1, 8):
        rot[r - 1] = buf[pl.ds(r, rows), :]


def _rows_at(buf, rot, offset, tb):
    q, r = divmod(offset, 8)
    if r == 0:
        return buf[pl.ds(8 * q, tb), :]
    return rot[r - 1, pl.ds(8 * q, tb), :]


def _dot(a, b):
    return jnp.dot(a, b, preferred_element_type=F32)


def _dot_nt(a, b):
    return lax.dot_general(a, b, (((1,), (1,)), ((), ())), preferred_element_type=F32)


def _dot_tn(a, b):
    return lax.dot_general(a, b, (((0,), (0,)), ((), ())), preferred_element_type=F32)


def _load_once(src, dst, sem):
    cp = pltpu.make_async_copy(src, dst, sem)
    cp.start()
    cp.wait()


def _full(a):
    nd = a.ndim
    return pl.BlockSpec(a.shape, lambda *_: (0,) * nd)


ANY = pl.BlockSpec(memory_space=pl.ANY)


def _place():
    x, y, c = lax.axis_index("x"), lax.axis_index("y"), lax.axis_index("c")
    chips = [(1 - x, y), (x, 1 - y), (1 - x, 1 - y)]
    return x, y, c, chips


def _piece(kind, shape):
    r, cc = shape
    return (r // 2, cc // N_CHIP) if kind == "col" else (r // (2 * N_CHIP), cc)


def _piece_at(ref, kind, shape, s, k):
    pr, pc = _piece(kind, shape)
    if kind == "col":
        return ref.at[pl.ds(k * pr, pr), pl.ds(pl.multiple_of(s * pc, LANES), pc)]
    return ref.at[pl.ds(pl.multiple_of((2 * s + k) * pr, 16), pr), :]


def _gather_start(idx, sh, full, send, recv):
    x, y, c, chips = _place()
    for i, m in enumerate(idx):
        _, kind, shape = BIG[m]
        pr, _ = _piece(kind, shape)
        for j, chip in enumerate(chips):
            pltpu.make_async_remote_copy(
                src_ref=sh[i].at[pl.ds(pl.multiple_of(c * pr, 16), pr), :], dst_ref=_piece_at(full[i], kind, shape, 2 * x + y, c),
                send_sem=send.at[i, j], recv_sem=recv.at[i, j], device_id=(*chip, c), device_id_type=MESH).start()


def _gather_finish(idx, sh, full, send, recv, fsend, frecv):
    x, y, c, chips = _place()
    sibling = (x, y, 1 - c)
    waits = []
    for i, m in enumerate(idx):
        _, kind, shape = BIG[m]
        pr, _ = _piece(kind, shape)
        for j, (cx, cy) in enumerate(chips):
            got = _piece_at(full[i], kind, shape, 2 * cx + cy, c)
            first = pltpu.make_async_remote_copy(
                src_ref=sh[i].at[pl.ds(pl.multiple_of(c * pr, 16), pr), :], dst_ref=got, send_sem=send.at[i, j],
                recv_sem=recv.at[i, j], device_id=(cx, cy, c), device_id_type=MESH)
            first.wait_recv()
            fwd = pltpu.make_async_remote_copy(src_ref=got, dst_ref=got, send_sem=fsend.at[i, j], recv_sem=frecv.at[i, j],
                                               device_id=sibling, device_id_type=MESH)
            fwd.start()
            waits += [first.wait_send, fwd.wait_send]
    for i, m in enumerate(idx):
        _, kind, shape = BIG[m]
        for j, (cx, cy) in enumerate(chips):
            got = _piece_at(full[i], kind, shape, 2 * cx + cy, 1 - c)
            pltpu.make_async_remote_copy(src_ref=got, dst_ref=got, send_sem=fsend.at[i, j], recv_sem=frecv.at[i, j],
                                         device_id=sibling, device_id_type=MESH).wait_recv()
    for w in waits:
        w()


def _scatter(idx, dw, got, send, recv):
    x, y, c, _ = _place()
    cps = []
    for i, m in enumerate(idx):
        _, kind, shape = BIG[m]
        for r in range(1, N_DEV):
            tx, ty, tc = (1 - x if r & 4 else x), (1 - y if r & 2 else y), (1 - c if r & 1 else c)
            cps.append(pltpu.make_async_remote_copy(
                src_ref=_piece_at(dw[i], kind, shape, 2 * tx + ty, tc), dst_ref=got[i].at[r - 1],
                send_sem=send.at[i, r - 1], recv_sem=recv.at[i, r - 1], device_id=(tx, ty, tc), device_id_type=MESH))
    return cps


def _f1_inproj(x, modv, b_in, w_in, mid_sh, mid_full, tb):
    t = x.shape[0]
    nt = t // tb
    nl = len(MID)
    chunks = [(j * 512, 512) for j in range(IN_PROJ_WIDTH // 512)]

    def body(x_ref, modv_ref, b_ref, w_hbm, *rest):
        sh, full = rest[:nl], rest[2 * nl:3 * nl]
        u4_ref, prest_ref, h_ref, w_v, sem, send, recv, fsend, frecv = rest[3 * nl:]

        @pl.when(pl.program_id(0) == 0)
        def _():
            _gather_start(MID, sh, full, send, recv)
            _load_once(w_hbm, w_v, sem)

        xn, _ = _ln_stats(x_ref[...])
        h = (xn * (1.0 + modv_ref[1:2, :]) + modv_ref[0:1, :]).astype(BF16)
        h_ref[...] = h
        for c0, cw in chunks:
            p = _dot(h, w_v[:, c0:c0 + cw]) + b_ref[:, c0:c0 + cw]
            if c0 == 0:
                for b in range(N_LANE_BLOCKS):
                    u4_ref[b] = p[:, b * LANES:(b + 1) * LANES]
            else:
                prest_ref[:, c0 - SSM_WIDTH:c0 - SSM_WIDTH + cw] = p

        @pl.when(pl.program_id(0) == nt - 1)
        def _():
            _gather_finish(MID, sh, full, send, recv, fsend, frecv)

    gsem = pltpu.SemaphoreType.DMA((nl, 3))
    return pl.pallas_call(
        body, name="f1_inproj", grid=(nt,),
        in_specs=[pl.BlockSpec((tb, D_MODEL), lambda i: (i, 0)), _full(modv), _full(b_in), ANY] + [ANY] * (2 * nl),
        out_specs=[ANY] * nl + [pl.BlockSpec((N_LANE_BLOCKS, tb, LANES), lambda i: (0, i, 0)),
                                pl.BlockSpec((tb, IN_PROJ_WIDTH - SSM_WIDTH), lambda i: (i, 0)),
                                pl.BlockSpec((tb, D_MODEL), lambda i: (i, 0))],
        input_output_aliases={4 + nl + k: k for k in range(nl)},
        out_shape=[jax.ShapeDtypeStruct(f.shape, f.dtype) for f in mid_full]
        + [jax.ShapeDtypeStruct((N_LANE_BLOCKS, t, LANES), F32),
                   jax.ShapeDtypeStruct((t, IN_PROJ_WIDTH - SSM_WIDTH), F32),
                   jax.ShapeDtypeStruct((t, D_MODEL), BF16)],
        scratch_shapes=[pltpu.VMEM(w_in.shape, BF16), pltpu.SemaphoreType.DMA, gsem, gsem, gsem, gsem],
        compiler_params=_params(("arbitrary",)),
    )(x, modv, b_in, w_in, *mid_sh, *mid_full)


TAP_GROUPS = 8


def _dot_f32(a, b, dims):
    return lax.dot_general(a, b, (dims, ((), ())), precision=lax.Precision.HIGHEST, preferred_element_type=F32)


def _taps_fwd(car, cai, bt_r, bt_i):
    el, g, p, n = SSM_CHUNK, SSM_GROUPS, SSM_GROUP, SSM_STATE

    def body(ar_ref, ai_ref, br_ref, bi_ref, o_ref):
        for gl in range(TAP_GROUPS):
            a_r = jnp.concatenate([ar_ref[k, gl] for k in range(el)], axis=0)
            a_i = jnp.concatenate([ai_ref[k, gl] for k in range(el)], axis=0)
            o_ref[gl] = _dot_f32(br_ref[gl], a_r, ((1,), (1,))) - _dot_f32(bi_ref[gl], a_i, ((1,), (1,)))

    ablk = pl.BlockSpec((el + 1, TAP_GROUPS, p, n), lambda i: (0, i, 0, 0))
    bblk = pl.BlockSpec((TAP_GROUPS, p, n), lambda i: (i, 0, 0))
    return pl.pallas_call(
        body, name="s5_taps", grid=(g // TAP_GROUPS,), in_specs=[ablk, ablk, bblk, bblk],
        out_specs=pl.BlockSpec((TAP_GROUPS, p, el * p), lambda i: (i, 0, 0)),
        out_shape=jax.ShapeDtypeStruct((g, p, el * p), F32), compiler_params=_params(("arbitrary",)),
    )(car, cai, bt_r, bt_i)


def _taps_bwd(dr, car, cai, bt_r, bt_i):
    el, g, p, n = SSM_CHUNK, SSM_GROUPS, SSM_GROUP, SSM_STATE

    def body(dr_ref, ar_ref, ai_ref, br_ref, bi_ref, dar_ref, dai_ref, dbr_ref, dbi_ref):
        for gl in range(TAP_GROUPS):
            dv = dr_ref[gl]
            a_r = jnp.concatenate([ar_ref[k, gl] for k in range(el)], axis=0)
            a_i = jnp.concatenate([ai_ref[k, gl] for k in range(el)], axis=0)
            dbr_ref[gl] = _dot_f32(dv, a_r, ((1,), (0,)))
            dbi_ref[gl] = -_dot_f32(dv, a_i, ((1,), (0,)))
            da_r = _dot_f32(dv, br_ref[gl], ((0,), (0,)))
            da_i = -_dot_f32(dv, bi_ref[gl], ((0,), (0,)))
            for k in range(el):
                dar_ref[k, gl] = da_r[k * p:(k + 1) * p, :]
                dai_ref[k, gl] = da_i[k * p:(k + 1) * p, :]
            dar_ref[el, gl] = jnp.zeros((p, n), F32)
            dai_ref[el, gl] = jnp.zeros((p, n), F32)

    ablk = pl.BlockSpec((el + 1, TAP_GROUPS, p, n), lambda i: (0, i, 0, 0))
    bblk = pl.BlockSpec((TAP_GROUPS, p, n), lambda i: (i, 0, 0))
    return pl.pallas_call(
        body, name="s5_taps_bwd", grid=(g // TAP_GROUPS,),
        in_specs=[pl.BlockSpec((TAP_GROUPS, p, el * p), lambda i: (i, 0, 0)), ablk, ablk, bblk, bblk],
        out_specs=[ablk, ablk, bblk, bblk],
        out_shape=[jax.ShapeDtypeStruct(car.shape, F32), jax.ShapeDtypeStruct(car.shape, F32),
                   jax.ShapeDtypeStruct(bt_r.shape, F32), jax.ShapeDtypeStruct(bt_r.shape, F32)],
        compiler_params=_params(("arbitrary",)),
    )(dr, car, cai, bt_r, bt_i)


@jax.custom_vjp
def _taps(car, cai, bt_r, bt_i):
    return _taps_fwd(car, cai, bt_r, bt_i)


_taps.defvjp(lambda *ops: (_taps_fwd(*ops), ops), lambda ops, dr: _taps_bwd(dr, *ops))


def _s5_build(lam_re, lam_im, log_dt, b_re, b_im, c_re, c_im, d):
    el, g, n, p, nb = SSM_CHUNK, SSM_GROUPS, SSM_STATE, SSM_GROUP, N_LANE_BLOCKS
    lr = jnp.minimum(lam_re, -1e-4)
    li = lam_im
    dt = jnp.exp(log_dt)[:, None]
    mag = jnp.exp(lr * dt)
    ang = li * dt
    lbr, lbi = mag * jnp.cos(ang), mag * jnp.sin(ang)
    num_r, num_i = lbr - 1.0, lbi
    den = lr * lr + li * li
    coef_r = (num_r * lr + num_i * li) / den
    coef_i = (num_i * lr - num_r * li) / den
    bbar_r = coef_r[..., None] * b_re - coef_i[..., None] * b_im
    bbar_i = coef_r[..., None] * b_im + coef_i[..., None] * b_re
    k = jnp.arange(el + 1, dtype=F32)[:, None, None]
    pmag = jnp.exp(k * (lr * dt)[None])
    pr, pi = pmag * jnp.cos(k * ang[None]), pmag * jnp.sin(k * ang[None])
    car = c_re[None] * pr[:, :, None, :] - c_im[None] * pi[:, :, None, :]
    cai = c_re[None] * pi[:, :, None, :] + c_im[None] * pr[:, :, None, :]
    bt_r = bbar_r.transpose(0, 2, 1)
    bt_i = bbar_i.transpose(0, 2, 1)
    kern = _taps(car, cai, bt_r, bt_i).reshape(g, p, el, p).transpose(2, 0, 1, 3)
    kern = kern.at[0].add(jnp.eye(p, dtype=F32)[None] * d[:, None, :])
    bt_r, bt_i = bt_r[None], bt_i[None]
    kc = kern.reshape(el, g * p, p)
    rev = el - 1 - jnp.arange(el)
    qr, qi = pr[rev][:, :, None, :], pi[rev][:, :, None, :]
    sw_r = (qr * bt_r - qi * bt_i).reshape(el, g * p, n)
    sw_i = (qr * bt_i + qi * bt_r).reshape(el, g * p, n)
    sg_r = car[1:].reshape(el, g * p, n)
    sg_i = (-cai[1:]).reshape(el, g * p, n)
    a = jnp.stack([pr[el].reshape(nb, LANE_GROUPS * n), pi[el].reshape(nb, LANE_GROUPS * n)], axis=1)
    return kc, sw_r, sw_i, sg_r, sg_i, a


def _expand(src, reps):
    rows, w = src.shape
    cols = reps * w
    r = lax.broadcasted_iota(jnp.int32, (w, cols), 0)
    c = lax.broadcasted_iota(jnp.int32, (w, cols), 1)
    rep = (r == (c & (w - 1))).astype(BF16)
    out = _dot(src.astype(BF16), rep)
    rg = lax.broadcasted_iota(jnp.int32, (rows, cols), 0) // SSM_GROUP
    cg = lax.broadcasted_iota(jnp.int32, (rows, cols), 1) // w
    return jnp.where(rg == cg, out, 0.0).astype(BF16)


def _fold(x, w):
    rows, cols = x.shape
    rg = lax.broadcasted_iota(jnp.int32, (rows, cols), 0) // SSM_GROUP
    cg = lax.broadcasted_iota(jnp.int32, (rows, cols), 1) // w
    x = jnp.where(rg == cg, x, 0.0)
    while cols > LANES:
        x = x[:, :cols // 2] + x[:, cols // 2:]
        cols //= 2
    s = LANES // 2
    while s >= w:
        x = x + pltpu.roll(x, s, axis=1)
        s //= 2
    return x[:, :w]


def _build_maps(s_ref, dst):
    for j in range(SSM_CHUNK):
        dst[j * LANES:(j + 1) * LANES, :] = _expand(s_ref[j], LANE_GROUPS)


def _build_toeplitz(kc_ref, dst):
    dst[...] = jnp.zeros_like(dst)
    for d in range(SSM_CHUNK):
        blk = _expand(kc_ref[d], LANE_GROUPS)
        for ji in range(SSM_CHUNK - d):
            jo = ji + d
            dst[ji * LANES:(ji + 1) * LANES, jo * LANES:(jo + 1) * LANES] = blk


def _cblk(w):
    return pl.BlockSpec((SSM_CHUNK, LANES, w), lambda b: (0, b, 0))


def _tblk(t):
    return pl.BlockSpec((1, t, LANES), lambda b: (b, 0, 0))


def _load_chunks(ref, nc):
    return jnp.concatenate([ref[0, pl.ds(j, nc, stride=SSM_CHUNK), :] for j in range(SSM_CHUNK)], axis=-1).astype(BF16)


def _store_chunks(ref, val, nc):
    for j in range(SSM_CHUNK):
        ref[0, pl.ds(j, nc, stride=SSM_CHUNK), :] = val[:, j * LANES:(j + 1) * LANES]


def _s5a_state(u4, sw_r, sw_i, a8):
    nb, t, _ = u4.shape
    nc = t // SSM_CHUNK
    sc = STATE_COLS

    def body(u_ref, swr_ref, swi_ref, a_ref, hr_ref, hi_ref, w_s, xr_s, xi_s):
        u = _load_chunks(u_ref, nc)
        _build_maps(swr_ref, w_s)
        xr_s[...] = _dot(u, w_s[...])
        _build_maps(swi_ref, w_s)
        xi_s[...] = _dot(u, w_s[...])
        ar = a_ref[0, 0:1, :]
        ai = a_ref[0, 1:2, :]

        def step(c, carry):
            hr, hi = carry
            hr_ref[0, pl.ds(c, 1), :] = hr
            hi_ref[0, pl.ds(c, 1), :] = hi
            xr = xr_s[pl.ds(c, 1), :]
            xi = xi_s[pl.ds(c, 1), :]
            return ar * hr - ai * hi + xr, ar * hi + ai * hr + xi

        z = jnp.zeros((1, sc), F32)
        lax.fori_loop(0, nc, step, (z, z))

    return pl.pallas_call(
        body, name="s5a_state", grid=(nb,),
        in_specs=[_tblk(t), _cblk(SSM_STATE), _cblk(SSM_STATE),
                  pl.BlockSpec((1, 8, sc), lambda b: (b, 0, 0))],
        out_specs=[pl.BlockSpec((1, nc, sc), lambda b: (b, 0, 0))] * 2,
        out_shape=[jax.ShapeDtypeStruct((nb, nc, sc), F32)] * 2,
        scratch_shapes=[pltpu.VMEM((CHUNK_COLS, sc), BF16), pltpu.VMEM((nc, sc), F32), pltpu.VMEM((nc, sc), F32)],
        compiler_params=_params(("arbitrary",)),
    )(u4, sw_r, sw_i, a8)


def _s5b_out(u4, kc, sg_r, sg_i, hr, hi):
    nb, t, _ = u4.shape
    nc = t // SSM_CHUNK
    sc = STATE_COLS
    cw = 512

    def body(u_ref, kc_ref, sgr_ref, sgi_ref, hr_ref, hi_ref, y_ref, tm_s, gr_s, gi_s):
        _build_toeplitz(kc_ref, tm_s)
        _build_maps(sgr_ref, gr_s)
        _build_maps(sgi_ref, gi_s)
        u = _load_chunks(u_ref, nc)
        h_r = hr_ref[0].astype(BF16)
        h_i = hi_ref[0].astype(BF16)
        for j in range(CHUNK_COLS // cw):
            cs = slice(j * cw, (j + 1) * cw)
            y = _dot(u, tm_s[:, cs]) + _dot_nt(h_r, gr_s[cs, :]) + _dot_nt(h_i, gi_s[cs, :])
            for q in range(cw // LANES):
                step = j * (cw // LANES) + q
                y_ref[0, pl.ds(step, nc, stride=SSM_CHUNK), :] = y[:, q * LANES:(q + 1) * LANES]

    return pl.pallas_call(
        body, name="s5b_out", grid=(nb,),
        in_specs=[_tblk(t), _cblk(SSM_GROUP), _cblk(SSM_STATE),
                  _cblk(SSM_STATE), pl.BlockSpec((1, nc, sc), lambda b: (b, 0, 0)),
                  pl.BlockSpec((1, nc, sc), lambda b: (b, 0, 0))],
        out_specs=_tblk(t),
        out_shape=jax.ShapeDtypeStruct((nb, t, LANES), F32),
        scratch_shapes=[pltpu.VMEM((CHUNK_COLS, CHUNK_COLS), BF16), pltpu.VMEM((CHUNK_COLS, sc), BF16),
                        pltpu.VMEM((CHUNK_COLS, sc), BF16)],
        compiler_params=_params(("arbitrary",)),
    )(u4, kc, sg_r, sg_i, hr, hi)


def _f4_mixer(ys4, prest, x, modv, cvv, cw32, w_a, w_b, w_pw, w_out, late_sh, late_full, tb):
    t = x.shape[0]
    hb = tb // CONV_HALO
    nt = t // tb
    nl = len(LATE)

    def body(ys_ref, pr_ref, halo_ref, x_ref, modv_ref, cvv_ref, cw_ref, wa_ref, wb_ref, wpw_ref, wout_ref, *rest):
        sh, full = rest[:nl], rest[2 * nl:3 * nl]
        r1_ref, ya_ref, yb_ref, ycv_ref, vc_ref, yg_ref, vs_ref, mg_ref, vbuf, vrot, send, recv, fsend, frecv = rest[3 * nl:]
        i = pl.program_id(0)

        @pl.when(i == 0)
        def _():
            _gather_start(LATE, sh, full, send, recv)

        ys = jnp.concatenate([ys_ref[b] for b in range(N_LANE_BLOCKS)], axis=-1)
        yg = _gelu(ys).astype(BF16)
        yg_ref[...] = yg
        ya = _dot(yg, wa_ref[...])
        yb = _dot(yg, wb_ref[...])
        ya_ref[...] = ya.astype(BF16)
        yb_ref[...] = yb.astype(BF16)
        yssm = ya * _sig(yb)
        hv = halo_ref[:, 0:CONV_WIDTH] * _sig(halo_ref[:, CONV_WIDTH:2 * CONV_WIDTH])
        vbuf[0:CONV_HALO, :] = jnp.where(i == 0, 0.0, hv)
        vbuf[CONV_HALO:, :] = pr_ref[:, 0:CONV_WIDTH] * _sig(pr_ref[:, CONV_WIDTH:2 * CONV_WIDTH])
        _fill_rotations(vbuf, vrot, tb + CONV_HALO - 8)
        acc = jnp.zeros((tb, CONV_WIDTH), F32)
        for k in range(CONV_KERNEL):
            acc += _rows_at(vbuf, vrot, CONV_HALO - CONV_KERNEL + 1 + k, tb) * cw_ref[k:k + 1, :]
        vc = acc + cvv_ref[0:1, :]
        vc_ref[...] = vc
        xh, _ = _ln_stats(vc)
        vl = xh * cvv_ref[1:2, :] + cvv_ref[2:3, :]
        vs = (vl * _sig(vl)).astype(BF16)
        vs_ref[...] = vs
        ycv = _dot(vs, wpw_ref[...])
        ycv_ref[...] = ycv.astype(BF16)
        gs = pr_ref[:, 2 * CONV_WIDTH:2 * CONV_WIDTH + D_MODEL]
        gc = pr_ref[:, 2 * CONV_WIDTH + D_MODEL:]
        merged = (_sig(gs) * yssm + _sig(gc) * ycv).astype(BF16)
        mg_ref[...] = merged
        ym = _dot(merged, wout_ref[...])
        r1_ref[...] = ALPHA * x_ref[...] + modv_ref[2:3, :] * ym

        @pl.when(i == nt - 1)
        def _():
            _gather_finish(LATE, sh, full, send, recv, fsend, frecv)

    tok = lambda w: pl.BlockSpec((tb, w), lambda i: (i, 0))
    sem = pltpu.SemaphoreType.DMA((nl, 3))
    n_in = 11
    return pl.pallas_call(
        body, name="f4_mixer", grid=(nt,),
        in_specs=[pl.BlockSpec((N_LANE_BLOCKS, tb, LANES), lambda i: (0, i, 0)), tok(prest.shape[1]),
                  pl.BlockSpec((CONV_HALO, 2 * CONV_WIDTH), lambda i: (jnp.maximum(i * hb - 1, 0), 0)),
                  tok(D_MODEL), _full(modv), _full(cvv), _full(cw32), _full(w_a), _full(w_b), _full(w_pw), _full(w_out)]
        + [ANY] * (2 * nl),
        out_specs=[ANY] * nl + [tok(D_MODEL), tok(D_MODEL), tok(D_MODEL), tok(D_MODEL), tok(CONV_WIDTH), tok(SSM_WIDTH),
                                tok(CONV_WIDTH), tok(D_MODEL)],
        input_output_aliases={n_in + nl + k: k for k in range(nl)},
        out_shape=[jax.ShapeDtypeStruct(f.shape, f.dtype) for f in late_full]
        + [jax.ShapeDtypeStruct((t, D_MODEL), F32), jax.ShapeDtypeStruct((t, D_MODEL), BF16),
                   jax.ShapeDtypeStruct((t, D_MODEL), BF16), jax.ShapeDtypeStruct((t, D_MODEL), BF16),
                   jax.ShapeDtypeStruct((t, CONV_WIDTH), F32), jax.ShapeDtypeStruct((t, SSM_WIDTH), BF16),
                   jax.ShapeDtypeStruct((t, CONV_WIDTH), BF16), jax.ShapeDtypeStruct((t, D_MODEL), BF16)],
        scratch_shapes=[pltpu.VMEM((tb + CONV_HALO, CONV_WIDTH), F32),
                        pltpu.VMEM((7, tb + CONV_HALO - 8, CONV_WIDTH), F32), sem, sem, sem, sem],
        compiler_params=_params(("arbitrary",)),
    )(ys4, prest, prest, x, modv, cvv, cw32, w_a, w_b, w_pw, w_out, *late_sh, *late_full)


FFN_COLS = 1408


def _f5_ffn(r1, tgt, modv, lnv, fdw, w_up, w_down, tb):
    t = r1.shape[0]
    fw = 2 * FFN_HIDDEN

    def body(r1_ref, tgt_ref, modv_ref, lnv_ref, fdw_ref, wup_hbm, wdn_hbm,
             dr2_ref, d_ref, up_ref, z_ref, acc_ref, wup_v, wdn_v, upbuf, gbuf, hbuf, sems):
        i = pl.program_id(0)

        @pl.when(i == 0)
        def _():
            _load_once(wup_hbm, wup_v, sems.at[0])
            _load_once(wdn_hbm, wdn_v, sems.at[1])
            acc_ref[...] = jnp.zeros_like(acc_ref)
            upbuf[0:8, :] = jnp.zeros((8, fw), F32)

        xh1, _ = _ln_stats(r1_ref[...])
        x1 = xh1 * lnv_ref[0:1, :] + lnv_ref[1:2, :]
        xn2, _ = _ln_stats(x1)
        h2 = (xn2 * (1.0 + modv_ref[4:5, :]) + modv_ref[3:4, :]).astype(BF16)
        for j in range(fw // FFN_COLS):
            cs = slice(j * FFN_COLS, (j + 1) * FFN_COLS)
            up = _dot(h2, wup_v[:, cs])
            upbuf[8:, cs] = up
            up_ref[:, cs] = up.astype(BF16)

        def conv(cs):
            return (fdw_ref[0:1, cs] * upbuf[pl.ds(6, tb), cs] + fdw_ref[1:2, cs] * upbuf[pl.ds(7, tb), cs]
                    + fdw_ref[2:3, cs] * upbuf[pl.ds(8, tb), cs] + fdw_ref[3:4, cs])

        halves = [(slice(j * FFN_COLS, (j + 1) * FFN_COLS),
                   slice(FFN_HIDDEN + j * FFN_COLS, FFN_HIDDEN + (j + 1) * FFN_COLS)) for j in range(FFN_HIDDEN // FFN_COLS)]
        yf = jnp.zeros((tb, D_MODEL), F32)
        for ca, cv in halves:
            v = conv(cv)
            g, dg = _gelu_and_grad(conv(ca))
            gbuf[:, ca] = g.astype(BF16)
            hbuf[:, ca] = (v * dg).astype(BF16)
            z = (g * v).astype(BF16)
            z_ref[:, ca] = z
            yf += _dot(z, wdn_v[ca, :])
        r2 = ALPHA * x1 + modv_ref[5:6, :] * yf
        xh2, rstd2 = _ln_stats(r2)
        e = xh2 * lnv_ref[2:3, :] + lnv_ref[3:4, :] - tgt_ref[...]
        dx2 = e * (1.0 / D_MODEL)
        acc_ref[3:4, :] += _colsum(e * e) * (0.5 / D_MODEL)
        acc_ref[0:1, :] += _colsum(dx2 * xh2)
        acc_ref[1:2, :] += _colsum(dx2)
        dr2 = _ln_bwd(dx2 * lnv_ref[2:3, :], xh2, rstd2)
        dr2_ref[...] = dr2
        acc_ref[2:3, :] += _colsum(dr2 * yf)
        dyf = (modv_ref[5:6, :] * dr2).astype(BF16)
        for ca, cv in halves:
            dz = _dot_nt(dyf, wdn_v[ca, :])
            d_ref[:, ca] = (dz * hbuf[:, ca].astype(F32)).astype(BF16)
            d_ref[:, cv] = (dz * gbuf[:, ca].astype(F32)).astype(BF16)
        upbuf[0:8, :] = upbuf[pl.ds(tb, 8), :]

    tok = lambda w: pl.BlockSpec((tb, w), lambda i: (i, 0))
    return pl.pallas_call(
        body, name="f5_ffn", grid=(t // tb,),
        in_specs=[tok(D_MODEL), tok(D_MODEL), _full(modv), _full(lnv), _full(fdw), ANY, ANY],
        out_specs=[tok(D_MODEL), tok(fw), tok(fw), tok(FFN_HIDDEN), pl.BlockSpec((8, D_MODEL), lambda i: (0, 0))],
        out_shape=[jax.ShapeDtypeStruct((t, D_MODEL), F32), jax.ShapeDtypeStruct((t, fw), BF16),
                   jax.ShapeDtypeStruct((t, fw), BF16), jax.ShapeDtypeStruct((t, FFN_HIDDEN), BF16),
                   jax.ShapeDtypeStruct((8, D_MODEL), F32)],
        scratch_shapes=[pltpu.VMEM(w_up.shape, BF16), pltpu.VMEM(w_down.shape, BF16),
                        pltpu.VMEM((tb + 8, fw), F32), pltpu.VMEM((tb, FFN_HIDDEN), BF16),
                        pltpu.VMEM((tb, FFN_HIDDEN), BF16), pltpu.SemaphoreType.DMA((2,))],
        compiler_params=_params(("arbitrary",)),
    )(r1, tgt, modv, lnv, fdw, w_up, w_down)


def _b1b_ffn_up(d, up, dr2, r1, modv, lnv, fdw, w_up, tb):
    t = dr2.shape[0]
    fw = 2 * FFN_HIDDEN
    nt = t // tb
    hb = tb // 16

    def body(d_ref, nxt_ref, up_ref, dr2_ref, r1_ref, modv_ref, lnv_ref, fdw_ref, wup_hbm, dup_ref, dr1_ref, h2_ref,
             dyf_ref, acc_ref, accw_ref, wup_v, dbuf, shifted, sem):
        i = pl.program_id(0)

        @pl.when(i == 0)
        def _():
            _load_once(wup_hbm, wup_v, sem)
            acc_ref[...] = jnp.zeros_like(acc_ref)
            accw_ref[...] = jnp.zeros_like(accw_ref)

        dbuf[0:tb, :] = d_ref[...].astype(F32)
        dbuf[tb:, :] = jnp.where(i == nt - 1, 0.0, nxt_ref[...].astype(F32))
        dh2 = jnp.zeros((tb, D_MODEL), F32)
        for j in range(fw // FFN_COLS):
            cs = slice(j * FFN_COLS, (j + 1) * FFN_COLS)
            for k in range(1, FFN_KERNEL):
                shifted[k - 1] = dbuf[pl.ds(k, tb), cs]
            ds = [dbuf[pl.ds(0, tb), cs], shifted[0], shifted[1]]
            dup = (fdw_ref[2:3, cs] * ds[0] + fdw_ref[1:2, cs] * ds[1] + fdw_ref[0:1, cs] * ds[2]).astype(BF16)
            dup_ref[:, cs] = dup
            dh2 += _dot_nt(dup, wup_v[:, cs])
            upf = up_ref[:, cs].astype(F32)
            for k in range(FFN_KERNEL):
                accw_ref[k:k + 1, cs] += _colsum(ds[FFN_KERNEL - 1 - k] * upf)
            accw_ref[3:4, cs] += _colsum(ds[0])
        xh1, rstd1 = _ln_stats(r1_ref[...])
        x1 = xh1 * lnv_ref[0:1, :] + lnv_ref[1:2, :]
        xn2, rstd2 = _ln_stats(x1)
        h2_ref[...] = (xn2 * (1.0 + modv_ref[4:5, :]) + modv_ref[3:4, :]).astype(BF16)
        dr2 = dr2_ref[...]
        dyf_ref[...] = (modv_ref[5:6, :] * dr2).astype(BF16)
        acc_ref[0:1, :] += _colsum(dh2 * xn2)
        acc_ref[1:2, :] += _colsum(dh2)
        dx1 = _ln_bwd(dh2 * (1.0 + modv_ref[4:5, :]), xn2, rstd2) + ALPHA * dr2
        acc_ref[2:3, :] += _colsum(dx1 * xh1)
        acc_ref[3:4, :] += _colsum(dx1)
        dr1_ref[...] = _ln_bwd(dx1 * lnv_ref[0:1, :], xh1, rstd1)

    tok = lambda w: pl.BlockSpec((tb, w), lambda i: (i, 0))
    return pl.pallas_call(
        body, name="b1b_ffn_up", grid=(nt,),
        in_specs=[tok(fw), pl.BlockSpec((16, fw), lambda i: (jnp.minimum((i + 1) * hb, t // 16 - 1), 0)), tok(fw),
                  tok(D_MODEL), tok(D_MODEL), _full(modv), _full(lnv), _full(fdw), ANY],
        out_specs=[tok(fw), tok(D_MODEL), tok(D_MODEL), tok(D_MODEL), pl.BlockSpec((8, D_MODEL), lambda i: (0, 0)),
                   pl.BlockSpec((8, fw), lambda i: (0, 0))],
        out_shape=[jax.ShapeDtypeStruct((t, fw), BF16), jax.ShapeDtypeStruct((t, D_MODEL), F32),
                   jax.ShapeDtypeStruct((t, D_MODEL), BF16), jax.ShapeDtypeStruct((t, D_MODEL), BF16),
                   jax.ShapeDtypeStruct((8, D_MODEL), F32), jax.ShapeDtypeStruct((8, fw), F32)],
        scratch_shapes=[pltpu.VMEM(w_up.shape, BF16), pltpu.VMEM((tb + 16, fw), F32),
                        pltpu.VMEM((FFN_KERNEL - 1, tb, FFN_COLS), F32), pltpu.SemaphoreType.DMA],
        compiler_params=_params(("arbitrary",)),
    )(d, d, up, dr2, r1, modv, lnv, fdw, w_up)


def _b2_mixer(dr1, ys4, prest, ya, yb, ycv, vc, merged, modv, cvv, cw32, w_a, w_b, w_pw, w_out, late_dw, tb):
    t = dr1.shape[0]
    nt = t // tb
    nl = len(LATE)
    cwd = CONV_WIDTH

    def body(dr1_ref, ys_ref, pr_ref, ya_ref, yb_ref, ycv_ref, vc_ref, mg_ref, modv_ref, cvv_ref, cw_ref,
             wa_ref, wb_ref, wpw_ref, wout_ref, *rest):
        dw, got = rest[:nl], rest[nl:2 * nl]
        (dys_ref, dpr_ref, dya_ref, dyb_ref, dycv_ref, dym_ref, acc_a, acc_b, acc_w, dvbuf, dvrot,
         send, recv) = rest[2 * nl:]
        i = pl.program_id(0)
        ti = nt - 1 - i

        @pl.when(i == 0)
        def _():
            for cp in _scatter(LATE, dw, got, send, recv):
                cp.start()
            acc_a[...] = jnp.zeros_like(acc_a)
            acc_b[...] = jnp.zeros_like(acc_b)
            acc_w[...] = jnp.zeros_like(acc_w)
            dvbuf[pl.ds(tb, CONV_HALO), :] = jnp.zeros((CONV_HALO, cwd), F32)

        dr1 = dr1_ref[...]
        dym = (modv_ref[2:3, :] * dr1).astype(BF16)
        dym_ref[...] = dym
        ym = _dot(mg_ref[...], wout_ref[...])
        acc_a[0:1, :] += _colsum(dr1 * ym)
        dmg = _dot_nt(dym, wout_ref[...])
        sgs = _sig(pr_ref[:, 2 * cwd:2 * cwd + D_MODEL])
        sgc = _sig(pr_ref[:, 2 * cwd + D_MODEL:])
        ya_v = ya_ref[...].astype(F32)
        syb = _sig(yb_ref[...].astype(F32))
        ycv_v = ycv_ref[...].astype(F32)
        dpr_ref[:, 2 * cwd:2 * cwd + D_MODEL] = (dmg * (ya_v * syb) * sgs * (1.0 - sgs)).astype(BF16)
        dpr_ref[:, 2 * cwd + D_MODEL:] = (dmg * ycv_v * sgc * (1.0 - sgc)).astype(BF16)
        dyssm = dmg * sgs
        dya = (dyssm * syb).astype(BF16)
        dyb = (dyssm * ya_v * syb * (1.0 - syb)).astype(BF16)
        dya_ref[...] = dya
        dyb_ref[...] = dyb
        dyg = _dot_nt(dya, wa_ref[...]) + _dot_nt(dyb, wb_ref[...])
        ys = jnp.concatenate([ys_ref[b] for b in range(N_LANE_BLOCKS)], axis=-1)
        dys = dyg * _dgelu(ys)
        for b in range(N_LANE_BLOCKS):
            dys_ref[b] = dys[:, b * LANES:(b + 1) * LANES]
        dycv = (dmg * sgc).astype(BF16)
        dycv_ref[...] = dycv
        dvs = _dot_nt(dycv, wpw_ref[...])
        xh, rstd = _ln_stats(vc_ref[...])
        vl = xh * cvv_ref[1:2, :] + cvv_ref[2:3, :]
        s = _sig(vl)
        dvl = dvs * s * (1.0 + vl * (1.0 - s))
        acc_b[1:2, :] += _colsum(dvl * xh)
        acc_b[2:3, :] += _colsum(dvl)
        dvc = _ln_bwd(dvl * cvv_ref[1:2, :], xh, rstd)
        acc_b[0:1, :] += _colsum(dvc)
        cva = pr_ref[:, 0:cwd]
        scg = _sig(pr_ref[:, cwd:2 * cwd])
        v = cva * scg
        dvbuf[0:tb, :] = dvc
        _fill_rotations(dvbuf, dvrot, tb + CONV_HALO - 8)
        dv = jnp.zeros((tb, cwd), F32)
        for k in range(CONV_KERNEL):
            later = _rows_at(dvbuf, dvrot, CONV_KERNEL - 1 - k, tb)
            dv += later * cw_ref[k:k + 1, :]
            acc_w[k:k + 1, :] += _colsum(later * v)
        dvbuf[pl.ds(tb, CONV_HALO), :] = dvbuf[0:CONV_HALO, :]
        dpr_ref[:, 0:cwd] = (dv * scg).astype(BF16)
        dpr_ref[:, cwd:2 * cwd] = (dv * cva * scg * (1.0 - scg)).astype(BF16)

        @pl.when(i == nt - 1)
        def _():
            for cp in _scatter(LATE, dw, got, send, recv):
                cp.wait()

    rtok = lambda w: pl.BlockSpec((tb, w), lambda i: (nt - 1 - i, 0))
    r4 = pl.BlockSpec((N_LANE_BLOCKS, tb, LANES), lambda i: (0, nt - 1 - i, 0))
    pw = prest.shape[1]
    return pl.pallas_call(
        body, name="b2_mixer", grid=(nt,),
        in_specs=[rtok(D_MODEL), r4, rtok(pw),
                  rtok(D_MODEL), rtok(D_MODEL), rtok(D_MODEL), rtok(cwd), rtok(D_MODEL),
                  _full(modv), _full(cvv), _full(cw32), _full(w_a), _full(w_b), _full(w_pw), _full(w_out)] + [ANY] * nl,
        out_specs=[ANY] * nl + [r4, rtok(pw), rtok(D_MODEL), rtok(D_MODEL), rtok(D_MODEL), rtok(D_MODEL),
                   pl.BlockSpec((8, D_MODEL), lambda i: (0, 0)), pl.BlockSpec((8, cwd), lambda i: (0, 0)),
                   pl.BlockSpec((CONV_HALO, cwd), lambda i: (0, 0))],
        out_shape=[jax.ShapeDtypeStruct((N_DEV - 1,) + _piece(*BIG[m][1:]), BF16) for m in LATE]
        + [jax.ShapeDtypeStruct((N_LANE_BLOCKS, t, LANES), F32), jax.ShapeDtypeStruct((t, pw), BF16),
                   jax.ShapeDtypeStruct((t, D_MODEL), BF16), jax.ShapeDtypeStruct((t, D_MODEL), BF16),
                   jax.ShapeDtypeStruct((t, D_MODEL), BF16), jax.ShapeDtypeStruct((t, D_MODEL), BF16),
                   jax.ShapeDtypeStruct((8, D_MODEL), F32), jax.ShapeDtypeStruct((8, cwd), F32),
                   jax.ShapeDtypeStruct((CONV_HALO, cwd), F32)],
        scratch_shapes=[pltpu.VMEM((tb + CONV_HALO, cwd), F32), pltpu.VMEM((7, tb + CONV_HALO - 8, cwd), F32),
                        pltpu.SemaphoreType.DMA((nl, N_DEV - 1)), pltpu.SemaphoreType.DMA((nl, N_DEV - 1))],
        compiler_params=_params(("arbitrary",)),
    )(dr1, ys4, prest, ya, yb, ycv, vc, merged, modv, cvv, cw32, w_a, w_b, w_pw, w_out, *late_dw)


def _s5c_state_bwd(dy4, sg_r, sg_i, a8, hr, hi):
    nb, t, _ = dy4.shape
    nc = t // SSM_CHUNK
    sc = STATE_COLS

    def body(dy_ref, sgr_ref, sgi_ref, a_ref, hr_ref, hi_ref, dxr_ref, dxi_ref, da_ref, dsgr_ref, dsgi_ref,
             g_s, lr_s, li_s, xr_s, xi_s):
        dy = _load_chunks(dy_ref, nc)
        _build_maps(sgr_ref, g_s)
        lr_s[...] = _dot(dy, g_s[...])
        _build_maps(sgi_ref, g_s)
        li_s[...] = _dot(dy, g_s[...])
        ar = a_ref[0, 0:1, :]
        ai = a_ref[0, 1:2, :]

        def step(k, carry):
            pr, pi, dar, dai = carry
            c = nc - 1 - k
            xr_s[pl.ds(c, 1), :] = pr
            xi_s[pl.ds(c, 1), :] = pi
            h_r = hr_ref[0, pl.ds(c, 1), :]
            h_i = hi_ref[0, pl.ds(c, 1), :]
            dar = dar + pr * h_r + pi * h_i
            dai = dai - pr * h_i + pi * h_r
            nr = lr_s[pl.ds(c, 1), :] + ar * pr + ai * pi
            ni = li_s[pl.ds(c, 1), :] - ai * pr + ar * pi
            return nr, ni, dar, dai

        z = jnp.zeros((1, sc), F32)
        _, _, dar, dai = lax.fori_loop(0, nc, step, (z, z, z, z))
        da_ref[0] = jnp.concatenate([dar, dai, jnp.zeros((6, sc), F32)], axis=0)
        dxr_ref[0] = xr_s[...].astype(BF16)
        dxi_ref[0] = xi_s[...].astype(BF16)
        for h_ref, o_ref in ((hr_ref, dsgr_ref), (hi_ref, dsgi_ref)):
            hb = h_ref[0].astype(BF16)
            for j in range(SSM_CHUNK):
                o_ref[j] = _fold(_dot_tn(dy[:, j * LANES:(j + 1) * LANES], hb), SSM_STATE)

    blk = lambda r, c: pl.BlockSpec((1, r, c), lambda b: (b, 0, 0))
    return pl.pallas_call(
        body, name="s5c_state_bwd", grid=(nb,),
        in_specs=[_tblk(t), _cblk(SSM_STATE), _cblk(SSM_STATE), blk(8, sc), blk(nc, sc), blk(nc, sc)],
        out_specs=[blk(nc, sc), blk(nc, sc), blk(8, sc), _cblk(SSM_STATE), _cblk(SSM_STATE)],
        out_shape=[jax.ShapeDtypeStruct((nb, nc, sc), BF16), jax.ShapeDtypeStruct((nb, nc, sc), BF16),
                   jax.ShapeDtypeStruct((nb, 8, sc), F32),
                   jax.ShapeDtypeStruct((SSM_CHUNK, SSM_WIDTH, SSM_STATE), F32),
                   jax.ShapeDtypeStruct((SSM_CHUNK, SSM_WIDTH, SSM_STATE), F32)],
        scratch_shapes=[pltpu.VMEM((CHUNK_COLS, sc), BF16)] + [pltpu.VMEM((nc, sc), F32)] * 4,
        compiler_params=_params(("arbitrary",)),
    )(dy4, sg_r, sg_i, a8, hr, hi)


def _s5d_input_bwd(dy4, u4, kc, sw_r, sw_i, dxr, dxi, mid_dw):
    nb, t, _ = dy4.shape
    nc = t // SSM_CHUNK
    sc = STATE_COLS
    nl = len(MID)

    def body(dy_ref, u_ref, kc_ref, swr_ref, swi_ref, dxr_ref, dxi_ref, *rest):
        dw, got = rest[:nl], rest[nl:2 * nl]
        du_ref, dkc_ref, dswr_ref, dswi_ref, tm_s, w_s, dk_s, send, recv = rest[2 * nl:]

        @pl.when(pl.program_id(0) == 0)
        def _():
            for cp in _scatter(MID, dw, got, send, recv):
                cp.start()

        dy = _load_chunks(dy_ref, nc)
        u = _load_chunks(u_ref, nc)
        _build_toeplitz(kc_ref, tm_s)
        du = _dot_nt(dy, tm_s[...])
        _build_maps(swr_ref, w_s)
        du += _dot_nt(dxr_ref[0], w_s[...])
        _build_maps(swi_ref, w_s)
        du += _dot_nt(dxi_ref[0], w_s[...])
        _store_chunks(du_ref, du, nc)
        dk_s[...] = jnp.zeros_like(dk_s)
        for ji in range(SSM_CHUNK):
            uj = u[:, ji * LANES:(ji + 1) * LANES]
            rows = _dot_tn(uj, dy)
            for jo in range(ji, SSM_CHUNK):
                dk_s[jo - ji] += rows[:, jo * LANES:(jo + 1) * LANES]
            dswr_ref[ji] = _fold(_dot_tn(uj, dxr_ref[0]), SSM_STATE)
            dswi_ref[ji] = _fold(_dot_tn(uj, dxi_ref[0]), SSM_STATE)
        for d in range(SSM_CHUNK):
            dkc_ref[d] = _fold(dk_s[d], SSM_GROUP)

        @pl.when(pl.program_id(0) == nb - 1)
        def _():
            for cp in _scatter(MID, dw, got, send, recv):
                cp.wait()

    blk = lambda r, c: pl.BlockSpec((1, r, c), lambda b: (b, 0, 0))
    ssem = pltpu.SemaphoreType.DMA((nl, N_DEV - 1))
    return pl.pallas_call(
        body, name="s5d_input_bwd", grid=(nb,),
        in_specs=[_tblk(t), _tblk(t), _cblk(SSM_GROUP), _cblk(SSM_STATE), _cblk(SSM_STATE),
                  blk(nc, sc), blk(nc, sc)] + [ANY] * nl,
        out_specs=[ANY] * nl + [_tblk(t), _cblk(SSM_GROUP), _cblk(SSM_STATE), _cblk(SSM_STATE)],
        out_shape=[jax.ShapeDtypeStruct((N_DEV - 1,) + _piece(*BIG[m][1:]), BF16) for m in MID]
        + [jax.ShapeDtypeStruct((nb, t, LANES), F32),
           jax.ShapeDtypeStruct((SSM_CHUNK, SSM_WIDTH, SSM_GROUP), F32),
           jax.ShapeDtypeStruct((SSM_CHUNK, SSM_WIDTH, SSM_STATE), F32),
           jax.ShapeDtypeStruct((SSM_CHUNK, SSM_WIDTH, SSM_STATE), F32)],
        scratch_shapes=[pltpu.VMEM((CHUNK_COLS, CHUNK_COLS), BF16), pltpu.VMEM((CHUNK_COLS, sc), BF16),
                        pltpu.VMEM((SSM_CHUNK, LANES, LANES), F32), ssem, ssem],
        compiler_params=_params(("arbitrary",)),
    )(dy4, u4, kc, sw_r, sw_i, dxr, dxi, *mid_dw)


def _b3_inproj(x, dr1, du4, dprest, modv, w_in, tb):
    t = x.shape[0]
    pw = IN_PROJ_WIDTH - SSM_WIDTH

    def body(x_ref, dr1_ref, du_ref, dpr_ref, modv_ref, w_hbm, gx_ref, dp_ref, acc_ref, accb_ref, w_v, sem):
        @pl.when(pl.program_id(0) == 0)
        def _():
            _load_once(w_hbm, w_v, sem)
            acc_ref[...] = jnp.zeros_like(acc_ref)
            accb_ref[...] = jnp.zeros_like(accb_ref)

        du = jnp.concatenate([du_ref[b] for b in range(N_LANE_BLOCKS)], axis=-1).astype(BF16)
        dpr = dpr_ref[...]
        dp_ref[:, 0:SSM_WIDTH] = du
        dp_ref[:, SSM_WIDTH:] = dpr
        accb_ref[0:1, 0:SSM_WIDTH] += _colsum(du.astype(F32))
        accb_ref[0:1, SSM_WIDTH:] += _colsum(dpr.astype(F32))
        dh = _dot_nt(du, w_v[:, 0:SSM_WIDTH]) + _dot_nt(dpr, w_v[:, SSM_WIDTH:])
        xn, rstd = _ln_stats(x_ref[...])
        acc_ref[0:1, :] += _colsum(dh * xn)
        acc_ref[1:2, :] += _colsum(dh)
        gx_ref[...] = _ln_bwd(dh * (1.0 + modv_ref[1:2, :]), xn, rstd) + ALPHA * dr1_ref[...]

    tok = lambda w: pl.BlockSpec((tb, w), lambda i: (i, 0))
    return pl.pallas_call(
        body, name="b3_inproj", grid=(t // tb,),
        in_specs=[tok(D_MODEL), tok(D_MODEL), pl.BlockSpec((N_LANE_BLOCKS, tb, LANES), lambda i: (0, i, 0)), tok(pw),
                  _full(modv), ANY],
        out_specs=[tok(D_MODEL), tok(IN_PROJ_WIDTH), pl.BlockSpec((8, D_MODEL), lambda i: (0, 0)),
                   pl.BlockSpec((8, IN_PROJ_WIDTH), lambda i: (0, 0))],
        out_shape=[jax.ShapeDtypeStruct((t, D_MODEL), F32), jax.ShapeDtypeStruct((t, IN_PROJ_WIDTH), BF16),
                   jax.ShapeDtypeStruct((8, D_MODEL), F32), jax.ShapeDtypeStruct((8, IN_PROJ_WIDTH), F32)],
        scratch_shapes=[pltpu.VMEM(w_in.shape, BF16), pltpu.SemaphoreType.DMA],
        compiler_params=_params(("arbitrary",)),
    )(x, dr1, du4, dprest, modv, w_in)


TN_ROWS = 2048


def _tn_matmul(name, a, b, tm, tn):
    t, m = a.shape
    n = b.shape[1]
    tt = min(TN_ROWS, t)
    nk = t // tt

    def body(a_ref, b_ref, o_ref, acc):
        k = pl.program_id(2)

        @pl.when(k == 0)
        def _():
            acc[...] = jnp.zeros_like(acc)

        acc[...] += _dot_tn(a_ref[...], b_ref[...])

        @pl.when(k == nk - 1)
        def _():
            o_ref[...] = acc[...].astype(BF16)

    return pl.pallas_call(
        body, name=name, grid=(m // tm, n // tn, nk),
        in_specs=[pl.BlockSpec((tt, tm), lambda i, j, k: (k, i)), pl.BlockSpec((tt, tn), lambda i, j, k: (k, j))],
        out_specs=pl.BlockSpec((tm, tn), lambda i, j, k: (i, j)),
        out_shape=jax.ShapeDtypeStruct((m, n), BF16),
        scratch_shapes=[pltpu.VMEM((tm, tn), F32)],
        compiler_params=_params(("arbitrary", "arbitrary", "arbitrary")),
    )(a, b)


def _local_step(x, tgt, modv, wb, shards, sp, tb=256):
    t = x.shape[0]
    row8 = lambda rows, w: jnp.concatenate([r.reshape(1, w) for r in rows] + [jnp.zeros((8 - len(rows), w), F32)], axis=0)
    lnv = row8([sp["ln1_g"], sp["ln1_b"], sp["ln2_g"], sp["ln2_b"]], D_MODEL)
    cvv = row8([sp["cv_dw_b"], sp["cv_ln_g"], sp["cv_ln_b"]], CONV_WIDTH)
    cw32 = jnp.concatenate([sp["cv_dw_w"].reshape(CONV_KERNEL, CONV_WIDTH), jnp.zeros((1, CONV_WIDTH), F32)], axis=0)
    fdw = row8(list(sp["ffn_dw_w"].reshape(FFN_KERNEL, 2 * FFN_HIDDEN)) + [sp["ffn_dw_b"]], 2 * FFN_HIDDEN)
    b_in = sp["b_in"].reshape(1, IN_PROJ_WIDTH)
    ssm = tuple(sp[k] for k in ("ssm_lambda_re", "ssm_lambda_im", "ssm_log_dt", "ssm_b_re", "ssm_b_im", "ssm_c_re",
                                "ssm_c_im", "ssm_d"))
    (kc, sw_r, sw_i, sg_r, sg_i, a), ssm_vjp = jax.vjp(_s5_build, *ssm)
    a8 = jnp.concatenate([a, jnp.zeros((N_LANE_BLOCKS, 6, STATE_COLS), F32)], axis=1)

    name = lambda m: BIG[m][0]
    *mid_w, u4, prest, h1 = _f1_inproj(x, modv, b_in, wb["w_in"], [shards[m] for m in MID], [wb[name(m)] for m in MID], tb)
    w_a, w_b, w_pw, w_out = mid_w
    hr, hi = _s5a_state(u4, sw_r, sw_i, a8)
    ys4 = _s5b_out(u4, kc, sg_r, sg_i, hr, hi)
    w_up, w_down, r1, ya, yb, ycv, vc, yg, vs, merged = _f4_mixer(
        ys4, prest, x, modv, cvv, cw32, w_a, w_b, w_pw, w_out, [shards[m] for m in LATE], [wb[name(m)] for m in LATE], tb)
    dr2, dconv, up, z, acc5 = _f5_ffn(r1, tgt, modv, lnv, fdw, w_up, w_down, tb)
    dup, dr1, h2, dyf, acc1b, acc1a = _b1b_ffn_up(dconv, up, dr2, r1, modv, lnv, fdw, w_up, tb)
    late_dw = [_tn_matmul("dw_up", h2, dup, 1024, FFN_COLS), _tn_matmul("dw_down", z, dyf, FFN_COLS, 1024)]
    got_up, got_down, dys4, dprest, dya, dyb, dycv, dym, acc2a, acc2b, acc2w = _b2_mixer(
        dr1, ys4, prest, ya, yb, ycv, vc, merged, modv, cvv, cw32, w_a, w_b, w_pw, w_out, late_dw, tb)
    mid_dw = [_tn_matmul("dw_glu_a", yg, dya, 512, 1024), _tn_matmul("dw_glu_b", yg, dyb, 512, 1024),
              _tn_matmul("dw_pw", vs, dycv, 512, 1024), _tn_matmul("dw_out", merged, dym, 1024, 1024)]
    dxr, dxi, da8, dsg_r, dsg_i = _s5c_state_bwd(dys4, sg_r, sg_i, a8, hr, hi)
    *mid_got, du4, dkc, dsw_r, dsw_i = _s5d_input_bwd(dys4, u4, kc, sw_r, sw_i, dxr, dxi, mid_dw)
    dssm = ssm_vjp((dkc, dsw_r, dsw_i, dsg_r, dsg_i, da8[:, 0:2, :]))
    gx, dp, acc3, acc3b = _b3_inproj(x, dr1, du4, dprest, modv, wb["w_in"], tb)
    dbig = [_tn_matmul("dw_in", h1, dp, 1024, 896)] + mid_dw + late_dw
    dmod = jnp.concatenate([acc3[1], acc3[0], acc2a[0], acc1b[1], acc1b[0], acc5[2]])
    small = {
        "dmod": dmod, "b_in": acc3b[0],
        "ssm_lambda_re": dssm[0], "ssm_lambda_im": dssm[1], "ssm_log_dt": dssm[2], "ssm_b_re": dssm[3],
        "ssm_b_im": dssm[4], "ssm_c_re": dssm[5], "ssm_c_im": dssm[6], "ssm_d": dssm[7],
        "cv_dw_w": acc2w[0:CONV_KERNEL], "cv_dw_b": acc2b[0], "cv_ln_g": acc2b[1], "cv_ln_b": acc2b[2],
        "ln1_g": acc1b[2], "ln1_b": acc1b[3], "ffn_dw_w": acc1a[0:FFN_KERNEL], "ffn_dw_b": acc1a[3],
        "ln2_g": acc5[0], "ln2_b": acc5[1], "loss": jnp.sum(acc5[3]).reshape(1),
    }
    return gx, dbig, list(mid_got) + [got_up, got_down], small


def _allgather_rows(x_ref, out_ref, send_sems, recv_sems, local_sem):
    m_per = x_ref.shape[0]
    x, y, c, chips = _place()
    me, sibling = (x, y, c), (x, y, 1 - c)

    def rows(px, py, pc):
        return out_ref.at[pl.ds((4 * px + 2 * py + pc) * m_per, m_per), :]

    def copy(k, block, to, src=None):
        return pltpu.make_async_remote_copy(
            src_ref=rows(*block) if src is None else src, dst_ref=rows(*block),
            send_sem=send_sems.at[k], recv_sem=recv_sems.at[k], device_id=to, device_id_type=MESH)

    mine = pltpu.make_async_copy(x_ref, rows(*me), local_sem)
    mine.start()
    first = [copy(0, me, sibling, src=x_ref)]
    first += [copy(1 + j, me, (*chip, c), src=x_ref) for j, chip in enumerate(chips)]
    for cp in first:
        cp.start()

    def finish():
        passed = [copy(4 + j, (*chip, c), sibling) for j, chip in enumerate(chips)]
        for j, chip in enumerate(chips):
            copy(1 + j, (*chip, c), me).wait_recv()
            passed[j].start()
        copy(0, sibling, me).wait_recv()
        for j, chip in enumerate(chips):
            copy(4 + j, (*chip, 1 - c), me).wait_recv()
        for cp in first + passed:
            cp.wait_send()
        mine.wait()

    return finish


def _allgather(name, shard):
    m_per, n = shard.shape

    def body(x_ref, out_ref, send_sems, recv_sems, local_sem):
        _allgather_rows(x_ref, out_ref, send_sems, recv_sems, local_sem)()

    return pl.pallas_call(
        body, name=name,
        out_shape=jax.ShapeDtypeStruct((N_DEV * m_per, n), shard.dtype),
        in_specs=[pl.BlockSpec(memory_space=pltpu.VMEM)],
        out_specs=pl.BlockSpec(memory_space=pltpu.VMEM),
        scratch_shapes=[pltpu.SemaphoreType.DMA((7,)), pltpu.SemaphoreType.DMA((7,)), pltpu.SemaphoreType.DMA],
        compiler_params=_params(),
    )(shard)


def _add_rows(pr):
    return 64 if pr % 64 == 0 else 16


def _gather_weights(shards, first):
    nm = len(BIG)
    nl = len(DIRECT)

    def body(*refs):
        ins, first_ref = refs[:nm], refs[nm]
        outs, lsh, first_all = refs[nm + 1:2 * nm + 1], refs[2 * nm + 1:2 * nm + 1 + nl], refs[2 * nm + 1 + nl]
        stage = refs[2 * nm + 2 + nl:3 * nm + 2 + nl]
        send, recv, fsend, frecv, lsem, ag_send, ag_recv, ag_local = refs[3 * nm + 2 + nl:]
        x, y, c, chips = _place()
        s_me = 2 * x + y
        sibling = (x, y, 1 - c)
        pend = []
        finish_first = _allgather_rows(first_ref, first_all, ag_send, ag_recv, ag_local)
        for m, (_, kind, shape) in enumerate(BIG):
            stage[m][...] = ins[m][...].astype(BF16)
            pr, pc = _piece(kind, shape)
            for k in range(2):
                cp = pltpu.make_async_copy(stage[m].at[pl.ds(k * pr, pr), :], _piece_at(outs[m], kind, shape, s_me, k),
                                           lsem.at[m, k])
                cp.start()
                pend.append(cp.wait)
            if m in DIRECT:
                cp = pltpu.make_async_copy(stage[m], lsh[DIRECT.index(m)], lsem.at[m, 2])
                cp.start()
                pend.append(cp.wait)
                continue
            for j, chip in enumerate(chips):
                cp = pltpu.make_async_remote_copy(
                    src_ref=stage[m].at[pl.ds(pl.multiple_of(c * pr, 16), pr), :],
                    dst_ref=_piece_at(outs[m], kind, shape, s_me, c),
                    send_sem=send.at[m, j], recv_sem=recv.at[m, j], device_id=(*chip, c), device_id_type=MESH)
                cp.start()
                pend.append(cp.wait_send)
        finish_first()
        for m in EARLY:
            _, kind, shape = BIG[m]
            for j, (cx, cy) in enumerate(chips):
                got = _piece_at(outs[m], kind, shape, 2 * cx + cy, c)
                pltpu.make_async_remote_copy(src_ref=got, dst_ref=got, send_sem=send.at[m, j], recv_sem=recv.at[m, j],
                                             device_id=(cx, cy, c), device_id_type=MESH).wait_recv()
                cp = pltpu.make_async_remote_copy(src_ref=got, dst_ref=got, send_sem=fsend.at[m, j],
                                                  recv_sem=frecv.at[m, j], device_id=sibling, device_id_type=MESH)
                cp.start()
                pend.append(cp.wait_send)
        for m in EARLY:
            _, kind, shape = BIG[m]
            for j, (cx, cy) in enumerate(chips):
                got = _piece_at(outs[m], kind, shape, 2 * cx + cy, 1 - c)
                pltpu.make_async_remote_copy(src_ref=got, dst_ref=got, send_sem=fsend.at[m, j], recv_sem=frecv.at[m, j],
                                             device_id=sibling, device_id_type=MESH).wait_recv()
        for w in pend:
            w()

    sem = lambda *s: pltpu.SemaphoreType.DMA(s)
    res = pl.pallas_call(
        body, name="gather_weights",
        out_shape=[jax.ShapeDtypeStruct(shape, BF16) for _, _, shape in BIG]
        + [jax.ShapeDtypeStruct(shards[m].shape, BF16) for m in DIRECT]
        + [jax.ShapeDtypeStruct((N_DEV * first.shape[0], first.shape[1]), F32)],
        in_specs=[pl.BlockSpec(memory_space=pltpu.VMEM)] * (nm + 1),
        out_specs=[ANY] * (nm + nl) + [pl.BlockSpec(memory_space=pltpu.VMEM)],
        scratch_shapes=[pltpu.VMEM(s.shape, BF16) for s in shards] + [sem(nm, 3), sem(nm, 3), sem(nm, 3), sem(nm, 3),
                                                                         sem(nm, 3), sem(7), sem(7), sem()],
        compiler_params=_params(),
    )(*shards, first)
    return res[:nm], dict(zip(DIRECT, res[nm:nm + nl])), res[nm + nl]


def _rs1_sibling(grads):
    mats = [BIG[m] for m in EARLY]
    nm = len(mats)

    def body(*refs):
        ins, outs = refs[:nm], refs[nm:2 * nm]
        send, recv = refs[2 * nm:]
        x, y, c, _ = _place()
        cps = []
        for m, (_, kind, shape) in enumerate(mats):
            for s in range(N_CHIP):
                cp = pltpu.make_async_remote_copy(
                    src_ref=_piece_at(ins[m], kind, shape, s, 1 - c), dst_ref=outs[m].at[s],
                    send_sem=send.at[m, s], recv_sem=recv.at[m, s], device_id=(x, y, 1 - c), device_id_type=MESH)
                cp.start()
                cps.append(cp)
        for cp in cps:
            cp.wait()

    sem = lambda *s: pltpu.SemaphoreType.DMA(s)
    return pl.pallas_call(
        body, name="rs1_sibling",
        out_shape=[jax.ShapeDtypeStruct((N_CHIP,) + _piece(kind, shape), BF16) for _, kind, shape in mats],
        in_specs=[ANY] * nm, out_specs=[ANY] * nm,
        scratch_shapes=[sem(nm, N_CHIP), sem(nm, N_CHIP)],
        compiler_params=_params(),
    )(*grads)


def _rs2_chips(grads, halves):
    mats = [BIG[m] for m in EARLY]
    nm = len(mats)

    def body(*refs):
        gin, hin = refs[:nm], refs[nm:2 * nm]
        own, got = refs[2 * nm:3 * nm], refs[3 * nm:4 * nm]
        send, recv, lsem = refs[4 * nm:]
        x, y, c, chips = _place()
        s_me = 2 * x + y
        for m, (_, kind, shape) in enumerate(mats):
            pr, pc = _piece(kind, shape)

            def scoped(a, b, m=m, kind=kind, shape=shape, pr=pr):
                loads = [pltpu.make_async_copy(_piece_at(gin[m], kind, shape, s, c), a.at[s], lsem.at[s])
                         for s in range(N_CHIP)]
                loads.append(pltpu.make_async_copy(hin[m], b, lsem.at[N_CHIP]))
                for cp in loads:
                    cp.start()
                for cp in loads:
                    cp.wait()
                step = _add_rows(pr)
                for s in range(N_CHIP):
                    def add(i, _, s=s):
                        r = pl.ds(pl.multiple_of(i * step, 16), step)
                        a[s, r, :] = (a[s, r, :].astype(F32) + b[s, r, :].astype(F32)).astype(BF16)
                        return 0

                    lax.fori_loop(0, pr // step, add, 0)
                waits = []
                for j, (cx, cy) in enumerate(chips):
                    cp = pltpu.make_async_remote_copy(src_ref=a.at[2 * cx + cy], dst_ref=got[m].at[j], send_sem=send.at[m, j],
                                                      recv_sem=recv.at[m, j], device_id=(cx, cy, c), device_id_type=MESH)
                    cp.start()
                    waits.append(cp.wait_send)
                cp = pltpu.make_async_copy(a.at[s_me], own[m], lsem.at[N_CHIP + 1])
                cp.start()
                waits.append(cp.wait)
                for w in waits:
                    w()

            pl.run_scoped(scoped, pltpu.VMEM((N_CHIP, pr, pc), BF16), pltpu.VMEM((N_CHIP, pr, pc), BF16))
        for m in range(nm):
            for j, (cx, cy) in enumerate(chips):
                pltpu.make_async_remote_copy(src_ref=got[m].at[j], dst_ref=got[m].at[j], send_sem=send.at[m, j],
                                             recv_sem=recv.at[m, j], device_id=(cx, cy, c), device_id_type=MESH).wait_recv()

    sem = lambda *s: pltpu.SemaphoreType.DMA(s)
    pieces = [_piece(kind, shape) for _, kind, shape in mats]
    return pl.pallas_call(
        body, name="rs2_chips",
        out_shape=[jax.ShapeDtypeStruct(p, BF16) for p in pieces] + [jax.ShapeDtypeStruct((3,) + p, BF16) for p in pieces],
        in_specs=[ANY] * (2 * nm), out_specs=[ANY] * (2 * nm),
        scratch_shapes=[sem(nm, 3), sem(nm, 3), sem(N_CHIP + 2)],
        compiler_params=_params(),
    )(*grads, *halves)


def _rs3_finish(own, got):
    nm = len(BIG)

    def body(*refs):
        oin, gin = refs[:nm], refs[nm:2 * nm]
        outs = refs[2 * nm:3 * nm]
        send, recv, lsem = refs[3 * nm:]
        x, y, c, _ = _place()
        for m, (_, kind, shape) in enumerate(BIG):
            pr, pc = _piece(kind, shape)
            ng = got[m].shape[0]

            def scoped(a, g, f, m=m, pr=pr, ng=ng, kind=kind, shape=shape):
                mine = _piece_at(oin[m], kind, shape, 2 * x + y, c) if m in DIRECT else oin[m]
                loads = [pltpu.make_async_copy(mine, a, lsem.at[0]), pltpu.make_async_copy(gin[m], g, lsem.at[1])]
                for cp in loads:
                    cp.start()
                for cp in loads:
                    cp.wait()
                step = _add_rows(pr)

                def add(i, _):
                    r = pl.ds(pl.multiple_of(i * step, 16), step)
                    acc = a[r, :].astype(F32)
                    for q in range(ng):
                        acc = acc + g[q, r, :].astype(F32)
                    f[r, :] = acc
                    return 0

                lax.fori_loop(0, pr // step, add, 0)
                dst = outs[m].at[pl.ds(pl.multiple_of(c * pr, 8), pr), :]
                local = pltpu.make_async_copy(f, dst, lsem.at[2])
                local.start()
                cp = pltpu.make_async_remote_copy(src_ref=f, dst_ref=dst, send_sem=send.at[m], recv_sem=recv.at[m],
                                                  device_id=(x, y, 1 - c), device_id_type=MESH)
                cp.start()
                cp.wait_send()
                local.wait()

            pl.run_scoped(scoped, pltpu.VMEM((pr, pc), BF16), pltpu.VMEM((ng, pr, pc), BF16), pltpu.VMEM((pr, pc), F32))
        for m, (_, kind, shape) in enumerate(BIG):
            pr, pc = _piece(kind, shape)
            dst = outs[m].at[pl.ds(pl.multiple_of((1 - c) * pr, 8), pr), :]
            pltpu.make_async_remote_copy(src_ref=dst, dst_ref=dst, send_sem=send.at[m], recv_sem=recv.at[m],
                                         device_id=(x, y, 1 - c), device_id_type=MESH).wait_recv()

    sem = lambda *s: pltpu.SemaphoreType.DMA(s)
    pieces = [_piece(kind, shape) for _, kind, shape in BIG]
    return pl.pallas_call(
        body, name="rs3_finish",
        out_shape=[jax.ShapeDtypeStruct((2 * pr, pc), F32) for pr, pc in pieces],
        in_specs=[ANY] * (2 * nm), out_specs=[ANY] * nm,
        scratch_shapes=[sem(nm), sem(nm), sem(3)],
        compiler_params=_params(),
    )(*own, *got)


def _cond_fwd(c_all, w_shard, b_shard):
    def body(c_ref, w_ref, b_ref, act_ref, mod_ref):
        cv = c_ref[...]
        act = cv * _sig(cv)
        act_ref[...] = act
        mod_ref[...] = _dot(act.astype(BF16), w_ref[...].astype(BF16)) + b_ref[...]

    return pl.pallas_call(
        body, name="cond_fwd",
        out_shape=[jax.ShapeDtypeStruct(c_all.shape, F32), jax.ShapeDtypeStruct((c_all.shape[0], w_shard.shape[1]), F32)],
        compiler_params=_params(),
    )(c_all, w_shard, b_shard)


def _cond_bwd(act_t, dmod_shard):
    k, n = act_t.shape[0], dmod_shard.shape[1]

    def body(a_ref, d_ref, o_ref):
        acc = a_ref[:, 0:1] * d_ref[0:1, :]
        for e in range(1, N_DEV):
            acc += a_ref[:, e:e + 1] * d_ref[e:e + 1, :]
        o_ref[...] = acc

    tr = 256
    return pl.pallas_call(
        body, name="cond_bwd", grid=(k // tr,),
        in_specs=[pl.BlockSpec((tr, N_DEV), lambda i: (i, 0)), _full(dmod_shard)],
        out_specs=pl.BlockSpec((tr, n), lambda i: (i, 0)),
        out_shape=jax.ShapeDtypeStruct((k, n), F32),
        compiler_params=_params(("arbitrary",)),
    )(act_t, dmod_shard)


RAW_ROWS = 8


def _allreduce_small(pack):
    rows = PACK_ROWS // N_DEV

    def body(x_ref, sum_ref, raw_ref, buf, s1, r1, s2, r2, s3, r3):
        x, y, c, _ = _place()
        me = 4 * x + 2 * y + c
        peers = []
        for r in range(1, N_DEV):
            tx, ty, tc = (1 - x if r & 4 else x), (1 - y if r & 2 else y), (1 - c if r & 1 else c)
            peers.append(((tx, ty, tc), 4 * tx + 2 * ty + tc))
        chunk = lambda ref, d: ref.at[pl.ds(pl.multiple_of(d * rows, 8), rows), :]
        mine_raw = raw_ref.at[pl.ds(pl.multiple_of(me * RAW_ROWS, 8), RAW_ROWS), :]
        first = []
        for q, (dev, pd) in enumerate(peers):
            first.append(pltpu.make_async_remote_copy(src_ref=chunk(x_ref, pd), dst_ref=buf.at[q], send_sem=s1.at[q],
                                                      recv_sem=r1.at[q], device_id=dev, device_id_type=MESH))
            first.append(pltpu.make_async_remote_copy(src_ref=x_ref.at[0:RAW_ROWS, :], dst_ref=mine_raw, send_sem=s3.at[q],
                                                      recv_sem=r3.at[q], device_id=dev, device_id_type=MESH))
        for cp in first:
            cp.start()
        raw_ref[pl.ds(pl.multiple_of(me * RAW_ROWS, 8), RAW_ROWS), :] = x_ref[0:RAW_ROWS, :]
        for q, (dev, pd) in enumerate(peers):
            first[2 * q].wait()
        acc = x_ref[pl.ds(pl.multiple_of(me * rows, 8), rows), :]
        for q in range(N_DEV - 1):
            acc = acc + buf[q]
        sum_ref[pl.ds(pl.multiple_of(me * rows, 8), rows), :] = acc
        second = [pltpu.make_async_remote_copy(src_ref=chunk(sum_ref, me), dst_ref=chunk(sum_ref, me), send_sem=s2.at[q],
                                               recv_sem=r2.at[q], device_id=dev, device_id_type=MESH)
                  for q, (dev, pd) in enumerate(peers)]
        for cp in second:
            cp.start()
        for q, (dev, pd) in enumerate(peers):
            pltpu.make_async_remote_copy(src_ref=chunk(sum_ref, pd), dst_ref=chunk(sum_ref, pd), send_sem=s2.at[q],
                                         recv_sem=r2.at[q], device_id=dev, device_id_type=MESH).wait()
            pltpu.make_async_remote_copy(src_ref=x_ref.at[0:RAW_ROWS, :],
                                         dst_ref=raw_ref.at[pl.ds(pl.multiple_of(pd * RAW_ROWS, 8), RAW_ROWS), :],
                                         send_sem=s3.at[q], recv_sem=r3.at[q], device_id=dev, device_id_type=MESH).wait()

    sem = pltpu.SemaphoreType.DMA((N_DEV - 1,))
    return pl.pallas_call(
        body, name="allreduce_small",
        out_shape=[jax.ShapeDtypeStruct((PACK_ROWS, PACK_COLS), F32), jax.ShapeDtypeStruct((N_DEV * RAW_ROWS, PACK_COLS), F32)],
        in_specs=[pl.BlockSpec(memory_space=pltpu.VMEM)],
        out_specs=[pl.BlockSpec(memory_space=pltpu.VMEM)] * 2,
        scratch_shapes=[pltpu.VMEM((N_DEV - 1, rows, PACK_COLS), F32), sem, sem, sem, sem, sem, sem],
        compiler_params=_params(),
    )(pack)


def _adamw(name, w, g, m, v):
    r, cc = w.shape
    tr = r
    for cand in (256, 128, 64, 32, 16, 8):
        if r % cand == 0:
            tr = cand
            break
    bc1 = 1.0 - ADAM_B1 ** ADAM_STEP
    bc2 = 1.0 - ADAM_B2 ** ADAM_STEP

    def body(w_ref, g_ref, m_ref, v_ref, d_ref, nm_ref, nv_ref):
        gv = g_ref[...]
        nm = ADAM_B1 * m_ref[...] + (1.0 - ADAM_B1) * gv
        nv = ADAM_B2 * v_ref[...] + (1.0 - ADAM_B2) * (gv * gv)
        nm_ref[...] = nm
        nv_ref[...] = nv
        d_ref[...] = -ADAM_LR * ((nm / bc1) / (jnp.sqrt(nv / bc2) + ADAM_EPS) + ADAM_WD * w_ref[...])

    spec = pl.BlockSpec((tr, cc), lambda i: (i, 0))
    return pl.pallas_call(
        body, name=name, grid=(r // tr,), in_specs=[spec] * 4, out_specs=[spec] * 3,
        out_shape=[jax.ShapeDtypeStruct((r, cc), F32)] * 3, compiler_params=_params(("arbitrary",)),
    )(w, g, m, v)


def _adamw_small(ws, gs, ms, vs):
    n = len(ws)
    bc1 = 1.0 - ADAM_B1 ** ADAM_STEP
    bc2 = 1.0 - ADAM_B2 ** ADAM_STEP

    def body(*refs):
        w, g, m, v = (refs[k * n:(k + 1) * n] for k in range(4))
        d, nm, nv = (refs[(4 + k) * n:(5 + k) * n] for k in range(3))
        for i in range(n):
            gv = g[i][...]
            m1 = ADAM_B1 * m[i][...] + (1.0 - ADAM_B1) * gv
            v1 = ADAM_B2 * v[i][...] + (1.0 - ADAM_B2) * (gv * gv)
            nm[i][...] = m1
            nv[i][...] = v1
            d[i][...] = -ADAM_LR * ((m1 / bc1) / (jnp.sqrt(v1 / bc2) + ADAM_EPS) + ADAM_WD * w[i][...])

    shapes = [jax.ShapeDtypeStruct(x.shape, F32) for x in ws]
    res = pl.pallas_call(body, name="adamw_small", out_shape=shapes * 3, compiler_params=_params())(*ws, *gs, *ms, *vs)
    return res[:n], res[n:2 * n], res[2 * n:]


def _pack(fields, layout):
    parts = [fields[name].reshape(-1).astype(F32) if name in fields else jnp.zeros((n,), F32) for name, n in layout]
    used = sum(n for _, n in layout)
    parts.append(jnp.zeros((PACK_ROWS * PACK_COLS - used,), F32))
    return jnp.concatenate(parts).reshape(PACK_ROWS, PACK_COLS)


def _unpack(flat, layout):
    flat = flat.reshape(-1)
    out, o = {}, 0
    for name, n in layout:
        out[name] = flat[o:o + n]
        o += n
    return out


def kernel(x, c, w_cond, b_cond, w_in, b_in, ssm_lambda_re, ssm_lambda_im, ssm_log_dt, ssm_b_re, ssm_b_im, ssm_c_re, ssm_c_im, ssm_d, ssm_glu_w_a, ssm_glu_w_b, cv_dw_w, cv_dw_b, cv_ln_g, cv_ln_b, cv_w_pw, w_out, ln1_g, ln1_b, ffn_w_up, ffn_dw_w, ffn_dw_b, ffn_w_down, ln2_g, ln2_b, loss_target, m_w_cond, m_b_cond, m_w_in, m_b_in, m_ssm_lambda_re, m_ssm_lambda_im, m_ssm_log_dt, m_ssm_b_re, m_ssm_b_im, m_ssm_c_re, m_ssm_c_im, m_ssm_d, m_ssm_glu_w_a, m_ssm_glu_w_b, m_cv_dw_w, m_cv_dw_b, m_cv_ln_g, m_cv_ln_b, m_cv_w_pw, m_w_out, m_ln1_g, m_ln1_b, m_ffn_w_up, m_ffn_dw_w, m_ffn_dw_b, m_ffn_w_down, m_ln2_g, m_ln2_b, v_w_cond, v_b_cond, v_w_in, v_b_in, v_ssm_lambda_re, v_ssm_lambda_im, v_ssm_log_dt, v_ssm_b_re, v_ssm_b_im, v_ssm_c_re, v_ssm_c_im, v_ssm_d, v_ssm_glu_w_a, v_ssm_glu_w_b, v_cv_dw_w, v_cv_dw_b, v_cv_ln_g, v_cv_ln_b, v_cv_w_pw, v_w_out, v_ln1_g, v_ln1_b, v_ffn_w_up, v_ffn_dw_w, v_ffn_dw_b, v_ffn_w_down, v_ln2_g, v_ln2_b):
    given = locals()
    a = {n: given[n] for n in INPUTS}
    xi, yi, ci = lax.axis_index("x"), lax.axis_index("y"), lax.axis_index("c")
    s_me = 2 * xi + yi
    e_me = 4 * xi + 2 * yi + ci

    first = jnp.concatenate([
        jnp.concatenate([a["c"], jnp.zeros((7, D_MODEL), F32)], axis=0),
        jnp.concatenate([a["cv_dw_w"].reshape(-1), a["ffn_dw_w"].reshape(-1)]).reshape(8, D_MODEL)], axis=0)
    full, shards, first_all = _gather_weights([a[n][0] for n, _, _ in BIG], first)
    wb = dict(zip([n for n, _, _ in BIG], full))
    first_all = first_all.reshape(N_DEV, 16, D_MODEL)
    c_all = first_all[:, 0, :]
    dw_all = first_all[0::2, 8:, :].reshape(N_CHIP, 8 * D_MODEL)
    n_cv = CONV_KERNEL * CONV_WIDTH // N_CHIP
    cv_dw_full = dw_all[:, :n_cv].reshape(N_CHIP, CONV_KERNEL, CONV_WIDTH // N_CHIP).transpose(1, 0, 2) \
        .reshape(CONV_KERNEL, CONV_WIDTH)
    ffn_dw_full = dw_all[:, n_cv:].reshape(N_CHIP, FFN_KERNEL, 2 * FFN_HIDDEN // N_CHIP).transpose(1, 0, 2) \
        .reshape(FFN_KERNEL, 2 * FFN_HIDDEN)
    ncols = N_COND * D_MODEL // N_CHIP
    b_cond_shard = lax.dynamic_slice(a["b_cond"], (0, s_me * ncols), (1, ncols))
    c_act_all, modp = _cond_fwd(c_all, a["w_cond"][0], b_cond_shard)
    modp_all = _allgather("gather_mod", modp).reshape(N_DEV, N_DEV, ncols)[0::2]
    mod_e = lax.dynamic_index_in_dim(modp_all, e_me, axis=1, keepdims=False).reshape(N_COND, D_MODEL)
    modv = jnp.concatenate([mod_e, jnp.zeros((2, D_MODEL), F32)], axis=0)

    sp = {n: a[n][0] for n in ("b_in", "ssm_lambda_re", "ssm_lambda_im", "ssm_log_dt", "ssm_b_re", "ssm_b_im",
                               "ssm_c_re", "ssm_c_im", "ssm_d", "cv_dw_b", "cv_ln_g", "cv_ln_b", "ln1_g", "ln1_b",
                               "ffn_dw_b", "ln2_g", "ln2_b")}
    sp["cv_dw_w"] = cv_dw_full
    sp["ffn_dw_w"] = ffn_dw_full
    gx, dbig, direct_got, small = _local_step(a["x"][0], a["loss_target"][0], modv, wb, shards, sp)

    tot_pack, raw_all = _allreduce_small(_pack(small, PACK))
    tot = _unpack(tot_pack, PACK)
    dmod_all = raw_all.reshape(N_DEV, RAW_ROWS * PACK_COLS)[:, 0:N_COND * D_MODEL]
    g_w_cond = _cond_bwd(c_act_all.T, lax.dynamic_slice(dmod_all, (0, s_me * ncols), (N_DEV, ncols)))

    glist = [dbig[m] for m in EARLY]
    halves = _rs1_sibling(glist)
    r2 = _rs2_chips(glist, halves)
    gsh = _rs3_finish(list(r2[:len(EARLY)]) + [dbig[m] for m in DIRECT], list(r2[len(EARLY):]) + direct_got)

    grads = {"w_cond": g_w_cond[None], "b_cond": tot["dmod"].reshape(1, -1)}
    for (n, kind, shape), g in zip(BIG, gsh):
        grads[n] = g.reshape(a[n].shape)
    for n in ("b_in", "ssm_lambda_re", "ssm_lambda_im", "ssm_log_dt", "ssm_b_re", "ssm_b_im", "ssm_c_re", "ssm_c_im",
              "ssm_d", "cv_dw_b", "cv_ln_g", "cv_ln_b", "ln1_g", "ln1_b", "ffn_dw_b", "ln2_g", "ln2_b"):
        grads[n] = tot[n].reshape(a[n].shape)
    wcv = CONV_WIDTH // N_CHIP
    grads["cv_dw_w"] = lax.dynamic_slice(tot["cv_dw_w"].reshape(CONV_KERNEL, CONV_WIDTH), (0, s_me * wcv),
                                         (CONV_KERNEL, wcv)).reshape(a["cv_dw_w"].shape)
    wff = 2 * FFN_HIDDEN // N_CHIP
    grads["ffn_dw_w"] = lax.dynamic_slice(tot["ffn_dw_w"].reshape(FFN_KERNEL, 2 * FFN_HIDDEN), (0, s_me * wff),
                                          (FFN_KERNEL, wff)).reshape(a["ffn_dw_w"].shape)

    delta, new_m, new_v = {}, {}, {}
    for n in ["w_cond"] + [n for n, _, _ in BIG]:
        d, nm_, nv_ = _adamw("adamw_" + n, a[n][0], grads[n][0], a["m_" + n][0], a["v_" + n][0])
        delta[n], new_m[n], new_v[n] = d[None], nm_[None], nv_[None]
    upd = [n for n in WEIGHTS if n not in delta]
    two_d = lambda t: t.reshape(-1, t.shape[-1])
    outs = _adamw_small([two_d(a[n]) for n in upd], [two_d(grads[n]) for n in upd],
                        [two_d(a["m_" + n]) for n in upd], [two_d(a["v_" + n]) for n in upd])
    for dst, vals in zip((delta, new_m, new_v), outs):
        for n, val in zip(upd, vals):
            dst[n] = val.reshape(a[n].shape)

    loss = tot["loss"].reshape(())
    return (loss, gx[None], *[grads[n] for n in WEIGHTS], *[delta[n] for n in WEIGHTS],
            *[new_m[n] for n in WEIGHTS], *[new_v[n] for n in WEIGHTS])
```

```python
import functools
import math

import jax
import jax.numpy as jnp
from jax import lax
from jax.experimental import pallas as pl
from jax.experimental.pallas import tpu as pltpu

F32 = jnp.float32
BF16 = jnp.bfloat16

D_MODEL = 1024
SSM_WIDTH = 512
SSM_GROUP = 16
SSM_GROUPS = 32
SSM_STATE = 64
CONV_WIDTH = 512
CONV_KERNEL = 31
FFN_HIDDEN = 2816
FFN_KERNEL = 3
IN_PROJ_WIDTH = 3584
N_COND = 6
ALPHA = 2.0 ** 0.25
LN_EPS = 1e-5
ADAM_LR, ADAM_B1, ADAM_B2, ADAM_EPS, ADAM_WD, ADAM_STEP = 0.001, 0.9, 0.999, 1e-08, 0.01, 10

N_DEV = 8
N_CHIP = 4
LANES = 128
SSM_CHUNK = 16
LANE_GROUPS = LANES // SSM_GROUP
N_LANE_BLOCKS = SSM_WIDTH // LANES
STATE_COLS = LANE_GROUPS * SSM_STATE
CHUNK_COLS = SSM_CHUNK * LANES
CONV_HALO = 32
VMEM_LIMIT = 56 * 1024 * 1024
MESH = pl.DeviceIdType.MESH

BIG = (
    ("w_in", "col", (D_MODEL, IN_PROJ_WIDTH)),
    ("ssm_glu_w_a", "col", (SSM_WIDTH, D_MODEL)),
    ("ssm_glu_w_b", "col", (SSM_WIDTH, D_MODEL)),
    ("cv_w_pw", "col", (CONV_WIDTH, D_MODEL)),
    ("w_out", "row", (D_MODEL, D_MODEL)),
    ("ffn_w_up", "col", (D_MODEL, 2 * FFN_HIDDEN)),
    ("ffn_w_down", "row", (FFN_HIDDEN, D_MODEL)),
)

EARLY = (0,)
MID = (1, 2, 3, 4)
LATE = (5, 6)
DIRECT = MID + LATE

WEIGHTS = ['w_cond', 'b_cond', 'w_in', 'b_in', 'ssm_lambda_re', 'ssm_lambda_im', 'ssm_log_dt', 'ssm_b_re', 'ssm_b_im',
           'ssm_c_re', 'ssm_c_im', 'ssm_d', 'ssm_glu_w_a', 'ssm_glu_w_b', 'cv_dw_w', 'cv_dw_b', 'cv_ln_g', 'cv_ln_b',
           'cv_w_pw', 'w_out', 'ln1_g', 'ln1_b', 'ffn_w_up', 'ffn_dw_w', 'ffn_dw_b', 'ffn_w_down', 'ln2_g', 'ln2_b']
INPUTS = ['x', 'c'] + WEIGHTS + ['loss_target'] + ['m_' + n for n in WEIGHTS] + ['v_' + n for n in WEIGHTS]

PACK = (
    ("dmod", N_COND * D_MODEL), ("c_act", D_MODEL), ("b_in", IN_PROJ_WIDTH),
    ("ssm_lambda_re", SSM_GROUPS * SSM_STATE), ("ssm_lambda_im", SSM_GROUPS * SSM_STATE), ("ssm_log_dt", SSM_GROUPS),
    ("ssm_b_re", SSM_GROUPS * SSM_STATE * SSM_GROUP), ("ssm_b_im", SSM_GROUPS * SSM_STATE * SSM_GROUP),
    ("ssm_c_re", SSM_GROUPS * SSM_STATE * SSM_GROUP), ("ssm_c_im", SSM_GROUPS * SSM_STATE * SSM_GROUP),
    ("ssm_d", SSM_GROUPS * SSM_GROUP), ("cv_dw_w", CONV_KERNEL * CONV_WIDTH), ("cv_dw_b", CONV_WIDTH),
    ("cv_ln_g", CONV_WIDTH), ("cv_ln_b", CONV_WIDTH), ("ln1_g", D_MODEL), ("ln1_b", D_MODEL),
    ("ffn_dw_w", FFN_KERNEL * 2 * FFN_HIDDEN), ("ffn_dw_b", 2 * FFN_HIDDEN), ("ln2_g", D_MODEL), ("ln2_b", D_MODEL),
    ("loss", 1),
)
PACK_COLS = 1024
PACK_ROWS = 192
assert sum(n for _, n in PACK) <= PACK_ROWS * PACK_COLS


def _params(sem=None, **kw):
    return pltpu.CompilerParams(dimension_semantics=sem, vmem_limit_bytes=VMEM_LIMIT, **kw)


def _ln_stats(x):
    mu = jnp.mean(x, axis=-1, keepdims=True)
    xc = x - mu
    var = jnp.mean(xc * xc, axis=-1, keepdims=True)
    rstd = lax.rsqrt(var + LN_EPS)
    return xc * rstd, rstd


def _ln_bwd(dxhat, xhat, rstd):
    m1 = jnp.mean(dxhat, axis=-1, keepdims=True)
    m2 = jnp.mean(dxhat * xhat, axis=-1, keepdims=True)
    return rstd * (dxhat - m1 - xhat * m2)


def _sig(x):
    return 1.0 / (1.0 + jnp.exp(-x))


def _gelu(x):
    return 0.5 * x * (1.0 + lax.erf(x * (1.0 / math.sqrt(2.0))))


def _dgelu(x):
    return 0.5 * (1.0 + lax.erf(x * (1.0 / math.sqrt(2.0)))) + x * jnp.exp(-0.5 * x * x) * (1.0 / math.sqrt(2.0 * math.pi))


def _gelu_and_grad(x):
    er = lax.erf(x * (1.0 / math.sqrt(2.0)))
    cdf = 0.5 * (1.0 + er)
    return x * cdf, cdf + x * jnp.exp(-0.5 * x * x) * (1.0 / math.sqrt(2.0 * math.pi))


def _colsum(a):
    return jnp.sum(a, axis=0, keepdims=True)


def _fill_rotations(buf, rot, rows):
    for r in range(1, 8):
        rot[r - 1] = buf[pl.ds(r, rows), :]


def _rows_at(buf, rot, offset, tb):
    q, r = divmod(offset, 8)
    if r == 0:
        return buf[pl.ds(8 * q, tb), :]
    return rot[r - 1, pl.ds(8 * q, tb), :]


def _dot(a, b):
    return jnp.dot(a, b, preferred_element_type=F32)


def _dot_nt(a, b):
    return lax.dot_general(a, b, (((1,), (1,)), ((), ())), preferred_element_type=F32)


def _dot_tn(a, b):
    return lax.dot_general(a, b, (((0,), (0,)), ((), ())), preferred_element_type=F32)


def _load_once(src, dst, sem):
    cp = pltpu.make_async_copy(src, dst, sem)
    cp.start()
    cp.wait()


def _full(a):
    nd = a.ndim
    return pl.BlockSpec(a.shape, lambda *_: (0,) * nd)


ANY = pl.BlockSpec(memory_space=pl.ANY)


def _place():
    x, y, c = lax.axis_index("x"), lax.axis_index("y"), lax.axis_index("c")
    chips = [(1 - x, y), (x, 1 - y), (1 - x, 1 - y)]
    return x, y, c, chips


def _piece(kind, shape):
    r, cc = shape
    return (r // 2, cc // N_CHIP) if kind == "col" else (r // (2 * N_CHIP), cc)


def _piece_at(ref, kind, shape, s, k):
    pr, pc = _piece(kind, shape)
    if kind == "col":
        return ref.at[pl.ds(k * pr, pr), pl.ds(pl.multiple_of(s * pc, LANES), pc)]
    return ref.at[pl.ds(pl.multiple_of((2 * s + k) * pr, 16), pr), :]


def _gather_start(idx, sh, full, send, recv):
    x, y, c, chips = _place()
    for i, m in enumerate(idx):
        _, kind, shape = BIG[m]
        pr, _ = _piece(kind, shape)
        for j, chip in enumerate(chips):
            pltpu.make_async_remote_copy(
                src_ref=sh[i].at[pl.ds(pl.multiple_of(c * pr, 16), pr), :], dst_ref=_piece_at(full[i], kind, shape, 2 * x + y, c),
                send_sem=send.at[i, j], recv_sem=recv.at[i, j], device_id=(*chip, c), device_id_type=MESH).start()


def _gather_finish(idx, sh, full, send, recv, fsend, frecv):
    x, y, c, chips = _place()
    sibling = (x, y, 1 - c)
    waits = []
    for i, m in enumerate(idx):
        _, kind, shape = BIG[m]
        pr, _ = _piece(kind, shape)
        for j, (cx, cy) in enumerate(chips):
            got = _piece_at(full[i], kind, shape, 2 * cx + cy, c)
            first = pltpu.make_async_remote_copy(
                src_ref=sh[i].at[pl.ds(pl.multiple_of(c * pr, 16), pr), :], dst_ref=got, send_sem=send.at[i, j],
                recv_sem=recv.at[i, j], device_id=(cx, cy, c), device_id_type=MESH)
            first.wait_recv()
            fwd = pltpu.make_async_remote_copy(src_ref=got, dst_ref=got, send_sem=fsend.at[i, j], recv_sem=frecv.at[i, j],
                                               device_id=sibling, device_id_type=MESH)
            fwd.start()
            waits += [first.wait_send, fwd.wait_send]
    for i, m in enumerate(idx):
        _, kind, shape = BIG[m]
        for j, (cx, cy) in enumerate(chips):
            got = _piece_at(full[i], kind, shape, 2 * cx + cy, 1 - c)
            pltpu.make_async_remote_copy(src_ref=got, dst_ref=got, send_sem=fsend.at[i, j], recv_sem=frecv.at[i, j],
                                         device_id=sibling, device_id_type=MESH).wait_recv()
    for w in waits:
        w()


def _scatter(idx, dw, got, send, recv):
    x, y, c, _ = _place()
    cps = []
    for i, m in enumerate(idx):
        _, kind, shape = BIG[m]
        for r in range(1, N_DEV):
            tx, ty, tc = (1 - x if r & 4 else x), (1 - y if r & 2 else y), (1 - c if r & 1 else c)
            cps.append(pltpu.make_async_remote_copy(
                src_ref=_piece_at(dw[i], kind, shape, 2 * tx + ty, tc), dst_ref=got[i].at[r - 1],
                send_sem=send.at[i, r - 1], recv_sem=recv.at[i, r - 1], device_id=(tx, ty, tc), device_id_type=MESH))
    return cps


def _f1_inproj(x, modv, b_in, w_in, mid_sh, mid_full, tb):
    t = x.shape[0]
    nt = t // tb
    nl = len(MID)
    chunks = [(j * 512, 512) for j in range(IN_PROJ_WIDTH // 512)]

    def body(x_ref, modv_ref, b_ref, w_hbm, *rest):
        sh, full = rest[:nl], rest[2 * nl:3 * nl]
        u4_ref, prest_ref, h_ref, w_v, sem, send, recv, fsend, frecv = rest[3 * nl:]

        @pl.when(pl.program_id(0) == 0)
        def _():
            _gather_start(MID, sh, full, send, recv)
            _load_once(w_hbm, w_v, sem)

        xn, _ = _ln_stats(x_ref[...])
        h = (xn * (1.0 + modv_ref[1:2, :]) + modv_ref[0:1, :]).astype(BF16)
        h_ref[...] = h
        for c0, cw in chunks:
            p = _dot(h, w_v[:, c0:c0 + cw]) + b_ref[:, c0:c0 + cw]
            if c0 == 0:
                for b in range(N_LANE_BLOCKS):
                    u4_ref[b] = p[:, b * LANES:(b + 1) * LANES]
            else:
                prest_ref[:, c0 - SSM_WIDTH:c0 - SSM_WIDTH + cw] = p

        @pl.when(pl.program_id(0) == nt - 1)
        def _():
            _gather_finish(MID, sh, full, send, recv, fsend, frecv)

    gsem = pltpu.SemaphoreType.DMA((nl, 3))
    return pl.pallas_call(
        body, name="f1_inproj", grid=(nt,),
        in_specs=[pl.BlockSpec((tb, D_MODEL), lambda i: (i, 0)), _full(modv), _full(b_in), ANY] + [ANY] * (2 * nl),
        out_specs=[ANY] * nl + [pl.BlockSpec((N_LANE_BLOCKS, tb, LANES), lambda i: (0, i, 0)),
                                pl.BlockSpec((tb, IN_PROJ_WIDTH - SSM_WIDTH), lambda i: (i, 0)),
                                pl.BlockSpec((tb, D_MODEL), lambda i: (i, 0))],
        input_output_aliases={4 + nl + k: k for k in range(nl)},
        out_shape=[jax.ShapeDtypeStruct(f.shape, f.dtype) for f in mid_full]
        + [jax.ShapeDtypeStruct((N_LANE_BLOCKS, t, LANES), F32),
                   jax.ShapeDtypeStruct((t, IN_PROJ_WIDTH - SSM_WIDTH), F32),
                   jax.ShapeDtypeStruct((t, D_MODEL), BF16)],
        scratch_shapes=[pltpu.VMEM(w_in.shape, BF16), pltpu.SemaphoreType.DMA, gsem, gsem, gsem, gsem],
        compiler_params=_params(("arbitrary",)),
    )(x, modv, b_in, w_in, *mid_sh, *mid_full)


TAP_GROUPS = 8


def _dot_f32(a, b, dims):
    return lax.dot_general(a, b, (dims, ((), ())), precision=lax.Precision.HIGHEST, preferred_element_type=F32)


def _taps_fwd(car, cai, bt_r, bt_i):
    el, g, p, n = SSM_CHUNK, SSM_GROUPS, SSM_GROUP, SSM_STATE

    def body(ar_ref, ai_ref, br_ref, bi_ref, o_ref):
        for gl in range(TAP_GROUPS):
            a_r = jnp.concatenate([ar_ref[k, gl] for k in range(el)], axis=0)
            a_i = jnp.concatenate([ai_ref[k, gl] for k in range(el)], axis=0)
            o_ref[gl] = _dot_f32(br_ref[gl], a_r, ((1,), (1,))) - _dot_f32(bi_ref[gl], a_i, ((1,), (1,)))

    ablk = pl.BlockSpec((el + 1, TAP_GROUPS, p, n), lambda i: (0, i, 0, 0))
    bblk = pl.BlockSpec((TAP_GROUPS, p, n), lambda i: (i, 0, 0))
    return pl.pallas_call(
        body, name="s5_taps", grid=(g // TAP_GROUPS,), in_specs=[ablk, ablk, bblk, bblk],
        out_specs=pl.BlockSpec((TAP_GROUPS, p, el * p), lambda i: (i, 0, 0)),
        out_shape=jax.ShapeDtypeStruct((g, p, el * p), F32), compiler_params=_params(("arbitrary",)),
    )(car, cai, bt_r, bt_i)


def _taps_bwd(dr, car, cai, bt_r, bt_i):
    el, g, p, n = SSM_CHUNK, SSM_GROUPS, SSM_GROUP, SSM_STATE

    def body(dr_ref, ar_ref, ai_ref, br_ref, bi_ref, dar_ref, dai_ref, dbr_ref, dbi_ref):
        for gl in range(TAP_GROUPS):
            dv = dr_ref[gl]
            a_r = jnp.concatenate([ar_ref[k, gl] for k in range(el)], axis=0)
            a_i = jnp.concatenate([ai_ref[k, gl] for k in range(el)], axis=0)
            dbr_ref[gl] = _dot_f32(dv, a_r, ((1,), (0,)))
            dbi_ref[gl] = -_dot_f32(dv, a_i, ((1,), (0,)))
            da_r = _dot_f32(dv, br_ref[gl], ((0,), (0,)))
            da_i = -_dot_f32(dv, bi_ref[gl], ((0,), (0,)))
            for k in range(el):
                dar_ref[k, gl] = da_r[k * p:(k + 1) * p, :]
                dai_ref[k, gl] = da_i[k * p:(k + 1) * p, :]
            dar_ref[el, gl] = jnp.zeros((p, n), F32)
            dai_ref[el, gl] = jnp.zeros((p, n), F32)

    ablk = pl.BlockSpec((el + 1, TAP_GROUPS, p, n), lambda i: (0, i, 0, 0))
    bblk = pl.BlockSpec((TAP_GROUPS, p, n), lambda i: (i, 0, 0))
    return pl.pallas_call(
        body, name="s5_taps_bwd", grid=(g // TAP_GROUPS,),
        in_specs=[pl.BlockSpec((TAP_GROUPS, p, el * p), lambda i: (i, 0, 0)), ablk, ablk, bblk, bblk],
        out_specs=[ablk, ablk, bblk, bblk],
        out_shape=[jax.ShapeDtypeStruct(car.shape, F32), jax.ShapeDtypeStruct(car.shape, F32),
                   jax.ShapeDtypeStruct(bt_r.shape, F32), jax.ShapeDtypeStruct(bt_r.shape, F32)],
        compiler_params=_params(("arbitrary",)),
    )(dr, car, cai, bt_r, bt_i)


@jax.custom_vjp
def _taps(car, cai, bt_r, bt_i):
    return _taps_fwd(car, cai, bt_r, bt_i)


_taps.defvjp(lambda *ops: (_taps_fwd(*ops), ops), lambda ops, dr: _taps_bwd(dr, *ops))


def _s5_build(lam_re, lam_im, log_dt, b_re, b_im, c_re, c_im, d):
    el, g, n, p, nb = SSM_CHUNK, SSM_GROUPS, SSM_STATE, SSM_GROUP, N_LANE_BLOCKS
    lr = jnp.minimum(lam_re, -1e-4)
    li = lam_im
    dt = jnp.exp(log_dt)[:, None]
    mag = jnp.exp(lr * dt)
    ang = li * dt
    lbr, lbi = mag * jnp.cos(ang), mag * jnp.sin(ang)
    num_r, num_i = lbr - 1.0, lbi
    den = lr * lr + li * li
    coef_r = (num_r * lr + num_i * li) / den
    coef_i = (num_i * lr - num_r * li) / den
    bbar_r = coef_r[..., None] * b_re - coef_i[..., None] * b_im
    bbar_i = coef_r[..., None] * b_im + coef_i[..., None] * b_re
    k = jnp.arange(el + 1, dtype=F32)[:, None, None]
    pmag = jnp.exp(k * (lr * dt)[None])
    pr, pi = pmag * jnp.cos(k * ang[None]), pmag * jnp.sin(k * ang[None])
    car = c_re[None] * pr[:, :, None, :] - c_im[None] * pi[:, :, None, :]
    cai = c_re[None] * pi[:, :, None, :] + c_im[None] * pr[:, :, None, :]
    bt_r = bbar_r.transpose(0, 2, 1)
    bt_i = bbar_i.transpose(0, 2, 1)
    kern = _taps(car, cai, bt_r, bt_i).reshape(g, p, el, p).transpose(2, 0, 1, 3)
    kern = kern.at[0].add(jnp.eye(p, dtype=F32)[None] * d[:, None, :])
    bt_r, bt_i = bt_r[None], bt_i[None]
    kc = kern.reshape(el, g * p, p)
    rev = el - 1 - jnp.arange(el)
    qr, qi = pr[rev][:, :, None, :], pi[rev][:, :, None, :]
    sw_r = (qr * bt_r - qi * bt_i).reshape(el, g * p, n)
    sw_i = (qr * bt_i + qi * bt_r).reshape(el, g * p, n)
    sg_r = car[1:].reshape(el, g * p, n)
    sg_i = (-cai[1:]).reshape(el, g * p, n)
    a = jnp.stack([pr[el].reshape(nb, LANE_GROUPS * n), pi[el].reshape(nb, LANE_GROUPS * n)], axis=1)
    return kc, sw_r, sw_i, sg_r, sg_i, a


def _expand(src, reps):
    rows, w = src.shape
    cols = reps * w
    r = lax.broadcasted_iota(jnp.int32, (w, cols), 0)
    c = lax.broadcasted_iota(jnp.int32, (w, cols), 1)
    rep = (r == (c & (w - 1))).astype(BF16)
    out = _dot(src.astype(BF16), rep)
    rg = lax.broadcasted_iota(jnp.int32, (rows, cols), 0) // SSM_GROUP
    cg = lax.broadcasted_iota(jnp.int32, (rows, cols), 1) // w
    return jnp.where(rg == cg, out, 0.0).astype(BF16)


def _fold(x, w):
    rows, cols = x.shape
    rg = lax.broadcasted_iota(jnp.int32, (rows, cols), 0) // SSM_GROUP
    cg = lax.broadcasted_iota(jnp.int32, (rows, cols), 1) // w
    x = jnp.where(rg == cg, x, 0.0)
    while cols > LANES:
        x = x[:, :cols // 2] + x[:, cols // 2:]
        cols //= 2
    s = LANES // 2
    while s >= w:
        x = x + pltpu.roll(x, s, axis=1)
        s //= 2
    return x[:, :w]


def _build_maps(s_ref, dst):
    for j in range(SSM_CHUNK):
        dst[j * LANES:(j + 1) * LANES, :] = _expand(s_ref[j], LANE_GROUPS)


def _build_toeplitz(kc_ref, dst):
    dst[...] = jnp.zeros_like(dst)
    for d in range(SSM_CHUNK):
        blk = _expand(kc_ref[d], LANE_GROUPS)
        for ji in range(SSM_CHUNK - d):
            jo = ji + d
            dst[ji * LANES:(ji + 1) * LANES, jo * LANES:(jo + 1) * LANES] = blk


def _cblk(w):
    return pl.BlockSpec((SSM_CHUNK, LANES, w), lambda b: (0, b, 0))


def _tblk(t):
    return pl.BlockSpec((1, t, LANES), lambda b: (b, 0, 0))


def _load_chunks(ref, nc):
    return jnp.concatenate([ref[0, pl.ds(j, nc, stride=SSM_CHUNK), :] for j in range(SSM_CHUNK)], axis=-1).astype(BF16)


def _store_chunks(ref, val, nc):
    for j in range(SSM_CHUNK):
        ref[0, pl.ds(j, nc, stride=SSM_CHUNK), :] = val[:, j * LANES:(j + 1) * LANES]


def _s5a_state(u4, sw_r, sw_i, a8):
    nb, t, _ = u4.shape
    nc = t // SSM_CHUNK
    sc = STATE_COLS

    def body(u_ref, swr_ref, swi_ref, a_ref, hr_ref, hi_ref, w_s, xr_s, xi_s):
        u = _load_chunks(u_ref, nc)
        _build_maps(swr_ref, w_s)
        xr_s[...] = _dot(u, w_s[...])
        _build_maps(swi_ref, w_s)
        xi_s[...] = _dot(u, w_s[...])
        ar = a_ref[0, 0:1, :]
        ai = a_ref[0, 1:2, :]

        def step(c, carry):
            hr, hi = carry
            hr_ref[0, pl.ds(c, 1), :] = hr
            hi_ref[0, pl.ds(c, 1), :] = hi
            xr = xr_s[pl.ds(c, 1), :]
            xi = xi_s[pl.ds(c, 1), :]
            return ar * hr - ai * hi + xr, ar * hi + ai * hr + xi

        z = jnp.zeros((1, sc), F32)
        lax.fori_loop(0, nc, step, (z, z))

    return pl.pallas_call(
        body, name="s5a_state", grid=(nb,),
        in_specs=[_tblk(t), _cblk(SSM_STATE), _cblk(SSM_STATE),
                  pl.BlockSpec((1, 8, sc), lambda b: (b, 0, 0))],
        out_specs=[pl.BlockSpec((1, nc, sc), lambda b: (b, 0, 0))] * 2,
        out_shape=[jax.ShapeDtypeStruct((nb, nc, sc), F32)] * 2,
        scratch_shapes=[pltpu.VMEM((CHUNK_COLS, sc), BF16), pltpu.VMEM((nc, sc), F32), pltpu.VMEM((nc, sc), F32)],
        compiler_params=_params(("arbitrary",)),
    )(u4, sw_r, sw_i, a8)


def _s5b_out(u4, kc, sg_r, sg_i, hr, hi):
    nb, t, _ = u4.shape
    nc = t // SSM_CHUNK
    sc = STATE_COLS
    cw = 512

    def body(u_ref, kc_ref, sgr_ref, sgi_ref, hr_ref, hi_ref, y_ref, tm_s, gr_s, gi_s):
        _build_toeplitz(kc_ref, tm_s)
        _build_maps(sgr_ref, gr_s)
        _build_maps(sgi_ref, gi_s)
        u = _load_chunks(u_ref, nc)
        h_r = hr_ref[0].astype(BF16)
        h_i = hi_ref[0].astype(BF16)
        for j in range(CHUNK_COLS // cw):
            cs = slice(j * cw, (j + 1) * cw)
            y = _dot(u, tm_s[:, cs]) + _dot_nt(h_r, gr_s[cs, :]) + _dot_nt(h_i, gi_s[cs, :])
            for q in range(cw // LANES):
                step = j * (cw // LANES) + q
                y_ref[0, pl.ds(step, nc, stride=SSM_CHUNK), :] = y[:, q * LANES:(q + 1) * LANES]

    return pl.pallas_call(
        body, name="s5b_out", grid=(nb,),
        in_specs=[_tblk(t), _cblk(SSM_GROUP), _cblk(SSM_STATE),
                  _cblk(SSM_STATE), pl.BlockSpec((1, nc, sc), lambda b: (b, 0, 0)),
                  pl.BlockSpec((1, nc, sc), lambda b: (b, 0, 0))],
        out_specs=_tblk(t),
        out_shape=jax.ShapeDtypeStruct((nb, t, LANES), F32),
        scratch_shapes=[pltpu.VMEM((CHUNK_COLS, CHUNK_COLS), BF16), pltpu.VMEM((CHUNK_COLS, sc), BF16),
                        pltpu.VMEM((CHUNK_COLS, sc), BF16)],
        compiler_params=_params(("arbitrary",)),
    )(u4, kc, sg_r, sg_i, hr, hi)


def _f4_mixer(ys4, prest, x, modv, cvv, cw32, w_a, w_b, w_pw, w_out, late_sh, late_full, tb):
    t = x.shape[0]
    hb = tb // CONV_HALO
    nt = t // tb
    nl = len(LATE)

    def body(ys_ref, pr_ref, halo_ref, x_ref, modv_ref, cvv_ref, cw_ref, wa_ref, wb_ref, wpw_ref, wout_ref, *rest):
        sh, full = rest[:nl], rest[2 * nl:3 * nl]
        r1_ref, ya_ref, yb_ref, ycv_ref, vc_ref, yg_ref, vs_ref, mg_ref, vbuf, vrot, send, recv, fsend, frecv = rest[3 * nl:]
        i = pl.program_id(0)

        @pl.when(i == 0)
        def _():
            _gather_start(LATE, sh, full, send, recv)

        ys = jnp.concatenate([ys_ref[b] for b in range(N_LANE_BLOCKS)], axis=-1)
        yg = _gelu(ys).astype(BF16)
        yg_ref[...] = yg
        ya = _dot(yg, wa_ref[...])
        yb = _dot(yg, wb_ref[...])
        ya_ref[...] = ya.astype(BF16)
        yb_ref[...] = yb.astype(BF16)
        yssm = ya * _sig(yb)
        hv = halo_ref[:, 0:CONV_WIDTH] * _sig(halo_ref[:, CONV_WIDTH:2 * CONV_WIDTH])
        vbuf[0:CONV_HALO, :] = jnp.where(i == 0, 0.0, hv)
        vbuf[CONV_HALO:, :] = pr_ref[:, 0:CONV_WIDTH] * _sig(pr_ref[:, CONV_WIDTH:2 * CONV_WIDTH])
        _fill_rotations(vbuf, vrot, tb + CONV_HALO - 8)
        acc = jnp.zeros((tb, CONV_WIDTH), F32)
        for k in range(CONV_KERNEL):
            acc += _rows_at(vbuf, vrot, CONV_HALO - CONV_KERNEL + 1 + k, tb) * cw_ref[k:k + 1, :]
        vc = acc + cvv_ref[0:1, :]
        vc_ref[...] = vc
        xh, _ = _ln_stats(vc)
        vl = xh * cvv_ref[1:2, :] + cvv_ref[2:3, :]
        vs = (vl * _sig(vl)).astype(BF16)
        vs_ref[...] = vs
        ycv = _dot(vs, wpw_ref[...])
        ycv_ref[...] = ycv.astype(BF16)
        gs = pr_ref[:, 2 * CONV_WIDTH:2 * CONV_WIDTH + D_MODEL]
        gc = pr_ref[:, 2 * CONV_WIDTH + D_MODEL:]
        merged = (_sig(gs) * yssm + _sig(gc) * ycv).astype(BF16)
        mg_ref[...] = merged
        ym = _dot(merged, wout_ref[...])
        r1_ref[...] = ALPHA * x_ref[...] + modv_ref[2:3, :] * ym

        @pl.when(i == nt - 1)
        def _():
            _gather_finish(LATE, sh, full, send, recv, fsend, frecv)

    tok = lambda w: pl.BlockSpec((tb, w), lambda i: (i, 0))
    sem = pltpu.SemaphoreType.DMA((nl, 3))
    n_in = 11
    return pl.pallas_call(
        body, name="f4_mixer", grid=(nt,),
        in_specs=[pl.BlockSpec((N_LANE_BLOCKS, tb, LANES), lambda i: (0, i, 0)), tok(prest.shape[1]),
                  pl.BlockSpec((CONV_HALO, 2 * CONV_WIDTH), lambda i: (jnp.maximum(i * hb - 1, 0), 0)),
                  tok(D_MODEL), _full(modv), _full(cvv), _full(cw32), _full(w_a), _full(w_b), _full(w_pw), _full(w_out)]
        + [ANY] * (2 * nl),
        out_specs=[ANY] * nl + [tok(D_MODEL), tok(D_MODEL), tok(D_MODEL), tok(D_MODEL), tok(CONV_WIDTH), tok(SSM_WIDTH),
                                tok(CONV_WIDTH), tok(D_MODEL)],
        input_output_aliases={n_in + nl + k: k for k in range(nl)},
        out_shape=[jax.ShapeDtypeStruct(f.shape, f.dtype) for f in late_full]
        + [jax.ShapeDtypeStruct((t, D_MODEL), F32), jax.ShapeDtypeStruct((t, D_MODEL), BF16),
                   jax.ShapeDtypeStruct((t, D_MODEL), BF16), jax.ShapeDtypeStruct((t, D_MODEL), BF16),
                   jax.ShapeDtypeStruct((t, CONV_WIDTH), F32), jax.ShapeDtypeStruct((t, SSM_WIDTH), BF16),
                   jax.ShapeDtypeStruct((t, CONV_WIDTH), BF16), jax.ShapeDtypeStruct((t, D_MODEL), BF16)],
        scratch_shapes=[pltpu.VMEM((tb + CONV_HALO, CONV_WIDTH), F32),
                        pltpu.VMEM((7, tb + CONV_HALO - 8, CONV_WIDTH), F32), sem, sem, sem, sem],
        compiler_params=_params(("arbitrary",)),
    )(ys4, prest, prest, x, modv, cvv, cw32, w_a, w_b, w_pw, w_out, *late_sh, *late_full)


FFN_COLS = 1408
FFN_CHUNK = 256


def _f5_ffn(r1, tgt, modv, lnv, fdw, w_up, w_down, tb):
    t = r1.shape[0]
    fw = 2 * FFN_HIDDEN

    def body(r1_ref, tgt_ref, modv_ref, lnv_ref, fdw_ref, wup_hbm, wdn_hbm,
             dr2_ref, d_ref, up_ref, z_ref, acc_ref, wup_v, wdn_v, upbuf, gbuf, hbuf, sems):
        i = pl.program_id(0)

        @pl.when(i == 0)
        def _():
            _load_once(wup_hbm, wup_v, sems.at[0])
            _load_once(wdn_hbm, wdn_v, sems.at[1])
            acc_ref[...] = jnp.zeros_like(acc_ref)
            upbuf[0:8, :] = jnp.zeros((8, fw), F32)

        xh1, _ = _ln_stats(r1_ref[...])
        x1 = xh1 * lnv_ref[0:1, :] + lnv_ref[1:2, :]
        xn2, _ = _ln_stats(x1)
        h2 = (xn2 * (1.0 + modv_ref[4:5, :]) + modv_ref[3:4, :]).astype(BF16)
        for j in range(fw // FFN_CHUNK):
            cs = slice(j * FFN_CHUNK, (j + 1) * FFN_CHUNK)
            up = _dot(h2, wup_v[:, cs])
            upbuf[8:, cs] = up
            up_ref[:, cs] = up.astype(BF16)

        def conv(cs):
            return (fdw_ref[0:1, cs] * upbuf[pl.ds(6, tb), cs] + fdw_ref[1:2, cs] * upbuf[pl.ds(7, tb), cs]
                    + fdw_ref[2:3, cs] * upbuf[pl.ds(8, tb), cs] + fdw_ref[3:4, cs])

        halves = [(slice(j * FFN_CHUNK, (j + 1) * FFN_CHUNK),
                   slice(FFN_HIDDEN + j * FFN_CHUNK, FFN_HIDDEN + (j + 1) * FFN_CHUNK)) for j in range(FFN_HIDDEN // FFN_CHUNK)]
        yf = jnp.zeros((tb, D_MODEL), F32)
        for ca, cv in halves:
            v = conv(cv)
            g, dg = _gelu_and_grad(conv(ca))
            gbuf[:, ca] = g.astype(BF16)
            hbuf[:, ca] = (v * dg).astype(BF16)
            z = (g * v).astype(BF16)
            z_ref[:, ca] = z
            yf += _dot(z, wdn_v[ca, :])
        r2 = ALPHA * x1 + modv_ref[5:6, :] * yf
        xh2, rstd2 = _ln_stats(r2)
        e = xh2 * lnv_ref[2:3, :] + lnv_ref[3:4, :] - tgt_ref[...]
        dx2 = e * (1.0 / D_MODEL)
        acc_ref[3:4, :] += _colsum(e * e) * (0.5 / D_MODEL)
        acc_ref[0:1, :] += _colsum(dx2 * xh2)
        acc_ref[1:2, :] += _colsum(dx2)
        dr2 = _ln_bwd(dx2 * lnv_ref[2:3, :], xh2, rstd2)
        dr2_ref[...] = dr2
        acc_ref[2:3, :] += _colsum(dr2 * yf)
        dyf = (modv_ref[5:6, :] * dr2).astype(BF16)
        for ca, cv in halves:
            dz = _dot_nt(dyf, wdn_v[ca, :])
            d_ref[:, ca] = (dz * hbuf[:, ca].astype(F32)).astype(BF16)
            d_ref[:, cv] = (dz * gbuf[:, ca].astype(F32)).astype(BF16)
        upbuf[0:8, :] = upbuf[pl.ds(tb, 8), :]

    tok = lambda w: pl.BlockSpec((tb, w), lambda i: (i, 0))
    return pl.pallas_call(
        body, name="f5_ffn", grid=(t // tb,),
        in_specs=[tok(D_MODEL), tok(D_MODEL), _full(modv), _full(lnv), _full(fdw), ANY, ANY],
        out_specs=[tok(D_MODEL), tok(fw), tok(fw), tok(FFN_HIDDEN), pl.BlockSpec((8, D_MODEL), lambda i: (0, 0))],
        out_shape=[jax.ShapeDtypeStruct((t, D_MODEL), F32), jax.ShapeDtypeStruct((t, fw), BF16),
                   jax.ShapeDtypeStruct((t, fw), BF16), jax.ShapeDtypeStruct((t, FFN_HIDDEN), BF16),
                   jax.ShapeDtypeStruct((8, D_MODEL), F32)],
        scratch_shapes=[pltpu.VMEM(w_up.shape, BF16), pltpu.VMEM(w_down.shape, BF16),
                        pltpu.VMEM((tb + 8, fw), F32), pltpu.VMEM((tb, FFN_HIDDEN), BF16),
                        pltpu.VMEM((tb, FFN_HIDDEN), BF16), pltpu.SemaphoreType.DMA((2,))],
        compiler_params=_params(("arbitrary",)),
    )(r1, tgt, modv, lnv, fdw, w_up, w_down)


def _b1b_ffn_up(d, up, dr2, r1, modv, lnv, fdw, w_up, tb):
    t = dr2.shape[0]
    fw = 2 * FFN_HIDDEN
    nt = t // tb
    hb = tb // 16

    def body(d_ref, nxt_ref, up_ref, dr2_ref, r1_ref, modv_ref, lnv_ref, fdw_ref, wup_hbm, dup_ref, dr1_ref, h2_ref,
             dyf_ref, acc_ref, accw_ref, wup_v, dbuf, shifted, sem):
        i = pl.program_id(0)

        @pl.when(i == 0)
        def _():
            _load_once(wup_hbm, wup_v, sem)
            acc_ref[...] = jnp.zeros_like(acc_ref)
            accw_ref[...] = jnp.zeros_like(accw_ref)

        dbuf[0:tb, :] = d_ref[...].astype(F32)
        dbuf[tb:, :] = jnp.where(i == nt - 1, 0.0, nxt_ref[...].astype(F32))
        dh2 = jnp.zeros((tb, D_MODEL), F32)
        for j in range(fw // FFN_CHUNK):
            cs = slice(j * FFN_CHUNK, (j + 1) * FFN_CHUNK)
            for k in range(1, FFN_KERNEL):
                shifted[k - 1] = dbuf[pl.ds(k, tb), cs]
            ds = [dbuf[pl.ds(0, tb), cs], shifted[0], shifted[1]]
            dup = (fdw_ref[2:3, cs] * ds[0] + fdw_ref[1:2, cs] * ds[1] + fdw_ref[0:1, cs] * ds[2]).astype(BF16)
            dup_ref[:, cs] = dup
            dh2 += _dot_nt(dup, wup_v[:, cs])
            upf = up_ref[:, cs].astype(F32)
            for k in range(FFN_KERNEL):
                accw_ref[k:k + 1, cs] += _colsum(ds[FFN_KERNEL - 1 - k] * upf)
            accw_ref[3:4, cs] += _colsum(ds[0])
        xh1, rstd1 = _ln_stats(r1_ref[...])
        x1 = xh1 * lnv_ref[0:1, :] + lnv_ref[1:2, :]
        xn2, rstd2 = _ln_stats(x1)
        h2_ref[...] = (xn2 * (1.0 + modv_ref[4:5, :]) + modv_ref[3:4, :]).astype(BF16)
        dr2 = dr2_ref[...]
        dyf_ref[...] = (modv_ref[5:6, :] * dr2).astype(BF16)
        acc_ref[0:1, :] += _colsum(dh2 * xn2)
        acc_ref[1:2, :] += _colsum(dh2)
        dx1 = _ln_bwd(dh2 * (1.0 + modv_ref[4:5, :]), xn2, rstd2) + ALPHA * dr2
        acc_ref[2:3, :] += _colsum(dx1 * xh1)
        acc_ref[3:4, :] += _colsum(dx1)
        dr1_ref[...] = _ln_bwd(dx1 * lnv_ref[0:1, :], xh1, rstd1)

    tok = lambda w: pl.BlockSpec((tb, w), lambda i: (i, 0))
    return pl.pallas_call(
        body, name="b1b_ffn_up", grid=(nt,),
        in_specs=[tok(fw), pl.BlockSpec((16, fw), lambda i: (jnp.minimum((i + 1) * hb, t // 16 - 1), 0)), tok(fw),
                  tok(D_MODEL), tok(D_MODEL), _full(modv), _full(lnv), _full(fdw), ANY],
        out_specs=[tok(fw), tok(D_MODEL), tok(D_MODEL), tok(D_MODEL), pl.BlockSpec((8, D_MODEL), lambda i: (0, 0)),
                   pl.BlockSpec((8, fw), lambda i: (0, 0))],
        out_shape=[jax.ShapeDtypeStruct((t, fw), BF16), jax.ShapeDtypeStruct((t, D_MODEL), F32),
                   jax.ShapeDtypeStruct((t, D_MODEL), BF16), jax.ShapeDtypeStruct((t, D_MODEL), BF16),
                   jax.ShapeDtypeStruct((8, D_MODEL), F32), jax.ShapeDtypeStruct((8, fw), F32)],
        scratch_shapes=[pltpu.VMEM(w_up.shape, BF16), pltpu.VMEM((tb + 16, fw), F32),
                        pltpu.VMEM((FFN_KERNEL - 1, tb, FFN_CHUNK), F32), pltpu.SemaphoreType.DMA],
        compiler_params=_params(("arbitrary",)),
    )(d, d, up, dr2, r1, modv, lnv, fdw, w_up)


def _b2_mixer(dr1, ys4, prest, ya, yb, ycv, vc, merged, modv, cvv, cw32, w_a, w_b, w_pw, w_out, late_dw, tb):
    t = dr1.shape[0]
    nt = t // tb
    nl = len(LATE)
    cwd = CONV_WIDTH

    def body(dr1_ref, ys_ref, pr_ref, ya_ref, yb_ref, ycv_ref, vc_ref, mg_ref, modv_ref, cvv_ref, cw_ref,
             wa_ref, wb_ref, wpw_ref, wout_ref, *rest):
        dw, got = rest[:nl], rest[nl:2 * nl]
        (dys_ref, dpr_ref, dya_ref, dyb_ref, dycv_ref, dym_ref, acc_a, acc_b, acc_w, dvbuf, dvrot,
         send, recv) = rest[2 * nl:]
        i = pl.program_id(0)
        ti = nt - 1 - i

        @pl.when(i == 0)
        def _():
            for cp in _scatter(LATE, dw, got, send, recv):
                cp.start()
            acc_a[...] = jnp.zeros_like(acc_a)
            acc_b[...] = jnp.zeros_like(acc_b)
            acc_w[...] = jnp.zeros_like(acc_w)
            dvbuf[pl.ds(tb, CONV_HALO), :] = jnp.zeros((CONV_HALO, cwd), F32)

        dr1 = dr1_ref[...]
        dym = (modv_ref[2:3, :] * dr1).astype(BF16)
        dym_ref[...] = dym
        ym = _dot(mg_ref[...], wout_ref[...])
        acc_a[0:1, :] += _colsum(dr1 * ym)
        dmg = _dot_nt(dym, wout_ref[...])
        sgs = _sig(pr_ref[:, 2 * cwd:2 * cwd + D_MODEL])
        sgc = _sig(pr_ref[:, 2 * cwd + D_MODEL:])
        ya_v = ya_ref[...].astype(F32)
        syb = _sig(yb_ref[...].astype(F32))
        ycv_v = ycv_ref[...].astype(F32)
        dpr_ref[:, 2 * cwd:2 * cwd + D_MODEL] = (dmg * (ya_v * syb) * sgs * (1.0 - sgs)).astype(BF16)
        dpr_ref[:, 2 * cwd + D_MODEL:] = (dmg * ycv_v * sgc * (1.0 - sgc)).astype(BF16)
        dyssm = dmg * sgs
        dya = (dyssm * syb).astype(BF16)
        dyb = (dyssm * ya_v * syb * (1.0 - syb)).astype(BF16)
        dya_ref[...] = dya
        dyb_ref[...] = dyb
        dyg = _dot_nt(dya, wa_ref[...]) + _dot_nt(dyb, wb_ref[...])
        ys = jnp.concatenate([ys_ref[b] for b in range(N_LANE_BLOCKS)], axis=-1)
        dys = dyg * _dgelu(ys)
        for b in range(N_LANE_BLOCKS):
            dys_ref[b] = dys[:, b * LANES:(b + 1) * LANES]
        dycv = (dmg * sgc).astype(BF16)
        dycv_ref[...] = dycv
        dvs = _dot_nt(dycv, wpw_ref[...])
        xh, rstd = _ln_stats(vc_ref[...])
        vl = xh * cvv_ref[1:2, :] + cvv_ref[2:3, :]
        s = _sig(vl)
        dvl = dvs * s * (1.0 + vl * (1.0 - s))
        acc_b[1:2, :] += _colsum(dvl * xh)
        acc_b[2:3, :] += _colsum(dvl)
        dvc = _ln_bwd(dvl * cvv_ref[1:2, :], xh, rstd)
        acc_b[0:1, :] += _colsum(dvc)
        cva = pr_ref[:, 0:cwd]
        scg = _sig(pr_ref[:, cwd:2 * cwd])
        v = cva * scg
        dvbuf[0:tb, :] = dvc
        _fill_rotations(dvbuf, dvrot, tb + CONV_HALO - 8)
        dv = jnp.zeros((tb, cwd), F32)
        for k in range(CONV_KERNEL):
            later = _rows_at(dvbuf, dvrot, CONV_KERNEL - 1 - k, tb)
            dv += later * cw_ref[k:k + 1, :]
            acc_w[k:k + 1, :] += _colsum(later * v)
        dvbuf[pl.ds(tb, CONV_HALO), :] = dvbuf[0:CONV_HALO, :]
        dpr_ref[:, 0:cwd] = (dv * scg).astype(BF16)
        dpr_ref[:, cwd:2 * cwd] = (dv * cva * scg * (1.0 - scg)).astype(BF16)

        @pl.when(i == nt - 1)
        def _():
            for cp in _scatter(LATE, dw, got, send, recv):
                cp.wait()

    rtok = lambda w: pl.BlockSpec((tb, w), lambda i: (nt - 1 - i, 0))
    r4 = pl.BlockSpec((N_LANE_BLOCKS, tb, LANES), lambda i: (0, nt - 1 - i, 0))
    pw = prest.shape[1]
    return pl.pallas_call(
        body, name="b2_mixer", grid=(nt,),
        in_specs=[rtok(D_MODEL), r4, rtok(pw),
                  rtok(D_MODEL), rtok(D_MODEL), rtok(D_MODEL), rtok(cwd), rtok(D_MODEL),
                  _full(modv), _full(cvv), _full(cw32), _full(w_a), _full(w_b), _full(w_pw), _full(w_out)] + [ANY] * nl,
        out_specs=[ANY] * nl + [r4, rtok(pw), rtok(D_MODEL), rtok(D_MODEL), rtok(D_MODEL), rtok(D_MODEL),
                   pl.BlockSpec((8, D_MODEL), lambda i: (0, 0)), pl.BlockSpec((8, cwd), lambda i: (0, 0)),
                   pl.BlockSpec((CONV_HALO, cwd), lambda i: (0, 0))],
        out_shape=[jax.ShapeDtypeStruct((N_DEV - 1,) + _piece(*BIG[m][1:]), BF16) for m in LATE]
        + [jax.ShapeDtypeStruct((N_LANE_BLOCKS, t, LANES), F32), jax.ShapeDtypeStruct((t, pw), BF16),
                   jax.ShapeDtypeStruct((t, D_MODEL), BF16), jax.ShapeDtypeStruct((t, D_MODEL), BF16),
                   jax.ShapeDtypeStruct((t, D_MODEL), BF16), jax.ShapeDtypeStruct((t, D_MODEL), BF16),
                   jax.ShapeDtypeStruct((8, D_MODEL), F32), jax.ShapeDtypeStruct((8, cwd), F32),
                   jax.ShapeDtypeStruct((CONV_HALO, cwd), F32)],
        scratch_shapes=[pltpu.VMEM((tb + CONV_HALO, cwd), F32), pltpu.VMEM((7, tb + CONV_HALO - 8, cwd), F32),
                        pltpu.SemaphoreType.DMA((nl, N_DEV - 1)), pltpu.SemaphoreType.DMA((nl, N_DEV - 1))],
        compiler_params=_params(("arbitrary",)),
    )(dr1, ys4, prest, ya, yb, ycv, vc, merged, modv, cvv, cw32, w_a, w_b, w_pw, w_out, *late_dw)


def _s5c_state_bwd(dy4, sg_r, sg_i, a8, hr, hi):
    nb, t, _ = dy4.shape
    nc = t // SSM_CHUNK
    sc = STATE_COLS

    def body(dy_ref, sgr_ref, sgi_ref, a_ref, hr_ref, hi_ref, dxr_ref, dxi_ref, da_ref, dsgr_ref, dsgi_ref,
             g_s, lr_s, li_s, xr_s, xi_s):
        dy = _load_chunks(dy_ref, nc)
        _build_maps(sgr_ref, g_s)
        lr_s[...] = _dot(dy, g_s[...])
        _build_maps(sgi_ref, g_s)
        li_s[...] = _dot(dy, g_s[...])
        ar = a_ref[0, 0:1, :]
        ai = a_ref[0, 1:2, :]

        def step(k, carry):
            pr, pi, dar, dai = carry
            c = nc - 1 - k
            xr_s[pl.ds(c, 1), :] = pr
            xi_s[pl.ds(c, 1), :] = pi
            h_r = hr_ref[0, pl.ds(c, 1), :]
            h_i = hi_ref[0, pl.ds(c, 1), :]
            dar = dar + pr * h_r + pi * h_i
            dai = dai - pr * h_i + pi * h_r
            nr = lr_s[pl.ds(c, 1), :] + ar * pr + ai * pi
            ni = li_s[pl.ds(c, 1), :] - ai * pr + ar * pi
            return nr, ni, dar, dai

        z = jnp.zeros((1, sc), F32)
        _, _, dar, dai = lax.fori_loop(0, nc, step, (z, z, z, z))
        da_ref[0] = jnp.concatenate([dar, dai, jnp.zeros((6, sc), F32)], axis=0)
        dxr_ref[0] = xr_s[...].astype(BF16)
        dxi_ref[0] = xi_s[...].astype(BF16)
        for h_ref, o_ref in ((hr_ref, dsgr_ref), (hi_ref, dsgi_ref)):
            hb = h_ref[0].astype(BF16)
            for j in range(SSM_CHUNK):
                o_ref[j] = _fold(_dot_tn(dy[:, j * LANES:(j + 1) * LANES], hb), SSM_STATE)

    blk = lambda r, c: pl.BlockSpec((1, r, c), lambda b: (b, 0, 0))
    return pl.pallas_call(
        body, name="s5c_state_bwd", grid=(nb,),
        in_specs=[_tblk(t), _cblk(SSM_STATE), _cblk(SSM_STATE), blk(8, sc), blk(nc, sc), blk(nc, sc)],
        out_specs=[blk(nc, sc), blk(nc, sc), blk(8, sc), _cblk(SSM_STATE), _cblk(SSM_STATE)],
        out_shape=[jax.ShapeDtypeStruct((nb, nc, sc), BF16), jax.ShapeDtypeStruct((nb, nc, sc), BF16),
                   jax.ShapeDtypeStruct((nb, 8, sc), F32),
                   jax.ShapeDtypeStruct((SSM_CHUNK, SSM_WIDTH, SSM_STATE), F32),
                   jax.ShapeDtypeStruct((SSM_CHUNK, SSM_WIDTH, SSM_STATE), F32)],
        scratch_shapes=[pltpu.VMEM((CHUNK_COLS, sc), BF16)] + [pltpu.VMEM((nc, sc), F32)] * 4,
        compiler_params=_params(("arbitrary",)),
    )(dy4, sg_r, sg_i, a8, hr, hi)


def _s5d_input_bwd(dy4, u4, kc, sw_r, sw_i, dxr, dxi, mid_dw):
    nb, t, _ = dy4.shape
    nc = t // SSM_CHUNK
    sc = STATE_COLS
    nl = len(MID)

    def body(dy_ref, u_ref, kc_ref, swr_ref, swi_ref, dxr_ref, dxi_ref, *rest):
        dw, got = rest[:nl], rest[nl:2 * nl]
        du_ref, dkc_ref, dswr_ref, dswi_ref, tm_s, w_s, dk_s, send, recv = rest[2 * nl:]

        @pl.when(pl.program_id(0) == 0)
        def _():
            for cp in _scatter(MID, dw, got, send, recv):
                cp.start()

        dy = _load_chunks(dy_ref, nc)
        u = _load_chunks(u_ref, nc)
        _build_toeplitz(kc_ref, tm_s)
        du = _dot_nt(dy, tm_s[...])
        _build_maps(swr_ref, w_s)
        du += _dot_nt(dxr_ref[0], w_s[...])
        _build_maps(swi_ref, w_s)
        du += _dot_nt(dxi_ref[0], w_s[...])
        _store_chunks(du_ref, du, nc)
        dk_s[...] = jnp.zeros_like(dk_s)
        for ji in range(SSM_CHUNK):
            uj = u[:, ji * LANES:(ji + 1) * LANES]
            rows = _dot_tn(uj, dy)
            for jo in range(ji, SSM_CHUNK):
                dk_s[jo - ji] += rows[:, jo * LANES:(jo + 1) * LANES]
            dswr_ref[ji] = _fold(_dot_tn(uj, dxr_ref[0]), SSM_STATE)
            dswi_ref[ji] = _fold(_dot_tn(uj, dxi_ref[0]), SSM_STATE)
        for d in range(SSM_CHUNK):
            dkc_ref[d] = _fold(dk_s[d], SSM_GROUP)

        @pl.when(pl.program_id(0) == nb - 1)
        def _():
            for cp in _scatter(MID, dw, got, send, recv):
                cp.wait()

    blk = lambda r, c: pl.BlockSpec((1, r, c), lambda b: (b, 0, 0))
    ssem = pltpu.SemaphoreType.DMA((nl, N_DEV - 1))
    return pl.pallas_call(
        body, name="s5d_input_bwd", grid=(nb,),
        in_specs=[_tblk(t), _tblk(t), _cblk(SSM_GROUP), _cblk(SSM_STATE), _cblk(SSM_STATE),
                  blk(nc, sc), blk(nc, sc)] + [ANY] * nl,
        out_specs=[ANY] * nl + [_tblk(t), _cblk(SSM_GROUP), _cblk(SSM_STATE), _cblk(SSM_STATE)],
        out_shape=[jax.ShapeDtypeStruct((N_DEV - 1,) + _piece(*BIG[m][1:]), BF16) for m in MID]
        + [jax.ShapeDtypeStruct((nb, t, LANES), F32),
           jax.ShapeDtypeStruct((SSM_CHUNK, SSM_WIDTH, SSM_GROUP), F32),
           jax.ShapeDtypeStruct((SSM_CHUNK, SSM_WIDTH, SSM_STATE), F32),
           jax.ShapeDtypeStruct((SSM_CHUNK, SSM_WIDTH, SSM_STATE), F32)],
        scratch_shapes=[pltpu.VMEM((CHUNK_COLS, CHUNK_COLS), BF16), pltpu.VMEM((CHUNK_COLS, sc), BF16),
                        pltpu.VMEM((SSM_CHUNK, LANES, LANES), F32), ssem, ssem],
        compiler_params=_params(("arbitrary",)),
    )(dy4, u4, kc, sw_r, sw_i, dxr, dxi, *mid_dw)


def _b3_inproj(x, dr1, du4, dprest, modv, w_in, tb):
    t = x.shape[0]
    pw = IN_PROJ_WIDTH - SSM_WIDTH

    def body(x_ref, dr1_ref, du_ref, dpr_ref, modv_ref, w_hbm, gx_ref, dp_ref, acc_ref, accb_ref, w_v, sem):
        @pl.when(pl.program_id(0) == 0)
        def _():
            _load_once(w_hbm, w_v, sem)
            acc_ref[...] = jnp.zeros_like(acc_ref)
            accb_ref[...] = jnp.zeros_like(accb_ref)

        du = jnp.concatenate([du_ref[b] for b in range(N_LANE_BLOCKS)], axis=-1).astype(BF16)
        dpr = dpr_ref[...]
        dp_ref[:, 0:SSM_WIDTH] = du
        dp_ref[:, SSM_WIDTH:] = dpr
        accb_ref[0:1, 0:SSM_WIDTH] += _colsum(du.astype(F32))
        accb_ref[0:1, SSM_WIDTH:] += _colsum(dpr.astype(F32))
        dh = _dot_nt(du, w_v[:, 0:SSM_WIDTH]) + _dot_nt(dpr, w_v[:, SSM_WIDTH:])
        xn, rstd = _ln_stats(x_ref[...])
        acc_ref[0:1, :] += _colsum(dh * xn)
        acc_ref[1:2, :] += _colsum(dh)
        gx_ref[...] = _ln_bwd(dh * (1.0 + modv_ref[1:2, :]), xn, rstd) + ALPHA * dr1_ref[...]

    tok = lambda w: pl.BlockSpec((tb, w), lambda i: (i, 0))
    return pl.pallas_call(
        body, name="b3_inproj", grid=(t // tb,),
        in_specs=[tok(D_MODEL), tok(D_MODEL), pl.BlockSpec((N_LANE_BLOCKS, tb, LANES), lambda i: (0, i, 0)), tok(pw),
                  _full(modv), ANY],
        out_specs=[tok(D_MODEL), tok(IN_PROJ_WIDTH), pl.BlockSpec((8, D_MODEL), lambda i: (0, 0)),
                   pl.BlockSpec((8, IN_PROJ_WIDTH), lambda i: (0, 0))],
        out_shape=[jax.ShapeDtypeStruct((t, D_MODEL), F32), jax.ShapeDtypeStruct((t, IN_PROJ_WIDTH), BF16),
                   jax.ShapeDtypeStruct((8, D_MODEL), F32), jax.ShapeDtypeStruct((8, IN_PROJ_WIDTH), F32)],
        scratch_shapes=[pltpu.VMEM(w_in.shape, BF16), pltpu.SemaphoreType.DMA],
        compiler_params=_params(("arbitrary",)),
    )(x, dr1, du4, dprest, modv, w_in)


TN_ROWS = 2048


def _tn_matmul(name, a, b, tm, tn):
    t, m = a.shape
    n = b.shape[1]
    tt = min(TN_ROWS, t)
    nk = t // tt

    def body(a_ref, b_ref, o_ref, acc):
        k = pl.program_id(2)

        @pl.when(k == 0)
        def _():
            acc[...] = jnp.zeros_like(acc)

        acc[...] += _dot_tn(a_ref[...], b_ref[...])

        @pl.when(k == nk - 1)
        def _():
            o_ref[...] = acc[...].astype(BF16)

    return pl.pallas_call(
        body, name=name, grid=(m // tm, n // tn, nk),
        in_specs=[pl.BlockSpec((tt, tm), lambda i, j, k: (k, i)), pl.BlockSpec((tt, tn), lambda i, j, k: (k, j))],
        out_specs=pl.BlockSpec((tm, tn), lambda i, j, k: (i, j)),
        out_shape=jax.ShapeDtypeStruct((m, n), BF16),
        scratch_shapes=[pltpu.VMEM((tm, tn), F32)],
        compiler_params=_params(("arbitrary", "arbitrary", "arbitrary")),
    )(a, b)


def _local_step(x, tgt, modv, wb, shards, sp, tb=256):
    t = x.shape[0]
    row8 = lambda rows, w: jnp.concatenate([r.reshape(1, w) for r in rows] + [jnp.zeros((8 - len(rows), w), F32)], axis=0)
    lnv = row8([sp["ln1_g"], sp["ln1_b"], sp["ln2_g"], sp["ln2_b"]], D_MODEL)
    cvv = row8([sp["cv_dw_b"], sp["cv_ln_g"], sp["cv_ln_b"]], CONV_WIDTH)
    cw32 = jnp.concatenate([sp["cv_dw_w"].reshape(CONV_KERNEL, CONV_WIDTH), jnp.zeros((1, CONV_WIDTH), F32)], axis=0)
    fdw = row8(list(sp["ffn_dw_w"].reshape(FFN_KERNEL, 2 * FFN_HIDDEN)) + [sp["ffn_dw_b"]], 2 * FFN_HIDDEN)
    b_in = sp["b_in"].reshape(1, IN_PROJ_WIDTH)
    ssm = tuple(sp[k] for k in ("ssm_lambda_re", "ssm_lambda_im", "ssm_log_dt", "ssm_b_re", "ssm_b_im", "ssm_c_re",
                                "ssm_c_im", "ssm_d"))
    (kc, sw_r, sw_i, sg_r, sg_i, a), ssm_vjp = jax.vjp(_s5_build, *ssm)
    a8 = jnp.concatenate([a, jnp.zeros((N_LANE_BLOCKS, 6, STATE_COLS), F32)], axis=1)

    name = lambda m: BIG[m][0]
    *mid_w, u4, prest, h1 = _f1_inproj(x, modv, b_in, wb["w_in"], [shards[m] for m in MID], [wb[name(m)] for m in MID], tb)
    w_a, w_b, w_pw, w_out = mid_w
    hr, hi = _s5a_state(u4, sw_r, sw_i, a8)
    ys4 = _s5b_out(u4, kc, sg_r, sg_i, hr, hi)
    w_up, w_down, r1, ya, yb, ycv, vc, yg, vs, merged = _f4_mixer(
        ys4, prest, x, modv, cvv, cw32, w_a, w_b, w_pw, w_out, [shards[m] for m in LATE], [wb[name(m)] for m in LATE], tb)
    dr2, dconv, up, z, acc5 = _f5_ffn(r1, tgt, modv, lnv, fdw, w_up, w_down, tb)
    dup, dr1, h2, dyf, acc1b, acc1a = _b1b_ffn_up(dconv, up, dr2, r1, modv, lnv, fdw, w_up, tb)
    late_dw = [_tn_matmul("dw_up", h2, dup, 1024, FFN_COLS), _tn_matmul("dw_down", z, dyf, FFN_COLS, 1024)]
    got_up, got_down, dys4, dprest, dya, dyb, dycv, dym, acc2a, acc2b, acc2w = _b2_mixer(
        dr1, ys4, prest, ya, yb, ycv, vc, merged, modv, cvv, cw32, w_a, w_b, w_pw, w_out, late_dw, tb)
    mid_dw = [_tn_matmul("dw_glu_a", yg, dya, 512, 1024), _tn_matmul("dw_glu_b", yg, dyb, 512, 1024),
              _tn_matmul("dw_pw", vs, dycv, 512, 1024), _tn_matmul("dw_out", merged, dym, 1024, 1024)]
    dxr, dxi, da8, dsg_r, dsg_i = _s5c_state_bwd(dys4, sg_r, sg_i, a8, hr, hi)
    *mid_got, du4, dkc, dsw_r, dsw_i = _s5d_input_bwd(dys4, u4, kc, sw_r, sw_i, dxr, dxi, mid_dw)
    dssm = ssm_vjp((dkc, dsw_r, dsw_i, dsg_r, dsg_i, da8[:, 0:2, :]))
    gx, dp, acc3, acc3b = _b3_inproj(x, dr1, du4, dprest, modv, wb["w_in"], tb)
    dbig = [_tn_matmul("dw_in", h1, dp, 1024, 896)] + mid_dw + late_dw
    dmod = jnp.concatenate([acc3[1], acc3[0], acc2a[0], acc1b[1], acc1b[0], acc5[2]])
    small = {
        "dmod": dmod, "b_in": acc3b[0],
        "ssm_lambda_re": dssm[0], "ssm_lambda_im": dssm[1], "ssm_log_dt": dssm[2], "ssm_b_re": dssm[3],
        "ssm_b_im": dssm[4], "ssm_c_re": dssm[5], "ssm_c_im": dssm[6], "ssm_d": dssm[7],
        "cv_dw_w": acc2w[0:CONV_KERNEL], "cv_dw_b": acc2b[0], "cv_ln_g": acc2b[1], "cv_ln_b": acc2b[2],
        "ln1_g": acc1b[2], "ln1_b": acc1b[3], "ffn_dw_w": acc1a[0:FFN_KERNEL], "ffn_dw_b": acc1a[3],
        "ln2_g": acc5[0], "ln2_b": acc5[1], "loss": jnp.sum(acc5[3]).reshape(1),
    }
    return gx, dbig, list(mid_got) + [got_up, got_down], small


def _allgather_rows(x_ref, out_ref, send_sems, recv_sems, local_sem):
    m_per = x_ref.shape[0]
    x, y, c, chips = _place()
    me, sibling = (x, y, c), (x, y, 1 - c)

    def rows(px, py, pc):
        return out_ref.at[pl.ds((4 * px + 2 * py + pc) * m_per, m_per), :]

    def copy(k, block, to, src=None):
        return pltpu.make_async_remote_copy(
            src_ref=rows(*block) if src is None else src, dst_ref=rows(*block),
            send_sem=send_sems.at[k], recv_sem=recv_sems.at[k], device_id=to, device_id_type=MESH)

    mine = pltpu.make_async_copy(x_ref, rows(*me), local_sem)
    mine.start()
    first = [copy(0, me, sibling, src=x_ref)]
    first += [copy(1 + j, me, (*chip, c), src=x_ref) for j, chip in enumerate(chips)]
    for cp in first:
        cp.start()

    def finish():
        passed = [copy(4 + j, (*chip, c), sibling) for j, chip in enumerate(chips)]
        for j, chip in enumerate(chips):
            copy(1 + j, (*chip, c), me).wait_recv()
            passed[j].start()
        copy(0, sibling, me).wait_recv()
        for j, chip in enumerate(chips):
            copy(4 + j, (*chip, 1 - c), me).wait_recv()
        for cp in first + passed:
            cp.wait_send()
        mine.wait()

    return finish


def _allgather(name, shard):
    m_per, n = shard.shape

    def body(x_ref, out_ref, send_sems, recv_sems, local_sem):
        _allgather_rows(x_ref, out_ref, send_sems, recv_sems, local_sem)()

    return pl.pallas_call(
        body, name=name,
        out_shape=jax.ShapeDtypeStruct((N_DEV * m_per, n), shard.dtype),
        in_specs=[pl.BlockSpec(memory_space=pltpu.VMEM)],
        out_specs=pl.BlockSpec(memory_space=pltpu.VMEM),
        scratch_shapes=[pltpu.SemaphoreType.DMA((7,)), pltpu.SemaphoreType.DMA((7,)), pltpu.SemaphoreType.DMA],
        compiler_params=_params(),
    )(shard)


def _add_rows(pr):
    return 64 if pr % 64 == 0 else 16


def _gather_weights(shards, first):
    nm = len(BIG)
    nl = len(DIRECT)

    def body(*refs):
        ins, first_ref = refs[:nm], refs[nm]
        outs, lsh, first_all = refs[nm + 1:2 * nm + 1], refs[2 * nm + 1:2 * nm + 1 + nl], refs[2 * nm + 1 + nl]
        stage = refs[2 * nm + 2 + nl:3 * nm + 2 + nl]
        send, recv, fsend, frecv, lsem, ag_send, ag_recv, ag_local = refs[3 * nm + 2 + nl:]
        x, y, c, chips = _place()
        s_me = 2 * x + y
        sibling = (x, y, 1 - c)
        pend = []
        finish_first = _allgather_rows(first_ref, first_all, ag_send, ag_recv, ag_local)
        for m, (_, kind, shape) in enumerate(BIG):
            stage[m][...] = ins[m][...].astype(BF16)
            pr, pc = _piece(kind, shape)
            for k in range(2):
                cp = pltpu.make_async_copy(stage[m].at[pl.ds(k * pr, pr), :], _piece_at(outs[m], kind, shape, s_me, k),
                                           lsem.at[m, k])
                cp.start()
                pend.append(cp.wait)
            if m in DIRECT:
                cp = pltpu.make_async_copy(stage[m], lsh[DIRECT.index(m)], lsem.at[m, 2])
                cp.start()
                pend.append(cp.wait)
                continue
            for j, chip in enumerate(chips):
                cp = pltpu.make_async_remote_copy(
                    src_ref=stage[m].at[pl.ds(pl.multiple_of(c * pr, 16), pr), :],
                    dst_ref=_piece_at(outs[m], kind, shape, s_me, c),
                    send_sem=send.at[m, j], recv_sem=recv.at[m, j], device_id=(*chip, c), device_id_type=MESH)
                cp.start()
                pend.append(cp.wait_send)
        finish_first()
        for m in EARLY:
            _, kind, shape = BIG[m]
            for j, (cx, cy) in enumerate(chips):
                got = _piece_at(outs[m], kind, shape, 2 * cx + cy, c)
                pltpu.make_async_remote_copy(src_ref=got, dst_ref=got, send_sem=send.at[m, j], recv_sem=recv.at[m, j],
                                             device_id=(cx, cy, c), device_id_type=MESH).wait_recv()
                cp = pltpu.make_async_remote_copy(src_ref=got, dst_ref=got, send_sem=fsend.at[m, j],
                                                  recv_sem=frecv.at[m, j], device_id=sibling, device_id_type=MESH)
                cp.start()
                pend.append(cp.wait_send)
        for m in EARLY:
            _, kind, shape = BIG[m]
            for j, (cx, cy) in enumerate(chips):
                got = _piece_at(outs[m], kind, shape, 2 * cx + cy, 1 - c)
                pltpu.make_async_remote_copy(src_ref=got, dst_ref=got, send_sem=fsend.at[m, j], recv_sem=frecv.at[m, j],
                                             device_id=sibling, device_id_type=MESH).wait_recv()
        for w in pend:
            w()

    sem = lambda *s: pltpu.SemaphoreType.DMA(s)
    res = pl.pallas_call(
        body, name="gather_weights",
        out_shape=[jax.ShapeDtypeStruct(shape, BF16) for _, _, shape in BIG]
        + [jax.ShapeDtypeStruct(shards[m].shape, BF16) for m in DIRECT]
        + [jax.ShapeDtypeStruct((N_DEV * first.shape[0], first.shape[1]), F32)],
        in_specs=[pl.BlockSpec(memory_space=pltpu.VMEM)] * (nm + 1),
        out_specs=[ANY] * (nm + nl) + [pl.BlockSpec(memory_space=pltpu.VMEM)],
        scratch_shapes=[pltpu.VMEM(s.shape, BF16) for s in shards] + [sem(nm, 3), sem(nm, 3), sem(nm, 3), sem(nm, 3),
                                                                         sem(nm, 3), sem(7), sem(7), sem()],
        compiler_params=_params(),
    )(*shards, first)
    return res[:nm], dict(zip(DIRECT, res[nm:nm + nl])), res[nm + nl]


def _rs1_sibling(grads):
    mats = [BIG[m] for m in EARLY]
    nm = len(mats)

    def body(*refs):
        ins, outs = refs[:nm], refs[nm:2 * nm]
        send, recv = refs[2 * nm:]
        x, y, c, _ = _place()
        cps = []
        for m, (_, kind, shape) in enumerate(mats):
            for s in range(N_CHIP):
                cp = pltpu.make_async_remote_copy(
                    src_ref=_piece_at(ins[m], kind, shape, s, 1 - c), dst_ref=outs[m].at[s],
                    send_sem=send.at[m, s], recv_sem=recv.at[m, s], device_id=(x, y, 1 - c), device_id_type=MESH)
                cp.start()
                cps.append(cp)
        for cp in cps:
            cp.wait()

    sem = lambda *s: pltpu.SemaphoreType.DMA(s)
    return pl.pallas_call(
        body, name="rs1_sibling",
        out_shape=[jax.ShapeDtypeStruct((N_CHIP,) + _piece(kind, shape), BF16) for _, kind, shape in mats],
        in_specs=[ANY] * nm, out_specs=[ANY] * nm,
        scratch_shapes=[sem(nm, N_CHIP), sem(nm, N_CHIP)],
        compiler_params=_params(),
    )(*grads)


def _rs2_chips(grads, halves):
    mats = [BIG[m] for m in EARLY]
    nm = len(mats)

    def body(*refs):
        gin, hin = refs[:nm], refs[nm:2 * nm]
        own, got = refs[2 * nm:3 * nm], refs[3 * nm:4 * nm]
        send, recv, lsem = refs[4 * nm:]
        x, y, c, chips = _place()
        s_me = 2 * x + y
        for m, (_, kind, shape) in enumerate(mats):
            pr, pc = _piece(kind, shape)

            def scoped(a, b, m=m, kind=kind, shape=shape, pr=pr):
                loads = [pltpu.make_async_copy(_piece_at(gin[m], kind, shape, s, c), a.at[s], lsem.at[s])
                         for s in range(N_CHIP)]
                loads.append(pltpu.make_async_copy(hin[m], b, lsem.at[N_CHIP]))
                for cp in loads:
                    cp.start()
                for cp in loads:
                    cp.wait()
                step = _add_rows(pr)
                for s in range(N_CHIP):
                    def add(i, _, s=s):
                        r = pl.ds(pl.multiple_of(i * step, 16), step)
                        a[s, r, :] = (a[s, r, :].astype(F32) + b[s, r, :].astype(F32)).astype(BF16)
                        return 0

                    lax.fori_loop(0, pr // step, add, 0)
                waits = []
                for j, (cx, cy) in enumerate(chips):
                    cp = pltpu.make_async_remote_copy(src_ref=a.at[2 * cx + cy], dst_ref=got[m].at[j], send_sem=send.at[m, j],
                                                      recv_sem=recv.at[m, j], device_id=(cx, cy, c), device_id_type=MESH)
                    cp.start()
                    waits.append(cp.wait_send)
                cp = pltpu.make_async_copy(a.at[s_me], own[m], lsem.at[N_CHIP + 1])
                cp.start()
                waits.append(cp.wait)
                for w in waits:
                    w()

            pl.run_scoped(scoped, pltpu.VMEM((N_CHIP, pr, pc), BF16), pltpu.VMEM((N_CHIP, pr, pc), BF16))
        for m in range(nm):
            for j, (cx, cy) in enumerate(chips):
                pltpu.make_async_remote_copy(src_ref=got[m].at[j], dst_ref=got[m].at[j], send_sem=send.at[m, j],
                                             recv_sem=recv.at[m, j], device_id=(cx, cy, c), device_id_type=MESH).wait_recv()

    sem = lambda *s: pltpu.SemaphoreType.DMA(s)
    pieces = [_piece(kind, shape) for _, kind, shape in mats]
    return pl.pallas_call(
        body, name="rs2_chips",
        out_shape=[jax.ShapeDtypeStruct(p, BF16) for p in pieces] + [jax.ShapeDtypeStruct((3,) + p, BF16) for p in pieces],
        in_specs=[ANY] * (2 * nm), out_specs=[ANY] * (2 * nm),
        scratch_shapes=[sem(nm, 3), sem(nm, 3), sem(N_CHIP + 2)],
        compiler_params=_params(),
    )(*grads, *halves)


def _rs3_finish(own, got):
    nm = len(BIG)

    def body(*refs):
        oin, gin = refs[:nm], refs[nm:2 * nm]
        outs = refs[2 * nm:3 * nm]
        send, recv, lsem = refs[3 * nm:]
        x, y, c, _ = _place()
        for m, (_, kind, shape) in enumerate(BIG):
            pr, pc = _piece(kind, shape)
            ng = got[m].shape[0]

            def scoped(a, g, f, m=m, pr=pr, ng=ng, kind=kind, shape=shape):
                mine = _piece_at(oin[m], kind, shape, 2 * x + y, c) if m in DIRECT else oin[m]
                loads = [pltpu.make_async_copy(mine, a, lsem.at[0]), pltpu.make_async_copy(gin[m], g, lsem.at[1])]
                for cp in loads:
                    cp.start()
                for cp in loads:
                    cp.wait()
                step = _add_rows(pr)

                def add(i, _):
                    r = pl.ds(pl.multiple_of(i * step, 16), step)
                    acc = a[r, :].astype(F32)
                    for q in range(ng):
                        acc = acc + g[q, r, :].astype(F32)
                    f[r, :] = acc
                    return 0

                lax.fori_loop(0, pr // step, add, 0)
                dst = outs[m].at[pl.ds(pl.multiple_of(c * pr, 8), pr), :]
                local = pltpu.make_async_copy(f, dst, lsem.at[2])
                local.start()
                cp = pltpu.make_async_remote_copy(src_ref=f, dst_ref=dst, send_sem=send.at[m], recv_sem=recv.at[m],
                                                  device_id=(x, y, 1 - c), device_id_type=MESH)
                cp.start()
                cp.wait_send()
                local.wait()

            pl.run_scoped(scoped, pltpu.VMEM((pr, pc), BF16), pltpu.VMEM((ng, pr, pc), BF16), pltpu.VMEM((pr, pc), F32))
        for m, (_, kind, shape) in enumerate(BIG):
            pr, pc = _piece(kind, shape)
            dst = outs[m].at[pl.ds(pl.multiple_of((1 - c) * pr, 8), pr), :]
            pltpu.make_async_remote_copy(src_ref=dst, dst_ref=dst, send_sem=send.at[m], recv_sem=recv.at[m],
                                         device_id=(x, y, 1 - c), device_id_type=MESH).wait_recv()

    sem = lambda *s: pltpu.SemaphoreType.DMA(s)
    pieces = [_piece(kind, shape) for _, kind, shape in BIG]
    return pl.pallas_call(
        body, name="rs3_finish",
        out_shape=[jax.ShapeDtypeStruct((2 * pr, pc), F32) for pr, pc in pieces],
        in_specs=[ANY] * (2 * nm), out_specs=[ANY] * nm,
        scratch_shapes=[sem(nm), sem(nm), sem(3)],
        compiler_params=_params(),
    )(*own, *got)


def _cond_fwd(c_all, w_shard, b_shard):
    def body(c_ref, w_ref, b_ref, act_ref, mod_ref):
        cv = c_ref[...]
        act = cv * _sig(cv)
        act_ref[...] = act
        mod_ref[...] = _dot(act.astype(BF16), w_ref[...].astype(BF16)) + b_ref[...]

    return pl.pallas_call(
        body, name="cond_fwd",
        out_shape=[jax.ShapeDtypeStruct(c_all.shape, F32), jax.ShapeDtypeStruct((c_all.shape[0], w_shard.shape[1]), F32)],
        compiler_params=_params(),
    )(c_all, w_shard, b_shard)


def _cond_bwd(act_t, dmod_shard):
    k, n = act_t.shape[0], dmod_shard.shape[1]

    def body(a_ref, d_ref, o_ref):
        acc = a_ref[:, 0:1] * d_ref[0:1, :]
        for e in range(1, N_DEV):
            acc += a_ref[:, e:e + 1] * d_ref[e:e + 1, :]
        o_ref[...] = acc

    tr = 256
    return pl.pallas_call(
        body, name="cond_bwd", grid=(k // tr,),
        in_specs=[pl.BlockSpec((tr, N_DEV), lambda i: (i, 0)), _full(dmod_shard)],
        out_specs=pl.BlockSpec((tr, n), lambda i: (i, 0)),
        out_shape=jax.ShapeDtypeStruct((k, n), F32),
        compiler_params=_params(("arbitrary",)),
    )(act_t, dmod_shard)


RAW_ROWS = 8


def _allreduce_small(pack):
    rows = PACK_ROWS // N_DEV

    def body(x_ref, sum_ref, raw_ref, buf, s1, r1, s2, r2, s3, r3):
        x, y, c, _ = _place()
        me = 4 * x + 2 * y + c
        peers = []
        for r in range(1, N_DEV):
            tx, ty, tc = (1 - x if r & 4 else x), (1 - y if r & 2 else y), (1 - c if r & 1 else c)
            peers.append(((tx, ty, tc), 4 * tx + 2 * ty + tc))
        chunk = lambda ref, d: ref.at[pl.ds(pl.multiple_of(d * rows, 8), rows), :]
        mine_raw = raw_ref.at[pl.ds(pl.multiple_of(me * RAW_ROWS, 8), RAW_ROWS), :]
        first = []
        for q, (dev, pd) in enumerate(peers):
            first.append(pltpu.make_async_remote_copy(src_ref=chunk(x_ref, pd), dst_ref=buf.at[q], send_sem=s1.at[q],
                                                      recv_sem=r1.at[q], device_id=dev, device_id_type=MESH))
            first.append(pltpu.make_async_remote_copy(src_ref=x_ref.at[0:RAW_ROWS, :], dst_ref=mine_raw, send_sem=s3.at[q],
                                                      recv_sem=r3.at[q], device_id=dev, device_id_type=MESH))
        for cp in first:
            cp.start()
        raw_ref[pl.ds(pl.multiple_of(me * RAW_ROWS, 8), RAW_ROWS), :] = x_ref[0:RAW_ROWS, :]
        for q, (dev, pd) in enumerate(peers):
            first[2 * q].wait()
        acc = x_ref[pl.ds(pl.multiple_of(me * rows, 8), rows), :]
        for q in range(N_DEV - 1):
            acc = acc + buf[q]
        sum_ref[pl.ds(pl.multiple_of(me * rows, 8), rows), :] = acc
        second = [pltpu.make_async_remote_copy(src_ref=chunk(sum_ref, me), dst_ref=chunk(sum_ref, me), send_sem=s2.at[q],
                                               recv_sem=r2.at[q], device_id=dev, device_id_type=MESH)
                  for q, (dev, pd) in enumerate(peers)]
        for cp in second:
            cp.start()
        for q, (dev, pd) in enumerate(peers):
            pltpu.make_async_remote_copy(src_ref=chunk(sum_ref, pd), dst_ref=chunk(sum_ref, pd), send_sem=s2.at[q],
                                         recv_sem=r2.at[q], device_id=dev, device_id_type=MESH).wait()
            pltpu.make_async_remote_copy(src_ref=x_ref.at[0:RAW_ROWS, :],
                                         dst_ref=raw_ref.at[pl.ds(pl.multiple_of(pd * RAW_ROWS, 8), RAW_ROWS), :],
                                         send_sem=s3.at[q], recv_sem=r3.at[q], device_id=dev, device_id_type=MESH).wait()

    sem = pltpu.SemaphoreType.DMA((N_DEV - 1,))
    return pl.pallas_call(
        body, name="allreduce_small",
        out_shape=[jax.ShapeDtypeStruct((PACK_ROWS, PACK_COLS), F32), jax.ShapeDtypeStruct((N_DEV * RAW_ROWS, PACK_COLS), F32)],
        in_specs=[pl.BlockSpec(memory_space=pltpu.VMEM)],
        out_specs=[pl.BlockSpec(memory_space=pltpu.VMEM)] * 2,
        scratch_shapes=[pltpu.VMEM((N_DEV - 1, rows, PACK_COLS), F32), sem, sem, sem, sem, sem, sem],
        compiler_params=_params(),
    )(pack)


def _adamw(name, w, g, m, v):
    r, cc = w.shape
    tr = r
    for cand in (256, 128, 64, 32, 16, 8):
        if r % cand == 0:
            tr = cand
            break
    bc1 = 1.0 - ADAM_B1 ** ADAM_STEP
    bc2 = 1.0 - ADAM_B2 ** ADAM_STEP

    def body(w_ref, g_ref, m_ref, v_ref, d_ref, nm_ref, nv_ref):
        gv = g_ref[...]
        nm = ADAM_B1 * m_ref[...] + (1.0 - ADAM_B1) * gv
        nv = ADAM_B2 * v_ref[...] + (1.0 - ADAM_B2) * (gv * gv)
        nm_ref[...] = nm
        nv_ref[...] = nv
        d_ref[...] = -ADAM_LR * ((nm / bc1) / (jnp.sqrt(nv / bc2) + ADAM_EPS) + ADAM_WD * w_ref[...])

    spec = pl.BlockSpec((tr, cc), lambda i: (i, 0))
    return pl.pallas_call(
        body, name=name, grid=(r // tr,), in_specs=[spec] * 4, out_specs=[spec] * 3,
        out_shape=[jax.ShapeDtypeStruct((r, cc), F32)] * 3, compiler_params=_params(("arbitrary",)),
    )(w, g, m, v)


def _adamw_small(ws, gs, ms, vs):
    n = len(ws)
    bc1 = 1.0 - ADAM_B1 ** ADAM_STEP
    bc2 = 1.0 - ADAM_B2 ** ADAM_STEP

    def body(*refs):
        w, g, m, v = (refs[k * n:(k + 1) * n] for k in range(4))
        d, nm, nv = (refs[(4 + k) * n:(5 + k) * n] for k in range(3))
        for i in range(n):
            gv = g[i][...]
            m1 = ADAM_B1 * m[i][...] + (1.0 - ADAM_B1) * gv
            v1 = ADAM_B2 * v[i][...] + (1.0 - ADAM_B2) * (gv * gv)
            nm[i][...] = m1
            nv[i][...] = v1
            d[i][...] = -ADAM_LR * ((m1 / bc1) / (jnp.sqrt(v1 / bc2) + ADAM_EPS) + ADAM_WD * w[i][...])

    shapes = [jax.ShapeDtypeStruct(x.shape, F32) for x in ws]
    res = pl.pallas_call(body, name="adamw_small", out_shape=shapes * 3, compiler_params=_params())(*ws, *gs, *ms, *vs)
    return res[:n], res[n:2 * n], res[2 * n:]


def _pack(fields, layout):
    parts = [fields[name].reshape(-1).astype(F32) if name in fields else jnp.zeros((n,), F32) for name, n in layout]
    used = sum(n for _, n in layout)
    parts.append(jnp.zeros((PACK_ROWS * PACK_COLS - used,), F32))
    return jnp.concatenate(parts).reshape(PACK_ROWS, PACK_COLS)


def _unpack(flat, layout):
    flat = flat.reshape(-1)
    out, o = {}, 0
    for name, n in layout:
        out[name] = flat[o:o + n]
        o += n
    return out


def kernel(x, c, w_cond, b_cond, w_in, b_in, ssm_lambda_re, ssm_lambda_im, ssm_log_dt, ssm_b_re, ssm_b_im, ssm_c_re, ssm_c_im, ssm_d, ssm_glu_w_a, ssm_glu_w_b, cv_dw_w, cv_dw_b, cv_ln_g, cv_ln_b, cv_w_pw, w_out, ln1_g, ln1_b, ffn_w_up, ffn_dw_w, ffn_dw_b, ffn_w_down, ln2_g, ln2_b, loss_target, m_w_cond, m_b_cond, m_w_in, m_b_in, m_ssm_lambda_re, m_ssm_lambda_im, m_ssm_log_dt, m_ssm_b_re, m_ssm_b_im, m_ssm_c_re, m_ssm_c_im, m_ssm_d, m_ssm_glu_w_a, m_ssm_glu_w_b, m_cv_dw_w, m_cv_dw_b, m_cv_ln_g, m_cv_ln_b, m_cv_w_pw, m_w_out, m_ln1_g, m_ln1_b, m_ffn_w_up, m_ffn_dw_w, m_ffn_dw_b, m_ffn_w_down, m_ln2_g, m_ln2_b, v_w_cond, v_b_cond, v_w_in, v_b_in, v_ssm_lambda_re, v_ssm_lambda_im, v_ssm_log_dt, v_ssm_b_re, v_ssm_b_im, v_ssm_c_re, v_ssm_c_im, v_ssm_d, v_ssm_glu_w_a, v_ssm_glu_w_b, v_cv_dw_w, v_cv_dw_b, v_cv_ln_g, v_cv_ln_b, v_cv_w_pw, v_w_out, v_ln1_g, v_ln1_b, v_ffn_w_up, v_ffn_dw_w, v_ffn_dw_b, v_ffn_w_down, v_ln2_g, v_ln2_b):
    given = locals()
    a = {n: given[n] for n in INPUTS}
    xi, yi, ci = lax.axis_index("x"), lax.axis_index("y"), lax.axis_index("c")
    s_me = 2 * xi + yi
    e_me = 4 * xi + 2 * yi + ci

    first = jnp.concatenate([
        jnp.concatenate([a["c"], jnp.zeros((7, D_MODEL), F32)], axis=0),
        jnp.concatenate([a["cv_dw_w"].reshape(-1), a["ffn_dw_w"].reshape(-1)]).reshape(8, D_MODEL)], axis=0)
    full, shards, first_all = _gather_weights([a[n][0] for n, _, _ in BIG], first)
    wb = dict(zip([n for n, _, _ in BIG], full))
    first_all = first_all.reshape(N_DEV, 16, D_MODEL)
    c_all = first_all[:, 0, :]
    dw_all = first_all[0::2, 8:, :].reshape(N_CHIP, 8 * D_MODEL)
    n_cv = CONV_KERNEL * CONV_WIDTH // N_CHIP
    cv_dw_full = dw_all[:, :n_cv].reshape(N_CHIP, CONV_KERNEL, CONV_WIDTH // N_CHIP).transpose(1, 0, 2) \
        .reshape(CONV_KERNEL, CONV_WIDTH)
    ffn_dw_full = dw_all[:, n_cv:].reshape(N_CHIP, FFN_KERNEL, 2 * FFN_HIDDEN // N_CHIP).transpose(1, 0, 2) \
        .reshape(FFN_KERNEL, 2 * FFN_HIDDEN)
    ncols = N_COND * D_MODEL // N_CHIP
    b_cond_shard = lax.dynamic_slice(a["b_cond"], (0, s_me * ncols), (1, ncols))
    c_act_all, modp = _cond_fwd(c_all, a["w_cond"][0], b_cond_shard)
    modp_all = _allgather("gather_mod", modp).reshape(N_DEV, N_DEV, ncols)[0::2]
    mod_e = lax.dynamic_index_in_dim(modp_all, e_me, axis=1, keepdims=False).reshape(N_COND, D_MODEL)
    modv = jnp.concatenate([mod_e, jnp.zeros((2, D_MODEL), F32)], axis=0)

    sp = {n: a[n][0] for n in ("b_in", "ssm_lambda_re", "ssm_lambda_im", "ssm_log_dt", "ssm_b_re", "ssm_b_im",
                               "ssm_c_re", "ssm_c_im", "ssm_d", "cv_dw_b", "cv_ln_g", "cv_ln_b", "ln1_g", "ln1_b",
                               "ffn_dw_b", "ln2_g", "ln2_b")}
    sp["cv_dw_w"] = cv_dw_full
    sp["ffn_dw_w"] = ffn_dw_full
    gx, dbig, direct_got, small = _local_step(a["x"][0], a["loss_target"][0], modv, wb, shards, sp)

    tot_pack, raw_all = _allreduce_small(_pack(small, PACK))
    tot = _unpack(tot_pack, PACK)
    dmod_all = raw_all.reshape(N_DEV, RAW_ROWS * PACK_COLS)[:, 0:N_COND * D_MODEL]
    g_w_cond = _cond_bwd(c_act_all.T, lax.dynamic_slice(dmod_all, (0, s_me * ncols), (N_DEV, ncols)))

    glist = [dbig[m] for m in EARLY]
    halves = _rs1_sibling(glist)
    r2 = _rs2_chips(glist, halves)
    gsh = _rs3_finish(list(r2[:len(EARLY)]) + [dbig[m] for m in DIRECT], list(r2[len(EARLY):]) + direct_got)

    grads = {"w_cond": g_w_cond[None], "b_cond": tot["dmod"].reshape(1, -1)}
    for (n, kind, shape), g in zip(BIG, gsh):
        grads[n] = g.reshape(a[n].shape)
    for n in ("b_in", "ssm_lambda_re", "ssm_lambda_im", "ssm_log_dt", "ssm_b_re", "ssm_b_im", "ssm_c_re", "ssm_c_im",
              "ssm_d", "cv_dw_b", "cv_ln_g", "cv_ln_b", "ln1_g", "ln1_b", "ffn_dw_b", "ln2_g", "ln2_b"):
        grads[n] = tot[n].reshape(a[n].shape)
    wcv = CONV_WIDTH // N_CHIP
    grads["cv_dw_w"] = lax.dynamic_slice(tot["cv_dw_w"].reshape(CONV_KERNEL, CONV_WIDTH), (0, s_me * wcv),
                                         (CONV_KERNEL, wcv)).reshape(a["cv_dw_w"].shape)
    wff = 2 * FFN_HIDDEN // N_CHIP
    grads["ffn_dw_w"] = lax.dynamic_slice(tot["ffn_dw_w"].reshape(FFN_KERNEL, 2 * FFN_HIDDEN), (0, s_me * wff),
                                          (FFN_KERNEL, wff)).reshape(a["ffn_dw_w"].shape)

    delta, new_m, new_v = {}, {}, {}
    for n in ["w_cond"] + [n for n, _, _ in BIG]:
        d, nm_, nv_ = _adamw("adamw_" + n, a[n][0], grads[n][0], a["m_" + n][0], a["v_" + n][0])
        delta[n], new_m[n], new_v[n] = d[None], nm_[None], nv_[None]
    upd = [n for n in WEIGHTS if n not in delta]
    two_d = lambda t: t.reshape(-1, t.shape[-1])
    outs = _adamw_small([two_d(a[n]) for n in upd], [two_d(grads[n]) for n in upd],
                        [two_d(a["m_" + n]) for n in upd], [two_d(a["v_" + n]) for n in upd])
    for dst, vals in zip((delta, new_m, new_v), outs):
        for n, val in zip(upd, vals):
            dst[n] = val.reshape(a[n].shape)

    loss = tot["loss"].reshape(())
    return (loss, gx[None], *[grads[n] for n in WEIGHTS], *[delta[n] for n in WEIGHTS],
            *[new_m[n] for n in WEIGHTS], *[new_v[n] for n in WEIGHTS])
```
